```python
import jax, jax.numpy as jnp
from jax import lax
import numpy as np

D_MODEL = 2048
BATCH = 8
SEQ = 4096
DEPTH = 1

Q_BLOCK = 128
HEAD_WIDTH = 128
MLA_HEADS = (D_MODEL // 2) // HEAD_WIDTH
MLA_NOPE_DIM = 128
MLA_ROPE_DIM = 64
MLA_V_DIM = 128
MLA_QK_DIM = MLA_NOPE_DIM + MLA_ROPE_DIM
MLA_Q_RANK = 768
MLA_KV_RANK = 512
MLA_WIDTH = MLA_HEADS * MLA_V_DIM
FOX_HEADS = (D_MODEL // 2) // HEAD_WIDTH
FOX_HEAD_DIM = HEAD_WIDTH
FOX_WIDTH = FOX_HEADS * FOX_HEAD_DIM
D_MIX = MLA_WIDTH + FOX_WIDTH
ROPE_THETA = 10000.0
NORM_EPS = 1e-6
IN_SPLITS = (MLA_Q_RANK, MLA_KV_RANK, MLA_ROPE_DIM, MLA_WIDTH,
             FOX_WIDTH, FOX_WIDTH, FOX_WIDTH, FOX_HEADS, FOX_WIDTH)
D_IN = (MLA_Q_RANK + MLA_KV_RANK + MLA_ROPE_DIM + MLA_WIDTH
        + 4 * FOX_WIDTH + FOX_HEADS)

kernel_name = "hybrid_mla_fox_parallel_heads"


def _rms_norm(x, g):
    xf = x.astype(jnp.float32)
    y = xf * lax.rsqrt(jnp.mean(xf * xf, axis=-1, keepdims=True) + NORM_EPS)
    return (y * g.astype(jnp.float32)).astype(x.dtype)


def _rope_angles(positions, dim):
    inv_freq = ROPE_THETA ** (-jnp.arange(0, dim, 2, dtype=jnp.float32) / dim)
    ang = positions.astype(jnp.float32)[..., None] * inv_freq
    return jnp.cos(ang), jnp.sin(ang)


def _apply_rope(x, cos, sin):
    xf = x.astype(jnp.float32)
    half = xf.shape[-1] // 2
    x1, x2 = xf[..., :half], xf[..., half:]
    out = jnp.concatenate([x1 * cos - x2 * sin, x2 * cos + x1 * sin], axis=-1)
    return out.astype(x.dtype)


def _causal_block_sweep(score_fn, v):
    b, seq = v.shape[0], v.shape[1]
    n_blocks = seq // Q_BLOCK
    key_pos = jnp.arange(seq)

    def one_block(i):
        start = i * Q_BLOCK
        logits = score_fn(start)
        q_pos = start + jnp.arange(Q_BLOCK)
        logits = jnp.where(key_pos[None, :] <= q_pos[:, None], logits, -jnp.inf)
        p = jax.nn.softmax(logits, axis=-1).astype(v.dtype)
        return jnp.einsum('bhqs,bshd->bqhd', p, v)

    out = lax.map(one_block, jnp.arange(n_blocks))
    return out.transpose(1, 0, 2, 3, 4).reshape(b, seq, -1)


def _mla_branch(q_lat, kv_lat, k_rope_raw, g_q, w_uq, g_kv, w_ukv, cos, sin):
    b, s, _ = q_lat.shape
    q = (_rms_norm(q_lat, g_q) @ w_uq).reshape(b, s, MLA_HEADS, MLA_QK_DIM)
    q_nope = q[..., :MLA_NOPE_DIM]
    q_rope = _apply_rope(q[..., MLA_NOPE_DIM:], cos[:, :, None, :], sin[:, :, None, :])
    kv = (_rms_norm(kv_lat, g_kv) @ w_ukv).reshape(b, s, MLA_HEADS, MLA_NOPE_DIM + MLA_V_DIM)
    k_nope, v = kv[..., :MLA_NOPE_DIM], kv[..., MLA_NOPE_DIM:]
    k_rope = _apply_rope(k_rope_raw, cos, sin)
    scale = MLA_QK_DIM ** -0.5

    def score(start):
        qn = lax.dynamic_slice_in_dim(q_nope, start, Q_BLOCK, axis=1)
        qr = lax.dynamic_slice_in_dim(q_rope, start, Q_BLOCK, axis=1)
        s_nope = jnp.einsum('bqhd,bshd->bhqs', qn, k_nope, preferred_element_type=jnp.float32)
        s_rope = jnp.einsum('bqhr,bsr->bhqs', qr, k_rope, preferred_element_type=jnp.float32)
        return (s_nope + s_rope) * scale

    return _causal_block_sweep(score, v)


def _fox_branch(q, k, v, f_logit, b_forget):
    b, s, _ = q.shape
    q = q.reshape(b, s, FOX_HEADS, FOX_HEAD_DIM)
    k = k.reshape(b, s, FOX_HEADS, FOX_HEAD_DIM)
    v = v.reshape(b, s, FOX_HEADS, FOX_HEAD_DIM)
    log_f = jax.nn.log_sigmoid(f_logit.astype(jnp.float32) + b_forget.astype(jnp.float32))
    c = jnp.cumsum(log_f, axis=1).transpose(0, 2, 1)
    scale = FOX_HEAD_DIM ** -0.5

    def score(start):
        qb = lax.dynamic_slice_in_dim(q, start, Q_BLOCK, axis=1)
        cq = lax.dynamic_slice_in_dim(c, start, Q_BLOCK, axis=2)
        logits = jnp.einsum('bqhd,bshd->bhqs', qb, k, preferred_element_type=jnp.float32) * scale
        return logits + cq[:, :, :, None] - c[:, :, None, :]

    return _causal_block_sweep(score, v)


def _hybrid_layer(x, cos, sin, g_pre, w_in, g_q, w_uq, g_kv, w_ukv, b_forget, w_out, g_post):
    h = _rms_norm(x, g_pre)
    proj = h @ w_in
    split_points = np.cumsum(IN_SPLITS)[:-1].tolist()
    (q_lat, kv_lat, k_rope_raw, gate_mla,
     fq, fk, fv, f_logit, gate_fox) = jnp.split(proj, split_points, axis=-1)
    o_mla = _mla_branch(q_lat, kv_lat, k_rope_raw, g_q, w_uq, g_kv, w_ukv, cos, sin) * jax.nn.silu(gate_mla)
    o_fox = _fox_branch(fq, fk, fv, f_logit, b_forget) * jax.nn.silu(gate_fox)
    o = jnp.concatenate([o_mla, o_fox], axis=-1) @ w_out
    return x + _rms_norm(o, g_post)


def _fwd_setup_inputs(seed: int = 0) -> dict:
    key = jax.random.key(seed)
    ks = jax.random.split(key, 12)
    f32 = jnp.float32
    x = jax.random.normal(ks[0], (BATCH, SEQ, D_MODEL), f32)
    offsets = jax.random.randint(ks[1], (BATCH, 1), 0, 1024, dtype=jnp.int32)
    positions = jnp.arange(SEQ, dtype=jnp.int32)[None, :] + offsets
    g_pre = 1.0 + 0.02 * jax.random.normal(ks[2], (DEPTH, D_MODEL), f32)
    w_in = jax.random.normal(ks[3], (DEPTH, D_MODEL, D_IN), f32) * D_MODEL ** -0.5
    g_q_latent = 1.0 + 0.02 * jax.random.normal(ks[4], (DEPTH, MLA_Q_RANK), f32)
    w_uq = jax.random.normal(ks[5], (DEPTH, MLA_Q_RANK, MLA_HEADS * MLA_QK_DIM), f32) * MLA_Q_RANK ** -0.5
    g_kv_latent = 1.0 + 0.02 * jax.random.normal(ks[6], (DEPTH, MLA_KV_RANK), f32)
    w_ukv = jax.random.normal(ks[7], (DEPTH, MLA_KV_RANK, MLA_HEADS * (MLA_NOPE_DIM + MLA_V_DIM)), f32) * MLA_KV_RANK ** -0.5
    b_forget = 3.0 + 0.1 * jax.random.normal(ks[8], (DEPTH, FOX_HEADS), f32)
    w_out = jax.random.normal(ks[9], (DEPTH, D_MIX, D_MODEL), f32) * D_MIX ** -0.5
    g_post = 1.0 + 0.02 * jax.random.normal(ks[10], (DEPTH, D_MODEL), f32)
    return {"x": x, "positions": positions, "g_pre": g_pre, "w_in": w_in,
            "g_q_latent": g_q_latent, "w_uq": w_uq, "g_kv_latent": g_kv_latent,
            "w_ukv": w_ukv, "b_forget": b_forget, "w_out": w_out, "g_post": g_post}


def _fwd_reference(x, positions, g_pre, w_in, g_q_latent, w_uq, g_kv_latent, w_ukv, b_forget, w_out, g_post):
    cos, sin = _rope_angles(positions, MLA_ROPE_DIM)
    for l in range(DEPTH):
        x = _hybrid_layer(x, cos, sin, g_pre[l], w_in[l], g_q_latent[l], w_uq[l],
                          g_kv_latent[l], w_ukv[l], b_forget[l], w_out[l], g_post[l])
    return x


import jax as _jax
import jax.numpy as _jnp

TWIN_FORMAT = 'train_step'
FWD_PARAMS = ['x', 'positions', 'g_pre', 'w_in', 'g_q_latent', 'w_uq', 'g_kv_latent', 'w_ukv', 'b_forget', 'w_out', 'g_post']
TWIN_WEIGHTS = ['g_pre', 'w_in', 'g_q_latent', 'w_uq', 'g_kv_latent', 'w_ukv', 'b_forget', 'w_out', 'g_post']
TWIN_DIFF_INPUT = 'x'
TWIN_INPUTS = ['x', 'positions', 'g_pre', 'w_in', 'g_q_latent', 'w_uq', 'g_kv_latent', 'w_ukv', 'b_forget', 'w_out', 'g_post', 'loss_target', 'm_g_pre', 'm_w_in', 'm_g_q_latent', 'm_w_uq', 'm_g_kv_latent', 'm_w_ukv', 'm_b_forget', 'm_w_out', 'm_g_post', 'v_g_pre', 'v_w_in', 'v_g_q_latent', 'v_w_uq', 'v_g_kv_latent', 'v_w_ukv', 'v_b_forget', 'v_w_out', 'v_g_post']
TWIN_OUTPUTS = ['loss', 'grad_x', 'grad_g_pre', 'grad_w_in', 'grad_g_q_latent', 'grad_w_uq', 'grad_g_kv_latent', 'grad_w_ukv', 'grad_b_forget', 'grad_w_out', 'grad_g_post', 'delta_g_pre', 'delta_w_in', 'delta_g_q_latent', 'delta_w_uq', 'delta_g_kv_latent', 'delta_w_ukv', 'delta_b_forget', 'delta_w_out', 'delta_g_post', 'new_m_g_pre', 'new_m_w_in', 'new_m_g_q_latent', 'new_m_w_uq', 'new_m_g_kv_latent', 'new_m_w_ukv', 'new_m_b_forget', 'new_m_w_out', 'new_m_g_post', 'new_v_g_pre', 'new_v_w_in', 'new_v_g_q_latent', 'new_v_w_uq', 'new_v_g_kv_latent', 'new_v_w_ukv', 'new_v_b_forget', 'new_v_w_out', 'new_v_g_post']
TWIN_LEAF_KINDS = {'loss': 'loss', 'grad_x': 'grad_x', 'grad_g_pre': 'grad_w', 'grad_w_in': 'grad_w', 'grad_g_q_latent': 'grad_w', 'grad_w_uq': 'grad_w', 'grad_g_kv_latent': 'grad_w', 'grad_w_ukv': 'grad_w', 'grad_b_forget': 'grad_w', 'grad_w_out': 'grad_w', 'grad_g_post': 'grad_w', 'delta_g_pre': 'delta_w', 'delta_w_in': 'delta_w', 'delta_g_q_latent': 'delta_w', 'delta_w_uq': 'delta_w', 'delta_g_kv_latent': 'delta_w', 'delta_w_ukv': 'delta_w', 'delta_b_forget': 'delta_w', 'delta_w_out': 'delta_w', 'delta_g_post': 'delta_w', 'new_m_g_pre': 'new_m', 'new_m_w_in': 'new_m', 'new_m_g_q_latent': 'new_m', 'new_m_w_uq': 'new_m', 'new_m_g_kv_latent': 'new_m', 'new_m_w_ukv': 'new_m', 'new_m_b_forget': 'new_m', 'new_m_w_out': 'new_m', 'new_m_g_post': 'new_m', 'new_v_g_pre': 'new_v', 'new_v_w_in': 'new_v', 'new_v_g_q_latent': 'new_v', 'new_v_w_uq': 'new_v', 'new_v_g_kv_latent': 'new_v', 'new_v_w_ukv': 'new_v', 'new_v_b_forget': 'new_v', 'new_v_w_out': 'new_v', 'new_v_g_post': 'new_v'}


def _forward(args):
    return _fwd_reference(*[args[k] for k in FWD_PARAMS])


def _output_shape():
    def fwd():
        inp = _fwd_setup_inputs(0)
        return _fwd_reference(*[inp[k] for k in FWD_PARAMS])
    out = _jax.eval_shape(fwd)
    return out.shape, out.dtype

N_MICROBATCH = 1
ADAM_LR = 0.001
ADAM_B1 = 0.9
ADAM_B2 = 0.999
ADAM_EPS = 1e-08
ADAM_WD = 0.01
ADAM_STEP = 10
PER_EXAMPLE_BATCH_AXIS = {'x': 0, 'positions': 0, 'loss_target': 0}
SHARED_INPUTS = []
_WEIGHT_DTYPES = {'g_pre': _jnp.float32, 'w_in': _jnp.float32, 'g_q_latent': _jnp.float32, 'w_uq': _jnp.float32, 'g_kv_latent': _jnp.float32, 'w_ukv': _jnp.float32, 'b_forget': _jnp.float32, 'w_out': _jnp.float32, 'g_post': _jnp.float32}
MOMENT_SCALE = {'g_pre': 2.524969e-01, 'w_in': 1.343155e-01, 'g_q_latent': 9.672268e-02, 'w_uq': 6.739084e-02, 'g_kv_latent': 1.649688e-01, 'w_ukv': 7.975633e-02, 'b_forget': 1.742353e+00, 'w_out': 1.283260e-01, 'g_post': 1.601165e+01}


def _to_microbatches(a, axis):
    t = _jnp.moveaxis(a, axis, 0)
    t = t.reshape((N_MICROBATCH, t.shape[0] // N_MICROBATCH) + t.shape[1:])
    return _jnp.moveaxis(t, 1, axis + 1)


def setup_inputs(seed: int = 0) -> dict:
    inp = _fwd_setup_inputs(seed)
    key = _jax.random.fold_in(_jax.random.key(seed), 7919)
    shape, _ = _output_shape()
    out = dict(inp)
    out["loss_target"] = _jax.random.normal(_jax.random.fold_in(key, 0), shape, _jnp.float32)
    for i, name in enumerate(TWIN_WEIGHTS):
        w = inp[name].astype(_jnp.float32)
        if MOMENT_SCALE is None:
            s = _jnp.sqrt(_jnp.mean(_jnp.square(w)) + 1e-30)
        else:
            s = MOMENT_SCALE[name]
        km, kv = _jax.random.split(_jax.random.fold_in(key, i + 1))
        out[name] = w
        out["m_" + name] = s * _jax.random.normal(km, w.shape, _jnp.float32)
        out["v_" + name] = (s * s) * _jax.random.uniform(kv, w.shape, _jnp.float32, 0.5, 1.5)
    if N_MICROBATCH > 1:
        for name, axis in PER_EXAMPLE_BATCH_AXIS.items():
            out[name] = _to_microbatches(out[name], axis)
    return {'x': out['x'], 'positions': out['positions'], 'g_pre': out['g_pre'], 'w_in': out['w_in'], 'g_q_latent': out['g_q_latent'], 'w_uq': out['w_uq'], 'g_kv_latent': out['g_kv_latent'], 'w_ukv': out['w_ukv'], 'b_forget': out['b_forget'], 'w_out': out['w_out'], 'g_post': out['g_post'], 'loss_target': out['loss_target'], 'm_g_pre': out['m_g_pre'], 'm_w_in': out['m_w_in'], 'm_g_q_latent': out['m_g_q_latent'], 'm_w_uq': out['m_w_uq'], 'm_g_kv_latent': out['m_g_kv_latent'], 'm_w_ukv': out['m_w_ukv'], 'm_b_forget': out['m_b_forget'], 'm_w_out': out['m_w_out'], 'm_g_post': out['m_g_post'], 'v_g_pre': out['v_g_pre'], 'v_w_in': out['v_w_in'], 'v_g_q_latent': out['v_g_q_latent'], 'v_w_uq': out['v_w_uq'], 'v_g_kv_latent': out['v_g_kv_latent'], 'v_w_ukv': out['v_w_ukv'], 'v_b_forget': out['v_b_forget'], 'v_w_out': out['v_w_out'], 'v_g_post': out['v_g_post']}


def _loss(weights, diff, rest, loss_target):
    with _jax.named_scope("forward"):
        args = {**rest, TWIN_DIFF_INPUT: diff, **{k: w.astype(_WEIGHT_DTYPES[k]) for k, w in weights.items()}}
        y = _forward(args)
    with _jax.named_scope("loss_head"):
        err = _jnp.square(y.astype(_jnp.float32) - loss_target)
        return 0.5 * _jnp.sum(_jnp.mean(err, axis=-1)) if err.ndim else 0.5 * err


def _adamw(w, g, m, v):
    m = ADAM_B1 * m + (1.0 - ADAM_B1) * g
    v = ADAM_B2 * v + (1.0 - ADAM_B2) * _jnp.square(g)
    m_hat = m / (1.0 - ADAM_B1 ** ADAM_STEP)
    v_hat = v / (1.0 - ADAM_B2 ** ADAM_STEP)
    delta = -ADAM_LR * (m_hat / (_jnp.sqrt(v_hat) + ADAM_EPS) + ADAM_WD * w)
    return delta, m, v


def reference(x, positions, g_pre, w_in, g_q_latent, w_uq, g_kv_latent, w_ukv, b_forget, w_out, g_post, loss_target, m_g_pre, m_w_in, m_g_q_latent, m_w_uq, m_g_kv_latent, m_w_ukv, m_b_forget, m_w_out, m_g_post, v_g_pre, v_w_in, v_g_q_latent, v_w_uq, v_g_kv_latent, v_w_ukv, v_b_forget, v_w_out, v_g_post):
    given = dict(x=x, positions=positions, g_pre=g_pre, w_in=w_in, g_q_latent=g_q_latent, w_uq=w_uq, g_kv_latent=g_kv_latent, w_ukv=w_ukv, b_forget=b_forget, w_out=w_out, g_post=g_post, loss_target=loss_target, m_g_pre=m_g_pre, m_w_in=m_w_in, m_g_q_latent=m_g_q_latent, m_w_uq=m_w_uq, m_g_kv_latent=m_g_kv_latent, m_w_ukv=m_w_ukv, m_b_forget=m_b_forget, m_w_out=m_w_out, m_g_post=m_g_post, v_g_pre=v_g_pre, v_w_in=v_w_in, v_g_q_latent=v_g_q_latent, v_w_uq=v_w_uq, v_g_kv_latent=v_g_kv_latent, v_w_ukv=v_w_ukv, v_b_forget=v_b_forget, v_w_out=v_w_out, v_g_post=v_g_post)
    weights = {n: given[n] for n in TWIN_WEIGHTS}
    shared = {n: given[n] for n in SHARED_INPUTS}
    per_example = {n: given[n] for n in ['x', 'positions']}
    grad_fn = _jax.value_and_grad(_loss, argnums=(0, 1))

    def one_microbatch(ex, loss_target):
        ex = dict(ex)
        diff = ex.pop(TWIN_DIFF_INPUT)
        return grad_fn(weights, diff, {**shared, **ex}, loss_target)

    if N_MICROBATCH == 1:
        loss, (grad_w, grad_x) = one_microbatch(per_example, given["loss_target"])
    else:
        def body(carry, xs):
            loss_sum, grad_sum = carry
            l_k, (gw_k, gx_k) = one_microbatch(xs[0], xs[1])
            with _jax.named_scope("update"):
                return (loss_sum + l_k, _jax.tree.map(_jnp.add, grad_sum, gw_k)), gx_k

        init = (_jnp.zeros((), _jnp.float32), _jax.tree.map(_jnp.zeros_like, weights))
        (loss, grad_w), grad_x = _jax.lax.scan(body, init, (per_example, given["loss_target"]))
    with _jax.named_scope("update"):
        delta_w, new_m, new_v = {}, {}, {}
        for n in TWIN_WEIGHTS:
            delta_w[n], new_m[n], new_v[n] = _adamw(weights[n], grad_w[n], given["m_" + n], given["v_" + n])
    return (loss, grad_x, *[grad_w[n] for n in TWIN_WEIGHTS], *[delta_w[n] for n in TWIN_WEIGHTS],
            *[new_m[n] for n in TWIN_WEIGHTS], *[new_v[n] for n in TWIN_WEIGHTS])
```

```python
import functools

import numpy as np
import jax
import jax.numpy as jnp
from jax import lax
from jax.experimental import pallas as pl
from jax.experimental.pallas import tpu as pltpu

F32 = jnp.float32
BF16 = jnp.bfloat16
MESH = pl.DeviceIdType.MESH

N_DEV = 8
D_MODEL = 2048
N_HEADS = 8
HEAD = 128
Q_RANK = 768
KV_RANK = 512
ROPE = 64
D_IN = 6472
SHARD_IN = D_IN // N_DEV
NORM_EPS = 1e-6
ROPE_THETA = 10000.0
MLA_SCALE = (HEAD + ROPE) ** -0.5
FOX_SCALE = HEAD ** -0.5

QL0, KVL0, KR0, GM0, FQ0, FK0, FV0, GF0, FL0, NP_IN = 0, 768, 1280, 1408, 2432, 3456, 4480, 5504, 6528, 6656
LANE = 128

ADAM_LR = 0.001
ADAM_B1 = 0.9
ADAM_B2 = 0.999
ADAM_EPS = 1e-08
ADAM_WD = 0.01
ADAM_STEP = 10

VMEM_LIMIT_BYTES = 56 * 1024 * 1024
SMALL_ROWS, SMALL_COLS = 8, 768


def _params(n_grid=0):
    return pltpu.CompilerParams(vmem_limit_bytes=VMEM_LIMIT_BYTES,
                                dimension_semantics=("arbitrary",) * n_grid if n_grid else None)


def _sigmoid(z):
    return 1.0 / (1.0 + jnp.exp(-z))


def _split3(v):
    a = v.astype(BF16)
    r = v - a.astype(F32)
    b = r.astype(BF16)
    c = (r - b.astype(F32)).astype(BF16)
    return a, b, c


def _mm(a, b, *, name, nt=False, out_dtype=F32, tm=1024, tn=512, tk=2048):
    m, k_dim = a.shape
    n = b.shape[0] if nt else b.shape[1]
    assert (b.shape[1] if nt else b.shape[0]) == k_dim
    tm, tn, tk = min(tm, m), min(tn, n), min(tk, k_dim)
    assert m % tm == 0 and n % tn == 0 and k_dim % tk == 0, (name, a.shape, b.shape)
    nk = k_dim // tk
    dims = (((1,), (1,)), ((), ())) if nt else (((1,), (0,)), ((), ()))

    def body(a_ref, b_ref, o_ref, acc_ref):
        k = pl.program_id(2)

        @pl.when(k == 0)
        def _():
            acc_ref[...] = jnp.zeros_like(acc_ref)

        acc_ref[...] += lax.dot_general(a_ref[...], b_ref[...], dims, preferred_element_type=F32)

        @pl.when(k == nk - 1)
        def _():
            o_ref[...] = acc_ref[...].astype(o_ref.dtype)

    b_spec = (pl.BlockSpec((tn, tk), lambda i, j, k: (j, k)) if nt
              else pl.BlockSpec((tk, tn), lambda i, j, k: (k, j)))
    return pl.pallas_call(
        body, name=name, grid=(m // tm, n // tn, nk),
        in_specs=[pl.BlockSpec((tm, tk), lambda i, j, k: (i, k)), b_spec],
        out_specs=pl.BlockSpec((tm, tn), lambda i, j, k: (i, j)),
        out_shape=jax.ShapeDtypeStruct((m, n), out_dtype),
        scratch_shapes=[pltpu.VMEM((tm, tn), F32)],
        compiler_params=_params(3),
    )(a, b)


def _prenorm(x, g, *, tm):
    t = x.shape[0]

    def body(x_ref, g_ref, h_ref):
        xv = x_ref[...]
        r = lax.rsqrt(jnp.mean(xv * xv, axis=-1, keepdims=True) + NORM_EPS)
        h_ref[...] = (xv * r * g_ref[...]).astype(BF16)

    return pl.pallas_call(
        body, name="prenorm", grid=(t // tm,),
        in_specs=[pl.BlockSpec((tm, D_MODEL), lambda i: (i, 0)), pl.BlockSpec((1, D_MODEL), lambda i: (0, 0))],
        out_specs=pl.BlockSpec((tm, D_MODEL), lambda i: (i, 0)),
        out_shape=jax.ShapeDtypeStruct((t, D_MODEL), BF16),
        compiler_params=_params(1),
    )(x, g)


def _rope_rows():
    inv = (np.float32(ROPE_THETA) ** (-np.arange(0, ROPE, 2, dtype=np.float32) / np.float32(ROPE))).astype(np.float32)
    invf = np.zeros((1, LANE), np.float32)
    sgn = np.zeros((1, LANE), np.float32)
    invf[0, 0:32] = inv
    invf[0, 64:96] = inv
    sgn[0, 0:32] = -1.0
    sgn[0, 64:96] = 1.0
    return jnp.asarray(invf), jnp.asarray(sgn)


def _rot(v, cos_t, sin_t):
    return v * cos_t + pltpu.roll(v, 64, 1) * sin_t


def _rot_bwd(dv, cos_t, sin_t):
    return dv * cos_t + pltpu.roll(dv * sin_t, 64, 1)


def _mid_fwd(proj, flog, g_q, g_kv, bf_row, pos_col, *, tm):
    t = proj.shape[0]
    invf, sgn = _rope_rows()

    def body(p_ref, fl_ref, gq_ref, gkv_ref, bf_ref, pos_ref, invf_ref, sgn_ref,
             qn_ref, kvn_ref, kr_ref, cos_ref, sin_ref, c_ref, carry_ref):
        i = pl.program_id(0)

        @pl.when(i == 0)
        def _():
            carry_ref[...] = jnp.zeros_like(carry_ref)

        ql = p_ref[:, QL0:KVL0].astype(F32)
        r = lax.rsqrt(jnp.mean(ql * ql, axis=-1, keepdims=True) + NORM_EPS)
        qn_ref[...] = (ql * r * gq_ref[...]).astype(BF16)
        kvl = p_ref[:, KVL0:KR0].astype(F32)
        r = lax.rsqrt(jnp.mean(kvl * kvl, axis=-1, keepdims=True) + NORM_EPS)
        kvn_ref[...] = (kvl * r * gkv_ref[...]).astype(BF16)

        ang = pos_ref[...] * invf_ref[...]
        cos_t = jnp.cos(ang)
        sin_t = jnp.sin(ang) * sgn_ref[...]
        cos_ref[...] = cos_t
        sin_ref[...] = sin_t
        kr_ref[...] = _rot(p_ref[:, KR0:GM0].astype(F32), cos_t, sin_t).astype(BF16)

        z = fl_ref[...] + bf_ref[...]
        logf = jnp.minimum(z, 0.0) - jnp.log(1.0 + jnp.exp(-jnp.abs(z)))
        row = lax.broadcasted_iota(jnp.int32, (tm, tm), 0)
        col = lax.broadcasted_iota(jnp.int32, (tm, tm), 1)
        tri = (col <= row).astype(BF16)
        acc = carry_ref[0:1, :]
        for part in _split3(logf):
            acc = acc + jnp.dot(tri, part, preferred_element_type=F32)
        c_ref[...] = acc
        carry_ref[0:1, :] = carry_ref[0:1, :] + jnp.sum(logf, axis=0, keepdims=True)

    row_spec = lambda w: pl.BlockSpec((tm, w), lambda i: (i, 0))
    vec_spec = lambda w: pl.BlockSpec((1, w), lambda i: (0, 0))
    return pl.pallas_call(
        body, name="mid_fwd", grid=(t // tm,),
        in_specs=[row_spec(GM0), row_spec(LANE), vec_spec(Q_RANK), vec_spec(KV_RANK), vec_spec(LANE),
                  pl.BlockSpec((tm, 1), lambda i: (i, 0)), vec_spec(LANE), vec_spec(LANE)],
        out_specs=[row_spec(Q_RANK), row_spec(KV_RANK), row_spec(LANE), row_spec(LANE), row_spec(LANE), row_spec(LANE)],
        out_shape=[jax.ShapeDtypeStruct((t, Q_RANK), BF16), jax.ShapeDtypeStruct((t, KV_RANK), BF16),
                   jax.ShapeDtypeStruct((t, LANE), BF16), jax.ShapeDtypeStruct((t, LANE), F32),
                   jax.ShapeDtypeStruct((t, LANE), F32), jax.ShapeDtypeStruct((t, LANE), F32)],
        scratch_shapes=[pltpu.VMEM((8, LANE), F32)],
        compiler_params=_params(1),
    )(proj, flog, g_q, g_kv, bf_row, pos_col, invf, sgn)


def _rope_q(q_raw, cos_t, sin_t, *, tm):
    t = q_raw.shape[0]

    def body(q_ref, cos_ref, sin_ref, o_ref):
        c, s = cos_ref[...], sin_ref[...]
        for h in range(N_HEADS):
            o_ref[:, 256 * h:256 * h + 128] = q_ref[:, 256 * h:256 * h + 128].astype(BF16)
            o_ref[:, 256 * h + 128:256 * h + 256] = _rot(q_ref[:, 256 * h + 128:256 * h + 256], c, s).astype(BF16)

    return pl.pallas_call(
        body, name="rope_q", grid=(t // tm,),
        in_specs=[pl.BlockSpec((tm, 2048), lambda i: (i, 0)), pl.BlockSpec((tm, LANE), lambda i: (i, 0)),
                  pl.BlockSpec((tm, LANE), lambda i: (i, 0))],
        out_specs=pl.BlockSpec((tm, 2048), lambda i: (i, 0)),
        out_shape=jax.ShapeDtypeStruct((t, 2048), BF16),
        compiler_params=_params(1),
    )(q_raw, cos_t, sin_t)


def _attn_fwd(fox, operands, *, t, tb, name):
    nb = t // tb
    scale = FOX_SCALE if fox else MLA_SCALE

    def body(*refs):
        if fox:
            q_ref, k_ref, v_ref, gate_ref, cq_ref, ck_ref, o_ref, og_ref, lse_ref, m_s, l_s, acc_s = refs
        else:
            q_ref, kn_ref, kr_ref, v_ref, gate_ref, o_ref, og_ref, lse_ref, m_s, l_s, acc_s = refs
        qi = pl.program_id(1)
        q = q_ref[...]
        m_s[...] = jnp.full_like(m_s, -jnp.inf)
        l_s[...] = jnp.zeros_like(l_s)
        acc_s[...] = jnp.zeros_like(acc_s)

        def chunk(kc, masked):
            off = pl.multiple_of(kc * tb, tb)
            if fox:
                kk = k_ref[pl.ds(off, tb), :]
            else:
                kk = jnp.concatenate([kn_ref[pl.ds(off, tb), :], kr_ref[pl.ds(off, tb), :]], axis=1)
            s = lax.dot_general(q, kk, (((1,), (1,)), ((), ())), preferred_element_type=F32) * scale
            if fox:
                s = s + cq_ref[0] - ck_ref[0, kc]
            if masked:
                row = lax.broadcasted_iota(jnp.int32, (tb, tb), 0)
                col = lax.broadcasted_iota(jnp.int32, (tb, tb), 1)
                s = jnp.where(col <= row, s, -jnp.inf)
            m_prev = m_s[...]
            m_new = jnp.maximum(m_prev, jnp.max(s, axis=1, keepdims=True))
            alpha = jnp.exp(m_prev - m_new)
            p = jnp.exp(s - m_new)
            l_s[...] = alpha * l_s[...] + jnp.sum(p, axis=1, keepdims=True)
            acc_s[...] = alpha * acc_s[...] + jnp.dot(p.astype(BF16), v_ref[pl.ds(off, tb), :],
                                                      preferred_element_type=F32)
            m_s[...] = m_new

        def loop_body(kc, carry):
            chunk(kc, False)
            return carry

        lax.fori_loop(0, qi, loop_body, 0)
        chunk(qi, True)
        o = acc_s[...] / l_s[...]
        o_ref[...] = o
        g = gate_ref[...].astype(F32)
        og_ref[...] = (o * (g * _sigmoid(g))).astype(BF16)
        lse_ref[0] = m_s[...] + jnp.log(l_s[...])

    if fox:
        proj, c_col, c_row4 = operands
        ins = [proj, proj, proj, proj, c_col, c_row4]
        in_specs = [pl.BlockSpec((tb, HEAD), lambda h, i: (i, FQ0 // HEAD + h)),
                    pl.BlockSpec((t, HEAD), lambda h, i: (0, FK0 // HEAD + h)),
                    pl.BlockSpec((t, HEAD), lambda h, i: (0, FV0 // HEAD + h)),
                    pl.BlockSpec((tb, HEAD), lambda h, i: (i, GF0 // HEAD + h)),
                    pl.BlockSpec((1, tb, 1), lambda h, i: (h, i, 0)),
                    pl.BlockSpec((1, nb, 1, tb), lambda h, i: (h, 0, 0, 0))]
    else:
        q_full, kv, kr, proj = operands
        ins = [q_full, kv, kr, kv, proj]
        in_specs = [pl.BlockSpec((tb, 256), lambda h, i: (i, h)),
                    pl.BlockSpec((t, HEAD), lambda h, i: (0, 2 * h)),
                    pl.BlockSpec((t, HEAD), lambda h, i: (0, 0)),
                    pl.BlockSpec((t, HEAD), lambda h, i: (0, 2 * h + 1)),
                    pl.BlockSpec((tb, HEAD), lambda h, i: (i, GM0 // HEAD + h))]
    return pl.pallas_call(
        body, name=name, grid=(N_HEADS, nb), in_specs=in_specs,
        out_specs=[pl.BlockSpec((tb, HEAD), lambda h, i: (i, h)), pl.BlockSpec((tb, HEAD), lambda h, i: (i, h)),
                   pl.BlockSpec((1, tb, 1), lambda h, i: (h, i, 0))],
        out_shape=[jax.ShapeDtypeStruct((t, N_HEADS * HEAD), F32), jax.ShapeDtypeStruct((t, N_HEADS * HEAD), BF16),
                   jax.ShapeDtypeStruct((N_HEADS, t, 1), F32)],
        scratch_shapes=[pltpu.VMEM((tb, 1), F32), pltpu.VMEM((tb, 1), F32), pltpu.VMEM((tb, HEAD), F32)],
        compiler_params=_params(2),
    )(*ins)


def _postnorm_loss(o, x, target, g, *, tm):
    t = o.shape[0]

    def body(o_ref, x_ref, t_ref, g_ref, dy_ref, do_ref, dg_ref, loss_ref):
        i = pl.program_id(0)

        @pl.when(i == 0)
        def _():
            dg_ref[...] = jnp.zeros_like(dg_ref)
            loss_ref[...] = jnp.zeros_like(loss_ref)

        ov = o_ref[...]
        gv = g_ref[...]
        r = lax.rsqrt(jnp.mean(ov * ov, axis=-1, keepdims=True) + NORM_EPS)
        oh = ov * r
        e = x_ref[...] + oh * gv - t_ref[...]
        loss_ref[...] += 0.5 * jnp.sum(jnp.mean(e * e, axis=-1, keepdims=True), axis=0, keepdims=True)
        dy = e * (1.0 / D_MODEL)
        dy_ref[...] = dy
        dyg = dy * gv
        do_ref[...] = (r * (dyg - oh * jnp.mean(dyg * oh, axis=-1, keepdims=True))).astype(BF16)
        dg_ref[...] += jnp.sum(dy * oh, axis=0, keepdims=True)

    row = pl.BlockSpec((tm, D_MODEL), lambda i: (i, 0))
    vec = pl.BlockSpec((1, D_MODEL), lambda i: (0, 0))
    return pl.pallas_call(
        body, name="postnorm_loss", grid=(t // tm,),
        in_specs=[row, row, row, vec],
        out_specs=[row, row, vec, pl.BlockSpec((1, 1), lambda i: (0, 0))],
        out_shape=[jax.ShapeDtypeStruct((t, D_MODEL), F32), jax.ShapeDtypeStruct((t, D_MODEL), BF16),
                   jax.ShapeDtypeStruct((1, D_MODEL), F32), jax.ShapeDtypeStruct((1, 1), F32)],
        compiler_params=_params(1),
    )(o, x, target, g)


def _gate_bwd(d_og, o_mla, o_fox, proj, *, tm):
    t = d_og.shape[0]

    def body(dog_ref, om_ref, of_ref, p_ref, do_ref, dgate_ref, delta_ref):
        lane = lax.broadcasted_iota(jnp.int32, (tm, LANE), 1)
        delta = jnp.zeros((tm, LANE), F32)
        for b, (o_ref, g0) in enumerate(((om_ref, GM0), (of_ref, GF0))):
            for h in range(N_HEADS):
                cols = slice(1024 * b + HEAD * h, 1024 * b + HEAD * (h + 1))
                g = p_ref[:, g0 + HEAD * h:g0 + HEAD * (h + 1)].astype(F32)
                ov = o_ref[:, HEAD * h:HEAD * (h + 1)]
                dog = dog_ref[:, cols]
                sg = _sigmoid(g)
                d_o = dog * (g * sg)
                do_ref[:, cols] = d_o.astype(BF16)
                dgate_ref[:, cols] = (dog * ov * (sg * (1.0 + g * (1.0 - sg)))).astype(BF16)
                delta = jnp.where(lane == 8 * b + h, jnp.sum(d_o * ov, axis=-1, keepdims=True), delta)
        delta_ref[...] = delta

    row = lambda w: pl.BlockSpec((tm, w), lambda i: (i, 0))
    return pl.pallas_call(
        body, name="gate_bwd", grid=(t // tm,),
        in_specs=[row(2048), row(1024), row(1024), row(NP_IN)],
        out_specs=[row(2048), row(2048), row(LANE)],
        out_shape=[jax.ShapeDtypeStruct((t, 2048), BF16), jax.ShapeDtypeStruct((t, 2048), BF16),
                   jax.ShapeDtypeStruct((t, LANE), F32)],
        compiler_params=_params(1),
    )(d_og, o_mla, o_fox, proj)


def _attn_bwd(fox, operands, *, t, tb, name):
    nb = t // tb
    scale = FOX_SCALE if fox else MLA_SCALE
    dq_w = HEAD if fox else 256
    nt_dims = (((1,), (1,)), ((), ()))
    tn_dims = (((0,), (0,)), ((), ()))

    def body(*refs):
        if fox:
            (q_ref, k_ref, v_ref, do_ref, lse_ref, dl_ref, cq_ref, ck_ref,
             dq_ref, dk_ref, dv_ref, dck_ref, dcq_ref, dq_s, dk_s, dv_s, dc_s, dcq_s) = refs
        else:
            (q_ref, kn_ref, kr_ref, v_ref, do_ref, lse_ref, dl_ref,
             dq_ref, dkv_ref, dkr_ref, dq_s, dk_s, dv_s) = refs
        ki = pl.program_id(1)

        @pl.when(ki == 0)
        def _():
            dq_s[...] = jnp.zeros_like(dq_s)
            if fox:
                dcq_s[...] = jnp.zeros_like(dcq_s)

        dk_s[...] = jnp.zeros_like(dk_s)
        dv_s[...] = jnp.zeros_like(dv_s)
        if fox:
            dc_s[...] = jnp.zeros_like(dc_s)
            kk = k_ref[...]
        else:
            kk = jnp.concatenate([kn_ref[...], kr_ref[...]], axis=1)
        vv = v_ref[...]

        def chunk(qc, masked):
            off = pl.multiple_of(qc * tb, tb)
            qq = q_ref[pl.ds(off, tb), :]
            dd = do_ref[pl.ds(off, tb), :]
            s = lax.dot_general(kk, qq, nt_dims, preferred_element_type=F32) * scale
            if fox:
                s = s + cq_ref[0, qc] - ck_ref[0]
            if masked:
                row = lax.broadcasted_iota(jnp.int32, (tb, tb), 0)
                col = lax.broadcasted_iota(jnp.int32, (tb, tb), 1)
                s = jnp.where(row <= col, s, -jnp.inf)
            p = jnp.exp(s - lse_ref[0, qc])
            dv_s[...] += jnp.dot(p.astype(BF16), dd, preferred_element_type=F32)
            dp = lax.dot_general(vv, dd, nt_dims, preferred_element_type=F32)
            ds = p * (dp - dl_ref[0, qc])
            if fox:
                dc_s[...] += jnp.sum(ds, axis=1, keepdims=True)
                dcq_s[qc] += jnp.sum(ds, axis=0, keepdims=True)
            dsb = (ds * scale).astype(BF16)
            dk_s[...] += jnp.dot(dsb, qq, preferred_element_type=F32)
            dq_s[pl.ds(off, tb), :] += lax.dot_general(dsb, kk, tn_dims, preferred_element_type=F32)

        chunk(ki, True)

        def loop_body(qc, carry):
            chunk(qc, False)
            return carry

        lax.fori_loop(ki + 1, nb, loop_body, 0)
        if fox:
            dk_ref[...] = dk_s[...].astype(BF16)
            dv_ref[...] = dv_s[...].astype(BF16)
            dck_ref[0] = -dc_s[...]
        else:
            dkv_ref[...] = jnp.concatenate([dk_s[:, :HEAD], dv_s[...]], axis=1).astype(BF16)
            dkr_ref[...] = dk_s[:, HEAD:]

        @pl.when(ki == nb - 1)
        def _():
            dq_ref[...] = dq_s[...].astype(dq_ref.dtype)
            if fox:
                dcq_ref[0] = dcq_s[...]

    stat = pl.BlockSpec((1, nb, 1, tb), lambda h, i: (h, 0, 0, 0))
    if fox:
        proj, d_o, lse4, delta4, c_row4, c_col = operands
        ins = [proj, proj, proj, d_o, lse4, delta4, c_row4, c_col]
        in_specs = [pl.BlockSpec((t, HEAD), lambda h, i: (0, FQ0 // HEAD + h)),
                    pl.BlockSpec((tb, HEAD), lambda h, i: (i, FK0 // HEAD + h)),
                    pl.BlockSpec((tb, HEAD), lambda h, i: (i, FV0 // HEAD + h)),
                    pl.BlockSpec((t, HEAD), lambda h, i: (0, N_HEADS + h)),
                    stat, stat, stat, pl.BlockSpec((1, tb, 1), lambda h, i: (h, i, 0))]
        out_specs = [pl.BlockSpec((t, HEAD), lambda h, i: (0, h)), pl.BlockSpec((tb, HEAD), lambda h, i: (i, h)),
                     pl.BlockSpec((tb, HEAD), lambda h, i: (i, h)), pl.BlockSpec((1, tb, 1), lambda h, i: (h, i, 0)),
                     stat]
        out_shape = [jax.ShapeDtypeStruct((t, 1024), BF16), jax.ShapeDtypeStruct((t, 1024), BF16),
                     jax.ShapeDtypeStruct((t, 1024), BF16), jax.ShapeDtypeStruct((N_HEADS, t, 1), F32),
                     jax.ShapeDtypeStruct((N_HEADS, nb, 1, tb), F32)]
        scratch = [pltpu.VMEM((t, HEAD), F32), pltpu.VMEM((tb, HEAD), F32), pltpu.VMEM((tb, HEAD), F32),
                   pltpu.VMEM((tb, 1), F32), pltpu.VMEM((nb, 1, tb), F32)]
    else:
        q_full, kv, kr, d_o, lse4, delta4 = operands
        ins = [q_full, kv, kr, kv, d_o, lse4, delta4]
        in_specs = [pl.BlockSpec((t, 256), lambda h, i: (0, h)),
                    pl.BlockSpec((tb, HEAD), lambda h, i: (i, 2 * h)),
                    pl.BlockSpec((tb, HEAD), lambda h, i: (i, 0)),
                    pl.BlockSpec((tb, HEAD), lambda h, i: (i, 2 * h + 1)),
                    pl.BlockSpec((t, HEAD), lambda h, i: (0, h)),
                    stat, stat]
        out_specs = [pl.BlockSpec((t, 256), lambda h, i: (0, h)), pl.BlockSpec((tb, 256), lambda h, i: (i, h)),
                     pl.BlockSpec((tb, HEAD), lambda h, i: (i, h))]
        out_shape = [jax.ShapeDtypeStruct((t, 2048), F32), jax.ShapeDtypeStruct((t, 2048), BF16),
                     jax.ShapeDtypeStruct((t, 1024), F32)]
        scratch = [pltpu.VMEM((t, 256), F32), pltpu.VMEM((tb, 256), F32), pltpu.VMEM((tb, HEAD), F32)]
    return pl.pallas_call(
        body, name=name, grid=(N_HEADS, nb), in_specs=in_specs, out_specs=out_specs, out_shape=out_shape,
        scratch_shapes=scratch, compiler_params=_params(2),
    )(*ins)


def _mid_bwd(dq_full, dkr, cos_t, sin_t, dck, flog, bf_row, *, tm):
    t = dq_full.shape[0]
    n = t // tm

    def body(dq_ref, dkr_ref, cos_ref, sin_ref, dck_ref, fl_ref, bf_ref,
             dq2_ref, dkraw_ref, dfl_ref, dbf_ref, carry_ref):
        i = pl.program_id(0)

        @pl.when(i == 0)
        def _():
            carry_ref[...] = jnp.zeros_like(carry_ref)
            dbf_ref[...] = jnp.zeros_like(dbf_ref)

        c, s = cos_ref[...], sin_ref[...]
        dkr_sum = jnp.zeros((tm, LANE), F32)
        for h in range(N_HEADS):
            dq2_ref[:, 256 * h:256 * h + 128] = dq_ref[:, 256 * h:256 * h + 128].astype(BF16)
            dq2_ref[:, 256 * h + 128:256 * h + 256] = _rot_bwd(dq_ref[:, 256 * h + 128:256 * h + 256], c, s).astype(BF16)
            dkr_sum = dkr_sum + dkr_ref[:, HEAD * h:HEAD * (h + 1)]
        dkraw_ref[...] = _rot_bwd(dkr_sum, c, s).astype(BF16)

        dc = dck_ref[...]
        row = lax.broadcasted_iota(jnp.int32, (tm, tm), 0)
        col = lax.broadcasted_iota(jnp.int32, (tm, tm), 1)
        tri = (col >= row).astype(BF16)
        acc = carry_ref[0:1, :]
        for part in _split3(dc):
            acc = acc + jnp.dot(tri, part, preferred_element_type=F32)
        carry_ref[0:1, :] = carry_ref[0:1, :] + jnp.sum(dc, axis=0, keepdims=True)
        z = fl_ref[...] + bf_ref[...]
        dz = acc / (1.0 + jnp.exp(z))
        dfl_ref[...] = dz.astype(BF16)
        dbf_ref[...] += jnp.sum(dz, axis=0, keepdims=True)

    rev = lambda w: pl.BlockSpec((tm, w), lambda i: (n - 1 - i, 0))
    vec = lambda w: pl.BlockSpec((1, w), lambda i: (0, 0))
    return pl.pallas_call(
        body, name="mid_bwd", grid=(n,),
        in_specs=[rev(2048), rev(1024), rev(LANE), rev(LANE), rev(LANE), rev(LANE), vec(LANE)],
        out_specs=[rev(2048), rev(LANE), rev(LANE), vec(LANE)],
        out_shape=[jax.ShapeDtypeStruct((t, 2048), BF16), jax.ShapeDtypeStruct((t, LANE), BF16),
                   jax.ShapeDtypeStruct((t, LANE), BF16), jax.ShapeDtypeStruct((1, LANE), F32)],
        scratch_shapes=[pltpu.VMEM((8, LANE), F32)],
        compiler_params=_params(1),
    )(dq_full, dkr, cos_t, sin_t, dck, flog, bf_row)


def _norm_bwd(proj, dqn, dkvn, g_q, g_kv, *, tm):
    t = proj.shape[0]

    def body(p_ref, dqn_ref, dkvn_ref, gq_ref, gkv_ref, dql_ref, dkvl_ref, dgq_ref, dgkv_ref):
        i = pl.program_id(0)

        @pl.when(i == 0)
        def _():
            dgq_ref[...] = jnp.zeros_like(dgq_ref)
            dgkv_ref[...] = jnp.zeros_like(dgkv_ref)

        for lo, hi, dn_ref, g_ref, dx_ref, dg_ref in ((QL0, KVL0, dqn_ref, gq_ref, dql_ref, dgq_ref),
                                                     (KVL0, KR0, dkvn_ref, gkv_ref, dkvl_ref, dgkv_ref)):
            xv = p_ref[:, lo:hi].astype(F32)
            r = lax.rsqrt(jnp.mean(xv * xv, axis=-1, keepdims=True) + NORM_EPS)
            xh = xv * r
            dn = dn_ref[...]
            dg_ref[...] += jnp.sum(dn * xh, axis=0, keepdims=True)
            dxh = dn * g_ref[...]
            dx_ref[...] = (r * (dxh - xh * jnp.mean(dxh * xh, axis=-1, keepdims=True))).astype(BF16)

    row = lambda w: pl.BlockSpec((tm, w), lambda i: (i, 0))
    vec = lambda w: pl.BlockSpec((1, w), lambda i: (0, 0))
    return pl.pallas_call(
        body, name="norm_bwd", grid=(t // tm,),
        in_specs=[row(GM0), row(Q_RANK), row(KV_RANK), vec(Q_RANK), vec(KV_RANK)],
        out_specs=[row(Q_RANK), row(KV_RANK), vec(Q_RANK), vec(KV_RANK)],
        out_shape=[jax.ShapeDtypeStruct((t, Q_RANK), BF16), jax.ShapeDtypeStruct((t, KV_RANK), BF16),
                   jax.ShapeDtypeStruct((1, Q_RANK), F32), jax.ShapeDtypeStruct((1, KV_RANK), F32)],
        compiler_params=_params(1),
    )(proj, dqn, dkvn, g_q, g_kv)


def _prenorm_bwd(dh, x, g, dy, *, tm):
    t = x.shape[0]

    def body(dh_ref, x_ref, g_ref, dy_ref, gx_ref, dg_ref):
        i = pl.program_id(0)

        @pl.when(i == 0)
        def _():
            dg_ref[...] = jnp.zeros_like(dg_ref)

        xv = x_ref[...]
        r = lax.rsqrt(jnp.mean(xv * xv, axis=-1, keepdims=True) + NORM_EPS)
        xh = xv * r
        dn = dh_ref[...]
        dg_ref[...] += jnp.sum(dn * xh, axis=0, keepdims=True)
        dxh = dn * g_ref[...]
        gx_ref[...] = dy_ref[...] + r * (dxh - xh * jnp.mean(dxh * xh, axis=-1, keepdims=True))

    row = pl.BlockSpec((tm, D_MODEL), lambda i: (i, 0))
    vec = pl.BlockSpec((1, D_MODEL), lambda i: (0, 0))
    return pl.pallas_call(
        body, name="prenorm_bwd", grid=(t // tm,),
        in_specs=[row, row, vec, row], out_specs=[row, vec],
        out_shape=[jax.ShapeDtypeStruct((t, D_MODEL), F32), jax.ShapeDtypeStruct((1, D_MODEL), F32)],
        compiler_params=_params(1),
    )(dh, x, g, dy)


def _adam_math(w, g, m, v):
    m = ADAM_B1 * m + (1.0 - ADAM_B1) * g
    v = ADAM_B2 * v + (1.0 - ADAM_B2) * (g * g)
    m_hat = m / (1.0 - ADAM_B1 ** ADAM_STEP)
    v_hat = v / (1.0 - ADAM_B2 ** ADAM_STEP)
    delta = -ADAM_LR * (m_hat / (jnp.sqrt(v_hat) + ADAM_EPS) + ADAM_WD * w)
    return delta, m, v


def _adamw(land, w, m, v, *, tr, name):
    rows, cols = w.shape

    def body(l_ref, w_ref, m_ref, v_ref, g_ref, d_ref, nm_ref, nv_ref):
        g = l_ref[0].astype(F32)
        for s in range(1, N_DEV):
            g = g + l_ref[s].astype(F32)
        g_ref[...] = g
        d_ref[...], nm_ref[...], nv_ref[...] = _adam_math(w_ref[...], g, m_ref[...], v_ref[...])

    blk = pl.BlockSpec((tr, cols), lambda i: (i, 0))
    return pl.pallas_call(
        body, name=name, grid=(rows // tr,),
        in_specs=[pl.BlockSpec((N_DEV, tr, cols), lambda i: (0, i, 0)), blk, blk, blk],
        out_specs=[blk, blk, blk, blk],
        out_shape=[jax.ShapeDtypeStruct((rows, cols), F32)] * 4,
        compiler_params=_params(1),
    )(land, w, m, v)


def _adamw_small(gathered, w, m, v):
    def body(a_ref, w_ref, m_ref, v_ref, g_ref, d_ref, nm_ref, nv_ref):
        g = a_ref[0:SMALL_ROWS, :]
        for s in range(1, N_DEV):
            g = g + a_ref[SMALL_ROWS * s:SMALL_ROWS * (s + 1), :]
        g_ref[...] = g
        d_ref[...], nm_ref[...], nv_ref[...] = _adam_math(w_ref[...], g, m_ref[...], v_ref[...])

    return pl.pallas_call(
        body, name="adamw_small",
        out_shape=[jax.ShapeDtypeStruct((SMALL_ROWS, SMALL_COLS), F32)] * 4,
        compiler_params=_params(),
    )(gathered, w, m, v)


def _place():
    x, y, c = lax.axis_index("x"), lax.axis_index("y"), lax.axis_index("c")
    return x, y, c


def _flip(p, k):
    x, y, c = p
    return (1 - x if k & 4 else x, 1 - y if k & 2 else y, 1 - c if k & 1 else c)


def _index(p):
    return 4 * p[0] + 2 * p[1] + p[2]


def _all_gather(shards):
    n = len(shards)
    hbm = pl.BlockSpec(memory_space=pl.ANY)

    def body(*refs):
        ins, outs = refs[:n], refs[n:2 * n]
        send_sems, recv_sems, local_sems = refs[2 * n:]
        me = _place()
        sibling = _flip(me, 1)
        chips = [_flip(me, 4), _flip(me, 2), _flip(me, 6)]

        def copy(a, k, block, to, src=None):
            dst = outs[a].at[_index(block)]
            return pltpu.make_async_remote_copy(
                src_ref=dst if src is None else src, dst_ref=dst,
                send_sem=send_sems.at[7 * a + k], recv_sem=recv_sems.at[7 * a + k],
                device_id=to, device_id_type=MESH)

        started = []
        for a in range(n):
            mine = pltpu.make_async_copy(ins[a], outs[a].at[_index(me)], local_sems.at[a])
            mine.start()
            started.append(mine)
        first = []
        for a in range(n):
            first.append(copy(a, 0, me, sibling, src=ins[a]))
            first += [copy(a, 1 + j, me, chip, src=ins[a]) for j, chip in enumerate(chips)]
        for cp in first:
            cp.start()
        passed = []
        for a in range(n):
            for j, chip in enumerate(chips):
                copy(a, 1 + j, chip, me).wait_recv()
                fwd = copy(a, 4 + j, chip, sibling)
                fwd.start()
                passed.append(fwd)
        for a in range(n):
            copy(a, 0, sibling, me).wait_recv()
            for j, chip in enumerate(chips):
                copy(a, 4 + j, _flip(chip, 1), me).wait_recv()
        for cp in first + passed:
            cp.wait_send()
        for mine in started:
            mine.wait()

    return pl.pallas_call(
        body, name="all_gather_weights",
        in_specs=[hbm] * n, out_specs=[hbm] * n,
        out_shape=[jax.ShapeDtypeStruct((N_DEV,) + s.shape, s.dtype) for s in shards],
        scratch_shapes=[pltpu.SemaphoreType.DMA((7 * n,)), pltpu.SemaphoreType.DMA((7 * n,)),
                        pltpu.SemaphoreType.DMA((n,))],
    )(*shards)


def _exchange_partials(parts):
    n = len(parts)
    hbm = pl.BlockSpec(memory_space=pl.ANY)

    def body(*refs):
        srcs, lands = refs[:n], refs[n:2 * n]
        send_sems, recv_sems, local_sems = refs[2 * n:]
        me = _place()
        my = _index(me)
        started, sends = [], []
        for a in range(n):
            mine = pltpu.make_async_copy(srcs[a].at[my], lands[a].at[my], local_sems.at[a])
            mine.start()
            started.append(mine)
        for k in range(1, N_DEV):
            peer = _flip(me, k)
            for a in range(n):
                cp = pltpu.make_async_remote_copy(
                    src_ref=srcs[a].at[_index(peer)], dst_ref=lands[a].at[my],
                    send_sem=send_sems.at[7 * a + k - 1], recv_sem=recv_sems.at[7 * a + k - 1],
                    device_id=peer, device_id_type=MESH)
                cp.start()
                sends.append(cp)
        for k in range(1, N_DEV):
            peer = _flip(me, k)
            for a in range(n):
                slot = lands[a].at[_index(peer)]
                pltpu.make_async_remote_copy(
                    src_ref=slot, dst_ref=slot,
                    send_sem=send_sems.at[7 * a + k - 1], recv_sem=recv_sems.at[7 * a + k - 1],
                    device_id=peer, device_id_type=MESH).wait_recv()
        for cp in sends:
            cp.wait_send()
        for mine in started:
            mine.wait()

    return pl.pallas_call(
        body, name="exchange_partials",
        in_specs=[hbm] * n, out_specs=[hbm] * n,
        out_shape=[jax.ShapeDtypeStruct(p.shape, p.dtype) for p in parts],
        scratch_shapes=[pltpu.SemaphoreType.DMA((7 * n,)), pltpu.SemaphoreType.DMA((7 * n,)),
                        pltpu.SemaphoreType.DMA((n,))],
    )(*parts)


def _gather_small(vec):
    def body(v_ref, out_ref, send_sems, recv_sems, local_sem):
        me = _place()

        def rows(p):
            return out_ref.at[pl.ds(pl.multiple_of(_index(p) * SMALL_ROWS, SMALL_ROWS), SMALL_ROWS), :]

        mine = pltpu.make_async_copy(v_ref, rows(me), local_sem)
        mine.start()
        sends = []
        for k in range(1, N_DEV):
            peer = _flip(me, k)
            cp = pltpu.make_async_remote_copy(src_ref=v_ref, dst_ref=rows(me), send_sem=send_sems.at[k - 1],
                                              recv_sem=recv_sems.at[k - 1], device_id=peer, device_id_type=MESH)
            cp.start()
            sends.append(cp)
        for k in range(1, N_DEV):
            peer = _flip(me, k)
            pltpu.make_async_remote_copy(src_ref=rows(peer), dst_ref=rows(peer), send_sem=send_sems.at[k - 1],
                                         recv_sem=recv_sems.at[k - 1], device_id=peer, device_id_type=MESH).wait_recv()
        for cp in sends:
            cp.wait_send()
        mine.wait()

    return pl.pallas_call(
        body, name="gather_small",
        in_specs=[pl.BlockSpec(memory_space=pltpu.VMEM)], out_specs=pl.BlockSpec(memory_space=pltpu.VMEM),
        out_shape=jax.ShapeDtypeStruct((N_DEV * SMALL_ROWS, SMALL_COLS), F32),
        scratch_shapes=[pltpu.SemaphoreType.DMA((7,)), pltpu.SemaphoreType.DMA((7,)), pltpu.SemaphoreType.DMA],
    )(vec)


def _w_in_nice(gathered):
    w = gathered.transpose(1, 0, 2).reshape(D_MODEL, D_IN)
    z = lambda n: jnp.zeros((D_MODEL, n), w.dtype)
    return jnp.concatenate([w[:, :1280], w[:, 1280:1312], z(32), w[:, 1312:1344], z(32), w[:, 1344:5440],
                            w[:, 5448:6472], w[:, 5440:5448], z(LANE - N_HEADS)], axis=1)


def _w_in_shards(dw):
    w = jnp.concatenate([dw[:, :1280], dw[:, 1280:1312], dw[:, 1344:1376], dw[:, GM0:GF0],
                         dw[:, FL0:FL0 + N_HEADS], dw[:, GF0:FL0]], axis=1)
    return w.reshape(D_MODEL, N_DEV, SHARD_IN).transpose(1, 0, 2)


def _w_uq_nice(shard):
    z = jnp.zeros((Q_RANK, 32), shard.dtype)
    return jnp.concatenate([shard[:, :128], shard[:, 128:160], z, shard[:, 160:192], z], axis=1)


def _pack_small(g_pre, g_post, g_q, g_kv, b_f, extra=None):
    parts = [g_pre.reshape(-1), g_post.reshape(-1), g_q.reshape(-1), g_kv.reshape(-1), b_f.reshape(-1)]
    if extra is not None:
        parts.append(extra.reshape(-1))
    flat = jnp.concatenate(parts)
    flat = jnp.concatenate([flat, jnp.zeros((SMALL_ROWS * SMALL_COLS - flat.shape[0],), F32)])
    return flat.reshape(SMALL_ROWS, SMALL_COLS)


def _unpack_small(packed):
    flat = packed.reshape(-1)
    o = 0
    out = []
    for n in (D_MODEL, D_MODEL, Q_RANK, KV_RANK, N_HEADS):
        out.append(flat[o:o + n].reshape(1, n))
        o += n
    return out, flat[o]


def kernel(x, positions, g_pre, w_in, g_q_latent, w_uq, g_kv_latent, w_ukv, b_forget, w_out, g_post, loss_target, m_g_pre, m_w_in, m_g_q_latent, m_w_uq, m_g_kv_latent, m_w_ukv, m_b_forget, m_w_out, m_g_post, v_g_pre, v_w_in, v_g_q_latent, v_w_uq, v_g_kv_latent, v_w_ukv, v_b_forget, v_w_out, v_g_post):
    t = x.shape[1]
    tb = min(512, t)
    tm = min(256, t)
    nb = t // tb
    x2 = x.reshape(t, D_MODEL)
    target = loss_target.reshape(t, D_MODEL)
    pos_col = positions.reshape(t, 1).astype(F32)
    bf_row = jnp.concatenate([b_forget.reshape(1, N_HEADS), jnp.zeros((1, LANE - N_HEADS), F32)], axis=1)

    g_in, g_uq, g_ukv, g_out = _all_gather([
        w_in[0].astype(BF16), _w_uq_nice(w_uq[0].astype(BF16)), w_ukv[0].astype(BF16), w_out[0].astype(BF16)])
    w_in_n = _w_in_nice(g_in)
    w_uq_n = g_uq.transpose(1, 0, 2).reshape(Q_RANK, N_HEADS * 256)
    w_ukv_n = g_ukv.transpose(1, 0, 2).reshape(KV_RANK, N_HEADS * 256)
    w_out_n = g_out.reshape(D_MODEL, D_MODEL)

    h = _prenorm(x2, g_pre, tm=tm)
    proj = _mm(h, w_in_n, name="proj_in", out_dtype=BF16, tm=1024, tn=512, tk=2048)
    flog = _mm(h, w_in_n[:, FL0:], name="proj_flog", out_dtype=F32, tm=1024, tn=LANE, tk=2048)
    qn, kvn, kr, cos_t, sin_t, c = _mid_fwd(proj, flog, g_q_latent, g_kv_latent, bf_row, pos_col, tm=tm)
    q_raw = _mm(qn, w_uq_n, name="q_up", out_dtype=F32, tm=1024, tn=512, tk=Q_RANK)
    q_full = _rope_q(q_raw, cos_t, sin_t, tm=tm)
    kv = _mm(kvn, w_ukv_n, name="kv_up", out_dtype=BF16, tm=1024, tn=512, tk=KV_RANK)
    c_heads = c[:, :N_HEADS].T
    c_col = c_heads.reshape(N_HEADS, t, 1)
    c_row4 = c_heads.reshape(N_HEADS, nb, 1, tb)
    o_mla, og_mla, lse_mla = _attn_fwd(False, (q_full, kv, kr, proj), t=t, tb=tb, name="mla_fwd")
    o_fox, og_fox, lse_fox = _attn_fwd(True, (proj, c_col, c_row4), t=t, tb=tb, name="fox_fwd")
    og = jnp.concatenate([og_mla, og_fox], axis=1)
    o = _mm(og, w_out_n, name="out_proj", out_dtype=F32, tm=1024, tn=512, tk=2048)
    dy, d_o_post, dg_post, loss_part = _postnorm_loss(o, x2, target, g_post, tm=tm)

    d_og = _mm(d_o_post, w_out_n, name="d_og", nt=True, out_dtype=F32, tm=1024, tn=512, tk=2048)
    dw_out = _mm(og.T, d_o_post, name="dw_out", out_dtype=BF16, tm=1024, tn=512, tk=1024)
    d_attn, d_gate, delta = _gate_bwd(d_og, o_mla, o_fox, proj, tm=tm)
    delta4 = delta[:, :2 * N_HEADS].T.reshape(2 * N_HEADS, nb, 1, tb)
    lse4_mla = lse_mla.reshape(N_HEADS, nb, 1, tb)
    lse4_fox = lse_fox.reshape(N_HEADS, nb, 1, tb)
    dq_full, dkv, dkr = _attn_bwd(False, (q_full, kv, kr, d_attn, lse4_mla, delta4[:N_HEADS]),
                                  t=t, tb=tb, name="mla_bwd")
    dfq, dfk, dfv, dck, dcq = _attn_bwd(True, (proj, d_attn, lse4_fox, delta4[N_HEADS:], c_row4, c_col),
                                   t=t, tb=tb, name="fox_bwd")
    dc_heads = dck.reshape(N_HEADS, t) + dcq.reshape(N_HEADS, t)
    dck_rows = jnp.concatenate([dc_heads.T, jnp.zeros((t, LANE - N_HEADS), F32)], axis=1)
    dq2, dkr_raw, dfl, dbf = _mid_bwd(dq_full, dkr, cos_t, sin_t, dck_rows, flog, bf_row, tm=tm)
    dqn = _mm(dq2, w_uq_n, name="d_qn", nt=True, out_dtype=F32, tm=1024, tn=Q_RANK, tk=2048)
    dkvn = _mm(dkv, w_ukv_n, name="d_kvn", nt=True, out_dtype=F32, tm=1024, tn=KV_RANK, tk=2048)
    dw_uq = _mm(qn.T, dq2, name="dw_uq", out_dtype=BF16, tm=Q_RANK, tn=512, tk=1024)
    dw_ukv = _mm(kvn.T, dkv, name="dw_ukv", out_dtype=BF16, tm=KV_RANK, tn=512, tk=1024)
    dql, dkvl, dg_q, dg_kv = _norm_bwd(proj, dqn, dkvn, g_q_latent, g_kv_latent, tm=tm)
    dproj = jnp.concatenate([dql, dkvl, dkr_raw, d_gate[:, :1024], dfq, dfk, dfv, d_gate[:, 1024:], dfl], axis=1)
    dh = _mm(dproj, w_in_n, name="d_h", nt=True, out_dtype=F32, tm=1024, tn=512, tk=NP_IN // 4)
    dw_in = _mm(h.T, dproj, name="dw_in", out_dtype=BF16, tm=1024, tn=512, tk=1024)
    grad_x, dg_pre = _prenorm_bwd(dh, x2, g_pre, dy, tm=tm)

    dw_uq_h = dw_uq.reshape(Q_RANK, N_HEADS, 256)
    s_uq = jnp.concatenate([dw_uq_h[:, :, :160], dw_uq_h[:, :, 192:224]], axis=2).transpose(1, 0, 2)
    s_ukv = dw_ukv.reshape(KV_RANK, N_HEADS, 256).transpose(1, 0, 2)
    s_out = dw_out.reshape(N_DEV, D_MODEL // N_DEV, D_MODEL)
    l_in, l_uq, l_ukv, l_out = _exchange_partials([_w_in_shards(dw_in), s_uq, s_ukv, s_out])
    small = _gather_small(_pack_small(dg_pre, dg_post, dg_q, dg_kv, dbf[:, :N_HEADS], loss_part))

    res_in = _adamw(l_in, w_in[0], m_w_in[0], v_w_in[0], tr=256, name="adamw_w_in")
    res_uq = _adamw(l_uq, w_uq[0], m_w_uq[0], v_w_uq[0], tr=256, name="adamw_w_uq")
    res_ukv = _adamw(l_ukv, w_ukv[0], m_w_ukv[0], v_w_ukv[0], tr=256, name="adamw_w_ukv")
    res_out = _adamw(l_out, w_out[0], m_w_out[0], v_w_out[0], tr=128, name="adamw_w_out")
    res_small = _adamw_small(
        small,
        _pack_small(g_pre, g_post, g_q_latent, g_kv_latent, b_forget),
        _pack_small(m_g_pre, m_g_post, m_g_q_latent, m_g_kv_latent, m_b_forget),
        _pack_small(v_g_pre, v_g_post, v_g_q_latent, v_g_kv_latent, v_b_forget))
    small_out = [_unpack_small(r) for r in res_small]
    loss = small_out[0][1]

    def leaves(kind):
        (s_pre, s_post, s_q, s_kv, s_bf), _ = small_out[kind]
        return [s_pre, res_in[kind][None], s_q, res_uq[kind][None], s_kv, res_ukv[kind][None], s_bf,
                res_out[kind][None], s_post]

    return (loss, grad_x.reshape(x.shape), *leaves(0), *leaves(1), *leaves(2), *leaves(3))
```

```python
import functools

import numpy as np
import jax
import jax.numpy as jnp
from jax import lax
from jax.experimental import pallas as pl
from jax.experimental.pallas import tpu as pltpu

F32 = jnp.float32
BF16 = jnp.bfloat16
MESH = pl.DeviceIdType.MESH

N_DEV = 8
D_MODEL = 2048
N_HEADS = 8
HEAD = 128
Q_RANK = 768
KV_RANK = 512
ROPE = 64
D_IN = 6472
SHARD_IN = D_IN // N_DEV
NORM_EPS = 1e-6
ROPE_THETA = 10000.0
MLA_SCALE = (HEAD + ROPE) ** -0.5
FOX_SCALE = HEAD ** -0.5

QL0, KR0, FL0, KVL0, GM0, GF0, FQ0, FK0, FV0, NP_IN = 0, 768, 896, 1024, 1536, 2560, 3584, 4608, 5632, 6656
LAT_W = GM0
LANE = 128

ADAM_LR = 0.001
ADAM_B1 = 0.9
ADAM_B2 = 0.999
ADAM_EPS = 1e-08
ADAM_WD = 0.01
ADAM_STEP = 10

VMEM_LIMIT_BYTES = 56 * 1024 * 1024
SMALL_ROWS, SMALL_COLS = 8, 768


def _params(n_grid=0):
    return pltpu.CompilerParams(vmem_limit_bytes=VMEM_LIMIT_BYTES,
                                dimension_semantics=("arbitrary",) * n_grid if n_grid else None)


def _sigmoid(z):
    return 1.0 / (1.0 + jnp.exp(-z))


def _split3(v):
    a = v.astype(BF16)
    r = v - a.astype(F32)
    b = r.astype(BF16)
    c = (r - b.astype(F32)).astype(BF16)
    return a, b, c


def _mm(a, b, *, name, nt=False, ta=False, out_dtype=F32, tm=1024, tn=512, tk=2048):
    assert not (nt and ta)
    k_dim, m = a.shape if ta else a.shape[::-1]
    n = b.shape[0] if nt else b.shape[1]
    assert (b.shape[1] if nt else b.shape[0]) == k_dim
    tm, tn, tk = min(tm, m), min(tn, n), min(tk, k_dim)
    assert m % tm == 0 and n % tn == 0 and k_dim % tk == 0, (name, a.shape, b.shape)
    nk = k_dim // tk
    dims = (((0 if ta else 1,), (1 if nt else 0,)), ((), ()))

    def body(a_ref, b_ref, o_ref, acc_ref):
        k = pl.program_id(2)

        @pl.when(k == 0)
        def _():
            acc_ref[...] = jnp.zeros_like(acc_ref)

        acc_ref[...] += lax.dot_general(a_ref[...], b_ref[...], dims, preferred_element_type=F32)

        @pl.when(k == nk - 1)
        def _():
            o_ref[...] = acc_ref[...].astype(o_ref.dtype)

    b_spec = (pl.BlockSpec((tn, tk), lambda i, j, k: (j, k)) if nt
              else pl.BlockSpec((tk, tn), lambda i, j, k: (k, j)))
    a_spec = (pl.BlockSpec((tk, tm), lambda i, j, k: (k, i)) if ta
              else pl.BlockSpec((tm, tk), lambda i, j, k: (i, k)))
    return pl.pallas_call(
        body, name=name, grid=(m // tm, n // tn, nk),
        in_specs=[a_spec, b_spec],
        out_specs=pl.BlockSpec((tm, tn), lambda i, j, k: (i, j)),
        out_shape=jax.ShapeDtypeStruct((m, n), out_dtype),
        scratch_shapes=[pltpu.VMEM((tm, tn), F32)],
        compiler_params=_params(3),
    )(a, b)


def _prenorm(x, g, *, tm):
    t = x.shape[0]

    def body(x_ref, g_ref, h_ref):
        xv = x_ref[...]
        r = lax.rsqrt(jnp.mean(xv * xv, axis=-1, keepdims=True) + NORM_EPS)
        h_ref[...] = (xv * r * g_ref[...]).astype(BF16)

    return pl.pallas_call(
        body, name="prenorm", grid=(t // tm,),
        in_specs=[pl.BlockSpec((tm, D_MODEL), lambda i: (i, 0)), pl.BlockSpec((1, D_MODEL), lambda i: (0, 0))],
        out_specs=pl.BlockSpec((tm, D_MODEL), lambda i: (i, 0)),
        out_shape=jax.ShapeDtypeStruct((t, D_MODEL), BF16),
        compiler_params=_params(1),
    )(x, g)


def _rope_rows():
    inv = (np.float32(ROPE_THETA) ** (-np.arange(0, ROPE, 2, dtype=np.float32) / np.float32(ROPE))).astype(np.float32)
    invf = np.zeros((1, LANE), np.float32)
    sgn = np.zeros((1, LANE), np.float32)
    invf[0, 0:32] = inv
    invf[0, 64:96] = inv
    sgn[0, 0:32] = -1.0
    sgn[0, 64:96] = 1.0
    return jnp.asarray(invf), jnp.asarray(sgn)


def _rot(v, cos_t, sin_t):
    return v * cos_t + pltpu.roll(v, 64, 1) * sin_t


def _rot_bwd(dv, cos_t, sin_t):
    return dv * cos_t + pltpu.roll(dv * sin_t, 64, 1)


def _mid_fwd(proj, flog, g_q, g_kv, bf_row, pos_col, *, tm):
    t = proj.shape[0]
    invf, sgn = _rope_rows()

    def body(p_ref, fl_ref, gq_ref, gkv_ref, bf_ref, pos_ref, invf_ref, sgn_ref,
             qn_ref, kvn_ref, kr_ref, cos_ref, sin_ref, c_ref, carry_ref):
        i = pl.program_id(0)

        @pl.when(i == 0)
        def _():
            carry_ref[...] = jnp.zeros_like(carry_ref)

        ql = p_ref[:, QL0:QL0 + Q_RANK].astype(F32)
        r = lax.rsqrt(jnp.mean(ql * ql, axis=-1, keepdims=True) + NORM_EPS)
        qn_ref[...] = (ql * r * gq_ref[...]).astype(BF16)
        kvl = p_ref[:, KVL0:KVL0 + KV_RANK].astype(F32)
        r = lax.rsqrt(jnp.mean(kvl * kvl, axis=-1, keepdims=True) + NORM_EPS)
        kvn_ref[...] = (kvl * r * gkv_ref[...]).astype(BF16)

        ang = pos_ref[...] * invf_ref[...]
        cos_t = jnp.cos(ang)
        sin_t = jnp.sin(ang) * sgn_ref[...]
        cos_ref[...] = cos_t
        sin_ref[...] = sin_t
        kr_ref[...] = _rot(p_ref[:, KR0:KR0 + LANE].astype(F32), cos_t, sin_t).astype(BF16)

        z = fl_ref[...] + bf_ref[...]
        logf = jnp.minimum(z, 0.0) - jnp.log(1.0 + jnp.exp(-jnp.abs(z)))
        row = lax.broadcasted_iota(jnp.int32, (tm, tm), 0)
        col = lax.broadcasted_iota(jnp.int32, (tm, tm), 1)
        tri = (col <= row).astype(BF16)
        acc = carry_ref[0:1, :]
        for part in _split3(logf):
            acc = acc + jnp.dot(tri, part, preferred_element_type=F32)
        c_ref[...] = acc
        carry_ref[0:1, :] = carry_ref[0:1, :] + jnp.sum(logf, axis=0, keepdims=True)

    row_spec = lambda w: pl.BlockSpec((tm, w), lambda i: (i, 0))
    vec_spec = lambda w: pl.BlockSpec((1, w), lambda i: (0, 0))
    return pl.pallas_call(
        body, name="mid_fwd", grid=(t // tm,),
        in_specs=[row_spec(LAT_W), row_spec(LANE), vec_spec(Q_RANK), vec_spec(KV_RANK), vec_spec(LANE),
                  pl.BlockSpec((tm, 1), lambda i: (i, 0)), vec_spec(LANE), vec_spec(LANE)],
        out_specs=[row_spec(Q_RANK), row_spec(KV_RANK), row_spec(LANE), row_spec(LANE), row_spec(LANE), row_spec(LANE)],
        out_shape=[jax.ShapeDtypeStruct((t, Q_RANK), BF16), jax.ShapeDtypeStruct((t, KV_RANK), BF16),
                   jax.ShapeDtypeStruct((t, LANE), BF16), jax.ShapeDtypeStruct((t, LANE), F32),
                   jax.ShapeDtypeStruct((t, LANE), F32), jax.ShapeDtypeStruct((t, LANE), F32)],
        scratch_shapes=[pltpu.VMEM((8, LANE), F32)],
        compiler_params=_params(1),
    )(proj, flog, g_q, g_kv, bf_row, pos_col, invf, sgn)


def _rope_q(q_raw, cos_t, sin_t, *, tm):
    t = q_raw.shape[0]

    def body(q_ref, cos_ref, sin_ref, o_ref):
        c, s = cos_ref[...], sin_ref[...]
        for h in range(N_HEADS):
            o_ref[:, 256 * h:256 * h + 128] = q_ref[:, 256 * h:256 * h + 128].astype(BF16)
            o_ref[:, 256 * h + 128:256 * h + 256] = _rot(q_ref[:, 256 * h + 128:256 * h + 256], c, s).astype(BF16)

    return pl.pallas_call(
        body, name="rope_q", grid=(t // tm,),
        in_specs=[pl.BlockSpec((tm, 2048), lambda i: (i, 0)), pl.BlockSpec((tm, LANE), lambda i: (i, 0)),
                  pl.BlockSpec((tm, LANE), lambda i: (i, 0))],
        out_specs=pl.BlockSpec((tm, 2048), lambda i: (i, 0)),
        out_shape=jax.ShapeDtypeStruct((t, 2048), BF16),
        compiler_params=_params(1),
    )(q_raw, cos_t, sin_t)


def _attn_fwd(fox, operands, *, t, tb, name):
    nb = t // tb
    scale = FOX_SCALE if fox else MLA_SCALE
    head0 = N_HEADS if fox else 0
    nt_dims = (((1,), (1,)), ((), ()))
    tn_dims = (((0,), (0,)), ((), ()))

    def body(*refs):
        if fox:
            (q_ref, k_ref, v_ref, gate_ref, cq_ref, ck_ref, _, _,
             o_ref, og_ref, lse_ref, m_s, l_s, acc_s) = refs
        else:
            q_ref, kn_ref, kr_ref, v_ref, gate_ref, o_ref, og_ref, lse_ref, m_s, l_s, acc_s = refs
        qi = pl.program_id(1)
        q = q_ref[...]
        m_s[...] = jnp.full_like(m_s, -jnp.inf)
        l_s[...] = jnp.zeros_like(l_s)
        acc_s[...] = jnp.zeros_like(acc_s)

        def chunk(kc, masked):
            off = pl.multiple_of(kc * tb, tb)
            if fox:
                kk = k_ref[pl.ds(off, tb), :]
            else:
                kk = jnp.concatenate([kn_ref[pl.ds(off, tb), :], kr_ref[pl.ds(off, tb), :]], axis=1)
            s = lax.dot_general(kk, q, nt_dims, preferred_element_type=F32) * scale
            if fox:
                s = s + cq_ref[0, 0] - ck_ref[0, pl.ds(off, tb), :]
            if masked:
                row = lax.broadcasted_iota(jnp.int32, (tb, tb), 0)
                col = lax.broadcasted_iota(jnp.int32, (tb, tb), 1)
                s = jnp.where(row <= col, s, -jnp.inf)
            m_prev = m_s[...]
            m_new = jnp.maximum(m_prev, jnp.max(s, axis=0, keepdims=True))
            alpha = jnp.exp(m_prev - m_new)
            p = jnp.exp(s - m_new)
            l_s[...] = alpha * l_s[...] + jnp.sum(p, axis=0, keepdims=True)
            acc_s[...] = alpha * acc_s[...] + lax.dot_general(v_ref[pl.ds(off, tb), :], p.astype(BF16), tn_dims,
                                                              preferred_element_type=F32)
            m_s[...] = m_new

        def loop_body(kc, carry):
            chunk(kc, False)
            return carry

        lax.fori_loop(0, qi, loop_body, 0)
        chunk(qi, True)
        o = (acc_s[...] / l_s[...]).T
        o_ref[...] = o
        g = gate_ref[...].astype(F32)
        og_ref[...] = (o * (g * _sigmoid(g))).astype(BF16)
        lse_ref[0, 0] = m_s[...] + jnp.log(l_s[...])

    any_spec = pl.BlockSpec(memory_space=pl.ANY)
    row_stat = pl.BlockSpec((1, 1, 1, tb), lambda h, i: (h, i, 0, 0))
    if fox:
        proj, c_col, c_row4, o_all, og_all = operands
        ins = [proj, proj, proj, proj, c_row4, c_col, o_all, og_all]
        in_specs = [pl.BlockSpec((tb, HEAD), lambda h, i: (i, FQ0 // HEAD + h)),
                    pl.BlockSpec((t, HEAD), lambda h, i: (0, FK0 // HEAD + h)),
                    pl.BlockSpec((t, HEAD), lambda h, i: (0, FV0 // HEAD + h)),
                    pl.BlockSpec((tb, HEAD), lambda h, i: (i, GF0 // HEAD + h)),
                    row_stat, pl.BlockSpec((1, t, 1), lambda h, i: (h, 0, 0)), any_spec, any_spec]
        aliases = {6: 0, 7: 1}
    else:
        q_full, kv, kr, proj = operands
        ins = [q_full, kv, kr, kv, proj]
        in_specs = [pl.BlockSpec((tb, 256), lambda h, i: (i, h)),
                    pl.BlockSpec((t, HEAD), lambda h, i: (0, 2 * h)),
                    pl.BlockSpec((t, HEAD), lambda h, i: (0, 0)),
                    pl.BlockSpec((t, HEAD), lambda h, i: (0, 2 * h + 1)),
                    pl.BlockSpec((tb, HEAD), lambda h, i: (i, GM0 // HEAD + h))]
        aliases = {}
    return pl.pallas_call(
        body, name=name, grid=(N_HEADS, nb), in_specs=in_specs,
        out_specs=[pl.BlockSpec((tb, HEAD), lambda h, i: (i, head0 + h)),
                   pl.BlockSpec((tb, HEAD), lambda h, i: (i, head0 + h)), row_stat],
        out_shape=[jax.ShapeDtypeStruct((t, 2 * N_HEADS * HEAD), F32), jax.ShapeDtypeStruct((t, 2 * N_HEADS * HEAD), BF16),
                   jax.ShapeDtypeStruct((N_HEADS, nb, 1, tb), F32)],
        scratch_shapes=[pltpu.VMEM((1, tb), F32), pltpu.VMEM((1, tb), F32), pltpu.VMEM((HEAD, tb), F32)],
        input_output_aliases=aliases,
        compiler_params=_params(2),
    )(*ins)


def _postnorm_loss(o, x, target, g, *, tm):
    t = o.shape[0]

    def body(o_ref, x_ref, t_ref, g_ref, dy_ref, do_ref, dg_ref, loss_ref):
        i = pl.program_id(0)

        @pl.when(i == 0)
        def _():
            dg_ref[...] = jnp.zeros_like(dg_ref)
            loss_ref[...] = jnp.zeros_like(loss_ref)

        ov = o_ref[...]
        gv = g_ref[...]
        r = lax.rsqrt(jnp.mean(ov * ov, axis=-1, keepdims=True) + NORM_EPS)
        oh = ov * r
        e = x_ref[...] + oh * gv - t_ref[...]
        loss_ref[...] += 0.5 * jnp.sum(jnp.mean(e * e, axis=-1, keepdims=True), axis=0, keepdims=True)
        dy = e * (1.0 / D_MODEL)
        dy_ref[...] = dy
        dyg = dy * gv
        do_ref[...] = (r * (dyg - oh * jnp.mean(dyg * oh, axis=-1, keepdims=True))).astype(BF16)
        dg_ref[...] += jnp.sum(dy * oh, axis=0, keepdims=True)

    row = pl.BlockSpec((tm, D_MODEL), lambda i: (i, 0))
    vec = pl.BlockSpec((1, D_MODEL), lambda i: (0, 0))
    return pl.pallas_call(
        body, name="postnorm_loss", grid=(t // tm,),
        in_specs=[row, row, row, vec],
        out_specs=[row, row, vec, pl.BlockSpec((1, 1), lambda i: (0, 0))],
        out_shape=[jax.ShapeDtypeStruct((t, D_MODEL), F32), jax.ShapeDtypeStruct((t, D_MODEL), BF16),
                   jax.ShapeDtypeStruct((1, D_MODEL), F32), jax.ShapeDtypeStruct((1, 1), F32)],
        compiler_params=_params(1),
    )(o, x, target, g)


def _dog_gate(d_o_post, w_out_n, o_all, proj, *, tm):
    t = d_o_post.shape[0]
    pair = 2 * HEAD
    gate_blk = GM0 // pair
    assert GM0 % pair == 0 and GF0 == GM0 + N_HEADS * HEAD

    def body(do_ref, w_ref, o_ref, p_ref, dattn_ref, dproj_ref, delta_ref):
        j = pl.program_id(1)

        @pl.when(j == 0)
        def _():
            delta_ref[...] = jnp.zeros_like(delta_ref)

        dog = lax.dot_general(do_ref[...], w_ref[...], (((1,), (1,)), ((), ())), preferred_element_type=F32)
        g = p_ref[...].astype(F32)
        ov = o_ref[...]
        sg = _sigmoid(g)
        d_o = dog * (g * sg)
        dattn_ref[...] = d_o.astype(BF16)
        dproj_ref[...] = (dog * ov * (sg * (1.0 + g * (1.0 - sg)))).astype(BF16)
        prod = d_o * ov
        lane = lax.broadcasted_iota(jnp.int32, (tm, LANE), 1)
        delta = delta_ref[...]
        for half in range(2):
            part = jnp.sum(prod[:, HEAD * half:HEAD * (half + 1)], axis=-1, keepdims=True)
            delta = jnp.where(lane == 2 * j + half, part, delta)
        delta_ref[...] = delta

    return pl.pallas_call(
        body, name="dog_gate", grid=(t // tm, N_HEADS),
        in_specs=[pl.BlockSpec((tm, D_MODEL), lambda i, j: (i, 0)), pl.BlockSpec((pair, D_MODEL), lambda i, j: (j, 0)),
                  pl.BlockSpec((tm, pair), lambda i, j: (i, j)), pl.BlockSpec((tm, pair), lambda i, j: (i, gate_blk + j))],
        out_specs=[pl.BlockSpec((tm, pair), lambda i, j: (i, j)), pl.BlockSpec((tm, pair), lambda i, j: (i, gate_blk + j)),
                   pl.BlockSpec((tm, LANE), lambda i, j: (i, 0))],
        out_shape=[jax.ShapeDtypeStruct((t, 2048), BF16), jax.ShapeDtypeStruct((t, NP_IN), BF16),
                   jax.ShapeDtypeStruct((t, LANE), F32)],
        compiler_params=_params(2),
    )(d_o_post, w_out_n, o_all, proj)


def _attn_bwd(fox, operands, *, t, tb, name):
    nb = t // tb
    scale = FOX_SCALE if fox else MLA_SCALE
    dq_w = HEAD if fox else 256
    nt_dims = (((1,), (1,)), ((), ()))
    tn_dims = (((0,), (0,)), ((), ()))

    def body(*refs):
        if fox:
            (q_ref, k_ref, v_ref, do_ref, lse_ref, dl_ref, cq_ref, ck_ref, _,
             dproj_ref, dck_ref, dcq_ref, dq_s, dk_s, dv_s, dc_s, dcq_s, stage_q, stage_k, stage_v, put_sems) = refs
        else:
            (q_ref, kn_ref, kr_ref, v_ref, do_ref, lse_ref, dl_ref,
             dq_ref, dkv_ref, dkr_ref, dq_s, dk_s, dv_s) = refs
        head = pl.program_id(0)
        ki = pl.program_id(1)

        @pl.when(ki == 0)
        def _():
            dq_s[...] = jnp.zeros_like(dq_s)
            if fox:
                dcq_s[...] = jnp.zeros_like(dcq_s)

        dk_s[...] = jnp.zeros_like(dk_s)
        dv_s[...] = jnp.zeros_like(dv_s)
        if fox:
            dc_s[...] = jnp.zeros_like(dc_s)
            kk = k_ref[...]
        else:
            kk = jnp.concatenate([kn_ref[...], kr_ref[...]], axis=1)
        vv = v_ref[...]

        def chunk(qc, masked):
            off = pl.multiple_of(qc * tb, tb)
            qq = q_ref[pl.ds(off, tb), :]
            dd = do_ref[pl.ds(off, tb), :]
            s = lax.dot_general(kk, qq, nt_dims, preferred_element_type=F32) * scale
            if fox:
                s = s + cq_ref[0, qc] - ck_ref[0]
            if masked:
                row = lax.broadcasted_iota(jnp.int32, (tb, tb), 0)
                col = lax.broadcasted_iota(jnp.int32, (tb, tb), 1)
                s = jnp.where(row <= col, s, -jnp.inf)
            p = jnp.exp(s - lse_ref[0, qc])
            dv_s[...] += jnp.dot(p.astype(BF16), dd, preferred_element_type=F32)
            dp = lax.dot_general(vv, dd, nt_dims, preferred_element_type=F32)
            ds = p * (dp - dl_ref[0, qc])
            if fox:
                dc_s[...] += jnp.sum(ds, axis=1, keepdims=True)
                dcq_s[qc] += jnp.sum(ds, axis=0, keepdims=True)
            dsb = (ds * scale).astype(BF16)
            dk_s[...] += jnp.dot(dsb, qq, preferred_element_type=F32)
            dq_s[pl.ds(off, tb), :] += lax.dot_general(dsb, kk, tn_dims, preferred_element_type=F32)

        chunk(ki, True)

        def loop_body(qc, carry):
            chunk(qc, False)
            return carry

        lax.fori_loop(ki + 1, nb, loop_body, 0)

        def put(stage_ref, value, rows, seg0, sem):
            stage_ref[...] = value.astype(BF16)
            col0 = pl.multiple_of(seg0 + head * HEAD, HEAD)
            cp = pltpu.make_async_copy(stage_ref, dproj_ref.at[rows, pl.ds(col0, HEAD)], sem)
            cp.start()
            cp.wait()

        if fox:
            rows = pl.ds(pl.multiple_of(ki * tb, tb), tb)
            put(stage_k, dk_s[...], rows, FK0, put_sems.at[1])
            put(stage_v, dv_s[...], rows, FV0, put_sems.at[2])
            dck_ref[0] = -dc_s[...]
        else:
            dkv_ref[...] = jnp.concatenate([dk_s[:, :HEAD], dv_s[...]], axis=1).astype(BF16)
            dkr_ref[...] = dk_s[:, HEAD:]

        @pl.when(ki == nb - 1)
        def _():
            if fox:
                put(stage_q, dq_s[...], pl.ds(0, t), FQ0, put_sems.at[0])
                dcq_ref[0] = dcq_s[...]
            else:
                dq_ref[...] = dq_s[...]

    stat = pl.BlockSpec((1, nb, 1, tb), lambda h, i: (h, 0, 0, 0))
    aliases = {}
    if fox:
        proj, d_o, lse4, delta4, c_row4, c_col, dproj = operands
        ins = [proj, proj, proj, d_o, lse4, delta4, c_row4, c_col, dproj]
        any_spec = pl.BlockSpec(memory_space=pl.ANY)
        in_specs = [pl.BlockSpec((t, HEAD), lambda h, i: (0, FQ0 // HEAD + h)),
                    pl.BlockSpec((tb, HEAD), lambda h, i: (i, FK0 // HEAD + h)),
                    pl.BlockSpec((tb, HEAD), lambda h, i: (i, FV0 // HEAD + h)),
                    pl.BlockSpec((t, HEAD), lambda h, i: (0, N_HEADS + h)),
                    stat, stat, stat, pl.BlockSpec((1, tb, 1), lambda h, i: (h, i, 0)), any_spec]
        aliases = {8: 0}
        out_specs = [any_spec, pl.BlockSpec((1, tb, 1), lambda h, i: (h, i, 0)), stat]
        out_shape = [jax.ShapeDtypeStruct(dproj.shape, dproj.dtype), jax.ShapeDtypeStruct((N_HEADS, t, 1), F32),
                     jax.ShapeDtypeStruct((N_HEADS, nb, 1, tb), F32)]
        scratch = [pltpu.VMEM((t, HEAD), F32), pltpu.VMEM((tb, HEAD), F32), pltpu.VMEM((tb, HEAD), F32),
                   pltpu.VMEM((tb, 1), F32), pltpu.VMEM((nb, 1, tb), F32),
                   pltpu.VMEM((t, HEAD), BF16), pltpu.VMEM((tb, HEAD), BF16), pltpu.VMEM((tb, HEAD), BF16),
                   pltpu.SemaphoreType.DMA((3,))]
    else:
        q_full, kv, kr, d_o, lse4, delta4 = operands
        ins = [q_full, kv, kr, kv, d_o, lse4, delta4]
        in_specs = [pl.BlockSpec((t, 256), lambda h, i: (0, h)),
                    pl.BlockSpec((tb, HEAD), lambda h, i: (i, 2 * h)),
                    pl.BlockSpec((tb, HEAD), lambda h, i: (i, 0)),
                    pl.BlockSpec((tb, HEAD), lambda h, i: (i, 2 * h + 1)),
                    pl.BlockSpec((t, HEAD), lambda h, i: (0, h)),
                    stat, stat]
        out_specs = [pl.BlockSpec((t, 256), lambda h, i: (0, h)), pl.BlockSpec((tb, 256), lambda h, i: (i, h)),
                     pl.BlockSpec((tb, HEAD), lambda h, i: (i, h))]
        out_shape = [jax.ShapeDtypeStruct((t, 2048), F32), jax.ShapeDtypeStruct((t, 2048), BF16),
                     jax.ShapeDtypeStruct((t, 1024), F32)]
        scratch = [pltpu.VMEM((t, 256), F32), pltpu.VMEM((tb, 256), F32), pltpu.VMEM((tb, HEAD), F32)]
    return pl.pallas_call(
        body, name=name, grid=(N_HEADS, nb), in_specs=in_specs, out_specs=out_specs, out_shape=out_shape,
        scratch_shapes=scratch, input_output_aliases=aliases, compiler_params=_params(2),
    )(*ins)


def _mid_bwd(dq_full, dkr, cos_t, sin_t, dck, flog, bf_row, *, tm):
    t = dq_full.shape[0]
    n = t // tm

    def body(dq_ref, dkr_ref, cos_ref, sin_ref, dck_ref, fl_ref, bf_ref,
             dq2_ref, dkraw_ref, dfl_ref, dbf_ref, carry_ref):
        i = pl.program_id(0)

        @pl.when(i == 0)
        def _():
            carry_ref[...] = jnp.zeros_like(carry_ref)
            dbf_ref[...] = jnp.zeros_like(dbf_ref)

        c, s = cos_ref[...], sin_ref[...]
        dkr_sum = jnp.zeros((tm, LANE), F32)
        for h in range(N_HEADS):
            dq2_ref[:, 256 * h:256 * h + 128] = dq_ref[:, 256 * h:256 * h + 128].astype(BF16)
            dq2_ref[:, 256 * h + 128:256 * h + 256] = _rot_bwd(dq_ref[:, 256 * h + 128:256 * h + 256], c, s).astype(BF16)
            dkr_sum = dkr_sum + dkr_ref[:, HEAD * h:HEAD * (h + 1)]
        dkraw_ref[...] = _rot_bwd(dkr_sum, c, s).astype(BF16)

        dc = dck_ref[...]
        row = lax.broadcasted_iota(jnp.int32, (tm, tm), 0)
        col = lax.broadcasted_iota(jnp.int32, (tm, tm), 1)
        tri = (col >= row).astype(BF16)
        acc = carry_ref[0:1, :]
        for part in _split3(dc):
            acc = acc + jnp.dot(tri, part, preferred_element_type=F32)
        carry_ref[0:1, :] = carry_ref[0:1, :] + jnp.sum(dc, axis=0, keepdims=True)
        z = fl_ref[...] + bf_ref[...]
        dz = acc / (1.0 + jnp.exp(z))
        dfl_ref[...] = dz.astype(BF16)
        dbf_ref[...] += jnp.sum(dz, axis=0, keepdims=True)

    rev = lambda w: pl.BlockSpec((tm, w), lambda i: (n - 1 - i, 0))
    vec = lambda w: pl.BlockSpec((1, w), lambda i: (0, 0))
    return pl.pallas_call(
        body, name="mid_bwd", grid=(n,),
        in_specs=[rev(2048), rev(1024), rev(LANE), rev(LANE), rev(LANE), rev(LANE), vec(LANE)],
        out_specs=[rev(2048), rev(LANE), rev(LANE), vec(LANE)],
        out_shape=[jax.ShapeDtypeStruct((t, 2048), BF16), jax.ShapeDtypeStruct((t, LANE), BF16),
                   jax.ShapeDtypeStruct((t, LANE), BF16), jax.ShapeDtypeStruct((1, LANE), F32)],
        scratch_shapes=[pltpu.VMEM((8, LANE), F32)],
        compiler_params=_params(1),
    )(dq_full, dkr, cos_t, sin_t, dck, flog, bf_row)


def _norm_bwd(proj, dqn, dkvn, g_q, g_kv, dkr_raw, dfl, dproj, *, tm):
    t = proj.shape[0]
    assert (KR0, FL0, KVL0, LAT_W) == (Q_RANK, Q_RANK + LANE, Q_RANK + 2 * LANE, Q_RANK + 2 * LANE + KV_RANK)

    def body(p_ref, dqn_ref, dkvn_ref, gq_ref, gkv_ref, dkr_ref, dfl_ref, _, dproj_ref, dgq_ref, dgkv_ref):
        i = pl.program_id(0)

        @pl.when(i == 0)
        def _():
            dgq_ref[...] = jnp.zeros_like(dgq_ref)
            dgkv_ref[...] = jnp.zeros_like(dgkv_ref)

        d_lat = []
        for lo, w, dn_ref, g_ref, dg_ref in ((QL0, Q_RANK, dqn_ref, gq_ref, dgq_ref),
                                             (KVL0, KV_RANK, dkvn_ref, gkv_ref, dgkv_ref)):
            xv = p_ref[:, lo:lo + w].astype(F32)
            r = lax.rsqrt(jnp.mean(xv * xv, axis=-1, keepdims=True) + NORM_EPS)
            xh = xv * r
            dn = dn_ref[...]
            dg_ref[...] += jnp.sum(dn * xh, axis=0, keepdims=True)
            dxh = dn * g_ref[...]
            d_lat.append((r * (dxh - xh * jnp.mean(dxh * xh, axis=-1, keepdims=True))).astype(BF16))
        dproj_ref[...] = jnp.concatenate([d_lat[0], dkr_ref[...], dfl_ref[...], d_lat[1]], axis=1)

    row = lambda w: pl.BlockSpec((tm, w), lambda i: (i, 0))
    vec = lambda w: pl.BlockSpec((1, w), lambda i: (0, 0))
    return pl.pallas_call(
        body, name="norm_bwd", grid=(t // tm,),
        in_specs=[row(LAT_W), row(Q_RANK), row(KV_RANK), vec(Q_RANK), vec(KV_RANK), row(LANE), row(LANE),
                  pl.BlockSpec(memory_space=pl.ANY)],
        out_specs=[row(LAT_W), vec(Q_RANK), vec(KV_RANK)],
        out_shape=[jax.ShapeDtypeStruct(dproj.shape, dproj.dtype),
                   jax.ShapeDtypeStruct((1, Q_RANK), F32), jax.ShapeDtypeStruct((1, KV_RANK), F32)],
        input_output_aliases={7: 0},
        compiler_params=_params(1),
    )(proj, dqn, dkvn, g_q, g_kv, dkr_raw, dfl, dproj)


def _prenorm_bwd(dh, x, g, dy, *, tm):
    t = x.shape[0]

    def body(dh_ref, x_ref, g_ref, dy_ref, gx_ref, dg_ref):
        i = pl.program_id(0)

        @pl.when(i == 0)
        def _():
            dg_ref[...] = jnp.zeros_like(dg_ref)

        xv = x_ref[...]
        r = lax.rsqrt(jnp.mean(xv * xv, axis=-1, keepdims=True) + NORM_EPS)
        xh = xv * r
        dn = dh_ref[...]
        dg_ref[...] += jnp.sum(dn * xh, axis=0, keepdims=True)
        dxh = dn * g_ref[...]
        gx_ref[...] = dy_ref[...] + r * (dxh - xh * jnp.mean(dxh * xh, axis=-1, keepdims=True))

    row = pl.BlockSpec((tm, D_MODEL), lambda i: (i, 0))
    vec = pl.BlockSpec((1, D_MODEL), lambda i: (0, 0))
    return pl.pallas_call(
        body, name="prenorm_bwd", grid=(t // tm,),
        in_specs=[row, row, vec, row], out_specs=[row, vec],
        out_shape=[jax.ShapeDtypeStruct((t, D_MODEL), F32), jax.ShapeDtypeStruct((1, D_MODEL), F32)],
        compiler_params=_params(1),
    )(dh, x, g, dy)


N_CHIPS = 4


def _pair_sum(parts, core, *, tr, name):
    mine, other = parts
    _, _, rows, cols = mine.shape
    tr = min(tr, rows)

    def body(core_ref, a_ref, b_ref, o_ref):
        o_ref[...] = (a_ref[0].astype(F32) + b_ref[...].astype(F32)).astype(BF16)

    return pl.pallas_call(
        body, name=name,
        grid_spec=pltpu.PrefetchScalarGridSpec(
            num_scalar_prefetch=1, grid=(N_CHIPS, rows // tr),
            in_specs=[pl.BlockSpec((1, 1, tr, cols), lambda j, i, core_ref: (core_ref[0], j, i, 0)),
                      pl.BlockSpec((1, tr, cols), lambda j, i, core_ref: (j, i, 0))],
            out_specs=pl.BlockSpec((1, tr, cols), lambda j, i, core_ref: (j, i, 0))),
        out_shape=jax.ShapeDtypeStruct(other.shape, BF16),
        compiler_params=_params(2),
    )(core, mine, other)
def _adam_math(w, g, m, v):
    m = ADAM_B1 * m + (1.0 - ADAM_B1) * g
    v = ADAM_B2 * v + (1.0 - ADAM_B2) * (g * g)
    m_hat = m / (1.0 - ADAM_B1 ** ADAM_STEP)
    v_hat = v / (1.0 - ADAM_B2 ** ADAM_STEP)
    delta = -ADAM_LR * (m_hat / (jnp.sqrt(v_hat) + ADAM_EPS) + ADAM_WD * w)
    return delta, m, v


def _adamw(land, w, m, v, *, tr, name):
    rows, cols = w.shape

    def body(l_ref, w_ref, m_ref, v_ref, g_ref, d_ref, nm_ref, nv_ref):
        g = l_ref[0].astype(F32)
        for s in range(1, N_CHIPS):
            g = g + l_ref[s].astype(F32)
        g_ref[...] = g
        d_ref[...], nm_ref[...], nv_ref[...] = _adam_math(w_ref[...], g, m_ref[...], v_ref[...])

    blk = pl.BlockSpec((tr, cols), lambda i: (i, 0))
    return pl.pallas_call(
        body, name=name, grid=(rows // tr,),
        in_specs=[pl.BlockSpec((N_CHIPS, tr, cols), lambda i: (0, i, 0)), blk, blk, blk],
        out_specs=[blk, blk, blk, blk],
        out_shape=[jax.ShapeDtypeStruct((rows, cols), F32)] * 4,
        compiler_params=_params(1),
    )(land, w, m, v)


def _adamw_small(gathered, w, m, v):
    def body(a_ref, w_ref, m_ref, v_ref, g_ref, d_ref, nm_ref, nv_ref):
        g = a_ref[0:SMALL_ROWS, :]
        for s in range(1, N_DEV):
            g = g + a_ref[SMALL_ROWS * s:SMALL_ROWS * (s + 1), :]
        g_ref[...] = g
        d_ref[...], nm_ref[...], nv_ref[...] = _adam_math(w_ref[...], g, m_ref[...], v_ref[...])

    return pl.pallas_call(
        body, name="adamw_small",
        out_shape=[jax.ShapeDtypeStruct((SMALL_ROWS, SMALL_COLS), F32)] * 4,
        compiler_params=_params(),
    )(gathered, w, m, v)


def _place():
    x, y, c = lax.axis_index("x"), lax.axis_index("y"), lax.axis_index("c")
    return x, y, c


def _flip(p, k):
    x, y, c = p
    return (1 - x if k & 4 else x, 1 - y if k & 2 else y, 1 - c if k & 1 else c)


def _index(p):
    return 4 * p[0] + 2 * p[1] + p[2]


def _all_gather(shards):
    n = len(shards)
    hbm = pl.BlockSpec(memory_space=pl.ANY)

    def body(*refs):
        ins, outs = refs[:n], refs[n:2 * n]
        send_sems, recv_sems, local_sems = refs[2 * n:]
        me = _place()
        sibling = _flip(me, 1)
        chips = [_flip(me, 4), _flip(me, 2), _flip(me, 6)]

        def copy(a, k, block, to, src=None):
            dst = outs[a].at[_index(block)]
            return pltpu.make_async_remote_copy(
                src_ref=dst if src is None else src, dst_ref=dst,
                send_sem=send_sems.at[7 * a + k], recv_sem=recv_sems.at[7 * a + k],
                device_id=to, device_id_type=MESH)

        started = []
        for a in range(n):
            mine = pltpu.make_async_copy(ins[a], outs[a].at[_index(me)], local_sems.at[a])
            mine.start()
            started.append(mine)
        first = []
        for a in range(n):
            first.append(copy(a, 0, me, sibling, src=ins[a]))
            first += [copy(a, 1 + j, me, chip, src=ins[a]) for j, chip in enumerate(chips)]
        for cp in first:
            cp.start()
        passed = []
        for a in range(n):
            for j, chip in enumerate(chips):
                copy(a, 1 + j, chip, me).wait_recv()
                fwd = copy(a, 4 + j, chip, sibling)
                fwd.start()
                passed.append(fwd)
        for a in range(n):
            copy(a, 0, sibling, me).wait_recv()
            for j, chip in enumerate(chips):
                copy(a, 4 + j, _flip(chip, 1), me).wait_recv()
        for cp in first + passed:
            cp.wait_send()
        for mine in started:
            mine.wait()

    return pl.pallas_call(
        body, name="all_gather_weights",
        in_specs=[hbm] * n, out_specs=[hbm] * n,
        out_shape=[jax.ShapeDtypeStruct((N_DEV,) + s.shape, s.dtype) for s in shards],
        scratch_shapes=[pltpu.SemaphoreType.DMA((7 * n,)), pltpu.SemaphoreType.DMA((7 * n,)),
                        pltpu.SemaphoreType.DMA((n,))],
    )(*shards)


def _exchange_cores(parts):
    n = len(parts)
    hbm = pl.BlockSpec(memory_space=pl.ANY)

    def body(*refs):
        srcs, lands = refs[:n], refs[n:2 * n]
        send_sems, recv_sems = refs[2 * n:]
        me = _place()
        sibling = _flip(me, 1)
        copies = []
        for a in range(n):
            cp = pltpu.make_async_remote_copy(
                src_ref=srcs[a].at[1 - me[2]], dst_ref=lands[a], send_sem=send_sems.at[a], recv_sem=recv_sems.at[a],
                device_id=sibling, device_id_type=MESH)
            cp.start()
            copies.append(cp)
        for cp in copies:
            cp.wait()

    return pl.pallas_call(
        body, name="exchange_cores",
        in_specs=[hbm] * n, out_specs=[hbm] * n,
        out_shape=[jax.ShapeDtypeStruct(p.shape[1:], p.dtype) for p in parts],
        scratch_shapes=[pltpu.SemaphoreType.DMA((n,)), pltpu.SemaphoreType.DMA((n,))],
    )(*parts)


def _exchange_chips(sums):
    n = len(sums)
    hbm = pl.BlockSpec(memory_space=pl.ANY)
    other_chips = (4, 2, 6)

    def body(*refs):
        srcs, lands = refs[:n], refs[n:2 * n]
        send_sems, recv_sems, local_sems = refs[2 * n:]
        me = _place()
        chip = lambda p: 2 * p[0] + p[1]
        started, sends = [], []
        for a in range(n):
            mine = pltpu.make_async_copy(srcs[a].at[chip(me)], lands[a].at[chip(me)], local_sems.at[a])
            mine.start()
            started.append(mine)
        for j, k in enumerate(other_chips):
            peer = _flip(me, k)
            for a in range(n):
                cp = pltpu.make_async_remote_copy(
                    src_ref=srcs[a].at[chip(peer)], dst_ref=lands[a].at[chip(me)],
                    send_sem=send_sems.at[3 * a + j], recv_sem=recv_sems.at[3 * a + j],
                    device_id=peer, device_id_type=MESH)
                cp.start()
                sends.append(cp)
        for j, k in enumerate(other_chips):
            peer = _flip(me, k)
            for a in range(n):
                slot = lands[a].at[chip(peer)]
                pltpu.make_async_remote_copy(
                    src_ref=slot, dst_ref=slot, send_sem=send_sems.at[3 * a + j], recv_sem=recv_sems.at[3 * a + j],
                    device_id=peer, device_id_type=MESH).wait_recv()
        for cp in sends:
            cp.wait_send()
        for mine in started:
            mine.wait()

    return pl.pallas_call(
        body, name="exchange_chips",
        in_specs=[hbm] * n, out_specs=[hbm] * n,
        out_shape=[jax.ShapeDtypeStruct(s.shape, s.dtype) for s in sums],
        scratch_shapes=[pltpu.SemaphoreType.DMA((3 * n,)), pltpu.SemaphoreType.DMA((3 * n,)),
                        pltpu.SemaphoreType.DMA((n,))],
    )(*sums)


def _gather_small(vec):
    def body(v_ref, out_ref, send_sems, recv_sems, local_sem):
        me = _place()

        def rows(p):
            return out_ref.at[pl.ds(pl.multiple_of(_index(p) * SMALL_ROWS, SMALL_ROWS), SMALL_ROWS), :]

        mine = pltpu.make_async_copy(v_ref, rows(me), local_sem)
        mine.start()
        sends = []
        for k in range(1, N_DEV):
            peer = _flip(me, k)
            cp = pltpu.make_async_remote_copy(src_ref=v_ref, dst_ref=rows(me), send_sem=send_sems.at[k - 1],
                                              recv_sem=recv_sems.at[k - 1], device_id=peer, device_id_type=MESH)
            cp.start()
            sends.append(cp)
        for k in range(1, N_DEV):
            peer = _flip(me, k)
            pltpu.make_async_remote_copy(src_ref=rows(peer), dst_ref=rows(peer), send_sem=send_sems.at[k - 1],
                                         recv_sem=recv_sems.at[k - 1], device_id=peer, device_id_type=MESH).wait_recv()
        for cp in sends:
            cp.wait_send()
        mine.wait()

    return pl.pallas_call(
        body, name="gather_small",
        in_specs=[pl.BlockSpec(memory_space=pltpu.VMEM)], out_specs=pl.BlockSpec(memory_space=pltpu.VMEM),
        out_shape=jax.ShapeDtypeStruct((N_DEV * SMALL_ROWS, SMALL_COLS), F32),
        scratch_shapes=[pltpu.SemaphoreType.DMA((7,)), pltpu.SemaphoreType.DMA((7,)), pltpu.SemaphoreType.DMA],
    )(vec)


def _w_in_nice(gathered):
    w = gathered.transpose(1, 0, 2).reshape(D_MODEL, D_IN)
    z = lambda n: jnp.zeros((D_MODEL, n), w.dtype)
    return jnp.concatenate([w[:, :768], w[:, 1280:1312], z(32), w[:, 1312:1344], z(32), w[:, 5440:5448],
                            z(LANE - N_HEADS), w[:, 768:1280], w[:, 1344:2368], w[:, 5448:6472], w[:, 2368:5440]], axis=1)


def _w_in_shards(dw):
    w = jnp.concatenate([dw[:, :Q_RANK], dw[:, KVL0:KVL0 + KV_RANK], dw[:, KR0:KR0 + 32], dw[:, KR0 + 64:KR0 + 96],
                         dw[:, GM0:GM0 + 1024], dw[:, FQ0:FQ0 + 3072], dw[:, FL0:FL0 + N_HEADS],
                         dw[:, GF0:GF0 + 1024]], axis=1)
    return w.reshape(D_MODEL, N_CHIPS, 2, SHARD_IN).transpose(2, 1, 0, 3)


def _by_core(shards):
    return shards.reshape((N_CHIPS, 2) + shards.shape[1:]).swapaxes(0, 1)


def _w_uq_nice(shard):
    z = jnp.zeros((Q_RANK, 32), shard.dtype)
    return jnp.concatenate([shard[:, :128], shard[:, 128:160], z, shard[:, 160:192], z], axis=1)


def _pack_small(g_pre, g_post, g_q, g_kv, b_f, extra=None):
    parts = [g_pre.reshape(-1), g_post.reshape(-1), g_q.reshape(-1), g_kv.reshape(-1), b_f.reshape(-1)]
    if extra is not None:
        parts.append(extra.reshape(-1))
    flat = jnp.concatenate(parts)
    flat = jnp.concatenate([flat, jnp.zeros((SMALL_ROWS * SMALL_COLS - flat.shape[0],), F32)])
    return flat.reshape(SMALL_ROWS, SMALL_COLS)


def _unpack_small(packed):
    flat = packed.reshape(-1)
    o = 0
    out = []
    for n in (D_MODEL, D_MODEL, Q_RANK, KV_RANK, N_HEADS):
        out.append(flat[o:o + n].reshape(1, n))
        o += n
    return out, flat[o]


def kernel(x, positions, g_pre, w_in, g_q_latent, w_uq, g_kv_latent, w_ukv, b_forget, w_out, g_post, loss_target, m_g_pre, m_w_in, m_g_q_latent, m_w_uq, m_g_kv_latent, m_w_ukv, m_b_forget, m_w_out, m_g_post, v_g_pre, v_w_in, v_g_q_latent, v_w_uq, v_g_kv_latent, v_w_ukv, v_b_forget, v_w_out, v_g_post):
    t = x.shape[1]
    tb = min(512, t)
    tm = min(256, t)
    nb = t // tb
    x2 = x.reshape(t, D_MODEL)
    target = loss_target.reshape(t, D_MODEL)
    pos_col = positions.reshape(t, 1).astype(F32)
    bf_row = jnp.concatenate([b_forget.reshape(1, N_HEADS), jnp.zeros((1, LANE - N_HEADS), F32)], axis=1)

    g_in, g_uq, g_ukv, g_out = _all_gather([
        w_in[0].astype(BF16), _w_uq_nice(w_uq[0].astype(BF16)), w_ukv[0].astype(BF16), w_out[0].astype(BF16)])
    w_in_n = _w_in_nice(g_in)
    w_uq_n = g_uq.transpose(1, 0, 2).reshape(Q_RANK, N_HEADS * 256)
    w_ukv_n = g_ukv.transpose(1, 0, 2).reshape(KV_RANK, N_HEADS * 256)
    w_out_n = g_out.reshape(D_MODEL, D_MODEL)

    h = _prenorm(x2, g_pre, tm=tm)
    proj = _mm(h, w_in_n, name="proj_in", out_dtype=BF16, tm=1024, tn=512, tk=2048)
    flog = _mm(h, w_in_n[:, FL0:FL0 + LANE], name="proj_flog", out_dtype=F32, tm=1024, tn=LANE, tk=2048)
    qn, kvn, kr, cos_t, sin_t, c = _mid_fwd(proj, flog, g_q_latent, g_kv_latent, bf_row, pos_col, tm=tm)
    q_raw = _mm(qn, w_uq_n, name="q_up", out_dtype=F32, tm=1024, tn=512, tk=Q_RANK)
    q_full = _rope_q(q_raw, cos_t, sin_t, tm=tm)
    kv = _mm(kvn, w_ukv_n, name="kv_up", out_dtype=BF16, tm=1024, tn=512, tk=KV_RANK)
    c_heads = c[:, :N_HEADS].T
    c_col = c_heads.reshape(N_HEADS, t, 1)
    c_row4 = c_heads.reshape(N_HEADS, nb, 1, tb)
    o_all, og_all, lse4_mla = _attn_fwd(False, (q_full, kv, kr, proj), t=t, tb=tb, name="mla_fwd")
    o_all, og_all, lse4_fox = _attn_fwd(True, (proj, c_col, c_row4, o_all, og_all), t=t, tb=tb, name="fox_fwd")
    o = _mm(og_all, w_out_n, name="out_proj", out_dtype=F32, tm=1024, tn=512, tk=2048)
    dy, d_o_post, dg_post, loss_part = _postnorm_loss(o, x2, target, g_post, tm=tm)

    dw_out = _mm(og_all, d_o_post, name="dw_out", ta=True, out_dtype=BF16, tm=1024, tn=1024, tk=512)
    d_attn, dproj, delta = _dog_gate(d_o_post, w_out_n, o_all, proj, tm=min(1024, t))
    delta4 = delta[:, :2 * N_HEADS].T.reshape(2 * N_HEADS, nb, 1, tb)
    dq_full, dkv, dkr = _attn_bwd(False, (q_full, kv, kr, d_attn, lse4_mla, delta4[:N_HEADS]),
                                  t=t, tb=tb, name="mla_bwd")
    dproj, dck, dcq = _attn_bwd(True, (proj, d_attn, lse4_fox, delta4[N_HEADS:], c_row4, c_col, dproj),
                                t=t, tb=tb, name="fox_bwd")
    dc_heads = dck.reshape(N_HEADS, t) + dcq.reshape(N_HEADS, t)
    dck_rows = jnp.concatenate([dc_heads.T, jnp.zeros((t, LANE - N_HEADS), F32)], axis=1)
    dq2, dkr_raw, dfl, dbf = _mid_bwd(dq_full, dkr, cos_t, sin_t, dck_rows, flog, bf_row, tm=tm)
    dqn = _mm(dq2, w_uq_n, name="d_qn", nt=True, out_dtype=F32, tm=1024, tn=Q_RANK, tk=2048)
    dkvn = _mm(dkv, w_ukv_n, name="d_kvn", nt=True, out_dtype=F32, tm=1024, tn=KV_RANK, tk=2048)
    dw_uq = _mm(qn, dq2, name="dw_uq", ta=True, out_dtype=BF16, tm=Q_RANK, tn=1024, tk=512)
    dw_ukv = _mm(kvn, dkv, name="dw_ukv", ta=True, out_dtype=BF16, tm=KV_RANK, tn=1024, tk=512)
    dproj, dg_q, dg_kv = _norm_bwd(proj, dqn, dkvn, g_q_latent, g_kv_latent, dkr_raw, dfl, dproj, tm=tm)
    dh = _mm(dproj, w_in_n, name="d_h", nt=True, out_dtype=F32, tm=1024, tn=512, tk=NP_IN // 4)
    dw_in = _mm(h, dproj, name="dw_in", ta=True, out_dtype=BF16, tm=1024, tn=NP_IN // 2, tk=512)
    grad_x, dg_pre = _prenorm_bwd(dh, x2, g_pre, dy, tm=tm)

    dw_uq_h = dw_uq.reshape(Q_RANK, N_HEADS, 256)
    s_uq = jnp.concatenate([dw_uq_h[:, :, :160], dw_uq_h[:, :, 192:224]], axis=2).transpose(1, 0, 2)
    s_ukv = dw_ukv.reshape(KV_RANK, N_HEADS, 256).transpose(1, 0, 2)
    s_out = dw_out.reshape(N_DEV, D_MODEL // N_DEV, D_MODEL)
    parts = [_w_in_shards(dw_in), _by_core(s_uq), _by_core(s_ukv), _by_core(s_out)]
    from_core = _exchange_cores(parts)
    core = lax.axis_index("c").astype(jnp.int32).reshape(1)
    sums = [_pair_sum((p, o_), core, tr=256, name=f"pair_sum_{i}") for i, (p, o_) in enumerate(zip(parts, from_core))]
    l_in, l_uq, l_ukv, l_out = _exchange_chips(sums)
    small = _gather_small(_pack_small(dg_pre, dg_post, dg_q, dg_kv, dbf[:, :N_HEADS], loss_part))

    res_in = _adamw(l_in, w_in[0], m_w_in[0], v_w_in[0], tr=256, name="adamw_w_in")
    res_uq = _adamw(l_uq, w_uq[0], m_w_uq[0], v_w_uq[0], tr=256, name="adamw_w_uq")
    res_ukv = _adamw(l_ukv, w_ukv[0], m_w_ukv[0], v_w_ukv[0], tr=256, name="adamw_w_ukv")
    res_out = _adamw(l_out, w_out[0], m_w_out[0], v_w_out[0], tr=128, name="adamw_w_out")
    res_small = _adamw_small(
        small,
        _pack_small(g_pre, g_post, g_q_latent, g_kv_latent, b_forget),
        _pack_small(m_g_pre, m_g_post, m_g_q_latent, m_g_kv_latent, m_b_forget),
        _pack_small(v_g_pre, v_g_post, v_g_q_latent, v_g_kv_latent, v_b_forget))
    small_out = [_unpack_small(r) for r in res_small]
    loss = small_out[0][1]

    def leaves(kind):
        (s_pre, s_post, s_q, s_kv, s_bf), _ = small_out[kind]
        return [s_pre, res_in[kind][None], s_q, res_uq[kind][None], s_kv, res_ukv[kind][None], s_bf,
                res_out[kind][None], s_post]

    return (loss, grad_x.reshape(x.shape), *leaves(0), *leaves(1), *leaves(2), *leaves(3))
```

```python
import functools

import numpy as np
import jax
import jax.numpy as jnp
from jax import lax
from jax.experimental import pallas as pl
from jax.experimental.pallas import tpu as pltpu

F32 = jnp.float32
BF16 = jnp.bfloat16
MESH = pl.DeviceIdType.MESH

N_DEV = 8
D_MODEL = 2048
N_HEADS = 8
HEAD = 128
Q_RANK = 768
KV_RANK = 512
ROPE = 64
D_IN = 6472
SHARD_IN = D_IN // N_DEV
NORM_EPS = 1e-6
ROPE_THETA = 10000.0
MLA_SCALE = (HEAD + ROPE) ** -0.5
FOX_SCALE = HEAD ** -0.5

QL0, KR0, FL0, KVL0, GM0, GF0, FQ0, FK0, FV0, NP_IN = 0, 768, 896, 1024, 1536, 2560, 3584, 4608, 5632, 6656
LAT_W = GM0
LANE = 128
_SEGMENTS = ((0, 768, QL0), (768, 512, KVL0), (1280, 32, KR0), (1312, 32, KR0 + 64), (1344, 1024, GM0),
             (2368, 3072, FQ0), (5440, 8, FL0), (5448, 1024, GF0))
LOG2E = 1.4426950408889634

ADAM_LR = 0.001
ADAM_B1 = 0.9
ADAM_B2 = 0.999
ADAM_EPS = 1e-08
ADAM_WD = 0.01
ADAM_STEP = 10

VMEM_LIMIT_BYTES = 56 * 1024 * 1024
SMALL_ROWS, SMALL_COLS = 8, 768


def _params(n_grid=0):
    return pltpu.CompilerParams(vmem_limit_bytes=VMEM_LIMIT_BYTES,
                                dimension_semantics=("arbitrary",) * n_grid if n_grid else None)


def _sigmoid(z):
    return 1.0 / (1.0 + jnp.exp(-z))


def _split3(v):
    a = v.astype(BF16)
    r = v - a.astype(F32)
    b = r.astype(BF16)
    c = (r - b.astype(F32)).astype(BF16)
    return a, b, c


def _mm(a, b, *, name, nt=False, ta=False, out_dtype=F32, tm=1024, tn=512, tk=2048):
    assert not (nt and ta)
    k_dim, m = a.shape if ta else a.shape[::-1]
    n = b.shape[0] if nt else b.shape[1]
    assert (b.shape[1] if nt else b.shape[0]) == k_dim
    tm, tn, tk = min(tm, m), min(tn, n), min(tk, k_dim)
    assert m % tm == 0 and n % tn == 0 and k_dim % tk == 0, (name, a.shape, b.shape)
    nk = k_dim // tk
    dims = (((0 if ta else 1,), (1 if nt else 0,)), ((), ()))

    def body(a_ref, b_ref, o_ref, acc_ref):
        k = pl.program_id(2)

        @pl.when(k == 0)
        def _():
            acc_ref[...] = jnp.zeros_like(acc_ref)

        acc_ref[...] += lax.dot_general(a_ref[...], b_ref[...], dims, preferred_element_type=F32)

        @pl.when(k == nk - 1)
        def _():
            o_ref[...] = acc_ref[...].astype(o_ref.dtype)

    b_spec = (pl.BlockSpec((tn, tk), lambda i, j, k: (j, k)) if nt
              else pl.BlockSpec((tk, tn), lambda i, j, k: (k, j)))
    a_spec = (pl.BlockSpec((tk, tm), lambda i, j, k: (k, i)) if ta
              else pl.BlockSpec((tm, tk), lambda i, j, k: (i, k)))
    return pl.pallas_call(
        body, name=name, grid=(m // tm, n // tn, nk),
        in_specs=[a_spec, b_spec],
        out_specs=pl.BlockSpec((tm, tn), lambda i, j, k: (i, j)),
        out_shape=jax.ShapeDtypeStruct((m, n), out_dtype),
        scratch_shapes=[pltpu.VMEM((tm, tn), F32)],
        compiler_params=_params(3),
    )(a, b)


def _prenorm(x, g, *, tm):
    t = x.shape[0]

    def body(x_ref, g_ref, h_ref):
        xv = x_ref[...]
        r = lax.rsqrt(jnp.mean(xv * xv, axis=-1, keepdims=True) + NORM_EPS)
        h_ref[...] = (xv * r * g_ref[...]).astype(BF16)

    return pl.pallas_call(
        body, name="prenorm", grid=(t // tm,),
        in_specs=[pl.BlockSpec((tm, D_MODEL), lambda i: (i, 0)), pl.BlockSpec((1, D_MODEL), lambda i: (0, 0))],
        out_specs=pl.BlockSpec((tm, D_MODEL), lambda i: (i, 0)),
        out_shape=jax.ShapeDtypeStruct((t, D_MODEL), BF16),
        compiler_params=_params(1),
    )(x, g)


def _rope_rows():
    inv = (np.float32(ROPE_THETA) ** (-np.arange(0, ROPE, 2, dtype=np.float32) / np.float32(ROPE))).astype(np.float32)
    invf = np.zeros((1, LANE), np.float32)
    sgn = np.zeros((1, LANE), np.float32)
    invf[0, 0:32] = inv
    invf[0, 64:96] = inv
    sgn[0, 0:32] = -1.0
    sgn[0, 64:96] = 1.0
    return jnp.asarray(invf), jnp.asarray(sgn)


def _rot(v, cos_t, sin_t):
    return v * cos_t + pltpu.roll(v, 64, 1) * sin_t


def _rot_bwd(dv, cos_t, sin_t):
    return dv * cos_t + pltpu.roll(dv * sin_t, 64, 1)


def _mid_fwd(proj, flog, g_q, g_kv, bf_row, pos_col, *, tm):
    t = proj.shape[0]
    invf, sgn = _rope_rows()

    def body(p_ref, fl_ref, gq_ref, gkv_ref, bf_ref, pos_ref, invf_ref, sgn_ref,
             qn_ref, kvn_ref, kr_ref, cos_ref, sin_ref, c_ref, carry_ref):
        i = pl.program_id(0)

        @pl.when(i == 0)
        def _():
            carry_ref[...] = jnp.zeros_like(carry_ref)

        ql = p_ref[:, QL0:QL0 + Q_RANK].astype(F32)
        r = lax.rsqrt(jnp.mean(ql * ql, axis=-1, keepdims=True) + NORM_EPS)
        qn_ref[...] = (ql * r * gq_ref[...]).astype(BF16)
        kvl = p_ref[:, KVL0:KVL0 + KV_RANK].astype(F32)
        r = lax.rsqrt(jnp.mean(kvl * kvl, axis=-1, keepdims=True) + NORM_EPS)
        kvn_ref[...] = (kvl * r * gkv_ref[...]).astype(BF16)

        ang = pos_ref[...] * invf_ref[...]
        cos_t = jnp.cos(ang)
        sin_t = jnp.sin(ang) * sgn_ref[...]
        cos_ref[...] = cos_t
        sin_ref[...] = sin_t
        kr_ref[...] = _rot(p_ref[:, KR0:KR0 + LANE].astype(F32), cos_t, sin_t).astype(BF16)

        z = fl_ref[...] + bf_ref[...]
        logf = jnp.minimum(z, 0.0) - jnp.log(1.0 + jnp.exp(-jnp.abs(z)))
        row = lax.broadcasted_iota(jnp.int32, (tm, tm), 0)
        col = lax.broadcasted_iota(jnp.int32, (tm, tm), 1)
        tri = (col <= row).astype(BF16)
        acc = carry_ref[0:1, :]
        for part in _split3(logf):
            acc = acc + jnp.dot(tri, part, preferred_element_type=F32)
        c_ref[...] = acc * (1.0 / FOX_SCALE)
        carry_ref[0:1, :] = carry_ref[0:1, :] + jnp.sum(logf, axis=0, keepdims=True)

    row_spec = lambda w: pl.BlockSpec((tm, w), lambda i: (i, 0))
    vec_spec = lambda w: pl.BlockSpec((1, w), lambda i: (0, 0))
    return pl.pallas_call(
        body, name="mid_fwd", grid=(t // tm,),
        in_specs=[row_spec(LAT_W), row_spec(LANE), vec_spec(Q_RANK), vec_spec(KV_RANK), vec_spec(LANE),
                  pl.BlockSpec((tm, 1), lambda i: (i, 0)), vec_spec(LANE), vec_spec(LANE)],
        out_specs=[row_spec(Q_RANK), row_spec(KV_RANK), row_spec(LANE), row_spec(LANE), row_spec(LANE), row_spec(LANE)],
        out_shape=[jax.ShapeDtypeStruct((t, Q_RANK), BF16), jax.ShapeDtypeStruct((t, KV_RANK), BF16),
                   jax.ShapeDtypeStruct((t, LANE), BF16), jax.ShapeDtypeStruct((t, LANE), F32),
                   jax.ShapeDtypeStruct((t, LANE), F32), jax.ShapeDtypeStruct((t, LANE), F32)],
        scratch_shapes=[pltpu.VMEM((8, LANE), F32)],
        compiler_params=_params(1),
    )(proj, flog, g_q, g_kv, bf_row, pos_col, invf, sgn)


def _rope_q(q_raw, cos_t, sin_t, *, tm):
    t = q_raw.shape[0]

    def body(q_ref, cos_ref, sin_ref, o_ref):
        c, s = cos_ref[...], sin_ref[...]
        for h in range(N_HEADS):
            o_ref[:, 256 * h:256 * h + 128] = q_ref[:, 256 * h:256 * h + 128].astype(BF16)
            o_ref[:, 256 * h + 128:256 * h + 256] = _rot(q_ref[:, 256 * h + 128:256 * h + 256], c, s).astype(BF16)

    return pl.pallas_call(
        body, name="rope_q", grid=(t // tm,),
        in_specs=[pl.BlockSpec((tm, 2048), lambda i: (i, 0)), pl.BlockSpec((tm, LANE), lambda i: (i, 0)),
                  pl.BlockSpec((tm, LANE), lambda i: (i, 0))],
        out_specs=pl.BlockSpec((tm, 2048), lambda i: (i, 0)),
        out_shape=jax.ShapeDtypeStruct((t, 2048), BF16),
        compiler_params=_params(1),
    )(q_raw, cos_t, sin_t)


def _attn_fwd(fox, operands, *, t, tb, name):
    nb = t // tb
    scale = FOX_SCALE if fox else MLA_SCALE
    exp2_scale = scale * LOG2E
    pair = 2 * HEAD
    pair0 = N_HEADS // 2 if fox else 0
    q_w = HEAD if fox else 2 * HEAD
    nt_dims = (((1,), (1,)), ((), ()))
    tn_dims = (((0,), (0,)), ((), ()))

    def body(*refs):
        if fox:
            (q_ref, k_ref, v_ref, gate_ref, cq_ref, ck_ref, _, _,
             o_ref, og_ref, lse_ref, m_s, l_s, acc_s) = refs
        else:
            q_ref, kv_ref, kr_ref, gate_ref, o_ref, og_ref, lse_ref, m_s, l_s, acc_s = refs
        qi = pl.program_id(1)
        m_s[...] = jnp.full_like(m_s, -jnp.inf)
        l_s[...] = jnp.zeros_like(l_s)
        acc_s[...] = jnp.zeros_like(acc_s)

        def chunk(kc, masked):
            off = pl.multiple_of(kc * tb, tb)
            scores = []
            for u in range(2):
                q = q_ref[:, q_w * u:q_w * (u + 1)]
                if fox:
                    kk = k_ref[pl.ds(off, tb), HEAD * u:HEAD * (u + 1)]
                else:
                    kk = jnp.concatenate([kv_ref[pl.ds(off, tb), pair * u:pair * u + HEAD],
                                          kr_ref[pl.ds(off, tb), :]], axis=1)
                s = lax.dot_general(kk, q, nt_dims, preferred_element_type=F32)
                if fox:
                    s = s + cq_ref[u, 0] - ck_ref[u, pl.ds(off, tb), :]
                if masked:
                    row = lax.broadcasted_iota(jnp.int32, (tb, tb), 0)
                    col = lax.broadcasted_iota(jnp.int32, (tb, tb), 1)
                    s = jnp.where(row <= col, s, -jnp.inf)
                scores.append(s)
            for u in range(2):
                s = scores[u]
                m_prev = m_s[u]
                m_new = jnp.maximum(m_prev, jnp.max(s, axis=0, keepdims=True))
                alpha = jnp.exp2((m_prev - m_new) * exp2_scale)
                p = jnp.exp2((s - m_new) * exp2_scale)
                l_s[u] = alpha * l_s[u] + jnp.sum(p, axis=0, keepdims=True)
                if fox:
                    vv = v_ref[pl.ds(off, tb), HEAD * u:HEAD * (u + 1)]
                else:
                    vv = kv_ref[pl.ds(off, tb), pair * u + HEAD:pair * (u + 1)]
                acc_s[u] = alpha * acc_s[u] + lax.dot_general(vv, p.astype(BF16), tn_dims,
                                                              preferred_element_type=F32)
                m_s[u] = m_new

        def loop_body(kc, carry):
            chunk(kc, False)
            return carry

        lax.fori_loop(0, qi, loop_body, 0)
        chunk(qi, True)
        for u in range(2):
            cols = slice(HEAD * u, HEAD * (u + 1))
            o = (acc_s[u] / l_s[u]).T
            o_ref[:, cols] = o
            g = gate_ref[:, cols].astype(F32)
            og_ref[:, cols] = (o * (g * _sigmoid(g))).astype(BF16)
            lse_ref[u, 0] = m_s[u] * scale + jnp.log(l_s[u])

    any_spec = pl.BlockSpec(memory_space=pl.ANY)
    row_stat = pl.BlockSpec((2, 1, 1, tb), lambda g, i: (g, i, 0, 0))
    if fox:
        proj, c_col, c_row4, o_all, og_all = operands
        ins = [proj, proj, proj, proj, c_row4, c_col, o_all, og_all]
        in_specs = [pl.BlockSpec((tb, pair), lambda g, i: (i, FQ0 // pair + g)),
                    pl.BlockSpec((t, pair), lambda g, i: (0, FK0 // pair + g)),
                    pl.BlockSpec((t, pair), lambda g, i: (0, FV0 // pair + g)),
                    pl.BlockSpec((tb, pair), lambda g, i: (i, GF0 // pair + g)),
                    row_stat, pl.BlockSpec((2, t, 1), lambda g, i: (g, 0, 0)), any_spec, any_spec]
        aliases = {6: 0, 7: 1}
    else:
        q_full, kv, kr, proj = operands
        ins = [q_full, kv, kr, proj]
        in_specs = [pl.BlockSpec((tb, 2 * pair), lambda g, i: (i, g)),
                    pl.BlockSpec((t, 2 * pair), lambda g, i: (0, g)),
                    pl.BlockSpec((t, HEAD), lambda g, i: (0, 0)),
                    pl.BlockSpec((tb, pair), lambda g, i: (i, GM0 // pair + g))]
        aliases = {}
    return pl.pallas_call(
        body, name=name, grid=(N_HEADS // 2, nb), in_specs=in_specs,
        out_specs=[pl.BlockSpec((tb, pair), lambda g, i: (i, pair0 + g)),
                   pl.BlockSpec((tb, pair), lambda g, i: (i, pair0 + g)), row_stat],
        out_shape=[jax.ShapeDtypeStruct((t, 2 * N_HEADS * HEAD), F32), jax.ShapeDtypeStruct((t, 2 * N_HEADS * HEAD), BF16),
                   jax.ShapeDtypeStruct((N_HEADS, nb, 1, tb), F32)],
        scratch_shapes=[pltpu.VMEM((2, 1, tb), F32), pltpu.VMEM((2, 1, tb), F32), pltpu.VMEM((2, HEAD, tb), F32)],
        input_output_aliases=aliases,
        compiler_params=_params(2),
    )(*ins)


def _postnorm_loss(o, x, target, g, *, tm):
    t = o.shape[0]

    def body(o_ref, x_ref, t_ref, g_ref, dy_ref, do_ref, dg_ref, loss_ref):
        i = pl.program_id(0)

        @pl.when(i == 0)
        def _():
            dg_ref[...] = jnp.zeros_like(dg_ref)
            loss_ref[...] = jnp.zeros_like(loss_ref)

        ov = o_ref[...]
        gv = g_ref[...]
        r = lax.rsqrt(jnp.mean(ov * ov, axis=-1, keepdims=True) + NORM_EPS)
        oh = ov * r
        e = x_ref[...] + oh * gv - t_ref[...]
        loss_ref[...] += 0.5 * jnp.sum(jnp.mean(e * e, axis=-1, keepdims=True), axis=0, keepdims=True)
        dy = e * (1.0 / D_MODEL)
        dy_ref[...] = dy
        dyg = dy * gv
        do_ref[...] = (r * (dyg - oh * jnp.mean(dyg * oh, axis=-1, keepdims=True))).astype(BF16)
        dg_ref[...] += jnp.sum(dy * oh, axis=0, keepdims=True)

    row = pl.BlockSpec((tm, D_MODEL), lambda i: (i, 0))
    vec = pl.BlockSpec((1, D_MODEL), lambda i: (0, 0))
    return pl.pallas_call(
        body, name="postnorm_loss", grid=(t // tm,),
        in_specs=[row, row, row, vec],
        out_specs=[row, row, vec, pl.BlockSpec((1, 1), lambda i: (0, 0))],
        out_shape=[jax.ShapeDtypeStruct((t, D_MODEL), F32), jax.ShapeDtypeStruct((t, D_MODEL), BF16),
                   jax.ShapeDtypeStruct((1, D_MODEL), F32), jax.ShapeDtypeStruct((1, 1), F32)],
        compiler_params=_params(1),
    )(o, x, target, g)


def _dog_gate(d_o_post, w_out_n, o_all, proj, *, tm):
    t = d_o_post.shape[0]
    pair = 2 * HEAD
    gate_blk = GM0 // pair
    assert GM0 % pair == 0 and GF0 == GM0 + N_HEADS * HEAD

    def body(do_ref, w_ref, o_ref, p_ref, dattn_ref, dproj_ref, delta_ref):
        j = pl.program_id(1)

        @pl.when(j == 0)
        def _():
            delta_ref[...] = jnp.zeros_like(delta_ref)

        dog = lax.dot_general(do_ref[...], w_ref[...], (((1,), (1,)), ((), ())), preferred_element_type=F32)
        g = p_ref[...].astype(F32)
        ov = o_ref[...]
        sg = _sigmoid(g)
        d_o = dog * (g * sg)
        dattn_ref[...] = d_o.astype(BF16)
        dproj_ref[...] = (dog * ov * (sg * (1.0 + g * (1.0 - sg)))).astype(BF16)
        prod = d_o * ov
        lane = lax.broadcasted_iota(jnp.int32, (tm, LANE), 1)
        delta = delta_ref[...]
        for half in range(2):
            part = jnp.sum(prod[:, HEAD * half:HEAD * (half + 1)], axis=-1, keepdims=True)
            delta = jnp.where(lane == 2 * j + half, part, delta)
        delta_ref[...] = delta

    return pl.pallas_call(
        body, name="dog_gate", grid=(t // tm, N_HEADS),
        in_specs=[pl.BlockSpec((tm, D_MODEL), lambda i, j: (i, 0)), pl.BlockSpec((pair, D_MODEL), lambda i, j: (j, 0)),
                  pl.BlockSpec((tm, pair), lambda i, j: (i, j)), pl.BlockSpec((tm, pair), lambda i, j: (i, gate_blk + j))],
        out_specs=[pl.BlockSpec((tm, pair), lambda i, j: (i, j)), pl.BlockSpec((tm, pair), lambda i, j: (i, gate_blk + j)),
                   pl.BlockSpec((tm, LANE), lambda i, j: (i, 0))],
        out_shape=[jax.ShapeDtypeStruct((t, 2048), BF16), jax.ShapeDtypeStruct((t, NP_IN), BF16),
                   jax.ShapeDtypeStruct((t, LANE), F32)],
        compiler_params=_params(2),
    )(d_o_post, w_out_n, o_all, proj)


def _attn_bwd(fox, operands, *, t, tb, name):
    nb = t // tb
    scale = FOX_SCALE if fox else MLA_SCALE
    dq_w = HEAD if fox else 256
    nt_dims = (((1,), (1,)), ((), ()))
    tn_dims = (((0,), (0,)), ((), ()))

    def body(*refs):
        if fox:
            (q_ref, k_ref, v_ref, do_ref, lse_ref, dl_ref, cq_ref, ck_ref, _,
             dproj_ref, dck_ref, dcq_ref, dq_s, dk_s, dv_s, dc_s, dcq_s, stage_q, stage_k, stage_v, put_sems) = refs
        else:
            (q_ref, kn_ref, kr_ref, v_ref, do_ref, lse_ref, dl_ref,
             dq_ref, dkv_ref, dkr_ref, dq_s, dk_s, dv_s) = refs
        head = pl.program_id(0)
        ki = pl.program_id(1)

        @pl.when(ki == 0)
        def _():
            dq_s[...] = jnp.zeros_like(dq_s)
            if fox:
                dcq_s[...] = jnp.zeros_like(dcq_s)

        dk_s[...] = jnp.zeros_like(dk_s)
        dv_s[...] = jnp.zeros_like(dv_s)
        if fox:
            dc_s[...] = jnp.zeros_like(dc_s)
            kk = k_ref[...]
        else:
            kk = jnp.concatenate([kn_ref[...], kr_ref[...]], axis=1)
        vv = v_ref[...]

        def chunk(qc, masked):
            off = pl.multiple_of(qc * tb, tb)
            qq = q_ref[pl.ds(off, tb), :]
            dd = do_ref[pl.ds(off, tb), :]
            s = lax.dot_general(kk, qq, nt_dims, preferred_element_type=F32)
            if fox:
                s = s + cq_ref[0, qc] - ck_ref[0]
            if masked:
                row = lax.broadcasted_iota(jnp.int32, (tb, tb), 0)
                col = lax.broadcasted_iota(jnp.int32, (tb, tb), 1)
                s = jnp.where(row <= col, s, -jnp.inf)
            p = jnp.exp2(s * (scale * LOG2E) - lse_ref[0, qc] * LOG2E)
            dv_s[...] += jnp.dot(p.astype(BF16), dd, preferred_element_type=F32)
            dp = lax.dot_general(vv, dd, nt_dims, preferred_element_type=F32)
            ds = p * (dp - dl_ref[0, qc])
            if fox:
                dc_s[...] += jnp.sum(ds, axis=1, keepdims=True)
                dcq_s[qc] += jnp.sum(ds, axis=0, keepdims=True)
            dsb = (ds * scale).astype(BF16)
            dk_s[...] += jnp.dot(dsb, qq, preferred_element_type=F32)
            dq_s[pl.ds(off, tb), :] += lax.dot_general(dsb, kk, tn_dims, preferred_element_type=F32)

        chunk(ki, True)

        def loop_body(qc, carry):
            chunk(qc, False)
            return carry

        lax.fori_loop(ki + 1, nb, loop_body, 0)

        def put(stage_ref, rows, seg0, sem):
            col0 = pl.multiple_of(seg0 + head * HEAD, HEAD)
            return pltpu.make_async_copy(stage_ref, dproj_ref.at[rows, pl.ds(col0, HEAD)], sem)

        if fox:
            rows = pl.ds(pl.multiple_of(ki * tb, tb), tb)
            block_puts = [put(stage_k, rows, FK0, put_sems.at[1]), put(stage_v, rows, FV0, put_sems.at[2])]
            head_put = put(stage_q, pl.ds(0, t), FQ0, put_sems.at[0])

            @pl.when(jnp.logical_or(head > 0, ki > 0))
            def _():
                for cp in block_puts:
                    cp.wait()

            stage_k[...] = dk_s[...].astype(BF16)
            stage_v[...] = dv_s[...].astype(BF16)
            for cp in block_puts:
                cp.start()
            dck_ref[0] = -dc_s[...]

            @pl.when(ki == nb - 1)
            def _():
                @pl.when(head > 0)
                def _():
                    head_put.wait()

                stage_q[...] = dq_s[...].astype(BF16)
                head_put.start()
                dcq_ref[0] = dcq_s[...]

            @pl.when(jnp.logical_and(head == N_HEADS - 1, ki == nb - 1))
            def _():
                for cp in block_puts + [head_put]:
                    cp.wait()
        else:
            dkv_ref[...] = jnp.concatenate([dk_s[:, :HEAD], dv_s[...]], axis=1).astype(BF16)
            dkr_ref[...] = dk_s[:, HEAD:]

            @pl.when(ki == nb - 1)
            def _():
                dq_ref[...] = dq_s[...]

    stat = pl.BlockSpec((1, nb, 1, tb), lambda h, i: (h, 0, 0, 0))
    aliases = {}
    if fox:
        proj, d_o, lse4, delta4, c_row4, c_col, dproj = operands
        ins = [proj, proj, proj, d_o, lse4, delta4, c_row4, c_col, dproj]
        any_spec = pl.BlockSpec(memory_space=pl.ANY)
        in_specs = [pl.BlockSpec((t, HEAD), lambda h, i: (0, FQ0 // HEAD + h)),
                    pl.BlockSpec((tb, HEAD), lambda h, i: (i, FK0 // HEAD + h)),
                    pl.BlockSpec((tb, HEAD), lambda h, i: (i, FV0 // HEAD + h)),
                    pl.BlockSpec((t, HEAD), lambda h, i: (0, N_HEADS + h)),
                    stat, stat, stat, pl.BlockSpec((1, tb, 1), lambda h, i: (h, i, 0)), any_spec]
        aliases = {8: 0}
        out_specs = [any_spec, pl.BlockSpec((1, tb, 1), lambda h, i: (h, i, 0)), stat]
        out_shape = [jax.ShapeDtypeStruct(dproj.shape, dproj.dtype), jax.ShapeDtypeStruct((N_HEADS, t, 1), F32),
                     jax.ShapeDtypeStruct((N_HEADS, nb, 1, tb), F32)]
        scratch = [pltpu.VMEM((t, HEAD), F32), pltpu.VMEM((tb, HEAD), F32), pltpu.VMEM((tb, HEAD), F32),
                   pltpu.VMEM((tb, 1), F32), pltpu.VMEM((nb, 1, tb), F32),
                   pltpu.VMEM((t, HEAD), BF16), pltpu.VMEM((tb, HEAD), BF16), pltpu.VMEM((tb, HEAD), BF16),
                   pltpu.SemaphoreType.DMA((3,))]
    else:
        q_full, kv, kr, d_o, lse4, delta4 = operands
        ins = [q_full, kv, kr, kv, d_o, lse4, delta4]
        in_specs = [pl.BlockSpec((t, 256), lambda h, i: (0, h)),
                    pl.BlockSpec((tb, HEAD), lambda h, i: (i, 2 * h)),
                    pl.BlockSpec((tb, HEAD), lambda h, i: (i, 0)),
                    pl.BlockSpec((tb, HEAD), lambda h, i: (i, 2 * h + 1)),
                    pl.BlockSpec((t, HEAD), lambda h, i: (0, h)),
                    stat, stat]
        out_specs = [pl.BlockSpec((t, 256), lambda h, i: (0, h)), pl.BlockSpec((tb, 256), lambda h, i: (i, h)),
                     pl.BlockSpec((tb, HEAD), lambda h, i: (i, h))]
        out_shape = [jax.ShapeDtypeStruct((t, 2048), F32), jax.ShapeDtypeStruct((t, 2048), BF16),
                     jax.ShapeDtypeStruct((t, 1024), F32)]
        scratch = [pltpu.VMEM((t, 256), F32), pltpu.VMEM((tb, 256), F32), pltpu.VMEM((tb, HEAD), F32)]
    return pl.pallas_call(
        body, name=name, grid=(N_HEADS, nb), in_specs=in_specs, out_specs=out_specs, out_shape=out_shape,
        scratch_shapes=scratch, input_output_aliases=aliases, compiler_params=_params(2),
    )(*ins)


def _mid_bwd(dq_full, dkr, cos_t, sin_t, dck, flog, bf_row, *, tm):
    t = dq_full.shape[0]
    n = t // tm

    def body(dq_ref, dkr_ref, cos_ref, sin_ref, dck_ref, fl_ref, bf_ref,
             dq2_ref, dkraw_ref, dfl_ref, dbf_ref, carry_ref):
        i = pl.program_id(0)

        @pl.when(i == 0)
        def _():
            carry_ref[...] = jnp.zeros_like(carry_ref)
            dbf_ref[...] = jnp.zeros_like(dbf_ref)

        c, s = cos_ref[...], sin_ref[...]
        dkr_sum = jnp.zeros((tm, LANE), F32)
        for h in range(N_HEADS):
            dq2_ref[:, 256 * h:256 * h + 128] = dq_ref[:, 256 * h:256 * h + 128].astype(BF16)
            dq2_ref[:, 256 * h + 128:256 * h + 256] = _rot_bwd(dq_ref[:, 256 * h + 128:256 * h + 256], c, s).astype(BF16)
            dkr_sum = dkr_sum + dkr_ref[:, HEAD * h:HEAD * (h + 1)]
        dkraw_ref[...] = _rot_bwd(dkr_sum, c, s).astype(BF16)

        dc = dck_ref[...]
        row = lax.broadcasted_iota(jnp.int32, (tm, tm), 0)
        col = lax.broadcasted_iota(jnp.int32, (tm, tm), 1)
        tri = (col >= row).astype(BF16)
        acc = carry_ref[0:1, :]
        for part in _split3(dc):
            acc = acc + jnp.dot(tri, part, preferred_element_type=F32)
        carry_ref[0:1, :] = carry_ref[0:1, :] + jnp.sum(dc, axis=0, keepdims=True)
        z = fl_ref[...] + bf_ref[...]
        dz = acc / (1.0 + jnp.exp(z))
        dfl_ref[...] = dz.astype(BF16)
        dbf_ref[...] += jnp.sum(dz, axis=0, keepdims=True)

    rev = lambda w: pl.BlockSpec((tm, w), lambda i: (n - 1 - i, 0))
    vec = lambda w: pl.BlockSpec((1, w), lambda i: (0, 0))
    return pl.pallas_call(
        body, name="mid_bwd", grid=(n,),
        in_specs=[rev(2048), rev(1024), rev(LANE), rev(LANE), rev(LANE), rev(LANE), vec(LANE)],
        out_specs=[rev(2048), rev(LANE), rev(LANE), vec(LANE)],
        out_shape=[jax.ShapeDtypeStruct((t, 2048), BF16), jax.ShapeDtypeStruct((t, LANE), BF16),
                   jax.ShapeDtypeStruct((t, LANE), BF16), jax.ShapeDtypeStruct((1, LANE), F32)],
        scratch_shapes=[pltpu.VMEM((8, LANE), F32)],
        compiler_params=_params(1),
    )(dq_full, dkr, cos_t, sin_t, dck, flog, bf_row)


def _norm_bwd(proj, dqn, dkvn, g_q, g_kv, dkr_raw, dfl, dproj, *, tm):
    t = proj.shape[0]
    assert (KR0, FL0, KVL0, LAT_W) == (Q_RANK, Q_RANK + LANE, Q_RANK + 2 * LANE, Q_RANK + 2 * LANE + KV_RANK)

    def body(p_ref, dqn_ref, dkvn_ref, gq_ref, gkv_ref, dkr_ref, dfl_ref, _, dproj_ref, dgq_ref, dgkv_ref):
        i = pl.program_id(0)

        @pl.when(i == 0)
        def _():
            dgq_ref[...] = jnp.zeros_like(dgq_ref)
            dgkv_ref[...] = jnp.zeros_like(dgkv_ref)

        d_lat = []
        for lo, w, dn_ref, g_ref, dg_ref in ((QL0, Q_RANK, dqn_ref, gq_ref, dgq_ref),
                                             (KVL0, KV_RANK, dkvn_ref, gkv_ref, dgkv_ref)):
            xv = p_ref[:, lo:lo + w].astype(F32)
            r = lax.rsqrt(jnp.mean(xv * xv, axis=-1, keepdims=True) + NORM_EPS)
            xh = xv * r
            dn = dn_ref[...]
            dg_ref[...] += jnp.sum(dn * xh, axis=0, keepdims=True)
            dxh = dn * g_ref[...]
            d_lat.append((r * (dxh - xh * jnp.mean(dxh * xh, axis=-1, keepdims=True))).astype(BF16))
        dproj_ref[...] = jnp.concatenate([d_lat[0], dkr_ref[...], dfl_ref[...], d_lat[1]], axis=1)

    row = lambda w: pl.BlockSpec((tm, w), lambda i: (i, 0))
    vec = lambda w: pl.BlockSpec((1, w), lambda i: (0, 0))
    return pl.pallas_call(
        body, name="norm_bwd", grid=(t // tm,),
        in_specs=[row(LAT_W), row(Q_RANK), row(KV_RANK), vec(Q_RANK), vec(KV_RANK), row(LANE), row(LANE),
                  pl.BlockSpec(memory_space=pl.ANY)],
        out_specs=[row(LAT_W), vec(Q_RANK), vec(KV_RANK)],
        out_shape=[jax.ShapeDtypeStruct(dproj.shape, dproj.dtype),
                   jax.ShapeDtypeStruct((1, Q_RANK), F32), jax.ShapeDtypeStruct((1, KV_RANK), F32)],
        input_output_aliases={7: 0},
        compiler_params=_params(1),
    )(proj, dqn, dkvn, g_q, g_kv, dkr_raw, dfl, dproj)


def _prenorm_bwd(dh, x, g, dy, *, tm):
    t = x.shape[0]

    def body(dh_ref, x_ref, g_ref, dy_ref, gx_ref, dg_ref):
        i = pl.program_id(0)

        @pl.when(i == 0)
        def _():
            dg_ref[...] = jnp.zeros_like(dg_ref)

        xv = x_ref[...]
        r = lax.rsqrt(jnp.mean(xv * xv, axis=-1, keepdims=True) + NORM_EPS)
        xh = xv * r
        dn = dh_ref[...]
        dg_ref[...] += jnp.sum(dn * xh, axis=0, keepdims=True)
        dxh = dn * g_ref[...]
        gx_ref[...] = dy_ref[...] + r * (dxh - xh * jnp.mean(dxh * xh, axis=-1, keepdims=True))

    row = pl.BlockSpec((tm, D_MODEL), lambda i: (i, 0))
    vec = pl.BlockSpec((1, D_MODEL), lambda i: (0, 0))
    return pl.pallas_call(
        body, name="prenorm_bwd", grid=(t // tm,),
        in_specs=[row, row, vec, row], out_specs=[row, vec],
        out_shape=[jax.ShapeDtypeStruct((t, D_MODEL), F32), jax.ShapeDtypeStruct((1, D_MODEL), F32)],
        compiler_params=_params(1),
    )(dh, x, g, dy)


N_CHIPS = 4


def _pair_sum(parts, core, *, tr, name):
    mine, other = parts
    _, _, rows, cols = mine.shape
    tr = min(tr, rows)

    def body(core_ref, a_ref, b_ref, o_ref):
        o_ref[...] = (a_ref[0].astype(F32) + b_ref[...].astype(F32)).astype(BF16)

    return pl.pallas_call(
        body, name=name,
        grid_spec=pltpu.PrefetchScalarGridSpec(
            num_scalar_prefetch=1, grid=(N_CHIPS, rows // tr),
            in_specs=[pl.BlockSpec((1, 1, tr, cols), lambda j, i, core_ref: (core_ref[0], j, i, 0)),
                      pl.BlockSpec((1, tr, cols), lambda j, i, core_ref: (j, i, 0))],
            out_specs=pl.BlockSpec((1, tr, cols), lambda j, i, core_ref: (j, i, 0))),
        out_shape=jax.ShapeDtypeStruct(other.shape, BF16),
        compiler_params=_params(2),
    )(core, mine, other)
def _adam_math(w, g, m, v):
    m = ADAM_B1 * m + (1.0 - ADAM_B1) * g
    v = ADAM_B2 * v + (1.0 - ADAM_B2) * (g * g)
    m_hat = m / (1.0 - ADAM_B1 ** ADAM_STEP)
    v_hat = v / (1.0 - ADAM_B2 ** ADAM_STEP)
    delta = -ADAM_LR * (m_hat / (jnp.sqrt(v_hat) + ADAM_EPS) + ADAM_WD * w)
    return delta, m, v


def _adamw(land, w, m, v, *, tr, name):
    rows, cols = w.shape

    def body(l_ref, w_ref, m_ref, v_ref, g_ref, d_ref, nm_ref, nv_ref):
        g = l_ref[0].astype(F32)
        for s in range(1, N_CHIPS):
            g = g + l_ref[s].astype(F32)
        g_ref[...] = g
        d_ref[...], nm_ref[...], nv_ref[...] = _adam_math(w_ref[...], g, m_ref[...], v_ref[...])

    blk = pl.BlockSpec((tr, cols), lambda i: (i, 0))
    return pl.pallas_call(
        body, name=name, grid=(rows // tr,),
        in_specs=[pl.BlockSpec((N_CHIPS, tr, cols), lambda i: (0, i, 0)), blk, blk, blk],
        out_specs=[blk, blk, blk, blk],
        out_shape=[jax.ShapeDtypeStruct((rows, cols), F32)] * 4,
        compiler_params=_params(1),
    )(land, w, m, v)


def _adamw_small(gathered, w, m, v):
    def body(a_ref, w_ref, m_ref, v_ref, g_ref, d_ref, nm_ref, nv_ref):
        g = a_ref[0:SMALL_ROWS, :]
        for s in range(1, N_DEV):
            g = g + a_ref[SMALL_ROWS * s:SMALL_ROWS * (s + 1), :]
        g_ref[...] = g
        d_ref[...], nm_ref[...], nv_ref[...] = _adam_math(w_ref[...], g, m_ref[...], v_ref[...])

    return pl.pallas_call(
        body, name="adamw_small",
        out_shape=[jax.ShapeDtypeStruct((SMALL_ROWS, SMALL_COLS), F32)] * 4,
        compiler_params=_params(),
    )(gathered, w, m, v)


def _place():
    x, y, c = lax.axis_index("x"), lax.axis_index("y"), lax.axis_index("c")
    return x, y, c


def _flip(p, k):
    x, y, c = p
    return (1 - x if k & 4 else x, 1 - y if k & 2 else y, 1 - c if k & 1 else c)


def _index(p):
    return 4 * p[0] + 2 * p[1] + p[2]


def _all_gather(shards):
    n = len(shards)
    hbm = pl.BlockSpec(memory_space=pl.ANY)

    def body(*refs):
        ins, outs = refs[:n], refs[n:2 * n]
        send_sems, recv_sems, local_sems = refs[2 * n:]
        me = _place()
        sibling = _flip(me, 1)
        chips = [_flip(me, 4), _flip(me, 2), _flip(me, 6)]

        def copy(a, k, block, to, src=None):
            dst = outs[a].at[_index(block)]
            return pltpu.make_async_remote_copy(
                src_ref=dst if src is None else src, dst_ref=dst,
                send_sem=send_sems.at[7 * a + k], recv_sem=recv_sems.at[7 * a + k],
                device_id=to, device_id_type=MESH)

        started = []
        for a in range(n):
            mine = pltpu.make_async_copy(ins[a], outs[a].at[_index(me)], local_sems.at[a])
            mine.start()
            started.append(mine)
        first = []
        for a in range(n):
            first.append(copy(a, 0, me, sibling, src=ins[a]))
            first += [copy(a, 1 + j, me, chip, src=ins[a]) for j, chip in enumerate(chips)]
        for cp in first:
            cp.start()
        passed = []
        for a in range(n):
            for j, chip in enumerate(chips):
                copy(a, 1 + j, chip, me).wait_recv()
                fwd = copy(a, 4 + j, chip, sibling)
                fwd.start()
                passed.append(fwd)
        for a in range(n):
            copy(a, 0, sibling, me).wait_recv()
            for j, chip in enumerate(chips):
                copy(a, 4 + j, _flip(chip, 1), me).wait_recv()
        for cp in first + passed:
            cp.wait_send()
        for mine in started:
            mine.wait()

    return pl.pallas_call(
        body, name="all_gather_weights",
        in_specs=[hbm] * n, out_specs=[hbm] * n,
        out_shape=[jax.ShapeDtypeStruct((N_DEV,) + s.shape, s.dtype) for s in shards],
        scratch_shapes=[pltpu.SemaphoreType.DMA((7 * n,)), pltpu.SemaphoreType.DMA((7 * n,)),
                        pltpu.SemaphoreType.DMA((n,))],
    )(*shards)


def _exchange_cores(parts):
    n = len(parts)
    hbm = pl.BlockSpec(memory_space=pl.ANY)

    def body(*refs):
        srcs, lands = refs[:n], refs[n:2 * n]
        send_sems, recv_sems = refs[2 * n:]
        me = _place()
        sibling = _flip(me, 1)
        copies = []
        for a in range(n):
            cp = pltpu.make_async_remote_copy(
                src_ref=srcs[a].at[1 - me[2]], dst_ref=lands[a], send_sem=send_sems.at[a], recv_sem=recv_sems.at[a],
                device_id=sibling, device_id_type=MESH)
            cp.start()
            copies.append(cp)
        for cp in copies:
            cp.wait()

    return pl.pallas_call(
        body, name="exchange_cores",
        in_specs=[hbm] * n, out_specs=[hbm] * n,
        out_shape=[jax.ShapeDtypeStruct(p.shape[1:], p.dtype) for p in parts],
        scratch_shapes=[pltpu.SemaphoreType.DMA((n,)), pltpu.SemaphoreType.DMA((n,))],
    )(*parts)


def _exchange_chips(sums):
    n = len(sums)
    hbm = pl.BlockSpec(memory_space=pl.ANY)
    other_chips = (4, 2, 6)

    def body(*refs):
        srcs, lands = refs[:n], refs[n:2 * n]
        send_sems, recv_sems, local_sems = refs[2 * n:]
        me = _place()
        chip = lambda p: 2 * p[0] + p[1]
        started, sends = [], []
        for a in range(n):
            mine = pltpu.make_async_copy(srcs[a].at[chip(me)], lands[a].at[chip(me)], local_sems.at[a])
            mine.start()
            started.append(mine)
        for j, k in enumerate(other_chips):
            peer = _flip(me, k)
            for a in range(n):
                cp = pltpu.make_async_remote_copy(
                    src_ref=srcs[a].at[chip(peer)], dst_ref=lands[a].at[chip(me)],
                    send_sem=send_sems.at[3 * a + j], recv_sem=recv_sems.at[3 * a + j],
                    device_id=peer, device_id_type=MESH)
                cp.start()
                sends.append(cp)
        for j, k in enumerate(other_chips):
            peer = _flip(me, k)
            for a in range(n):
                slot = lands[a].at[chip(peer)]
                pltpu.make_async_remote_copy(
                    src_ref=slot, dst_ref=slot, send_sem=send_sems.at[3 * a + j], recv_sem=recv_sems.at[3 * a + j],
                    device_id=peer, device_id_type=MESH).wait_recv()
        for cp in sends:
            cp.wait_send()
        for mine in started:
            mine.wait()

    return pl.pallas_call(
        body, name="exchange_chips",
        in_specs=[hbm] * n, out_specs=[hbm] * n,
        out_shape=[jax.ShapeDtypeStruct(s.shape, s.dtype) for s in sums],
        scratch_shapes=[pltpu.SemaphoreType.DMA((3 * n,)), pltpu.SemaphoreType.DMA((3 * n,)),
                        pltpu.SemaphoreType.DMA((n,))],
    )(*sums)


def _gather_small(vec):
    def body(v_ref, out_ref, send_sems, recv_sems, local_sem):
        me = _place()

        def rows(p):
            return out_ref.at[pl.ds(pl.multiple_of(_index(p) * SMALL_ROWS, SMALL_ROWS), SMALL_ROWS), :]

        mine = pltpu.make_async_copy(v_ref, rows(me), local_sem)
        mine.start()
        sends = []
        for k in range(1, N_DEV):
            peer = _flip(me, k)
            cp = pltpu.make_async_remote_copy(src_ref=v_ref, dst_ref=rows(me), send_sem=send_sems.at[k - 1],
                                              recv_sem=recv_sems.at[k - 1], device_id=peer, device_id_type=MESH)
            cp.start()
            sends.append(cp)
        for k in range(1, N_DEV):
            peer = _flip(me, k)
            pltpu.make_async_remote_copy(src_ref=rows(peer), dst_ref=rows(peer), send_sem=send_sems.at[k - 1],
                                         recv_sem=recv_sems.at[k - 1], device_id=peer, device_id_type=MESH).wait_recv()
        for cp in sends:
            cp.wait_send()
        mine.wait()

    return pl.pallas_call(
        body, name="gather_small",
        in_specs=[pl.BlockSpec(memory_space=pltpu.VMEM)], out_specs=pl.BlockSpec(memory_space=pltpu.VMEM),
        out_shape=jax.ShapeDtypeStruct((N_DEV * SMALL_ROWS, SMALL_COLS), F32),
        scratch_shapes=[pltpu.SemaphoreType.DMA((7,)), pltpu.SemaphoreType.DMA((7,)), pltpu.SemaphoreType.DMA],
    )(vec)


def _w_in_nice(gathered):
    pieces, pos = [], 0
    for o0, width, n0 in sorted(_SEGMENTS, key=lambda seg: seg[2]):
        if n0 > pos:
            pieces.append(jnp.zeros((D_MODEL, n0 - pos), gathered.dtype))
        o = o0
        while o < o0 + width:
            d = o // SHARD_IN
            hi = min(o0 + width, (d + 1) * SHARD_IN)
            pieces.append(gathered[d][:, o - d * SHARD_IN:hi - d * SHARD_IN])
            o = hi
        pos = n0 + width
    pieces.append(jnp.zeros((D_MODEL, NP_IN - pos), gathered.dtype))
    return jnp.concatenate(pieces, axis=1)


def _w_in_shards(dw):
    blocks = []
    for core in range(2):
        for chip in range(N_CHIPS):
            lo = (2 * chip + core) * SHARD_IN
            runs = []
            for o0, width, n0 in _SEGMENTS:
                a, b = max(lo, o0), min(lo + SHARD_IN, o0 + width)
                if a < b:
                    runs.append(dw[:, n0 + a - o0:n0 + b - o0])
            blocks.append(jnp.concatenate(runs, axis=1))
    return jnp.stack(blocks).reshape(2, N_CHIPS, D_MODEL, SHARD_IN)


def _by_core(shards):
    return shards.reshape((N_CHIPS, 2) + shards.shape[1:]).swapaxes(0, 1)


def _w_uq_nice(shard):
    z = jnp.zeros((Q_RANK, 32), shard.dtype)
    return jnp.concatenate([shard[:, :128], shard[:, 128:160], z, shard[:, 160:192], z], axis=1)


def _pack_small(g_pre, g_post, g_q, g_kv, b_f, extra=None):
    parts = [g_pre.reshape(-1), g_post.reshape(-1), g_q.reshape(-1), g_kv.reshape(-1), b_f.reshape(-1)]
    if extra is not None:
        parts.append(extra.reshape(-1))
    flat = jnp.concatenate(parts)
    flat = jnp.concatenate([flat, jnp.zeros((SMALL_ROWS * SMALL_COLS - flat.shape[0],), F32)])
    return flat.reshape(SMALL_ROWS, SMALL_COLS)


def _unpack_small(packed):
    flat = packed.reshape(-1)
    o = 0
    out = []
    for n in (D_MODEL, D_MODEL, Q_RANK, KV_RANK, N_HEADS):
        out.append(flat[o:o + n].reshape(1, n))
        o += n
    return out, flat[o]


def kernel(x, positions, g_pre, w_in, g_q_latent, w_uq, g_kv_latent, w_ukv, b_forget, w_out, g_post, loss_target, m_g_pre, m_w_in, m_g_q_latent, m_w_uq, m_g_kv_latent, m_w_ukv, m_b_forget, m_w_out, m_g_post, v_g_pre, v_w_in, v_g_q_latent, v_w_uq, v_g_kv_latent, v_w_ukv, v_b_forget, v_w_out, v_g_post):
    t = x.shape[1]
    tb = min(512, t)
    tm = min(256, t)
    nb = t // tb
    x2 = x.reshape(t, D_MODEL)
    target = loss_target.reshape(t, D_MODEL)
    pos_col = positions.reshape(t, 1).astype(F32)
    bf_row = jnp.concatenate([b_forget.reshape(1, N_HEADS), jnp.zeros((1, LANE - N_HEADS), F32)], axis=1)

    g_in, g_uq, g_ukv, g_out = _all_gather([
        w_in[0].astype(BF16), _w_uq_nice(w_uq[0].astype(BF16)), w_ukv[0].astype(BF16), w_out[0].astype(BF16)])
    w_in_n = _w_in_nice(g_in)
    w_uq_n = g_uq.transpose(1, 0, 2).reshape(Q_RANK, N_HEADS * 256)
    w_ukv_n = g_ukv.transpose(1, 0, 2).reshape(KV_RANK, N_HEADS * 256)
    w_out_n = g_out.reshape(D_MODEL, D_MODEL)

    h = _prenorm(x2, g_pre, tm=tm)
    proj = _mm(h, w_in_n, name="proj_in", out_dtype=BF16, tm=1024, tn=512, tk=2048)
    flog = _mm(h, w_in_n[:, FL0:FL0 + LANE], name="proj_flog", out_dtype=F32, tm=1024, tn=LANE, tk=2048)
    qn, kvn, kr, cos_t, sin_t, c = _mid_fwd(proj, flog, g_q_latent, g_kv_latent, bf_row, pos_col, tm=tm)
    q_raw = _mm(qn, w_uq_n, name="q_up", out_dtype=F32, tm=1024, tn=512, tk=Q_RANK)
    q_full = _rope_q(q_raw, cos_t, sin_t, tm=tm)
    kv = _mm(kvn, w_ukv_n, name="kv_up", out_dtype=BF16, tm=1024, tn=512, tk=KV_RANK)
    c_heads = c[:, :N_HEADS].T
    c_col = c_heads.reshape(N_HEADS, t, 1)
    c_row4 = c_heads.reshape(N_HEADS, nb, 1, tb)
    o_all, og_all, lse4_mla = _attn_fwd(False, (q_full, kv, kr, proj), t=t, tb=tb, name="mla_fwd")
    o_all, og_all, lse4_fox = _attn_fwd(True, (proj, c_col, c_row4, o_all, og_all), t=t, tb=tb, name="fox_fwd")
    o = _mm(og_all, w_out_n, name="out_proj", out_dtype=F32, tm=1024, tn=512, tk=2048)
    dy, d_o_post, dg_post, loss_part = _postnorm_loss(o, x2, target, g_post, tm=tm)

    dw_out = _mm(og_all, d_o_post, name="dw_out", ta=True, out_dtype=BF16, tm=1024, tn=1024, tk=512)
    d_attn, dproj, delta = _dog_gate(d_o_post, w_out_n, o_all, proj, tm=min(1024, t))
    delta4 = delta[:, :2 * N_HEADS].T.reshape(2 * N_HEADS, nb, 1, tb)
    dq_full, dkv, dkr = _attn_bwd(False, (q_full, kv, kr, d_attn, lse4_mla, delta4[:N_HEADS]),
                                  t=t, tb=tb, name="mla_bwd")
    dproj, dck, dcq = _attn_bwd(True, (proj, d_attn, lse4_fox, delta4[N_HEADS:], c_row4, c_col, dproj),
                                t=t, tb=tb, name="fox_bwd")
    dc_heads = dck.reshape(N_HEADS, t) + dcq.reshape(N_HEADS, t)
    dck_rows = jnp.concatenate([dc_heads.T, jnp.zeros((t, LANE - N_HEADS), F32)], axis=1)
    dq2, dkr_raw, dfl, dbf = _mid_bwd(dq_full, dkr, cos_t, sin_t, dck_rows, flog, bf_row, tm=tm)
    dqn = _mm(dq2, w_uq_n, name="d_qn", nt=True, out_dtype=F32, tm=1024, tn=Q_RANK, tk=2048)
    dkvn = _mm(dkv, w_ukv_n, name="d_kvn", nt=True, out_dtype=F32, tm=1024, tn=KV_RANK, tk=2048)
    dw_uq = _mm(qn, dq2, name="dw_uq", ta=True, out_dtype=BF16, tm=Q_RANK, tn=1024, tk=512)
    dw_ukv = _mm(kvn, dkv, name="dw_ukv", ta=True, out_dtype=BF16, tm=KV_RANK, tn=1024, tk=512)
    dproj, dg_q, dg_kv = _norm_bwd(proj, dqn, dkvn, g_q_latent, g_kv_latent, dkr_raw, dfl, dproj, tm=tm)
    dh = _mm(dproj, w_in_n, name="d_h", nt=True, out_dtype=F32, tm=1024, tn=512, tk=NP_IN // 4)
    dw_in = _mm(h, dproj, name="dw_in", ta=True, out_dtype=BF16, tm=1024, tn=NP_IN // 2, tk=512)
    grad_x, dg_pre = _prenorm_bwd(dh, x2, g_pre, dy, tm=tm)

    dw_uq_h = dw_uq.reshape(Q_RANK, N_HEADS, 256)
    s_uq = jnp.concatenate([dw_uq_h[:, :, :160], dw_uq_h[:, :, 192:224]], axis=2).transpose(1, 0, 2)
    s_ukv = dw_ukv.reshape(KV_RANK, N_HEADS, 256).transpose(1, 0, 2)
    s_out = dw_out.reshape(N_DEV, D_MODEL // N_DEV, D_MODEL)
    parts = [_w_in_shards(dw_in), _by_core(s_uq), _by_core(s_ukv), _by_core(s_out)]
    from_core = _exchange_cores(parts)
    core = lax.axis_index("c").astype(jnp.int32).reshape(1)
    sums = [_pair_sum((p, o_), core, tr=256, name=f"pair_sum_{i}") for i, (p, o_) in enumerate(zip(parts, from_core))]
    l_in, l_uq, l_ukv, l_out = _exchange_chips(sums)
    small = _gather_small(_pack_small(dg_pre, dg_post, dg_q, dg_kv, dbf[:, :N_HEADS], loss_part))

    res_in = _adamw(l_in, w_in[0], m_w_in[0], v_w_in[0], tr=256, name="adamw_w_in")
    res_uq = _adamw(l_uq, w_uq[0], m_w_uq[0], v_w_uq[0], tr=256, name="adamw_w_uq")
    res_ukv = _adamw(l_ukv, w_ukv[0], m_w_ukv[0], v_w_ukv[0], tr=256, name="adamw_w_ukv")
    res_out = _adamw(l_out, w_out[0], m_w_out[0], v_w_out[0], tr=128, name="adamw_w_out")
    res_small = _adamw_small(
        small,
        _pack_small(g_pre, g_post, g_q_latent, g_kv_latent, b_forget),
        _pack_small(m_g_pre, m_g_post, m_g_q_latent, m_g_kv_latent, m_b_forget),
        _pack_small(v_g_pre, v_g_post, v_g_q_latent, v_g_kv_latent, v_b_forget))
    small_out = [_unpack_small(r) for r in res_small]
    loss = small_out[0][1]

    def leaves(kind):
        (s_pre, s_post, s_q, s_kv, s_bf), _ = small_out[kind]
        return [s_pre, res_in[kind][None], s_q, res_uq[kind][None], s_kv, res_ukv[kind][None], s_bf,
                res_out[kind][None], s_post]

    return (loss, grad_x.reshape(x.shape), *leaves(0), *leaves(1), *leaves(2), *leaves(3))
```

```python
import functools

import numpy as np
import jax
import jax.numpy as jnp
from jax import lax
from jax.experimental import pallas as pl
from jax.experimental.pallas import tpu as pltpu

F32 = jnp.float32
BF16 = jnp.bfloat16
MESH = pl.DeviceIdType.MESH

N_DEV = 8
D_MODEL = 2048
N_HEADS = 8
HEAD = 128
Q_RANK = 768
KV_RANK = 512
ROPE = 64
D_IN = 6472
SHARD_IN = D_IN // N_DEV
NORM_EPS = 1e-6
ROPE_THETA = 10000.0
MLA_SCALE = (HEAD + ROPE) ** -0.5
FOX_SCALE = HEAD ** -0.5

QL0, KR0, FL0, KVL0, GM0, GF0, FQ0, FK0, FV0, NP_IN = 0, 768, 896, 1024, 1536, 2560, 3584, 4608, 5632, 6656
LAT_W = GM0
LANE = 128
_SEGMENTS = ((0, 768, QL0), (768, 512, KVL0), (1280, 32, KR0), (1312, 32, KR0 + 64), (1344, 1024, GM0),
             (2368, 3072, FQ0), (5440, 8, FL0), (5448, 1024, GF0))
LOG2E = 1.4426950408889634

ADAM_LR = 0.001
ADAM_B1 = 0.9
ADAM_B2 = 0.999
ADAM_EPS = 1e-08
ADAM_WD = 0.01
ADAM_STEP = 10

VMEM_LIMIT_BYTES = 56 * 1024 * 1024
SMALL_ROWS, SMALL_COLS = 8, 768


def _params(n_grid=0):
    return pltpu.CompilerParams(vmem_limit_bytes=VMEM_LIMIT_BYTES,
                                dimension_semantics=("arbitrary",) * n_grid if n_grid else None)


def _sigmoid(z):
    return 1.0 / (1.0 + jnp.exp(-z))


def _split3(v):
    a = v.astype(BF16)
    r = v - a.astype(F32)
    b = r.astype(BF16)
    c = (r - b.astype(F32)).astype(BF16)
    return a, b, c


def _mm(a, b, *, name, nt=False, ta=False, out_dtype=F32, tm=1024, tn=512, tk=2048, b_cols=None, rider=None):
    assert not (nt and ta)
    k_dim, m = a.shape if ta else a.shape[::-1]
    n = b.shape[0] if nt else b.shape[1]
    col0 = 0
    if b_cols is not None:
        assert not nt
        col0, n = b_cols
    assert (b.shape[1] if nt else b.shape[0]) == k_dim
    tm, tn, tk = min(tm, m), min(tn, n), min(tk, k_dim)
    assert m % tm == 0 and n % tn == 0 and k_dim % tk == 0 and col0 % tn == 0, (name, a.shape, b.shape)
    nk = k_dim // tk
    j0 = col0 // tn
    grid = (m // tm, n // tn, nk)
    dims = (((0 if ta else 1,), (1 if nt else 0,)), ((), ()))
    n_rin = len(rider.operands) if rider else 0
    n_rout = len(rider.out_shape) if rider else 0

    def body(*refs):
        a_ref, b_ref = refs[:2]
        o_ref = refs[2 + n_rin]
        acc_ref = refs[3 + n_rin + n_rout]
        i, j, k = pl.program_id(0), pl.program_id(1), pl.program_id(2)
        if rider:
            rider_refs = (refs[2:2 + n_rin], refs[3 + n_rin:3 + n_rin + n_rout], refs[4 + n_rin + n_rout:])

            @pl.when(jnp.logical_and(i == 0, jnp.logical_and(j == 0, k == 0)))
            def _():
                rider.start(*rider_refs)

        @pl.when(k == 0)
        def _():
            acc_ref[...] = jnp.zeros_like(acc_ref)

        acc_ref[...] += lax.dot_general(a_ref[...], b_ref[...], dims, preferred_element_type=F32)

        @pl.when(k == nk - 1)
        def _():
            o_ref[...] = acc_ref[...].astype(o_ref.dtype)

        if rider:
            @pl.when(jnp.logical_and(i == grid[0] - 1, jnp.logical_and(j == grid[1] - 1, k == nk - 1)))
            def _():
                rider.wait(*rider_refs)

    b_spec = (pl.BlockSpec((tn, tk), lambda i, j, k: (j, k)) if nt
              else pl.BlockSpec((tk, tn), lambda i, j, k: (k, j0 + j)))
    a_spec = (pl.BlockSpec((tk, tm), lambda i, j, k: (k, i)) if ta
              else pl.BlockSpec((tm, tk), lambda i, j, k: (i, k)))
    any_spec = pl.BlockSpec(memory_space=pl.ANY)
    out = pl.pallas_call(
        body, name=name, grid=grid,
        in_specs=[a_spec, b_spec] + [any_spec] * n_rin,
        out_specs=[pl.BlockSpec((tm, tn), lambda i, j, k: (i, j))] + [any_spec] * n_rout,
        out_shape=[jax.ShapeDtypeStruct((m, n), out_dtype)] + (list(rider.out_shape) if rider else []),
        scratch_shapes=[pltpu.VMEM((tm, tn), F32)] + (list(rider.scratch) if rider else []),
        input_output_aliases={2 + i_in: 1 + i_out for i_in, i_out in rider.aliases.items()} if rider else {},
        compiler_params=_params(3),
    )(a, b, *(rider.operands if rider else ()))
    return out if rider else out[0]


def _prenorm(x, g, *, tm):
    t = x.shape[0]

    def body(x_ref, g_ref, h_ref):
        xv = x_ref[...]
        r = lax.rsqrt(jnp.mean(xv * xv, axis=-1, keepdims=True) + NORM_EPS)
        h_ref[...] = (xv * r * g_ref[...]).astype(BF16)

    return pl.pallas_call(
        body, name="prenorm", grid=(t // tm,),
        in_specs=[pl.BlockSpec((tm, D_MODEL), lambda i: (i, 0)), pl.BlockSpec((1, D_MODEL), lambda i: (0, 0))],
        out_specs=pl.BlockSpec((tm, D_MODEL), lambda i: (i, 0)),
        out_shape=jax.ShapeDtypeStruct((t, D_MODEL), BF16),
        compiler_params=_params(1),
    )(x, g)


def _rope_rows():
    inv = (np.float32(ROPE_THETA) ** (-np.arange(0, ROPE, 2, dtype=np.float32) / np.float32(ROPE))).astype(np.float32)
    invf = np.zeros((1, LANE), np.float32)
    sgn = np.zeros((1, LANE), np.float32)
    invf[0, 0:32] = inv
    invf[0, 64:96] = inv
    sgn[0, 0:32] = -1.0
    sgn[0, 64:96] = 1.0
    return jnp.asarray(invf), jnp.asarray(sgn)


def _rot(v, cos_t, sin_t):
    return v * cos_t + pltpu.roll(v, 64, 1) * sin_t


def _rot_bwd(dv, cos_t, sin_t):
    return dv * cos_t + pltpu.roll(dv * sin_t, 64, 1)


def _mid_fwd(proj, flog, g_q, g_kv, bf_row, pos_col, *, tm):
    t = proj.shape[0]
    invf, sgn = _rope_rows()

    def body(p_ref, fl_ref, gq_ref, gkv_ref, bf_ref, pos_ref, invf_ref, sgn_ref,
             qn_ref, kvn_ref, kr_ref, cos_ref, sin_ref, c_ref, carry_ref):
        i = pl.program_id(0)

        @pl.when(i == 0)
        def _():
            carry_ref[...] = jnp.zeros_like(carry_ref)

        ql = p_ref[:, QL0:QL0 + Q_RANK].astype(F32)
        r = lax.rsqrt(jnp.mean(ql * ql, axis=-1, keepdims=True) + NORM_EPS)
        qn_ref[...] = (ql * r * gq_ref[...]).astype(BF16)
        kvl = p_ref[:, KVL0:KVL0 + KV_RANK].astype(F32)
        r = lax.rsqrt(jnp.mean(kvl * kvl, axis=-1, keepdims=True) + NORM_EPS)
        kvn_ref[...] = (kvl * r * gkv_ref[...]).astype(BF16)

        ang = pos_ref[...] * invf_ref[...]
        cos_t = jnp.cos(ang)
        sin_t = jnp.sin(ang) * sgn_ref[...]
        cos_ref[...] = cos_t
        sin_ref[...] = sin_t
        kr_ref[...] = _rot(p_ref[:, KR0:KR0 + LANE].astype(F32), cos_t, sin_t).astype(BF16)

        z = fl_ref[...] + bf_ref[...]
        logf = jnp.minimum(z, 0.0) - jnp.log(1.0 + jnp.exp(-jnp.abs(z)))
        row = lax.broadcasted_iota(jnp.int32, (tm, tm), 0)
        col = lax.broadcasted_iota(jnp.int32, (tm, tm), 1)
        tri = (col <= row).astype(BF16)
        acc = carry_ref[0:1, :]
        for part in _split3(logf):
            acc = acc + jnp.dot(tri, part, preferred_element_type=F32)
        c_ref[...] = acc * (1.0 / FOX_SCALE)
        carry_ref[0:1, :] = carry_ref[0:1, :] + jnp.sum(logf, axis=0, keepdims=True)

    row_spec = lambda w: pl.BlockSpec((tm, w), lambda i: (i, 0))
    vec_spec = lambda w: pl.BlockSpec((1, w), lambda i: (0, 0))
    return pl.pallas_call(
        body, name="mid_fwd", grid=(t // tm,),
        in_specs=[row_spec(LAT_W), row_spec(LANE), vec_spec(Q_RANK), vec_spec(KV_RANK), vec_spec(LANE),
                  pl.BlockSpec((tm, 1), lambda i: (i, 0)), vec_spec(LANE), vec_spec(LANE)],
        out_specs=[row_spec(Q_RANK), row_spec(KV_RANK), row_spec(LANE), row_spec(LANE), row_spec(LANE), row_spec(LANE)],
        out_shape=[jax.ShapeDtypeStruct((t, Q_RANK), BF16), jax.ShapeDtypeStruct((t, KV_RANK), BF16),
                   jax.ShapeDtypeStruct((t, LANE), BF16), jax.ShapeDtypeStruct((t, LANE), F32),
                   jax.ShapeDtypeStruct((t, LANE), F32), jax.ShapeDtypeStruct((t, LANE), F32)],
        scratch_shapes=[pltpu.VMEM((8, LANE), F32)],
        compiler_params=_params(1),
    )(proj, flog, g_q, g_kv, bf_row, pos_col, invf, sgn)


def _rope_q(q_raw, cos_t, sin_t, *, tm):
    t = q_raw.shape[0]

    def body(q_ref, cos_ref, sin_ref, o_ref):
        c, s = cos_ref[...], sin_ref[...]
        for h in range(N_HEADS):
            o_ref[:, 256 * h:256 * h + 128] = q_ref[:, 256 * h:256 * h + 128].astype(BF16)
            o_ref[:, 256 * h + 128:256 * h + 256] = _rot(q_ref[:, 256 * h + 128:256 * h + 256], c, s).astype(BF16)

    return pl.pallas_call(
        body, name="rope_q", grid=(t // tm,),
        in_specs=[pl.BlockSpec((tm, 2048), lambda i: (i, 0)), pl.BlockSpec((tm, LANE), lambda i: (i, 0)),
                  pl.BlockSpec((tm, LANE), lambda i: (i, 0))],
        out_specs=pl.BlockSpec((tm, 2048), lambda i: (i, 0)),
        out_shape=jax.ShapeDtypeStruct((t, 2048), BF16),
        compiler_params=_params(1),
    )(q_raw, cos_t, sin_t)


def _attn_fwd(fox, operands, *, t, tb, name):
    nb = t // tb
    scale = FOX_SCALE if fox else MLA_SCALE
    exp2_scale = scale * LOG2E
    pair = 2 * HEAD
    pair0 = N_HEADS // 2 if fox else 0
    q_w = HEAD if fox else 2 * HEAD
    nt_dims = (((1,), (1,)), ((), ()))
    tn_dims = (((0,), (0,)), ((), ()))

    def body(*refs):
        if fox:
            (q_ref, k_ref, v_ref, gate_ref, cq_ref, ck_ref, _, _,
             o_ref, og_ref, lse_ref, m_s, l_s, acc_s) = refs
        else:
            q_ref, kv_ref, kr_ref, gate_ref, o_ref, og_ref, lse_ref, m_s, l_s, acc_s = refs
        qi = pl.program_id(1)
        m_s[...] = jnp.full_like(m_s, -jnp.inf)
        l_s[...] = jnp.zeros_like(l_s)
        acc_s[...] = jnp.zeros_like(acc_s)

        def chunk(kc, masked):
            off = pl.multiple_of(kc * tb, tb)
            scores = []
            for u in range(2):
                q = q_ref[:, q_w * u:q_w * (u + 1)]
                if fox:
                    kk = k_ref[pl.ds(off, tb), HEAD * u:HEAD * (u + 1)]
                else:
                    kk = jnp.concatenate([kv_ref[pl.ds(off, tb), pair * u:pair * u + HEAD],
                                          kr_ref[pl.ds(off, tb), :]], axis=1)
                s = lax.dot_general(kk, q, nt_dims, preferred_element_type=F32)
                if fox:
                    s = s + cq_ref[u, 0] - ck_ref[u, pl.ds(off, tb), :]
                if masked:
                    row = lax.broadcasted_iota(jnp.int32, (tb, tb), 0)
                    col = lax.broadcasted_iota(jnp.int32, (tb, tb), 1)
                    s = jnp.where(row <= col, s, -jnp.inf)
                scores.append(s)
            for u in range(2):
                s = scores[u]
                m_prev = m_s[u]
                m_new = jnp.maximum(m_prev, jnp.max(s, axis=0, keepdims=True))
                alpha = jnp.exp2((m_prev - m_new) * exp2_scale)
                p = jnp.exp2((s - m_new) * exp2_scale)
                l_s[u] = alpha * l_s[u] + jnp.sum(p, axis=0, keepdims=True)
                if fox:
                    vv = v_ref[pl.ds(off, tb), HEAD * u:HEAD * (u + 1)]
                else:
                    vv = kv_ref[pl.ds(off, tb), pair * u + HEAD:pair * (u + 1)]
                acc_s[u] = alpha * acc_s[u] + lax.dot_general(vv, p.astype(BF16), tn_dims,
                                                              preferred_element_type=F32)
                m_s[u] = m_new

        def loop_body(kc, carry):
            chunk(kc, False)
            return carry

        lax.fori_loop(0, qi, loop_body, 0)
        chunk(qi, True)
        for u in range(2):
            cols = slice(HEAD * u, HEAD * (u + 1))
            o = (acc_s[u] / l_s[u]).T
            o_ref[:, cols] = o
            g = gate_ref[:, cols].astype(F32)
            og_ref[:, cols] = (o * (g * _sigmoid(g))).astype(BF16)
            lse_ref[u, 0] = m_s[u] * scale + jnp.log(l_s[u])

    any_spec = pl.BlockSpec(memory_space=pl.ANY)
    row_stat = pl.BlockSpec((2, 1, 1, tb), lambda g, i: (g, i, 0, 0))
    if fox:
        proj, c_col, c_row4, o_all, og_all = operands
        ins = [proj, proj, proj, proj, c_row4, c_col, o_all, og_all]
        in_specs = [pl.BlockSpec((tb, pair), lambda g, i: (i, FQ0 // pair + g)),
                    pl.BlockSpec((t, pair), lambda g, i: (0, FK0 // pair + g)),
                    pl.BlockSpec((t, pair), lambda g, i: (0, FV0 // pair + g)),
                    pl.BlockSpec((tb, pair), lambda g, i: (i, GF0 // pair + g)),
                    row_stat, pl.BlockSpec((2, t, 1), lambda g, i: (g, 0, 0)), any_spec, any_spec]
        aliases = {6: 0, 7: 1}
    else:
        q_full, kv, kr, proj = operands
        ins = [q_full, kv, kr, proj]
        in_specs = [pl.BlockSpec((tb, 2 * pair), lambda g, i: (i, g)),
                    pl.BlockSpec((t, 2 * pair), lambda g, i: (0, g)),
                    pl.BlockSpec((t, HEAD), lambda g, i: (0, 0)),
                    pl.BlockSpec((tb, pair), lambda g, i: (i, GM0 // pair + g))]
        aliases = {}
    return pl.pallas_call(
        body, name=name, grid=(N_HEADS // 2, nb), in_specs=in_specs,
        out_specs=[pl.BlockSpec((tb, pair), lambda g, i: (i, pair0 + g)),
                   pl.BlockSpec((tb, pair), lambda g, i: (i, pair0 + g)), row_stat],
        out_shape=[jax.ShapeDtypeStruct((t, 2 * N_HEADS * HEAD), F32), jax.ShapeDtypeStruct((t, 2 * N_HEADS * HEAD), BF16),
                   jax.ShapeDtypeStruct((N_HEADS, nb, 1, tb), F32)],
        scratch_shapes=[pltpu.VMEM((2, 1, tb), F32), pltpu.VMEM((2, 1, tb), F32), pltpu.VMEM((2, HEAD, tb), F32)],
        input_output_aliases=aliases,
        compiler_params=_params(2),
    )(*ins)


def _postnorm_loss(o, x, target, g, *, tm):
    t = o.shape[0]

    def body(o_ref, x_ref, t_ref, g_ref, dy_ref, do_ref, dg_ref, loss_ref):
        i = pl.program_id(0)

        @pl.when(i == 0)
        def _():
            dg_ref[...] = jnp.zeros_like(dg_ref)
            loss_ref[...] = jnp.zeros_like(loss_ref)

        ov = o_ref[...]
        gv = g_ref[...]
        r = lax.rsqrt(jnp.mean(ov * ov, axis=-1, keepdims=True) + NORM_EPS)
        oh = ov * r
        e = x_ref[...] + oh * gv - t_ref[...]
        loss_ref[...] += 0.5 * jnp.sum(jnp.mean(e * e, axis=-1, keepdims=True), axis=0, keepdims=True)
        dy = e * (1.0 / D_MODEL)
        dy_ref[...] = dy
        dyg = dy * gv
        do_ref[...] = (r * (dyg - oh * jnp.mean(dyg * oh, axis=-1, keepdims=True))).astype(BF16)
        dg_ref[...] += jnp.sum(dy * oh, axis=0, keepdims=True)

    row = pl.BlockSpec((tm, D_MODEL), lambda i: (i, 0))
    vec = pl.BlockSpec((1, D_MODEL), lambda i: (0, 0))
    return pl.pallas_call(
        body, name="postnorm_loss", grid=(t // tm,),
        in_specs=[row, row, row, vec],
        out_specs=[row, row, vec, pl.BlockSpec((1, 1), lambda i: (0, 0))],
        out_shape=[jax.ShapeDtypeStruct((t, D_MODEL), F32), jax.ShapeDtypeStruct((t, D_MODEL), BF16),
                   jax.ShapeDtypeStruct((1, D_MODEL), F32), jax.ShapeDtypeStruct((1, 1), F32)],
        compiler_params=_params(1),
    )(o, x, target, g)


def _dog_gate(d_o_post, w_out_n, o_all, proj, *, tm):
    t = d_o_post.shape[0]
    pair = 2 * HEAD
    gate_blk = GM0 // pair
    assert GM0 % pair == 0 and GF0 == GM0 + N_HEADS * HEAD

    def body(do_ref, w_ref, o_ref, p_ref, dattn_ref, dproj_ref, delta_ref):
        j = pl.program_id(1)

        @pl.when(j == 0)
        def _():
            delta_ref[...] = jnp.zeros_like(delta_ref)

        dog = lax.dot_general(do_ref[...], w_ref[...], (((1,), (1,)), ((), ())), preferred_element_type=F32)
        g = p_ref[...].astype(F32)
        ov = o_ref[...]
        sg = _sigmoid(g)
        d_o = dog * (g * sg)
        dattn_ref[...] = d_o.astype(BF16)
        dproj_ref[...] = (dog * ov * (sg * (1.0 + g * (1.0 - sg)))).astype(BF16)
        prod = d_o * ov
        lane = lax.broadcasted_iota(jnp.int32, (tm, LANE), 1)
        delta = delta_ref[...]
        for half in range(2):
            part = jnp.sum(prod[:, HEAD * half:HEAD * (half + 1)], axis=-1, keepdims=True)
            delta = jnp.where(lane == 2 * j + half, part, delta)
        delta_ref[...] = delta

    return pl.pallas_call(
        body, name="dog_gate", grid=(t // tm, N_HEADS),
        in_specs=[pl.BlockSpec((tm, D_MODEL), lambda i, j: (i, 0)), pl.BlockSpec((pair, D_MODEL), lambda i, j: (j, 0)),
                  pl.BlockSpec((tm, pair), lambda i, j: (i, j)), pl.BlockSpec((tm, pair), lambda i, j: (i, gate_blk + j))],
        out_specs=[pl.BlockSpec((tm, pair), lambda i, j: (i, j)), pl.BlockSpec((tm, pair), lambda i, j: (i, gate_blk + j)),
                   pl.BlockSpec((tm, LANE), lambda i, j: (i, 0))],
        out_shape=[jax.ShapeDtypeStruct((t, 2048), BF16), jax.ShapeDtypeStruct((t, NP_IN), BF16),
                   jax.ShapeDtypeStruct((t, LANE), F32)],
        compiler_params=_params(2),
    )(d_o_post, w_out_n, o_all, proj)


def _attn_bwd(fox, operands, *, t, tb, name, rider=None):
    nb = t // tb
    scale = FOX_SCALE if fox else MLA_SCALE
    nt_dims = (((1,), (1,)), ((), ()))
    tn_dims = (((0,), (0,)), ((), ()))
    n_rin = len(rider.operands) if rider else 0
    n_rout = len(rider.out_shape) if rider else 0
    assert not (fox and rider)

    def body(*refs):
        if fox:
            (q_ref, k_ref, v_ref, do_ref, lse_ref, dl_ref, cq_ref, ck_ref, _,
             dproj_ref, dck_ref, dcq_ref, dq_s, dk_s, dv_s, dc_s, dcq_s, stage_q, stage_k, stage_v, put_sems) = refs
        else:
            q_ref, kn_ref, kr_ref, v_ref, do_ref, lse_ref, dl_ref = refs[:7]
            dq_ref, dkv_ref, dkr_ref = refs[7 + n_rin:10 + n_rin]
            dq_s, dk_s, dv_s = refs[10 + n_rin + n_rout:13 + n_rin + n_rout]
            rider_refs = (refs[7:7 + n_rin], refs[10 + n_rin:10 + n_rin + n_rout], refs[13 + n_rin + n_rout:])
        head = pl.program_id(0)
        ki = pl.program_id(1)
        if rider:
            @pl.when(jnp.logical_and(head == 0, ki == 0))
            def _():
                rider.start(*rider_refs)

        @pl.when(ki == 0)
        def _():
            dq_s[...] = jnp.zeros_like(dq_s)
            if fox:
                dcq_s[...] = jnp.zeros_like(dcq_s)

        dk_s[...] = jnp.zeros_like(dk_s)
        dv_s[...] = jnp.zeros_like(dv_s)
        if fox:
            dc_s[...] = jnp.zeros_like(dc_s)
            kk = k_ref[...]
        else:
            kk = jnp.concatenate([kn_ref[...], kr_ref[...]], axis=1)
        vv = v_ref[...]

        def chunk(qc, masked):
            off = pl.multiple_of(qc * tb, tb)
            qq = q_ref[pl.ds(off, tb), :]
            dd = do_ref[pl.ds(off, tb), :]
            s = lax.dot_general(kk, qq, nt_dims, preferred_element_type=F32)
            if fox:
                s = s + cq_ref[0, qc] - ck_ref[0]
            if masked:
                row = lax.broadcasted_iota(jnp.int32, (tb, tb), 0)
                col = lax.broadcasted_iota(jnp.int32, (tb, tb), 1)
                s = jnp.where(row <= col, s, -jnp.inf)
            p = jnp.exp2(s * (scale * LOG2E) - lse_ref[0, qc] * LOG2E)
            dv_s[...] += jnp.dot(p.astype(BF16), dd, preferred_element_type=F32)
            dp = lax.dot_general(vv, dd, nt_dims, preferred_element_type=F32)
            ds = p * (dp - dl_ref[0, qc])
            if fox:
                dc_s[...] += jnp.sum(ds, axis=1, keepdims=True)
                dcq_s[qc] += jnp.sum(ds, axis=0, keepdims=True)
            dsb = (ds * scale).astype(BF16)
            dk_s[...] += jnp.dot(dsb, qq, preferred_element_type=F32)
            dq_s[pl.ds(off, tb), :] += lax.dot_general(dsb, kk, tn_dims, preferred_element_type=F32)

        chunk(ki, True)

        def loop_body(qc, carry):
            chunk(qc, False)
            return carry

        lax.fori_loop(ki + 1, nb, loop_body, 0)

        def put(stage_ref, rows, seg0, sem):
            col0 = pl.multiple_of(seg0 + head * HEAD, HEAD)
            return pltpu.make_async_copy(stage_ref, dproj_ref.at[rows, pl.ds(col0, HEAD)], sem)

        if fox:
            rows = pl.ds(pl.multiple_of(ki * tb, tb), tb)
            block_puts = [put(stage_k, rows, FK0, put_sems.at[1]), put(stage_v, rows, FV0, put_sems.at[2])]
            head_put = put(stage_q, pl.ds(0, t), FQ0, put_sems.at[0])

            @pl.when(jnp.logical_or(head > 0, ki > 0))
            def _():
                for cp in block_puts:
                    cp.wait()

            stage_k[...] = dk_s[...].astype(BF16)
            stage_v[...] = dv_s[...].astype(BF16)
            for cp in block_puts:
                cp.start()
            dck_ref[0] = -dc_s[...]

            @pl.when(ki == nb - 1)
            def _():
                @pl.when(head > 0)
                def _():
                    head_put.wait()

                stage_q[...] = dq_s[...].astype(BF16)
                head_put.start()
                dcq_ref[0] = dcq_s[...]

            @pl.when(jnp.logical_and(head == N_HEADS - 1, ki == nb - 1))
            def _():
                for cp in block_puts + [head_put]:
                    cp.wait()
        else:
            dkv_ref[...] = jnp.concatenate([dk_s[:, :HEAD], dv_s[...]], axis=1).astype(BF16)
            dkr_ref[...] = dk_s[:, HEAD:]

            @pl.when(ki == nb - 1)
            def _():
                dq_ref[...] = dq_s[...]

        if rider:
            @pl.when(jnp.logical_and(head == N_HEADS - 1, ki == nb - 1))
            def _():
                rider.wait(*rider_refs)

    stat = pl.BlockSpec((1, nb, 1, tb), lambda h, i: (h, 0, 0, 0))
    aliases = {}
    if fox:
        proj, d_o, lse4, delta4, c_row4, c_col, dproj = operands
        ins = [proj, proj, proj, d_o, lse4, delta4, c_row4, c_col, dproj]
        any_spec = pl.BlockSpec(memory_space=pl.ANY)
        in_specs = [pl.BlockSpec((t, HEAD), lambda h, i: (0, FQ0 // HEAD + h)),
                    pl.BlockSpec((tb, HEAD), lambda h, i: (i, FK0 // HEAD + h)),
                    pl.BlockSpec((tb, HEAD), lambda h, i: (i, FV0 // HEAD + h)),
                    pl.BlockSpec((t, HEAD), lambda h, i: (0, N_HEADS + h)),
                    stat, stat, stat, pl.BlockSpec((1, tb, 1), lambda h, i: (h, i, 0)), any_spec]
        aliases = {8: 0}
        out_specs = [any_spec, pl.BlockSpec((1, tb, 1), lambda h, i: (h, i, 0)), stat]
        out_shape = [jax.ShapeDtypeStruct(dproj.shape, dproj.dtype), jax.ShapeDtypeStruct((N_HEADS, t, 1), F32),
                     jax.ShapeDtypeStruct((N_HEADS, nb, 1, tb), F32)]
        scratch = [pltpu.VMEM((t, HEAD), F32), pltpu.VMEM((tb, HEAD), F32), pltpu.VMEM((tb, HEAD), F32),
                   pltpu.VMEM((tb, 1), F32), pltpu.VMEM((nb, 1, tb), F32),
                   pltpu.VMEM((t, HEAD), BF16), pltpu.VMEM((tb, HEAD), BF16), pltpu.VMEM((tb, HEAD), BF16),
                   pltpu.SemaphoreType.DMA((3,))]
    else:
        q_full, kv, kr, d_o, lse4, delta4 = operands
        ins = [q_full, kv, kr, kv, d_o, lse4, delta4]
        in_specs = [pl.BlockSpec((t, 256), lambda h, i: (0, h)),
                    pl.BlockSpec((tb, HEAD), lambda h, i: (i, 2 * h)),
                    pl.BlockSpec((tb, HEAD), lambda h, i: (i, 0)),
                    pl.BlockSpec((tb, HEAD), lambda h, i: (i, 2 * h + 1)),
                    pl.BlockSpec((t, HEAD), lambda h, i: (0, h)),
                    stat, stat]
        out_specs = [pl.BlockSpec((t, 256), lambda h, i: (0, h)), pl.BlockSpec((tb, 256), lambda h, i: (i, h)),
                     pl.BlockSpec((tb, HEAD), lambda h, i: (i, h))]
        out_shape = [jax.ShapeDtypeStruct((t, 2048), F32), jax.ShapeDtypeStruct((t, 2048), BF16),
                     jax.ShapeDtypeStruct((t, 1024), F32)]
        scratch = [pltpu.VMEM((t, 256), F32), pltpu.VMEM((tb, 256), F32), pltpu.VMEM((tb, HEAD), F32)]
        if rider:
            any_spec = pl.BlockSpec(memory_space=pl.ANY)
            aliases = {len(ins) + i_in: len(out_specs) + i_out for i_in, i_out in rider.aliases.items()}
            ins = ins + list(rider.operands)
            in_specs = in_specs + [any_spec] * n_rin
            out_specs = out_specs + [any_spec] * n_rout
            out_shape = out_shape + list(rider.out_shape)
            scratch = scratch + list(rider.scratch)
    return pl.pallas_call(
        body, name=name, grid=(N_HEADS, nb), in_specs=in_specs, out_specs=out_specs, out_shape=out_shape,
        scratch_shapes=scratch, input_output_aliases=aliases, compiler_params=_params(2),
    )(*ins)


def _mid_bwd(dq_full, dkr, cos_t, sin_t, dck, flog, bf_row, *, tm):
    t = dq_full.shape[0]
    n = t // tm

    def body(dq_ref, dkr_ref, cos_ref, sin_ref, dck_ref, fl_ref, bf_ref,
             dq2_ref, dkraw_ref, dfl_ref, dbf_ref, carry_ref):
        i = pl.program_id(0)

        @pl.when(i == 0)
        def _():
            carry_ref[...] = jnp.zeros_like(carry_ref)
            dbf_ref[...] = jnp.zeros_like(dbf_ref)

        c, s = cos_ref[...], sin_ref[...]
        dkr_sum = jnp.zeros((tm, LANE), F32)
        for h in range(N_HEADS):
            dq2_ref[:, 256 * h:256 * h + 128] = dq_ref[:, 256 * h:256 * h + 128].astype(BF16)
            dq2_ref[:, 256 * h + 128:256 * h + 256] = _rot_bwd(dq_ref[:, 256 * h + 128:256 * h + 256], c, s).astype(BF16)
            dkr_sum = dkr_sum + dkr_ref[:, HEAD * h:HEAD * (h + 1)]
        dkraw_ref[...] = _rot_bwd(dkr_sum, c, s).astype(BF16)

        dc = dck_ref[...]
        row = lax.broadcasted_iota(jnp.int32, (tm, tm), 0)
        col = lax.broadcasted_iota(jnp.int32, (tm, tm), 1)
        tri = (col >= row).astype(BF16)
        acc = carry_ref[0:1, :]
        for part in _split3(dc):
            acc = acc + jnp.dot(tri, part, preferred_element_type=F32)
        carry_ref[0:1, :] = carry_ref[0:1, :] + jnp.sum(dc, axis=0, keepdims=True)
        z = fl_ref[...] + bf_ref[...]
        dz = acc / (1.0 + jnp.exp(z))
        dfl_ref[...] = dz.astype(BF16)
        dbf_ref[...] += jnp.sum(dz, axis=0, keepdims=True)

    rev = lambda w: pl.BlockSpec((tm, w), lambda i: (n - 1 - i, 0))
    vec = lambda w: pl.BlockSpec((1, w), lambda i: (0, 0))
    return pl.pallas_call(
        body, name="mid_bwd", grid=(n,),
        in_specs=[rev(2048), rev(1024), rev(LANE), rev(LANE), rev(LANE), rev(LANE), vec(LANE)],
        out_specs=[rev(2048), rev(LANE), rev(LANE), vec(LANE)],
        out_shape=[jax.ShapeDtypeStruct((t, 2048), BF16), jax.ShapeDtypeStruct((t, LANE), BF16),
                   jax.ShapeDtypeStruct((t, LANE), BF16), jax.ShapeDtypeStruct((1, LANE), F32)],
        scratch_shapes=[pltpu.VMEM((8, LANE), F32)],
        compiler_params=_params(1),
    )(dq_full, dkr, cos_t, sin_t, dck, flog, bf_row)


def _norm_bwd(proj, dqn, dkvn, g_q, g_kv, dkr_raw, dfl, dproj, *, tm):
    t = proj.shape[0]
    assert (KR0, FL0, KVL0, LAT_W) == (Q_RANK, Q_RANK + LANE, Q_RANK + 2 * LANE, Q_RANK + 2 * LANE + KV_RANK)

    def body(p_ref, dqn_ref, dkvn_ref, gq_ref, gkv_ref, dkr_ref, dfl_ref, _, dproj_ref, dgq_ref, dgkv_ref):
        i = pl.program_id(0)

        @pl.when(i == 0)
        def _():
            dgq_ref[...] = jnp.zeros_like(dgq_ref)
            dgkv_ref[...] = jnp.zeros_like(dgkv_ref)

        d_lat = []
        for lo, w, dn_ref, g_ref, dg_ref in ((QL0, Q_RANK, dqn_ref, gq_ref, dgq_ref),
                                             (KVL0, KV_RANK, dkvn_ref, gkv_ref, dgkv_ref)):
            xv = p_ref[:, lo:lo + w].astype(F32)
            r = lax.rsqrt(jnp.mean(xv * xv, axis=-1, keepdims=True) + NORM_EPS)
            xh = xv * r
            dn = dn_ref[...]
            dg_ref[...] += jnp.sum(dn * xh, axis=0, keepdims=True)
            dxh = dn * g_ref[...]
            d_lat.append((r * (dxh - xh * jnp.mean(dxh * xh, axis=-1, keepdims=True))).astype(BF16))
        dproj_ref[...] = jnp.concatenate([d_lat[0], dkr_ref[...], dfl_ref[...], d_lat[1]], axis=1)

    row = lambda w: pl.BlockSpec((tm, w), lambda i: (i, 0))
    vec = lambda w: pl.BlockSpec((1, w), lambda i: (0, 0))
    return pl.pallas_call(
        body, name="norm_bwd", grid=(t // tm,),
        in_specs=[row(LAT_W), row(Q_RANK), row(KV_RANK), vec(Q_RANK), vec(KV_RANK), row(LANE), row(LANE),
                  pl.BlockSpec(memory_space=pl.ANY)],
        out_specs=[row(LAT_W), vec(Q_RANK), vec(KV_RANK)],
        out_shape=[jax.ShapeDtypeStruct(dproj.shape, dproj.dtype),
                   jax.ShapeDtypeStruct((1, Q_RANK), F32), jax.ShapeDtypeStruct((1, KV_RANK), F32)],
        input_output_aliases={7: 0},
        compiler_params=_params(1),
    )(proj, dqn, dkvn, g_q, g_kv, dkr_raw, dfl, dproj)


def _prenorm_bwd(dh, x, g, dy, *, tm):
    t = x.shape[0]

    def body(dh_ref, x_ref, g_ref, dy_ref, gx_ref, dg_ref):
        i = pl.program_id(0)

        @pl.when(i == 0)
        def _():
            dg_ref[...] = jnp.zeros_like(dg_ref)

        xv = x_ref[...]
        r = lax.rsqrt(jnp.mean(xv * xv, axis=-1, keepdims=True) + NORM_EPS)
        xh = xv * r
        dn = dh_ref[...]
        dg_ref[...] += jnp.sum(dn * xh, axis=0, keepdims=True)
        dxh = dn * g_ref[...]
        gx_ref[...] = dy_ref[...] + r * (dxh - xh * jnp.mean(dxh * xh, axis=-1, keepdims=True))

    row = pl.BlockSpec((tm, D_MODEL), lambda i: (i, 0))
    vec = pl.BlockSpec((1, D_MODEL), lambda i: (0, 0))
    return pl.pallas_call(
        body, name="prenorm_bwd", grid=(t // tm,),
        in_specs=[row, row, vec, row], out_specs=[row, vec],
        out_shape=[jax.ShapeDtypeStruct((t, D_MODEL), F32), jax.ShapeDtypeStruct((1, D_MODEL), F32)],
        compiler_params=_params(1),
    )(dh, x, g, dy)


def _adam_math(w, g, m, v):
    m = ADAM_B1 * m + (1.0 - ADAM_B1) * g
    v = ADAM_B2 * v + (1.0 - ADAM_B2) * (g * g)
    m_hat = m / (1.0 - ADAM_B1 ** ADAM_STEP)
    v_hat = v / (1.0 - ADAM_B2 ** ADAM_STEP)
    delta = -ADAM_LR * (m_hat / (jnp.sqrt(v_hat) + ADAM_EPS) + ADAM_WD * w)
    return delta, m, v


def _adamw(land, w, m, v, *, tr, name):
    rows, cols = w.shape

    def body(l_ref, w_ref, m_ref, v_ref, g_ref, d_ref, nm_ref, nv_ref):
        g = l_ref[0].astype(F32)
        for s in range(1, N_DEV):
            g = g + l_ref[s].astype(F32)
        g_ref[...] = g
        d_ref[...], nm_ref[...], nv_ref[...] = _adam_math(w_ref[...], g, m_ref[...], v_ref[...])

    blk = pl.BlockSpec((tr, cols), lambda i: (i, 0))
    return pl.pallas_call(
        body, name=name, grid=(rows // tr,),
        in_specs=[pl.BlockSpec((N_DEV, tr, cols), lambda i: (0, i, 0)), blk, blk, blk],
        out_specs=[blk, blk, blk, blk],
        out_shape=[jax.ShapeDtypeStruct((rows, cols), F32)] * 4,
        compiler_params=_params(1),
    )(land, w, m, v)


def _adamw_small(gathered, w, m, v):
    def body(a_ref, w_ref, m_ref, v_ref, g_ref, d_ref, nm_ref, nv_ref):
        g = a_ref[0:SMALL_ROWS, :]
        for s in range(1, N_DEV):
            g = g + a_ref[SMALL_ROWS * s:SMALL_ROWS * (s + 1), :]
        g_ref[...] = g
        d_ref[...], nm_ref[...], nv_ref[...] = _adam_math(w_ref[...], g, m_ref[...], v_ref[...])

    return pl.pallas_call(
        body, name="adamw_small",
        out_shape=[jax.ShapeDtypeStruct((SMALL_ROWS, SMALL_COLS), F32)] * 4,
        compiler_params=_params(),
    )(gathered, w, m, v)


def _place():
    x, y, c = lax.axis_index("x"), lax.axis_index("y"), lax.axis_index("c")
    return x, y, c


def _flip(p, k):
    x, y, c = p
    return (1 - x if k & 4 else x, 1 - y if k & 2 else y, 1 - c if k & 1 else c)


def _index(p):
    return 4 * p[0] + 2 * p[1] + p[2]


def _all_gather(shards):
    n = len(shards)
    hbm = pl.BlockSpec(memory_space=pl.ANY)

    def body(*refs):
        ins, outs = refs[:n], refs[n:2 * n]
        send_sems, recv_sems, local_sems = refs[2 * n:]
        me = _place()
        sibling = _flip(me, 1)
        chips = [_flip(me, 4), _flip(me, 2), _flip(me, 6)]

        def copy(a, k, block, to, src=None):
            dst = outs[a].at[_index(block)]
            return pltpu.make_async_remote_copy(
                src_ref=dst if src is None else src, dst_ref=dst,
                send_sem=send_sems.at[7 * a + k], recv_sem=recv_sems.at[7 * a + k],
                device_id=to, device_id_type=MESH)

        started = []
        for a in range(n):
            mine = pltpu.make_async_copy(ins[a], outs[a].at[_index(me)], local_sems.at[a])
            mine.start()
            started.append(mine)
        first = []
        for a in range(n):
            first.append(copy(a, 0, me, sibling, src=ins[a]))
            first += [copy(a, 1 + j, me, chip, src=ins[a]) for j, chip in enumerate(chips)]
        for cp in first:
            cp.start()
        passed = []
        for a in range(n):
            for j, chip in enumerate(chips):
                copy(a, 1 + j, chip, me).wait_recv()
                fwd = copy(a, 4 + j, chip, sibling)
                fwd.start()
                passed.append(fwd)
        for a in range(n):
            copy(a, 0, sibling, me).wait_recv()
            for j, chip in enumerate(chips):
                copy(a, 4 + j, _flip(chip, 1), me).wait_recv()
        for cp in first + passed:
            cp.wait_send()
        for mine in started:
            mine.wait()

    return pl.pallas_call(
        body, name="all_gather_weights",
        in_specs=[hbm] * n, out_specs=[hbm] * n,
        out_shape=[jax.ShapeDtypeStruct((N_DEV,) + s.shape, s.dtype) for s in shards],
        scratch_shapes=[pltpu.SemaphoreType.DMA((7 * n,)), pltpu.SemaphoreType.DMA((7 * n,)),
                        pltpu.SemaphoreType.DMA((n,))],
    )(*shards)


class _Exchange:
    def __init__(self, tasks):
        self.tasks = tasks
        taken = [land for _, _, land, _ in tasks if land is not None]
        self.operands = [src for src, _, _, _ in tasks] + taken
        self.out_shape = [jax.ShapeDtypeStruct((N_DEV,) + (src.shape if same else src.shape[1:]), src.dtype)
                          for src, _, _, same in tasks]
        self.aliases, n_taken = {}, 0
        for a, (_, _, land, _) in enumerate(tasks):
            if land is not None:
                self.aliases[len(tasks) + n_taken] = a
                n_taken += 1
        self.scratch = [pltpu.SemaphoreType.DMA((N_DEV,)), pltpu.SemaphoreType.DMA((N_DEV,)),
                        pltpu.SemaphoreType.DMA(())] * len(tasks)

    def _copies(self, ins, outs, scratch):
        my = _index(_place())
        for a, (_, dests, _, same) in enumerate(self.tasks):
            send_sems, recv_sems, local_sem = scratch[3 * a:3 * a + 3]
            for i, d in enumerate(dests):
                src = ins[a] if same else ins[a].at[i]
                pair = jnp.bitwise_xor(my, d)
                remote = pltpu.make_async_remote_copy(
                    src_ref=src, dst_ref=outs[a].at[my], send_sem=send_sems.at[pair], recv_sem=recv_sems.at[pair],
                    device_id=((d >> 2) & 1, (d >> 1) & 1, d & 1), device_id_type=MESH)
                local = pltpu.make_async_copy(src, outs[a].at[my], local_sem)
                yield a, d, my, remote, local, (send_sems, recv_sems)

    def start(self, ins, outs, scratch):
        for _, d, my, remote, local, _ in self._copies(ins, outs, scratch):
            pl.when(my != d)(remote.start)
            pl.when(my == d)(local.start)

    def wait(self, ins, outs, scratch):
        for a, d, my, remote, local, (send_sems, recv_sems) in self._copies(ins, outs, scratch):
            pl.when(my != d)(remote.wait_send)

            @pl.when(my == d)
            def _():
                local.wait()
                for s in range(N_DEV):
                    if s != d:
                        slot = outs[a].at[s]
                        pltpu.make_async_remote_copy(
                            src_ref=slot, dst_ref=slot, send_sem=send_sems.at[d ^ s], recv_sem=recv_sems.at[d ^ s],
                            device_id=((s >> 2) & 1, (s >> 1) & 1, s & 1), device_id_type=MESH).wait_recv()


ALL_DEVICES = tuple(range(N_DEV))


def _gather_small(vec):
    def body(v_ref, out_ref, send_sems, recv_sems, local_sem):
        me = _place()

        def rows(p):
            return out_ref.at[pl.ds(pl.multiple_of(_index(p) * SMALL_ROWS, SMALL_ROWS), SMALL_ROWS), :]

        mine = pltpu.make_async_copy(v_ref, rows(me), local_sem)
        mine.start()
        sends = []
        for k in range(1, N_DEV):
            peer = _flip(me, k)
            cp = pltpu.make_async_remote_copy(src_ref=v_ref, dst_ref=rows(me), send_sem=send_sems.at[k - 1],
                                              recv_sem=recv_sems.at[k - 1], device_id=peer, device_id_type=MESH)
            cp.start()
            sends.append(cp)
        for k in range(1, N_DEV):
            peer = _flip(me, k)
            pltpu.make_async_remote_copy(src_ref=rows(peer), dst_ref=rows(peer), send_sem=send_sems.at[k - 1],
                                         recv_sem=recv_sems.at[k - 1], device_id=peer, device_id_type=MESH).wait_recv()
        for cp in sends:
            cp.wait_send()
        mine.wait()

    return pl.pallas_call(
        body, name="gather_small",
        in_specs=[pl.BlockSpec(memory_space=pltpu.VMEM)], out_specs=pl.BlockSpec(memory_space=pltpu.VMEM),
        out_shape=jax.ShapeDtypeStruct((N_DEV * SMALL_ROWS, SMALL_COLS), F32),
        scratch_shapes=[pltpu.SemaphoreType.DMA((7,)), pltpu.SemaphoreType.DMA((7,)), pltpu.SemaphoreType.DMA],
    )(vec)


def _w_in_nice(gathered):
    pieces, pos = [], 0
    for o0, width, n0 in sorted(_SEGMENTS, key=lambda seg: seg[2]):
        if n0 > pos:
            pieces.append(jnp.zeros((D_MODEL, n0 - pos), gathered.dtype))
        o = o0
        while o < o0 + width:
            d = o // SHARD_IN
            hi = min(o0 + width, (d + 1) * SHARD_IN)
            pieces.append(gathered[d][:, o - d * SHARD_IN:hi - d * SHARD_IN])
            o = hi
        pos = n0 + width
    pieces.append(jnp.zeros((D_MODEL, NP_IN - pos), gathered.dtype))
    return jnp.concatenate(pieces, axis=1)


def _w_in_blocks(dests, dw_lat, dw_rest):
    blocks = []
    for d in dests:
        lo = d * SHARD_IN
        runs = []
        for o0, width, n0 in _SEGMENTS:
            a, b = max(lo, o0), min(lo + SHARD_IN, o0 + width)
            if a < b:
                n_a, n_b = n0 + a - o0, n0 + b - o0
                runs.append(dw_lat[:, n_a:n_b] if n_b <= LAT_W else dw_rest[:, n_a - LAT_W:n_b - LAT_W])
        blocks.append(jnp.concatenate(runs, axis=1))
    return jnp.stack(blocks)


EARLY_DESTS = (2, 3, 4, 5)
LATE_DESTS = (0, 1, 6, 7)


def _w_uq_nice(shard):
    z = jnp.zeros((Q_RANK, 32), shard.dtype)
    return jnp.concatenate([shard[:, :128], shard[:, 128:160], z, shard[:, 160:192], z], axis=1)


def _pack_small(g_pre, g_post, g_q, g_kv, b_f, extra=None):
    parts = [g_pre.reshape(-1), g_post.reshape(-1), g_q.reshape(-1), g_kv.reshape(-1), b_f.reshape(-1)]
    if extra is not None:
        parts.append(extra.reshape(-1))
    flat = jnp.concatenate(parts)
    flat = jnp.concatenate([flat, jnp.zeros((SMALL_ROWS * SMALL_COLS - flat.shape[0],), F32)])
    return flat.reshape(SMALL_ROWS, SMALL_COLS)


def _unpack_small(packed):
    flat = packed.reshape(-1)
    o = 0
    out = []
    for n in (D_MODEL, D_MODEL, Q_RANK, KV_RANK, N_HEADS):
        out.append(flat[o:o + n].reshape(1, n))
        o += n
    return out, flat[o]


def kernel(x, positions, g_pre, w_in, g_q_latent, w_uq, g_kv_latent, w_ukv, b_forget, w_out, g_post, loss_target, m_g_pre, m_w_in, m_g_q_latent, m_w_uq, m_g_kv_latent, m_w_ukv, m_b_forget, m_w_out, m_g_post, v_g_pre, v_w_in, v_g_q_latent, v_w_uq, v_g_kv_latent, v_w_ukv, v_b_forget, v_w_out, v_g_post):
    t = x.shape[1]
    tb = min(512, t)
    tm = min(256, t)
    nb = t // tb
    x2 = x.reshape(t, D_MODEL)
    target = loss_target.reshape(t, D_MODEL)
    pos_col = positions.reshape(t, 1).astype(F32)
    bf_row = jnp.concatenate([b_forget.reshape(1, N_HEADS), jnp.zeros((1, LANE - N_HEADS), F32)], axis=1)

    (g_in,) = _all_gather([w_in[0].astype(BF16)])
    w_in_n = _w_in_nice(g_in)
    gather_rest = _Exchange([(w, ALL_DEVICES, None, True) for w in
                             (_w_uq_nice(w_uq[0].astype(BF16)), w_ukv[0].astype(BF16), w_out[0].astype(BF16))])

    h = _prenorm(x2, g_pre, tm=tm)
    proj, g_uq, g_ukv, g_out = _mm(h, w_in_n, name="proj_in", out_dtype=BF16, tm=1024, tn=512, tk=2048,
                                   rider=gather_rest)
    w_uq_n = g_uq.transpose(1, 0, 2).reshape(Q_RANK, N_HEADS * 256)
    w_ukv_n = g_ukv.transpose(1, 0, 2).reshape(KV_RANK, N_HEADS * 256)
    w_out_n = g_out.reshape(D_MODEL, D_MODEL)
    flog = _mm(h, w_in_n[:, FL0:FL0 + LANE], name="proj_flog", out_dtype=F32, tm=1024, tn=LANE, tk=2048)
    qn, kvn, kr, cos_t, sin_t, c = _mid_fwd(proj, flog, g_q_latent, g_kv_latent, bf_row, pos_col, tm=tm)
    q_raw = _mm(qn, w_uq_n, name="q_up", out_dtype=F32, tm=1024, tn=512, tk=Q_RANK)
    q_full = _rope_q(q_raw, cos_t, sin_t, tm=tm)
    kv = _mm(kvn, w_ukv_n, name="kv_up", out_dtype=BF16, tm=1024, tn=512, tk=KV_RANK)
    c_heads = c[:, :N_HEADS].T
    c_col = c_heads.reshape(N_HEADS, t, 1)
    c_row4 = c_heads.reshape(N_HEADS, nb, 1, tb)
    o_all, og_all, lse4_mla = _attn_fwd(False, (q_full, kv, kr, proj), t=t, tb=tb, name="mla_fwd")
    o_all, og_all, lse4_fox = _attn_fwd(True, (proj, c_col, c_row4, o_all, og_all), t=t, tb=tb, name="fox_fwd")
    o = _mm(og_all, w_out_n, name="out_proj", out_dtype=F32, tm=1024, tn=512, tk=2048)
    dy, d_o_post, dg_post, loss_part = _postnorm_loss(o, x2, target, g_post, tm=tm)

    dw_out = _mm(og_all, d_o_post, name="dw_out", ta=True, out_dtype=BF16, tm=1024, tn=1024, tk=512)
    d_attn, dproj, delta = _dog_gate(d_o_post, w_out_n, o_all, proj, tm=min(1024, t))
    delta4 = delta[:, :2 * N_HEADS].T.reshape(2 * N_HEADS, nb, 1, tb)
    dproj, dck, dcq = _attn_bwd(True, (proj, d_attn, lse4_fox, delta4[N_HEADS:], c_row4, c_col, dproj),
                                t=t, tb=tb, name="fox_bwd")
    dw_in_rest = _mm(h, dproj, name="dw_in_rest", ta=True, out_dtype=BF16, tm=2048, tn=512, tk=512,
                     b_cols=(LAT_W, NP_IN - LAT_W))
    early = _Exchange([(dw_out.reshape(N_DEV, D_MODEL // N_DEV, D_MODEL), ALL_DEVICES, None, False),
                       (_w_in_blocks(EARLY_DESTS, None, dw_in_rest), EARLY_DESTS, None, False)])
    dq_full, dkv, dkr, l_out, l_in = _attn_bwd(False, (q_full, kv, kr, d_attn, lse4_mla, delta4[:N_HEADS]),
                                               t=t, tb=tb, name="mla_bwd", rider=early)
    dc_heads = dck.reshape(N_HEADS, t) + dcq.reshape(N_HEADS, t)
    dck_rows = jnp.concatenate([dc_heads.T, jnp.zeros((t, LANE - N_HEADS), F32)], axis=1)
    dq2, dkr_raw, dfl, dbf = _mid_bwd(dq_full, dkr, cos_t, sin_t, dck_rows, flog, bf_row, tm=tm)
    dqn = _mm(dq2, w_uq_n, name="d_qn", nt=True, out_dtype=F32, tm=1024, tn=Q_RANK, tk=2048)
    dkvn = _mm(dkv, w_ukv_n, name="d_kvn", nt=True, out_dtype=F32, tm=1024, tn=KV_RANK, tk=2048)
    dw_uq = _mm(qn, dq2, name="dw_uq", ta=True, out_dtype=BF16, tm=Q_RANK, tn=1024, tk=512)
    dw_ukv = _mm(kvn, dkv, name="dw_ukv", ta=True, out_dtype=BF16, tm=KV_RANK, tn=1024, tk=512)
    dproj, dg_q, dg_kv = _norm_bwd(proj, dqn, dkvn, g_q_latent, g_kv_latent, dkr_raw, dfl, dproj, tm=tm)
    dw_in_lat = _mm(h, dproj, name="dw_in_lat", ta=True, out_dtype=BF16, tm=1024, tn=LAT_W, tk=512, b_cols=(0, LAT_W))
    dw_uq_h = dw_uq.reshape(Q_RANK, N_HEADS, 256)
    s_uq = jnp.concatenate([dw_uq_h[:, :, :160], dw_uq_h[:, :, 192:224]], axis=2).transpose(1, 0, 2)
    s_ukv = dw_ukv.reshape(KV_RANK, N_HEADS, 256).transpose(1, 0, 2)
    late = _Exchange([(s_uq, ALL_DEVICES, None, False), (s_ukv, ALL_DEVICES, None, False),
                      (_w_in_blocks(LATE_DESTS, dw_in_lat, dw_in_rest), LATE_DESTS, l_in, False)])
    dh, l_uq, l_ukv, l_in = _mm(dproj, w_in_n, name="d_h", nt=True, out_dtype=F32, tm=1024, tn=512, tk=NP_IN // 4,
                                rider=late)
    grad_x, dg_pre = _prenorm_bwd(dh, x2, g_pre, dy, tm=tm)

    small = _gather_small(_pack_small(dg_pre, dg_post, dg_q, dg_kv, dbf[:, :N_HEADS], loss_part))

    res_in = _adamw(l_in, w_in[0], m_w_in[0], v_w_in[0], tr=256, name="adamw_w_in")
    res_uq = _adamw(l_uq, w_uq[0], m_w_uq[0], v_w_uq[0], tr=256, name="adamw_w_uq")
    res_ukv = _adamw(l_ukv, w_ukv[0], m_w_ukv[0], v_w_ukv[0], tr=256, name="adamw_w_ukv")
    res_out = _adamw(l_out, w_out[0], m_w_out[0], v_w_out[0], tr=128, name="adamw_w_out")
    res_small = _adamw_small(
        small,
        _pack_small(g_pre, g_post, g_q_latent, g_kv_latent, b_forget),
        _pack_small(m_g_pre, m_g_post, m_g_q_latent, m_g_kv_latent, m_b_forget),
        _pack_small(v_g_pre, v_g_post, v_g_q_latent, v_g_kv_latent, v_b_forget))
    small_out = [_unpack_small(r) for r in res_small]
    loss = small_out[0][1]

    def leaves(kind):
        (s_pre, s_post, s_q, s_kv, s_bf), _ = small_out[kind]
        return [s_pre, res_in[kind][None], s_q, res_uq[kind][None], s_kv, res_ukv[kind][None], s_bf,
                res_out[kind][None], s_post]

    return (loss, grad_x.reshape(x.shape), *leaves(0), *leaves(1), *leaves(2), *leaves(3))
```

```python
import functools

import numpy as np
import jax
import jax.numpy as jnp
from jax import lax
from jax.experimental import pallas as pl
from jax.experimental.pallas import tpu as pltpu

F32 = jnp.float32
BF16 = jnp.bfloat16
MESH = pl.DeviceIdType.MESH

N_DEV = 8
D_MODEL = 2048
N_HEADS = 8
HEAD = 128
Q_RANK = 768
KV_RANK = 512
ROPE = 64
D_IN = 6472
SHARD_IN = D_IN // N_DEV
NORM_EPS = 1e-6
ROPE_THETA = 10000.0
MLA_SCALE = (HEAD + ROPE) ** -0.5
FOX_SCALE = HEAD ** -0.5

QL0, KR0, FL0, KVL0, GM0, GF0, FQ0, FK0, FV0, NP_IN = 0, 768, 896, 1024, 1536, 2560, 3584, 4608, 5632, 6656
LAT_W = GM0
LANE = 128
_SEGMENTS = ((0, 768, QL0), (768, 512, KVL0), (1280, 32, KR0), (1312, 32, KR0 + 64), (1344, 1024, GM0),
             (2368, 3072, FQ0), (5440, 8, FL0), (5448, 1024, GF0))
LOG2E = 1.4426950408889634

ADAM_LR = 0.001
ADAM_B1 = 0.9
ADAM_B2 = 0.999
ADAM_EPS = 1e-08
ADAM_WD = 0.01
ADAM_STEP = 10

VMEM_LIMIT_BYTES = 56 * 1024 * 1024
SMALL_ROWS, SMALL_COLS = 8, 768


def _params(n_grid=0):
    return pltpu.CompilerParams(vmem_limit_bytes=VMEM_LIMIT_BYTES,
                                dimension_semantics=("arbitrary",) * n_grid if n_grid else None)


def _sigmoid(z):
    return 1.0 / (1.0 + jnp.exp(-z))


def _split3(v):
    a = v.astype(BF16)
    r = v - a.astype(F32)
    b = r.astype(BF16)
    c = (r - b.astype(F32)).astype(BF16)
    return a, b, c


def _mm(a, b, *, name, nt=False, ta=False, out_dtype=F32, tm=1024, tn=512, tk=2048, b_cols=None, rider=None):
    assert not (nt and ta)
    k_dim, m = a.shape if ta else a.shape[::-1]
    n = b.shape[0] if nt else b.shape[1]
    col0 = 0
    if b_cols is not None:
        assert not nt
        col0, n = b_cols
    assert (b.shape[1] if nt else b.shape[0]) == k_dim
    tm, tn, tk = min(tm, m), min(tn, n), min(tk, k_dim)
    assert m % tm == 0 and n % tn == 0 and k_dim % tk == 0 and col0 % tn == 0, (name, a.shape, b.shape)
    nk = k_dim // tk
    j0 = col0 // tn
    grid = (m // tm, n // tn, nk)
    dims = (((0 if ta else 1,), (1 if nt else 0,)), ((), ()))
    n_rin = len(rider.operands) if rider else 0
    n_rout = len(rider.out_shape) if rider else 0

    def body(*refs):
        a_ref, b_ref = refs[:2]
        o_ref = refs[2 + n_rin]
        acc_ref = refs[3 + n_rin + n_rout]
        i, j, k = pl.program_id(0), pl.program_id(1), pl.program_id(2)
        if rider:
            rider_refs = (refs[2:2 + n_rin], refs[3 + n_rin:3 + n_rin + n_rout], refs[4 + n_rin + n_rout:])

            @pl.when(jnp.logical_and(i == 0, jnp.logical_and(j == 0, k == 0)))
            def _():
                rider.start(*rider_refs)

        @pl.when(k == 0)
        def _():
            acc_ref[...] = jnp.zeros_like(acc_ref)

        acc_ref[...] += lax.dot_general(a_ref[...], b_ref[...], dims, preferred_element_type=F32)

        @pl.when(k == nk - 1)
        def _():
            o_ref[...] = acc_ref[...].astype(o_ref.dtype)

        if rider:
            @pl.when(jnp.logical_and(i == grid[0] - 1, jnp.logical_and(j == grid[1] - 1, k == nk - 1)))
            def _():
                rider.wait(*rider_refs)

    b_spec = (pl.BlockSpec((tn, tk), lambda i, j, k: (j, k)) if nt
              else pl.BlockSpec((tk, tn), lambda i, j, k: (k, j0 + j)))
    a_spec = (pl.BlockSpec((tk, tm), lambda i, j, k: (k, i)) if ta
              else pl.BlockSpec((tm, tk), lambda i, j, k: (i, k)))
    any_spec = pl.BlockSpec(memory_space=pl.ANY)
    out = pl.pallas_call(
        body, name=name, grid=grid,
        in_specs=[a_spec, b_spec] + [any_spec] * n_rin,
        out_specs=[pl.BlockSpec((tm, tn), lambda i, j, k: (i, j))] + [any_spec] * n_rout,
        out_shape=[jax.ShapeDtypeStruct((m, n), out_dtype)] + (list(rider.out_shape) if rider else []),
        scratch_shapes=[pltpu.VMEM((tm, tn), F32)] + (list(rider.scratch) if rider else []),
        input_output_aliases={2 + i_in: 1 + i_out for i_in, i_out in rider.aliases.items()} if rider else {},
        compiler_params=_params(3),
    )(a, b, *(rider.operands if rider else ()))
    return out if rider else out[0]


def _prenorm(x, g, *, tm):
    t = x.shape[0]

    def body(x_ref, g_ref, h_ref):
        xv = x_ref[...]
        r = lax.rsqrt(jnp.mean(xv * xv, axis=-1, keepdims=True) + NORM_EPS)
        h_ref[...] = (xv * r * g_ref[...]).astype(BF16)

    return pl.pallas_call(
        body, name="prenorm", grid=(t // tm,),
        in_specs=[pl.BlockSpec((tm, D_MODEL), lambda i: (i, 0)), pl.BlockSpec((1, D_MODEL), lambda i: (0, 0))],
        out_specs=pl.BlockSpec((tm, D_MODEL), lambda i: (i, 0)),
        out_shape=jax.ShapeDtypeStruct((t, D_MODEL), BF16),
        compiler_params=_params(1),
    )(x, g)


def _rope_rows():
    inv = (np.float32(ROPE_THETA) ** (-np.arange(0, ROPE, 2, dtype=np.float32) / np.float32(ROPE))).astype(np.float32)
    invf = np.zeros((1, LANE), np.float32)
    sgn = np.zeros((1, LANE), np.float32)
    invf[0, 0:32] = inv
    invf[0, 64:96] = inv
    sgn[0, 0:32] = -1.0
    sgn[0, 64:96] = 1.0
    return jnp.asarray(invf), jnp.asarray(sgn)


def _rot(v, cos_t, sin_t):
    return v * cos_t + pltpu.roll(v, 64, 1) * sin_t


def _rot_bwd(dv, cos_t, sin_t):
    return dv * cos_t + pltpu.roll(dv * sin_t, 64, 1)


def _mid_fwd(proj, flog, g_q, g_kv, bf_row, pos_col, *, tm):
    t = proj.shape[0]
    invf, sgn = _rope_rows()

    def body(p_ref, fl_ref, gq_ref, gkv_ref, bf_ref, pos_ref, invf_ref, sgn_ref,
             qn_ref, kvn_ref, kr_ref, cos_ref, sin_ref, c_ref, carry_ref):
        i = pl.program_id(0)

        @pl.when(i == 0)
        def _():
            carry_ref[...] = jnp.zeros_like(carry_ref)

        ql = p_ref[:, QL0:QL0 + Q_RANK].astype(F32)
        r = lax.rsqrt(jnp.mean(ql * ql, axis=-1, keepdims=True) + NORM_EPS)
        qn_ref[...] = (ql * r * gq_ref[...]).astype(BF16)
        kvl = p_ref[:, KVL0:KVL0 + KV_RANK].astype(F32)
        r = lax.rsqrt(jnp.mean(kvl * kvl, axis=-1, keepdims=True) + NORM_EPS)
        kvn_ref[...] = (kvl * r * gkv_ref[...]).astype(BF16)

        ang = pos_ref[...] * invf_ref[...]
        cos_t = jnp.cos(ang)
        sin_t = jnp.sin(ang) * sgn_ref[...]
        cos_ref[...] = cos_t
        sin_ref[...] = sin_t
        kr_ref[...] = _rot(p_ref[:, KR0:KR0 + LANE].astype(F32), cos_t, sin_t).astype(BF16)

        z = fl_ref[...] + bf_ref[...]
        logf = jnp.minimum(z, 0.0) - jnp.log(1.0 + jnp.exp(-jnp.abs(z)))
        row = lax.broadcasted_iota(jnp.int32, (tm, tm), 0)
        col = lax.broadcasted_iota(jnp.int32, (tm, tm), 1)
        tri = (col <= row).astype(BF16)
        acc = carry_ref[0:1, :]
        for part in _split3(logf):
            acc = acc + jnp.dot(tri, part, preferred_element_type=F32)
        c_ref[...] = acc * (1.0 / FOX_SCALE)
        carry_ref[0:1, :] = carry_ref[0:1, :] + jnp.sum(logf, axis=0, keepdims=True)

    row_spec = lambda w: pl.BlockSpec((tm, w), lambda i: (i, 0))
    vec_spec = lambda w: pl.BlockSpec((1, w), lambda i: (0, 0))
    return pl.pallas_call(
        body, name="mid_fwd", grid=(t // tm,),
        in_specs=[row_spec(LAT_W), row_spec(LANE), vec_spec(Q_RANK), vec_spec(KV_RANK), vec_spec(LANE),
                  pl.BlockSpec((tm, 1), lambda i: (i, 0)), vec_spec(LANE), vec_spec(LANE)],
        out_specs=[row_spec(Q_RANK), row_spec(KV_RANK), row_spec(LANE), row_spec(LANE), row_spec(LANE), row_spec(LANE)],
        out_shape=[jax.ShapeDtypeStruct((t, Q_RANK), BF16), jax.ShapeDtypeStruct((t, KV_RANK), BF16),
                   jax.ShapeDtypeStruct((t, LANE), BF16), jax.ShapeDtypeStruct((t, LANE), F32),
                   jax.ShapeDtypeStruct((t, LANE), F32), jax.ShapeDtypeStruct((t, LANE), F32)],
        scratch_shapes=[pltpu.VMEM((8, LANE), F32)],
        compiler_params=_params(1),
    )(proj, flog, g_q, g_kv, bf_row, pos_col, invf, sgn)


def _rope_q(q_raw, cos_t, sin_t, *, tm):
    t = q_raw.shape[0]

    def body(q_ref, cos_ref, sin_ref, o_ref):
        c, s = cos_ref[...], sin_ref[...]
        for h in range(N_HEADS):
            o_ref[:, 256 * h:256 * h + 128] = q_ref[:, 256 * h:256 * h + 128].astype(BF16)
            o_ref[:, 256 * h + 128:256 * h + 256] = _rot(q_ref[:, 256 * h + 128:256 * h + 256], c, s).astype(BF16)

    return pl.pallas_call(
        body, name="rope_q", grid=(t // tm,),
        in_specs=[pl.BlockSpec((tm, 2048), lambda i: (i, 0)), pl.BlockSpec((tm, LANE), lambda i: (i, 0)),
                  pl.BlockSpec((tm, LANE), lambda i: (i, 0))],
        out_specs=pl.BlockSpec((tm, 2048), lambda i: (i, 0)),
        out_shape=jax.ShapeDtypeStruct((t, 2048), BF16),
        compiler_params=_params(1),
    )(q_raw, cos_t, sin_t)


def _attn_fwd(fox, operands, *, t, tb, name):
    nb = t // tb
    scale = FOX_SCALE if fox else MLA_SCALE
    exp2_scale = scale * LOG2E
    pair = 2 * HEAD
    pair0 = N_HEADS // 2 if fox else 0
    q_w = HEAD if fox else 2 * HEAD
    nt_dims = (((1,), (1,)), ((), ()))
    tn_dims = (((0,), (0,)), ((), ()))

    def body(*refs):
        if fox:
            (q_ref, k_ref, v_ref, gate_ref, cq_ref, ck_ref, _, _,
             o_ref, og_ref, lse_ref, m_s, l_s, acc_s) = refs
        else:
            q_ref, kv_ref, kr_ref, gate_ref, o_ref, og_ref, lse_ref, m_s, l_s, acc_s = refs
        qi = pl.program_id(1)
        m_s[...] = jnp.full_like(m_s, -jnp.inf)
        l_s[...] = jnp.zeros_like(l_s)
        acc_s[...] = jnp.zeros_like(acc_s)

        def chunk(kc, masked):
            off = pl.multiple_of(kc * tb, tb)
            scores = []
            for u in range(2):
                q = q_ref[:, q_w * u:q_w * (u + 1)]
                if fox:
                    kk = k_ref[pl.ds(off, tb), HEAD * u:HEAD * (u + 1)]
                else:
                    kk = jnp.concatenate([kv_ref[pl.ds(off, tb), pair * u:pair * u + HEAD],
                                          kr_ref[pl.ds(off, tb), :]], axis=1)
                s = lax.dot_general(kk, q, nt_dims, preferred_element_type=F32)
                if fox:
                    s = s + cq_ref[u, 0] - ck_ref[u, pl.ds(off, tb), :]
                if masked:
                    row = lax.broadcasted_iota(jnp.int32, (tb, tb), 0)
                    col = lax.broadcasted_iota(jnp.int32, (tb, tb), 1)
                    s = jnp.where(row <= col, s, -jnp.inf)
                scores.append(s)
            for u in range(2):
                s = scores[u]
                m_prev = m_s[u]
                m_new = jnp.maximum(m_prev, jnp.max(s, axis=0, keepdims=True))
                alpha = jnp.exp2((m_prev - m_new) * exp2_scale)
                p = jnp.exp2((s - m_new) * exp2_scale)
                l_s[u] = alpha * l_s[u] + jnp.sum(p, axis=0, keepdims=True)
                if fox:
                    vv = v_ref[pl.ds(off, tb), HEAD * u:HEAD * (u + 1)]
                else:
                    vv = kv_ref[pl.ds(off, tb), pair * u + HEAD:pair * (u + 1)]
                acc_s[u] = alpha * acc_s[u] + lax.dot_general(vv, p.astype(BF16), tn_dims,
                                                              preferred_element_type=F32)
                m_s[u] = m_new

        def loop_body(kc, carry):
            chunk(kc, False)
            return carry

        lax.fori_loop(0, qi, loop_body, 0)
        chunk(qi, True)
        for u in range(2):
            cols = slice(HEAD * u, HEAD * (u + 1))
            o = (acc_s[u] / l_s[u]).T
            o_ref[:, cols] = o
            g = gate_ref[:, cols].astype(F32)
            og_ref[:, cols] = (o * (g * _sigmoid(g))).astype(BF16)
            lse_ref[u, 0] = m_s[u] * scale + jnp.log(l_s[u])

    any_spec = pl.BlockSpec(memory_space=pl.ANY)
    row_stat = pl.BlockSpec((2, 1, 1, tb), lambda g, i: (g, i, 0, 0))
    if fox:
        proj, c_col, c_row4, o_all, og_all = operands
        ins = [proj, proj, proj, proj, c_row4, c_col, o_all, og_all]
        in_specs = [pl.BlockSpec((tb, pair), lambda g, i: (i, FQ0 // pair + g)),
                    pl.BlockSpec((t, pair), lambda g, i: (0, FK0 // pair + g)),
                    pl.BlockSpec((t, pair), lambda g, i: (0, FV0 // pair + g)),
                    pl.BlockSpec((tb, pair), lambda g, i: (i, GF0 // pair + g)),
                    row_stat, pl.BlockSpec((2, t, 1), lambda g, i: (g, 0, 0)), any_spec, any_spec]
        aliases = {6: 0, 7: 1}
    else:
        q_full, kv, kr, proj = operands
        ins = [q_full, kv, kr, proj]
        in_specs = [pl.BlockSpec((tb, 2 * pair), lambda g, i: (i, g)),
                    pl.BlockSpec((t, 2 * pair), lambda g, i: (0, g)),
                    pl.BlockSpec((t, HEAD), lambda g, i: (0, 0)),
                    pl.BlockSpec((tb, pair), lambda g, i: (i, GM0 // pair + g))]
        aliases = {}
    return pl.pallas_call(
        body, name=name, grid=(N_HEADS // 2, nb), in_specs=in_specs,
        out_specs=[pl.BlockSpec((tb, pair), lambda g, i: (i, pair0 + g)),
                   pl.BlockSpec((tb, pair), lambda g, i: (i, pair0 + g)), row_stat],
        out_shape=[jax.ShapeDtypeStruct((t, 2 * N_HEADS * HEAD), F32), jax.ShapeDtypeStruct((t, 2 * N_HEADS * HEAD), BF16),
                   jax.ShapeDtypeStruct((N_HEADS, nb, 1, tb), F32)],
        scratch_shapes=[pltpu.VMEM((2, 1, tb), F32), pltpu.VMEM((2, 1, tb), F32), pltpu.VMEM((2, HEAD, tb), F32)],
        input_output_aliases=aliases,
        compiler_params=_params(2),
    )(*ins)


def _postnorm_loss(o, x, target, g, *, tm):
    t = o.shape[0]

    def body(o_ref, x_ref, t_ref, g_ref, dy_ref, do_ref, dg_ref, loss_ref):
        i = pl.program_id(0)

        @pl.when(i == 0)
        def _():
            dg_ref[...] = jnp.zeros_like(dg_ref)
            loss_ref[...] = jnp.zeros_like(loss_ref)

        ov = o_ref[...]
        gv = g_ref[...]
        r = lax.rsqrt(jnp.mean(ov * ov, axis=-1, keepdims=True) + NORM_EPS)
        oh = ov * r
        e = x_ref[...] + oh * gv - t_ref[...]
        loss_ref[...] += 0.5 * jnp.sum(jnp.mean(e * e, axis=-1, keepdims=True), axis=0, keepdims=True)
        dy = e * (1.0 / D_MODEL)
        dy_ref[...] = dy
        dyg = dy * gv
        do_ref[...] = (r * (dyg - oh * jnp.mean(dyg * oh, axis=-1, keepdims=True))).astype(BF16)
        dg_ref[...] += jnp.sum(dy * oh, axis=0, keepdims=True)

    row = pl.BlockSpec((tm, D_MODEL), lambda i: (i, 0))
    vec = pl.BlockSpec((1, D_MODEL), lambda i: (0, 0))
    return pl.pallas_call(
        body, name="postnorm_loss", grid=(t // tm,),
        in_specs=[row, row, row, vec],
        out_specs=[row, row, vec, pl.BlockSpec((1, 1), lambda i: (0, 0))],
        out_shape=[jax.ShapeDtypeStruct((t, D_MODEL), F32), jax.ShapeDtypeStruct((t, D_MODEL), BF16),
                   jax.ShapeDtypeStruct((1, D_MODEL), F32), jax.ShapeDtypeStruct((1, 1), F32)],
        compiler_params=_params(1),
    )(o, x, target, g)


def _dog_gate(d_o_post, w_out_n, o_all, proj, *, tm):
    t = d_o_post.shape[0]
    pair = 2 * HEAD
    gate_blk = GM0 // pair
    assert GM0 % pair == 0 and GF0 == GM0 + N_HEADS * HEAD

    def body(do_ref, w_ref, o_ref, p_ref, dattn_ref, dproj_ref, delta_ref):
        j = pl.program_id(1)

        @pl.when(j == 0)
        def _():
            delta_ref[...] = jnp.zeros_like(delta_ref)

        dog = lax.dot_general(do_ref[...], w_ref[...], (((1,), (1,)), ((), ())), preferred_element_type=F32)
        g = p_ref[...].astype(F32)
        ov = o_ref[...]
        sg = _sigmoid(g)
        d_o = dog * (g * sg)
        dattn_ref[...] = d_o.astype(BF16)
        dproj_ref[...] = (dog * ov * (sg * (1.0 + g * (1.0 - sg)))).astype(BF16)
        prod = d_o * ov
        lane = lax.broadcasted_iota(jnp.int32, (tm, LANE), 1)
        delta = delta_ref[...]
        for half in range(2):
            part = jnp.sum(prod[:, HEAD * half:HEAD * (half + 1)], axis=-1, keepdims=True)
            delta = jnp.where(lane == 2 * j + half, part, delta)
        delta_ref[...] = delta

    return pl.pallas_call(
        body, name="dog_gate", grid=(t // tm, N_HEADS),
        in_specs=[pl.BlockSpec((tm, D_MODEL), lambda i, j: (i, 0)), pl.BlockSpec((pair, D_MODEL), lambda i, j: (j, 0)),
                  pl.BlockSpec((tm, pair), lambda i, j: (i, j)), pl.BlockSpec((tm, pair), lambda i, j: (i, gate_blk + j))],
        out_specs=[pl.BlockSpec((tm, pair), lambda i, j: (i, j)), pl.BlockSpec((tm, pair), lambda i, j: (i, gate_blk + j)),
                   pl.BlockSpec((tm, LANE), lambda i, j: (i, 0))],
        out_shape=[jax.ShapeDtypeStruct((t, 2048), BF16), jax.ShapeDtypeStruct((t, NP_IN), BF16),
                   jax.ShapeDtypeStruct((t, LANE), F32)],
        compiler_params=_params(2),
    )(d_o_post, w_out_n, o_all, proj)


def _attn_bwd(fox, operands, *, t, tb, name, rider=None):
    nb = t // tb
    scale = FOX_SCALE if fox else MLA_SCALE
    nt_dims = (((1,), (1,)), ((), ()))
    tn_dims = (((0,), (0,)), ((), ()))
    n_rin = len(rider.operands) if rider else 0
    n_rout = len(rider.out_shape) if rider else 0
    n_in, n_out, n_scr = (9, 3, 9) if fox else (7, 3, 3)

    def body(*refs):
        ends = np.cumsum([0, n_in, n_rin, n_out, n_rout, n_scr])
        in_refs, rider_in, out_refs, rider_out, scr_refs = (refs[a:b] for a, b in zip(ends[:-1], ends[1:]))
        rider_refs = (rider_in, rider_out, refs[ends[-1]:])
        if fox:
            q_ref, k_ref, v_ref, do_ref, lse_ref, dl_ref, cq_ref, ck_ref, _ = in_refs
            dproj_ref, dck_ref, dcq_ref = out_refs
            dq_s, dk_s, dv_s, dc_s, dcq_s, stage_q, stage_k, stage_v, put_sems = scr_refs
        else:
            q_ref, kn_ref, kr_ref, v_ref, do_ref, lse_ref, dl_ref = in_refs
            dq_ref, dkv_ref, dkr_ref = out_refs
            dq_s, dk_s, dv_s = scr_refs
        head = pl.program_id(0)
        ki = pl.program_id(1)
        if rider:
            @pl.when(jnp.logical_and(head == 0, ki == 0))
            def _():
                rider.start(*rider_refs)

        @pl.when(ki == 0)
        def _():
            dq_s[...] = jnp.zeros_like(dq_s)
            if fox:
                dcq_s[...] = jnp.zeros_like(dcq_s)

        dk_s[...] = jnp.zeros_like(dk_s)
        dv_s[...] = jnp.zeros_like(dv_s)
        if fox:
            dc_s[...] = jnp.zeros_like(dc_s)
            kk = k_ref[...]
        else:
            kk = jnp.concatenate([kn_ref[...], kr_ref[...]], axis=1)
        vv = v_ref[...]

        def chunk(qc, masked):
            off = pl.multiple_of(qc * tb, tb)
            qq = q_ref[pl.ds(off, tb), :]
            dd = do_ref[pl.ds(off, tb), :]
            s = lax.dot_general(kk, qq, nt_dims, preferred_element_type=F32)
            if fox:
                s = s + cq_ref[0, qc] - ck_ref[0]
            if masked:
                row = lax.broadcasted_iota(jnp.int32, (tb, tb), 0)
                col = lax.broadcasted_iota(jnp.int32, (tb, tb), 1)
                s = jnp.where(row <= col, s, -jnp.inf)
            p = jnp.exp2(s * (scale * LOG2E) - lse_ref[0, qc] * LOG2E)
            dv_s[...] += jnp.dot(p.astype(BF16), dd, preferred_element_type=F32)
            dp = lax.dot_general(vv, dd, nt_dims, preferred_element_type=F32)
            ds = p * (dp - dl_ref[0, qc])
            if fox:
                dc_s[...] += jnp.sum(ds, axis=1, keepdims=True)
                dcq_s[qc] += jnp.sum(ds, axis=0, keepdims=True)
            dsb = (ds * scale).astype(BF16)
            dk_s[...] += jnp.dot(dsb, qq, preferred_element_type=F32)
            dq_s[pl.ds(off, tb), :] += lax.dot_general(dsb, kk, tn_dims, preferred_element_type=F32)

        chunk(ki, True)

        def loop_body(qc, carry):
            chunk(qc, False)
            return carry

        lax.fori_loop(ki + 1, nb, loop_body, 0)

        def put(stage_ref, rows, seg0, sem):
            col0 = pl.multiple_of(seg0 + head * HEAD, HEAD)
            return pltpu.make_async_copy(stage_ref, dproj_ref.at[rows, pl.ds(col0, HEAD)], sem)

        if fox:
            rows = pl.ds(pl.multiple_of(ki * tb, tb), tb)
            block_puts = [put(stage_k, rows, FK0, put_sems.at[1]), put(stage_v, rows, FV0, put_sems.at[2])]
            head_put = put(stage_q, pl.ds(0, t), FQ0, put_sems.at[0])

            @pl.when(jnp.logical_or(head > 0, ki > 0))
            def _():
                for cp in block_puts:
                    cp.wait()

            stage_k[...] = dk_s[...].astype(BF16)
            stage_v[...] = dv_s[...].astype(BF16)
            for cp in block_puts:
                cp.start()
            dck_ref[0] = -dc_s[...]

            @pl.when(ki == nb - 1)
            def _():
                @pl.when(head > 0)
                def _():
                    head_put.wait()

                stage_q[...] = dq_s[...].astype(BF16)
                head_put.start()
                dcq_ref[0] = dcq_s[...]

            @pl.when(jnp.logical_and(head == N_HEADS - 1, ki == nb - 1))
            def _():
                for cp in block_puts + [head_put]:
                    cp.wait()
        else:
            dkv_ref[...] = jnp.concatenate([dk_s[:, :HEAD], dv_s[...]], axis=1).astype(BF16)
            dkr_ref[...] = dk_s[:, HEAD:]

            @pl.when(ki == nb - 1)
            def _():
                dq_ref[...] = dq_s[...]

        if rider:
            @pl.when(jnp.logical_and(head == N_HEADS - 1, ki == nb - 1))
            def _():
                rider.wait(*rider_refs)

    stat = pl.BlockSpec((1, nb, 1, tb), lambda h, i: (h, 0, 0, 0))
    aliases = {}
    if fox:
        proj, d_o, lse4, delta4, c_row4, c_col, dproj = operands
        ins = [proj, proj, proj, d_o, lse4, delta4, c_row4, c_col, dproj]
        any_spec = pl.BlockSpec(memory_space=pl.ANY)
        in_specs = [pl.BlockSpec((t, HEAD), lambda h, i: (0, FQ0 // HEAD + h)),
                    pl.BlockSpec((tb, HEAD), lambda h, i: (i, FK0 // HEAD + h)),
                    pl.BlockSpec((tb, HEAD), lambda h, i: (i, FV0 // HEAD + h)),
                    pl.BlockSpec((t, HEAD), lambda h, i: (0, N_HEADS + h)),
                    stat, stat, stat, pl.BlockSpec((1, tb, 1), lambda h, i: (h, i, 0)), any_spec]
        aliases = {8: 0}
        out_specs = [any_spec, pl.BlockSpec((1, tb, 1), lambda h, i: (h, i, 0)), stat]
        out_shape = [jax.ShapeDtypeStruct(dproj.shape, dproj.dtype), jax.ShapeDtypeStruct((N_HEADS, t, 1), F32),
                     jax.ShapeDtypeStruct((N_HEADS, nb, 1, tb), F32)]
        scratch = [pltpu.VMEM((t, HEAD), F32), pltpu.VMEM((tb, HEAD), F32), pltpu.VMEM((tb, HEAD), F32),
                   pltpu.VMEM((tb, 1), F32), pltpu.VMEM((nb, 1, tb), F32),
                   pltpu.VMEM((t, HEAD), BF16), pltpu.VMEM((tb, HEAD), BF16), pltpu.VMEM((tb, HEAD), BF16),
                   pltpu.SemaphoreType.DMA((3,))]
    else:
        q_full, kv, kr, d_o, lse4, delta4 = operands
        ins = [q_full, kv, kr, kv, d_o, lse4, delta4]
        in_specs = [pl.BlockSpec((t, 256), lambda h, i: (0, h)),
                    pl.BlockSpec((tb, HEAD), lambda h, i: (i, 2 * h)),
                    pl.BlockSpec((tb, HEAD), lambda h, i: (i, 0)),
                    pl.BlockSpec((tb, HEAD), lambda h, i: (i, 2 * h + 1)),
                    pl.BlockSpec((t, HEAD), lambda h, i: (0, h)),
                    stat, stat]
        out_specs = [pl.BlockSpec((t, 256), lambda h, i: (0, h)), pl.BlockSpec((tb, 256), lambda h, i: (i, h)),
                     pl.BlockSpec((tb, HEAD), lambda h, i: (i, h))]
        out_shape = [jax.ShapeDtypeStruct((t, 2048), F32), jax.ShapeDtypeStruct((t, 2048), BF16),
                     jax.ShapeDtypeStruct((t, 1024), F32)]
        scratch = [pltpu.VMEM((t, 256), F32), pltpu.VMEM((tb, 256), F32), pltpu.VMEM((tb, HEAD), F32)]
    assert (len(ins), len(out_specs), len(scratch)) == (n_in, n_out, n_scr)
    if rider:
        any_spec = pl.BlockSpec(memory_space=pl.ANY)
        aliases = {**aliases, **{n_in + i_in: n_out + i_out for i_in, i_out in rider.aliases.items()}}
        ins = ins + list(rider.operands)
        in_specs = in_specs + [any_spec] * n_rin
        out_specs = out_specs + [any_spec] * n_rout
        out_shape = out_shape + list(rider.out_shape)
        scratch = scratch + list(rider.scratch)
    return pl.pallas_call(
        body, name=name, grid=(N_HEADS, nb), in_specs=in_specs, out_specs=out_specs, out_shape=out_shape,
        scratch_shapes=scratch, input_output_aliases=aliases, compiler_params=_params(2),
    )(*ins)


def _mid_bwd(dq_full, dkr, cos_t, sin_t, dck, flog, bf_row, *, tm):
    t = dq_full.shape[0]
    n = t // tm

    def body(dq_ref, dkr_ref, cos_ref, sin_ref, dck_ref, fl_ref, bf_ref,
             dq2_ref, dkraw_ref, dfl_ref, dbf_ref, carry_ref):
        i = pl.program_id(0)

        @pl.when(i == 0)
        def _():
            carry_ref[...] = jnp.zeros_like(carry_ref)
            dbf_ref[...] = jnp.zeros_like(dbf_ref)

        c, s = cos_ref[...], sin_ref[...]
        dkr_sum = jnp.zeros((tm, LANE), F32)
        for h in range(N_HEADS):
            dq2_ref[:, 256 * h:256 * h + 128] = dq_ref[:, 256 * h:256 * h + 128].astype(BF16)
            dq2_ref[:, 256 * h + 128:256 * h + 256] = _rot_bwd(dq_ref[:, 256 * h + 128:256 * h + 256], c, s).astype(BF16)
            dkr_sum = dkr_sum + dkr_ref[:, HEAD * h:HEAD * (h + 1)]
        dkraw_ref[...] = _rot_bwd(dkr_sum, c, s).astype(BF16)

        dc = dck_ref[...]
        row = lax.broadcasted_iota(jnp.int32, (tm, tm), 0)
        col = lax.broadcasted_iota(jnp.int32, (tm, tm), 1)
        tri = (col >= row).astype(BF16)
        acc = carry_ref[0:1, :]
        for part in _split3(dc):
            acc = acc + jnp.dot(tri, part, preferred_element_type=F32)
        carry_ref[0:1, :] = carry_ref[0:1, :] + jnp.sum(dc, axis=0, keepdims=True)
        z = fl_ref[...] + bf_ref[...]
        dz = acc / (1.0 + jnp.exp(z))
        dfl_ref[...] = dz.astype(BF16)
        dbf_ref[...] += jnp.sum(dz, axis=0, keepdims=True)

    rev = lambda w: pl.BlockSpec((tm, w), lambda i: (n - 1 - i, 0))
    vec = lambda w: pl.BlockSpec((1, w), lambda i: (0, 0))
    return pl.pallas_call(
        body, name="mid_bwd", grid=(n,),
        in_specs=[rev(2048), rev(1024), rev(LANE), rev(LANE), rev(LANE), rev(LANE), vec(LANE)],
        out_specs=[rev(2048), rev(LANE), rev(LANE), vec(LANE)],
        out_shape=[jax.ShapeDtypeStruct((t, 2048), BF16), jax.ShapeDtypeStruct((t, LANE), BF16),
                   jax.ShapeDtypeStruct((t, LANE), BF16), jax.ShapeDtypeStruct((1, LANE), F32)],
        scratch_shapes=[pltpu.VMEM((8, LANE), F32)],
        compiler_params=_params(1),
    )(dq_full, dkr, cos_t, sin_t, dck, flog, bf_row)


def _norm_bwd(proj, dqn, dkvn, g_q, g_kv, dkr_raw, dfl, dproj, *, tm):
    t = proj.shape[0]
    assert (KR0, FL0, KVL0, LAT_W) == (Q_RANK, Q_RANK + LANE, Q_RANK + 2 * LANE, Q_RANK + 2 * LANE + KV_RANK)

    def body(p_ref, dqn_ref, dkvn_ref, gq_ref, gkv_ref, dkr_ref, dfl_ref, _, dproj_ref, dgq_ref, dgkv_ref):
        i = pl.program_id(0)

        @pl.when(i == 0)
        def _():
            dgq_ref[...] = jnp.zeros_like(dgq_ref)
            dgkv_ref[...] = jnp.zeros_like(dgkv_ref)

        d_lat = []
        for lo, w, dn_ref, g_ref, dg_ref in ((QL0, Q_RANK, dqn_ref, gq_ref, dgq_ref),
                                             (KVL0, KV_RANK, dkvn_ref, gkv_ref, dgkv_ref)):
            xv = p_ref[:, lo:lo + w].astype(F32)
            r = lax.rsqrt(jnp.mean(xv * xv, axis=-1, keepdims=True) + NORM_EPS)
            xh = xv * r
            dn = dn_ref[...]
            dg_ref[...] += jnp.sum(dn * xh, axis=0, keepdims=True)
            dxh = dn * g_ref[...]
            d_lat.append((r * (dxh - xh * jnp.mean(dxh * xh, axis=-1, keepdims=True))).astype(BF16))
        dproj_ref[...] = jnp.concatenate([d_lat[0], dkr_ref[...], dfl_ref[...], d_lat[1]], axis=1)

    row = lambda w: pl.BlockSpec((tm, w), lambda i: (i, 0))
    vec = lambda w: pl.BlockSpec((1, w), lambda i: (0, 0))
    return pl.pallas_call(
        body, name="norm_bwd", grid=(t // tm,),
        in_specs=[row(LAT_W), row(Q_RANK), row(KV_RANK), vec(Q_RANK), vec(KV_RANK), row(LANE), row(LANE),
                  pl.BlockSpec(memory_space=pl.ANY)],
        out_specs=[row(LAT_W), vec(Q_RANK), vec(KV_RANK)],
        out_shape=[jax.ShapeDtypeStruct(dproj.shape, dproj.dtype),
                   jax.ShapeDtypeStruct((1, Q_RANK), F32), jax.ShapeDtypeStruct((1, KV_RANK), F32)],
        input_output_aliases={7: 0},
        compiler_params=_params(1),
    )(proj, dqn, dkvn, g_q, g_kv, dkr_raw, dfl, dproj)


def _prenorm_bwd(dh, x, g, dy, *, tm):
    t = x.shape[0]

    def body(dh_ref, x_ref, g_ref, dy_ref, gx_ref, dg_ref):
        i = pl.program_id(0)

        @pl.when(i == 0)
        def _():
            dg_ref[...] = jnp.zeros_like(dg_ref)

        xv = x_ref[...]
        r = lax.rsqrt(jnp.mean(xv * xv, axis=-1, keepdims=True) + NORM_EPS)
        xh = xv * r
        dn = dh_ref[...]
        dg_ref[...] += jnp.sum(dn * xh, axis=0, keepdims=True)
        dxh = dn * g_ref[...]
        gx_ref[...] = dy_ref[...] + r * (dxh - xh * jnp.mean(dxh * xh, axis=-1, keepdims=True))

    row = pl.BlockSpec((tm, D_MODEL), lambda i: (i, 0))
    vec = pl.BlockSpec((1, D_MODEL), lambda i: (0, 0))
    return pl.pallas_call(
        body, name="prenorm_bwd", grid=(t // tm,),
        in_specs=[row, row, vec, row], out_specs=[row, vec],
        out_shape=[jax.ShapeDtypeStruct((t, D_MODEL), F32), jax.ShapeDtypeStruct((1, D_MODEL), F32)],
        compiler_params=_params(1),
    )(dh, x, g, dy)


def _adam_math(w, g, m, v):
    m = ADAM_B1 * m + (1.0 - ADAM_B1) * g
    v = ADAM_B2 * v + (1.0 - ADAM_B2) * (g * g)
    m_hat = m / (1.0 - ADAM_B1 ** ADAM_STEP)
    v_hat = v / (1.0 - ADAM_B2 ** ADAM_STEP)
    delta = -ADAM_LR * (m_hat / (jnp.sqrt(v_hat) + ADAM_EPS) + ADAM_WD * w)
    return delta, m, v


def _adamw(land, w, m, v, *, tr, name):
    rows, cols = w.shape

    def body(l_ref, w_ref, m_ref, v_ref, g_ref, d_ref, nm_ref, nv_ref):
        g = l_ref[0].astype(F32)
        for s in range(1, N_CHIPS):
            g = g + l_ref[s].astype(F32)
        g_ref[...] = g
        d_ref[...], nm_ref[...], nv_ref[...] = _adam_math(w_ref[...], g, m_ref[...], v_ref[...])

    blk = pl.BlockSpec((tr, cols), lambda i: (i, 0))
    return pl.pallas_call(
        body, name=name, grid=(rows // tr,),
        in_specs=[pl.BlockSpec((N_CHIPS, tr, cols), lambda i: (0, i, 0)), blk, blk, blk],
        out_specs=[blk, blk, blk, blk],
        out_shape=[jax.ShapeDtypeStruct((rows, cols), F32)] * 4,
        compiler_params=_params(1),
    )(land, w, m, v)


def _adamw_small(gathered, w, m, v):
    def body(a_ref, w_ref, m_ref, v_ref, g_ref, d_ref, nm_ref, nv_ref):
        g = a_ref[0:SMALL_ROWS, :]
        for s in range(1, N_DEV):
            g = g + a_ref[SMALL_ROWS * s:SMALL_ROWS * (s + 1), :]
        g_ref[...] = g
        d_ref[...], nm_ref[...], nv_ref[...] = _adam_math(w_ref[...], g, m_ref[...], v_ref[...])

    return pl.pallas_call(
        body, name="adamw_small",
        out_shape=[jax.ShapeDtypeStruct((SMALL_ROWS, SMALL_COLS), F32)] * 4,
        compiler_params=_params(),
    )(gathered, w, m, v)


def _place():
    x, y, c = lax.axis_index("x"), lax.axis_index("y"), lax.axis_index("c")
    return x, y, c


def _flip(p, k):
    x, y, c = p
    return (1 - x if k & 4 else x, 1 - y if k & 2 else y, 1 - c if k & 1 else c)


def _index(p):
    return 4 * p[0] + 2 * p[1] + p[2]


def _all_gather(shards):
    n = len(shards)
    hbm = pl.BlockSpec(memory_space=pl.ANY)

    def body(*refs):
        ins, outs = refs[:n], refs[n:2 * n]
        send_sems, recv_sems, local_sems = refs[2 * n:]
        me = _place()
        sibling = _flip(me, 1)
        chips = [_flip(me, 4), _flip(me, 2), _flip(me, 6)]

        def copy(a, k, block, to, src=None):
            dst = outs[a].at[_index(block)]
            return pltpu.make_async_remote_copy(
                src_ref=dst if src is None else src, dst_ref=dst,
                send_sem=send_sems.at[7 * a + k], recv_sem=recv_sems.at[7 * a + k],
                device_id=to, device_id_type=MESH)

        started = []
        for a in range(n):
            mine = pltpu.make_async_copy(ins[a], outs[a].at[_index(me)], local_sems.at[a])
            mine.start()
            started.append(mine)
        first = []
        for a in range(n):
            first.append(copy(a, 0, me, sibling, src=ins[a]))
            first += [copy(a, 1 + j, me, chip, src=ins[a]) for j, chip in enumerate(chips)]
        for cp in first:
            cp.start()
        passed = []
        for a in range(n):
            for j, chip in enumerate(chips):
                copy(a, 1 + j, chip, me).wait_recv()
                fwd = copy(a, 4 + j, chip, sibling)
                fwd.start()
                passed.append(fwd)
        for a in range(n):
            copy(a, 0, sibling, me).wait_recv()
            for j, chip in enumerate(chips):
                copy(a, 4 + j, _flip(chip, 1), me).wait_recv()
        for cp in first + passed:
            cp.wait_send()
        for mine in started:
            mine.wait()

    return pl.pallas_call(
        body, name="all_gather_weights",
        in_specs=[hbm] * n, out_specs=[hbm] * n,
        out_shape=[jax.ShapeDtypeStruct((N_DEV,) + s.shape, s.dtype) for s in shards],
        scratch_shapes=[pltpu.SemaphoreType.DMA((7 * n,)), pltpu.SemaphoreType.DMA((7 * n,)),
                        pltpu.SemaphoreType.DMA((n,))],
    )(*shards)


class _Exchange:
    def __init__(self, tasks):
        self.tasks = tasks
        taken = [land for _, _, land, _, _ in tasks if land is not None]
        self.operands = [src for src, _, _, _, _ in tasks] + taken
        self.out_shape = [
            jax.ShapeDtypeStruct((N_CHIPS,) + ((2,) if by_core else ()) + (src.shape if same else src.shape[1:]), src.dtype)
            for src, _, _, same, by_core in tasks]
        self.aliases, n_taken = {}, 0
        for a, (_, _, land, _, _) in enumerate(tasks):
            if land is not None:
                self.aliases[len(tasks) + n_taken] = a
                n_taken += 1
        self.scratch = [pltpu.SemaphoreType.DMA((N_CHIPS,)), pltpu.SemaphoreType.DMA((N_CHIPS,)),
                        pltpu.SemaphoreType.DMA(())] * len(tasks)

    def _copies(self, ins, outs, scratch):
        x, y, core = _place()
        my = 2 * x + y
        for a, (_, chips, _, same, by_core) in enumerate(self.tasks):
            send_sems, recv_sems, local_sem = scratch[3 * a:3 * a + 3]
            slot = (lambda s, a=a, by_core=by_core: outs[a].at[s, core] if by_core else outs[a].at[s])
            for i, j in enumerate(chips):
                src = ins[a] if same else ins[a].at[i]
                pair = jnp.bitwise_xor(my, j)
                remote = pltpu.make_async_remote_copy(
                    src_ref=src, dst_ref=slot(my), send_sem=send_sems.at[pair], recv_sem=recv_sems.at[pair],
                    device_id=(j >> 1, j & 1, core), device_id_type=MESH)
                local = pltpu.make_async_copy(src, slot(my), local_sem)
                yield j, my, core, remote, local, slot, (send_sems, recv_sems)

    def start(self, ins, outs, scratch):
        for j, my, _, remote, local, _, _ in self._copies(ins, outs, scratch):
            pl.when(my != j)(remote.start)
            pl.when(my == j)(local.start)

    def wait(self, ins, outs, scratch):
        for j, my, core, remote, local, slot, (send_sems, recv_sems) in self._copies(ins, outs, scratch):
            pl.when(my != j)(remote.wait_send)

            @pl.when(my == j)
            def _():
                local.wait()
                for s in range(N_CHIPS):
                    if s != j:
                        pltpu.make_async_remote_copy(
                            src_ref=slot(s), dst_ref=slot(s), send_sem=send_sems.at[j ^ s], recv_sem=recv_sems.at[j ^ s],
                            device_id=(s >> 1, s & 1, core), device_id_type=MESH).wait_recv()


N_CHIPS = 4
ALL_CHIPS = tuple(range(N_CHIPS))


def _to_other_core(parts, *, name):
    n_arr = len(parts)
    hbm = pl.BlockSpec(memory_space=pl.ANY)

    def body(*refs):
        srcs, lands = refs[:n_arr], refs[n_arr:2 * n_arr]
        send_sems, recv_sems = refs[2 * n_arr:]
        me = _place()
        copies = [pltpu.make_async_remote_copy(src_ref=srcs[a].at[1 - me[2]], dst_ref=lands[a], send_sem=send_sems.at[a],
                                               recv_sem=recv_sems.at[a], device_id=_flip(me, 1), device_id_type=MESH)
                  for a in range(n_arr)]
        for cp in copies:
            cp.start()
        for cp in copies:
            cp.wait()

    return pl.pallas_call(
        body, name=name, in_specs=[hbm] * n_arr, out_specs=[hbm] * n_arr,
        out_shape=[jax.ShapeDtypeStruct(p.shape[1:], p.dtype) for p in parts],
        scratch_shapes=[pltpu.SemaphoreType.DMA((n_arr,)), pltpu.SemaphoreType.DMA((n_arr,))],
    )(*parts)


def _share_with_other_core(gathered, *, name):
    n_arr = len(gathered)
    hbm = pl.BlockSpec(memory_space=pl.ANY)

    def body(*refs):
        bufs = refs[n_arr:2 * n_arr]
        send_sems, recv_sems = refs[2 * n_arr:]
        me = _place()
        copies = []
        for a in range(n_arr):
            for j in range(N_CHIPS):
                block = bufs[a].at[j, me[2]]
                copies.append(pltpu.make_async_remote_copy(
                    src_ref=block, dst_ref=block, send_sem=send_sems.at[N_CHIPS * a + j],
                    recv_sem=recv_sems.at[N_CHIPS * a + j], device_id=_flip(me, 1), device_id_type=MESH))
        for cp in copies:
            cp.start()
        for cp in copies:
            cp.wait()

    return pl.pallas_call(
        body, name=name, in_specs=[hbm] * n_arr, out_specs=[hbm] * n_arr,
        out_shape=[jax.ShapeDtypeStruct(g.shape, g.dtype) for g in gathered],
        scratch_shapes=[pltpu.SemaphoreType.DMA((N_CHIPS * n_arr,)), pltpu.SemaphoreType.DMA((N_CHIPS * n_arr,))],
        input_output_aliases={a: a for a in range(n_arr)},
    )(*gathered)


def _pair_sum(mine, other, core, *, tr, name):
    _, n, rows, cols = mine.shape
    tr = min(tr, rows)

    def body(core_ref, a_ref, b_ref, o_ref):
        o_ref[...] = (a_ref[0].astype(F32) + b_ref[...].astype(F32)).astype(BF16)

    return pl.pallas_call(
        body, name=name,
        grid_spec=pltpu.PrefetchScalarGridSpec(
            num_scalar_prefetch=1, grid=(n, rows // tr),
            in_specs=[pl.BlockSpec((1, 1, tr, cols), lambda j, i, core_ref: (core_ref[0], j, i, 0)),
                      pl.BlockSpec((1, tr, cols), lambda j, i, core_ref: (j, i, 0))],
            out_specs=pl.BlockSpec((1, tr, cols), lambda j, i, core_ref: (j, i, 0))),
        out_shape=jax.ShapeDtypeStruct(other.shape, BF16),
        compiler_params=_params(2),
    )(core, mine, other)


def _gather_small(vec):
    def body(v_ref, out_ref, send_sems, recv_sems, local_sem):
        me = _place()

        def rows(p):
            return out_ref.at[pl.ds(pl.multiple_of(_index(p) * SMALL_ROWS, SMALL_ROWS), SMALL_ROWS), :]

        mine = pltpu.make_async_copy(v_ref, rows(me), local_sem)
        mine.start()
        sends = []
        for k in range(1, N_DEV):
            peer = _flip(me, k)
            cp = pltpu.make_async_remote_copy(src_ref=v_ref, dst_ref=rows(me), send_sem=send_sems.at[k - 1],
                                              recv_sem=recv_sems.at[k - 1], device_id=peer, device_id_type=MESH)
            cp.start()
            sends.append(cp)
        for k in range(1, N_DEV):
            peer = _flip(me, k)
            pltpu.make_async_remote_copy(src_ref=rows(peer), dst_ref=rows(peer), send_sem=send_sems.at[k - 1],
                                         recv_sem=recv_sems.at[k - 1], device_id=peer, device_id_type=MESH).wait_recv()
        for cp in sends:
            cp.wait_send()
        mine.wait()

    return pl.pallas_call(
        body, name="gather_small",
        in_specs=[pl.BlockSpec(memory_space=pltpu.VMEM)], out_specs=pl.BlockSpec(memory_space=pltpu.VMEM),
        out_shape=jax.ShapeDtypeStruct((N_DEV * SMALL_ROWS, SMALL_COLS), F32),
        scratch_shapes=[pltpu.SemaphoreType.DMA((7,)), pltpu.SemaphoreType.DMA((7,)), pltpu.SemaphoreType.DMA],
    )(vec)


def _w_in_nice(gathered):
    pieces, pos = [], 0
    for o0, width, n0 in sorted(_SEGMENTS, key=lambda seg: seg[2]):
        if n0 > pos:
            pieces.append(jnp.zeros((D_MODEL, n0 - pos), gathered.dtype))
        o = o0
        while o < o0 + width:
            d = o // SHARD_IN
            hi = min(o0 + width, (d + 1) * SHARD_IN)
            pieces.append(gathered[d][:, o - d * SHARD_IN:hi - d * SHARD_IN])
            o = hi
        pos = n0 + width
    pieces.append(jnp.zeros((D_MODEL, NP_IN - pos), gathered.dtype))
    return jnp.concatenate(pieces, axis=1)


def _w_in_blocks(chips, dw_lat, dw_rest):
    blocks = []
    for core in range(2):
        for chip in chips:
            lo = (2 * chip + core) * SHARD_IN
            runs = []
            for o0, width, n0 in _SEGMENTS:
                a, b = max(lo, o0), min(lo + SHARD_IN, o0 + width)
                if a < b:
                    n_a, n_b = n0 + a - o0, n0 + b - o0
                    runs.append(dw_lat[:, n_a:n_b] if n_b <= LAT_W else dw_rest[:, n_a - LAT_W:n_b - LAT_W])
            blocks.append(jnp.concatenate(runs, axis=1))
    return jnp.stack(blocks).reshape(2, len(chips), D_MODEL, SHARD_IN)


def _by_core(shards):
    return shards.reshape((N_CHIPS, 2) + shards.shape[1:]).swapaxes(0, 1)


EARLY_CHIPS = (1, 2)
LATE_CHIPS = (0, 3)


def _w_uq_nice(shard):
    z = jnp.zeros((Q_RANK, 32), shard.dtype)
    return jnp.concatenate([shard[:, :128], shard[:, 128:160], z, shard[:, 160:192], z], axis=1)


def _pack_small(g_pre, g_post, g_q, g_kv, b_f, extra=None):
    parts = [g_pre.reshape(-1), g_post.reshape(-1), g_q.reshape(-1), g_kv.reshape(-1), b_f.reshape(-1)]
    if extra is not None:
        parts.append(extra.reshape(-1))
    flat = jnp.concatenate(parts)
    flat = jnp.concatenate([flat, jnp.zeros((SMALL_ROWS * SMALL_COLS - flat.shape[0],), F32)])
    return flat.reshape(SMALL_ROWS, SMALL_COLS)


def _unpack_small(packed):
    flat = packed.reshape(-1)
    o = 0
    out = []
    for n in (D_MODEL, D_MODEL, Q_RANK, KV_RANK, N_HEADS):
        out.append(flat[o:o + n].reshape(1, n))
        o += n
    return out, flat[o]


def kernel(x, positions, g_pre, w_in, g_q_latent, w_uq, g_kv_latent, w_ukv, b_forget, w_out, g_post, loss_target, m_g_pre, m_w_in, m_g_q_latent, m_w_uq, m_g_kv_latent, m_w_ukv, m_b_forget, m_w_out, m_g_post, v_g_pre, v_w_in, v_g_q_latent, v_w_uq, v_g_kv_latent, v_w_ukv, v_b_forget, v_w_out, v_g_post):
    t = x.shape[1]
    tb = min(512, t)
    tm = min(256, t)
    nb = t // tb
    x2 = x.reshape(t, D_MODEL)
    target = loss_target.reshape(t, D_MODEL)
    pos_col = positions.reshape(t, 1).astype(F32)
    bf_row = jnp.concatenate([b_forget.reshape(1, N_HEADS), jnp.zeros((1, LANE - N_HEADS), F32)], axis=1)

    (g_in,) = _all_gather([w_in[0].astype(BF16)])
    w_in_n = _w_in_nice(g_in)
    gather_rest = _Exchange([(w, ALL_CHIPS, None, True, True) for w in
                             (_w_uq_nice(w_uq[0].astype(BF16)), w_ukv[0].astype(BF16), w_out[0].astype(BF16))])
    core = lax.axis_index("c").astype(jnp.int32).reshape(1)

    h = _prenorm(x2, g_pre, tm=tm)
    proj, g_uq, g_ukv, g_out = _mm(h, w_in_n, name="proj_in", out_dtype=BF16, tm=1024, tn=512, tk=2048,
                                   rider=gather_rest)
    g_uq, g_ukv, g_out = _share_with_other_core([g_uq, g_ukv, g_out], name="share_weights")
    w_uq_n = g_uq.reshape(N_DEV, Q_RANK, 256).transpose(1, 0, 2).reshape(Q_RANK, N_HEADS * 256)
    w_ukv_n = g_ukv.reshape(N_DEV, KV_RANK, 256).transpose(1, 0, 2).reshape(KV_RANK, N_HEADS * 256)
    w_out_n = g_out.reshape(D_MODEL, D_MODEL)
    flog = _mm(h, w_in_n[:, FL0:FL0 + LANE], name="proj_flog", out_dtype=F32, tm=1024, tn=LANE, tk=2048)
    qn, kvn, kr, cos_t, sin_t, c = _mid_fwd(proj, flog, g_q_latent, g_kv_latent, bf_row, pos_col, tm=tm)
    q_raw = _mm(qn, w_uq_n, name="q_up", out_dtype=F32, tm=1024, tn=512, tk=Q_RANK)
    q_full = _rope_q(q_raw, cos_t, sin_t, tm=tm)
    kv = _mm(kvn, w_ukv_n, name="kv_up", out_dtype=BF16, tm=1024, tn=512, tk=KV_RANK)
    c_heads = c[:, :N_HEADS].T
    c_col = c_heads.reshape(N_HEADS, t, 1)
    c_row4 = c_heads.reshape(N_HEADS, nb, 1, tb)
    o_all, og_all, lse4_mla = _attn_fwd(False, (q_full, kv, kr, proj), t=t, tb=tb, name="mla_fwd")
    o_all, og_all, lse4_fox = _attn_fwd(True, (proj, c_col, c_row4, o_all, og_all), t=t, tb=tb, name="fox_fwd")
    o = _mm(og_all, w_out_n, name="out_proj", out_dtype=F32, tm=1024, tn=512, tk=2048)
    dy, d_o_post, dg_post, loss_part = _postnorm_loss(o, x2, target, g_post, tm=tm)

    dw_out = _mm(og_all, d_o_post, name="dw_out", ta=True, out_dtype=BF16, tm=1024, tn=1024, tk=512)
    p_out = _by_core(dw_out.reshape(N_DEV, D_MODEL // N_DEV, D_MODEL))
    (o_out,) = _to_other_core([p_out], name="dw_out_to_core")
    s_out = _pair_sum(p_out, o_out, core, tr=256, name="dw_out_pair_sum")
    d_attn, dproj, delta = _dog_gate(d_o_post, w_out_n, o_all, proj, tm=min(1024, t))
    delta4 = delta[:, :2 * N_HEADS].T.reshape(2 * N_HEADS, nb, 1, tb)
    dproj, dck, dcq, l_out = _attn_bwd(True, (proj, d_attn, lse4_fox, delta4[N_HEADS:], c_row4, c_col, dproj),
                                       t=t, tb=tb, name="fox_bwd",
                                       rider=_Exchange([(s_out, ALL_CHIPS, None, False, False)]))
    dw_in_rest = _mm(h, dproj, name="dw_in_rest", ta=True, out_dtype=BF16, tm=2048, tn=512, tk=512,
                     b_cols=(LAT_W, NP_IN - LAT_W))
    p_in = _w_in_blocks(EARLY_CHIPS, None, dw_in_rest)
    (o_in,) = _to_other_core([p_in], name="dw_in_early_to_core")
    s_in = _pair_sum(p_in, o_in, core, tr=256, name="dw_in_early_pair_sum")
    dq_full, dkv, dkr, l_in = _attn_bwd(False, (q_full, kv, kr, d_attn, lse4_mla, delta4[:N_HEADS]),
                                        t=t, tb=tb, name="mla_bwd",
                                        rider=_Exchange([(s_in, EARLY_CHIPS, None, False, False)]))
    dc_heads = dck.reshape(N_HEADS, t) + dcq.reshape(N_HEADS, t)
    dck_rows = jnp.concatenate([dc_heads.T, jnp.zeros((t, LANE - N_HEADS), F32)], axis=1)
    dq2, dkr_raw, dfl, dbf = _mid_bwd(dq_full, dkr, cos_t, sin_t, dck_rows, flog, bf_row, tm=tm)
    dqn = _mm(dq2, w_uq_n, name="d_qn", nt=True, out_dtype=F32, tm=1024, tn=Q_RANK, tk=2048)
    dkvn = _mm(dkv, w_ukv_n, name="d_kvn", nt=True, out_dtype=F32, tm=1024, tn=KV_RANK, tk=2048)
    dw_uq = _mm(qn, dq2, name="dw_uq", ta=True, out_dtype=BF16, tm=Q_RANK, tn=1024, tk=512)
    dw_ukv = _mm(kvn, dkv, name="dw_ukv", ta=True, out_dtype=BF16, tm=KV_RANK, tn=1024, tk=512)
    dproj, dg_q, dg_kv = _norm_bwd(proj, dqn, dkvn, g_q_latent, g_kv_latent, dkr_raw, dfl, dproj, tm=tm)
    dw_in_lat = _mm(h, dproj, name="dw_in_lat", ta=True, out_dtype=BF16, tm=1024, tn=LAT_W, tk=512, b_cols=(0, LAT_W))
    dw_uq_h = dw_uq.reshape(Q_RANK, N_HEADS, 256)
    s_uq = jnp.concatenate([dw_uq_h[:, :, :160], dw_uq_h[:, :, 192:224]], axis=2).transpose(1, 0, 2)
    s_ukv = dw_ukv.reshape(KV_RANK, N_HEADS, 256).transpose(1, 0, 2)
    late_parts = [_by_core(s_uq), _by_core(s_ukv), _w_in_blocks(LATE_CHIPS, dw_in_lat, dw_in_rest)]
    late_other = _to_other_core(late_parts, name="dw_late_to_core")
    late_sums = [_pair_sum(p, o_, core, tr=256, name=f"dw_late_pair_sum_{i}")
                 for i, (p, o_) in enumerate(zip(late_parts, late_other))]
    late = _Exchange([(late_sums[0], ALL_CHIPS, None, False, False), (late_sums[1], ALL_CHIPS, None, False, False),
                      (late_sums[2], LATE_CHIPS, l_in, False, False)])
    dh, l_uq, l_ukv, l_in = _mm(dproj, w_in_n, name="d_h", nt=True, out_dtype=F32, tm=1024, tn=512, tk=NP_IN // 4,
                                rider=late)
    grad_x, dg_pre = _prenorm_bwd(dh, x2, g_pre, dy, tm=tm)

    small = _gather_small(_pack_small(dg_pre, dg_post, dg_q, dg_kv, dbf[:, :N_HEADS], loss_part))

    res_in = _adamw(l_in, w_in[0], m_w_in[0], v_w_in[0], tr=256, name="adamw_w_in")
    res_uq = _adamw(l_uq, w_uq[0], m_w_uq[0], v_w_uq[0], tr=256, name="adamw_w_uq")
    res_ukv = _adamw(l_ukv, w_ukv[0], m_w_ukv[0], v_w_ukv[0], tr=256, name="adamw_w_ukv")
    res_out = _adamw(l_out, w_out[0], m_w_out[0], v_w_out[0], tr=128, name="adamw_w_out")
    res_small = _adamw_small(
        small,
        _pack_small(g_pre, g_post, g_q_latent, g_kv_latent, b_forget),
        _pack_small(m_g_pre, m_g_post, m_g_q_latent, m_g_kv_latent, m_b_forget),
        _pack_small(v_g_pre, v_g_post, v_g_q_latent, v_g_kv_latent, v_b_forget))
    small_out = [_unpack_small(r) for r in res_small]
    loss = small_out[0][1]

    def leaves(kind):
        (s_pre, s_post, s_q, s_kv, s_bf), _ = small_out[kind]
        return [s_pre, res_in[kind][None], s_q, res_uq[kind][None], s_kv, res_ukv[kind][None], s_bf,
                res_out[kind][None], s_post]

    return (loss, grad_x.reshape(x.shape), *leaves(0), *leaves(1), *leaves(2), *leaves(3))
```

```python
import functools

import numpy as np
import jax
import jax.numpy as jnp
from jax import lax
from jax.experimental import pallas as pl
from jax.experimental.pallas import tpu as pltpu

F32 = jnp.float32
BF16 = jnp.bfloat16
MESH = pl.DeviceIdType.MESH

N_DEV = 8
D_MODEL = 2048
N_HEADS = 8
HEAD = 128
Q_RANK = 768
KV_RANK = 512
ROPE = 64
D_IN = 6472
SHARD_IN = D_IN // N_DEV
NORM_EPS = 1e-6
ROPE_THETA = 10000.0
MLA_SCALE = (HEAD + ROPE) ** -0.5
FOX_SCALE = HEAD ** -0.5

QL0, KR0, FL0, KVL0, GM0, GF0, FQ0, FK0, FV0, NP_IN = 0, 768, 896, 1024, 1536, 2560, 3584, 4608, 5632, 6656
LAT_W = GM0
LANE = 128
_SEGMENTS = ((0, 768, QL0), (768, 512, KVL0), (1280, 32, KR0), (1312, 32, KR0 + 64), (1344, 1024, GM0),
             (2368, 3072, FQ0), (5440, 8, FL0), (5448, 1024, GF0))
LOG2E = 1.4426950408889634

ADAM_LR = 0.001
ADAM_B1 = 0.9
ADAM_B2 = 0.999
ADAM_EPS = 1e-08
ADAM_WD = 0.01
ADAM_STEP = 10

VMEM_LIMIT_BYTES = 56 * 1024 * 1024
SMALL_ROWS, SMALL_COLS = 8, 768


def _params(n_grid=0):
    return pltpu.CompilerParams(vmem_limit_bytes=VMEM_LIMIT_BYTES,
                                dimension_semantics=("arbitrary",) * n_grid if n_grid else None)


def _sigmoid(z):
    return 1.0 / (1.0 + jnp.exp(-z))


def _split3(v):
    a = v.astype(BF16)
    r = v - a.astype(F32)
    b = r.astype(BF16)
    c = (r - b.astype(F32)).astype(BF16)
    return a, b, c


def _mm(a, b, *, name, nt=False, ta=False, out_dtype=F32, tm=1024, tn=512, tk=2048, b_cols=None, rider=None):
    assert not (nt and ta)
    k_dim, m = a.shape if ta else a.shape[::-1]
    n = b.shape[0] if nt else b.shape[1]
    col0 = 0
    if b_cols is not None:
        assert not nt
        col0, n = b_cols
    assert (b.shape[1] if nt else b.shape[0]) == k_dim
    tm, tn, tk = min(tm, m), min(tn, n), min(tk, k_dim)
    assert m % tm == 0 and n % tn == 0 and k_dim % tk == 0 and col0 % tn == 0, (name, a.shape, b.shape)
    nk = k_dim // tk
    j0 = col0 // tn
    grid = (m // tm, n // tn, nk)
    dims = (((0 if ta else 1,), (1 if nt else 0,)), ((), ()))
    n_rin = len(rider.operands) if rider else 0
    n_rout = len(rider.out_shape) if rider else 0

    def body(*refs):
        a_ref, b_ref = refs[:2]
        o_ref = refs[2 + n_rin]
        acc_ref = refs[3 + n_rin + n_rout]
        i, j, k = pl.program_id(0), pl.program_id(1), pl.program_id(2)
        if rider:
            rider_refs = (refs[2:2 + n_rin], refs[3 + n_rin:3 + n_rin + n_rout], refs[4 + n_rin + n_rout:])

            @pl.when(jnp.logical_and(i == 0, jnp.logical_and(j == 0, k == 0)))
            def _():
                rider.start(*rider_refs)

        @pl.when(k == 0)
        def _():
            acc_ref[...] = jnp.zeros_like(acc_ref)

        acc_ref[...] += lax.dot_general(a_ref[...], b_ref[...], dims, preferred_element_type=F32)

        @pl.when(k == nk - 1)
        def _():
            o_ref[...] = acc_ref[...].astype(o_ref.dtype)

        if rider:
            @pl.when(jnp.logical_and(i == grid[0] - 1, jnp.logical_and(j == grid[1] - 1, k == nk - 1)))
            def _():
                rider.wait(*rider_refs)

    b_spec = (pl.BlockSpec((tn, tk), lambda i, j, k: (j, k)) if nt
              else pl.BlockSpec((tk, tn), lambda i, j, k: (k, j0 + j)))
    a_spec = (pl.BlockSpec((tk, tm), lambda i, j, k: (k, i)) if ta
              else pl.BlockSpec((tm, tk), lambda i, j, k: (i, k)))
    any_spec = pl.BlockSpec(memory_space=pl.ANY)
    out = pl.pallas_call(
        body, name=name, grid=grid,
        in_specs=[a_spec, b_spec] + [any_spec] * n_rin,
        out_specs=[pl.BlockSpec((tm, tn), lambda i, j, k: (i, j))] + [any_spec] * n_rout,
        out_shape=[jax.ShapeDtypeStruct((m, n), out_dtype)] + (list(rider.out_shape) if rider else []),
        scratch_shapes=[pltpu.VMEM((tm, tn), F32)] + (list(rider.scratch) if rider else []),
        input_output_aliases={2 + i_in: 1 + i_out for i_in, i_out in rider.aliases.items()} if rider else {},
        compiler_params=_params(3),
    )(a, b, *(rider.operands if rider else ()))
    return out if rider else out[0]


def _prenorm(x, g, *, tm):
    t = x.shape[0]

    def body(x_ref, g_ref, h_ref):
        xv = x_ref[...]
        r = lax.rsqrt(jnp.mean(xv * xv, axis=-1, keepdims=True) + NORM_EPS)
        h_ref[...] = (xv * r * g_ref[...]).astype(BF16)

    return pl.pallas_call(
        body, name="prenorm", grid=(t // tm,),
        in_specs=[pl.BlockSpec((tm, D_MODEL), lambda i: (i, 0)), pl.BlockSpec((1, D_MODEL), lambda i: (0, 0))],
        out_specs=pl.BlockSpec((tm, D_MODEL), lambda i: (i, 0)),
        out_shape=jax.ShapeDtypeStruct((t, D_MODEL), BF16),
        compiler_params=_params(1),
    )(x, g)


def _rope_rows():
    inv = (np.float32(ROPE_THETA) ** (-np.arange(0, ROPE, 2, dtype=np.float32) / np.float32(ROPE))).astype(np.float32)
    invf = np.zeros((1, LANE), np.float32)
    sgn = np.zeros((1, LANE), np.float32)
    invf[0, 0:32] = inv
    invf[0, 64:96] = inv
    sgn[0, 0:32] = -1.0
    sgn[0, 64:96] = 1.0
    return jnp.asarray(invf), jnp.asarray(sgn)


def _rot(v, cos_t, sin_t):
    return v * cos_t + pltpu.roll(v, 64, 1) * sin_t


def _rot_bwd(dv, cos_t, sin_t):
    return dv * cos_t + pltpu.roll(dv * sin_t, 64, 1)


def _mid_fwd(proj, flog, g_q, g_kv, bf_row, pos_col, *, tm):
    t = proj.shape[0]
    invf, sgn = _rope_rows()

    def body(p_ref, fl_ref, gq_ref, gkv_ref, bf_ref, pos_ref, invf_ref, sgn_ref,
             qn_ref, kvn_ref, kr_ref, cos_ref, sin_ref, c_ref, carry_ref):
        i = pl.program_id(0)

        @pl.when(i == 0)
        def _():
            carry_ref[...] = jnp.zeros_like(carry_ref)

        ql = p_ref[:, QL0:QL0 + Q_RANK].astype(F32)
        r = lax.rsqrt(jnp.mean(ql * ql, axis=-1, keepdims=True) + NORM_EPS)
        qn_ref[...] = (ql * r * gq_ref[...]).astype(BF16)
        kvl = p_ref[:, KVL0:KVL0 + KV_RANK].astype(F32)
        r = lax.rsqrt(jnp.mean(kvl * kvl, axis=-1, keepdims=True) + NORM_EPS)
        kvn_ref[...] = (kvl * r * gkv_ref[...]).astype(BF16)

        ang = pos_ref[...] * invf_ref[...]
        cos_t = jnp.cos(ang)
        sin_t = jnp.sin(ang) * sgn_ref[...]
        cos_ref[...] = cos_t
        sin_ref[...] = sin_t
        kr_ref[...] = _rot(p_ref[:, KR0:KR0 + LANE].astype(F32), cos_t, sin_t).astype(BF16)

        z = fl_ref[...] + bf_ref[...]
        logf = jnp.minimum(z, 0.0) - jnp.log(1.0 + jnp.exp(-jnp.abs(z)))
        row = lax.broadcasted_iota(jnp.int32, (tm, tm), 0)
        col = lax.broadcasted_iota(jnp.int32, (tm, tm), 1)
        tri = (col <= row).astype(BF16)
        acc = carry_ref[0:1, :]
        for part in _split3(logf):
            acc = acc + jnp.dot(tri, part, preferred_element_type=F32)
        c_ref[...] = acc * (1.0 / FOX_SCALE)
        carry_ref[0:1, :] = carry_ref[0:1, :] + jnp.sum(logf, axis=0, keepdims=True)

    row_spec = lambda w: pl.BlockSpec((tm, w), lambda i: (i, 0))
    vec_spec = lambda w: pl.BlockSpec((1, w), lambda i: (0, 0))
    return pl.pallas_call(
        body, name="mid_fwd", grid=(t // tm,),
        in_specs=[row_spec(LAT_W), row_spec(LANE), vec_spec(Q_RANK), vec_spec(KV_RANK), vec_spec(LANE),
                  pl.BlockSpec((tm, 1), lambda i: (i, 0)), vec_spec(LANE), vec_spec(LANE)],
        out_specs=[row_spec(Q_RANK), row_spec(KV_RANK), row_spec(LANE), row_spec(LANE), row_spec(LANE), row_spec(LANE)],
        out_shape=[jax.ShapeDtypeStruct((t, Q_RANK), BF16), jax.ShapeDtypeStruct((t, KV_RANK), BF16),
                   jax.ShapeDtypeStruct((t, LANE), BF16), jax.ShapeDtypeStruct((t, LANE), F32),
                   jax.ShapeDtypeStruct((t, LANE), F32), jax.ShapeDtypeStruct((t, LANE), F32)],
        scratch_shapes=[pltpu.VMEM((8, LANE), F32)],
        compiler_params=_params(1),
    )(proj, flog, g_q, g_kv, bf_row, pos_col, invf, sgn)


def _rope_q(q_raw, cos_t, sin_t, *, tm):
    t = q_raw.shape[0]

    def body(q_ref, cos_ref, sin_ref, o_ref):
        c, s = cos_ref[...], sin_ref[...]
        for h in range(N_HEADS):
            o_ref[:, 256 * h:256 * h + 128] = q_ref[:, 256 * h:256 * h + 128].astype(BF16)
            o_ref[:, 256 * h + 128:256 * h + 256] = _rot(q_ref[:, 256 * h + 128:256 * h + 256], c, s).astype(BF16)

    return pl.pallas_call(
        body, name="rope_q", grid=(t // tm,),
        in_specs=[pl.BlockSpec((tm, 2048), lambda i: (i, 0)), pl.BlockSpec((tm, LANE), lambda i: (i, 0)),
                  pl.BlockSpec((tm, LANE), lambda i: (i, 0))],
        out_specs=pl.BlockSpec((tm, 2048), lambda i: (i, 0)),
        out_shape=jax.ShapeDtypeStruct((t, 2048), BF16),
        compiler_params=_params(1),
    )(q_raw, cos_t, sin_t)


def _attn_fwd(fox, operands, *, t, tb, name):
    nb = t // tb
    scale = FOX_SCALE if fox else MLA_SCALE
    exp2_scale = scale * LOG2E
    pair = 2 * HEAD
    pair0 = N_HEADS // 2 if fox else 0
    q_w = HEAD if fox else 2 * HEAD
    nt_dims = (((1,), (1,)), ((), ()))
    tn_dims = (((0,), (0,)), ((), ()))

    def body(*refs):
        if fox:
            (q_ref, k_ref, v_ref, gate_ref, cq_ref, ck_ref, _, _,
             o_ref, og_ref, lse_ref, m_s, l_s, acc_s) = refs
        else:
            q_ref, kv_ref, kr_ref, gate_ref, o_ref, og_ref, lse_ref, m_s, l_s, acc_s = refs
        qi = pl.program_id(1)
        m_s[...] = jnp.full_like(m_s, -jnp.inf)
        l_s[...] = jnp.zeros_like(l_s)
        acc_s[...] = jnp.zeros_like(acc_s)

        def chunk(kc, masked):
            off = pl.multiple_of(kc * tb, tb)
            scores = []
            for u in range(2):
                q = q_ref[:, q_w * u:q_w * (u + 1)]
                if fox:
                    kk = k_ref[pl.ds(off, tb), HEAD * u:HEAD * (u + 1)]
                else:
                    kk = jnp.concatenate([kv_ref[pl.ds(off, tb), pair * u:pair * u + HEAD],
                                          kr_ref[pl.ds(off, tb), :]], axis=1)
                s = lax.dot_general(kk, q, nt_dims, preferred_element_type=F32)
                if fox:
                    s = s + cq_ref[u, 0] - ck_ref[u, pl.ds(off, tb), :]
                if masked:
                    row = lax.broadcasted_iota(jnp.int32, (tb, tb), 0)
                    col = lax.broadcasted_iota(jnp.int32, (tb, tb), 1)
                    s = jnp.where(row <= col, s, -jnp.inf)
                scores.append(s)
            for u in range(2):
                s = scores[u]
                m_prev = m_s[u]
                m_new = jnp.maximum(m_prev, jnp.max(s, axis=0, keepdims=True))
                alpha = jnp.exp2((m_prev - m_new) * exp2_scale)
                p = jnp.exp2((s - m_new) * exp2_scale)
                l_s[u] = alpha * l_s[u] + jnp.sum(p, axis=0, keepdims=True)
                if fox:
                    vv = v_ref[pl.ds(off, tb), HEAD * u:HEAD * (u + 1)]
                else:
                    vv = kv_ref[pl.ds(off, tb), pair * u + HEAD:pair * (u + 1)]
                acc_s[u] = alpha * acc_s[u] + lax.dot_general(vv, p.astype(BF16), tn_dims,
                                                              preferred_element_type=F32)
                m_s[u] = m_new

        def loop_body(kc, carry):
            chunk(kc, False)
            return carry

        lax.fori_loop(0, qi, loop_body, 0)
        chunk(qi, True)
        for u in range(2):
            cols = slice(HEAD * u, HEAD * (u + 1))
            o = (acc_s[u] / l_s[u]).T
            o_ref[:, cols] = o
            g = gate_ref[:, cols].astype(F32)
            og_ref[:, cols] = (o * (g * _sigmoid(g))).astype(BF16)
            lse_ref[u, 0] = m_s[u] * scale + jnp.log(l_s[u])

    any_spec = pl.BlockSpec(memory_space=pl.ANY)
    row_stat = pl.BlockSpec((2, 1, 1, tb), lambda g, i: (g, i, 0, 0))
    if fox:
        proj, c_col, c_row4, o_all, og_all = operands
        ins = [proj, proj, proj, proj, c_row4, c_col, o_all, og_all]
        in_specs = [pl.BlockSpec((tb, pair), lambda g, i: (i, FQ0 // pair + g)),
                    pl.BlockSpec((t, pair), lambda g, i: (0, FK0 // pair + g)),
                    pl.BlockSpec((t, pair), lambda g, i: (0, FV0 // pair + g)),
                    pl.BlockSpec((tb, pair), lambda g, i: (i, GF0 // pair + g)),
                    row_stat, pl.BlockSpec((2, t, 1), lambda g, i: (g, 0, 0)), any_spec, any_spec]
        aliases = {6: 0, 7: 1}
    else:
        q_full, kv, kr, proj = operands
        ins = [q_full, kv, kr, proj]
        in_specs = [pl.BlockSpec((tb, 2 * pair), lambda g, i: (i, g)),
                    pl.BlockSpec((t, 2 * pair), lambda g, i: (0, g)),
                    pl.BlockSpec((t, HEAD), lambda g, i: (0, 0)),
                    pl.BlockSpec((tb, pair), lambda g, i: (i, GM0 // pair + g))]
        aliases = {}
    return pl.pallas_call(
        body, name=name, grid=(N_HEADS // 2, nb), in_specs=in_specs,
        out_specs=[pl.BlockSpec((tb, pair), lambda g, i: (i, pair0 + g)),
                   pl.BlockSpec((tb, pair), lambda g, i: (i, pair0 + g)), row_stat],
        out_shape=[jax.ShapeDtypeStruct((t, 2 * N_HEADS * HEAD), F32), jax.ShapeDtypeStruct((t, 2 * N_HEADS * HEAD), BF16),
                   jax.ShapeDtypeStruct((N_HEADS, nb, 1, tb), F32)],
        scratch_shapes=[pltpu.VMEM((2, 1, tb), F32), pltpu.VMEM((2, 1, tb), F32), pltpu.VMEM((2, HEAD, tb), F32)],
        input_output_aliases=aliases,
        compiler_params=_params(2),
    )(*ins)


def _postnorm_loss(o, x, target, g, *, tm):
    t = o.shape[0]

    def body(o_ref, x_ref, t_ref, g_ref, dy_ref, do_ref, dg_ref, loss_ref):
        i = pl.program_id(0)

        @pl.when(i == 0)
        def _():
            dg_ref[...] = jnp.zeros_like(dg_ref)
            loss_ref[...] = jnp.zeros_like(loss_ref)

        ov = o_ref[...]
        gv = g_ref[...]
        r = lax.rsqrt(jnp.mean(ov * ov, axis=-1, keepdims=True) + NORM_EPS)
        oh = ov * r
        e = x_ref[...] + oh * gv - t_ref[...]
        loss_ref[...] += 0.5 * jnp.sum(jnp.mean(e * e, axis=-1, keepdims=True), axis=0, keepdims=True)
        dy = e * (1.0 / D_MODEL)
        dy_ref[...] = dy
        dyg = dy * gv
        do_ref[...] = (r * (dyg - oh * jnp.mean(dyg * oh, axis=-1, keepdims=True))).astype(BF16)
        dg_ref[...] += jnp.sum(dy * oh, axis=0, keepdims=True)

    row = pl.BlockSpec((tm, D_MODEL), lambda i: (i, 0))
    vec = pl.BlockSpec((1, D_MODEL), lambda i: (0, 0))
    return pl.pallas_call(
        body, name="postnorm_loss", grid=(t // tm,),
        in_specs=[row, row, row, vec],
        out_specs=[row, row, vec, pl.BlockSpec((1, 1), lambda i: (0, 0))],
        out_shape=[jax.ShapeDtypeStruct((t, D_MODEL), F32), jax.ShapeDtypeStruct((t, D_MODEL), BF16),
                   jax.ShapeDtypeStruct((1, D_MODEL), F32), jax.ShapeDtypeStruct((1, 1), F32)],
        compiler_params=_params(1),
    )(o, x, target, g)


def _dog_gate(d_o_post, w_out_n, o_all, proj, *, tm):
    t = d_o_post.shape[0]
    pair = 2 * HEAD
    gate_blk = GM0 // pair
    assert GM0 % pair == 0 and GF0 == GM0 + N_HEADS * HEAD

    def body(do_ref, w_ref, o_ref, p_ref, dattn_ref, dproj_ref, delta_ref):
        j = pl.program_id(1)

        @pl.when(j == 0)
        def _():
            delta_ref[...] = jnp.zeros_like(delta_ref)

        dog = lax.dot_general(do_ref[...], w_ref[...], (((1,), (1,)), ((), ())), preferred_element_type=F32)
        g = p_ref[...].astype(F32)
        ov = o_ref[...]
        sg = _sigmoid(g)
        d_o = dog * (g * sg)
        dattn_ref[...] = d_o.astype(BF16)
        dproj_ref[...] = (dog * ov * (sg * (1.0 + g * (1.0 - sg)))).astype(BF16)
        prod = d_o * ov
        lane = lax.broadcasted_iota(jnp.int32, (tm, LANE), 1)
        delta = delta_ref[...]
        for half in range(2):
            part = jnp.sum(prod[:, HEAD * half:HEAD * (half + 1)], axis=-1, keepdims=True)
            delta = jnp.where(lane == 2 * j + half, part, delta)
        delta_ref[...] = delta

    return pl.pallas_call(
        body, name="dog_gate", grid=(t // tm, N_HEADS),
        in_specs=[pl.BlockSpec((tm, D_MODEL), lambda i, j: (i, 0)), pl.BlockSpec((pair, D_MODEL), lambda i, j: (j, 0)),
                  pl.BlockSpec((tm, pair), lambda i, j: (i, j)), pl.BlockSpec((tm, pair), lambda i, j: (i, gate_blk + j))],
        out_specs=[pl.BlockSpec((tm, pair), lambda i, j: (i, j)), pl.BlockSpec((tm, pair), lambda i, j: (i, gate_blk + j)),
                   pl.BlockSpec((tm, LANE), lambda i, j: (i, 0))],
        out_shape=[jax.ShapeDtypeStruct((t, 2048), BF16), jax.ShapeDtypeStruct((t, NP_IN), BF16),
                   jax.ShapeDtypeStruct((t, LANE), F32)],
        compiler_params=_params(2),
    )(d_o_post, w_out_n, o_all, proj)


def _attn_bwd(fox, operands, *, t, tb, name, rider=None):
    nb = t // tb
    n_pairs = N_HEADS // 2
    pair = 2 * HEAD
    scale = FOX_SCALE if fox else MLA_SCALE
    q_w = HEAD if fox else 2 * HEAD
    nt_dims = (((1,), (1,)), ((), ()))
    tn_dims = (((0,), (0,)), ((), ()))
    n_rin = len(rider.operands) if rider else 0
    n_rout = len(rider.out_shape) if rider else 0
    n_in, n_out, n_scr = (9, 3, 9) if fox else (6, 3, 2)

    def body(*refs):
        ends = np.cumsum([0, n_in, n_rin, n_out, n_rout, n_scr])
        in_refs, rider_in, out_refs, rider_out, scr_refs = (refs[a:b] for a, b in zip(ends[:-1], ends[1:]))
        rider_refs = (rider_in, rider_out, refs[ends[-1]:])
        if fox:
            q_ref, k_ref, v_ref, do_ref, lse_ref, dl_ref, cq_ref, ck_ref, _ = in_refs
            dproj_ref, dck_ref, dcq_ref = out_refs
            dq_acc, dk_s, dv_s, dc_s, dcq_s, stage_q, stage_k, stage_v, put_sems = scr_refs
        else:
            q_ref, kv_ref, kr_ref, do_ref, lse_ref, dl_ref = in_refs
            dq_acc, dkv_ref, dkr_ref = out_refs
            dk_s, dv_s = scr_refs
        g = pl.program_id(0)
        ki = pl.program_id(1)
        if rider:
            @pl.when(jnp.logical_and(g == 0, ki == 0))
            def _():
                rider.start(*rider_refs)

        @pl.when(ki == 0)
        def _():
            dq_acc[...] = jnp.zeros_like(dq_acc)
            if fox:
                dcq_s[...] = jnp.zeros_like(dcq_s)

        dk_s[...] = jnp.zeros_like(dk_s)
        dv_s[...] = jnp.zeros_like(dv_s)
        if fox:
            dc_s[...] = jnp.zeros_like(dc_s)
            keys = [k_ref[:, HEAD * u:HEAD * (u + 1)] for u in range(2)]
            vals = [v_ref[:, HEAD * u:HEAD * (u + 1)] for u in range(2)]
        else:
            keys = [jnp.concatenate([kv_ref[:, pair * u:pair * u + HEAD], kr_ref[...]], axis=1) for u in range(2)]
            vals = [kv_ref[:, pair * u + HEAD:pair * (u + 1)] for u in range(2)]

        def chunk(qc, masked):
            off = pl.multiple_of(qc * tb, tb)
            for u in range(2):
                kk, vv = keys[u], vals[u]
                qq = q_ref[pl.ds(off, tb), q_w * u:q_w * (u + 1)]
                dd = do_ref[pl.ds(off, tb), HEAD * u:HEAD * (u + 1)]
                s = lax.dot_general(kk, qq, nt_dims, preferred_element_type=F32)
                if fox:
                    s = s + cq_ref[u, qc] - ck_ref[u]
                if masked:
                    row = lax.broadcasted_iota(jnp.int32, (tb, tb), 0)
                    col = lax.broadcasted_iota(jnp.int32, (tb, tb), 1)
                    s = jnp.where(row <= col, s, -jnp.inf)
                p = jnp.exp2(s * (scale * LOG2E) - lse_ref[u, qc] * LOG2E)
                dv_s[u] += jnp.dot(p.astype(BF16), dd, preferred_element_type=F32)
                dp = lax.dot_general(vv, dd, nt_dims, preferred_element_type=F32)
                ds = p * (dp - dl_ref[u, qc])
                if fox:
                    dc_s[u] += jnp.sum(ds, axis=1, keepdims=True)
                    dcq_s[u, qc] += jnp.sum(ds, axis=0, keepdims=True)
                dsb = (ds * scale).astype(BF16)
                dk_s[u] += jnp.dot(dsb, qq, preferred_element_type=F32)
                dq_acc[pl.ds(off, tb), q_w * u:q_w * (u + 1)] += lax.dot_general(dsb, kk, tn_dims,
                                                                                 preferred_element_type=F32)

        chunk(ki, True)

        def loop_body(qc, carry):
            chunk(qc, False)
            return carry

        lax.fori_loop(ki + 1, nb, loop_body, 0)

        def put(stage_ref, rows, seg0, sem):
            col0 = pl.multiple_of(seg0 + g * pair, pair)
            return pltpu.make_async_copy(stage_ref, dproj_ref.at[rows, pl.ds(col0, pair)], sem)

        if fox:
            rows = pl.ds(pl.multiple_of(ki * tb, tb), tb)
            block_puts = [put(stage_k, rows, FK0, put_sems.at[1]), put(stage_v, rows, FV0, put_sems.at[2])]
            pair_put = put(stage_q, pl.ds(0, t), FQ0, put_sems.at[0])

            @pl.when(jnp.logical_or(g > 0, ki > 0))
            def _():
                for cp in block_puts:
                    cp.wait()

            for u in range(2):
                stage_k[:, HEAD * u:HEAD * (u + 1)] = dk_s[u].astype(BF16)
                stage_v[:, HEAD * u:HEAD * (u + 1)] = dv_s[u].astype(BF16)
                dck_ref[u] = -dc_s[u]
            for cp in block_puts:
                cp.start()

            @pl.when(ki == nb - 1)
            def _():
                @pl.when(g > 0)
                def _():
                    pair_put.wait()

                stage_q[...] = dq_acc[...].astype(BF16)
                pair_put.start()
                dcq_ref[...] = dcq_s[...]

            @pl.when(jnp.logical_and(g == n_pairs - 1, ki == nb - 1))
            def _():
                for cp in block_puts + [pair_put]:
                    cp.wait()
        else:
            dkv_ref[...] = jnp.concatenate([dk_s[0, :, :HEAD], dv_s[0], dk_s[1, :, :HEAD], dv_s[1]], axis=1).astype(BF16)
            dkr_ref[...] = jnp.concatenate([dk_s[0, :, HEAD:], dk_s[1, :, HEAD:]], axis=1)

        if rider:
            @pl.when(jnp.logical_and(g == n_pairs - 1, ki == nb - 1))
            def _():
                rider.wait(*rider_refs)

    stat = pl.BlockSpec((2, nb, 1, tb), lambda g, i: (g, 0, 0, 0))
    aliases = {}
    if fox:
        proj, d_o, lse4, delta4, c_row4, c_col, dproj = operands
        ins = [proj, proj, proj, d_o, lse4, delta4, c_row4, c_col, dproj]
        any_spec = pl.BlockSpec(memory_space=pl.ANY)
        in_specs = [pl.BlockSpec((t, pair), lambda g, i: (0, FQ0 // pair + g)),
                    pl.BlockSpec((tb, pair), lambda g, i: (i, FK0 // pair + g)),
                    pl.BlockSpec((tb, pair), lambda g, i: (i, FV0 // pair + g)),
                    pl.BlockSpec((t, pair), lambda g, i: (0, n_pairs + g)),
                    stat, stat, stat, pl.BlockSpec((2, tb, 1), lambda g, i: (g, i, 0)), any_spec]
        aliases = {8: 0}
        out_specs = [any_spec, pl.BlockSpec((2, tb, 1), lambda g, i: (g, i, 0)), stat]
        out_shape = [jax.ShapeDtypeStruct(dproj.shape, dproj.dtype), jax.ShapeDtypeStruct((N_HEADS, t, 1), F32),
                     jax.ShapeDtypeStruct((N_HEADS, nb, 1, tb), F32)]
        scratch = [pltpu.VMEM((t, pair), F32), pltpu.VMEM((2, tb, HEAD), F32), pltpu.VMEM((2, tb, HEAD), F32),
                   pltpu.VMEM((2, tb, 1), F32), pltpu.VMEM((2, nb, 1, tb), F32),
                   pltpu.VMEM((t, pair), BF16), pltpu.VMEM((tb, pair), BF16), pltpu.VMEM((tb, pair), BF16),
                   pltpu.SemaphoreType.DMA((3,))]
    else:
        q_full, kv, kr, d_o, lse4, delta4 = operands
        ins = [q_full, kv, kr, d_o, lse4, delta4]
        in_specs = [pl.BlockSpec((t, 2 * pair), lambda g, i: (0, g)),
                    pl.BlockSpec((tb, 2 * pair), lambda g, i: (i, g)),
                    pl.BlockSpec((tb, HEAD), lambda g, i: (i, 0)),
                    pl.BlockSpec((t, pair), lambda g, i: (0, g)),
                    stat, stat]
        out_specs = [pl.BlockSpec((t, 2 * pair), lambda g, i: (0, g)), pl.BlockSpec((tb, 2 * pair), lambda g, i: (i, g)),
                     pl.BlockSpec((tb, pair), lambda g, i: (i, g))]
        out_shape = [jax.ShapeDtypeStruct((t, 2048), F32), jax.ShapeDtypeStruct((t, 2048), BF16),
                     jax.ShapeDtypeStruct((t, 1024), F32)]
        scratch = [pltpu.VMEM((2, tb, 2 * HEAD), F32), pltpu.VMEM((2, tb, HEAD), F32)]
    assert (len(ins), len(out_specs), len(scratch)) == (n_in, n_out, n_scr)
    if rider:
        any_spec = pl.BlockSpec(memory_space=pl.ANY)
        aliases = {**aliases, **{n_in + i_in: n_out + i_out for i_in, i_out in rider.aliases.items()}}
        ins = ins + list(rider.operands)
        in_specs = in_specs + [any_spec] * n_rin
        out_specs = out_specs + [any_spec] * n_rout
        out_shape = out_shape + list(rider.out_shape)
        scratch = scratch + list(rider.scratch)
    return pl.pallas_call(
        body, name=name, grid=(n_pairs, nb), in_specs=in_specs, out_specs=out_specs, out_shape=out_shape,
        scratch_shapes=scratch, input_output_aliases=aliases, compiler_params=_params(2),
    )(*ins)


def _mid_bwd(dq_full, dkr, cos_t, sin_t, dck, flog, bf_row, *, tm):
    t = dq_full.shape[0]
    n = t // tm

    def body(dq_ref, dkr_ref, cos_ref, sin_ref, dck_ref, fl_ref, bf_ref,
             dq2_ref, dkraw_ref, dfl_ref, dbf_ref, carry_ref):
        i = pl.program_id(0)

        @pl.when(i == 0)
        def _():
            carry_ref[...] = jnp.zeros_like(carry_ref)
            dbf_ref[...] = jnp.zeros_like(dbf_ref)

        c, s = cos_ref[...], sin_ref[...]
        dkr_sum = jnp.zeros((tm, LANE), F32)
        for h in range(N_HEADS):
            dq2_ref[:, 256 * h:256 * h + 128] = dq_ref[:, 256 * h:256 * h + 128].astype(BF16)
            dq2_ref[:, 256 * h + 128:256 * h + 256] = _rot_bwd(dq_ref[:, 256 * h + 128:256 * h + 256], c, s).astype(BF16)
            dkr_sum = dkr_sum + dkr_ref[:, HEAD * h:HEAD * (h + 1)]
        dkraw_ref[...] = _rot_bwd(dkr_sum, c, s).astype(BF16)

        dc = dck_ref[...]
        row = lax.broadcasted_iota(jnp.int32, (tm, tm), 0)
        col = lax.broadcasted_iota(jnp.int32, (tm, tm), 1)
        tri = (col >= row).astype(BF16)
        acc = carry_ref[0:1, :]
        for part in _split3(dc):
            acc = acc + jnp.dot(tri, part, preferred_element_type=F32)
        carry_ref[0:1, :] = carry_ref[0:1, :] + jnp.sum(dc, axis=0, keepdims=True)
        z = fl_ref[...] + bf_ref[...]
        dz = acc / (1.0 + jnp.exp(z))
        dfl_ref[...] = dz.astype(BF16)
        dbf_ref[...] += jnp.sum(dz, axis=0, keepdims=True)

    rev = lambda w: pl.BlockSpec((tm, w), lambda i: (n - 1 - i, 0))
    vec = lambda w: pl.BlockSpec((1, w), lambda i: (0, 0))
    return pl.pallas_call(
        body, name="mid_bwd", grid=(n,),
        in_specs=[rev(2048), rev(1024), rev(LANE), rev(LANE), rev(LANE), rev(LANE), vec(LANE)],
        out_specs=[rev(2048), rev(LANE), rev(LANE), vec(LANE)],
        out_shape=[jax.ShapeDtypeStruct((t, 2048), BF16), jax.ShapeDtypeStruct((t, LANE), BF16),
                   jax.ShapeDtypeStruct((t, LANE), BF16), jax.ShapeDtypeStruct((1, LANE), F32)],
        scratch_shapes=[pltpu.VMEM((8, LANE), F32)],
        compiler_params=_params(1),
    )(dq_full, dkr, cos_t, sin_t, dck, flog, bf_row)


def _norm_bwd(proj, dqn, dkvn, g_q, g_kv, dkr_raw, dfl, dproj, *, tm):
    t = proj.shape[0]
    assert (KR0, FL0, KVL0, LAT_W) == (Q_RANK, Q_RANK + LANE, Q_RANK + 2 * LANE, Q_RANK + 2 * LANE + KV_RANK)

    def body(p_ref, dqn_ref, dkvn_ref, gq_ref, gkv_ref, dkr_ref, dfl_ref, _, dproj_ref, dgq_ref, dgkv_ref):
        i = pl.program_id(0)

        @pl.when(i == 0)
        def _():
            dgq_ref[...] = jnp.zeros_like(dgq_ref)
            dgkv_ref[...] = jnp.zeros_like(dgkv_ref)

        d_lat = []
        for lo, w, dn_ref, g_ref, dg_ref in ((QL0, Q_RANK, dqn_ref, gq_ref, dgq_ref),
                                             (KVL0, KV_RANK, dkvn_ref, gkv_ref, dgkv_ref)):
            xv = p_ref[:, lo:lo + w].astype(F32)
            r = lax.rsqrt(jnp.mean(xv * xv, axis=-1, keepdims=True) + NORM_EPS)
            xh = xv * r
            dn = dn_ref[...]
            dg_ref[...] += jnp.sum(dn * xh, axis=0, keepdims=True)
            dxh = dn * g_ref[...]
            d_lat.append((r * (dxh - xh * jnp.mean(dxh * xh, axis=-1, keepdims=True))).astype(BF16))
        dproj_ref[...] = jnp.concatenate([d_lat[0], dkr_ref[...], dfl_ref[...], d_lat[1]], axis=1)

    row = lambda w: pl.BlockSpec((tm, w), lambda i: (i, 0))
    vec = lambda w: pl.BlockSpec((1, w), lambda i: (0, 0))
    return pl.pallas_call(
        body, name="norm_bwd", grid=(t // tm,),
        in_specs=[row(LAT_W), row(Q_RANK), row(KV_RANK), vec(Q_RANK), vec(KV_RANK), row(LANE), row(LANE),
                  pl.BlockSpec(memory_space=pl.ANY)],
        out_specs=[row(LAT_W), vec(Q_RANK), vec(KV_RANK)],
        out_shape=[jax.ShapeDtypeStruct(dproj.shape, dproj.dtype),
                   jax.ShapeDtypeStruct((1, Q_RANK), F32), jax.ShapeDtypeStruct((1, KV_RANK), F32)],
        input_output_aliases={7: 0},
        compiler_params=_params(1),
    )(proj, dqn, dkvn, g_q, g_kv, dkr_raw, dfl, dproj)


def _prenorm_bwd(dh, x, g, dy, *, tm):
    t = x.shape[0]

    def body(dh_ref, x_ref, g_ref, dy_ref, gx_ref, dg_ref):
        i = pl.program_id(0)

        @pl.when(i == 0)
        def _():
            dg_ref[...] = jnp.zeros_like(dg_ref)

        xv = x_ref[...]
        r = lax.rsqrt(jnp.mean(xv * xv, axis=-1, keepdims=True) + NORM_EPS)
        xh = xv * r
        dn = dh_ref[...]
        dg_ref[...] += jnp.sum(dn * xh, axis=0, keepdims=True)
        dxh = dn * g_ref[...]
        gx_ref[...] = dy_ref[...] + r * (dxh - xh * jnp.mean(dxh * xh, axis=-1, keepdims=True))

    row = pl.BlockSpec((tm, D_MODEL), lambda i: (i, 0))
    vec = pl.BlockSpec((1, D_MODEL), lambda i: (0, 0))
    return pl.pallas_call(
        body, name="prenorm_bwd", grid=(t // tm,),
        in_specs=[row, row, vec, row], out_specs=[row, vec],
        out_shape=[jax.ShapeDtypeStruct((t, D_MODEL), F32), jax.ShapeDtypeStruct((1, D_MODEL), F32)],
        compiler_params=_params(1),
    )(dh, x, g, dy)


def _adam_math(w, g, m, v):
    m = ADAM_B1 * m + (1.0 - ADAM_B1) * g
    v = ADAM_B2 * v + (1.0 - ADAM_B2) * (g * g)
    m_hat = m / (1.0 - ADAM_B1 ** ADAM_STEP)
    v_hat = v / (1.0 - ADAM_B2 ** ADAM_STEP)
    delta = -ADAM_LR * (m_hat / (jnp.sqrt(v_hat) + ADAM_EPS) + ADAM_WD * w)
    return delta, m, v


def _adamw(land, w, m, v, *, tr, name):
    rows, cols = w.shape

    def body(l_ref, w_ref, m_ref, v_ref, g_ref, d_ref, nm_ref, nv_ref):
        g = l_ref[0].astype(F32)
        for s in range(1, N_CHIPS):
            g = g + l_ref[s].astype(F32)
        g_ref[...] = g
        d_ref[...], nm_ref[...], nv_ref[...] = _adam_math(w_ref[...], g, m_ref[...], v_ref[...])

    blk = pl.BlockSpec((tr, cols), lambda i: (i, 0))
    return pl.pallas_call(
        body, name=name, grid=(rows // tr,),
        in_specs=[pl.BlockSpec((N_CHIPS, tr, cols), lambda i: (0, i, 0)), blk, blk, blk],
        out_specs=[blk, blk, blk, blk],
        out_shape=[jax.ShapeDtypeStruct((rows, cols), F32)] * 4,
        compiler_params=_params(1),
    )(land, w, m, v)


def _adamw_small(gathered, w, m, v):
    def body(a_ref, w_ref, m_ref, v_ref, g_ref, d_ref, nm_ref, nv_ref):
        g = a_ref[0:SMALL_ROWS, :]
        for s in range(1, N_DEV):
            g = g + a_ref[SMALL_ROWS * s:SMALL_ROWS * (s + 1), :]
        g_ref[...] = g
        d_ref[...], nm_ref[...], nv_ref[...] = _adam_math(w_ref[...], g, m_ref[...], v_ref[...])

    return pl.pallas_call(
        body, name="adamw_small",
        out_shape=[jax.ShapeDtypeStruct((SMALL_ROWS, SMALL_COLS), F32)] * 4,
        compiler_params=_params(),
    )(gathered, w, m, v)


def _place():
    x, y, c = lax.axis_index("x"), lax.axis_index("y"), lax.axis_index("c")
    return x, y, c


def _flip(p, k):
    x, y, c = p
    return (1 - x if k & 4 else x, 1 - y if k & 2 else y, 1 - c if k & 1 else c)


def _index(p):
    return 4 * p[0] + 2 * p[1] + p[2]


def _all_gather(shards):
    n = len(shards)
    hbm = pl.BlockSpec(memory_space=pl.ANY)

    def body(*refs):
        ins, outs = refs[:n], refs[n:2 * n]
        send_sems, recv_sems, local_sems = refs[2 * n:]
        me = _place()
        sibling = _flip(me, 1)
        chips = [_flip(me, 4), _flip(me, 2), _flip(me, 6)]

        def copy(a, k, block, to, src=None):
            dst = outs[a].at[_index(block)]
            return pltpu.make_async_remote_copy(
                src_ref=dst if src is None else src, dst_ref=dst,
                send_sem=send_sems.at[7 * a + k], recv_sem=recv_sems.at[7 * a + k],
                device_id=to, device_id_type=MESH)

        started = []
        for a in range(n):
            mine = pltpu.make_async_copy(ins[a], outs[a].at[_index(me)], local_sems.at[a])
            mine.start()
            started.append(mine)
        first = []
        for a in range(n):
            first.append(copy(a, 0, me, sibling, src=ins[a]))
            first += [copy(a, 1 + j, me, chip, src=ins[a]) for j, chip in enumerate(chips)]
        for cp in first:
            cp.start()
        passed = []
        for a in range(n):
            for j, chip in enumerate(chips):
                copy(a, 1 + j, chip, me).wait_recv()
                fwd = copy(a, 4 + j, chip, sibling)
                fwd.start()
                passed.append(fwd)
        for a in range(n):
            copy(a, 0, sibling, me).wait_recv()
            for j, chip in enumerate(chips):
                copy(a, 4 + j, _flip(chip, 1), me).wait_recv()
        for cp in first + passed:
            cp.wait_send()
        for mine in started:
            mine.wait()

    return pl.pallas_call(
        body, name="all_gather_weights",
        in_specs=[hbm] * n, out_specs=[hbm] * n,
        out_shape=[jax.ShapeDtypeStruct((N_DEV,) + s.shape, s.dtype) for s in shards],
        scratch_shapes=[pltpu.SemaphoreType.DMA((7 * n,)), pltpu.SemaphoreType.DMA((7 * n,)),
                        pltpu.SemaphoreType.DMA((n,))],
    )(*shards)


class _Exchange:
    def __init__(self, tasks):
        self.tasks = tasks
        taken = [land for _, _, land, _, _ in tasks if land is not None]
        self.operands = [src for src, _, _, _, _ in tasks] + taken
        self.out_shape = [
            jax.ShapeDtypeStruct((N_CHIPS,) + ((2,) if by_core else ()) + (src.shape if same else src.shape[1:]), src.dtype)
            for src, _, _, same, by_core in tasks]
        self.aliases, n_taken = {}, 0
        for a, (_, _, land, _, _) in enumerate(tasks):
            if land is not None:
                self.aliases[len(tasks) + n_taken] = a
                n_taken += 1
        self.scratch = [pltpu.SemaphoreType.DMA((N_CHIPS,)), pltpu.SemaphoreType.DMA((N_CHIPS,)),
                        pltpu.SemaphoreType.DMA(())] * len(tasks)

    def _copies(self, ins, outs, scratch):
        x, y, core = _place()
        my = 2 * x + y
        for a, (_, chips, _, same, by_core) in enumerate(self.tasks):
            send_sems, recv_sems, local_sem = scratch[3 * a:3 * a + 3]
            slot = (lambda s, a=a, by_core=by_core: outs[a].at[s, core] if by_core else outs[a].at[s])
            for i, j in enumerate(chips):
                src = ins[a] if same else ins[a].at[i]
                pair = jnp.bitwise_xor(my, j)
                remote = pltpu.make_async_remote_copy(
                    src_ref=src, dst_ref=slot(my), send_sem=send_sems.at[pair], recv_sem=recv_sems.at[pair],
                    device_id=(j >> 1, j & 1, core), device_id_type=MESH)
                local = pltpu.make_async_copy(src, slot(my), local_sem)
                yield j, my, core, remote, local, slot, (send_sems, recv_sems)

    def start(self, ins, outs, scratch):
        for j, my, _, remote, local, _, _ in self._copies(ins, outs, scratch):
            pl.when(my != j)(remote.start)
            pl.when(my == j)(local.start)

    def wait(self, ins, outs, scratch):
        for j, my, core, remote, local, slot, (send_sems, recv_sems) in self._copies(ins, outs, scratch):
            pl.when(my != j)(remote.wait_send)

            @pl.when(my == j)
            def _():
                local.wait()
                for s in range(N_CHIPS):
                    if s != j:
                        pltpu.make_async_remote_copy(
                            src_ref=slot(s), dst_ref=slot(s), send_sem=send_sems.at[j ^ s], recv_sem=recv_sems.at[j ^ s],
                            device_id=(s >> 1, s & 1, core), device_id_type=MESH).wait_recv()


N_CHIPS = 4
ALL_CHIPS = tuple(range(N_CHIPS))


def _to_other_core(parts, *, name):
    n_arr = len(parts)
    hbm = pl.BlockSpec(memory_space=pl.ANY)

    def body(*refs):
        srcs, lands = refs[:n_arr], refs[n_arr:2 * n_arr]
        send_sems, recv_sems = refs[2 * n_arr:]
        me = _place()
        copies = [pltpu.make_async_remote_copy(src_ref=srcs[a].at[1 - me[2]], dst_ref=lands[a], send_sem=send_sems.at[a],
                                               recv_sem=recv_sems.at[a], device_id=_flip(me, 1), device_id_type=MESH)
                  for a in range(n_arr)]
        for cp in copies:
            cp.start()
        for cp in copies:
            cp.wait()

    return pl.pallas_call(
        body, name=name, in_specs=[hbm] * n_arr, out_specs=[hbm] * n_arr,
        out_shape=[jax.ShapeDtypeStruct(p.shape[1:], p.dtype) for p in parts],
        scratch_shapes=[pltpu.SemaphoreType.DMA((n_arr,)), pltpu.SemaphoreType.DMA((n_arr,))],
    )(*parts)


def _share_with_other_core(gathered, *, name):
    n_arr = len(gathered)
    hbm = pl.BlockSpec(memory_space=pl.ANY)

    def body(*refs):
        bufs = refs[n_arr:2 * n_arr]
        send_sems, recv_sems = refs[2 * n_arr:]
        me = _place()
        copies = []
        for a in range(n_arr):
            for j in range(N_CHIPS):
                block = bufs[a].at[j, me[2]]
                copies.append(pltpu.make_async_remote_copy(
                    src_ref=block, dst_ref=block, send_sem=send_sems.at[N_CHIPS * a + j],
                    recv_sem=recv_sems.at[N_CHIPS * a + j], device_id=_flip(me, 1), device_id_type=MESH))
        for cp in copies:
            cp.start()
        for cp in copies:
            cp.wait()

    return pl.pallas_call(
        body, name=name, in_specs=[hbm] * n_arr, out_specs=[hbm] * n_arr,
        out_shape=[jax.ShapeDtypeStruct(g.shape, g.dtype) for g in gathered],
        scratch_shapes=[pltpu.SemaphoreType.DMA((N_CHIPS * n_arr,)), pltpu.SemaphoreType.DMA((N_CHIPS * n_arr,))],
        input_output_aliases={a: a for a in range(n_arr)},
    )(*gathered)


def _pair_sum(mine, other, core, *, tr, name):
    _, n, rows, cols = mine.shape
    tr = min(tr, rows)

    def body(core_ref, a_ref, b_ref, o_ref):
        o_ref[...] = (a_ref[0].astype(F32) + b_ref[...].astype(F32)).astype(BF16)

    return pl.pallas_call(
        body, name=name,
        grid_spec=pltpu.PrefetchScalarGridSpec(
            num_scalar_prefetch=1, grid=(n, rows // tr),
            in_specs=[pl.BlockSpec((1, 1, tr, cols), lambda j, i, core_ref: (core_ref[0], j, i, 0)),
                      pl.BlockSpec((1, tr, cols), lambda j, i, core_ref: (j, i, 0))],
            out_specs=pl.BlockSpec((1, tr, cols), lambda j, i, core_ref: (j, i, 0))),
        out_shape=jax.ShapeDtypeStruct(other.shape, BF16),
        compiler_params=_params(2),
    )(core, mine, other)


def _gather_small(vec):
    def body(v_ref, out_ref, send_sems, recv_sems, local_sem):
        me = _place()

        def rows(p):
            return out_ref.at[pl.ds(pl.multiple_of(_index(p) * SMALL_ROWS, SMALL_ROWS), SMALL_ROWS), :]

        mine = pltpu.make_async_copy(v_ref, rows(me), local_sem)
        mine.start()
        sends = []
        for k in range(1, N_DEV):
            peer = _flip(me, k)
            cp = pltpu.make_async_remote_copy(src_ref=v_ref, dst_ref=rows(me), send_sem=send_sems.at[k - 1],
                                              recv_sem=recv_sems.at[k - 1], device_id=peer, device_id_type=MESH)
            cp.start()
            sends.append(cp)
        for k in range(1, N_DEV):
            peer = _flip(me, k)
            pltpu.make_async_remote_copy(src_ref=rows(peer), dst_ref=rows(peer), send_sem=send_sems.at[k - 1],
                                         recv_sem=recv_sems.at[k - 1], device_id=peer, device_id_type=MESH).wait_recv()
        for cp in sends:
            cp.wait_send()
        mine.wait()

    return pl.pallas_call(
        body, name="gather_small",
        in_specs=[pl.BlockSpec(memory_space=pltpu.VMEM)], out_specs=pl.BlockSpec(memory_space=pltpu.VMEM),
        out_shape=jax.ShapeDtypeStruct((N_DEV * SMALL_ROWS, SMALL_COLS), F32),
        scratch_shapes=[pltpu.SemaphoreType.DMA((7,)), pltpu.SemaphoreType.DMA((7,)), pltpu.SemaphoreType.DMA],
    )(vec)


def _w_in_nice(gathered):
    pieces, pos = [], 0
    for o0, width, n0 in sorted(_SEGMENTS, key=lambda seg: seg[2]):
        if n0 > pos:
            pieces.append(jnp.zeros((D_MODEL, n0 - pos), gathered.dtype))
        o = o0
        while o < o0 + width:
            d = o // SHARD_IN
            hi = min(o0 + width, (d + 1) * SHARD_IN)
            pieces.append(gathered[d][:, o - d * SHARD_IN:hi - d * SHARD_IN])
            o = hi
        pos = n0 + width
    pieces.append(jnp.zeros((D_MODEL, NP_IN - pos), gathered.dtype))
    return jnp.concatenate(pieces, axis=1)


def _w_in_blocks(chips, dw_lat, dw_rest):
    blocks = []
    for core in range(2):
        for chip in chips:
            lo = (2 * chip + core) * SHARD_IN
            runs = []
            for o0, width, n0 in _SEGMENTS:
                a, b = max(lo, o0), min(lo + SHARD_IN, o0 + width)
                if a < b:
                    n_a, n_b = n0 + a - o0, n0 + b - o0
                    runs.append(dw_lat[:, n_a:n_b] if n_b <= LAT_W else dw_rest[:, n_a - LAT_W:n_b - LAT_W])
            blocks.append(jnp.concatenate(runs, axis=1))
    return jnp.stack(blocks).reshape(2, len(chips), D_MODEL, SHARD_IN)


def _by_core(shards):
    return shards.reshape((N_CHIPS, 2) + shards.shape[1:]).swapaxes(0, 1)


EARLY_CHIPS = (1, 2)
LATE_CHIPS = (0, 3)


def _w_uq_nice(shard):
    z = jnp.zeros((Q_RANK, 32), shard.dtype)
    return jnp.concatenate([shard[:, :128], shard[:, 128:160], z, shard[:, 160:192], z], axis=1)


def _pack_small(g_pre, g_post, g_q, g_kv, b_f, extra=None):
    parts = [g_pre.reshape(-1), g_post.reshape(-1), g_q.reshape(-1), g_kv.reshape(-1), b_f.reshape(-1)]
    if extra is not None:
        parts.append(extra.reshape(-1))
    flat = jnp.concatenate(parts)
    flat = jnp.concatenate([flat, jnp.zeros((SMALL_ROWS * SMALL_COLS - flat.shape[0],), F32)])
    return flat.reshape(SMALL_ROWS, SMALL_COLS)


def _unpack_small(packed):
    flat = packed.reshape(-1)
    o = 0
    out = []
    for n in (D_MODEL, D_MODEL, Q_RANK, KV_RANK, N_HEADS):
        out.append(flat[o:o + n].reshape(1, n))
        o += n
    return out, flat[o]


def kernel(x, positions, g_pre, w_in, g_q_latent, w_uq, g_kv_latent, w_ukv, b_forget, w_out, g_post, loss_target, m_g_pre, m_w_in, m_g_q_latent, m_w_uq, m_g_kv_latent, m_w_ukv, m_b_forget, m_w_out, m_g_post, v_g_pre, v_w_in, v_g_q_latent, v_w_uq, v_g_kv_latent, v_w_ukv, v_b_forget, v_w_out, v_g_post):
    t = x.shape[1]
    tb = min(512, t)
    tm = min(256, t)
    nb = t // tb
    x2 = x.reshape(t, D_MODEL)
    target = loss_target.reshape(t, D_MODEL)
    pos_col = positions.reshape(t, 1).astype(F32)
    bf_row = jnp.concatenate([b_forget.reshape(1, N_HEADS), jnp.zeros((1, LANE - N_HEADS), F32)], axis=1)

    (g_in,) = _all_gather([w_in[0].astype(BF16)])
    w_in_n = _w_in_nice(g_in)
    gather_rest = _Exchange([(w, ALL_CHIPS, None, True, True) for w in
                             (_w_uq_nice(w_uq[0].astype(BF16)), w_ukv[0].astype(BF16), w_out[0].astype(BF16))])
    core = lax.axis_index("c").astype(jnp.int32).reshape(1)

    h = _prenorm(x2, g_pre, tm=tm)
    proj, g_uq, g_ukv, g_out = _mm(h, w_in_n, name="proj_in", out_dtype=BF16, tm=2048, tn=512, tk=2048,
                                   rider=gather_rest)
    g_uq, g_ukv, g_out = _share_with_other_core([g_uq, g_ukv, g_out], name="share_weights")
    w_uq_n = g_uq.reshape(N_DEV, Q_RANK, 256).transpose(1, 0, 2).reshape(Q_RANK, N_HEADS * 256)
    w_ukv_n = g_ukv.reshape(N_DEV, KV_RANK, 256).transpose(1, 0, 2).reshape(KV_RANK, N_HEADS * 256)
    w_out_n = g_out.reshape(D_MODEL, D_MODEL)
    flog = _mm(h, w_in_n[:, FL0:FL0 + LANE], name="proj_flog", out_dtype=F32, tm=1024, tn=LANE, tk=2048)
    qn, kvn, kr, cos_t, sin_t, c = _mid_fwd(proj, flog, g_q_latent, g_kv_latent, bf_row, pos_col, tm=tm)
    q_raw = _mm(qn, w_uq_n, name="q_up", out_dtype=F32, tm=1024, tn=512, tk=Q_RANK)
    q_full = _rope_q(q_raw, cos_t, sin_t, tm=tm)
    kv = _mm(kvn, w_ukv_n, name="kv_up", out_dtype=BF16, tm=1024, tn=512, tk=KV_RANK)
    c_heads = c[:, :N_HEADS].T
    c_col = c_heads.reshape(N_HEADS, t, 1)
    c_row4 = c_heads.reshape(N_HEADS, nb, 1, tb)
    o_all, og_all, lse4_mla = _attn_fwd(False, (q_full, kv, kr, proj), t=t, tb=tb, name="mla_fwd")
    o_all, og_all, lse4_fox = _attn_fwd(True, (proj, c_col, c_row4, o_all, og_all), t=t, tb=tb, name="fox_fwd")
    o = _mm(og_all, w_out_n, name="out_proj", out_dtype=F32, tm=1024, tn=512, tk=2048)
    dy, d_o_post, dg_post, loss_part = _postnorm_loss(o, x2, target, g_post, tm=tm)

    dw_out = _mm(og_all, d_o_post, name="dw_out", ta=True, out_dtype=BF16, tm=1024, tn=1024, tk=512)
    p_out = _by_core(dw_out.reshape(N_DEV, D_MODEL // N_DEV, D_MODEL))
    (o_out,) = _to_other_core([p_out], name="dw_out_to_core")
    s_out = _pair_sum(p_out, o_out, core, tr=256, name="dw_out_pair_sum")
    d_attn, dproj, delta = _dog_gate(d_o_post, w_out_n, o_all, proj, tm=min(1024, t))
    delta4 = delta[:, :2 * N_HEADS].T.reshape(2 * N_HEADS, nb, 1, tb)
    dproj, dck, dcq, l_out = _attn_bwd(True, (proj, d_attn, lse4_fox, delta4[N_HEADS:], c_row4, c_col, dproj),
                                       t=t, tb=tb, name="fox_bwd",
                                       rider=_Exchange([(s_out, ALL_CHIPS, None, False, False)]))
    dw_in_rest = _mm(h, dproj, name="dw_in_rest", ta=True, out_dtype=BF16, tm=2048, tn=512, tk=1024,
                     b_cols=(LAT_W, NP_IN - LAT_W))
    p_in = _w_in_blocks(EARLY_CHIPS, None, dw_in_rest)
    (o_in,) = _to_other_core([p_in], name="dw_in_early_to_core")
    s_in = _pair_sum(p_in, o_in, core, tr=256, name="dw_in_early_pair_sum")
    dq_full, dkv, dkr, l_in = _attn_bwd(False, (q_full, kv, kr, d_attn, lse4_mla, delta4[:N_HEADS]),
                                        t=t, tb=tb, name="mla_bwd",
                                        rider=_Exchange([(s_in, EARLY_CHIPS, None, False, False)]))
    dc_heads = dck.reshape(N_HEADS, t) + dcq.reshape(N_HEADS, t)
    dck_rows = jnp.concatenate([dc_heads.T, jnp.zeros((t, LANE - N_HEADS), F32)], axis=1)
    dq2, dkr_raw, dfl, dbf = _mid_bwd(dq_full, dkr, cos_t, sin_t, dck_rows, flog, bf_row, tm=tm)
    dqn = _mm(dq2, w_uq_n, name="d_qn", nt=True, out_dtype=F32, tm=1024, tn=Q_RANK, tk=2048)
    dkvn = _mm(dkv, w_ukv_n, name="d_kvn", nt=True, out_dtype=F32, tm=1024, tn=KV_RANK, tk=2048)
    dw_uq = _mm(qn, dq2, name="dw_uq", ta=True, out_dtype=BF16, tm=Q_RANK, tn=1024, tk=512)
    dw_ukv = _mm(kvn, dkv, name="dw_ukv", ta=True, out_dtype=BF16, tm=KV_RANK, tn=1024, tk=512)
    dproj, dg_q, dg_kv = _norm_bwd(proj, dqn, dkvn, g_q_latent, g_kv_latent, dkr_raw, dfl, dproj, tm=tm)
    dw_in_lat = _mm(h, dproj, name="dw_in_lat", ta=True, out_dtype=BF16, tm=1024, tn=LAT_W, tk=512, b_cols=(0, LAT_W))
    dw_uq_h = dw_uq.reshape(Q_RANK, N_HEADS, 256)
    s_uq = jnp.concatenate([dw_uq_h[:, :, :160], dw_uq_h[:, :, 192:224]], axis=2).transpose(1, 0, 2)
    s_ukv = dw_ukv.reshape(KV_RANK, N_HEADS, 256).transpose(1, 0, 2)
    late_parts = [_by_core(s_uq), _by_core(s_ukv), _w_in_blocks(LATE_CHIPS, dw_in_lat, dw_in_rest)]
    late_other = _to_other_core(late_parts, name="dw_late_to_core")
    late_sums = [_pair_sum(p, o_, core, tr=256, name=f"dw_late_pair_sum_{i}")
                 for i, (p, o_) in enumerate(zip(late_parts, late_other))]
    late = _Exchange([(late_sums[0], ALL_CHIPS, None, False, False), (late_sums[1], ALL_CHIPS, None, False, False),
                      (late_sums[2], LATE_CHIPS, l_in, False, False)])
    dh, l_uq, l_ukv, l_in = _mm(dproj, w_in_n, name="d_h", nt=True, out_dtype=F32, tm=2048, tn=1024, tk=NP_IN // 4,
                                rider=late)
    grad_x, dg_pre = _prenorm_bwd(dh, x2, g_pre, dy, tm=tm)

    small = _gather_small(_pack_small(dg_pre, dg_post, dg_q, dg_kv, dbf[:, :N_HEADS], loss_part))

    res_in = _adamw(l_in, w_in[0], m_w_in[0], v_w_in[0], tr=256, name="adamw_w_in")
    res_uq = _adamw(l_uq, w_uq[0], m_w_uq[0], v_w_uq[0], tr=256, name="adamw_w_uq")
    res_ukv = _adamw(l_ukv, w_ukv[0], m_w_ukv[0], v_w_ukv[0], tr=256, name="adamw_w_ukv")
    res_out = _adamw(l_out, w_out[0], m_w_out[0], v_w_out[0], tr=128, name="adamw_w_out")
    res_small = _adamw_small(
        small,
        _pack_small(g_pre, g_post, g_q_latent, g_kv_latent, b_forget),
        _pack_small(m_g_pre, m_g_post, m_g_q_latent, m_g_kv_latent, m_b_forget),
        _pack_small(v_g_pre, v_g_post, v_g_q_latent, v_g_kv_latent, v_b_forget))
    small_out = [_unpack_small(r) for r in res_small]
    loss = small_out[0][1]

    def leaves(kind):
        (s_pre, s_post, s_q, s_kv, s_bf), _ = small_out[kind]
        return [s_pre, res_in[kind][None], s_q, res_uq[kind][None], s_kv, res_ukv[kind][None], s_bf,
                res_out[kind][None], s_post]

    return (loss, grad_x.reshape(x.shape), *leaves(0), *leaves(1), *leaves(2), *leaves(3))
```

```python
import functools

import numpy as np
import jax
import jax.numpy as jnp
from jax import lax
from jax.experimental import pallas as pl
from jax.experimental.pallas import tpu as pltpu

F32 = jnp.float32
BF16 = jnp.bfloat16
MESH = pl.DeviceIdType.MESH

N_DEV = 8
D_MODEL = 2048
N_HEADS = 8
HEAD = 128
Q_RANK = 768
KV_RANK = 512
ROPE = 64
D_IN = 6472
SHARD_IN = D_IN // N_DEV
NORM_EPS = 1e-6
ROPE_THETA = 10000.0
MLA_SCALE = (HEAD + ROPE) ** -0.5
FOX_SCALE = HEAD ** -0.5

QL0, KR0, FL0, KVL0, GM0, GF0, FQ0, FK0, FV0, NP_IN = 0, 768, 896, 1024, 1536, 2560, 3584, 4608, 5632, 6656
LAT_W = GM0
LANE = 128
_SEGMENTS = ((0, 768, QL0), (768, 512, KVL0), (1280, 32, KR0), (1312, 32, KR0 + 64), (1344, 1024, GM0),
             (2368, 3072, FQ0), (5440, 8, FL0), (5448, 1024, GF0))
LOG2E = 1.4426950408889634

ADAM_LR = 0.001
ADAM_B1 = 0.9
ADAM_B2 = 0.999
ADAM_EPS = 1e-08
ADAM_WD = 0.01
ADAM_STEP = 10

VMEM_LIMIT_BYTES = 56 * 1024 * 1024
SMALL_ROWS, SMALL_COLS = 8, 768


def _params(n_grid=0):
    return pltpu.CompilerParams(vmem_limit_bytes=VMEM_LIMIT_BYTES,
                                dimension_semantics=("arbitrary",) * n_grid if n_grid else None)


def _sigmoid(z):
    return 1.0 / (1.0 + jnp.exp(-z))


def _split3(v):
    a = v.astype(BF16)
    r = v - a.astype(F32)
    b = r.astype(BF16)
    c = (r - b.astype(F32)).astype(BF16)
    return a, b, c


def _mm(a, b, *, name, nt=False, out_dtype=F32, tm=1024, tn=512, tk=2048, b_cols=None, rider=None):
    m, k_dim = a.shape
    n = b.shape[0] if nt else b.shape[1]
    col0 = 0
    if b_cols is not None:
        assert not nt
        col0, n = b_cols
    assert (b.shape[1] if nt else b.shape[0]) == k_dim
    tm, tn, tk = min(tm, m), min(tn, n), min(tk, k_dim)
    assert m % tm == 0 and n % tn == 0 and k_dim % tk == 0 and col0 % tn == 0, (name, a.shape, b.shape)
    nk = k_dim // tk
    j0 = col0 // tn
    grid = (m // tm, n // tn, nk)
    dims = (((1,), (1 if nt else 0,)), ((), ()))
    n_rin = len(rider.operands) if rider else 0
    n_rout = len(rider.out_shape) if rider else 0

    def body(*refs):
        a_ref, b_ref = refs[:2]
        o_ref = refs[2 + n_rin]
        acc_ref = refs[3 + n_rin + n_rout]
        i, j, k = pl.program_id(0), pl.program_id(1), pl.program_id(2)
        if rider:
            rider_refs = (refs[2:2 + n_rin], refs[3 + n_rin:3 + n_rin + n_rout], refs[4 + n_rin + n_rout:])

            @pl.when(jnp.logical_and(i == 0, jnp.logical_and(j == 0, k == 0)))
            def _():
                rider.start(*rider_refs)

        @pl.when(k == 0)
        def _():
            acc_ref[...] = jnp.zeros_like(acc_ref)

        acc_ref[...] += lax.dot_general(a_ref[...], b_ref[...], dims, preferred_element_type=F32)

        @pl.when(k == nk - 1)
        def _():
            o_ref[...] = acc_ref[...].astype(o_ref.dtype)

        if rider:
            @pl.when(jnp.logical_and(i == grid[0] - 1, jnp.logical_and(j == grid[1] - 1, k == nk - 1)))
            def _():
                rider.wait(*rider_refs)

    b_spec = (pl.BlockSpec((tn, tk), lambda i, j, k: (j, k)) if nt
              else pl.BlockSpec((tk, tn), lambda i, j, k: (k, j0 + j)))
    a_spec = pl.BlockSpec((tm, tk), lambda i, j, k: (i, k))
    any_spec = pl.BlockSpec(memory_space=pl.ANY)
    out = pl.pallas_call(
        body, name=name, grid=grid,
        in_specs=[a_spec, b_spec] + [any_spec] * n_rin,
        out_specs=[pl.BlockSpec((tm, tn), lambda i, j, k: (i, j))] + [any_spec] * n_rout,
        out_shape=[jax.ShapeDtypeStruct((m, n), out_dtype)] + (list(rider.out_shape) if rider else []),
        scratch_shapes=[pltpu.VMEM((tm, tn), F32)] + (list(rider.scratch) if rider else []),
        input_output_aliases={2 + i_in: 1 + i_out for i_in, i_out in rider.aliases.items()} if rider else {},
        compiler_params=_params(3),
    )(a, b, *(rider.operands if rider else ()))
    return out if rider else out[0]


def _prenorm(x, g, *, tm):
    t = x.shape[0]

    def body(x_ref, g_ref, h_ref, ht_ref):
        xv = x_ref[...]
        r = lax.rsqrt(jnp.mean(xv * xv, axis=-1, keepdims=True) + NORM_EPS)
        h = xv * r * g_ref[...]
        h_ref[...] = h.astype(BF16)
        ht_ref[...] = h.T.astype(BF16)

    return pl.pallas_call(
        body, name="prenorm", grid=(t // tm,),
        in_specs=[pl.BlockSpec((tm, D_MODEL), lambda i: (i, 0)), pl.BlockSpec((1, D_MODEL), lambda i: (0, 0))],
        out_specs=[pl.BlockSpec((tm, D_MODEL), lambda i: (i, 0)), pl.BlockSpec((D_MODEL, tm), lambda i: (0, i))],
        out_shape=[jax.ShapeDtypeStruct((t, D_MODEL), BF16), jax.ShapeDtypeStruct((D_MODEL, t), BF16)],
        compiler_params=_params(1),
    )(x, g)


def _rope_rows():
    inv = (np.float32(ROPE_THETA) ** (-np.arange(0, ROPE, 2, dtype=np.float32) / np.float32(ROPE))).astype(np.float32)
    invf = np.zeros((1, LANE), np.float32)
    sgn = np.zeros((1, LANE), np.float32)
    invf[0, 0:32] = inv
    invf[0, 64:96] = inv
    sgn[0, 0:32] = -1.0
    sgn[0, 64:96] = 1.0
    return jnp.asarray(invf), jnp.asarray(sgn)


def _rot(v, cos_t, sin_t):
    return v * cos_t + pltpu.roll(v, 64, 1) * sin_t


def _rot_bwd(dv, cos_t, sin_t):
    return dv * cos_t + pltpu.roll(dv * sin_t, 64, 1)


def _mid_fwd(proj, flog, g_q, g_kv, bf_row, pos_col, *, tm):
    t = proj.shape[0]
    invf, sgn = _rope_rows()

    def body(p_ref, fl_ref, gq_ref, gkv_ref, bf_ref, pos_ref, invf_ref, sgn_ref,
             qn_ref, kvn_ref, kr_ref, cos_ref, sin_ref, c_ref, qnt_ref, kvnt_ref, carry_ref):
        i = pl.program_id(0)

        @pl.when(i == 0)
        def _():
            carry_ref[...] = jnp.zeros_like(carry_ref)

        ql = p_ref[:, QL0:QL0 + Q_RANK].astype(F32)
        r = lax.rsqrt(jnp.mean(ql * ql, axis=-1, keepdims=True) + NORM_EPS)
        qn = ql * r * gq_ref[...]
        qn_ref[...] = qn.astype(BF16)
        qnt_ref[...] = qn.T.astype(BF16)
        kvl = p_ref[:, KVL0:KVL0 + KV_RANK].astype(F32)
        r = lax.rsqrt(jnp.mean(kvl * kvl, axis=-1, keepdims=True) + NORM_EPS)
        kvn = kvl * r * gkv_ref[...]
        kvn_ref[...] = kvn.astype(BF16)
        kvnt_ref[...] = kvn.T.astype(BF16)

        ang = pos_ref[...] * invf_ref[...]
        cos_t = jnp.cos(ang)
        sin_t = jnp.sin(ang) * sgn_ref[...]
        cos_ref[...] = cos_t
        sin_ref[...] = sin_t
        kr_ref[...] = _rot(p_ref[:, KR0:KR0 + LANE].astype(F32), cos_t, sin_t).astype(BF16)

        z = fl_ref[...] + bf_ref[...]
        logf = jnp.minimum(z, 0.0) - jnp.log(1.0 + jnp.exp(-jnp.abs(z)))
        row = lax.broadcasted_iota(jnp.int32, (tm, tm), 0)
        col = lax.broadcasted_iota(jnp.int32, (tm, tm), 1)
        tri = (col <= row).astype(BF16)
        acc = carry_ref[0:1, :]
        for part in _split3(logf):
            acc = acc + jnp.dot(tri, part, preferred_element_type=F32)
        c_ref[...] = acc * (1.0 / FOX_SCALE)
        carry_ref[0:1, :] = carry_ref[0:1, :] + jnp.sum(logf, axis=0, keepdims=True)

    row_spec = lambda w: pl.BlockSpec((tm, w), lambda i: (i, 0))
    vec_spec = lambda w: pl.BlockSpec((1, w), lambda i: (0, 0))
    return pl.pallas_call(
        body, name="mid_fwd", grid=(t // tm,),
        in_specs=[row_spec(LAT_W), row_spec(LANE), vec_spec(Q_RANK), vec_spec(KV_RANK), vec_spec(LANE),
                  pl.BlockSpec((tm, 1), lambda i: (i, 0)), vec_spec(LANE), vec_spec(LANE)],
        out_specs=[row_spec(Q_RANK), row_spec(KV_RANK), row_spec(LANE), row_spec(LANE), row_spec(LANE), row_spec(LANE),
                   pl.BlockSpec((Q_RANK, tm), lambda i: (0, i)), pl.BlockSpec((KV_RANK, tm), lambda i: (0, i))],
        out_shape=[jax.ShapeDtypeStruct((t, Q_RANK), BF16), jax.ShapeDtypeStruct((t, KV_RANK), BF16),
                   jax.ShapeDtypeStruct((t, LANE), BF16), jax.ShapeDtypeStruct((t, LANE), F32),
                   jax.ShapeDtypeStruct((t, LANE), F32), jax.ShapeDtypeStruct((t, LANE), F32),
                   jax.ShapeDtypeStruct((Q_RANK, t), BF16), jax.ShapeDtypeStruct((KV_RANK, t), BF16)],
        scratch_shapes=[pltpu.VMEM((8, LANE), F32)],
        compiler_params=_params(1),
    )(proj, flog, g_q, g_kv, bf_row, pos_col, invf, sgn)


def _q_up_rope(qn, w_uq_n, cos_t, sin_t, *, tm):
    t = qn.shape[0]
    tn = 2 * 256

    def body(a_ref, b_ref, cos_ref, sin_ref, o_ref):
        q = jnp.dot(a_ref[...], b_ref[...], preferred_element_type=F32)
        c, s = cos_ref[...], sin_ref[...]
        for u in range(tn // 256):
            o_ref[:, 256 * u:256 * u + 128] = q[:, 256 * u:256 * u + 128].astype(BF16)
            o_ref[:, 256 * u + 128:256 * u + 256] = _rot(q[:, 256 * u + 128:256 * u + 256], c, s).astype(BF16)

    return pl.pallas_call(
        body, name="q_up_rope", grid=(t // tm, N_HEADS * 256 // tn),
        in_specs=[pl.BlockSpec((tm, Q_RANK), lambda i, j: (i, 0)), pl.BlockSpec((Q_RANK, tn), lambda i, j: (0, j)),
                  pl.BlockSpec((tm, LANE), lambda i, j: (i, 0)), pl.BlockSpec((tm, LANE), lambda i, j: (i, 0))],
        out_specs=pl.BlockSpec((tm, tn), lambda i, j: (i, j)),
        out_shape=jax.ShapeDtypeStruct((t, N_HEADS * 256), BF16),
        compiler_params=_params(2),
    )(qn, w_uq_n, cos_t, sin_t)


def _attn_fwd(fox, operands, *, t, tb, name):
    nb = t // tb
    scale = FOX_SCALE if fox else MLA_SCALE
    exp2_scale = scale * LOG2E
    pair = 2 * HEAD
    pair0 = N_HEADS // 2 if fox else 0
    q_w = HEAD if fox else 2 * HEAD
    nt_dims = (((1,), (1,)), ((), ()))
    tn_dims = (((0,), (0,)), ((), ()))

    def body(*refs):
        if fox:
            (q_ref, k_ref, v_ref, gate_ref, cq_ref, ck_ref, _, _, _,
             o_ref, og_ref, ogt_ref, lse_ref, m_s, l_s, acc_s) = refs
        else:
            q_ref, kv_ref, kr_ref, gate_ref, o_ref, og_ref, ogt_ref, lse_ref, m_s, l_s, acc_s = refs
        qi = pl.program_id(1)
        m_s[...] = jnp.full_like(m_s, -jnp.inf)
        l_s[...] = jnp.zeros_like(l_s)
        acc_s[...] = jnp.zeros_like(acc_s)

        def chunk(kc, masked):
            off = pl.multiple_of(kc * tb, tb)
            scores = []
            for u in range(2):
                q = q_ref[:, q_w * u:q_w * (u + 1)]
                if fox:
                    kk = k_ref[pl.ds(off, tb), HEAD * u:HEAD * (u + 1)]
                else:
                    kk = jnp.concatenate([kv_ref[pl.ds(off, tb), pair * u:pair * u + HEAD],
                                          kr_ref[pl.ds(off, tb), :]], axis=1)
                s = lax.dot_general(kk, q, nt_dims, preferred_element_type=F32)
                if fox:
                    s = s + cq_ref[u, 0] - ck_ref[u, pl.ds(off, tb), :]
                if masked:
                    row = lax.broadcasted_iota(jnp.int32, (tb, tb), 0)
                    col = lax.broadcasted_iota(jnp.int32, (tb, tb), 1)
                    s = jnp.where(row <= col, s, -jnp.inf)
                scores.append(s)
            for u in range(2):
                s = scores[u]
                m_prev = m_s[u]
                m_new = jnp.maximum(m_prev, jnp.max(s, axis=0, keepdims=True))
                alpha = jnp.exp2((m_prev - m_new) * exp2_scale)
                p = jnp.exp2((s - m_new) * exp2_scale)
                l_s[u] = alpha * l_s[u] + jnp.sum(p, axis=0, keepdims=True)
                if fox:
                    vv = v_ref[pl.ds(off, tb), HEAD * u:HEAD * (u + 1)]
                else:
                    vv = kv_ref[pl.ds(off, tb), pair * u + HEAD:pair * (u + 1)]
                acc_s[u] = alpha * acc_s[u] + lax.dot_general(vv, p.astype(BF16), tn_dims,
                                                              preferred_element_type=F32)
                m_s[u] = m_new

        def loop_body(kc, carry):
            chunk(kc, False)
            return carry

        lax.fori_loop(0, qi, loop_body, 0)
        chunk(qi, True)
        for u in range(2):
            cols = slice(HEAD * u, HEAD * (u + 1))
            o_t = acc_s[u] / l_s[u]
            o = o_t.T
            o_ref[:, cols] = o
            g = gate_ref[:, cols].astype(F32)
            silu = g * _sigmoid(g)
            og_ref[:, cols] = (o * silu).astype(BF16)
            ogt_ref[cols, :] = (o_t * silu.T).astype(BF16)
            lse_ref[u, 0] = m_s[u] * scale + jnp.log(l_s[u])

    any_spec = pl.BlockSpec(memory_space=pl.ANY)
    row_stat = pl.BlockSpec((2, 1, 1, tb), lambda g, i: (g, i, 0, 0))
    if fox:
        proj, c_col, c_row4, o_all, og_all, ogt_all = operands
        ins = [proj, proj, proj, proj, c_row4, c_col, o_all, og_all, ogt_all]
        in_specs = [pl.BlockSpec((tb, pair), lambda g, i: (i, FQ0 // pair + g)),
                    pl.BlockSpec((t, pair), lambda g, i: (0, FK0 // pair + g)),
                    pl.BlockSpec((t, pair), lambda g, i: (0, FV0 // pair + g)),
                    pl.BlockSpec((tb, pair), lambda g, i: (i, GF0 // pair + g)),
                    row_stat, pl.BlockSpec((2, t, 1), lambda g, i: (g, 0, 0)), any_spec, any_spec, any_spec]
        aliases = {6: 0, 7: 1, 8: 2}
    else:
        q_full, kv, kr, proj = operands
        ins = [q_full, kv, kr, proj]
        in_specs = [pl.BlockSpec((tb, 2 * pair), lambda g, i: (i, g)),
                    pl.BlockSpec((t, 2 * pair), lambda g, i: (0, g)),
                    pl.BlockSpec((t, HEAD), lambda g, i: (0, 0)),
                    pl.BlockSpec((tb, pair), lambda g, i: (i, GM0 // pair + g))]
        aliases = {}
    return pl.pallas_call(
        body, name=name, grid=(N_HEADS // 2, nb), in_specs=in_specs,
        out_specs=[pl.BlockSpec((tb, pair), lambda g, i: (i, pair0 + g)),
                   pl.BlockSpec((tb, pair), lambda g, i: (i, pair0 + g)),
                   pl.BlockSpec((pair, tb), lambda g, i: (pair0 + g, i)), row_stat],
        out_shape=[jax.ShapeDtypeStruct((t, 2 * N_HEADS * HEAD), F32), jax.ShapeDtypeStruct((t, 2 * N_HEADS * HEAD), BF16),
                   jax.ShapeDtypeStruct((2 * N_HEADS * HEAD, t), BF16), jax.ShapeDtypeStruct((N_HEADS, nb, 1, tb), F32)],
        scratch_shapes=[pltpu.VMEM((2, 1, tb), F32), pltpu.VMEM((2, 1, tb), F32), pltpu.VMEM((2, HEAD, tb), F32)],
        input_output_aliases=aliases,
        compiler_params=_params(2),
    )(*ins)


def _out_norm_loss(og, w_out_n, x, target, g, *, tm):
    t = og.shape[0]

    def body(og_ref, w_ref, x_ref, t_ref, g_ref, dy_ref, do_ref, dg_ref, loss_ref):
        i = pl.program_id(0)

        @pl.when(i == 0)
        def _():
            dg_ref[...] = jnp.zeros_like(dg_ref)
            loss_ref[...] = jnp.zeros_like(loss_ref)

        ov = jnp.dot(og_ref[...], w_ref[...], preferred_element_type=F32)
        gv = g_ref[...]
        r = lax.rsqrt(jnp.mean(ov * ov, axis=-1, keepdims=True) + NORM_EPS)
        oh = ov * r
        e = x_ref[...] + oh * gv - t_ref[...]
        loss_ref[...] += 0.5 * jnp.sum(jnp.mean(e * e, axis=-1, keepdims=True), axis=0, keepdims=True)
        dy = e * (1.0 / D_MODEL)
        dy_ref[...] = dy
        dyg = dy * gv
        do_ref[...] = (r * (dyg - oh * jnp.mean(dyg * oh, axis=-1, keepdims=True))).astype(BF16)
        dg_ref[...] += jnp.sum(dy * oh, axis=0, keepdims=True)

    row = pl.BlockSpec((tm, D_MODEL), lambda i: (i, 0))
    vec = pl.BlockSpec((1, D_MODEL), lambda i: (0, 0))
    whole_w = pl.BlockSpec((D_MODEL, D_MODEL), lambda i: (0, 0), pipeline_mode=pl.Buffered(1))
    return pl.pallas_call(
        body, name="out_norm_loss", grid=(t // tm,),
        in_specs=[row, whole_w, row, row, vec],
        out_specs=[row, row, vec, pl.BlockSpec((1, 1), lambda i: (0, 0))],
        out_shape=[jax.ShapeDtypeStruct((t, D_MODEL), F32), jax.ShapeDtypeStruct((t, D_MODEL), BF16),
                   jax.ShapeDtypeStruct((1, D_MODEL), F32), jax.ShapeDtypeStruct((1, 1), F32)],
        compiler_params=_params(1),
    )(og, w_out_n, x, target, g)


def _dog_gate(d_o_post, w_out_n, o_all, proj, *, tm):
    t = d_o_post.shape[0]
    n_group = 4
    pair = n_group * HEAD
    gate_blk = GM0 // pair
    assert GM0 % pair == 0 and GF0 == GM0 + N_HEADS * HEAD

    def body(do_ref, w_ref, o_ref, p_ref, dattn_ref, dproj_ref, delta_ref):
        j = pl.program_id(1)

        @pl.when(j == 0)
        def _():
            delta_ref[...] = jnp.zeros_like(delta_ref)

        dog = lax.dot_general(do_ref[...], w_ref[...], (((1,), (1,)), ((), ())), preferred_element_type=F32)
        g = p_ref[...].astype(F32)
        ov = o_ref[...]
        sg = _sigmoid(g)
        d_o = dog * (g * sg)
        dattn_ref[...] = d_o.astype(BF16)
        dproj_ref[...] = (dog * ov * (sg * (1.0 + g * (1.0 - sg)))).astype(BF16)
        prod = d_o * ov
        lane = lax.broadcasted_iota(jnp.int32, (tm, LANE), 1)
        delta = delta_ref[...]
        for u in range(n_group):
            part = jnp.sum(prod[:, HEAD * u:HEAD * (u + 1)], axis=-1, keepdims=True)
            delta = jnp.where(lane == n_group * j + u, part, delta)
        delta_ref[...] = delta

    return pl.pallas_call(
        body, name="dog_gate", grid=(t // tm, 2 * N_HEADS // n_group),
        in_specs=[pl.BlockSpec((tm, D_MODEL), lambda i, j: (i, 0)), pl.BlockSpec((pair, D_MODEL), lambda i, j: (j, 0)),
                  pl.BlockSpec((tm, pair), lambda i, j: (i, j)), pl.BlockSpec((tm, pair), lambda i, j: (i, gate_blk + j))],
        out_specs=[pl.BlockSpec((tm, pair), lambda i, j: (i, j)), pl.BlockSpec((tm, pair), lambda i, j: (i, gate_blk + j)),
                   pl.BlockSpec((tm, LANE), lambda i, j: (i, 0))],
        out_shape=[jax.ShapeDtypeStruct((t, 2048), BF16), jax.ShapeDtypeStruct((t, NP_IN), BF16),
                   jax.ShapeDtypeStruct((t, LANE), F32)],
        compiler_params=_params(2),
    )(d_o_post, w_out_n, o_all, proj)


def _attn_bwd(fox, operands, *, t, tb, name, rider=None):
    nb = t // tb
    n_pairs = N_HEADS // 2
    pair = 2 * HEAD
    scale = FOX_SCALE if fox else MLA_SCALE
    q_w = HEAD if fox else 2 * HEAD
    nt_dims = (((1,), (1,)), ((), ()))
    tn_dims = (((0,), (0,)), ((), ()))
    n_rin = len(rider.operands) if rider else 0
    n_rout = len(rider.out_shape) if rider else 0
    n_in, n_out, n_scr = (9, 3, 9) if fox else (6, 3, 2)

    def body(*refs):
        ends = np.cumsum([0, n_in, n_rin, n_out, n_rout, n_scr])
        in_refs, rider_in, out_refs, rider_out, scr_refs = (refs[a:b] for a, b in zip(ends[:-1], ends[1:]))
        rider_refs = (rider_in, rider_out, refs[ends[-1]:])
        if fox:
            q_ref, k_ref, v_ref, do_ref, lse_ref, dl_ref, cq_ref, ck_ref, _ = in_refs
            dproj_ref, dck_ref, dcq_ref = out_refs
            dq_acc, dk_s, dv_s, dc_s, dcq_s, stage_q, stage_k, stage_v, put_sems = scr_refs
        else:
            q_ref, kv_ref, kr_ref, do_ref, lse_ref, dl_ref = in_refs
            dq_acc, dkv_ref, dkr_ref = out_refs
            dk_s, dv_s = scr_refs
        g = pl.program_id(0)
        ki = pl.program_id(1)
        if rider:
            @pl.when(jnp.logical_and(g == 0, ki == 0))
            def _():
                rider.start(*rider_refs)

        @pl.when(ki == 0)
        def _():
            dq_acc[...] = jnp.zeros_like(dq_acc)
            if fox:
                dcq_s[...] = jnp.zeros_like(dcq_s)

        dk_s[...] = jnp.zeros_like(dk_s)
        dv_s[...] = jnp.zeros_like(dv_s)
        if fox:
            dc_s[...] = jnp.zeros_like(dc_s)
            keys = [k_ref[:, HEAD * u:HEAD * (u + 1)] for u in range(2)]
            vals = [v_ref[:, HEAD * u:HEAD * (u + 1)] for u in range(2)]
        else:
            keys = [jnp.concatenate([kv_ref[:, pair * u:pair * u + HEAD], kr_ref[...]], axis=1) for u in range(2)]
            vals = [kv_ref[:, pair * u + HEAD:pair * (u + 1)] for u in range(2)]

        def chunk(qc, masked):
            off = pl.multiple_of(qc * tb, tb)
            for u in range(2):
                kk, vv = keys[u], vals[u]
                qq = q_ref[pl.ds(off, tb), q_w * u:q_w * (u + 1)]
                dd = do_ref[pl.ds(off, tb), HEAD * u:HEAD * (u + 1)]
                s = lax.dot_general(kk, qq, nt_dims, preferred_element_type=F32)
                if fox:
                    s = s + cq_ref[u, qc] - ck_ref[u]
                if masked:
                    row = lax.broadcasted_iota(jnp.int32, (tb, tb), 0)
                    col = lax.broadcasted_iota(jnp.int32, (tb, tb), 1)
                    s = jnp.where(row <= col, s, -jnp.inf)
                p = jnp.exp2(s * (scale * LOG2E) - lse_ref[u, qc] * LOG2E)
                dv_s[u] += jnp.dot(p.astype(BF16), dd, preferred_element_type=F32)
                dp = lax.dot_general(vv, dd, nt_dims, preferred_element_type=F32)
                ds = p * (dp - dl_ref[u, qc])
                if fox:
                    dc_s[u] += jnp.sum(ds, axis=1, keepdims=True)
                    dcq_s[u, qc] += jnp.sum(ds, axis=0, keepdims=True)
                dsb = (ds * scale).astype(BF16)
                dk_s[u] += jnp.dot(dsb, qq, preferred_element_type=F32)
                dq_acc[pl.ds(off, tb), q_w * u:q_w * (u + 1)] += lax.dot_general(dsb, kk, tn_dims,
                                                                                 preferred_element_type=F32)

        chunk(ki, True)

        def loop_body(qc, carry):
            chunk(qc, False)
            return carry

        lax.fori_loop(ki + 1, nb, loop_body, 0)

        def put(stage_ref, rows, seg0, sem):
            col0 = pl.multiple_of(seg0 + g * pair, pair)
            return pltpu.make_async_copy(stage_ref, dproj_ref.at[rows, pl.ds(col0, pair)], sem)

        if fox:
            rows = pl.ds(pl.multiple_of(ki * tb, tb), tb)
            block_puts = [put(stage_k, rows, FK0, put_sems.at[1]), put(stage_v, rows, FV0, put_sems.at[2])]
            pair_put = put(stage_q, pl.ds(0, t), FQ0, put_sems.at[0])

            @pl.when(jnp.logical_or(g > 0, ki > 0))
            def _():
                for cp in block_puts:
                    cp.wait()

            for u in range(2):
                stage_k[:, HEAD * u:HEAD * (u + 1)] = dk_s[u].astype(BF16)
                stage_v[:, HEAD * u:HEAD * (u + 1)] = dv_s[u].astype(BF16)
                dck_ref[u] = -dc_s[u]
            for cp in block_puts:
                cp.start()

            @pl.when(ki == nb - 1)
            def _():
                @pl.when(g > 0)
                def _():
                    pair_put.wait()

                stage_q[...] = dq_acc[...].astype(BF16)
                pair_put.start()
                dcq_ref[...] = dcq_s[...]

            @pl.when(jnp.logical_and(g == n_pairs - 1, ki == nb - 1))
            def _():
                for cp in block_puts + [pair_put]:
                    cp.wait()
        else:
            dkv_ref[...] = jnp.concatenate([dk_s[0, :, :HEAD], dv_s[0], dk_s[1, :, :HEAD], dv_s[1]], axis=1).astype(BF16)
            dkr_ref[...] = jnp.concatenate([dk_s[0, :, HEAD:], dk_s[1, :, HEAD:]], axis=1)

        if rider:
            @pl.when(jnp.logical_and(g == n_pairs - 1, ki == nb - 1))
            def _():
                rider.wait(*rider_refs)

    stat = pl.BlockSpec((2, nb, 1, tb), lambda g, i: (g, 0, 0, 0))
    aliases = {}
    if fox:
        proj, d_o, lse4, delta4, c_row4, c_col, dproj = operands
        ins = [proj, proj, proj, d_o, lse4, delta4, c_row4, c_col, dproj]
        any_spec = pl.BlockSpec(memory_space=pl.ANY)
        in_specs = [pl.BlockSpec((t, pair), lambda g, i: (0, FQ0 // pair + g)),
                    pl.BlockSpec((tb, pair), lambda g, i: (i, FK0 // pair + g)),
                    pl.BlockSpec((tb, pair), lambda g, i: (i, FV0 // pair + g)),
                    pl.BlockSpec((t, pair), lambda g, i: (0, n_pairs + g)),
                    stat, stat, stat, pl.BlockSpec((2, tb, 1), lambda g, i: (g, i, 0)), any_spec]
        aliases = {8: 0}
        out_specs = [any_spec, pl.BlockSpec((2, tb, 1), lambda g, i: (g, i, 0)), stat]
        out_shape = [jax.ShapeDtypeStruct(dproj.shape, dproj.dtype), jax.ShapeDtypeStruct((N_HEADS, t, 1), F32),
                     jax.ShapeDtypeStruct((N_HEADS, nb, 1, tb), F32)]
        scratch = [pltpu.VMEM((t, pair), F32), pltpu.VMEM((2, tb, HEAD), F32), pltpu.VMEM((2, tb, HEAD), F32),
                   pltpu.VMEM((2, tb, 1), F32), pltpu.VMEM((2, nb, 1, tb), F32),
                   pltpu.VMEM((t, pair), BF16), pltpu.VMEM((tb, pair), BF16), pltpu.VMEM((tb, pair), BF16),
                   pltpu.SemaphoreType.DMA((3,))]
    else:
        q_full, kv, kr, d_o, lse4, delta4 = operands
        ins = [q_full, kv, kr, d_o, lse4, delta4]
        in_specs = [pl.BlockSpec((t, 2 * pair), lambda g, i: (0, g)),
                    pl.BlockSpec((tb, 2 * pair), lambda g, i: (i, g)),
                    pl.BlockSpec((tb, HEAD), lambda g, i: (i, 0)),
                    pl.BlockSpec((t, pair), lambda g, i: (0, g)),
                    stat, stat]
        out_specs = [pl.BlockSpec((t, 2 * pair), lambda g, i: (0, g)), pl.BlockSpec((tb, 2 * pair), lambda g, i: (i, g)),
                     pl.BlockSpec((tb, pair), lambda g, i: (i, g))]
        out_shape = [jax.ShapeDtypeStruct((t, 2048), F32), jax.ShapeDtypeStruct((t, 2048), BF16),
                     jax.ShapeDtypeStruct((t, 1024), F32)]
        scratch = [pltpu.VMEM((2, tb, 2 * HEAD), F32), pltpu.VMEM((2, tb, HEAD), F32)]
    assert (len(ins), len(out_specs), len(scratch)) == (n_in, n_out, n_scr)
    if rider:
        any_spec = pl.BlockSpec(memory_space=pl.ANY)
        aliases = {**aliases, **{n_in + i_in: n_out + i_out for i_in, i_out in rider.aliases.items()}}
        ins = ins + list(rider.operands)
        in_specs = in_specs + [any_spec] * n_rin
        out_specs = out_specs + [any_spec] * n_rout
        out_shape = out_shape + list(rider.out_shape)
        scratch = scratch + list(rider.scratch)
    return pl.pallas_call(
        body, name=name, grid=(n_pairs, nb), in_specs=in_specs, out_specs=out_specs, out_shape=out_shape,
        scratch_shapes=scratch, input_output_aliases=aliases, compiler_params=_params(2),
    )(*ins)


def _mid_bwd(dq_full, dkr, cos_t, sin_t, dck, flog, bf_row, *, tm):
    t = dq_full.shape[0]
    n = t // tm

    def body(dq_ref, dkr_ref, cos_ref, sin_ref, dck_ref, fl_ref, bf_ref,
             dq2_ref, dkraw_ref, dfl_ref, dbf_ref, carry_ref):
        i = pl.program_id(0)

        @pl.when(i == 0)
        def _():
            carry_ref[...] = jnp.zeros_like(carry_ref)
            dbf_ref[...] = jnp.zeros_like(dbf_ref)

        c, s = cos_ref[...], sin_ref[...]
        dkr_sum = jnp.zeros((tm, LANE), F32)
        for h in range(N_HEADS):
            dq2_ref[:, 256 * h:256 * h + 128] = dq_ref[:, 256 * h:256 * h + 128].astype(BF16)
            dq2_ref[:, 256 * h + 128:256 * h + 256] = _rot_bwd(dq_ref[:, 256 * h + 128:256 * h + 256], c, s).astype(BF16)
            dkr_sum = dkr_sum + dkr_ref[:, HEAD * h:HEAD * (h + 1)]
        dkraw_ref[...] = _rot_bwd(dkr_sum, c, s).astype(BF16)

        dc = dck_ref[...]
        row = lax.broadcasted_iota(jnp.int32, (tm, tm), 0)
        col = lax.broadcasted_iota(jnp.int32, (tm, tm), 1)
        tri = (col >= row).astype(BF16)
        acc = carry_ref[0:1, :]
        for part in _split3(dc):
            acc = acc + jnp.dot(tri, part, preferred_element_type=F32)
        carry_ref[0:1, :] = carry_ref[0:1, :] + jnp.sum(dc, axis=0, keepdims=True)
        z = fl_ref[...] + bf_ref[...]
        dz = acc / (1.0 + jnp.exp(z))
        dfl_ref[...] = dz.astype(BF16)
        dbf_ref[...] += jnp.sum(dz, axis=0, keepdims=True)

    rev = lambda w: pl.BlockSpec((tm, w), lambda i: (n - 1 - i, 0))
    vec = lambda w: pl.BlockSpec((1, w), lambda i: (0, 0))
    return pl.pallas_call(
        body, name="mid_bwd", grid=(n,),
        in_specs=[rev(2048), rev(1024), rev(LANE), rev(LANE), rev(LANE), rev(LANE), vec(LANE)],
        out_specs=[rev(2048), rev(LANE), rev(LANE), vec(LANE)],
        out_shape=[jax.ShapeDtypeStruct((t, 2048), BF16), jax.ShapeDtypeStruct((t, LANE), BF16),
                   jax.ShapeDtypeStruct((t, LANE), BF16), jax.ShapeDtypeStruct((1, LANE), F32)],
        scratch_shapes=[pltpu.VMEM((8, LANE), F32)],
        compiler_params=_params(1),
    )(dq_full, dkr, cos_t, sin_t, dck, flog, bf_row)


def _norm_bwd(proj, dqn, dkvn, g_q, g_kv, dkr_raw, dfl, dproj, *, tm):
    t = proj.shape[0]
    assert (KR0, FL0, KVL0, LAT_W) == (Q_RANK, Q_RANK + LANE, Q_RANK + 2 * LANE, Q_RANK + 2 * LANE + KV_RANK)

    def body(p_ref, dqn_ref, dkvn_ref, gq_ref, gkv_ref, dkr_ref, dfl_ref, _, dproj_ref, dgq_ref, dgkv_ref):
        i = pl.program_id(0)

        @pl.when(i == 0)
        def _():
            dgq_ref[...] = jnp.zeros_like(dgq_ref)
            dgkv_ref[...] = jnp.zeros_like(dgkv_ref)

        d_lat = []
        for lo, w, dn_ref, g_ref, dg_ref in ((QL0, Q_RANK, dqn_ref, gq_ref, dgq_ref),
                                             (KVL0, KV_RANK, dkvn_ref, gkv_ref, dgkv_ref)):
            xv = p_ref[:, lo:lo + w].astype(F32)
            r = lax.rsqrt(jnp.mean(xv * xv, axis=-1, keepdims=True) + NORM_EPS)
            xh = xv * r
            dn = dn_ref[...]
            dg_ref[...] += jnp.sum(dn * xh, axis=0, keepdims=True)
            dxh = dn * g_ref[...]
            d_lat.append((r * (dxh - xh * jnp.mean(dxh * xh, axis=-1, keepdims=True))).astype(BF16))
        dproj_ref[...] = jnp.concatenate([d_lat[0], dkr_ref[...], dfl_ref[...], d_lat[1]], axis=1)

    row = lambda w: pl.BlockSpec((tm, w), lambda i: (i, 0))
    vec = lambda w: pl.BlockSpec((1, w), lambda i: (0, 0))
    return pl.pallas_call(
        body, name="norm_bwd", grid=(t // tm,),
        in_specs=[row(LAT_W), row(Q_RANK), row(KV_RANK), vec(Q_RANK), vec(KV_RANK), row(LANE), row(LANE),
                  pl.BlockSpec(memory_space=pl.ANY)],
        out_specs=[row(LAT_W), vec(Q_RANK), vec(KV_RANK)],
        out_shape=[jax.ShapeDtypeStruct(dproj.shape, dproj.dtype),
                   jax.ShapeDtypeStruct((1, Q_RANK), F32), jax.ShapeDtypeStruct((1, KV_RANK), F32)],
        input_output_aliases={7: 0},
        compiler_params=_params(1),
    )(proj, dqn, dkvn, g_q, g_kv, dkr_raw, dfl, dproj)


def _prenorm_bwd(dh, x, g, dy, *, tm):
    t = x.shape[0]

    def body(dh_ref, x_ref, g_ref, dy_ref, gx_ref, dg_ref):
        i = pl.program_id(0)

        @pl.when(i == 0)
        def _():
            dg_ref[...] = jnp.zeros_like(dg_ref)

        xv = x_ref[...]
        r = lax.rsqrt(jnp.mean(xv * xv, axis=-1, keepdims=True) + NORM_EPS)
        xh = xv * r
        dn = dh_ref[...]
        dg_ref[...] += jnp.sum(dn * xh, axis=0, keepdims=True)
        dxh = dn * g_ref[...]
        gx_ref[...] = dy_ref[...] + r * (dxh - xh * jnp.mean(dxh * xh, axis=-1, keepdims=True))

    row = pl.BlockSpec((tm, D_MODEL), lambda i: (i, 0))
    vec = pl.BlockSpec((1, D_MODEL), lambda i: (0, 0))
    return pl.pallas_call(
        body, name="prenorm_bwd", grid=(t // tm,),
        in_specs=[row, row, vec, row], out_specs=[row, vec],
        out_shape=[jax.ShapeDtypeStruct((t, D_MODEL), F32), jax.ShapeDtypeStruct((1, D_MODEL), F32)],
        compiler_params=_params(1),
    )(dh, x, g, dy)


def _adam_math(w, g, m, v):
    m = ADAM_B1 * m + (1.0 - ADAM_B1) * g
    v = ADAM_B2 * v + (1.0 - ADAM_B2) * (g * g)
    m_hat = m / (1.0 - ADAM_B1 ** ADAM_STEP)
    v_hat = v / (1.0 - ADAM_B2 ** ADAM_STEP)
    delta = -ADAM_LR * (m_hat / (jnp.sqrt(v_hat) + ADAM_EPS) + ADAM_WD * w)
    return delta, m, v


def _adamw(land, w, m, v, *, tr, name):
    rows, cols = w.shape

    def body(l_ref, w_ref, m_ref, v_ref, g_ref, d_ref, nm_ref, nv_ref):
        g = l_ref[0].astype(F32)
        for s in range(1, N_CHIPS):
            g = g + l_ref[s].astype(F32)
        g_ref[...] = g
        d_ref[...], nm_ref[...], nv_ref[...] = _adam_math(w_ref[...], g, m_ref[...], v_ref[...])

    blk = pl.BlockSpec((tr, cols), lambda i: (i, 0))
    return pl.pallas_call(
        body, name=name, grid=(rows // tr,),
        in_specs=[pl.BlockSpec((N_CHIPS, tr, cols), lambda i: (0, i, 0)), blk, blk, blk],
        out_specs=[blk, blk, blk, blk],
        out_shape=[jax.ShapeDtypeStruct((rows, cols), F32)] * 4,
        compiler_params=_params(1),
    )(land, w, m, v)


def _adamw_small(gathered, w, m, v):
    def body(a_ref, w_ref, m_ref, v_ref, g_ref, d_ref, nm_ref, nv_ref):
        g = a_ref[0:SMALL_ROWS, :]
        for s in range(1, N_DEV):
            g = g + a_ref[SMALL_ROWS * s:SMALL_ROWS * (s + 1), :]
        g_ref[...] = g
        d_ref[...], nm_ref[...], nv_ref[...] = _adam_math(w_ref[...], g, m_ref[...], v_ref[...])

    return pl.pallas_call(
        body, name="adamw_small",
        out_shape=[jax.ShapeDtypeStruct((SMALL_ROWS, SMALL_COLS), F32)] * 4,
        compiler_params=_params(),
    )(gathered, w, m, v)


def _place():
    x, y, c = lax.axis_index("x"), lax.axis_index("y"), lax.axis_index("c")
    return x, y, c


def _flip(p, k):
    x, y, c = p
    return (1 - x if k & 4 else x, 1 - y if k & 2 else y, 1 - c if k & 1 else c)


def _index(p):
    return 4 * p[0] + 2 * p[1] + p[2]


def _all_gather(shards):
    n = len(shards)
    hbm = pl.BlockSpec(memory_space=pl.ANY)

    def body(*refs):
        ins, outs = refs[:n], refs[n:2 * n]
        send_sems, recv_sems, local_sems = refs[2 * n:]
        me = _place()
        sibling = _flip(me, 1)
        chips = [_flip(me, 4), _flip(me, 2), _flip(me, 6)]

        def copy(a, k, block, to, src=None):
            dst = outs[a].at[_index(block)]
            return pltpu.make_async_remote_copy(
                src_ref=dst if src is None else src, dst_ref=dst,
                send_sem=send_sems.at[7 * a + k], recv_sem=recv_sems.at[7 * a + k],
                device_id=to, device_id_type=MESH)

        started = []
        for a in range(n):
            mine = pltpu.make_async_copy(ins[a], outs[a].at[_index(me)], local_sems.at[a])
            mine.start()
            started.append(mine)
        first = []
        for a in range(n):
            first.append(copy(a, 0, me, sibling, src=ins[a]))
            first += [copy(a, 1 + j, me, chip, src=ins[a]) for j, chip in enumerate(chips)]
        for cp in first:
            cp.start()
        passed = []
        for a in range(n):
            for j, chip in enumerate(chips):
                copy(a, 1 + j, chip, me).wait_recv()
                fwd = copy(a, 4 + j, chip, sibling)
                fwd.start()
                passed.append(fwd)
        for a in range(n):
            copy(a, 0, sibling, me).wait_recv()
            for j, chip in enumerate(chips):
                copy(a, 4 + j, _flip(chip, 1), me).wait_recv()
        for cp in first + passed:
            cp.wait_send()
        for mine in started:
            mine.wait()

    return pl.pallas_call(
        body, name="all_gather_weights",
        in_specs=[hbm] * n, out_specs=[hbm] * n,
        out_shape=[jax.ShapeDtypeStruct((N_DEV,) + s.shape, s.dtype) for s in shards],
        scratch_shapes=[pltpu.SemaphoreType.DMA((7 * n,)), pltpu.SemaphoreType.DMA((7 * n,)),
                        pltpu.SemaphoreType.DMA((n,))],
    )(*shards)


class _Exchange:
    def __init__(self, tasks):
        self.tasks = tasks
        taken = [land for _, _, land, _, _ in tasks if land is not None]
        self.operands = [src for src, _, _, _, _ in tasks] + taken
        self.out_shape = [
            jax.ShapeDtypeStruct((N_CHIPS,) + ((2,) if by_core else ()) + (src.shape if same else src.shape[1:]), src.dtype)
            for src, _, _, same, by_core in tasks]
        self.aliases, n_taken = {}, 0
        for a, (_, _, land, _, _) in enumerate(tasks):
            if land is not None:
                self.aliases[len(tasks) + n_taken] = a
                n_taken += 1
        self.scratch = [pltpu.SemaphoreType.DMA((N_CHIPS,)), pltpu.SemaphoreType.DMA((N_CHIPS,)),
                        pltpu.SemaphoreType.DMA(())] * len(tasks)

    def _copies(self, ins, outs, scratch):
        x, y, core = _place()
        my = 2 * x + y
        for a, (_, chips, _, same, by_core) in enumerate(self.tasks):
            send_sems, recv_sems, local_sem = scratch[3 * a:3 * a + 3]
            slot = (lambda s, a=a, by_core=by_core: outs[a].at[s, core] if by_core else outs[a].at[s])
            for i, j in enumerate(chips):
                src = ins[a] if same else ins[a].at[i]
                pair = jnp.bitwise_xor(my, j)
                remote = pltpu.make_async_remote_copy(
                    src_ref=src, dst_ref=slot(my), send_sem=send_sems.at[pair], recv_sem=recv_sems.at[pair],
                    device_id=(j >> 1, j & 1, core), device_id_type=MESH)
                local = pltpu.make_async_copy(src, slot(my), local_sem)
                yield j, my, core, remote, local, slot, (send_sems, recv_sems)

    def start(self, ins, outs, scratch):
        for j, my, _, remote, local, _, _ in self._copies(ins, outs, scratch):
            pl.when(my != j)(remote.start)
            pl.when(my == j)(local.start)

    def wait(self, ins, outs, scratch):
        for j, my, core, remote, local, slot, (send_sems, recv_sems) in self._copies(ins, outs, scratch):
            pl.when(my != j)(remote.wait_send)

            @pl.when(my == j)
            def _():
                local.wait()
                for s in range(N_CHIPS):
                    if s != j:
                        pltpu.make_async_remote_copy(
                            src_ref=slot(s), dst_ref=slot(s), send_sem=send_sems.at[j ^ s], recv_sem=recv_sems.at[j ^ s],
                            device_id=(s >> 1, s & 1, core), device_id_type=MESH).wait_recv()


N_CHIPS = 4
ALL_CHIPS = tuple(range(N_CHIPS))


def _to_other_core(parts, *, name):
    n_arr = len(parts)
    hbm = pl.BlockSpec(memory_space=pl.ANY)

    def body(*refs):
        srcs, lands = refs[:n_arr], refs[n_arr:2 * n_arr]
        send_sems, recv_sems = refs[2 * n_arr:]
        me = _place()
        copies = [pltpu.make_async_remote_copy(src_ref=srcs[a].at[1 - me[2]], dst_ref=lands[a], send_sem=send_sems.at[a],
                                               recv_sem=recv_sems.at[a], device_id=_flip(me, 1), device_id_type=MESH)
                  for a in range(n_arr)]
        for cp in copies:
            cp.start()
        for cp in copies:
            cp.wait()

    return pl.pallas_call(
        body, name=name, in_specs=[hbm] * n_arr, out_specs=[hbm] * n_arr,
        out_shape=[jax.ShapeDtypeStruct(p.shape[1:], p.dtype) for p in parts],
        scratch_shapes=[pltpu.SemaphoreType.DMA((n_arr,)), pltpu.SemaphoreType.DMA((n_arr,))],
    )(*parts)


def _share_with_other_core(gathered, *, name):
    n_arr = len(gathered)
    hbm = pl.BlockSpec(memory_space=pl.ANY)

    def body(*refs):
        bufs = refs[n_arr:2 * n_arr]
        send_sems, recv_sems = refs[2 * n_arr:]
        me = _place()
        copies = []
        for a in range(n_arr):
            for j in range(N_CHIPS):
                block = bufs[a].at[j, me[2]]
                copies.append(pltpu.make_async_remote_copy(
                    src_ref=block, dst_ref=block, send_sem=send_sems.at[N_CHIPS * a + j],
                    recv_sem=recv_sems.at[N_CHIPS * a + j], device_id=_flip(me, 1), device_id_type=MESH))
        for cp in copies:
            cp.start()
        for cp in copies:
            cp.wait()

    return pl.pallas_call(
        body, name=name, in_specs=[hbm] * n_arr, out_specs=[hbm] * n_arr,
        out_shape=[jax.ShapeDtypeStruct(g.shape, g.dtype) for g in gathered],
        scratch_shapes=[pltpu.SemaphoreType.DMA((N_CHIPS * n_arr,)), pltpu.SemaphoreType.DMA((N_CHIPS * n_arr,))],
        input_output_aliases={a: a for a in range(n_arr)},
    )(*gathered)


def _pair_sum(mine, other, core, *, tr, name):
    _, n, rows, cols = mine.shape
    tr = min(tr, rows)

    def body(core_ref, a_ref, b_ref, o_ref):
        o_ref[...] = (a_ref[0].astype(F32) + b_ref[...].astype(F32)).astype(BF16)

    return pl.pallas_call(
        body, name=name,
        grid_spec=pltpu.PrefetchScalarGridSpec(
            num_scalar_prefetch=1, grid=(n, rows // tr),
            in_specs=[pl.BlockSpec((1, 1, tr, cols), lambda j, i, core_ref: (core_ref[0], j, i, 0)),
                      pl.BlockSpec((1, tr, cols), lambda j, i, core_ref: (j, i, 0))],
            out_specs=pl.BlockSpec((1, tr, cols), lambda j, i, core_ref: (j, i, 0))),
        out_shape=jax.ShapeDtypeStruct(other.shape, BF16),
        compiler_params=_params(2),
    )(core, mine, other)


def _gather_small(vec):
    def body(v_ref, out_ref, send_sems, recv_sems, local_sem):
        me = _place()

        def rows(p):
            return out_ref.at[pl.ds(pl.multiple_of(_index(p) * SMALL_ROWS, SMALL_ROWS), SMALL_ROWS), :]

        mine = pltpu.make_async_copy(v_ref, rows(me), local_sem)
        mine.start()
        sends = []
        for k in range(1, N_DEV):
            peer = _flip(me, k)
            cp = pltpu.make_async_remote_copy(src_ref=v_ref, dst_ref=rows(me), send_sem=send_sems.at[k - 1],
                                              recv_sem=recv_sems.at[k - 1], device_id=peer, device_id_type=MESH)
            cp.start()
            sends.append(cp)
        for k in range(1, N_DEV):
            peer = _flip(me, k)
            pltpu.make_async_remote_copy(src_ref=rows(peer), dst_ref=rows(peer), send_sem=send_sems.at[k - 1],
                                         recv_sem=recv_sems.at[k - 1], device_id=peer, device_id_type=MESH).wait_recv()
        for cp in sends:
            cp.wait_send()
        mine.wait()

    return pl.pallas_call(
        body, name="gather_small",
        in_specs=[pl.BlockSpec(memory_space=pltpu.VMEM)], out_specs=pl.BlockSpec(memory_space=pltpu.VMEM),
        out_shape=jax.ShapeDtypeStruct((N_DEV * SMALL_ROWS, SMALL_COLS), F32),
        scratch_shapes=[pltpu.SemaphoreType.DMA((7,)), pltpu.SemaphoreType.DMA((7,)), pltpu.SemaphoreType.DMA],
    )(vec)


def _w_in_nice(gathered):
    pieces, pos = [], 0
    for o0, width, n0 in sorted(_SEGMENTS, key=lambda seg: seg[2]):
        if n0 > pos:
            pieces.append(jnp.zeros((D_MODEL, n0 - pos), gathered.dtype))
        o = o0
        while o < o0 + width:
            d = o // SHARD_IN
            hi = min(o0 + width, (d + 1) * SHARD_IN)
            pieces.append(gathered[d][:, o - d * SHARD_IN:hi - d * SHARD_IN])
            o = hi
        pos = n0 + width
    pieces.append(jnp.zeros((D_MODEL, NP_IN - pos), gathered.dtype))
    return jnp.concatenate(pieces, axis=1)


def _w_in_blocks(chips, dw_lat, dw_rest):
    blocks = []
    for core in range(2):
        for chip in chips:
            lo = (2 * chip + core) * SHARD_IN
            runs = []
            for o0, width, n0 in _SEGMENTS:
                a, b = max(lo, o0), min(lo + SHARD_IN, o0 + width)
                if a < b:
                    n_a, n_b = n0 + a - o0, n0 + b - o0
                    runs.append(dw_lat[:, n_a:n_b] if n_b <= LAT_W else dw_rest[:, n_a - LAT_W:n_b - LAT_W])
            blocks.append(jnp.concatenate(runs, axis=1))
    return jnp.stack(blocks).reshape(2, len(chips), D_MODEL, SHARD_IN)


def _by_core(shards):
    return shards.reshape((N_CHIPS, 2) + shards.shape[1:]).swapaxes(0, 1)


EARLY_CHIPS = (1, 2)
LATE_CHIPS = (0, 3)


def _w_uq_nice(shard):
    z = jnp.zeros((Q_RANK, 32), shard.dtype)
    return jnp.concatenate([shard[:, :128], shard[:, 128:160], z, shard[:, 160:192], z], axis=1)


def _pack_small(g_pre, g_post, g_q, g_kv, b_f, extra=None):
    parts = [g_pre.reshape(-1), g_post.reshape(-1), g_q.reshape(-1), g_kv.reshape(-1), b_f.reshape(-1)]
    if extra is not None:
        parts.append(extra.reshape(-1))
    flat = jnp.concatenate(parts)
    flat = jnp.concatenate([flat, jnp.zeros((SMALL_ROWS * SMALL_COLS - flat.shape[0],), F32)])
    return flat.reshape(SMALL_ROWS, SMALL_COLS)


def _unpack_small(packed):
    flat = packed.reshape(-1)
    o = 0
    out = []
    for n in (D_MODEL, D_MODEL, Q_RANK, KV_RANK, N_HEADS):
        out.append(flat[o:o + n].reshape(1, n))
        o += n
    return out, flat[o]


def kernel(x, positions, g_pre, w_in, g_q_latent, w_uq, g_kv_latent, w_ukv, b_forget, w_out, g_post, loss_target, m_g_pre, m_w_in, m_g_q_latent, m_w_uq, m_g_kv_latent, m_w_ukv, m_b_forget, m_w_out, m_g_post, v_g_pre, v_w_in, v_g_q_latent, v_w_uq, v_g_kv_latent, v_w_ukv, v_b_forget, v_w_out, v_g_post):
    t = x.shape[1]
    tb = min(512, t)
    tm = min(256, t)
    nb = t // tb
    x2 = x.reshape(t, D_MODEL)
    target = loss_target.reshape(t, D_MODEL)
    pos_col = positions.reshape(t, 1).astype(F32)
    bf_row = jnp.concatenate([b_forget.reshape(1, N_HEADS), jnp.zeros((1, LANE - N_HEADS), F32)], axis=1)

    (g_in,) = _all_gather([w_in[0].astype(BF16)])
    w_in_n = _w_in_nice(g_in)
    gather_rest = _Exchange([(w, ALL_CHIPS, None, True, True) for w in
                             (_w_uq_nice(w_uq[0].astype(BF16)), w_ukv[0].astype(BF16), w_out[0].astype(BF16))])
    core = lax.axis_index("c").astype(jnp.int32).reshape(1)

    h, h_t = _prenorm(x2, g_pre, tm=tm)
    proj, g_uq, g_ukv, g_out = _mm(h, w_in_n, name="proj_in", out_dtype=BF16, tm=2048, tn=512, tk=2048,
                                   rider=gather_rest)
    g_uq, g_ukv, g_out = _share_with_other_core([g_uq, g_ukv, g_out], name="share_weights")
    w_uq_n = g_uq.reshape(N_DEV, Q_RANK, 256).transpose(1, 0, 2).reshape(Q_RANK, N_HEADS * 256)
    w_ukv_n = g_ukv.reshape(N_DEV, KV_RANK, 256).transpose(1, 0, 2).reshape(KV_RANK, N_HEADS * 256)
    w_out_n = g_out.reshape(D_MODEL, D_MODEL)
    flog = _mm(h, w_in_n[:, FL0:FL0 + LANE], name="proj_flog", out_dtype=F32, tm=1024, tn=LANE, tk=2048)
    qn, kvn, kr, cos_t, sin_t, c, qn_t, kvn_t = _mid_fwd(proj, flog, g_q_latent, g_kv_latent, bf_row, pos_col, tm=tm)
    q_full = _q_up_rope(qn, w_uq_n, cos_t, sin_t, tm=min(1024, t))
    kv = _mm(kvn, w_ukv_n, name="kv_up", out_dtype=BF16, tm=1024, tn=512, tk=KV_RANK)
    c_heads = c[:, :N_HEADS].T
    c_col = c_heads.reshape(N_HEADS, t, 1)
    c_row4 = c_heads.reshape(N_HEADS, nb, 1, tb)
    o_all, og_all, og_t, lse4_mla = _attn_fwd(False, (q_full, kv, kr, proj), t=t, tb=tb, name="mla_fwd")
    o_all, og_all, og_t, lse4_fox = _attn_fwd(True, (proj, c_col, c_row4, o_all, og_all, og_t), t=t, tb=tb,
                                              name="fox_fwd")
    dy, d_o_post, dg_post, loss_part = _out_norm_loss(og_all, w_out_n, x2, target, g_post, tm=min(512, t))

    dw_out = _mm(og_t, d_o_post, name="dw_out", out_dtype=BF16, tm=1024, tn=1024, tk=1024)
    p_out = _by_core(dw_out.reshape(N_DEV, D_MODEL // N_DEV, D_MODEL))
    (o_out,) = _to_other_core([p_out], name="dw_out_to_core")
    s_out = _pair_sum(p_out, o_out, core, tr=256, name="dw_out_pair_sum")
    d_attn, dproj, delta = _dog_gate(d_o_post, w_out_n, o_all, proj, tm=min(1024, t))
    delta4 = delta[:, :2 * N_HEADS].T.reshape(2 * N_HEADS, nb, 1, tb)
    dproj, dck, dcq, l_out = _attn_bwd(True, (proj, d_attn, lse4_fox, delta4[N_HEADS:], c_row4, c_col, dproj),
                                       t=t, tb=tb, name="fox_bwd",
                                       rider=_Exchange([(s_out, ALL_CHIPS, None, False, False)]))
    dw_in_rest = _mm(h_t, dproj, name="dw_in_rest", out_dtype=BF16, tm=2048, tn=512, tk=1024,
                     b_cols=(LAT_W, NP_IN - LAT_W))
    p_in = _w_in_blocks(EARLY_CHIPS, None, dw_in_rest)
    (o_in,) = _to_other_core([p_in], name="dw_in_early_to_core")
    s_in = _pair_sum(p_in, o_in, core, tr=256, name="dw_in_early_pair_sum")
    dq_full, dkv, dkr, l_in = _attn_bwd(False, (q_full, kv, kr, d_attn, lse4_mla, delta4[:N_HEADS]),
                                        t=t, tb=tb, name="mla_bwd",
                                        rider=_Exchange([(s_in, EARLY_CHIPS, None, False, False)]))
    dc_heads = dck.reshape(N_HEADS, t) + dcq.reshape(N_HEADS, t)
    dck_rows = jnp.concatenate([dc_heads.T, jnp.zeros((t, LANE - N_HEADS), F32)], axis=1)
    dq2, dkr_raw, dfl, dbf = _mid_bwd(dq_full, dkr, cos_t, sin_t, dck_rows, flog, bf_row, tm=tm)
    dqn = _mm(dq2, w_uq_n, name="d_qn", nt=True, out_dtype=F32, tm=1024, tn=Q_RANK, tk=2048)
    dkvn = _mm(dkv, w_ukv_n, name="d_kvn", nt=True, out_dtype=F32, tm=1024, tn=KV_RANK, tk=2048)
    dw_uq = _mm(qn_t, dq2, name="dw_uq", out_dtype=BF16, tm=Q_RANK, tn=1024, tk=1024)
    dw_ukv = _mm(kvn_t, dkv, name="dw_ukv", out_dtype=BF16, tm=KV_RANK, tn=1024, tk=1024)
    dproj, dg_q, dg_kv = _norm_bwd(proj, dqn, dkvn, g_q_latent, g_kv_latent, dkr_raw, dfl, dproj, tm=tm)
    dw_in_lat = _mm(h_t, dproj, name="dw_in_lat", out_dtype=BF16, tm=1024, tn=LAT_W, tk=1024, b_cols=(0, LAT_W))
    dw_uq_h = dw_uq.reshape(Q_RANK, N_HEADS, 256)
    s_uq = jnp.concatenate([dw_uq_h[:, :, :160], dw_uq_h[:, :, 192:224]], axis=2).transpose(1, 0, 2)
    s_ukv = dw_ukv.reshape(KV_RANK, N_HEADS, 256).transpose(1, 0, 2)
    late_parts = [_by_core(s_uq), _by_core(s_ukv), _w_in_blocks(LATE_CHIPS, dw_in_lat, dw_in_rest)]
    late_other = _to_other_core(late_parts, name="dw_late_to_core")
    late_sums = [_pair_sum(p, o_, core, tr=256, name=f"dw_late_pair_sum_{i}")
                 for i, (p, o_) in enumerate(zip(late_parts, late_other))]
    late = _Exchange([(late_sums[0], ALL_CHIPS, None, False, False), (late_sums[1], ALL_CHIPS, None, False, False),
                      (late_sums[2], LATE_CHIPS, l_in, False, False)])
    dh, l_uq, l_ukv, l_in = _mm(dproj, w_in_n, name="d_h", nt=True, out_dtype=F32, tm=2048, tn=1024, tk=NP_IN // 4,
                                rider=late)
    grad_x, dg_pre = _prenorm_bwd(dh, x2, g_pre, dy, tm=tm)

    small = _gather_small(_pack_small(dg_pre, dg_post, dg_q, dg_kv, dbf[:, :N_HEADS], loss_part))

    res_in = _adamw(l_in, w_in[0], m_w_in[0], v_w_in[0], tr=256, name="adamw_w_in")
    res_uq = _adamw(l_uq, w_uq[0], m_w_uq[0], v_w_uq[0], tr=256, name="adamw_w_uq")
    res_ukv = _adamw(l_ukv, w_ukv[0], m_w_ukv[0], v_w_ukv[0], tr=256, name="adamw_w_ukv")
    res_out = _adamw(l_out, w_out[0], m_w_out[0], v_w_out[0], tr=128, name="adamw_w_out")
    res_small = _adamw_small(
        small,
        _pack_small(g_pre, g_post, g_q_latent, g_kv_latent, b_forget),
        _pack_small(m_g_pre, m_g_post, m_g_q_latent, m_g_kv_latent, m_b_forget),
        _pack_small(v_g_pre, v_g_post, v_g_q_latent, v_g_kv_latent, v_b_forget))
    small_out = [_unpack_small(r) for r in res_small]
    loss = small_out[0][1]

    def leaves(kind):
        (s_pre, s_post, s_q, s_kv, s_bf), _ = small_out[kind]
        return [s_pre, res_in[kind][None], s_q, res_uq[kind][None], s_kv, res_ukv[kind][None], s_bf,
                res_out[kind][None], s_post]

    return (loss, grad_x.reshape(x.shape), *leaves(0), *leaves(1), *leaves(2), *leaves(3))
```

```python
import functools

import numpy as np
import jax
import jax.numpy as jnp
from jax import lax
from jax.experimental import pallas as pl
from jax.experimental.pallas import tpu as pltpu

F32 = jnp.float32
BF16 = jnp.bfloat16
MESH = pl.DeviceIdType.MESH

N_DEV = 8
D_MODEL = 2048
N_HEADS = 8
HEAD = 128
Q_RANK = 768
KV_RANK = 512
ROPE = 64
D_IN = 6472
SHARD_IN = D_IN // N_DEV
NORM_EPS = 1e-6
ROPE_THETA = 10000.0
MLA_SCALE = (HEAD + ROPE) ** -0.5
FOX_SCALE = HEAD ** -0.5

QL0, KR0, FL0, KVL0, GM0, GF0, FQ0, FK0, FV0, NP_IN = 0, 768, 896, 1024, 1536, 2560, 3584, 4608, 5632, 6656
LAT_W = GM0
LANE = 128
_SEGMENTS = ((0, 768, QL0), (768, 512, KVL0), (1280, 32, KR0), (1312, 32, KR0 + 64), (1344, 1024, GM0),
             (2368, 3072, FQ0), (5440, 8, FL0), (5448, 1024, GF0))
LOG2E = 1.4426950408889634

ADAM_LR = 0.001
ADAM_B1 = 0.9
ADAM_B2 = 0.999
ADAM_EPS = 1e-08
ADAM_WD = 0.01
ADAM_STEP = 10

VMEM_LIMIT_BYTES = 56 * 1024 * 1024
SMALL_ROWS, SMALL_COLS = 8, 768


def _params(n_grid=0):
    return pltpu.CompilerParams(vmem_limit_bytes=VMEM_LIMIT_BYTES,
                                dimension_semantics=("arbitrary",) * n_grid if n_grid else None)


def _sigmoid(z):
    return 1.0 / (1.0 + jnp.exp(-z))


def _split3(v):
    a = v.astype(BF16)
    r = v - a.astype(F32)
    b = r.astype(BF16)
    c = (r - b.astype(F32)).astype(BF16)
    return a, b, c


def _mm(a, b, *, name, nt=False, out_dtype=F32, tm=1024, tn=512, tk=2048, b_cols=None, rider=None):
    m, k_dim = a.shape
    n = b.shape[0] if nt else b.shape[1]
    col0 = 0
    if b_cols is not None:
        assert not nt
        col0, n = b_cols
    assert (b.shape[1] if nt else b.shape[0]) == k_dim
    tm, tn, tk = min(tm, m), min(tn, n), min(tk, k_dim)
    assert m % tm == 0 and n % tn == 0 and k_dim % tk == 0 and col0 % tn == 0, (name, a.shape, b.shape)
    nk = k_dim // tk
    j0 = col0 // tn
    grid = (m // tm, n // tn, nk)
    dims = (((1,), (1 if nt else 0,)), ((), ()))
    n_rin = len(rider.operands) if rider else 0
    n_rout = len(rider.out_shape) if rider else 0

    def body(*refs):
        a_ref, b_ref = refs[:2]
        o_ref = refs[2 + n_rin]
        acc_ref = refs[3 + n_rin + n_rout]
        i, j, k = pl.program_id(0), pl.program_id(1), pl.program_id(2)
        if rider:
            rider_refs = (refs[2:2 + n_rin], refs[3 + n_rin:3 + n_rin + n_rout], refs[4 + n_rin + n_rout:])

            @pl.when(jnp.logical_and(i == 0, jnp.logical_and(j == 0, k == 0)))
            def _():
                rider.start(*rider_refs)

        @pl.when(k == 0)
        def _():
            acc_ref[...] = jnp.zeros_like(acc_ref)

        acc_ref[...] += lax.dot_general(a_ref[...], b_ref[...], dims, preferred_element_type=F32)

        @pl.when(k == nk - 1)
        def _():
            o_ref[...] = acc_ref[...].astype(o_ref.dtype)

        if rider:
            @pl.when(jnp.logical_and(i == grid[0] - 1, jnp.logical_and(j == grid[1] - 1, k == nk - 1)))
            def _():
                rider.wait(*rider_refs)

    b_spec = (pl.BlockSpec((tn, tk), lambda i, j, k: (j, k)) if nt
              else pl.BlockSpec((tk, tn), lambda i, j, k: (k, j0 + j)))
    a_spec = pl.BlockSpec((tm, tk), lambda i, j, k: (i, k))
    any_spec = pl.BlockSpec(memory_space=pl.ANY)
    out = pl.pallas_call(
        body, name=name, grid=grid,
        in_specs=[a_spec, b_spec] + [any_spec] * n_rin,
        out_specs=[pl.BlockSpec((tm, tn), lambda i, j, k: (i, j))] + [any_spec] * n_rout,
        out_shape=[jax.ShapeDtypeStruct((m, n), out_dtype)] + (list(rider.out_shape) if rider else []),
        scratch_shapes=[pltpu.VMEM((tm, tn), F32)] + (list(rider.scratch) if rider else []),
        input_output_aliases={2 + i_in: 1 + i_out for i_in, i_out in rider.aliases.items()} if rider else {},
        compiler_params=_params(3),
    )(a, b, *(rider.operands if rider else ()))
    return out if rider else out[0]


def _prenorm(x, g, *, tm):
    t = x.shape[0]

    def body(x_ref, g_ref, h_ref, ht_ref):
        xv = x_ref[...]
        r = lax.rsqrt(jnp.mean(xv * xv, axis=-1, keepdims=True) + NORM_EPS)
        h = xv * r * g_ref[...]
        h_ref[...] = h.astype(BF16)
        ht_ref[...] = h.T.astype(BF16)

    return pl.pallas_call(
        body, name="prenorm", grid=(t // tm,),
        in_specs=[pl.BlockSpec((tm, D_MODEL), lambda i: (i, 0)), pl.BlockSpec((1, D_MODEL), lambda i: (0, 0))],
        out_specs=[pl.BlockSpec((tm, D_MODEL), lambda i: (i, 0)), pl.BlockSpec((D_MODEL, tm), lambda i: (0, i))],
        out_shape=[jax.ShapeDtypeStruct((t, D_MODEL), BF16), jax.ShapeDtypeStruct((D_MODEL, t), BF16)],
        compiler_params=_params(1),
    )(x, g)


def _rope_rows():
    inv = (np.float32(ROPE_THETA) ** (-np.arange(0, ROPE, 2, dtype=np.float32) / np.float32(ROPE))).astype(np.float32)
    invf = np.zeros((1, LANE), np.float32)
    sgn = np.zeros((1, LANE), np.float32)
    invf[0, 0:32] = inv
    invf[0, 64:96] = inv
    sgn[0, 0:32] = -1.0
    sgn[0, 64:96] = 1.0
    return jnp.asarray(invf), jnp.asarray(sgn)


def _rot(v, cos_t, sin_t):
    return v * cos_t + pltpu.roll(v, 64, 1) * sin_t


def _rot_bwd(dv, cos_t, sin_t):
    return dv * cos_t + pltpu.roll(dv * sin_t, 64, 1)


def _mid_fwd(proj, flog, g_q, g_kv, bf_row, pos_col, *, tm):
    t = proj.shape[0]
    invf, sgn = _rope_rows()

    def body(p_ref, fl_ref, gq_ref, gkv_ref, bf_ref, pos_ref, invf_ref, sgn_ref,
             qn_ref, kvn_ref, kr_ref, cos_ref, sin_ref, c_ref, qnt_ref, kvnt_ref, carry_ref):
        i = pl.program_id(0)

        @pl.when(i == 0)
        def _():
            carry_ref[...] = jnp.zeros_like(carry_ref)

        ql = p_ref[:, QL0:QL0 + Q_RANK].astype(F32)
        r = lax.rsqrt(jnp.mean(ql * ql, axis=-1, keepdims=True) + NORM_EPS)
        qn = ql * r * gq_ref[...]
        qn_ref[...] = qn.astype(BF16)
        qnt_ref[...] = qn.T.astype(BF16)
        kvl = p_ref[:, KVL0:KVL0 + KV_RANK].astype(F32)
        r = lax.rsqrt(jnp.mean(kvl * kvl, axis=-1, keepdims=True) + NORM_EPS)
        kvn = kvl * r * gkv_ref[...]
        kvn_ref[...] = kvn.astype(BF16)
        kvnt_ref[...] = kvn.T.astype(BF16)

        ang = pos_ref[...] * invf_ref[...]
        cos_t = jnp.cos(ang)
        sin_t = jnp.sin(ang) * sgn_ref[...]
        cos_ref[...] = cos_t
        sin_ref[...] = sin_t
        kr_ref[...] = _rot(p_ref[:, KR0:KR0 + LANE].astype(F32), cos_t, sin_t).astype(BF16)

        z = fl_ref[...] + bf_ref[...]
        logf = jnp.minimum(z, 0.0) - jnp.log(1.0 + jnp.exp(-jnp.abs(z)))
        row = lax.broadcasted_iota(jnp.int32, (tm, tm), 0)
        col = lax.broadcasted_iota(jnp.int32, (tm, tm), 1)
        tri = (col <= row).astype(BF16)
        acc = carry_ref[0:1, :]
        for part in _split3(logf):
            acc = acc + jnp.dot(tri, part, preferred_element_type=F32)
        c_ref[...] = acc * (1.0 / FOX_SCALE)
        carry_ref[0:1, :] = carry_ref[0:1, :] + jnp.sum(logf, axis=0, keepdims=True)

    row_spec = lambda w: pl.BlockSpec((tm, w), lambda i: (i, 0))
    vec_spec = lambda w: pl.BlockSpec((1, w), lambda i: (0, 0))
    return pl.pallas_call(
        body, name="mid_fwd", grid=(t // tm,),
        in_specs=[row_spec(LAT_W), row_spec(LANE), vec_spec(Q_RANK), vec_spec(KV_RANK), vec_spec(LANE),
                  pl.BlockSpec((tm, 1), lambda i: (i, 0)), vec_spec(LANE), vec_spec(LANE)],
        out_specs=[row_spec(Q_RANK), row_spec(KV_RANK), row_spec(LANE), row_spec(LANE), row_spec(LANE), row_spec(LANE),
                   pl.BlockSpec((Q_RANK, tm), lambda i: (0, i)), pl.BlockSpec((KV_RANK, tm), lambda i: (0, i))],
        out_shape=[jax.ShapeDtypeStruct((t, Q_RANK), BF16), jax.ShapeDtypeStruct((t, KV_RANK), BF16),
                   jax.ShapeDtypeStruct((t, LANE), BF16), jax.ShapeDtypeStruct((t, LANE), F32),
                   jax.ShapeDtypeStruct((t, LANE), F32), jax.ShapeDtypeStruct((t, LANE), F32),
                   jax.ShapeDtypeStruct((Q_RANK, t), BF16), jax.ShapeDtypeStruct((KV_RANK, t), BF16)],
        scratch_shapes=[pltpu.VMEM((8, LANE), F32)],
        compiler_params=_params(1),
    )(proj, flog, g_q, g_kv, bf_row, pos_col, invf, sgn)


def _q_up_rope(qn, w_uq_n, cos_t, sin_t, *, tm):
    t = qn.shape[0]
    tn = 2 * 256

    def body(a_ref, b_ref, cos_ref, sin_ref, o_ref):
        q = jnp.dot(a_ref[...], b_ref[...], preferred_element_type=F32)
        c, s = cos_ref[...], sin_ref[...]
        for u in range(tn // 256):
            o_ref[:, 256 * u:256 * u + 128] = q[:, 256 * u:256 * u + 128].astype(BF16)
            o_ref[:, 256 * u + 128:256 * u + 256] = _rot(q[:, 256 * u + 128:256 * u + 256], c, s).astype(BF16)

    return pl.pallas_call(
        body, name="q_up_rope", grid=(t // tm, N_HEADS * 256 // tn),
        in_specs=[pl.BlockSpec((tm, Q_RANK), lambda i, j: (i, 0)), pl.BlockSpec((Q_RANK, tn), lambda i, j: (0, j)),
                  pl.BlockSpec((tm, LANE), lambda i, j: (i, 0)), pl.BlockSpec((tm, LANE), lambda i, j: (i, 0))],
        out_specs=pl.BlockSpec((tm, tn), lambda i, j: (i, j)),
        out_shape=jax.ShapeDtypeStruct((t, N_HEADS * 256), BF16),
        compiler_params=_params(2),
    )(qn, w_uq_n, cos_t, sin_t)


def _attn_fwd(fox, operands, *, t, tb, name):
    nb = t // tb
    scale = FOX_SCALE if fox else MLA_SCALE
    exp2_scale = scale * LOG2E
    pair = 2 * HEAD
    pair0 = N_HEADS // 2 if fox else 0
    q_w = HEAD if fox else 2 * HEAD
    nt_dims = (((1,), (1,)), ((), ()))
    tn_dims = (((0,), (0,)), ((), ()))

    def body(*refs):
        if fox:
            (q_ref, k_ref, v_ref, gate_ref, cq_ref, ck_ref, _, _, _,
             o_ref, og_ref, ogt_ref, lse_ref, m_s, l_s, acc_s, s_a, s_b) = refs
        else:
            q_ref, kv_ref, kr_ref, gate_ref, o_ref, og_ref, ogt_ref, lse_ref, m_s, l_s, acc_s, s_a, s_b = refs
        qi = pl.program_id(1)
        m_s[...] = jnp.full_like(m_s, -jnp.inf)
        l_s[...] = jnp.zeros_like(l_s)
        acc_s[...] = jnp.zeros_like(acc_s)

        def scores(kc, s_ref):
            off = pl.multiple_of(kc * tb, tb)
            for u in range(2):
                q = q_ref[:, q_w * u:q_w * (u + 1)]
                if fox:
                    kk = k_ref[pl.ds(off, tb), HEAD * u:HEAD * (u + 1)]
                else:
                    kk = jnp.concatenate([kv_ref[pl.ds(off, tb), pair * u:pair * u + HEAD],
                                          kr_ref[pl.ds(off, tb), :]], axis=1)
                s_ref[u] = lax.dot_general(kk, q, nt_dims, preferred_element_type=F32)

        def softmax_pv(kc, s_ref, masked):
            off = pl.multiple_of(kc * tb, tb)
            for u in range(2):
                s = s_ref[u]
                if fox:
                    s = s + cq_ref[u, 0] - ck_ref[u, pl.ds(off, tb), :]
                if masked:
                    row = lax.broadcasted_iota(jnp.int32, (tb, tb), 0)
                    col = lax.broadcasted_iota(jnp.int32, (tb, tb), 1)
                    s = jnp.where(row <= col, s, -jnp.inf)
                m_prev = m_s[u]
                m_new = jnp.maximum(m_prev, jnp.max(s, axis=0, keepdims=True))
                alpha = jnp.exp2((m_prev - m_new) * exp2_scale)
                p = jnp.exp2((s - m_new) * exp2_scale)
                l_s[u] = alpha * l_s[u] + jnp.sum(p, axis=0, keepdims=True)
                if fox:
                    vv = v_ref[pl.ds(off, tb), HEAD * u:HEAD * (u + 1)]
                else:
                    vv = kv_ref[pl.ds(off, tb), pair * u + HEAD:pair * (u + 1)]
                acc_s[u] = alpha * acc_s[u] + lax.dot_general(vv, p.astype(BF16), tn_dims,
                                                              preferred_element_type=F32)
                m_s[u] = m_new

        def step(kc, cur, nxt):
            scores(kc + 1, nxt)
            softmax_pv(kc, cur, False)

        def loop_body(kc, carry):
            pl.when(kc % 2 == 0)(lambda: step(kc, s_a, s_b))
            pl.when(kc % 2 == 1)(lambda: step(kc, s_b, s_a))
            return carry

        scores(0, s_a)
        lax.fori_loop(0, qi, loop_body, 0)
        pl.when(qi % 2 == 0)(lambda: softmax_pv(qi, s_a, True))
        pl.when(qi % 2 == 1)(lambda: softmax_pv(qi, s_b, True))
        for u in range(2):
            cols = slice(HEAD * u, HEAD * (u + 1))
            o_t = acc_s[u] / l_s[u]
            o = o_t.T
            o_ref[:, cols] = o
            g = gate_ref[:, cols].astype(F32)
            silu = g * _sigmoid(g)
            og_ref[:, cols] = (o * silu).astype(BF16)
            ogt_ref[cols, :] = (o_t * silu.T).astype(BF16)
            lse_ref[u, 0] = m_s[u] * scale + jnp.log(l_s[u])

    any_spec = pl.BlockSpec(memory_space=pl.ANY)
    row_stat = pl.BlockSpec((2, 1, 1, tb), lambda g, i: (g, i, 0, 0))
    if fox:
        proj, c_col, c_row4, o_all, og_all, ogt_all = operands
        ins = [proj, proj, proj, proj, c_row4, c_col, o_all, og_all, ogt_all]
        in_specs = [pl.BlockSpec((tb, pair), lambda g, i: (i, FQ0 // pair + g)),
                    pl.BlockSpec((t, pair), lambda g, i: (0, FK0 // pair + g)),
                    pl.BlockSpec((t, pair), lambda g, i: (0, FV0 // pair + g)),
                    pl.BlockSpec((tb, pair), lambda g, i: (i, GF0 // pair + g)),
                    row_stat, pl.BlockSpec((2, t, 1), lambda g, i: (g, 0, 0)), any_spec, any_spec, any_spec]
        aliases = {6: 0, 7: 1, 8: 2}
    else:
        q_full, kv, kr, proj = operands
        ins = [q_full, kv, kr, proj]
        in_specs = [pl.BlockSpec((tb, 2 * pair), lambda g, i: (i, g)),
                    pl.BlockSpec((t, 2 * pair), lambda g, i: (0, g)),
                    pl.BlockSpec((t, HEAD), lambda g, i: (0, 0)),
                    pl.BlockSpec((tb, pair), lambda g, i: (i, GM0 // pair + g))]
        aliases = {}
    return pl.pallas_call(
        body, name=name, grid=(N_HEADS // 2, nb), in_specs=in_specs,
        out_specs=[pl.BlockSpec((tb, pair), lambda g, i: (i, pair0 + g)),
                   pl.BlockSpec((tb, pair), lambda g, i: (i, pair0 + g)),
                   pl.BlockSpec((pair, tb), lambda g, i: (pair0 + g, i)), row_stat],
        out_shape=[jax.ShapeDtypeStruct((t, 2 * N_HEADS * HEAD), F32), jax.ShapeDtypeStruct((t, 2 * N_HEADS * HEAD), BF16),
                   jax.ShapeDtypeStruct((2 * N_HEADS * HEAD, t), BF16), jax.ShapeDtypeStruct((N_HEADS, nb, 1, tb), F32)],
        scratch_shapes=[pltpu.VMEM((2, 1, tb), F32), pltpu.VMEM((2, 1, tb), F32), pltpu.VMEM((2, HEAD, tb), F32),
                        pltpu.VMEM((2, tb, tb), F32), pltpu.VMEM((2, tb, tb), F32)],
        input_output_aliases=aliases,
        compiler_params=_params(2),
    )(*ins)


def _out_norm_loss(og, w_out_n, x, target, g, *, tm):
    t = og.shape[0]

    def body(og_ref, w_ref, x_ref, t_ref, g_ref, dy_ref, do_ref, dg_ref, loss_ref):
        i = pl.program_id(0)

        @pl.when(i == 0)
        def _():
            dg_ref[...] = jnp.zeros_like(dg_ref)
            loss_ref[...] = jnp.zeros_like(loss_ref)

        ov = jnp.dot(og_ref[...], w_ref[...], preferred_element_type=F32)
        gv = g_ref[...]
        r = lax.rsqrt(jnp.mean(ov * ov, axis=-1, keepdims=True) + NORM_EPS)
        oh = ov * r
        e = x_ref[...] + oh * gv - t_ref[...]
        loss_ref[...] += 0.5 * jnp.sum(jnp.mean(e * e, axis=-1, keepdims=True), axis=0, keepdims=True)
        dy = e * (1.0 / D_MODEL)
        dy_ref[...] = dy
        dyg = dy * gv
        do_ref[...] = (r * (dyg - oh * jnp.mean(dyg * oh, axis=-1, keepdims=True))).astype(BF16)
        dg_ref[...] += jnp.sum(dy * oh, axis=0, keepdims=True)

    row = pl.BlockSpec((tm, D_MODEL), lambda i: (i, 0))
    vec = pl.BlockSpec((1, D_MODEL), lambda i: (0, 0))
    whole_w = pl.BlockSpec((D_MODEL, D_MODEL), lambda i: (0, 0), pipeline_mode=pl.Buffered(1))
    return pl.pallas_call(
        body, name="out_norm_loss", grid=(t // tm,),
        in_specs=[row, whole_w, row, row, vec],
        out_specs=[row, row, vec, pl.BlockSpec((1, 1), lambda i: (0, 0))],
        out_shape=[jax.ShapeDtypeStruct((t, D_MODEL), F32), jax.ShapeDtypeStruct((t, D_MODEL), BF16),
                   jax.ShapeDtypeStruct((1, D_MODEL), F32), jax.ShapeDtypeStruct((1, 1), F32)],
        compiler_params=_params(1),
    )(og, w_out_n, x, target, g)


def _dog_gate(d_o_post, w_out_n, o_all, proj, *, tm):
    t = d_o_post.shape[0]
    n_group = 4
    pair = n_group * HEAD
    gate_blk = GM0 // pair
    assert GM0 % pair == 0 and GF0 == GM0 + N_HEADS * HEAD

    def body(do_ref, w_ref, o_ref, p_ref, dattn_ref, dproj_ref, delta_ref):
        j = pl.program_id(1)

        @pl.when(j == 0)
        def _():
            delta_ref[...] = jnp.zeros_like(delta_ref)

        dog = lax.dot_general(do_ref[...], w_ref[...], (((1,), (1,)), ((), ())), preferred_element_type=F32)
        g = p_ref[...].astype(F32)
        ov = o_ref[...]
        sg = _sigmoid(g)
        d_o = dog * (g * sg)
        dattn_ref[...] = d_o.astype(BF16)
        dproj_ref[...] = (dog * ov * (sg * (1.0 + g * (1.0 - sg)))).astype(BF16)
        prod = d_o * ov
        lane = lax.broadcasted_iota(jnp.int32, (tm, LANE), 1)
        delta = delta_ref[...]
        for u in range(n_group):
            part = jnp.sum(prod[:, HEAD * u:HEAD * (u + 1)], axis=-1, keepdims=True)
            delta = jnp.where(lane == n_group * j + u, part, delta)
        delta_ref[...] = delta

    return pl.pallas_call(
        body, name="dog_gate", grid=(t // tm, 2 * N_HEADS // n_group),
        in_specs=[pl.BlockSpec((tm, D_MODEL), lambda i, j: (i, 0)), pl.BlockSpec((pair, D_MODEL), lambda i, j: (j, 0)),
                  pl.BlockSpec((tm, pair), lambda i, j: (i, j)), pl.BlockSpec((tm, pair), lambda i, j: (i, gate_blk + j))],
        out_specs=[pl.BlockSpec((tm, pair), lambda i, j: (i, j)), pl.BlockSpec((tm, pair), lambda i, j: (i, gate_blk + j)),
                   pl.BlockSpec((tm, LANE), lambda i, j: (i, 0))],
        out_shape=[jax.ShapeDtypeStruct((t, 2048), BF16), jax.ShapeDtypeStruct((t, NP_IN), BF16),
                   jax.ShapeDtypeStruct((t, LANE), F32)],
        compiler_params=_params(2),
    )(d_o_post, w_out_n, o_all, proj)


def _attn_bwd(fox, operands, *, t, tb, name, rider=None):
    nb = t // tb
    n_pairs = N_HEADS // 2
    pair = 2 * HEAD
    scale = FOX_SCALE if fox else MLA_SCALE
    q_w = HEAD if fox else 2 * HEAD
    nt_dims = (((1,), (1,)), ((), ()))
    tn_dims = (((0,), (0,)), ((), ()))
    n_rin = len(rider.operands) if rider else 0
    n_rout = len(rider.out_shape) if rider else 0
    n_in, n_out, n_scr = (9, 3, 9) if fox else (6, 3, 2)

    def body(*refs):
        ends = np.cumsum([0, n_in, n_rin, n_out, n_rout, n_scr])
        in_refs, rider_in, out_refs, rider_out, scr_refs = (refs[a:b] for a, b in zip(ends[:-1], ends[1:]))
        rider_refs = (rider_in, rider_out, refs[ends[-1]:])
        if fox:
            q_ref, k_ref, v_ref, do_ref, lse_ref, dl_ref, cq_ref, ck_ref, _ = in_refs
            dproj_ref, dck_ref, dcq_ref = out_refs
            dq_acc, dk_s, dv_s, dc_s, dcq_s, stage_q, stage_k, stage_v, put_sems = scr_refs
        else:
            q_ref, kv_ref, kr_ref, do_ref, lse_ref, dl_ref = in_refs
            dq_acc, dkv_ref, dkr_ref = out_refs
            dk_s, dv_s = scr_refs
        g = pl.program_id(0)
        ki = pl.program_id(1)
        if rider:
            @pl.when(jnp.logical_and(g == 0, ki == 0))
            def _():
                rider.start(*rider_refs)

        @pl.when(ki == 0)
        def _():
            dq_acc[...] = jnp.zeros_like(dq_acc)
            if fox:
                dcq_s[...] = jnp.zeros_like(dcq_s)

        dk_s[...] = jnp.zeros_like(dk_s)
        dv_s[...] = jnp.zeros_like(dv_s)
        if fox:
            dc_s[...] = jnp.zeros_like(dc_s)
            keys = [k_ref[:, HEAD * u:HEAD * (u + 1)] for u in range(2)]
            vals = [v_ref[:, HEAD * u:HEAD * (u + 1)] for u in range(2)]
        else:
            keys = [jnp.concatenate([kv_ref[:, pair * u:pair * u + HEAD], kr_ref[...]], axis=1) for u in range(2)]
            vals = [kv_ref[:, pair * u + HEAD:pair * (u + 1)] for u in range(2)]

        def chunk(qc, masked):
            off = pl.multiple_of(qc * tb, tb)
            for u in range(2):
                kk, vv = keys[u], vals[u]
                qq = q_ref[pl.ds(off, tb), q_w * u:q_w * (u + 1)]
                dd = do_ref[pl.ds(off, tb), HEAD * u:HEAD * (u + 1)]
                s = lax.dot_general(kk, qq, nt_dims, preferred_element_type=F32)
                if fox:
                    s = s + cq_ref[u, qc] - ck_ref[u]
                if masked:
                    row = lax.broadcasted_iota(jnp.int32, (tb, tb), 0)
                    col = lax.broadcasted_iota(jnp.int32, (tb, tb), 1)
                    s = jnp.where(row <= col, s, -jnp.inf)
                p = jnp.exp2(s * (scale * LOG2E) - lse_ref[u, qc] * LOG2E)
                dv_s[u] += jnp.dot(p.astype(BF16), dd, preferred_element_type=F32)
                dp = lax.dot_general(vv, dd, nt_dims, preferred_element_type=F32)
                ds = p * (dp - dl_ref[u, qc])
                if fox:
                    dc_s[u] += jnp.sum(ds, axis=1, keepdims=True)
                    dcq_s[u, qc] += jnp.sum(ds, axis=0, keepdims=True)
                dsb = (ds * scale).astype(BF16)
                dk_s[u] += jnp.dot(dsb, qq, preferred_element_type=F32)
                dq_acc[pl.ds(off, tb), q_w * u:q_w * (u + 1)] += lax.dot_general(dsb, kk, tn_dims,
                                                                                 preferred_element_type=F32)

        chunk(ki, True)

        def loop_body(qc, carry):
            chunk(qc, False)
            return carry

        lax.fori_loop(ki + 1, nb, loop_body, 0)

        def put(stage_ref, rows, seg0, sem):
            col0 = pl.multiple_of(seg0 + g * pair, pair)
            return pltpu.make_async_copy(stage_ref, dproj_ref.at[rows, pl.ds(col0, pair)], sem)

        if fox:
            rows = pl.ds(pl.multiple_of(ki * tb, tb), tb)
            block_puts = [put(stage_k, rows, FK0, put_sems.at[1]), put(stage_v, rows, FV0, put_sems.at[2])]
            pair_put = put(stage_q, pl.ds(0, t), FQ0, put_sems.at[0])

            @pl.when(jnp.logical_or(g > 0, ki > 0))
            def _():
                for cp in block_puts:
                    cp.wait()

            for u in range(2):
                stage_k[:, HEAD * u:HEAD * (u + 1)] = dk_s[u].astype(BF16)
                stage_v[:, HEAD * u:HEAD * (u + 1)] = dv_s[u].astype(BF16)
                dck_ref[u] = -dc_s[u]
            for cp in block_puts:
                cp.start()

            @pl.when(ki == nb - 1)
            def _():
                @pl.when(g > 0)
                def _():
                    pair_put.wait()

                stage_q[...] = dq_acc[...].astype(BF16)
                pair_put.start()
                dcq_ref[...] = dcq_s[...]

            @pl.when(jnp.logical_and(g == n_pairs - 1, ki == nb - 1))
            def _():
                for cp in block_puts + [pair_put]:
                    cp.wait()
        else:
            dkv_ref[...] = jnp.concatenate([dk_s[0, :, :HEAD], dv_s[0], dk_s[1, :, :HEAD], dv_s[1]], axis=1).astype(BF16)
            dkr_ref[...] = jnp.concatenate([dk_s[0, :, HEAD:], dk_s[1, :, HEAD:]], axis=1)

        if rider:
            @pl.when(jnp.logical_and(g == n_pairs - 1, ki == nb - 1))
            def _():
                rider.wait(*rider_refs)

    stat = pl.BlockSpec((2, nb, 1, tb), lambda g, i: (g, 0, 0, 0))
    aliases = {}
    if fox:
        proj, d_o, lse4, delta4, c_row4, c_col, dproj = operands
        ins = [proj, proj, proj, d_o, lse4, delta4, c_row4, c_col, dproj]
        any_spec = pl.BlockSpec(memory_space=pl.ANY)
        in_specs = [pl.BlockSpec((t, pair), lambda g, i: (0, FQ0 // pair + g)),
                    pl.BlockSpec((tb, pair), lambda g, i: (i, FK0 // pair + g)),
                    pl.BlockSpec((tb, pair), lambda g, i: (i, FV0 // pair + g)),
                    pl.BlockSpec((t, pair), lambda g, i: (0, n_pairs + g)),
                    stat, stat, stat, pl.BlockSpec((2, tb, 1), lambda g, i: (g, i, 0)), any_spec]
        aliases = {8: 0}
        out_specs = [any_spec, pl.BlockSpec((2, tb, 1), lambda g, i: (g, i, 0)), stat]
        out_shape = [jax.ShapeDtypeStruct(dproj.shape, dproj.dtype), jax.ShapeDtypeStruct((N_HEADS, t, 1), F32),
                     jax.ShapeDtypeStruct((N_HEADS, nb, 1, tb), F32)]
        scratch = [pltpu.VMEM((t, pair), F32), pltpu.VMEM((2, tb, HEAD), F32), pltpu.VMEM((2, tb, HEAD), F32),
                   pltpu.VMEM((2, tb, 1), F32), pltpu.VMEM((2, nb, 1, tb), F32),
                   pltpu.VMEM((t, pair), BF16), pltpu.VMEM((tb, pair), BF16), pltpu.VMEM((tb, pair), BF16),
                   pltpu.SemaphoreType.DMA((3,))]
    else:
        q_full, kv, kr, d_o, lse4, delta4 = operands
        ins = [q_full, kv, kr, d_o, lse4, delta4]
        in_specs = [pl.BlockSpec((t, 2 * pair), lambda g, i: (0, g)),
                    pl.BlockSpec((tb, 2 * pair), lambda g, i: (i, g)),
                    pl.BlockSpec((tb, HEAD), lambda g, i: (i, 0)),
                    pl.BlockSpec((t, pair), lambda g, i: (0, g)),
                    stat, stat]
        out_specs = [pl.BlockSpec((t, 2 * pair), lambda g, i: (0, g)), pl.BlockSpec((tb, 2 * pair), lambda g, i: (i, g)),
                     pl.BlockSpec((tb, pair), lambda g, i: (i, g))]
        out_shape = [jax.ShapeDtypeStruct((t, 2048), F32), jax.ShapeDtypeStruct((t, 2048), BF16),
                     jax.ShapeDtypeStruct((t, 1024), F32)]
        scratch = [pltpu.VMEM((2, tb, 2 * HEAD), F32), pltpu.VMEM((2, tb, HEAD), F32)]
    assert (len(ins), len(out_specs), len(scratch)) == (n_in, n_out, n_scr)
    if rider:
        any_spec = pl.BlockSpec(memory_space=pl.ANY)
        aliases = {**aliases, **{n_in + i_in: n_out + i_out for i_in, i_out in rider.aliases.items()}}
        ins = ins + list(rider.operands)
        in_specs = in_specs + [any_spec] * n_rin
        out_specs = out_specs + [any_spec] * n_rout
        out_shape = out_shape + list(rider.out_shape)
        scratch = scratch + list(rider.scratch)
    return pl.pallas_call(
        body, name=name, grid=(n_pairs, nb), in_specs=in_specs, out_specs=out_specs, out_shape=out_shape,
        scratch_shapes=scratch, input_output_aliases=aliases, compiler_params=_params(2),
    )(*ins)


def _mid_bwd(dq_full, dkr, cos_t, sin_t, dck, flog, bf_row, *, tm):
    t = dq_full.shape[0]
    n = t // tm

    def body(dq_ref, dkr_ref, cos_ref, sin_ref, dck_ref, fl_ref, bf_ref,
             dq2_ref, dkraw_ref, dfl_ref, dbf_ref, carry_ref):
        i = pl.program_id(0)

        @pl.when(i == 0)
        def _():
            carry_ref[...] = jnp.zeros_like(carry_ref)
            dbf_ref[...] = jnp.zeros_like(dbf_ref)

        c, s = cos_ref[...], sin_ref[...]
        dkr_sum = jnp.zeros((tm, LANE), F32)
        for h in range(N_HEADS):
            dq2_ref[:, 256 * h:256 * h + 128] = dq_ref[:, 256 * h:256 * h + 128].astype(BF16)
            dq2_ref[:, 256 * h + 128:256 * h + 256] = _rot_bwd(dq_ref[:, 256 * h + 128:256 * h + 256], c, s).astype(BF16)
            dkr_sum = dkr_sum + dkr_ref[:, HEAD * h:HEAD * (h + 1)]
        dkraw_ref[...] = _rot_bwd(dkr_sum, c, s).astype(BF16)

        dc = dck_ref[...]
        row = lax.broadcasted_iota(jnp.int32, (tm, tm), 0)
        col = lax.broadcasted_iota(jnp.int32, (tm, tm), 1)
        tri = (col >= row).astype(BF16)
        acc = carry_ref[0:1, :]
        for part in _split3(dc):
            acc = acc + jnp.dot(tri, part, preferred_element_type=F32)
        carry_ref[0:1, :] = carry_ref[0:1, :] + jnp.sum(dc, axis=0, keepdims=True)
        z = fl_ref[...] + bf_ref[...]
        dz = acc / (1.0 + jnp.exp(z))
        dfl_ref[...] = dz.astype(BF16)
        dbf_ref[...] += jnp.sum(dz, axis=0, keepdims=True)

    rev = lambda w: pl.BlockSpec((tm, w), lambda i: (n - 1 - i, 0))
    vec = lambda w: pl.BlockSpec((1, w), lambda i: (0, 0))
    return pl.pallas_call(
        body, name="mid_bwd", grid=(n,),
        in_specs=[rev(2048), rev(1024), rev(LANE), rev(LANE), rev(LANE), rev(LANE), vec(LANE)],
        out_specs=[rev(2048), rev(LANE), rev(LANE), vec(LANE)],
        out_shape=[jax.ShapeDtypeStruct((t, 2048), BF16), jax.ShapeDtypeStruct((t, LANE), BF16),
                   jax.ShapeDtypeStruct((t, LANE), BF16), jax.ShapeDtypeStruct((1, LANE), F32)],
        scratch_shapes=[pltpu.VMEM((8, LANE), F32)],
        compiler_params=_params(1),
    )(dq_full, dkr, cos_t, sin_t, dck, flog, bf_row)


def _norm_bwd(proj, dqn, dkvn, g_q, g_kv, dkr_raw, dfl, dproj, *, tm):
    t = proj.shape[0]
    assert (KR0, FL0, KVL0, LAT_W) == (Q_RANK, Q_RANK + LANE, Q_RANK + 2 * LANE, Q_RANK + 2 * LANE + KV_RANK)

    def body(p_ref, dqn_ref, dkvn_ref, gq_ref, gkv_ref, dkr_ref, dfl_ref, _, dproj_ref, dgq_ref, dgkv_ref):
        i = pl.program_id(0)

        @pl.when(i == 0)
        def _():
            dgq_ref[...] = jnp.zeros_like(dgq_ref)
            dgkv_ref[...] = jnp.zeros_like(dgkv_ref)

        d_lat = []
        for lo, w, dn_ref, g_ref, dg_ref in ((QL0, Q_RANK, dqn_ref, gq_ref, dgq_ref),
                                             (KVL0, KV_RANK, dkvn_ref, gkv_ref, dgkv_ref)):
            xv = p_ref[:, lo:lo + w].astype(F32)
            r = lax.rsqrt(jnp.mean(xv * xv, axis=-1, keepdims=True) + NORM_EPS)
            xh = xv * r
            dn = dn_ref[...]
            dg_ref[...] += jnp.sum(dn * xh, axis=0, keepdims=True)
            dxh = dn * g_ref[...]
            d_lat.append((r * (dxh - xh * jnp.mean(dxh * xh, axis=-1, keepdims=True))).astype(BF16))
        dproj_ref[...] = jnp.concatenate([d_lat[0], dkr_ref[...], dfl_ref[...], d_lat[1]], axis=1)

    row = lambda w: pl.BlockSpec((tm, w), lambda i: (i, 0))
    vec = lambda w: pl.BlockSpec((1, w), lambda i: (0, 0))
    return pl.pallas_call(
        body, name="norm_bwd", grid=(t // tm,),
        in_specs=[row(LAT_W), row(Q_RANK), row(KV_RANK), vec(Q_RANK), vec(KV_RANK), row(LANE), row(LANE),
                  pl.BlockSpec(memory_space=pl.ANY)],
        out_specs=[row(LAT_W), vec(Q_RANK), vec(KV_RANK)],
        out_shape=[jax.ShapeDtypeStruct(dproj.shape, dproj.dtype),
                   jax.ShapeDtypeStruct((1, Q_RANK), F32), jax.ShapeDtypeStruct((1, KV_RANK), F32)],
        input_output_aliases={7: 0},
        compiler_params=_params(1),
    )(proj, dqn, dkvn, g_q, g_kv, dkr_raw, dfl, dproj)


def _prenorm_bwd(dh, x, g, dy, *, tm):
    t = x.shape[0]

    def body(dh_ref, x_ref, g_ref, dy_ref, gx_ref, dg_ref):
        i = pl.program_id(0)

        @pl.when(i == 0)
        def _():
            dg_ref[...] = jnp.zeros_like(dg_ref)

        xv = x_ref[...]
        r = lax.rsqrt(jnp.mean(xv * xv, axis=-1, keepdims=True) + NORM_EPS)
        xh = xv * r
        dn = dh_ref[...]
        dg_ref[...] += jnp.sum(dn * xh, axis=0, keepdims=True)
        dxh = dn * g_ref[...]
        gx_ref[...] = dy_ref[...] + r * (dxh - xh * jnp.mean(dxh * xh, axis=-1, keepdims=True))

    row = pl.BlockSpec((tm, D_MODEL), lambda i: (i, 0))
    vec = pl.BlockSpec((1, D_MODEL), lambda i: (0, 0))
    return pl.pallas_call(
        body, name="prenorm_bwd", grid=(t // tm,),
        in_specs=[row, row, vec, row], out_specs=[row, vec],
        out_shape=[jax.ShapeDtypeStruct((t, D_MODEL), F32), jax.ShapeDtypeStruct((1, D_MODEL), F32)],
        compiler_params=_params(1),
    )(dh, x, g, dy)


def _adam_math(w, g, m, v):
    m = ADAM_B1 * m + (1.0 - ADAM_B1) * g
    v = ADAM_B2 * v + (1.0 - ADAM_B2) * (g * g)
    m_hat = m / (1.0 - ADAM_B1 ** ADAM_STEP)
    v_hat = v / (1.0 - ADAM_B2 ** ADAM_STEP)
    delta = -ADAM_LR * (m_hat / (jnp.sqrt(v_hat) + ADAM_EPS) + ADAM_WD * w)
    return delta, m, v


def _adamw(land, w, m, v, *, tr, name):
    rows, cols = w.shape

    def body(l_ref, w_ref, m_ref, v_ref, g_ref, d_ref, nm_ref, nv_ref):
        g = l_ref[0].astype(F32)
        for s in range(1, N_CHIPS):
            g = g + l_ref[s].astype(F32)
        g_ref[...] = g
        d_ref[...], nm_ref[...], nv_ref[...] = _adam_math(w_ref[...], g, m_ref[...], v_ref[...])

    blk = pl.BlockSpec((tr, cols), lambda i: (i, 0))
    return pl.pallas_call(
        body, name=name, grid=(rows // tr,),
        in_specs=[pl.BlockSpec((N_CHIPS, tr, cols), lambda i: (0, i, 0)), blk, blk, blk],
        out_specs=[blk, blk, blk, blk],
        out_shape=[jax.ShapeDtypeStruct((rows, cols), F32)] * 4,
        compiler_params=_params(1),
    )(land, w, m, v)


def _adamw_small(gathered, w, m, v):
    def body(a_ref, w_ref, m_ref, v_ref, g_ref, d_ref, nm_ref, nv_ref):
        g = a_ref[0:SMALL_ROWS, :]
        for s in range(1, N_DEV):
            g = g + a_ref[SMALL_ROWS * s:SMALL_ROWS * (s + 1), :]
        g_ref[...] = g
        d_ref[...], nm_ref[...], nv_ref[...] = _adam_math(w_ref[...], g, m_ref[...], v_ref[...])

    return pl.pallas_call(
        body, name="adamw_small",
        out_shape=[jax.ShapeDtypeStruct((SMALL_ROWS, SMALL_COLS), F32)] * 4,
        compiler_params=_params(),
    )(gathered, w, m, v)


def _place():
    x, y, c = lax.axis_index("x"), lax.axis_index("y"), lax.axis_index("c")
    return x, y, c


def _flip(p, k):
    x, y, c = p
    return (1 - x if k & 4 else x, 1 - y if k & 2 else y, 1 - c if k & 1 else c)


def _index(p):
    return 4 * p[0] + 2 * p[1] + p[2]


def _all_gather(shards):
    n = len(shards)
    hbm = pl.BlockSpec(memory_space=pl.ANY)

    def body(*refs):
        ins, outs = refs[:n], refs[n:2 * n]
        send_sems, recv_sems, local_sems = refs[2 * n:]
        me = _place()
        sibling = _flip(me, 1)
        chips = [_flip(me, 4), _flip(me, 2), _flip(me, 6)]

        def copy(a, k, block, to, src=None):
            dst = outs[a].at[_index(block)]
            return pltpu.make_async_remote_copy(
                src_ref=dst if src is None else src, dst_ref=dst,
                send_sem=send_sems.at[7 * a + k], recv_sem=recv_sems.at[7 * a + k],
                device_id=to, device_id_type=MESH)

        started = []
        for a in range(n):
            mine = pltpu.make_async_copy(ins[a], outs[a].at[_index(me)], local_sems.at[a])
            mine.start()
            started.append(mine)
        first = []
        for a in range(n):
            first.append(copy(a, 0, me, sibling, src=ins[a]))
            first += [copy(a, 1 + j, me, chip, src=ins[a]) for j, chip in enumerate(chips)]
        for cp in first:
            cp.start()
        passed = []
        for a in range(n):
            for j, chip in enumerate(chips):
                copy(a, 1 + j, chip, me).wait_recv()
                fwd = copy(a, 4 + j, chip, sibling)
                fwd.start()
                passed.append(fwd)
        for a in range(n):
            copy(a, 0, sibling, me).wait_recv()
            for j, chip in enumerate(chips):
                copy(a, 4 + j, _flip(chip, 1), me).wait_recv()
        for cp in first + passed:
            cp.wait_send()
        for mine in started:
            mine.wait()

    return pl.pallas_call(
        body, name="all_gather_weights",
        in_specs=[hbm] * n, out_specs=[hbm] * n,
        out_shape=[jax.ShapeDtypeStruct((N_DEV,) + s.shape, s.dtype) for s in shards],
        scratch_shapes=[pltpu.SemaphoreType.DMA((7 * n,)), pltpu.SemaphoreType.DMA((7 * n,)),
                        pltpu.SemaphoreType.DMA((n,))],
    )(*shards)


class _Exchange:
    def __init__(self, tasks):
        self.tasks = tasks
        taken = [land for _, _, land, _, _ in tasks if land is not None]
        self.operands = [src for src, _, _, _, _ in tasks] + taken
        self.out_shape = [
            jax.ShapeDtypeStruct((N_CHIPS,) + ((2,) if by_core else ()) + (src.shape if same else src.shape[1:]), src.dtype)
            for src, _, _, same, by_core in tasks]
        self.aliases, n_taken = {}, 0
        for a, (_, _, land, _, _) in enumerate(tasks):
            if land is not None:
                self.aliases[len(tasks) + n_taken] = a
                n_taken += 1
        self.scratch = [pltpu.SemaphoreType.DMA((N_CHIPS,)), pltpu.SemaphoreType.DMA((N_CHIPS,)),
                        pltpu.SemaphoreType.DMA(())] * len(tasks)

    def _copies(self, ins, outs, scratch):
        x, y, core = _place()
        my = 2 * x + y
        for a, (_, chips, _, same, by_core) in enumerate(self.tasks):
            send_sems, recv_sems, local_sem = scratch[3 * a:3 * a + 3]
            slot = (lambda s, a=a, by_core=by_core: outs[a].at[s, core] if by_core else outs[a].at[s])
            for i, j in enumerate(chips):
                src = ins[a] if same else ins[a].at[i]
                pair = jnp.bitwise_xor(my, j)
                remote = pltpu.make_async_remote_copy(
                    src_ref=src, dst_ref=slot(my), send_sem=send_sems.at[pair], recv_sem=recv_sems.at[pair],
                    device_id=(j >> 1, j & 1, core), device_id_type=MESH)
                local = pltpu.make_async_copy(src, slot(my), local_sem)
                yield j, my, core, remote, local, slot, (send_sems, recv_sems)

    def start(self, ins, outs, scratch):
        for j, my, _, remote, local, _, _ in self._copies(ins, outs, scratch):
            pl.when(my != j)(remote.start)
            pl.when(my == j)(local.start)

    def wait(self, ins, outs, scratch):
        for j, my, core, remote, local, slot, (send_sems, recv_sems) in self._copies(ins, outs, scratch):
            pl.when(my != j)(remote.wait_send)

            @pl.when(my == j)
            def _():
                local.wait()
                for s in range(N_CHIPS):
                    if s != j:
                        pltpu.make_async_remote_copy(
                            src_ref=slot(s), dst_ref=slot(s), send_sem=send_sems.at[j ^ s], recv_sem=recv_sems.at[j ^ s],
                            device_id=(s >> 1, s & 1, core), device_id_type=MESH).wait_recv()


N_CHIPS = 4
ALL_CHIPS = tuple(range(N_CHIPS))


def _to_other_core(parts, *, name):
    n_arr = len(parts)
    hbm = pl.BlockSpec(memory_space=pl.ANY)

    def body(*refs):
        srcs, lands = refs[:n_arr], refs[n_arr:2 * n_arr]
        send_sems, recv_sems = refs[2 * n_arr:]
        me = _place()
        copies = [pltpu.make_async_remote_copy(src_ref=srcs[a].at[1 - me[2]], dst_ref=lands[a], send_sem=send_sems.at[a],
                                               recv_sem=recv_sems.at[a], device_id=_flip(me, 1), device_id_type=MESH)
                  for a in range(n_arr)]
        for cp in copies:
            cp.start()
        for cp in copies:
            cp.wait()

    return pl.pallas_call(
        body, name=name, in_specs=[hbm] * n_arr, out_specs=[hbm] * n_arr,
        out_shape=[jax.ShapeDtypeStruct(p.shape[1:], p.dtype) for p in parts],
        scratch_shapes=[pltpu.SemaphoreType.DMA((n_arr,)), pltpu.SemaphoreType.DMA((n_arr,))],
    )(*parts)


def _share_with_other_core(gathered, *, name):
    n_arr = len(gathered)
    hbm = pl.BlockSpec(memory_space=pl.ANY)

    def body(*refs):
        bufs = refs[n_arr:2 * n_arr]
        send_sems, recv_sems = refs[2 * n_arr:]
        me = _place()
        copies = []
        for a in range(n_arr):
            for j in range(N_CHIPS):
                block = bufs[a].at[j, me[2]]
                copies.append(pltpu.make_async_remote_copy(
                    src_ref=block, dst_ref=block, send_sem=send_sems.at[N_CHIPS * a + j],
                    recv_sem=recv_sems.at[N_CHIPS * a + j], device_id=_flip(me, 1), device_id_type=MESH))
        for cp in copies:
            cp.start()
        for cp in copies:
            cp.wait()

    return pl.pallas_call(
        body, name=name, in_specs=[hbm] * n_arr, out_specs=[hbm] * n_arr,
        out_shape=[jax.ShapeDtypeStruct(g.shape, g.dtype) for g in gathered],
        scratch_shapes=[pltpu.SemaphoreType.DMA((N_CHIPS * n_arr,)), pltpu.SemaphoreType.DMA((N_CHIPS * n_arr,))],
        input_output_aliases={a: a for a in range(n_arr)},
    )(*gathered)


def _pair_sum(mine, other, core, *, tr, name):
    _, n, rows, cols = mine.shape
    tr = min(tr, rows)

    def body(core_ref, a_ref, b_ref, o_ref):
        o_ref[...] = (a_ref[0].astype(F32) + b_ref[...].astype(F32)).astype(BF16)

    return pl.pallas_call(
        body, name=name,
        grid_spec=pltpu.PrefetchScalarGridSpec(
            num_scalar_prefetch=1, grid=(n, rows // tr),
            in_specs=[pl.BlockSpec((1, 1, tr, cols), lambda j, i, core_ref: (core_ref[0], j, i, 0)),
                      pl.BlockSpec((1, tr, cols), lambda j, i, core_ref: (j, i, 0))],
            out_specs=pl.BlockSpec((1, tr, cols), lambda j, i, core_ref: (j, i, 0))),
        out_shape=jax.ShapeDtypeStruct(other.shape, BF16),
        compiler_params=_params(2),
    )(core, mine, other)


def _gather_small(vec):
    def body(v_ref, out_ref, send_sems, recv_sems, local_sem):
        me = _place()

        def rows(p):
            return out_ref.at[pl.ds(pl.multiple_of(_index(p) * SMALL_ROWS, SMALL_ROWS), SMALL_ROWS), :]

        mine = pltpu.make_async_copy(v_ref, rows(me), local_sem)
        mine.start()
        sends = []
        for k in range(1, N_DEV):
            peer = _flip(me, k)
            cp = pltpu.make_async_remote_copy(src_ref=v_ref, dst_ref=rows(me), send_sem=send_sems.at[k - 1],
                                              recv_sem=recv_sems.at[k - 1], device_id=peer, device_id_type=MESH)
            cp.start()
            sends.append(cp)
        for k in range(1, N_DEV):
            peer = _flip(me, k)
            pltpu.make_async_remote_copy(src_ref=rows(peer), dst_ref=rows(peer), send_sem=send_sems.at[k - 1],
                                         recv_sem=recv_sems.at[k - 1], device_id=peer, device_id_type=MESH).wait_recv()
        for cp in sends:
            cp.wait_send()
        mine.wait()

    return pl.pallas_call(
        body, name="gather_small",
        in_specs=[pl.BlockSpec(memory_space=pltpu.VMEM)], out_specs=pl.BlockSpec(memory_space=pltpu.VMEM),
        out_shape=jax.ShapeDtypeStruct((N_DEV * SMALL_ROWS, SMALL_COLS), F32),
        scratch_shapes=[pltpu.SemaphoreType.DMA((7,)), pltpu.SemaphoreType.DMA((7,)), pltpu.SemaphoreType.DMA],
    )(vec)


def _w_in_nice(gathered):
    pieces, pos = [], 0
    for o0, width, n0 in sorted(_SEGMENTS, key=lambda seg: seg[2]):
        if n0 > pos:
            pieces.append(jnp.zeros((D_MODEL, n0 - pos), gathered.dtype))
        o = o0
        while o < o0 + width:
            d = o // SHARD_IN
            hi = min(o0 + width, (d + 1) * SHARD_IN)
            pieces.append(gathered[d][:, o - d * SHARD_IN:hi - d * SHARD_IN])
            o = hi
        pos = n0 + width
    pieces.append(jnp.zeros((D_MODEL, NP_IN - pos), gathered.dtype))
    return jnp.concatenate(pieces, axis=1)


def _w_in_blocks(chips, dw_lat, dw_rest):
    blocks = []
    for core in range(2):
        for chip in chips:
            lo = (2 * chip + core) * SHARD_IN
            runs = []
            for o0, width, n0 in _SEGMENTS:
                a, b = max(lo, o0), min(lo + SHARD_IN, o0 + width)
                if a < b:
                    n_a, n_b = n0 + a - o0, n0 + b - o0
                    runs.append(dw_lat[:, n_a:n_b] if n_b <= LAT_W else dw_rest[:, n_a - LAT_W:n_b - LAT_W])
            blocks.append(jnp.concatenate(runs, axis=1))
    return jnp.stack(blocks).reshape(2, len(chips), D_MODEL, SHARD_IN)


def _by_core(shards):
    return shards.reshape((N_CHIPS, 2) + shards.shape[1:]).swapaxes(0, 1)


EARLY_CHIPS = (1, 2)
LATE_CHIPS = (0, 3)


def _w_uq_nice(shard):
    z = jnp.zeros((Q_RANK, 32), shard.dtype)
    return jnp.concatenate([shard[:, :128], shard[:, 128:160], z, shard[:, 160:192], z], axis=1)


def _pack_small(g_pre, g_post, g_q, g_kv, b_f, extra=None):
    parts = [g_pre.reshape(-1), g_post.reshape(-1), g_q.reshape(-1), g_kv.reshape(-1), b_f.reshape(-1)]
    if extra is not None:
        parts.append(extra.reshape(-1))
    flat = jnp.concatenate(parts)
    flat = jnp.concatenate([flat, jnp.zeros((SMALL_ROWS * SMALL_COLS - flat.shape[0],), F32)])
    return flat.reshape(SMALL_ROWS, SMALL_COLS)


def _unpack_small(packed):
    flat = packed.reshape(-1)
    o = 0
    out = []
    for n in (D_MODEL, D_MODEL, Q_RANK, KV_RANK, N_HEADS):
        out.append(flat[o:o + n].reshape(1, n))
        o += n
    return out, flat[o]


def kernel(x, positions, g_pre, w_in, g_q_latent, w_uq, g_kv_latent, w_ukv, b_forget, w_out, g_post, loss_target, m_g_pre, m_w_in, m_g_q_latent, m_w_uq, m_g_kv_latent, m_w_ukv, m_b_forget, m_w_out, m_g_post, v_g_pre, v_w_in, v_g_q_latent, v_w_uq, v_g_kv_latent, v_w_ukv, v_b_forget, v_w_out, v_g_post):
    t = x.shape[1]
    tb = min(512, t)
    tm = min(256, t)
    nb = t // tb
    x2 = x.reshape(t, D_MODEL)
    target = loss_target.reshape(t, D_MODEL)
    pos_col = positions.reshape(t, 1).astype(F32)
    bf_row = jnp.concatenate([b_forget.reshape(1, N_HEADS), jnp.zeros((1, LANE - N_HEADS), F32)], axis=1)

    (g_in,) = _all_gather([w_in[0].astype(BF16)])
    w_in_n = _w_in_nice(g_in)
    gather_rest = _Exchange([(w, ALL_CHIPS, None, True, True) for w in
                             (_w_uq_nice(w_uq[0].astype(BF16)), w_ukv[0].astype(BF16), w_out[0].astype(BF16))])
    core = lax.axis_index("c").astype(jnp.int32).reshape(1)

    h, h_t = _prenorm(x2, g_pre, tm=tm)
    proj, g_uq, g_ukv, g_out = _mm(h, w_in_n, name="proj_in", out_dtype=BF16, tm=2048, tn=512, tk=2048,
                                   rider=gather_rest)
    g_uq, g_ukv, g_out = _share_with_other_core([g_uq, g_ukv, g_out], name="share_weights")
    w_uq_n = g_uq.reshape(N_DEV, Q_RANK, 256).transpose(1, 0, 2).reshape(Q_RANK, N_HEADS * 256)
    w_ukv_n = g_ukv.reshape(N_DEV, KV_RANK, 256).transpose(1, 0, 2).reshape(KV_RANK, N_HEADS * 256)
    w_out_n = g_out.reshape(D_MODEL, D_MODEL)
    flog = _mm(h, w_in_n[:, FL0:FL0 + LANE], name="proj_flog", out_dtype=F32, tm=1024, tn=LANE, tk=2048)
    qn, kvn, kr, cos_t, sin_t, c, qn_t, kvn_t = _mid_fwd(proj, flog, g_q_latent, g_kv_latent, bf_row, pos_col, tm=tm)
    q_full = _q_up_rope(qn, w_uq_n, cos_t, sin_t, tm=min(1024, t))
    kv = _mm(kvn, w_ukv_n, name="kv_up", out_dtype=BF16, tm=1024, tn=512, tk=KV_RANK)
    c_heads = c[:, :N_HEADS].T
    c_col = c_heads.reshape(N_HEADS, t, 1)
    c_row4 = c_heads.reshape(N_HEADS, nb, 1, tb)
    o_all, og_all, og_t, lse4_mla = _attn_fwd(False, (q_full, kv, kr, proj), t=t, tb=tb, name="mla_fwd")
    o_all, og_all, og_t, lse4_fox = _attn_fwd(True, (proj, c_col, c_row4, o_all, og_all, og_t), t=t, tb=tb,
                                              name="fox_fwd")
    dy, d_o_post, dg_post, loss_part = _out_norm_loss(og_all, w_out_n, x2, target, g_post, tm=min(512, t))

    dw_out = _mm(og_t, d_o_post, name="dw_out", out_dtype=BF16, tm=1024, tn=1024, tk=1024)
    p_out = _by_core(dw_out.reshape(N_DEV, D_MODEL // N_DEV, D_MODEL))
    (o_out,) = _to_other_core([p_out], name="dw_out_to_core")
    s_out = _pair_sum(p_out, o_out, core, tr=256, name="dw_out_pair_sum")
    d_attn, dproj, delta = _dog_gate(d_o_post, w_out_n, o_all, proj, tm=min(1024, t))
    delta4 = delta[:, :2 * N_HEADS].T.reshape(2 * N_HEADS, nb, 1, tb)
    dproj, dck, dcq, l_out = _attn_bwd(True, (proj, d_attn, lse4_fox, delta4[N_HEADS:], c_row4, c_col, dproj),
                                       t=t, tb=tb, name="fox_bwd",
                                       rider=_Exchange([(s_out, ALL_CHIPS, None, False, False)]))
    dw_in_rest = _mm(h_t, dproj, name="dw_in_rest", out_dtype=BF16, tm=2048, tn=512, tk=1024,
                     b_cols=(LAT_W, NP_IN - LAT_W))
    p_in = _w_in_blocks(EARLY_CHIPS, None, dw_in_rest)
    (o_in,) = _to_other_core([p_in], name="dw_in_early_to_core")
    s_in = _pair_sum(p_in, o_in, core, tr=256, name="dw_in_early_pair_sum")
    dq_full, dkv, dkr, l_in = _attn_bwd(False, (q_full, kv, kr, d_attn, lse4_mla, delta4[:N_HEADS]),
                                        t=t, tb=tb, name="mla_bwd",
                                        rider=_Exchange([(s_in, EARLY_CHIPS, None, False, False)]))
    dc_heads = dck.reshape(N_HEADS, t) + dcq.reshape(N_HEADS, t)
    dck_rows = jnp.concatenate([dc_heads.T, jnp.zeros((t, LANE - N_HEADS), F32)], axis=1)
    dq2, dkr_raw, dfl, dbf = _mid_bwd(dq_full, dkr, cos_t, sin_t, dck_rows, flog, bf_row, tm=tm)
    dqn = _mm(dq2, w_uq_n, name="d_qn", nt=True, out_dtype=F32, tm=1024, tn=Q_RANK, tk=2048)
    dkvn = _mm(dkv, w_ukv_n, name="d_kvn", nt=True, out_dtype=F32, tm=1024, tn=KV_RANK, tk=2048)
    dw_uq = _mm(qn_t, dq2, name="dw_uq", out_dtype=BF16, tm=Q_RANK, tn=1024, tk=1024)
    dw_ukv = _mm(kvn_t, dkv, name="dw_ukv", out_dtype=BF16, tm=KV_RANK, tn=1024, tk=1024)
    dproj, dg_q, dg_kv = _norm_bwd(proj, dqn, dkvn, g_q_latent, g_kv_latent, dkr_raw, dfl, dproj, tm=tm)
    dw_in_lat = _mm(h_t, dproj, name="dw_in_lat", out_dtype=BF16, tm=1024, tn=LAT_W, tk=1024, b_cols=(0, LAT_W))
    dw_uq_h = dw_uq.reshape(Q_RANK, N_HEADS, 256)
    s_uq = jnp.concatenate([dw_uq_h[:, :, :160], dw_uq_h[:, :, 192:224]], axis=2).transpose(1, 0, 2)
    s_ukv = dw_ukv.reshape(KV_RANK, N_HEADS, 256).transpose(1, 0, 2)
    late_parts = [_by_core(s_uq), _by_core(s_ukv), _w_in_blocks(LATE_CHIPS, dw_in_lat, dw_in_rest)]
    late_other = _to_other_core(late_parts, name="dw_late_to_core")
    late_sums = [_pair_sum(p, o_, core, tr=256, name=f"dw_late_pair_sum_{i}")
                 for i, (p, o_) in enumerate(zip(late_parts, late_other))]
    late = _Exchange([(late_sums[0], ALL_CHIPS, None, False, False), (late_sums[1], ALL_CHIPS, None, False, False),
                      (late_sums[2], LATE_CHIPS, l_in, False, False)])
    dh, l_uq, l_ukv, l_in = _mm(dproj, w_in_n, name="d_h", nt=True, out_dtype=F32, tm=2048, tn=1024, tk=NP_IN // 4,
                                rider=late)
    grad_x, dg_pre = _prenorm_bwd(dh, x2, g_pre, dy, tm=tm)

    small = _gather_small(_pack_small(dg_pre, dg_post, dg_q, dg_kv, dbf[:, :N_HEADS], loss_part))

    res_in = _adamw(l_in, w_in[0], m_w_in[0], v_w_in[0], tr=256, name="adamw_w_in")
    res_uq = _adamw(l_uq, w_uq[0], m_w_uq[0], v_w_uq[0], tr=256, name="adamw_w_uq")
    res_ukv = _adamw(l_ukv, w_ukv[0], m_w_ukv[0], v_w_ukv[0], tr=256, name="adamw_w_ukv")
    res_out = _adamw(l_out, w_out[0], m_w_out[0], v_w_out[0], tr=128, name="adamw_w_out")
    res_small = _adamw_small(
        small,
        _pack_small(g_pre, g_post, g_q_latent, g_kv_latent, b_forget),
        _pack_small(m_g_pre, m_g_post, m_g_q_latent, m_g_kv_latent, m_b_forget),
        _pack_small(v_g_pre, v_g_post, v_g_q_latent, v_g_kv_latent, v_b_forget))
    small_out = [_unpack_small(r) for r in res_small]
    loss = small_out[0][1]

    def leaves(kind):
        (s_pre, s_post, s_q, s_kv, s_bf), _ = small_out[kind]
        return [s_pre, res_in[kind][None], s_q, res_uq[kind][None], s_kv, res_ukv[kind][None], s_bf,
                res_out[kind][None], s_post]

    return (loss, grad_x.reshape(x.shape), *leaves(0), *leaves(1), *leaves(2), *leaves(3))
```

```python
import functools

import numpy as np
import jax
import jax.numpy as jnp
from jax import lax
from jax.experimental import pallas as pl
from jax.experimental.pallas import tpu as pltpu

F32 = jnp.float32
BF16 = jnp.bfloat16
MESH = pl.DeviceIdType.MESH

N_DEV = 8
D_MODEL = 2048
N_HEADS = 8
HEAD = 128
Q_RANK = 768
KV_RANK = 512
ROPE = 64
D_IN = 6472
SHARD_IN = D_IN // N_DEV
NORM_EPS = 1e-6
ROPE_THETA = 10000.0
MLA_SCALE = (HEAD + ROPE) ** -0.5
FOX_SCALE = HEAD ** -0.5

QL0, KR0, FL0, KVL0, GM0, GF0, FQ0, FK0, FV0, NP_IN = 0, 768, 896, 1024, 1536, 2560, 3584, 4608, 5632, 6656
LAT_W = GM0
LANE = 128
_SEGMENTS = ((0, 768, QL0), (768, 512, KVL0), (1280, 32, KR0), (1312, 32, KR0 + 64), (1344, 1024, GM0),
             (2368, 3072, FQ0), (5440, 8, FL0), (5448, 1024, GF0))
LOG2E = 1.4426950408889634

ADAM_LR = 0.001
ADAM_B1 = 0.9
ADAM_B2 = 0.999
ADAM_EPS = 1e-08
ADAM_WD = 0.01
ADAM_STEP = 10

VMEM_LIMIT_BYTES = 56 * 1024 * 1024
SMALL_ROWS, SMALL_COLS = 8, 768


def _params(n_grid=0):
    return pltpu.CompilerParams(vmem_limit_bytes=VMEM_LIMIT_BYTES,
                                dimension_semantics=("arbitrary",) * n_grid if n_grid else None)


def _sigmoid(z):
    return 1.0 / (1.0 + jnp.exp(-z))


def _split3(v):
    a = v.astype(BF16)
    r = v - a.astype(F32)
    b = r.astype(BF16)
    c = (r - b.astype(F32)).astype(BF16)
    return a, b, c


def _mm(a, b, *, name, nt=False, out_dtype=F32, tm=1024, tn=512, tk=2048, b_cols=None, rider=None):
    m, k_dim = a.shape
    n = b.shape[0] if nt else b.shape[1]
    col0 = 0
    if b_cols is not None:
        assert not nt
        col0, n = b_cols
    assert (b.shape[1] if nt else b.shape[0]) == k_dim
    tm, tn, tk = min(tm, m), min(tn, n), min(tk, k_dim)
    assert m % tm == 0 and n % tn == 0 and k_dim % tk == 0 and col0 % tn == 0, (name, a.shape, b.shape)
    nk = k_dim // tk
    j0 = col0 // tn
    grid = (m // tm, n // tn, nk)
    dims = (((1,), (1 if nt else 0,)), ((), ()))
    n_rin = len(rider.operands) if rider else 0
    n_rout = len(rider.out_shape) if rider else 0

    def body(*refs):
        a_ref, b_ref = refs[:2]
        o_ref = refs[2 + n_rin]
        acc_ref = refs[3 + n_rin + n_rout]
        i, j, k = pl.program_id(0), pl.program_id(1), pl.program_id(2)
        if rider:
            rider_refs = (refs[2:2 + n_rin], refs[3 + n_rin:3 + n_rin + n_rout], refs[4 + n_rin + n_rout:])

            @pl.when(jnp.logical_and(i == 0, jnp.logical_and(j == 0, k == 0)))
            def _():
                rider.start(*rider_refs)

        @pl.when(k == 0)
        def _():
            acc_ref[...] = jnp.zeros_like(acc_ref)

        acc_ref[...] += lax.dot_general(a_ref[...], b_ref[...], dims, preferred_element_type=F32)

        @pl.when(k == nk - 1)
        def _():
            o_ref[...] = acc_ref[...].astype(o_ref.dtype)

        if rider:
            @pl.when(jnp.logical_and(i == grid[0] - 1, jnp.logical_and(j == grid[1] - 1, k == nk - 1)))
            def _():
                rider.wait(*rider_refs)

    b_spec = (pl.BlockSpec((tn, tk), lambda i, j, k: (j, k)) if nt
              else pl.BlockSpec((tk, tn), lambda i, j, k: (k, j0 + j)))
    a_spec = pl.BlockSpec((tm, tk), lambda i, j, k: (i, k))
    any_spec = pl.BlockSpec(memory_space=pl.ANY)
    out = pl.pallas_call(
        body, name=name, grid=grid,
        in_specs=[a_spec, b_spec] + [any_spec] * n_rin,
        out_specs=[pl.BlockSpec((tm, tn), lambda i, j, k: (i, j))] + [any_spec] * n_rout,
        out_shape=[jax.ShapeDtypeStruct((m, n), out_dtype)] + (list(rider.out_shape) if rider else []),
        scratch_shapes=[pltpu.VMEM((tm, tn), F32)] + (list(rider.scratch) if rider else []),
        input_output_aliases={2 + i_in: 1 + i_out for i_in, i_out in rider.aliases.items()} if rider else {},
        compiler_params=_params(3),
    )(a, b, *(rider.operands if rider else ()))
    return out if rider else out[0]


def _prenorm(x, g, *, tm):
    t = x.shape[0]

    def body(x_ref, g_ref, h_ref, ht_ref):
        xv = x_ref[...]
        r = lax.rsqrt(jnp.mean(xv * xv, axis=-1, keepdims=True) + NORM_EPS)
        h = xv * r * g_ref[...]
        h_ref[...] = h.astype(BF16)
        ht_ref[...] = h.T.astype(BF16)

    return pl.pallas_call(
        body, name="prenorm", grid=(t // tm,),
        in_specs=[pl.BlockSpec((tm, D_MODEL), lambda i: (i, 0)), pl.BlockSpec((1, D_MODEL), lambda i: (0, 0))],
        out_specs=[pl.BlockSpec((tm, D_MODEL), lambda i: (i, 0)), pl.BlockSpec((D_MODEL, tm), lambda i: (0, i))],
        out_shape=[jax.ShapeDtypeStruct((t, D_MODEL), BF16), jax.ShapeDtypeStruct((D_MODEL, t), BF16)],
        compiler_params=_params(1),
    )(x, g)


def _rope_rows():
    inv = (np.float32(ROPE_THETA) ** (-np.arange(0, ROPE, 2, dtype=np.float32) / np.float32(ROPE))).astype(np.float32)
    invf = np.zeros((1, LANE), np.float32)
    sgn = np.zeros((1, LANE), np.float32)
    invf[0, 0:32] = inv
    invf[0, 64:96] = inv
    sgn[0, 0:32] = -1.0
    sgn[0, 64:96] = 1.0
    return jnp.asarray(invf), jnp.asarray(sgn)


def _rot(v, cos_t, sin_t):
    return v * cos_t + pltpu.roll(v, 64, 1) * sin_t


def _rot_bwd(dv, cos_t, sin_t):
    return dv * cos_t + pltpu.roll(dv * sin_t, 64, 1)


def _mid_fwd(proj, flog, g_q, g_kv, bf_row, pos_col, *, tm):
    t = proj.shape[0]
    invf, sgn = _rope_rows()

    def body(p_ref, fl_ref, gq_ref, gkv_ref, bf_ref, pos_ref, invf_ref, sgn_ref,
             qn_ref, kvn_ref, kr_ref, cos_ref, sin_ref, c_ref, qnt_ref, kvnt_ref, carry_ref):
        i = pl.program_id(0)

        @pl.when(i == 0)
        def _():
            carry_ref[...] = jnp.zeros_like(carry_ref)

        ql = p_ref[:, QL0:QL0 + Q_RANK].astype(F32)
        r = lax.rsqrt(jnp.mean(ql * ql, axis=-1, keepdims=True) + NORM_EPS)
        qn = ql * r * gq_ref[...]
        qn_ref[...] = qn.astype(BF16)
        qnt_ref[...] = qn.T.astype(BF16)
        kvl = p_ref[:, KVL0:KVL0 + KV_RANK].astype(F32)
        r = lax.rsqrt(jnp.mean(kvl * kvl, axis=-1, keepdims=True) + NORM_EPS)
        kvn = kvl * r * gkv_ref[...]
        kvn_ref[...] = kvn.astype(BF16)
        kvnt_ref[...] = kvn.T.astype(BF16)

        ang = pos_ref[...] * invf_ref[...]
        cos_t = jnp.cos(ang)
        sin_t = jnp.sin(ang) * sgn_ref[...]
        cos_ref[...] = cos_t
        sin_ref[...] = sin_t
        kr_ref[...] = _rot(p_ref[:, KR0:KR0 + LANE].astype(F32), cos_t, sin_t).astype(BF16)

        z = fl_ref[...] + bf_ref[...]
        logf = jnp.minimum(z, 0.0) - jnp.log(1.0 + jnp.exp(-jnp.abs(z)))
        row = lax.broadcasted_iota(jnp.int32, (tm, tm), 0)
        col = lax.broadcasted_iota(jnp.int32, (tm, tm), 1)
        tri = (col <= row).astype(BF16)
        acc = carry_ref[0:1, :]
        for part in _split3(logf):
            acc = acc + jnp.dot(tri, part, preferred_element_type=F32)
        c_ref[...] = acc * (1.0 / FOX_SCALE)
        carry_ref[0:1, :] = carry_ref[0:1, :] + jnp.sum(logf, axis=0, keepdims=True)

    row_spec = lambda w: pl.BlockSpec((tm, w), lambda i: (i, 0))
    vec_spec = lambda w: pl.BlockSpec((1, w), lambda i: (0, 0))
    return pl.pallas_call(
        body, name="mid_fwd", grid=(t // tm,),
        in_specs=[row_spec(LAT_W), row_spec(LANE), vec_spec(Q_RANK), vec_spec(KV_RANK), vec_spec(LANE),
                  pl.BlockSpec((tm, 1), lambda i: (i, 0)), vec_spec(LANE), vec_spec(LANE)],
        out_specs=[row_spec(Q_RANK), row_spec(KV_RANK), row_spec(LANE), row_spec(LANE), row_spec(LANE), row_spec(LANE),
                   pl.BlockSpec((Q_RANK, tm), lambda i: (0, i)), pl.BlockSpec((KV_RANK, tm), lambda i: (0, i))],
        out_shape=[jax.ShapeDtypeStruct((t, Q_RANK), BF16), jax.ShapeDtypeStruct((t, KV_RANK), BF16),
                   jax.ShapeDtypeStruct((t, LANE), BF16), jax.ShapeDtypeStruct((t, LANE), F32),
                   jax.ShapeDtypeStruct((t, LANE), F32), jax.ShapeDtypeStruct((t, LANE), F32),
                   jax.ShapeDtypeStruct((Q_RANK, t), BF16), jax.ShapeDtypeStruct((KV_RANK, t), BF16)],
        scratch_shapes=[pltpu.VMEM((8, LANE), F32)],
        compiler_params=_params(1),
    )(proj, flog, g_q, g_kv, bf_row, pos_col, invf, sgn)


def _q_up_rope(qn, w_uq_n, cos_t, sin_t, *, tm):
    t = qn.shape[0]
    tn = 2 * 256

    def body(a_ref, b_ref, cos_ref, sin_ref, o_ref):
        q = jnp.dot(a_ref[...], b_ref[...], preferred_element_type=F32)
        c, s = cos_ref[...], sin_ref[...]
        for u in range(tn // 256):
            o_ref[:, 256 * u:256 * u + 128] = q[:, 256 * u:256 * u + 128].astype(BF16)
            o_ref[:, 256 * u + 128:256 * u + 256] = _rot(q[:, 256 * u + 128:256 * u + 256], c, s).astype(BF16)

    return pl.pallas_call(
        body, name="q_up_rope", grid=(t // tm, N_HEADS * 256 // tn),
        in_specs=[pl.BlockSpec((tm, Q_RANK), lambda i, j: (i, 0)), pl.BlockSpec((Q_RANK, tn), lambda i, j: (0, j)),
                  pl.BlockSpec((tm, LANE), lambda i, j: (i, 0)), pl.BlockSpec((tm, LANE), lambda i, j: (i, 0))],
        out_specs=pl.BlockSpec((tm, tn), lambda i, j: (i, j)),
        out_shape=jax.ShapeDtypeStruct((t, N_HEADS * 256), BF16),
        compiler_params=_params(2),
    )(qn, w_uq_n, cos_t, sin_t)


def _attn_fwd(fox, operands, *, t, tb, name):
    nb = t // tb
    scale = FOX_SCALE if fox else MLA_SCALE
    exp2_scale = scale * LOG2E
    pair = 2 * HEAD
    pair0 = N_HEADS // 2 if fox else 0
    q_w = HEAD if fox else 2 * HEAD
    nt_dims = (((1,), (1,)), ((), ()))
    tn_dims = (((0,), (0,)), ((), ()))

    def body(*refs):
        if fox:
            (q_ref, k_ref, v_ref, gate_ref, cq_ref, ck_ref, _, _, _,
             o_ref, og_ref, ogt_ref, lse_ref, m_s, l_s, acc_s) = refs
        else:
            q_ref, kv_ref, kr_ref, gate_ref, o_ref, og_ref, ogt_ref, lse_ref, m_s, l_s, acc_s = refs
        qi = pl.program_id(1)
        m_s[...] = jnp.full_like(m_s, -jnp.inf)
        l_s[...] = jnp.zeros_like(l_s)
        acc_s[...] = jnp.zeros_like(acc_s)

        def chunk(kc, masked):
            off = pl.multiple_of(kc * tb, tb)
            scores = []
            for u in range(2):
                q = q_ref[:, q_w * u:q_w * (u + 1)]
                if fox:
                    kk = k_ref[pl.ds(off, tb), HEAD * u:HEAD * (u + 1)]
                else:
                    kk = jnp.concatenate([kv_ref[pl.ds(off, tb), pair * u:pair * u + HEAD],
                                          kr_ref[pl.ds(off, tb), :]], axis=1)
                s = lax.dot_general(kk, q, nt_dims, preferred_element_type=F32)
                if fox:
                    s = s + cq_ref[u, 0] - ck_ref[u, pl.ds(off, tb), :]
                if masked:
                    row = lax.broadcasted_iota(jnp.int32, (tb, tb), 0)
                    col = lax.broadcasted_iota(jnp.int32, (tb, tb), 1)
                    s = jnp.where(row <= col, s, -jnp.inf)
                scores.append(s)
            for u in range(2):
                s = scores[u]
                m_prev = m_s[u]
                m_new = jnp.maximum(m_prev, jnp.max(s, axis=0, keepdims=True))
                alpha = jnp.exp2((m_prev - m_new) * exp2_scale)
                p = jnp.exp2((s - m_new) * exp2_scale)
                l_s[u] = alpha * l_s[u] + jnp.sum(p, axis=0, keepdims=True)
                if fox:
                    vv = v_ref[pl.ds(off, tb), HEAD * u:HEAD * (u + 1)]
                else:
                    vv = kv_ref[pl.ds(off, tb), pair * u + HEAD:pair * (u + 1)]
                acc_s[u] = alpha * acc_s[u] + lax.dot_general(vv, p.astype(BF16), tn_dims,
                                                              preferred_element_type=F32)
                m_s[u] = m_new

        def loop_body(kc, carry):
            chunk(kc, False)
            return carry

        lax.fori_loop(0, qi, loop_body, 0)
        chunk(qi, True)
        for u in range(2):
            cols = slice(HEAD * u, HEAD * (u + 1))
            o_t = acc_s[u] / l_s[u]
            o = o_t.T
            o_ref[:, cols] = o
            g = gate_ref[:, cols].astype(F32)
            silu = g * _sigmoid(g)
            og_ref[:, cols] = (o * silu).astype(BF16)
            ogt_ref[cols, :] = (o_t * silu.T).astype(BF16)
            lse_ref[u, 0] = m_s[u] * scale + jnp.log(l_s[u])

    any_spec = pl.BlockSpec(memory_space=pl.ANY)
    row_stat = pl.BlockSpec((2, 1, 1, tb), lambda g, i: (g, i, 0, 0))
    if fox:
        proj, c_col, c_row4, o_all, og_all, ogt_all = operands
        ins = [proj, proj, proj, proj, c_row4, c_col, o_all, og_all, ogt_all]
        in_specs = [pl.BlockSpec((tb, pair), lambda g, i: (i, FQ0 // pair + g)),
                    pl.BlockSpec((t, pair), lambda g, i: (0, FK0 // pair + g)),
                    pl.BlockSpec((t, pair), lambda g, i: (0, FV0 // pair + g)),
                    pl.BlockSpec((tb, pair), lambda g, i: (i, GF0 // pair + g)),
                    row_stat, pl.BlockSpec((2, t, 1), lambda g, i: (g, 0, 0)), any_spec, any_spec, any_spec]
        aliases = {6: 0, 7: 1, 8: 2}
    else:
        q_full, kv, kr, proj = operands
        ins = [q_full, kv, kr, proj]
        in_specs = [pl.BlockSpec((tb, 2 * pair), lambda g, i: (i, g)),
                    pl.BlockSpec((t, 2 * pair), lambda g, i: (0, g)),
                    pl.BlockSpec((t, HEAD), lambda g, i: (0, 0)),
                    pl.BlockSpec((tb, pair), lambda g, i: (i, GM0 // pair + g))]
        aliases = {}
    return pl.pallas_call(
        body, name=name, grid=(N_HEADS // 2, nb), in_specs=in_specs,
        out_specs=[pl.BlockSpec((tb, pair), lambda g, i: (i, pair0 + g)),
                   pl.BlockSpec((tb, pair), lambda g, i: (i, pair0 + g)),
                   pl.BlockSpec((pair, tb), lambda g, i: (pair0 + g, i)), row_stat],
        out_shape=[jax.ShapeDtypeStruct((t, 2 * N_HEADS * HEAD), F32), jax.ShapeDtypeStruct((t, 2 * N_HEADS * HEAD), BF16),
                   jax.ShapeDtypeStruct((2 * N_HEADS * HEAD, t), BF16), jax.ShapeDtypeStruct((N_HEADS, nb, 1, tb), F32)],
        scratch_shapes=[pltpu.VMEM((2, 1, tb), F32), pltpu.VMEM((2, 1, tb), F32), pltpu.VMEM((2, HEAD, tb), F32)],
        input_output_aliases=aliases,
        compiler_params=_params(2),
    )(*ins)


def _out_norm_loss(og, w_out_n, x, target, g, *, tm):
    t = og.shape[0]

    def body(og_ref, w_ref, x_ref, t_ref, g_ref, dy_ref, do_ref, dg_ref, loss_ref):
        i = pl.program_id(0)

        @pl.when(i == 0)
        def _():
            dg_ref[...] = jnp.zeros_like(dg_ref)
            loss_ref[...] = jnp.zeros_like(loss_ref)

        ov = jnp.dot(og_ref[...], w_ref[...], preferred_element_type=F32)
        gv = g_ref[...]
        r = lax.rsqrt(jnp.mean(ov * ov, axis=-1, keepdims=True) + NORM_EPS)
        oh = ov * r
        e = x_ref[...] + oh * gv - t_ref[...]
        loss_ref[...] += 0.5 * jnp.sum(jnp.mean(e * e, axis=-1, keepdims=True), axis=0, keepdims=True)
        dy = e * (1.0 / D_MODEL)
        dy_ref[...] = dy
        dyg = dy * gv
        do_ref[...] = (r * (dyg - oh * jnp.mean(dyg * oh, axis=-1, keepdims=True))).astype(BF16)
        dg_ref[...] += jnp.sum(dy * oh, axis=0, keepdims=True)

    row = pl.BlockSpec((tm, D_MODEL), lambda i: (i, 0))
    vec = pl.BlockSpec((1, D_MODEL), lambda i: (0, 0))
    whole_w = pl.BlockSpec((D_MODEL, D_MODEL), lambda i: (0, 0), pipeline_mode=pl.Buffered(1))
    return pl.pallas_call(
        body, name="out_norm_loss", grid=(t // tm,),
        in_specs=[row, whole_w, row, row, vec],
        out_specs=[row, row, vec, pl.BlockSpec((1, 1), lambda i: (0, 0))],
        out_shape=[jax.ShapeDtypeStruct((t, D_MODEL), F32), jax.ShapeDtypeStruct((t, D_MODEL), BF16),
                   jax.ShapeDtypeStruct((1, D_MODEL), F32), jax.ShapeDtypeStruct((1, 1), F32)],
        compiler_params=_params(1),
    )(og, w_out_n, x, target, g)


def _dog_gate(d_o_post, w_out_n, o_all, proj, *, tm):
    t = d_o_post.shape[0]
    n_group = 4
    pair = n_group * HEAD
    gate_blk = GM0 // pair
    assert GM0 % pair == 0 and GF0 == GM0 + N_HEADS * HEAD

    def body(do_ref, w_ref, o_ref, p_ref, dattn_ref, dproj_ref, delta_ref):
        j = pl.program_id(1)

        @pl.when(j == 0)
        def _():
            delta_ref[...] = jnp.zeros_like(delta_ref)

        dog = lax.dot_general(do_ref[...], w_ref[...], (((1,), (1,)), ((), ())), preferred_element_type=F32)
        g = p_ref[...].astype(F32)
        ov = o_ref[...]
        sg = _sigmoid(g)
        d_o = dog * (g * sg)
        dattn_ref[...] = d_o.astype(BF16)
        dproj_ref[...] = (dog * ov * (sg * (1.0 + g * (1.0 - sg)))).astype(BF16)
        prod = d_o * ov
        lane = lax.broadcasted_iota(jnp.int32, (tm, LANE), 1)
        delta = delta_ref[...]
        for u in range(n_group):
            part = jnp.sum(prod[:, HEAD * u:HEAD * (u + 1)], axis=-1, keepdims=True)
            delta = jnp.where(lane == n_group * j + u, part, delta)
        delta_ref[...] = delta

    return pl.pallas_call(
        body, name="dog_gate", grid=(t // tm, 2 * N_HEADS // n_group),
        in_specs=[pl.BlockSpec((tm, D_MODEL), lambda i, j: (i, 0)), pl.BlockSpec((pair, D_MODEL), lambda i, j: (j, 0)),
                  pl.BlockSpec((tm, pair), lambda i, j: (i, j)), pl.BlockSpec((tm, pair), lambda i, j: (i, gate_blk + j))],
        out_specs=[pl.BlockSpec((tm, pair), lambda i, j: (i, j)), pl.BlockSpec((tm, pair), lambda i, j: (i, gate_blk + j)),
                   pl.BlockSpec((tm, LANE), lambda i, j: (i, 0))],
        out_shape=[jax.ShapeDtypeStruct((t, 2048), BF16), jax.ShapeDtypeStruct((t, NP_IN), BF16),
                   jax.ShapeDtypeStruct((t, LANE), F32)],
        compiler_params=_params(2),
    )(d_o_post, w_out_n, o_all, proj)


def _attn_bwd(fox, operands, *, t, tb, name, rider=None):
    nb = t // tb
    n_pairs = N_HEADS // 2
    pair = 2 * HEAD
    scale = FOX_SCALE if fox else MLA_SCALE
    q_w = HEAD if fox else 2 * HEAD
    nt_dims = (((1,), (1,)), ((), ()))
    tn_dims = (((0,), (0,)), ((), ()))
    n_rin = len(rider.operands) if rider else 0
    n_rout = len(rider.out_shape) if rider else 0
    n_in, n_out, n_scr = (9, 3, 9) if fox else (6, 3, 2)

    def body(*refs):
        ends = np.cumsum([0, n_in, n_rin, n_out, n_rout, n_scr])
        in_refs, rider_in, out_refs, rider_out, scr_refs = (refs[a:b] for a, b in zip(ends[:-1], ends[1:]))
        rider_refs = (rider_in, rider_out, refs[ends[-1]:])
        if fox:
            q_ref, k_ref, v_ref, do_ref, lse_ref, dl_ref, cq_ref, ck_ref, _ = in_refs
            dproj_ref, dck_ref, dcq_ref = out_refs
            dq_acc, dk_s, dv_s, dc_s, dcq_s, stage_q, stage_k, stage_v, put_sems = scr_refs
        else:
            q_ref, kv_ref, kr_ref, do_ref, lse_ref, dl_ref = in_refs
            dq_acc, dkv_ref, dkr_ref = out_refs
            dk_s, dv_s = scr_refs
        g = pl.program_id(0)
        ki = pl.program_id(1)
        if rider:
            @pl.when(jnp.logical_and(g == 0, ki == 0))
            def _():
                rider.start(*rider_refs)

        @pl.when(ki == 0)
        def _():
            dq_acc[...] = jnp.zeros_like(dq_acc)
            if fox:
                dcq_s[...] = jnp.zeros_like(dcq_s)

        dk_s[...] = jnp.zeros_like(dk_s)
        dv_s[...] = jnp.zeros_like(dv_s)
        if fox:
            dc_s[...] = jnp.zeros_like(dc_s)
            keys = [k_ref[:, HEAD * u:HEAD * (u + 1)] for u in range(2)]
            vals = [v_ref[:, HEAD * u:HEAD * (u + 1)] for u in range(2)]
        else:
            keys = [jnp.concatenate([kv_ref[:, pair * u:pair * u + HEAD], kr_ref[...]], axis=1) for u in range(2)]
            vals = [kv_ref[:, pair * u + HEAD:pair * (u + 1)] for u in range(2)]

        def chunk(qc, masked):
            off = pl.multiple_of(qc * tb, tb)
            for u in range(2):
                kk, vv = keys[u], vals[u]
                qq = q_ref[pl.ds(off, tb), q_w * u:q_w * (u + 1)]
                dd = do_ref[pl.ds(off, tb), HEAD * u:HEAD * (u + 1)]
                s = lax.dot_general(kk, qq, nt_dims, preferred_element_type=F32)
                if fox:
                    s = s + cq_ref[u, qc] - ck_ref[u]
                if masked:
                    row = lax.broadcasted_iota(jnp.int32, (tb, tb), 0)
                    col = lax.broadcasted_iota(jnp.int32, (tb, tb), 1)
                    s = jnp.where(row <= col, s, -jnp.inf)
                p = jnp.exp2(s * (scale * LOG2E) - lse_ref[u, qc] * LOG2E)
                dv_s[u] += jnp.dot(p.astype(BF16), dd, preferred_element_type=F32)
                dp = lax.dot_general(vv, dd, nt_dims, preferred_element_type=F32)
                ds = p * (dp - dl_ref[u, qc])
                if fox:
                    dc_s[u] += jnp.sum(ds, axis=1, keepdims=True)
                    dcq_s[u, qc] += jnp.sum(ds, axis=0, keepdims=True)
                dsb = (ds * scale).astype(BF16)
                dk_s[u] += jnp.dot(dsb, qq, preferred_element_type=F32)
                dq_acc[pl.ds(off, tb), q_w * u:q_w * (u + 1)] += lax.dot_general(dsb, kk, tn_dims,
                                                                                 preferred_element_type=F32)

        chunk(ki, True)

        def loop_body(qc, carry):
            chunk(qc, False)
            return carry

        lax.fori_loop(ki + 1, nb, loop_body, 0)

        def put(stage_ref, rows, seg0, sem):
            col0 = pl.multiple_of(seg0 + g * pair, pair)
            return pltpu.make_async_copy(stage_ref, dproj_ref.at[rows, pl.ds(col0, pair)], sem)

        if fox:
            rows = pl.ds(pl.multiple_of(ki * tb, tb), tb)
            block_puts = [put(stage_k, rows, FK0, put_sems.at[1]), put(stage_v, rows, FV0, put_sems.at[2])]
            pair_put = put(stage_q, pl.ds(0, t), FQ0, put_sems.at[0])

            @pl.when(jnp.logical_or(g > 0, ki > 0))
            def _():
                for cp in block_puts:
                    cp.wait()

            for u in range(2):
                stage_k[:, HEAD * u:HEAD * (u + 1)] = dk_s[u].astype(BF16)
                stage_v[:, HEAD * u:HEAD * (u + 1)] = dv_s[u].astype(BF16)
                dck_ref[u] = -dc_s[u]
            for cp in block_puts:
                cp.start()

            @pl.when(ki == nb - 1)
            def _():
                @pl.when(g > 0)
                def _():
                    pair_put.wait()

                stage_q[...] = dq_acc[...].astype(BF16)
                pair_put.start()
                dcq_ref[...] = dcq_s[...]

            @pl.when(jnp.logical_and(g == n_pairs - 1, ki == nb - 1))
            def _():
                for cp in block_puts + [pair_put]:
                    cp.wait()
        else:
            dkv_ref[...] = jnp.concatenate([dk_s[0, :, :HEAD], dv_s[0], dk_s[1, :, :HEAD], dv_s[1]], axis=1).astype(BF16)
            dkr_ref[...] = jnp.concatenate([dk_s[0, :, HEAD:], dk_s[1, :, HEAD:]], axis=1)

        if rider:
            @pl.when(jnp.logical_and(g == n_pairs - 1, ki == nb - 1))
            def _():
                rider.wait(*rider_refs)

    stat = pl.BlockSpec((2, nb, 1, tb), lambda g, i: (g, 0, 0, 0))
    aliases = {}
    if fox:
        proj, d_o, lse4, delta4, c_row4, c_col, dproj = operands
        ins = [proj, proj, proj, d_o, lse4, delta4, c_row4, c_col, dproj]
        any_spec = pl.BlockSpec(memory_space=pl.ANY)
        in_specs = [pl.BlockSpec((t, pair), lambda g, i: (0, FQ0 // pair + g)),
                    pl.BlockSpec((tb, pair), lambda g, i: (i, FK0 // pair + g)),
                    pl.BlockSpec((tb, pair), lambda g, i: (i, FV0 // pair + g)),
                    pl.BlockSpec((t, pair), lambda g, i: (0, n_pairs + g)),
                    stat, stat, stat, pl.BlockSpec((2, tb, 1), lambda g, i: (g, i, 0)), any_spec]
        aliases = {8: 0}
        out_specs = [any_spec, pl.BlockSpec((2, tb, 1), lambda g, i: (g, i, 0)), stat]
        out_shape = [jax.ShapeDtypeStruct(dproj.shape, dproj.dtype), jax.ShapeDtypeStruct((N_HEADS, t, 1), F32),
                     jax.ShapeDtypeStruct((N_HEADS, nb, 1, tb), F32)]
        scratch = [pltpu.VMEM((t, pair), F32), pltpu.VMEM((2, tb, HEAD), F32), pltpu.VMEM((2, tb, HEAD), F32),
                   pltpu.VMEM((2, tb, 1), F32), pltpu.VMEM((2, nb, 1, tb), F32),
                   pltpu.VMEM((t, pair), BF16), pltpu.VMEM((tb, pair), BF16), pltpu.VMEM((tb, pair), BF16),
                   pltpu.SemaphoreType.DMA((3,))]
    else:
        q_full, kv, kr, d_o, lse4, delta4 = operands
        ins = [q_full, kv, kr, d_o, lse4, delta4]
        in_specs = [pl.BlockSpec((t, 2 * pair), lambda g, i: (0, g)),
                    pl.BlockSpec((tb, 2 * pair), lambda g, i: (i, g)),
                    pl.BlockSpec((tb, HEAD), lambda g, i: (i, 0)),
                    pl.BlockSpec((t, pair), lambda g, i: (0, g)),
                    stat, stat]
        out_specs = [pl.BlockSpec((t, 2 * pair), lambda g, i: (0, g)), pl.BlockSpec((tb, 2 * pair), lambda g, i: (i, g)),
                     pl.BlockSpec((tb, pair), lambda g, i: (i, g))]
        out_shape = [jax.ShapeDtypeStruct((t, 2048), F32), jax.ShapeDtypeStruct((t, 2048), BF16),
                     jax.ShapeDtypeStruct((t, 1024), F32)]
        scratch = [pltpu.VMEM((2, tb, 2 * HEAD), F32), pltpu.VMEM((2, tb, HEAD), F32)]
    assert (len(ins), len(out_specs), len(scratch)) == (n_in, n_out, n_scr)
    if rider:
        any_spec = pl.BlockSpec(memory_space=pl.ANY)
        aliases = {**aliases, **{n_in + i_in: n_out + i_out for i_in, i_out in rider.aliases.items()}}
        ins = ins + list(rider.operands)
        in_specs = in_specs + [any_spec] * n_rin
        out_specs = out_specs + [any_spec] * n_rout
        out_shape = out_shape + list(rider.out_shape)
        scratch = scratch + list(rider.scratch)
    return pl.pallas_call(
        body, name=name, grid=(n_pairs, nb), in_specs=in_specs, out_specs=out_specs, out_shape=out_shape,
        scratch_shapes=scratch, input_output_aliases=aliases, compiler_params=_params(2),
    )(*ins)


def _mid_bwd(dq_full, dkr, cos_t, sin_t, dck, flog, bf_row, *, tm):
    t = dq_full.shape[0]
    n = t // tm

    def body(dq_ref, dkr_ref, cos_ref, sin_ref, dck_ref, fl_ref, bf_ref,
             dq2_ref, dkraw_ref, dfl_ref, dbf_ref, carry_ref):
        i = pl.program_id(0)

        @pl.when(i == 0)
        def _():
            carry_ref[...] = jnp.zeros_like(carry_ref)
            dbf_ref[...] = jnp.zeros_like(dbf_ref)

        c, s = cos_ref[...], sin_ref[...]
        dkr_sum = jnp.zeros((tm, LANE), F32)
        for h in range(N_HEADS):
            dq2_ref[:, 256 * h:256 * h + 128] = dq_ref[:, 256 * h:256 * h + 128].astype(BF16)
            dq2_ref[:, 256 * h + 128:256 * h + 256] = _rot_bwd(dq_ref[:, 256 * h + 128:256 * h + 256], c, s).astype(BF16)
            dkr_sum = dkr_sum + dkr_ref[:, HEAD * h:HEAD * (h + 1)]
        dkraw_ref[...] = _rot_bwd(dkr_sum, c, s).astype(BF16)

        dc = dck_ref[...]
        row = lax.broadcasted_iota(jnp.int32, (tm, tm), 0)
        col = lax.broadcasted_iota(jnp.int32, (tm, tm), 1)
        tri = (col >= row).astype(BF16)
        acc = carry_ref[0:1, :]
        for part in _split3(dc):
            acc = acc + jnp.dot(tri, part, preferred_element_type=F32)
        carry_ref[0:1, :] = carry_ref[0:1, :] + jnp.sum(dc, axis=0, keepdims=True)
        z = fl_ref[...] + bf_ref[...]
        dz = acc / (1.0 + jnp.exp(z))
        dfl_ref[...] = dz.astype(BF16)
        dbf_ref[...] += jnp.sum(dz, axis=0, keepdims=True)

    rev = lambda w: pl.BlockSpec((tm, w), lambda i: (n - 1 - i, 0))
    vec = lambda w: pl.BlockSpec((1, w), lambda i: (0, 0))
    return pl.pallas_call(
        body, name="mid_bwd", grid=(n,),
        in_specs=[rev(2048), rev(1024), rev(LANE), rev(LANE), rev(LANE), rev(LANE), vec(LANE)],
        out_specs=[rev(2048), rev(LANE), rev(LANE), vec(LANE)],
        out_shape=[jax.ShapeDtypeStruct((t, 2048), BF16), jax.ShapeDtypeStruct((t, LANE), BF16),
                   jax.ShapeDtypeStruct((t, LANE), BF16), jax.ShapeDtypeStruct((1, LANE), F32)],
        scratch_shapes=[pltpu.VMEM((8, LANE), F32)],
        compiler_params=_params(1),
    )(dq_full, dkr, cos_t, sin_t, dck, flog, bf_row)


def _norm_bwd(proj, dqn, dkvn, g_q, g_kv, dkr_raw, dfl, dproj, *, tm):
    t = proj.shape[0]
    assert (KR0, FL0, KVL0, LAT_W) == (Q_RANK, Q_RANK + LANE, Q_RANK + 2 * LANE, Q_RANK + 2 * LANE + KV_RANK)

    def body(p_ref, dqn_ref, dkvn_ref, gq_ref, gkv_ref, dkr_ref, dfl_ref, _, dproj_ref, dgq_ref, dgkv_ref):
        i = pl.program_id(0)

        @pl.when(i == 0)
        def _():
            dgq_ref[...] = jnp.zeros_like(dgq_ref)
            dgkv_ref[...] = jnp.zeros_like(dgkv_ref)

        d_lat = []
        for lo, w, dn_ref, g_ref, dg_ref in ((QL0, Q_RANK, dqn_ref, gq_ref, dgq_ref),
                                             (KVL0, KV_RANK, dkvn_ref, gkv_ref, dgkv_ref)):
            xv = p_ref[:, lo:lo + w].astype(F32)
            r = lax.rsqrt(jnp.mean(xv * xv, axis=-1, keepdims=True) + NORM_EPS)
            xh = xv * r
            dn = dn_ref[...]
            dg_ref[...] += jnp.sum(dn * xh, axis=0, keepdims=True)
            dxh = dn * g_ref[...]
            d_lat.append((r * (dxh - xh * jnp.mean(dxh * xh, axis=-1, keepdims=True))).astype(BF16))
        dproj_ref[...] = jnp.concatenate([d_lat[0], dkr_ref[...], dfl_ref[...], d_lat[1]], axis=1)

    row = lambda w: pl.BlockSpec((tm, w), lambda i: (i, 0))
    vec = lambda w: pl.BlockSpec((1, w), lambda i: (0, 0))
    return pl.pallas_call(
        body, name="norm_bwd", grid=(t // tm,),
        in_specs=[row(LAT_W), row(Q_RANK), row(KV_RANK), vec(Q_RANK), vec(KV_RANK), row(LANE), row(LANE),
                  pl.BlockSpec(memory_space=pl.ANY)],
        out_specs=[row(LAT_W), vec(Q_RANK), vec(KV_RANK)],
        out_shape=[jax.ShapeDtypeStruct(dproj.shape, dproj.dtype),
                   jax.ShapeDtypeStruct((1, Q_RANK), F32), jax.ShapeDtypeStruct((1, KV_RANK), F32)],
        input_output_aliases={7: 0},
        compiler_params=_params(1),
    )(proj, dqn, dkvn, g_q, g_kv, dkr_raw, dfl, dproj)


def _prenorm_bwd(dh, x, g, dy, *, tm):
    t = x.shape[0]

    def body(dh_ref, x_ref, g_ref, dy_ref, gx_ref, dg_ref):
        i = pl.program_id(0)

        @pl.when(i == 0)
        def _():
            dg_ref[...] = jnp.zeros_like(dg_ref)

        xv = x_ref[...]
        r = lax.rsqrt(jnp.mean(xv * xv, axis=-1, keepdims=True) + NORM_EPS)
        xh = xv * r
        dn = dh_ref[...]
        dg_ref[...] += jnp.sum(dn * xh, axis=0, keepdims=True)
        dxh = dn * g_ref[...]
        gx_ref[...] = dy_ref[...] + r * (dxh - xh * jnp.mean(dxh * xh, axis=-1, keepdims=True))

    row = pl.BlockSpec((tm, D_MODEL), lambda i: (i, 0))
    vec = pl.BlockSpec((1, D_MODEL), lambda i: (0, 0))
    return pl.pallas_call(
        body, name="prenorm_bwd", grid=(t // tm,),
        in_specs=[row, row, vec, row], out_specs=[row, vec],
        out_shape=[jax.ShapeDtypeStruct((t, D_MODEL), F32), jax.ShapeDtypeStruct((1, D_MODEL), F32)],
        compiler_params=_params(1),
    )(dh, x, g, dy)


def _adam_math(w, g, m, v):
    m = ADAM_B1 * m + (1.0 - ADAM_B1) * g
    v = ADAM_B2 * v + (1.0 - ADAM_B2) * (g * g)
    m_hat = m / (1.0 - ADAM_B1 ** ADAM_STEP)
    v_hat = v / (1.0 - ADAM_B2 ** ADAM_STEP)
    delta = -ADAM_LR * (m_hat / (jnp.sqrt(v_hat) + ADAM_EPS) + ADAM_WD * w)
    return delta, m, v


def _adamw(land, w, m, v, *, tr, name):
    rows, cols = w.shape

    def body(l_ref, w_ref, m_ref, v_ref, g_ref, d_ref, nm_ref, nv_ref):
        g = l_ref[0].astype(F32)
        for s in range(1, N_CHIPS):
            g = g + l_ref[s].astype(F32)
        g_ref[...] = g
        d_ref[...], nm_ref[...], nv_ref[...] = _adam_math(w_ref[...], g, m_ref[...], v_ref[...])

    blk = pl.BlockSpec((tr, cols), lambda i: (i, 0))
    return pl.pallas_call(
        body, name=name, grid=(rows // tr,),
        in_specs=[pl.BlockSpec((N_CHIPS, tr, cols), lambda i: (0, i, 0)), blk, blk, blk],
        out_specs=[blk, blk, blk, blk],
        out_shape=[jax.ShapeDtypeStruct((rows, cols), F32)] * 4,
        compiler_params=_params(1),
    )(land, w, m, v)


def _adamw_small(gathered, w, m, v):
    def body(a_ref, w_ref, m_ref, v_ref, g_ref, d_ref, nm_ref, nv_ref):
        g = a_ref[0:SMALL_ROWS, :]
        for s in range(1, N_DEV):
            g = g + a_ref[SMALL_ROWS * s:SMALL_ROWS * (s + 1), :]
        g_ref[...] = g
        d_ref[...], nm_ref[...], nv_ref[...] = _adam_math(w_ref[...], g, m_ref[...], v_ref[...])

    return pl.pallas_call(
        body, name="adamw_small",
        out_shape=[jax.ShapeDtypeStruct((SMALL_ROWS, SMALL_COLS), F32)] * 4,
        compiler_params=_params(),
    )(gathered, w, m, v)


def _place():
    x, y, c = lax.axis_index("x"), lax.axis_index("y"), lax.axis_index("c")
    return x, y, c


def _flip(p, k):
    x, y, c = p
    return (1 - x if k & 4 else x, 1 - y if k & 2 else y, 1 - c if k & 1 else c)


def _index(p):
    return 4 * p[0] + 2 * p[1] + p[2]


def _all_gather(shard):
    rows = shard.shape[0]
    half = rows // 2
    assert rows % 32 == 0
    hbm = pl.BlockSpec(memory_space=pl.ANY)

    def body(in_ref, out_ref, send_sems, recv_sems, local_sem):
        me = _place()
        sibling, x_nbr, y_nbr, diagonal = _flip(me, 1), _flip(me, 4), _flip(me, 2), _flip(me, 6)
        top, bottom = pl.ds(0, half), pl.ds(half, half)

        def copy(k, block, to, part=None, src=None):
            dst = out_ref.at[_index(block)] if part is None else out_ref.at[_index(block), part]
            return pltpu.make_async_remote_copy(
                src_ref=dst if src is None else src, dst_ref=dst, send_sem=send_sems.at[k], recv_sem=recv_sems.at[k],
                device_id=to, device_id_type=MESH)

        mine = pltpu.make_async_copy(in_ref, out_ref.at[_index(me)], local_sem)
        mine.start()
        sent = [copy(0, me, sibling, src=in_ref), copy(1, me, x_nbr, src=in_ref), copy(2, me, y_nbr, src=in_ref)]
        for cp in sent:
            cp.start()
        arrivals = [(1, x_nbr, None, [(3, sibling, None), (5, y_nbr, top)]),
                    (2, y_nbr, None, [(4, sibling, None), (6, x_nbr, bottom)]),
                    (5, diagonal, top, [(7, sibling, top)]),
                    (6, diagonal, bottom, [(8, sibling, bottom)])]
        for k, block, part, onward in arrivals:
            copy(k, block, me, part).wait_recv()
            for k_on, to, part_on in onward:
                cp = copy(k_on, block, to, part_on)
                cp.start()
                sent.append(cp)
        other = lambda p: _flip(p, 1)
        for k, block, part in ((0, sibling, None), (3, other(x_nbr), None), (4, other(y_nbr), None),
                               (7, other(diagonal), top), (8, other(diagonal), bottom)):
            copy(k, block, me, part).wait_recv()
        for cp in sent:
            cp.wait_send()
        mine.wait()

    return pl.pallas_call(
        body, name="all_gather_w_in",
        in_specs=[hbm], out_specs=hbm,
        out_shape=jax.ShapeDtypeStruct((N_DEV,) + shard.shape, shard.dtype),
        scratch_shapes=[pltpu.SemaphoreType.DMA((9,)), pltpu.SemaphoreType.DMA((9,)), pltpu.SemaphoreType.DMA(())],
    )(shard)


class _Exchange:
    def __init__(self, tasks):
        self.tasks = tasks
        taken = [land for _, _, land, _, _ in tasks if land is not None]
        self.operands = [src for src, _, _, _, _ in tasks] + taken
        self.out_shape = [
            jax.ShapeDtypeStruct((N_CHIPS,) + ((2,) if by_core else ()) + (src.shape if same else src.shape[1:]), src.dtype)
            for src, _, _, same, by_core in tasks]
        self.aliases, n_taken = {}, 0
        for a, (_, _, land, _, _) in enumerate(tasks):
            if land is not None:
                self.aliases[len(tasks) + n_taken] = a
                n_taken += 1
        self.scratch = [pltpu.SemaphoreType.DMA((N_CHIPS,)), pltpu.SemaphoreType.DMA((N_CHIPS,)),
                        pltpu.SemaphoreType.DMA(())] * len(tasks)

    def _copies(self, ins, outs, scratch):
        x, y, core = _place()
        my = 2 * x + y
        for a, (_, chips, _, same, by_core) in enumerate(self.tasks):
            send_sems, recv_sems, local_sem = scratch[3 * a:3 * a + 3]
            slot = (lambda s, a=a, by_core=by_core: outs[a].at[s, core] if by_core else outs[a].at[s])
            for i, j in enumerate(chips):
                src = ins[a] if same else ins[a].at[i]
                pair = jnp.bitwise_xor(my, j)
                remote = pltpu.make_async_remote_copy(
                    src_ref=src, dst_ref=slot(my), send_sem=send_sems.at[pair], recv_sem=recv_sems.at[pair],
                    device_id=(j >> 1, j & 1, core), device_id_type=MESH)
                local = pltpu.make_async_copy(src, slot(my), local_sem)
                yield j, my, core, remote, local, slot, (send_sems, recv_sems)

    def start(self, ins, outs, scratch):
        for j, my, _, remote, local, _, _ in self._copies(ins, outs, scratch):
            pl.when(my != j)(remote.start)
            pl.when(my == j)(local.start)

    def wait(self, ins, outs, scratch):
        for j, my, core, remote, local, slot, (send_sems, recv_sems) in self._copies(ins, outs, scratch):
            pl.when(my != j)(remote.wait_send)

            @pl.when(my == j)
            def _():
                local.wait()
                for s in range(N_CHIPS):
                    if s != j:
                        pltpu.make_async_remote_copy(
                            src_ref=slot(s), dst_ref=slot(s), send_sem=send_sems.at[j ^ s], recv_sem=recv_sems.at[j ^ s],
                            device_id=(s >> 1, s & 1, core), device_id_type=MESH).wait_recv()


N_CHIPS = 4
ALL_CHIPS = tuple(range(N_CHIPS))


def _to_other_core(parts, *, name):
    n_arr = len(parts)
    hbm = pl.BlockSpec(memory_space=pl.ANY)

    def body(*refs):
        srcs, lands = refs[:n_arr], refs[n_arr:2 * n_arr]
        send_sems, recv_sems = refs[2 * n_arr:]
        me = _place()
        copies = [pltpu.make_async_remote_copy(src_ref=srcs[a].at[1 - me[2]], dst_ref=lands[a], send_sem=send_sems.at[a],
                                               recv_sem=recv_sems.at[a], device_id=_flip(me, 1), device_id_type=MESH)
                  for a in range(n_arr)]
        for cp in copies:
            cp.start()
        for cp in copies:
            cp.wait()

    return pl.pallas_call(
        body, name=name, in_specs=[hbm] * n_arr, out_specs=[hbm] * n_arr,
        out_shape=[jax.ShapeDtypeStruct(p.shape[1:], p.dtype) for p in parts],
        scratch_shapes=[pltpu.SemaphoreType.DMA((n_arr,)), pltpu.SemaphoreType.DMA((n_arr,))],
    )(*parts)


def _share_with_other_core(gathered, *, name):
    n_arr = len(gathered)
    hbm = pl.BlockSpec(memory_space=pl.ANY)

    def body(*refs):
        bufs = refs[n_arr:2 * n_arr]
        send_sems, recv_sems = refs[2 * n_arr:]
        me = _place()
        copies = []
        for a in range(n_arr):
            for j in range(N_CHIPS):
                block = bufs[a].at[j, me[2]]
                copies.append(pltpu.make_async_remote_copy(
                    src_ref=block, dst_ref=block, send_sem=send_sems.at[N_CHIPS * a + j],
                    recv_sem=recv_sems.at[N_CHIPS * a + j], device_id=_flip(me, 1), device_id_type=MESH))
        for cp in copies:
            cp.start()
        for cp in copies:
            cp.wait()

    return pl.pallas_call(
        body, name=name, in_specs=[hbm] * n_arr, out_specs=[hbm] * n_arr,
        out_shape=[jax.ShapeDtypeStruct(g.shape, g.dtype) for g in gathered],
        scratch_shapes=[pltpu.SemaphoreType.DMA((N_CHIPS * n_arr,)), pltpu.SemaphoreType.DMA((N_CHIPS * n_arr,))],
        input_output_aliases={a: a for a in range(n_arr)},
    )(*gathered)


def _pair_sum(mine, other, core, *, tr, name):
    _, n, rows, cols = mine.shape
    tr = min(tr, rows)

    def body(core_ref, a_ref, b_ref, o_ref):
        o_ref[...] = (a_ref[0].astype(F32) + b_ref[...].astype(F32)).astype(BF16)

    return pl.pallas_call(
        body, name=name,
        grid_spec=pltpu.PrefetchScalarGridSpec(
            num_scalar_prefetch=1, grid=(n, rows // tr),
            in_specs=[pl.BlockSpec((1, 1, tr, cols), lambda j, i, core_ref: (core_ref[0], j, i, 0)),
                      pl.BlockSpec((1, tr, cols), lambda j, i, core_ref: (j, i, 0))],
            out_specs=pl.BlockSpec((1, tr, cols), lambda j, i, core_ref: (j, i, 0))),
        out_shape=jax.ShapeDtypeStruct(other.shape, BF16),
        compiler_params=_params(2),
    )(core, mine, other)


def _gather_small(vec):
    def body(v_ref, out_ref, send_sems, recv_sems, local_sem):
        me = _place()

        def rows(p):
            return out_ref.at[pl.ds(pl.multiple_of(_index(p) * SMALL_ROWS, SMALL_ROWS), SMALL_ROWS), :]

        mine = pltpu.make_async_copy(v_ref, rows(me), local_sem)
        mine.start()
        sends = []
        for k in range(1, N_DEV):
            peer = _flip(me, k)
            cp = pltpu.make_async_remote_copy(src_ref=v_ref, dst_ref=rows(me), send_sem=send_sems.at[k - 1],
                                              recv_sem=recv_sems.at[k - 1], device_id=peer, device_id_type=MESH)
            cp.start()
            sends.append(cp)
        for k in range(1, N_DEV):
            peer = _flip(me, k)
            pltpu.make_async_remote_copy(src_ref=rows(peer), dst_ref=rows(peer), send_sem=send_sems.at[k - 1],
                                         recv_sem=recv_sems.at[k - 1], device_id=peer, device_id_type=MESH).wait_recv()
        for cp in sends:
            cp.wait_send()
        mine.wait()

    return pl.pallas_call(
        body, name="gather_small",
        in_specs=[pl.BlockSpec(memory_space=pltpu.VMEM)], out_specs=pl.BlockSpec(memory_space=pltpu.VMEM),
        out_shape=jax.ShapeDtypeStruct((N_DEV * SMALL_ROWS, SMALL_COLS), F32),
        scratch_shapes=[pltpu.SemaphoreType.DMA((7,)), pltpu.SemaphoreType.DMA((7,)), pltpu.SemaphoreType.DMA],
    )(vec)


def _w_in_nice(gathered):
    pieces, pos = [], 0
    for o0, width, n0 in sorted(_SEGMENTS, key=lambda seg: seg[2]):
        if n0 > pos:
            pieces.append(jnp.zeros((D_MODEL, n0 - pos), gathered.dtype))
        o = o0
        while o < o0 + width:
            d = o // SHARD_IN
            hi = min(o0 + width, (d + 1) * SHARD_IN)
            pieces.append(gathered[d][:, o - d * SHARD_IN:hi - d * SHARD_IN])
            o = hi
        pos = n0 + width
    pieces.append(jnp.zeros((D_MODEL, NP_IN - pos), gathered.dtype))
    return jnp.concatenate(pieces, axis=1)


def _w_in_blocks(chips, dw_lat, dw_rest):
    blocks = []
    for core in range(2):
        for chip in chips:
            lo = (2 * chip + core) * SHARD_IN
            runs = []
            for o0, width, n0 in _SEGMENTS:
                a, b = max(lo, o0), min(lo + SHARD_IN, o0 + width)
                if a < b:
                    n_a, n_b = n0 + a - o0, n0 + b - o0
                    runs.append(dw_lat[:, n_a:n_b] if n_b <= LAT_W else dw_rest[:, n_a - LAT_W:n_b - LAT_W])
            blocks.append(jnp.concatenate(runs, axis=1))
    return jnp.stack(blocks).reshape(2, len(chips), D_MODEL, SHARD_IN)


def _by_core(shards):
    return shards.reshape((N_CHIPS, 2) + shards.shape[1:]).swapaxes(0, 1)


EARLY_CHIPS = (1, 2)
LATE_CHIPS = (0, 3)


def _w_uq_nice(shard):
    z = jnp.zeros((Q_RANK, 32), shard.dtype)
    return jnp.concatenate([shard[:, :128], shard[:, 128:160], z, shard[:, 160:192], z], axis=1)


def _pack_small(g_pre, g_post, g_q, g_kv, b_f, extra=None):
    parts = [g_pre.reshape(-1), g_post.reshape(-1), g_q.reshape(-1), g_kv.reshape(-1), b_f.reshape(-1)]
    if extra is not None:
        parts.append(extra.reshape(-1))
    flat = jnp.concatenate(parts)
    flat = jnp.concatenate([flat, jnp.zeros((SMALL_ROWS * SMALL_COLS - flat.shape[0],), F32)])
    return flat.reshape(SMALL_ROWS, SMALL_COLS)


def _unpack_small(packed):
    flat = packed.reshape(-1)
    o = 0
    out = []
    for n in (D_MODEL, D_MODEL, Q_RANK, KV_RANK, N_HEADS):
        out.append(flat[o:o + n].reshape(1, n))
        o += n
    return out, flat[o]


def kernel(x, positions, g_pre, w_in, g_q_latent, w_uq, g_kv_latent, w_ukv, b_forget, w_out, g_post, loss_target, m_g_pre, m_w_in, m_g_q_latent, m_w_uq, m_g_kv_latent, m_w_ukv, m_b_forget, m_w_out, m_g_post, v_g_pre, v_w_in, v_g_q_latent, v_w_uq, v_g_kv_latent, v_w_ukv, v_b_forget, v_w_out, v_g_post):
    t = x.shape[1]
    tb = min(512, t)
    tm = min(256, t)
    nb = t // tb
    x2 = x.reshape(t, D_MODEL)
    target = loss_target.reshape(t, D_MODEL)
    pos_col = positions.reshape(t, 1).astype(F32)
    bf_row = jnp.concatenate([b_forget.reshape(1, N_HEADS), jnp.zeros((1, LANE - N_HEADS), F32)], axis=1)

    g_in = _all_gather(w_in[0].astype(BF16))
    w_in_n = _w_in_nice(g_in)
    gather_rest = _Exchange([(w, ALL_CHIPS, None, True, True) for w in
                             (_w_uq_nice(w_uq[0].astype(BF16)), w_ukv[0].astype(BF16), w_out[0].astype(BF16))])
    core = lax.axis_index("c").astype(jnp.int32).reshape(1)

    h, h_t = _prenorm(x2, g_pre, tm=tm)
    proj, g_uq, g_ukv, g_out = _mm(h, w_in_n, name="proj_in", out_dtype=BF16, tm=2048, tn=512, tk=2048,
                                   rider=gather_rest)
    g_uq, g_ukv, g_out = _share_with_other_core([g_uq, g_ukv, g_out], name="share_weights")
    w_uq_n = g_uq.reshape(N_DEV, Q_RANK, 256).transpose(1, 0, 2).reshape(Q_RANK, N_HEADS * 256)
    w_ukv_n = g_ukv.reshape(N_DEV, KV_RANK, 256).transpose(1, 0, 2).reshape(KV_RANK, N_HEADS * 256)
    w_out_n = g_out.reshape(D_MODEL, D_MODEL)
    flog = _mm(h, w_in_n[:, FL0:FL0 + LANE], name="proj_flog", out_dtype=F32, tm=1024, tn=LANE, tk=2048)
    qn, kvn, kr, cos_t, sin_t, c, qn_t, kvn_t = _mid_fwd(proj, flog, g_q_latent, g_kv_latent, bf_row, pos_col, tm=tm)
    q_full = _q_up_rope(qn, w_uq_n, cos_t, sin_t, tm=min(1024, t))
    kv = _mm(kvn, w_ukv_n, name="kv_up", out_dtype=BF16, tm=1024, tn=512, tk=KV_RANK)
    c_heads = c[:, :N_HEADS].T
    c_col = c_heads.reshape(N_HEADS, t, 1)
    c_row4 = c_heads.reshape(N_HEADS, nb, 1, tb)
    o_all, og_all, og_t, lse4_mla = _attn_fwd(False, (q_full, kv, kr, proj), t=t, tb=tb, name="mla_fwd")
    o_all, og_all, og_t, lse4_fox = _attn_fwd(True, (proj, c_col, c_row4, o_all, og_all, og_t), t=t, tb=tb,
                                              name="fox_fwd")
    dy, d_o_post, dg_post, loss_part = _out_norm_loss(og_all, w_out_n, x2, target, g_post, tm=min(512, t))

    dw_out = _mm(og_t, d_o_post, name="dw_out", out_dtype=BF16, tm=1024, tn=1024, tk=1024)
    p_out = _by_core(dw_out.reshape(N_DEV, D_MODEL // N_DEV, D_MODEL))
    (o_out,) = _to_other_core([p_out], name="dw_out_to_core")
    s_out = _pair_sum(p_out, o_out, core, tr=256, name="dw_out_pair_sum")
    d_attn, dproj, delta = _dog_gate(d_o_post, w_out_n, o_all, proj, tm=min(1024, t))
    delta4 = delta[:, :2 * N_HEADS].T.reshape(2 * N_HEADS, nb, 1, tb)
    dproj, dck, dcq, l_out = _attn_bwd(True, (proj, d_attn, lse4_fox, delta4[N_HEADS:], c_row4, c_col, dproj),
                                       t=t, tb=tb, name="fox_bwd",
                                       rider=_Exchange([(s_out, ALL_CHIPS, None, False, False)]))
    dw_in_rest = _mm(h_t, dproj, name="dw_in_rest", out_dtype=BF16, tm=2048, tn=512, tk=1024,
                     b_cols=(LAT_W, NP_IN - LAT_W))
    p_in = _w_in_blocks(EARLY_CHIPS, None, dw_in_rest)
    (o_in,) = _to_other_core([p_in], name="dw_in_early_to_core")
    s_in = _pair_sum(p_in, o_in, core, tr=256, name="dw_in_early_pair_sum")
    dq_full, dkv, dkr, l_in = _attn_bwd(False, (q_full, kv, kr, d_attn, lse4_mla, delta4[:N_HEADS]),
                                        t=t, tb=tb, name="mla_bwd",
                                        rider=_Exchange([(s_in, EARLY_CHIPS, None, False, False)]))
    dc_heads = dck.reshape(N_HEADS, t) + dcq.reshape(N_HEADS, t)
    dck_rows = jnp.concatenate([dc_heads.T, jnp.zeros((t, LANE - N_HEADS), F32)], axis=1)
    dq2, dkr_raw, dfl, dbf = _mid_bwd(dq_full, dkr, cos_t, sin_t, dck_rows, flog, bf_row, tm=tm)
    dqn = _mm(dq2, w_uq_n, name="d_qn", nt=True, out_dtype=F32, tm=1024, tn=Q_RANK, tk=2048)
    dkvn = _mm(dkv, w_ukv_n, name="d_kvn", nt=True, out_dtype=F32, tm=1024, tn=KV_RANK, tk=2048)
    dw_uq = _mm(qn_t, dq2, name="dw_uq", out_dtype=BF16, tm=Q_RANK, tn=1024, tk=1024)
    dw_ukv = _mm(kvn_t, dkv, name="dw_ukv", out_dtype=BF16, tm=KV_RANK, tn=1024, tk=1024)
    dproj, dg_q, dg_kv = _norm_bwd(proj, dqn, dkvn, g_q_latent, g_kv_latent, dkr_raw, dfl, dproj, tm=tm)
    dw_in_lat = _mm(h_t, dproj, name="dw_in_lat", out_dtype=BF16, tm=1024, tn=LAT_W, tk=1024, b_cols=(0, LAT_W))
    dw_uq_h = dw_uq.reshape(Q_RANK, N_HEADS, 256)
    s_uq = jnp.concatenate([dw_uq_h[:, :, :160], dw_uq_h[:, :, 192:224]], axis=2).transpose(1, 0, 2)
    s_ukv = dw_ukv.reshape(KV_RANK, N_HEADS, 256).transpose(1, 0, 2)
    late_parts = [_by_core(s_uq), _by_core(s_ukv), _w_in_blocks(LATE_CHIPS, dw_in_lat, dw_in_rest)]
    late_other = _to_other_core(late_parts, name="dw_late_to_core")
    late_sums = [_pair_sum(p, o_, core, tr=256, name=f"dw_late_pair_sum_{i}")
                 for i, (p, o_) in enumerate(zip(late_parts, late_other))]
    late = _Exchange([(late_sums[0], ALL_CHIPS, None, False, False), (late_sums[1], ALL_CHIPS, None, False, False),
                      (late_sums[2], LATE_CHIPS, l_in, False, False)])
    dh, l_uq, l_ukv, l_in = _mm(dproj, w_in_n, name="d_h", nt=True, out_dtype=F32, tm=2048, tn=1024, tk=NP_IN // 4,
                                rider=late)
    grad_x, dg_pre = _prenorm_bwd(dh, x2, g_pre, dy, tm=tm)

    small = _gather_small(_pack_small(dg_pre, dg_post, dg_q, dg_kv, dbf[:, :N_HEADS], loss_part))

    res_in = _adamw(l_in, w_in[0], m_w_in[0], v_w_in[0], tr=256, name="adamw_w_in")
    res_uq = _adamw(l_uq, w_uq[0], m_w_uq[0], v_w_uq[0], tr=256, name="adamw_w_uq")
    res_ukv = _adamw(l_ukv, w_ukv[0], m_w_ukv[0], v_w_ukv[0], tr=256, name="adamw_w_ukv")
    res_out = _adamw(l_out, w_out[0], m_w_out[0], v_w_out[0], tr=128, name="adamw_w_out")
    res_small = _adamw_small(
        small,
        _pack_small(g_pre, g_post, g_q_latent, g_kv_latent, b_forget),
        _pack_small(m_g_pre, m_g_post, m_g_q_latent, m_g_kv_latent, m_b_forget),
        _pack_small(v_g_pre, v_g_post, v_g_q_latent, v_g_kv_latent, v_b_forget))
    small_out = [_unpack_small(r) for r in res_small]
    loss = small_out[0][1]

    def leaves(kind):
        (s_pre, s_post, s_q, s_kv, s_bf), _ = small_out[kind]
        return [s_pre, res_in[kind][None], s_q, res_uq[kind][None], s_kv, res_ukv[kind][None], s_bf,
                res_out[kind][None], s_post]

    return (loss, grad_x.reshape(x.shape), *leaves(0), *leaves(1), *leaves(2), *leaves(3))
```

```python
import functools

import numpy as np
import jax
import jax.numpy as jnp
from jax import lax
from jax.experimental import pallas as pl
from jax.experimental.pallas import tpu as pltpu

F32 = jnp.float32
BF16 = jnp.bfloat16
MESH = pl.DeviceIdType.MESH

N_DEV = 8
D_MODEL = 2048
N_HEADS = 8
HEAD = 128
Q_RANK = 768
KV_RANK = 512
ROPE = 64
D_IN = 6472
SHARD_IN = D_IN // N_DEV
NORM_EPS = 1e-6
ROPE_THETA = 10000.0
MLA_SCALE = (HEAD + ROPE) ** -0.5
FOX_SCALE = HEAD ** -0.5

QL0, KR0, FL0, KVL0, GM0, GF0, FQ0, FK0, FV0, NP_IN = 0, 768, 896, 1024, 1536, 2560, 3584, 4608, 5632, 6656
LAT_W = GM0
LANE = 128
_SEGMENTS = ((0, 768, QL0), (768, 512, KVL0), (1280, 32, KR0), (1312, 32, KR0 + 64), (1344, 1024, GM0),
             (2368, 3072, FQ0), (5440, 8, FL0), (5448, 1024, GF0))
LOG2E = 1.4426950408889634

ADAM_LR = 0.001
ADAM_B1 = 0.9
ADAM_B2 = 0.999
ADAM_EPS = 1e-08
ADAM_WD = 0.01
ADAM_STEP = 10

VMEM_LIMIT_BYTES = 56 * 1024 * 1024
SMALL_ROWS, SMALL_COLS = 8, 768


def _params(n_grid=0):
    return pltpu.CompilerParams(vmem_limit_bytes=VMEM_LIMIT_BYTES,
                                dimension_semantics=("arbitrary",) * n_grid if n_grid else None)


def _sigmoid(z):
    return 1.0 / (1.0 + jnp.exp(-z))


def _split3(v):
    a = v.astype(BF16)
    r = v - a.astype(F32)
    b = r.astype(BF16)
    c = (r - b.astype(F32)).astype(BF16)
    return a, b, c


def _mm(a, b, *, name, nt=False, out_dtype=F32, tm=1024, tn=512, tk=2048, b_cols=None, rider=None):
    m, k_dim = a.shape
    n = b.shape[0] if nt else b.shape[1]
    col0 = 0
    if b_cols is not None:
        assert not nt
        col0, n = b_cols
    assert (b.shape[1] if nt else b.shape[0]) == k_dim
    tm, tn, tk = min(tm, m), min(tn, n), min(tk, k_dim)
    assert m % tm == 0 and n % tn == 0 and k_dim % tk == 0 and col0 % tn == 0, (name, a.shape, b.shape)
    nk = k_dim // tk
    j0 = col0 // tn
    grid = (m // tm, n // tn, nk)
    dims = (((1,), (1 if nt else 0,)), ((), ()))
    n_rin = len(rider.operands) if rider else 0
    n_rout = len(rider.out_shape) if rider else 0

    def body(*refs):
        a_ref, b_ref = refs[:2]
        o_ref = refs[2 + n_rin]
        acc_ref = refs[3 + n_rin + n_rout]
        i, j, k = pl.program_id(0), pl.program_id(1), pl.program_id(2)
        if rider:
            rider_refs = (refs[2:2 + n_rin], refs[3 + n_rin:3 + n_rin + n_rout], refs[4 + n_rin + n_rout:])

            @pl.when(jnp.logical_and(i == 0, jnp.logical_and(j == 0, k == 0)))
            def _():
                rider.start(*rider_refs)

        @pl.when(k == 0)
        def _():
            acc_ref[...] = jnp.zeros_like(acc_ref)

        acc_ref[...] += lax.dot_general(a_ref[...], b_ref[...], dims, preferred_element_type=F32)

        @pl.when(k == nk - 1)
        def _():
            o_ref[...] = acc_ref[...].astype(o_ref.dtype)

        if rider:
            @pl.when(jnp.logical_and(i == grid[0] - 1, jnp.logical_and(j == grid[1] - 1, k == nk - 1)))
            def _():
                rider.wait(*rider_refs)

    b_spec = (pl.BlockSpec((tn, tk), lambda i, j, k: (j, k)) if nt
              else pl.BlockSpec((tk, tn), lambda i, j, k: (k, j0 + j)))
    a_spec = pl.BlockSpec((tm, tk), lambda i, j, k: (i, k))
    any_spec = pl.BlockSpec(memory_space=pl.ANY)
    out = pl.pallas_call(
        body, name=name, grid=grid,
        in_specs=[a_spec, b_spec] + [any_spec] * n_rin,
        out_specs=[pl.BlockSpec((tm, tn), lambda i, j, k: (i, j))] + [any_spec] * n_rout,
        out_shape=[jax.ShapeDtypeStruct((m, n), out_dtype)] + (list(rider.out_shape) if rider else []),
        scratch_shapes=[pltpu.VMEM((tm, tn), F32)] + (list(rider.scratch) if rider else []),
        input_output_aliases={2 + i_in: 1 + i_out for i_in, i_out in rider.aliases.items()} if rider else {},
        compiler_params=_params(3),
    )(a, b, *(rider.operands if rider else ()))
    return out if rider else out[0]


def _prenorm(x, g, rider, *, tm):
    t = x.shape[0]
    n_steps = t // tm
    n_rin, n_rout = len(rider.operands), len(rider.out_shape)

    def body(*refs):
        x_ref, g_ref = refs[:2]
        h_ref, ht_ref = refs[2 + n_rin:4 + n_rin]
        rider_refs = (refs[2:2 + n_rin], refs[4 + n_rin:4 + n_rin + n_rout], refs[4 + n_rin + n_rout:])
        i = pl.program_id(0)
        pl.when(i == 0)(lambda: rider.start(*rider_refs))
        xv = x_ref[...]
        r = lax.rsqrt(jnp.mean(xv * xv, axis=-1, keepdims=True) + NORM_EPS)
        h = xv * r * g_ref[...]
        h_ref[...] = h.astype(BF16)
        ht_ref[...] = h.T.astype(BF16)
        pl.when(i == n_steps - 1)(lambda: rider.wait(*rider_refs))

    any_spec = pl.BlockSpec(memory_space=pl.ANY)
    return pl.pallas_call(
        body, name="prenorm", grid=(n_steps,),
        in_specs=[pl.BlockSpec((tm, D_MODEL), lambda i: (i, 0)), pl.BlockSpec((1, D_MODEL), lambda i: (0, 0))]
        + [any_spec] * n_rin,
        out_specs=[pl.BlockSpec((tm, D_MODEL), lambda i: (i, 0)), pl.BlockSpec((D_MODEL, tm), lambda i: (0, i))]
        + [any_spec] * n_rout,
        out_shape=[jax.ShapeDtypeStruct((t, D_MODEL), BF16), jax.ShapeDtypeStruct((D_MODEL, t), BF16)]
        + list(rider.out_shape),
        scratch_shapes=list(rider.scratch),
        compiler_params=_params(1),
    )(x, g, *rider.operands)


def _rope_rows():
    inv = (np.float32(ROPE_THETA) ** (-np.arange(0, ROPE, 2, dtype=np.float32) / np.float32(ROPE))).astype(np.float32)
    invf = np.zeros((1, LANE), np.float32)
    sgn = np.zeros((1, LANE), np.float32)
    invf[0, 0:32] = inv
    invf[0, 64:96] = inv
    sgn[0, 0:32] = -1.0
    sgn[0, 64:96] = 1.0
    return jnp.asarray(invf), jnp.asarray(sgn)


def _rot(v, cos_t, sin_t):
    return v * cos_t + pltpu.roll(v, 64, 1) * sin_t


def _rot_bwd(dv, cos_t, sin_t):
    return dv * cos_t + pltpu.roll(dv * sin_t, 64, 1)


def _mid_fwd(proj, flog, g_q, g_kv, bf_row, pos_col, *, tm):
    t = proj.shape[0]
    invf, sgn = _rope_rows()

    def body(p_ref, fl_ref, gq_ref, gkv_ref, bf_ref, pos_ref, invf_ref, sgn_ref,
             qn_ref, kvn_ref, kr_ref, cos_ref, sin_ref, c_ref, qnt_ref, kvnt_ref, carry_ref):
        i = pl.program_id(0)

        @pl.when(i == 0)
        def _():
            carry_ref[...] = jnp.zeros_like(carry_ref)

        ql = p_ref[:, QL0:QL0 + Q_RANK].astype(F32)
        r = lax.rsqrt(jnp.mean(ql * ql, axis=-1, keepdims=True) + NORM_EPS)
        qn = ql * r * gq_ref[...]
        qn_ref[...] = qn.astype(BF16)
        qnt_ref[...] = qn.T.astype(BF16)
        kvl = p_ref[:, KVL0:KVL0 + KV_RANK].astype(F32)
        r = lax.rsqrt(jnp.mean(kvl * kvl, axis=-1, keepdims=True) + NORM_EPS)
        kvn = kvl * r * gkv_ref[...]
        kvn_ref[...] = kvn.astype(BF16)
        kvnt_ref[...] = kvn.T.astype(BF16)

        ang = pos_ref[...] * invf_ref[...]
        cos_t = jnp.cos(ang)
        sin_t = jnp.sin(ang) * sgn_ref[...]
        cos_ref[...] = cos_t
        sin_ref[...] = sin_t
        kr_ref[...] = _rot(p_ref[:, KR0:KR0 + LANE].astype(F32), cos_t, sin_t).astype(BF16)

        z = fl_ref[...] + bf_ref[...]
        logf = jnp.minimum(z, 0.0) - jnp.log(1.0 + jnp.exp(-jnp.abs(z)))
        row = lax.broadcasted_iota(jnp.int32, (tm, tm), 0)
        col = lax.broadcasted_iota(jnp.int32, (tm, tm), 1)
        tri = (col <= row).astype(BF16)
        acc = carry_ref[0:1, :]
        for part in _split3(logf):
            acc = acc + jnp.dot(tri, part, preferred_element_type=F32)
        c_ref[...] = acc * (1.0 / FOX_SCALE)
        carry_ref[0:1, :] = carry_ref[0:1, :] + jnp.sum(logf, axis=0, keepdims=True)

    row_spec = lambda w: pl.BlockSpec((tm, w), lambda i: (i, 0))
    vec_spec = lambda w: pl.BlockSpec((1, w), lambda i: (0, 0))
    return pl.pallas_call(
        body, name="mid_fwd", grid=(t // tm,),
        in_specs=[row_spec(LAT_W), row_spec(LANE), vec_spec(Q_RANK), vec_spec(KV_RANK), vec_spec(LANE),
                  pl.BlockSpec((tm, 1), lambda i: (i, 0)), vec_spec(LANE), vec_spec(LANE)],
        out_specs=[row_spec(Q_RANK), row_spec(KV_RANK), row_spec(LANE), row_spec(LANE), row_spec(LANE), row_spec(LANE),
                   pl.BlockSpec((Q_RANK, tm), lambda i: (0, i)), pl.BlockSpec((KV_RANK, tm), lambda i: (0, i))],
        out_shape=[jax.ShapeDtypeStruct((t, Q_RANK), BF16), jax.ShapeDtypeStruct((t, KV_RANK), BF16),
                   jax.ShapeDtypeStruct((t, LANE), BF16), jax.ShapeDtypeStruct((t, LANE), F32),
                   jax.ShapeDtypeStruct((t, LANE), F32), jax.ShapeDtypeStruct((t, LANE), F32),
                   jax.ShapeDtypeStruct((Q_RANK, t), BF16), jax.ShapeDtypeStruct((KV_RANK, t), BF16)],
        scratch_shapes=[pltpu.VMEM((8, LANE), F32)],
        compiler_params=_params(1),
    )(proj, flog, g_q, g_kv, bf_row, pos_col, invf, sgn)


def _q_up_rope(qn, w_uq_n, cos_t, sin_t, *, tm):
    t = qn.shape[0]
    tn = 2 * 256

    def body(a_ref, b_ref, cos_ref, sin_ref, o_ref):
        q = jnp.dot(a_ref[...], b_ref[...], preferred_element_type=F32)
        c, s = cos_ref[...], sin_ref[...]
        for u in range(tn // 256):
            o_ref[:, 256 * u:256 * u + 128] = q[:, 256 * u:256 * u + 128].astype(BF16)
            o_ref[:, 256 * u + 128:256 * u + 256] = _rot(q[:, 256 * u + 128:256 * u + 256], c, s).astype(BF16)

    return pl.pallas_call(
        body, name="q_up_rope", grid=(t // tm, N_HEADS * 256 // tn),
        in_specs=[pl.BlockSpec((tm, Q_RANK), lambda i, j: (i, 0)), pl.BlockSpec((Q_RANK, tn), lambda i, j: (0, j)),
                  pl.BlockSpec((tm, LANE), lambda i, j: (i, 0)), pl.BlockSpec((tm, LANE), lambda i, j: (i, 0))],
        out_specs=pl.BlockSpec((tm, tn), lambda i, j: (i, j)),
        out_shape=jax.ShapeDtypeStruct((t, N_HEADS * 256), BF16),
        compiler_params=_params(2),
    )(qn, w_uq_n, cos_t, sin_t)


def _attn_fwd(fox, operands, *, t, tb, name):
    nb = t // tb
    scale = FOX_SCALE if fox else MLA_SCALE
    exp2_scale = scale * LOG2E
    pair = 2 * HEAD
    pair0 = N_HEADS // 2 if fox else 0
    q_w = HEAD if fox else 2 * HEAD
    nt_dims = (((1,), (1,)), ((), ()))
    tn_dims = (((0,), (0,)), ((), ()))

    def body(*refs):
        if fox:
            (q_ref, k_ref, v_ref, gate_ref, cq_ref, ck_ref, _, _, _,
             o_ref, og_ref, ogt_ref, lse_ref, m_s, l_s, acc_s) = refs
        else:
            q_ref, kv_ref, kr_ref, gate_ref, o_ref, og_ref, ogt_ref, lse_ref, m_s, l_s, acc_s = refs
        qi = pl.program_id(1)
        m_s[...] = jnp.full_like(m_s, -jnp.inf)
        l_s[...] = jnp.zeros_like(l_s)
        acc_s[...] = jnp.zeros_like(acc_s)

        def chunk(kc, masked):
            off = pl.multiple_of(kc * tb, tb)
            scores = []
            for u in range(2):
                q = q_ref[:, q_w * u:q_w * (u + 1)]
                if fox:
                    kk = k_ref[pl.ds(off, tb), HEAD * u:HEAD * (u + 1)]
                else:
                    kk = jnp.concatenate([kv_ref[pl.ds(off, tb), pair * u:pair * u + HEAD],
                                          kr_ref[pl.ds(off, tb), :]], axis=1)
                s = lax.dot_general(kk, q, nt_dims, preferred_element_type=F32)
                if fox:
                    s = s + cq_ref[u, 0] - ck_ref[u, pl.ds(off, tb), :]
                if masked:
                    row = lax.broadcasted_iota(jnp.int32, (tb, tb), 0)
                    col = lax.broadcasted_iota(jnp.int32, (tb, tb), 1)
                    s = jnp.where(row <= col, s, -jnp.inf)
                scores.append(s)
            for u in range(2):
                s = scores[u]
                m_prev = m_s[u]
                m_new = jnp.maximum(m_prev, jnp.max(s, axis=0, keepdims=True))
                alpha = jnp.exp2((m_prev - m_new) * exp2_scale)
                p = jnp.exp2((s - m_new) * exp2_scale)
                l_s[u] = alpha * l_s[u] + jnp.sum(p, axis=0, keepdims=True)
                if fox:
                    vv = v_ref[pl.ds(off, tb), HEAD * u:HEAD * (u + 1)]
                else:
                    vv = kv_ref[pl.ds(off, tb), pair * u + HEAD:pair * (u + 1)]
                acc_s[u] = alpha * acc_s[u] + lax.dot_general(vv, p.astype(BF16), tn_dims,
                                                              preferred_element_type=F32)
                m_s[u] = m_new

        def loop_body(kc, carry):
            chunk(kc, False)
            return carry

        lax.fori_loop(0, qi, loop_body, 0)
        chunk(qi, True)
        for u in range(2):
            cols = slice(HEAD * u, HEAD * (u + 1))
            o_t = acc_s[u] / l_s[u]
            o = o_t.T
            o_ref[:, cols] = o
            g = gate_ref[:, cols].astype(F32)
            silu = g * _sigmoid(g)
            og_ref[:, cols] = (o * silu).astype(BF16)
            ogt_ref[cols, :] = (o_t * silu.T).astype(BF16)
            lse_ref[u, 0] = m_s[u] * scale + jnp.log(l_s[u])

    any_spec = pl.BlockSpec(memory_space=pl.ANY)
    row_stat = pl.BlockSpec((2, 1, 1, tb), lambda g, i: (g, i, 0, 0))
    if fox:
        proj, c_col, c_row4, o_all, og_all, ogt_all = operands
        ins = [proj, proj, proj, proj, c_row4, c_col, o_all, og_all, ogt_all]
        in_specs = [pl.BlockSpec((tb, pair), lambda g, i: (i, FQ0 // pair + g)),
                    pl.BlockSpec((t, pair), lambda g, i: (0, FK0 // pair + g)),
                    pl.BlockSpec((t, pair), lambda g, i: (0, FV0 // pair + g)),
                    pl.BlockSpec((tb, pair), lambda g, i: (i, GF0 // pair + g)),
                    row_stat, pl.BlockSpec((2, t, 1), lambda g, i: (g, 0, 0)), any_spec, any_spec, any_spec]
        aliases = {6: 0, 7: 1, 8: 2}
    else:
        q_full, kv, kr, proj = operands
        ins = [q_full, kv, kr, proj]
        in_specs = [pl.BlockSpec((tb, 2 * pair), lambda g, i: (i, g)),
                    pl.BlockSpec((t, 2 * pair), lambda g, i: (0, g)),
                    pl.BlockSpec((t, HEAD), lambda g, i: (0, 0)),
                    pl.BlockSpec((tb, pair), lambda g, i: (i, GM0 // pair + g))]
        aliases = {}
    return pl.pallas_call(
        body, name=name, grid=(N_HEADS // 2, nb), in_specs=in_specs,
        out_specs=[pl.BlockSpec((tb, pair), lambda g, i: (i, pair0 + g)),
                   pl.BlockSpec((tb, pair), lambda g, i: (i, pair0 + g)),
                   pl.BlockSpec((pair, tb), lambda g, i: (pair0 + g, i)), row_stat],
        out_shape=[jax.ShapeDtypeStruct((t, 2 * N_HEADS * HEAD), F32), jax.ShapeDtypeStruct((t, 2 * N_HEADS * HEAD), BF16),
                   jax.ShapeDtypeStruct((2 * N_HEADS * HEAD, t), BF16), jax.ShapeDtypeStruct((N_HEADS, nb, 1, tb), F32)],
        scratch_shapes=[pltpu.VMEM((2, 1, tb), F32), pltpu.VMEM((2, 1, tb), F32), pltpu.VMEM((2, HEAD, tb), F32)],
        input_output_aliases=aliases,
        compiler_params=_params(2),
    )(*ins)


def _out_norm_loss(og, w_out_n, x, target, g, *, tm):
    t = og.shape[0]

    def body(og_ref, w_ref, x_ref, t_ref, g_ref, dy_ref, do_ref, dg_ref, loss_ref):
        i = pl.program_id(0)

        @pl.when(i == 0)
        def _():
            dg_ref[...] = jnp.zeros_like(dg_ref)
            loss_ref[...] = jnp.zeros_like(loss_ref)

        ov = jnp.dot(og_ref[...], w_ref[...], preferred_element_type=F32)
        gv = g_ref[...]
        r = lax.rsqrt(jnp.mean(ov * ov, axis=-1, keepdims=True) + NORM_EPS)
        oh = ov * r
        e = x_ref[...] + oh * gv - t_ref[...]
        loss_ref[...] += 0.5 * jnp.sum(jnp.mean(e * e, axis=-1, keepdims=True), axis=0, keepdims=True)
        dy = e * (1.0 / D_MODEL)
        dy_ref[...] = dy
        dyg = dy * gv
        do_ref[...] = (r * (dyg - oh * jnp.mean(dyg * oh, axis=-1, keepdims=True))).astype(BF16)
        dg_ref[...] += jnp.sum(dy * oh, axis=0, keepdims=True)

    row = pl.BlockSpec((tm, D_MODEL), lambda i: (i, 0))
    vec = pl.BlockSpec((1, D_MODEL), lambda i: (0, 0))
    whole_w = pl.BlockSpec((D_MODEL, D_MODEL), lambda i: (0, 0), pipeline_mode=pl.Buffered(1))
    return pl.pallas_call(
        body, name="out_norm_loss", grid=(t // tm,),
        in_specs=[row, whole_w, row, row, vec],
        out_specs=[row, row, vec, pl.BlockSpec((1, 1), lambda i: (0, 0))],
        out_shape=[jax.ShapeDtypeStruct((t, D_MODEL), F32), jax.ShapeDtypeStruct((t, D_MODEL), BF16),
                   jax.ShapeDtypeStruct((1, D_MODEL), F32), jax.ShapeDtypeStruct((1, 1), F32)],
        compiler_params=_params(1),
    )(og, w_out_n, x, target, g)


def _dog_gate(d_o_post, w_out_n, o_all, proj, *, tm):
    t = d_o_post.shape[0]
    n_group = 4
    pair = n_group * HEAD
    gate_blk = GM0 // pair
    assert GM0 % pair == 0 and GF0 == GM0 + N_HEADS * HEAD

    def body(do_ref, w_ref, o_ref, p_ref, dattn_ref, dproj_ref, delta_ref):
        j = pl.program_id(1)

        @pl.when(j == 0)
        def _():
            delta_ref[...] = jnp.zeros_like(delta_ref)

        dog = lax.dot_general(do_ref[...], w_ref[...], (((1,), (1,)), ((), ())), preferred_element_type=F32)
        g = p_ref[...].astype(F32)
        ov = o_ref[...]
        sg = _sigmoid(g)
        d_o = dog * (g * sg)
        dattn_ref[...] = d_o.astype(BF16)
        dproj_ref[...] = (dog * ov * (sg * (1.0 + g * (1.0 - sg)))).astype(BF16)
        prod = d_o * ov
        lane = lax.broadcasted_iota(jnp.int32, (tm, LANE), 1)
        delta = delta_ref[...]
        for u in range(n_group):
            part = jnp.sum(prod[:, HEAD * u:HEAD * (u + 1)], axis=-1, keepdims=True)
            delta = jnp.where(lane == n_group * j + u, part, delta)
        delta_ref[...] = delta

    return pl.pallas_call(
        body, name="dog_gate", grid=(t // tm, 2 * N_HEADS // n_group),
        in_specs=[pl.BlockSpec((tm, D_MODEL), lambda i, j: (i, 0)), pl.BlockSpec((pair, D_MODEL), lambda i, j: (j, 0)),
                  pl.BlockSpec((tm, pair), lambda i, j: (i, j)), pl.BlockSpec((tm, pair), lambda i, j: (i, gate_blk + j))],
        out_specs=[pl.BlockSpec((tm, pair), lambda i, j: (i, j)), pl.BlockSpec((tm, pair), lambda i, j: (i, gate_blk + j)),
                   pl.BlockSpec((tm, LANE), lambda i, j: (i, 0))],
        out_shape=[jax.ShapeDtypeStruct((t, 2048), BF16), jax.ShapeDtypeStruct((t, NP_IN), BF16),
                   jax.ShapeDtypeStruct((t, LANE), F32)],
        compiler_params=_params(2),
    )(d_o_post, w_out_n, o_all, proj)


def _attn_bwd(fox, operands, *, t, tb, name, rider=None):
    nb = t // tb
    n_pairs = N_HEADS // 2
    pair = 2 * HEAD
    scale = FOX_SCALE if fox else MLA_SCALE
    q_w = HEAD if fox else 2 * HEAD
    nt_dims = (((1,), (1,)), ((), ()))
    tn_dims = (((0,), (0,)), ((), ()))
    n_rin = len(rider.operands) if rider else 0
    n_rout = len(rider.out_shape) if rider else 0
    n_in, n_out, n_scr = (9, 3, 9) if fox else (6, 3, 2)

    def body(*refs):
        ends = np.cumsum([0, n_in, n_rin, n_out, n_rout, n_scr])
        in_refs, rider_in, out_refs, rider_out, scr_refs = (refs[a:b] for a, b in zip(ends[:-1], ends[1:]))
        rider_refs = (rider_in, rider_out, refs[ends[-1]:])
        if fox:
            q_ref, k_ref, v_ref, do_ref, lse_ref, dl_ref, cq_ref, ck_ref, _ = in_refs
            dproj_ref, dck_ref, dcq_ref = out_refs
            dq_acc, dk_s, dv_s, dc_s, dcq_s, stage_q, stage_k, stage_v, put_sems = scr_refs
        else:
            q_ref, kv_ref, kr_ref, do_ref, lse_ref, dl_ref = in_refs
            dq_acc, dkv_ref, dkr_ref = out_refs
            dk_s, dv_s = scr_refs
        g = pl.program_id(0)
        ki = pl.program_id(1)
        if rider:
            @pl.when(jnp.logical_and(g == 0, ki == 0))
            def _():
                rider.start(*rider_refs)

        @pl.when(ki == 0)
        def _():
            dq_acc[...] = jnp.zeros_like(dq_acc)
            if fox:
                dcq_s[...] = jnp.zeros_like(dcq_s)

        dk_s[...] = jnp.zeros_like(dk_s)
        dv_s[...] = jnp.zeros_like(dv_s)
        if fox:
            dc_s[...] = jnp.zeros_like(dc_s)
            keys = [k_ref[:, HEAD * u:HEAD * (u + 1)] for u in range(2)]
            vals = [v_ref[:, HEAD * u:HEAD * (u + 1)] for u in range(2)]
        else:
            keys = [jnp.concatenate([kv_ref[:, pair * u:pair * u + HEAD], kr_ref[...]], axis=1) for u in range(2)]
            vals = [kv_ref[:, pair * u + HEAD:pair * (u + 1)] for u in range(2)]

        def chunk(qc, masked):
            off = pl.multiple_of(qc * tb, tb)
            for u in range(2):
                kk, vv = keys[u], vals[u]
                qq = q_ref[pl.ds(off, tb), q_w * u:q_w * (u + 1)]
                dd = do_ref[pl.ds(off, tb), HEAD * u:HEAD * (u + 1)]
                s = lax.dot_general(kk, qq, nt_dims, preferred_element_type=F32)
                if fox:
                    s = s + cq_ref[u, qc] - ck_ref[u]
                if masked:
                    row = lax.broadcasted_iota(jnp.int32, (tb, tb), 0)
                    col = lax.broadcasted_iota(jnp.int32, (tb, tb), 1)
                    s = jnp.where(row <= col, s, -jnp.inf)
                p = jnp.exp2(s * (scale * LOG2E) - lse_ref[u, qc] * LOG2E)
                dv_s[u] += jnp.dot(p.astype(BF16), dd, preferred_element_type=F32)
                dp = lax.dot_general(vv, dd, nt_dims, preferred_element_type=F32)
                ds = p * (dp - dl_ref[u, qc])
                if fox:
                    dc_s[u] += jnp.sum(ds, axis=1, keepdims=True)
                    dcq_s[u, qc] += jnp.sum(ds, axis=0, keepdims=True)
                dsb = (ds * scale).astype(BF16)
                dk_s[u] += jnp.dot(dsb, qq, preferred_element_type=F32)
                dq_acc[pl.ds(off, tb), q_w * u:q_w * (u + 1)] += lax.dot_general(dsb, kk, tn_dims,
                                                                                 preferred_element_type=F32)

        chunk(ki, True)

        def loop_body(qc, carry):
            chunk(qc, False)
            return carry

        lax.fori_loop(ki + 1, nb, loop_body, 0)

        def put(stage_ref, rows, seg0, sem):
            col0 = pl.multiple_of(seg0 + g * pair, pair)
            return pltpu.make_async_copy(stage_ref, dproj_ref.at[rows, pl.ds(col0, pair)], sem)

        if fox:
            rows = pl.ds(pl.multiple_of(ki * tb, tb), tb)
            block_puts = [put(stage_k, rows, FK0, put_sems.at[1]), put(stage_v, rows, FV0, put_sems.at[2])]
            pair_put = put(stage_q, pl.ds(0, t), FQ0, put_sems.at[0])

            @pl.when(jnp.logical_or(g > 0, ki > 0))
            def _():
                for cp in block_puts:
                    cp.wait()

            for u in range(2):
                stage_k[:, HEAD * u:HEAD * (u + 1)] = dk_s[u].astype(BF16)
                stage_v[:, HEAD * u:HEAD * (u + 1)] = dv_s[u].astype(BF16)
                dck_ref[u] = -dc_s[u]
            for cp in block_puts:
                cp.start()

            @pl.when(ki == nb - 1)
            def _():
                @pl.when(g > 0)
                def _():
                    pair_put.wait()

                stage_q[...] = dq_acc[...].astype(BF16)
                pair_put.start()
                dcq_ref[...] = dcq_s[...]

            @pl.when(jnp.logical_and(g == n_pairs - 1, ki == nb - 1))
            def _():
                for cp in block_puts + [pair_put]:
                    cp.wait()
        else:
            dkv_ref[...] = jnp.concatenate([dk_s[0, :, :HEAD], dv_s[0], dk_s[1, :, :HEAD], dv_s[1]], axis=1).astype(BF16)
            dkr_ref[...] = jnp.concatenate([dk_s[0, :, HEAD:], dk_s[1, :, HEAD:]], axis=1)

        if rider:
            @pl.when(jnp.logical_and(g == n_pairs - 1, ki == nb - 1))
            def _():
                rider.wait(*rider_refs)

    stat = pl.BlockSpec((2, nb, 1, tb), lambda g, i: (g, 0, 0, 0))
    aliases = {}
    if fox:
        proj, d_o, lse4, delta4, c_row4, c_col, dproj = operands
        ins = [proj, proj, proj, d_o, lse4, delta4, c_row4, c_col, dproj]
        any_spec = pl.BlockSpec(memory_space=pl.ANY)
        in_specs = [pl.BlockSpec((t, pair), lambda g, i: (0, FQ0 // pair + g)),
                    pl.BlockSpec((tb, pair), lambda g, i: (i, FK0 // pair + g)),
                    pl.BlockSpec((tb, pair), lambda g, i: (i, FV0 // pair + g)),
                    pl.BlockSpec((t, pair), lambda g, i: (0, n_pairs + g)),
                    stat, stat, stat, pl.BlockSpec((2, tb, 1), lambda g, i: (g, i, 0)), any_spec]
        aliases = {8: 0}
        out_specs = [any_spec, pl.BlockSpec((2, tb, 1), lambda g, i: (g, i, 0)), stat]
        out_shape = [jax.ShapeDtypeStruct(dproj.shape, dproj.dtype), jax.ShapeDtypeStruct((N_HEADS, t, 1), F32),
                     jax.ShapeDtypeStruct((N_HEADS, nb, 1, tb), F32)]
        scratch = [pltpu.VMEM((t, pair), F32), pltpu.VMEM((2, tb, HEAD), F32), pltpu.VMEM((2, tb, HEAD), F32),
                   pltpu.VMEM((2, tb, 1), F32), pltpu.VMEM((2, nb, 1, tb), F32),
                   pltpu.VMEM((t, pair), BF16), pltpu.VMEM((tb, pair), BF16), pltpu.VMEM((tb, pair), BF16),
                   pltpu.SemaphoreType.DMA((3,))]
    else:
        q_full, kv, kr, d_o, lse4, delta4 = operands
        ins = [q_full, kv, kr, d_o, lse4, delta4]
        in_specs = [pl.BlockSpec((t, 2 * pair), lambda g, i: (0, g)),
                    pl.BlockSpec((tb, 2 * pair), lambda g, i: (i, g)),
                    pl.BlockSpec((tb, HEAD), lambda g, i: (i, 0)),
                    pl.BlockSpec((t, pair), lambda g, i: (0, g)),
                    stat, stat]
        out_specs = [pl.BlockSpec((t, 2 * pair), lambda g, i: (0, g)), pl.BlockSpec((tb, 2 * pair), lambda g, i: (i, g)),
                     pl.BlockSpec((tb, pair), lambda g, i: (i, g))]
        out_shape = [jax.ShapeDtypeStruct((t, 2048), F32), jax.ShapeDtypeStruct((t, 2048), BF16),
                     jax.ShapeDtypeStruct((t, 1024), F32)]
        scratch = [pltpu.VMEM((2, tb, 2 * HEAD), F32), pltpu.VMEM((2, tb, HEAD), F32)]
    assert (len(ins), len(out_specs), len(scratch)) == (n_in, n_out, n_scr)
    if rider:
        any_spec = pl.BlockSpec(memory_space=pl.ANY)
        aliases = {**aliases, **{n_in + i_in: n_out + i_out for i_in, i_out in rider.aliases.items()}}
        ins = ins + list(rider.operands)
        in_specs = in_specs + [any_spec] * n_rin
        out_specs = out_specs + [any_spec] * n_rout
        out_shape = out_shape + list(rider.out_shape)
        scratch = scratch + list(rider.scratch)
    return pl.pallas_call(
        body, name=name, grid=(n_pairs, nb), in_specs=in_specs, out_specs=out_specs, out_shape=out_shape,
        scratch_shapes=scratch, input_output_aliases=aliases, compiler_params=_params(2),
    )(*ins)


def _mid_bwd(dq_full, dkr, cos_t, sin_t, dck, flog, bf_row, *, tm):
    t = dq_full.shape[0]
    n = t // tm

    def body(dq_ref, dkr_ref, cos_ref, sin_ref, dck_ref, fl_ref, bf_ref,
             dq2_ref, dkraw_ref, dfl_ref, dbf_ref, carry_ref):
        i = pl.program_id(0)

        @pl.when(i == 0)
        def _():
            carry_ref[...] = jnp.zeros_like(carry_ref)
            dbf_ref[...] = jnp.zeros_like(dbf_ref)

        c, s = cos_ref[...], sin_ref[...]
        dkr_sum = jnp.zeros((tm, LANE), F32)
        for h in range(N_HEADS):
            dq2_ref[:, 256 * h:256 * h + 128] = dq_ref[:, 256 * h:256 * h + 128].astype(BF16)
            dq2_ref[:, 256 * h + 128:256 * h + 256] = _rot_bwd(dq_ref[:, 256 * h + 128:256 * h + 256], c, s).astype(BF16)
            dkr_sum = dkr_sum + dkr_ref[:, HEAD * h:HEAD * (h + 1)]
        dkraw_ref[...] = _rot_bwd(dkr_sum, c, s).astype(BF16)

        dc = dck_ref[...]
        row = lax.broadcasted_iota(jnp.int32, (tm, tm), 0)
        col = lax.broadcasted_iota(jnp.int32, (tm, tm), 1)
        tri = (col >= row).astype(BF16)
        acc = carry_ref[0:1, :]
        for part in _split3(dc):
            acc = acc + jnp.dot(tri, part, preferred_element_type=F32)
        carry_ref[0:1, :] = carry_ref[0:1, :] + jnp.sum(dc, axis=0, keepdims=True)
        z = fl_ref[...] + bf_ref[...]
        dz = acc / (1.0 + jnp.exp(z))
        dfl_ref[...] = dz.astype(BF16)
        dbf_ref[...] += jnp.sum(dz, axis=0, keepdims=True)

    rev = lambda w: pl.BlockSpec((tm, w), lambda i: (n - 1 - i, 0))
    vec = lambda w: pl.BlockSpec((1, w), lambda i: (0, 0))
    return pl.pallas_call(
        body, name="mid_bwd", grid=(n,),
        in_specs=[rev(2048), rev(1024), rev(LANE), rev(LANE), rev(LANE), rev(LANE), vec(LANE)],
        out_specs=[rev(2048), rev(LANE), rev(LANE), vec(LANE)],
        out_shape=[jax.ShapeDtypeStruct((t, 2048), BF16), jax.ShapeDtypeStruct((t, LANE), BF16),
                   jax.ShapeDtypeStruct((t, LANE), BF16), jax.ShapeDtypeStruct((1, LANE), F32)],
        scratch_shapes=[pltpu.VMEM((8, LANE), F32)],
        compiler_params=_params(1),
    )(dq_full, dkr, cos_t, sin_t, dck, flog, bf_row)


def _norm_bwd(proj, dqn, dkvn, g_q, g_kv, dkr_raw, dfl, dproj, *, tm):
    t = proj.shape[0]
    assert (KR0, FL0, KVL0, LAT_W) == (Q_RANK, Q_RANK + LANE, Q_RANK + 2 * LANE, Q_RANK + 2 * LANE + KV_RANK)

    def body(p_ref, dqn_ref, dkvn_ref, gq_ref, gkv_ref, dkr_ref, dfl_ref, _, dproj_ref, dgq_ref, dgkv_ref):
        i = pl.program_id(0)

        @pl.when(i == 0)
        def _():
            dgq_ref[...] = jnp.zeros_like(dgq_ref)
            dgkv_ref[...] = jnp.zeros_like(dgkv_ref)

        d_lat = []
        for lo, w, dn_ref, g_ref, dg_ref in ((QL0, Q_RANK, dqn_ref, gq_ref, dgq_ref),
                                             (KVL0, KV_RANK, dkvn_ref, gkv_ref, dgkv_ref)):
            xv = p_ref[:, lo:lo + w].astype(F32)
            r = lax.rsqrt(jnp.mean(xv * xv, axis=-1, keepdims=True) + NORM_EPS)
            xh = xv * r
            dn = dn_ref[...]
            dg_ref[...] += jnp.sum(dn * xh, axis=0, keepdims=True)
            dxh = dn * g_ref[...]
            d_lat.append((r * (dxh - xh * jnp.mean(dxh * xh, axis=-1, keepdims=True))).astype(BF16))
        dproj_ref[...] = jnp.concatenate([d_lat[0], dkr_ref[...], dfl_ref[...], d_lat[1]], axis=1)

    row = lambda w: pl.BlockSpec((tm, w), lambda i: (i, 0))
    vec = lambda w: pl.BlockSpec((1, w), lambda i: (0, 0))
    return pl.pallas_call(
        body, name="norm_bwd", grid=(t // tm,),
        in_specs=[row(LAT_W), row(Q_RANK), row(KV_RANK), vec(Q_RANK), vec(KV_RANK), row(LANE), row(LANE),
                  pl.BlockSpec(memory_space=pl.ANY)],
        out_specs=[row(LAT_W), vec(Q_RANK), vec(KV_RANK)],
        out_shape=[jax.ShapeDtypeStruct(dproj.shape, dproj.dtype),
                   jax.ShapeDtypeStruct((1, Q_RANK), F32), jax.ShapeDtypeStruct((1, KV_RANK), F32)],
        input_output_aliases={7: 0},
        compiler_params=_params(1),
    )(proj, dqn, dkvn, g_q, g_kv, dkr_raw, dfl, dproj)


def _prenorm_bwd(dh, x, g, dy, *, tm):
    t = x.shape[0]

    def body(dh_ref, x_ref, g_ref, dy_ref, gx_ref, dg_ref):
        i = pl.program_id(0)

        @pl.when(i == 0)
        def _():
            dg_ref[...] = jnp.zeros_like(dg_ref)

        xv = x_ref[...]
        r = lax.rsqrt(jnp.mean(xv * xv, axis=-1, keepdims=True) + NORM_EPS)
        xh = xv * r
        dn = dh_ref[...]
        dg_ref[...] += jnp.sum(dn * xh, axis=0, keepdims=True)
        dxh = dn * g_ref[...]
        gx_ref[...] = dy_ref[...] + r * (dxh - xh * jnp.mean(dxh * xh, axis=-1, keepdims=True))

    row = pl.BlockSpec((tm, D_MODEL), lambda i: (i, 0))
    vec = pl.BlockSpec((1, D_MODEL), lambda i: (0, 0))
    return pl.pallas_call(
        body, name="prenorm_bwd", grid=(t // tm,),
        in_specs=[row, row, vec, row], out_specs=[row, vec],
        out_shape=[jax.ShapeDtypeStruct((t, D_MODEL), F32), jax.ShapeDtypeStruct((1, D_MODEL), F32)],
        compiler_params=_params(1),
    )(dh, x, g, dy)


def _adam_math(w, g, m, v):
    m = ADAM_B1 * m + (1.0 - ADAM_B1) * g
    v = ADAM_B2 * v + (1.0 - ADAM_B2) * (g * g)
    m_hat = m / (1.0 - ADAM_B1 ** ADAM_STEP)
    v_hat = v / (1.0 - ADAM_B2 ** ADAM_STEP)
    delta = -ADAM_LR * (m_hat / (jnp.sqrt(v_hat) + ADAM_EPS) + ADAM_WD * w)
    return delta, m, v


def _adamw(land, w, m, v, *, tr, name):
    rows, cols = w.shape

    def body(l_ref, w_ref, m_ref, v_ref, g_ref, d_ref, nm_ref, nv_ref):
        g = l_ref[0].astype(F32)
        for s in range(1, N_CHIPS):
            g = g + l_ref[s].astype(F32)
        g_ref[...] = g
        d_ref[...], nm_ref[...], nv_ref[...] = _adam_math(w_ref[...], g, m_ref[...], v_ref[...])

    blk = pl.BlockSpec((tr, cols), lambda i: (i, 0))
    return pl.pallas_call(
        body, name=name, grid=(rows // tr,),
        in_specs=[pl.BlockSpec((N_CHIPS, tr, cols), lambda i: (0, i, 0)), blk, blk, blk],
        out_specs=[blk, blk, blk, blk],
        out_shape=[jax.ShapeDtypeStruct((rows, cols), F32)] * 4,
        compiler_params=_params(1),
    )(land, w, m, v)


def _adamw_small(gathered, w, m, v):
    def body(a_ref, w_ref, m_ref, v_ref, g_ref, d_ref, nm_ref, nv_ref):
        g = a_ref[0:SMALL_ROWS, :]
        for s in range(1, N_DEV):
            g = g + a_ref[SMALL_ROWS * s:SMALL_ROWS * (s + 1), :]
        g_ref[...] = g
        d_ref[...], nm_ref[...], nv_ref[...] = _adam_math(w_ref[...], g, m_ref[...], v_ref[...])

    return pl.pallas_call(
        body, name="adamw_small",
        out_shape=[jax.ShapeDtypeStruct((SMALL_ROWS, SMALL_COLS), F32)] * 4,
        compiler_params=_params(),
    )(gathered, w, m, v)


def _place():
    x, y, c = lax.axis_index("x"), lax.axis_index("y"), lax.axis_index("c")
    return x, y, c


def _flip(p, k):
    x, y, c = p
    return (1 - x if k & 4 else x, 1 - y if k & 2 else y, 1 - c if k & 1 else c)


def _index(p):
    return 4 * p[0] + 2 * p[1] + p[2]


class _AllGather:
    def __init__(self, shard):
        assert shard.shape[0] % 32 == 0
        self.half = shard.shape[0] // 2
        self.operands = [shard]
        self.out_shape = [jax.ShapeDtypeStruct((N_DEV,) + shard.shape, shard.dtype)]
        self.aliases = {}
        self.scratch = [pltpu.SemaphoreType.DMA((9,)), pltpu.SemaphoreType.DMA((9,)), pltpu.SemaphoreType.DMA(())]

    def _parts(self, ins, outs, scratch):
        (in_ref,), (out_ref,), (send_sems, recv_sems, local_sem) = ins, outs, scratch
        me = _place()

        def copy(k, block, to, part=None, src=None):
            dst = out_ref.at[_index(block)] if part is None else out_ref.at[_index(block), part]
            return pltpu.make_async_remote_copy(
                src_ref=dst if src is None else src, dst_ref=dst, send_sem=send_sems.at[k], recv_sem=recv_sems.at[k],
                device_id=to, device_id_type=MESH)

        mine = pltpu.make_async_copy(in_ref, out_ref.at[_index(me)], local_sem)
        first = [copy(0, me, _flip(me, 1), src=in_ref), copy(1, me, _flip(me, 4), src=in_ref),
                 copy(2, me, _flip(me, 2), src=in_ref)]
        return me, copy, mine, first

    def start(self, ins, outs, scratch):
        _, _, mine, first = self._parts(ins, outs, scratch)
        mine.start()
        for cp in first:
            cp.start()

    def wait(self, ins, outs, scratch):
        me, copy, mine, sent = self._parts(ins, outs, scratch)
        sibling, x_nbr, y_nbr, diagonal = _flip(me, 1), _flip(me, 4), _flip(me, 2), _flip(me, 6)
        top, bottom = pl.ds(0, self.half), pl.ds(self.half, self.half)
        arrivals = [(1, x_nbr, None, [(3, sibling, None), (5, y_nbr, top)]),
                    (2, y_nbr, None, [(4, sibling, None), (6, x_nbr, bottom)]),
                    (5, diagonal, top, [(7, sibling, top)]),
                    (6, diagonal, bottom, [(8, sibling, bottom)])]
        for k, block, part, onward in arrivals:
            copy(k, block, me, part).wait_recv()
            for k_on, to, part_on in onward:
                cp = copy(k_on, block, to, part_on)
                cp.start()
                sent.append(cp)
        other = lambda p: _flip(p, 1)
        for k, block, part in ((0, sibling, None), (3, other(x_nbr), None), (4, other(y_nbr), None),
                               (7, other(diagonal), top), (8, other(diagonal), bottom)):
            copy(k, block, me, part).wait_recv()
        for cp in sent:
            cp.wait_send()
        mine.wait()


class _Exchange:
    def __init__(self, tasks):
        self.tasks = tasks
        taken = [land for _, _, land, _, _ in tasks if land is not None]
        self.operands = [src for src, _, _, _, _ in tasks] + taken
        self.out_shape = [
            jax.ShapeDtypeStruct((N_CHIPS,) + ((2,) if by_core else ()) + (src.shape if same else src.shape[1:]), src.dtype)
            for src, _, _, same, by_core in tasks]
        self.aliases, n_taken = {}, 0
        for a, (_, _, land, _, _) in enumerate(tasks):
            if land is not None:
                self.aliases[len(tasks) + n_taken] = a
                n_taken += 1
        self.scratch = [pltpu.SemaphoreType.DMA((N_CHIPS,)), pltpu.SemaphoreType.DMA((N_CHIPS,)),
                        pltpu.SemaphoreType.DMA(())] * len(tasks)

    def _copies(self, ins, outs, scratch):
        x, y, core = _place()
        my = 2 * x + y
        for a, (_, chips, _, same, by_core) in enumerate(self.tasks):
            send_sems, recv_sems, local_sem = scratch[3 * a:3 * a + 3]
            slot = (lambda s, a=a, by_core=by_core: outs[a].at[s, core] if by_core else outs[a].at[s])
            for i, j in enumerate(chips):
                src = ins[a] if same else ins[a].at[i]
                pair = jnp.bitwise_xor(my, j)
                remote = pltpu.make_async_remote_copy(
                    src_ref=src, dst_ref=slot(my), send_sem=send_sems.at[pair], recv_sem=recv_sems.at[pair],
                    device_id=(j >> 1, j & 1, core), device_id_type=MESH)
                local = pltpu.make_async_copy(src, slot(my), local_sem)
                yield j, my, core, remote, local, slot, (send_sems, recv_sems)

    def start(self, ins, outs, scratch):
        for j, my, _, remote, local, _, _ in self._copies(ins, outs, scratch):
            pl.when(my != j)(remote.start)
            pl.when(my == j)(local.start)

    def wait(self, ins, outs, scratch):
        for j, my, core, remote, local, slot, (send_sems, recv_sems) in self._copies(ins, outs, scratch):
            pl.when(my != j)(remote.wait_send)

            @pl.when(my == j)
            def _():
                local.wait()
                for s in range(N_CHIPS):
                    if s != j:
                        pltpu.make_async_remote_copy(
                            src_ref=slot(s), dst_ref=slot(s), send_sem=send_sems.at[j ^ s], recv_sem=recv_sems.at[j ^ s],
                            device_id=(s >> 1, s & 1, core), device_id_type=MESH).wait_recv()


N_CHIPS = 4
ALL_CHIPS = tuple(range(N_CHIPS))


def _to_other_core(parts, *, name):
    n_arr = len(parts)
    hbm = pl.BlockSpec(memory_space=pl.ANY)

    def body(*refs):
        srcs, lands = refs[:n_arr], refs[n_arr:2 * n_arr]
        send_sems, recv_sems = refs[2 * n_arr:]
        me = _place()
        copies = [pltpu.make_async_remote_copy(src_ref=srcs[a].at[1 - me[2]], dst_ref=lands[a], send_sem=send_sems.at[a],
                                               recv_sem=recv_sems.at[a], device_id=_flip(me, 1), device_id_type=MESH)
                  for a in range(n_arr)]
        for cp in copies:
            cp.start()
        for cp in copies:
            cp.wait()

    return pl.pallas_call(
        body, name=name, in_specs=[hbm] * n_arr, out_specs=[hbm] * n_arr,
        out_shape=[jax.ShapeDtypeStruct(p.shape[1:], p.dtype) for p in parts],
        scratch_shapes=[pltpu.SemaphoreType.DMA((n_arr,)), pltpu.SemaphoreType.DMA((n_arr,))],
    )(*parts)


def _share_with_other_core(gathered, *, name):
    n_arr = len(gathered)
    hbm = pl.BlockSpec(memory_space=pl.ANY)

    def body(*refs):
        bufs = refs[n_arr:2 * n_arr]
        send_sems, recv_sems = refs[2 * n_arr:]
        me = _place()
        copies = []
        for a in range(n_arr):
            for j in range(N_CHIPS):
                block = bufs[a].at[j, me[2]]
                copies.append(pltpu.make_async_remote_copy(
                    src_ref=block, dst_ref=block, send_sem=send_sems.at[N_CHIPS * a + j],
                    recv_sem=recv_sems.at[N_CHIPS * a + j], device_id=_flip(me, 1), device_id_type=MESH))
        for cp in copies:
            cp.start()
        for cp in copies:
            cp.wait()

    return pl.pallas_call(
        body, name=name, in_specs=[hbm] * n_arr, out_specs=[hbm] * n_arr,
        out_shape=[jax.ShapeDtypeStruct(g.shape, g.dtype) for g in gathered],
        scratch_shapes=[pltpu.SemaphoreType.DMA((N_CHIPS * n_arr,)), pltpu.SemaphoreType.DMA((N_CHIPS * n_arr,))],
        input_output_aliases={a: a for a in range(n_arr)},
    )(*gathered)


def _pair_sum(mine, other, core, *, tr, name):
    _, n, rows, cols = mine.shape
    tr = min(tr, rows)

    def body(core_ref, a_ref, b_ref, o_ref):
        o_ref[...] = (a_ref[0].astype(F32) + b_ref[...].astype(F32)).astype(BF16)

    return pl.pallas_call(
        body, name=name,
        grid_spec=pltpu.PrefetchScalarGridSpec(
            num_scalar_prefetch=1, grid=(n, rows // tr),
            in_specs=[pl.BlockSpec((1, 1, tr, cols), lambda j, i, core_ref: (core_ref[0], j, i, 0)),
                      pl.BlockSpec((1, tr, cols), lambda j, i, core_ref: (j, i, 0))],
            out_specs=pl.BlockSpec((1, tr, cols), lambda j, i, core_ref: (j, i, 0))),
        out_shape=jax.ShapeDtypeStruct(other.shape, BF16),
        compiler_params=_params(2),
    )(core, mine, other)


def _gather_small(vec):
    def body(v_ref, out_ref, send_sems, recv_sems, local_sem):
        me = _place()

        def rows(p):
            return out_ref.at[pl.ds(pl.multiple_of(_index(p) * SMALL_ROWS, SMALL_ROWS), SMALL_ROWS), :]

        mine = pltpu.make_async_copy(v_ref, rows(me), local_sem)
        mine.start()
        sends = []
        for k in range(1, N_DEV):
            peer = _flip(me, k)
            cp = pltpu.make_async_remote_copy(src_ref=v_ref, dst_ref=rows(me), send_sem=send_sems.at[k - 1],
                                              recv_sem=recv_sems.at[k - 1], device_id=peer, device_id_type=MESH)
            cp.start()
            sends.append(cp)
        for k in range(1, N_DEV):
            peer = _flip(me, k)
            pltpu.make_async_remote_copy(src_ref=rows(peer), dst_ref=rows(peer), send_sem=send_sems.at[k - 1],
                                         recv_sem=recv_sems.at[k - 1], device_id=peer, device_id_type=MESH).wait_recv()
        for cp in sends:
            cp.wait_send()
        mine.wait()

    return pl.pallas_call(
        body, name="gather_small",
        in_specs=[pl.BlockSpec(memory_space=pltpu.VMEM)], out_specs=pl.BlockSpec(memory_space=pltpu.VMEM),
        out_shape=jax.ShapeDtypeStruct((N_DEV * SMALL_ROWS, SMALL_COLS), F32),
        scratch_shapes=[pltpu.SemaphoreType.DMA((7,)), pltpu.SemaphoreType.DMA((7,)), pltpu.SemaphoreType.DMA],
    )(vec)


def _w_in_nice(gathered):
    pieces, pos = [], 0
    for o0, width, n0 in sorted(_SEGMENTS, key=lambda seg: seg[2]):
        if n0 > pos:
            pieces.append(jnp.zeros((D_MODEL, n0 - pos), gathered.dtype))
        o = o0
        while o < o0 + width:
            d = o // SHARD_IN
            hi = min(o0 + width, (d + 1) * SHARD_IN)
            pieces.append(gathered[d][:, o - d * SHARD_IN:hi - d * SHARD_IN])
            o = hi
        pos = n0 + width
    pieces.append(jnp.zeros((D_MODEL, NP_IN - pos), gathered.dtype))
    return jnp.concatenate(pieces, axis=1)


def _w_in_blocks(chips, dw_lat, dw_rest):
    blocks = []
    for core in range(2):
        for chip in chips:
            lo = (2 * chip + core) * SHARD_IN
            runs = []
            for o0, width, n0 in _SEGMENTS:
                a, b = max(lo, o0), min(lo + SHARD_IN, o0 + width)
                if a < b:
                    n_a, n_b = n0 + a - o0, n0 + b - o0
                    runs.append(dw_lat[:, n_a:n_b] if n_b <= LAT_W else dw_rest[:, n_a - LAT_W:n_b - LAT_W])
            blocks.append(jnp.concatenate(runs, axis=1))
    return jnp.stack(blocks).reshape(2, len(chips), D_MODEL, SHARD_IN)


def _by_core(shards):
    return shards.reshape((N_CHIPS, 2) + shards.shape[1:]).swapaxes(0, 1)


EARLY_CHIPS = (1, 2)
LATE_CHIPS = (0, 3)


def _w_uq_nice(shard):
    z = jnp.zeros((Q_RANK, 32), shard.dtype)
    return jnp.concatenate([shard[:, :128], shard[:, 128:160], z, shard[:, 160:192], z], axis=1)


def _pack_small(g_pre, g_post, g_q, g_kv, b_f, extra=None):
    parts = [g_pre.reshape(-1), g_post.reshape(-1), g_q.reshape(-1), g_kv.reshape(-1), b_f.reshape(-1)]
    if extra is not None:
        parts.append(extra.reshape(-1))
    flat = jnp.concatenate(parts)
    flat = jnp.concatenate([flat, jnp.zeros((SMALL_ROWS * SMALL_COLS - flat.shape[0],), F32)])
    return flat.reshape(SMALL_ROWS, SMALL_COLS)


def _unpack_small(packed):
    flat = packed.reshape(-1)
    o = 0
    out = []
    for n in (D_MODEL, D_MODEL, Q_RANK, KV_RANK, N_HEADS):
        out.append(flat[o:o + n].reshape(1, n))
        o += n
    return out, flat[o]


def kernel(x, positions, g_pre, w_in, g_q_latent, w_uq, g_kv_latent, w_ukv, b_forget, w_out, g_post, loss_target, m_g_pre, m_w_in, m_g_q_latent, m_w_uq, m_g_kv_latent, m_w_ukv, m_b_forget, m_w_out, m_g_post, v_g_pre, v_w_in, v_g_q_latent, v_w_uq, v_g_kv_latent, v_w_ukv, v_b_forget, v_w_out, v_g_post):
    t = x.shape[1]
    tb = min(512, t)
    tm = min(256, t)
    nb = t // tb
    x2 = x.reshape(t, D_MODEL)
    target = loss_target.reshape(t, D_MODEL)
    pos_col = positions.reshape(t, 1).astype(F32)
    bf_row = jnp.concatenate([b_forget.reshape(1, N_HEADS), jnp.zeros((1, LANE - N_HEADS), F32)], axis=1)

    h, h_t, g_in = _prenorm(x2, g_pre, _AllGather(w_in[0].astype(BF16)), tm=tm)
    w_in_n = _w_in_nice(g_in)
    gather_rest = _Exchange([(w, ALL_CHIPS, None, True, True) for w in
                             (_w_uq_nice(w_uq[0].astype(BF16)), w_ukv[0].astype(BF16), w_out[0].astype(BF16))])
    core = lax.axis_index("c").astype(jnp.int32).reshape(1)
    proj, g_uq, g_ukv, g_out = _mm(h, w_in_n, name="proj_in", out_dtype=BF16, tm=2048, tn=512, tk=2048,
                                   rider=gather_rest)
    g_uq, g_ukv, g_out = _share_with_other_core([g_uq, g_ukv, g_out], name="share_weights")
    w_uq_n = g_uq.reshape(N_DEV, Q_RANK, 256).transpose(1, 0, 2).reshape(Q_RANK, N_HEADS * 256)
    w_ukv_n = g_ukv.reshape(N_DEV, KV_RANK, 256).transpose(1, 0, 2).reshape(KV_RANK, N_HEADS * 256)
    w_out_n = g_out.reshape(D_MODEL, D_MODEL)
    flog = _mm(h, w_in_n[:, FL0:FL0 + LANE], name="proj_flog", out_dtype=F32, tm=1024, tn=LANE, tk=2048)
    qn, kvn, kr, cos_t, sin_t, c, qn_t, kvn_t = _mid_fwd(proj, flog, g_q_latent, g_kv_latent, bf_row, pos_col, tm=tm)
    q_full = _q_up_rope(qn, w_uq_n, cos_t, sin_t, tm=min(1024, t))
    kv = _mm(kvn, w_ukv_n, name="kv_up", out_dtype=BF16, tm=1024, tn=512, tk=KV_RANK)
    c_heads = c[:, :N_HEADS].T
    c_col = c_heads.reshape(N_HEADS, t, 1)
    c_row4 = c_heads.reshape(N_HEADS, nb, 1, tb)
    o_all, og_all, og_t, lse4_mla = _attn_fwd(False, (q_full, kv, kr, proj), t=t, tb=tb, name="mla_fwd")
    o_all, og_all, og_t, lse4_fox = _attn_fwd(True, (proj, c_col, c_row4, o_all, og_all, og_t), t=t, tb=tb,
                                              name="fox_fwd")
    dy, d_o_post, dg_post, loss_part = _out_norm_loss(og_all, w_out_n, x2, target, g_post, tm=min(512, t))

    dw_out = _mm(og_t, d_o_post, name="dw_out", out_dtype=BF16, tm=1024, tn=1024, tk=1024)
    p_out = _by_core(dw_out.reshape(N_DEV, D_MODEL // N_DEV, D_MODEL))
    (o_out,) = _to_other_core([p_out], name="dw_out_to_core")
    s_out = _pair_sum(p_out, o_out, core, tr=256, name="dw_out_pair_sum")
    d_attn, dproj, delta = _dog_gate(d_o_post, w_out_n, o_all, proj, tm=min(1024, t))
    delta4 = delta[:, :2 * N_HEADS].T.reshape(2 * N_HEADS, nb, 1, tb)
    dproj, dck, dcq, l_out = _attn_bwd(True, (proj, d_attn, lse4_fox, delta4[N_HEADS:], c_row4, c_col, dproj),
                                       t=t, tb=tb, name="fox_bwd",
                                       rider=_Exchange([(s_out, ALL_CHIPS, None, False, False)]))
    dw_in_rest = _mm(h_t, dproj, name="dw_in_rest", out_dtype=BF16, tm=2048, tn=512, tk=1024,
                     b_cols=(LAT_W, NP_IN - LAT_W))
    p_in = _w_in_blocks(EARLY_CHIPS, None, dw_in_rest)
    (o_in,) = _to_other_core([p_in], name="dw_in_early_to_core")
    s_in = _pair_sum(p_in, o_in, core, tr=256, name="dw_in_early_pair_sum")
    dq_full, dkv, dkr, l_in = _attn_bwd(False, (q_full, kv, kr, d_attn, lse4_mla, delta4[:N_HEADS]),
                                        t=t, tb=tb, name="mla_bwd",
                                        rider=_Exchange([(s_in, EARLY_CHIPS, None, False, False)]))
    dc_heads = dck.reshape(N_HEADS, t) + dcq.reshape(N_HEADS, t)
    dck_rows = jnp.concatenate([dc_heads.T, jnp.zeros((t, LANE - N_HEADS), F32)], axis=1)
    dq2, dkr_raw, dfl, dbf = _mid_bwd(dq_full, dkr, cos_t, sin_t, dck_rows, flog, bf_row, tm=tm)
    dqn = _mm(dq2, w_uq_n, name="d_qn", nt=True, out_dtype=F32, tm=1024, tn=Q_RANK, tk=2048)
    dkvn = _mm(dkv, w_ukv_n, name="d_kvn", nt=True, out_dtype=F32, tm=1024, tn=KV_RANK, tk=2048)
    dw_uq = _mm(qn_t, dq2, name="dw_uq", out_dtype=BF16, tm=Q_RANK, tn=1024, tk=1024)
    dw_ukv = _mm(kvn_t, dkv, name="dw_ukv", out_dtype=BF16, tm=KV_RANK, tn=1024, tk=1024)
    dproj, dg_q, dg_kv = _norm_bwd(proj, dqn, dkvn, g_q_latent, g_kv_latent, dkr_raw, dfl, dproj, tm=tm)
    dw_in_lat = _mm(h_t, dproj, name="dw_in_lat", out_dtype=BF16, tm=1024, tn=LAT_W, tk=1024, b_cols=(0, LAT_W))
    dw_uq_h = dw_uq.reshape(Q_RANK, N_HEADS, 256)
    s_uq = jnp.concatenate([dw_uq_h[:, :, :160], dw_uq_h[:, :, 192:224]], axis=2).transpose(1, 0, 2)
    s_ukv = dw_ukv.reshape(KV_RANK, N_HEADS, 256).transpose(1, 0, 2)
    late_parts = [_by_core(s_uq), _by_core(s_ukv), _w_in_blocks(LATE_CHIPS, dw_in_lat, dw_in_rest)]
    late_other = _to_other_core(late_parts, name="dw_late_to_core")
    late_sums = [_pair_sum(p, o_, core, tr=256, name=f"dw_late_pair_sum_{i}")
                 for i, (p, o_) in enumerate(zip(late_parts, late_other))]
    late = _Exchange([(late_sums[0], ALL_CHIPS, None, False, False), (late_sums[1], ALL_CHIPS, None, False, False),
                      (late_sums[2], LATE_CHIPS, l_in, False, False)])
    dh, l_uq, l_ukv, l_in = _mm(dproj, w_in_n, name="d_h", nt=True, out_dtype=F32, tm=2048, tn=1024, tk=NP_IN // 4,
                                rider=late)
    grad_x, dg_pre = _prenorm_bwd(dh, x2, g_pre, dy, tm=tm)

    small = _gather_small(_pack_small(dg_pre, dg_post, dg_q, dg_kv, dbf[:, :N_HEADS], loss_part))

    res_in = _adamw(l_in, w_in[0], m_w_in[0], v_w_in[0], tr=256, name="adamw_w_in")
    res_uq = _adamw(l_uq, w_uq[0], m_w_uq[0], v_w_uq[0], tr=256, name="adamw_w_uq")
    res_ukv = _adamw(l_ukv, w_ukv[0], m_w_ukv[0], v_w_ukv[0], tr=256, name="adamw_w_ukv")
    res_out = _adamw(l_out, w_out[0], m_w_out[0], v_w_out[0], tr=128, name="adamw_w_out")
    res_small = _adamw_small(
        small,
        _pack_small(g_pre, g_post, g_q_latent, g_kv_latent, b_forget),
        _pack_small(m_g_pre, m_g_post, m_g_q_latent, m_g_kv_latent, m_b_forget),
        _pack_small(v_g_pre, v_g_post, v_g_q_latent, v_g_kv_latent, v_b_forget))
    small_out = [_unpack_small(r) for r in res_small]
    loss = small_out[0][1]

    def leaves(kind):
        (s_pre, s_post, s_q, s_kv, s_bf), _ = small_out[kind]
        return [s_pre, res_in[kind][None], s_q, res_uq[kind][None], s_kv, res_ukv[kind][None], s_bf,
                res_out[kind][None], s_post]

    return (loss, grad_x.reshape(x.shape), *leaves(0), *leaves(1), *leaves(2), *leaves(3))
```

```python
import functools

import numpy as np
import jax
import jax.numpy as jnp
from jax import lax
from jax.experimental import pallas as pl
from jax.experimental.pallas import tpu as pltpu

F32 = jnp.float32
BF16 = jnp.bfloat16
MESH = pl.DeviceIdType.MESH

N_DEV = 8
D_MODEL = 2048
N_HEADS = 8
HEAD = 128
Q_RANK = 768
KV_RANK = 512
ROPE = 64
D_IN = 6472
SHARD_IN = D_IN // N_DEV
NORM_EPS = 1e-6
ROPE_THETA = 10000.0
MLA_SCALE = (HEAD + ROPE) ** -0.5
FOX_SCALE = HEAD ** -0.5

QL0, KR0, FL0, KVL0, GM0, GF0, FQ0, FK0, FV0, NP_IN = 0, 768, 896, 1024, 1536, 2560, 3584, 4608, 5632, 6656
LAT_W = GM0
LANE = 128
_SEGMENTS = ((0, 768, QL0), (768, 512, KVL0), (1280, 32, KR0), (1312, 32, KR0 + 64), (1344, 1024, GM0),
             (2368, 3072, FQ0), (5440, 8, FL0), (5448, 1024, GF0))
LOG2E = 1.4426950408889634

ADAM_LR = 0.001
ADAM_B1 = 0.9
ADAM_B2 = 0.999
ADAM_EPS = 1e-08
ADAM_WD = 0.01
ADAM_STEP = 10

VMEM_LIMIT_BYTES = 56 * 1024 * 1024
SMALL_ROWS, SMALL_COLS = 8, 768


def _params(n_grid=0):
    return pltpu.CompilerParams(vmem_limit_bytes=VMEM_LIMIT_BYTES,
                                dimension_semantics=("arbitrary",) * n_grid if n_grid else None)


def _sigmoid(z):
    return 1.0 / (1.0 + jnp.exp(-z))


def _split3(v):
    a = v.astype(BF16)
    r = v - a.astype(F32)
    b = r.astype(BF16)
    c = (r - b.astype(F32)).astype(BF16)
    return a, b, c


def _mm(a, b, *, name, nt=False, out_dtype=F32, tm=1024, tn=512, tk=2048, b_cols=None, rider=None):
    m, k_dim = a.shape
    n = b.shape[0] if nt else b.shape[1]
    col0 = 0
    if b_cols is not None:
        assert not nt
        col0, n = b_cols
    assert (b.shape[1] if nt else b.shape[0]) == k_dim
    tm, tn, tk = min(tm, m), min(tn, n), min(tk, k_dim)
    assert m % tm == 0 and n % tn == 0 and k_dim % tk == 0 and col0 % tn == 0, (name, a.shape, b.shape)
    nk = k_dim // tk
    j0 = col0 // tn
    grid = (m // tm, n // tn, nk)
    dims = (((1,), (1 if nt else 0,)), ((), ()))
    n_rin = len(rider.operands) if rider else 0
    n_rout = len(rider.out_shape) if rider else 0

    def body(*refs):
        a_ref, b_ref = refs[:2]
        o_ref = refs[2 + n_rin]
        acc_ref = refs[3 + n_rin + n_rout]
        i, j, k = pl.program_id(0), pl.program_id(1), pl.program_id(2)
        if rider:
            rider_refs = (refs[2:2 + n_rin], refs[3 + n_rin:3 + n_rin + n_rout], refs[4 + n_rin + n_rout:])

            @pl.when(jnp.logical_and(i == 0, jnp.logical_and(j == 0, k == 0)))
            def _():
                rider.start(*rider_refs)

        @pl.when(k == 0)
        def _():
            acc_ref[...] = jnp.zeros_like(acc_ref)

        acc_ref[...] += lax.dot_general(a_ref[...], b_ref[...], dims, preferred_element_type=F32)

        @pl.when(k == nk - 1)
        def _():
            o_ref[...] = acc_ref[...].astype(o_ref.dtype)

        if rider:
            @pl.when(jnp.logical_and(i == grid[0] - 1, jnp.logical_and(j == grid[1] - 1, k == nk - 1)))
            def _():
                rider.wait(*rider_refs)

    b_spec = (pl.BlockSpec((tn, tk), lambda i, j, k: (j, k)) if nt
              else pl.BlockSpec((tk, tn), lambda i, j, k: (k, j0 + j)))
    a_spec = pl.BlockSpec((tm, tk), lambda i, j, k: (i, k))
    any_spec = pl.BlockSpec(memory_space=pl.ANY)
    out = pl.pallas_call(
        body, name=name, grid=grid,
        in_specs=[a_spec, b_spec] + [any_spec] * n_rin,
        out_specs=[pl.BlockSpec((tm, tn), lambda i, j, k: (i, j))] + [any_spec] * n_rout,
        out_shape=[jax.ShapeDtypeStruct((m, n), out_dtype)] + (list(rider.out_shape) if rider else []),
        scratch_shapes=[pltpu.VMEM((tm, tn), F32)] + (list(rider.scratch) if rider else []),
        input_output_aliases={2 + i_in: 1 + i_out for i_in, i_out in rider.aliases.items()} if rider else {},
        compiler_params=_params(3),
    )(a, b, *(rider.operands if rider else ()))
    return out if rider else out[0]


def _prenorm(x, g, rider, *, tm):
    t = x.shape[0]
    n_steps = t // tm
    n_rin, n_rout = len(rider.operands), len(rider.out_shape)

    def body(*refs):
        x_ref, g_ref = refs[:2]
        h_ref, ht_ref = refs[2 + n_rin:4 + n_rin]
        rider_refs = (refs[2:2 + n_rin], refs[4 + n_rin:4 + n_rin + n_rout], refs[4 + n_rin + n_rout:])
        i = pl.program_id(0)
        pl.when(i == 0)(lambda: rider.start(*rider_refs))
        xv = x_ref[...]
        r = lax.rsqrt(jnp.mean(xv * xv, axis=-1, keepdims=True) + NORM_EPS)
        h = xv * r * g_ref[...]
        h_ref[...] = h.astype(BF16)
        ht_ref[...] = h.T.astype(BF16)
        pl.when(i == n_steps - 1)(lambda: rider.wait(*rider_refs))

    any_spec = pl.BlockSpec(memory_space=pl.ANY)
    return pl.pallas_call(
        body, name="prenorm", grid=(n_steps,),
        in_specs=[pl.BlockSpec((tm, D_MODEL), lambda i: (i, 0)), pl.BlockSpec((1, D_MODEL), lambda i: (0, 0))]
        + [any_spec] * n_rin,
        out_specs=[pl.BlockSpec((tm, D_MODEL), lambda i: (i, 0)), pl.BlockSpec((D_MODEL, tm), lambda i: (0, i))]
        + [any_spec] * n_rout,
        out_shape=[jax.ShapeDtypeStruct((t, D_MODEL), BF16), jax.ShapeDtypeStruct((D_MODEL, t), BF16)]
        + list(rider.out_shape),
        scratch_shapes=list(rider.scratch),
        compiler_params=_params(1),
    )(x, g, *rider.operands)


def _rope_rows():
    inv = (np.float32(ROPE_THETA) ** (-np.arange(0, ROPE, 2, dtype=np.float32) / np.float32(ROPE))).astype(np.float32)
    invf = np.zeros((1, LANE), np.float32)
    sgn = np.zeros((1, LANE), np.float32)
    invf[0, 0:32] = inv
    invf[0, 64:96] = inv
    sgn[0, 0:32] = -1.0
    sgn[0, 64:96] = 1.0
    return jnp.asarray(invf), jnp.asarray(sgn)


def _rot(v, cos_t, sin_t):
    return v * cos_t + pltpu.roll(v, 64, 1) * sin_t


def _rot_bwd(dv, cos_t, sin_t):
    return dv * cos_t + pltpu.roll(dv * sin_t, 64, 1)


def _mid_fwd(proj, flog, g_q, g_kv, bf_row, pos_col, *, tm):
    t = proj.shape[0]
    invf, sgn = _rope_rows()

    def body(p_ref, fl_ref, gq_ref, gkv_ref, bf_ref, pos_ref, invf_ref, sgn_ref,
             qn_ref, kvn_ref, kr_ref, cos_ref, sin_ref, c_ref, qnt_ref, kvnt_ref, carry_ref):
        i = pl.program_id(0)

        @pl.when(i == 0)
        def _():
            carry_ref[...] = jnp.zeros_like(carry_ref)

        ql = p_ref[:, QL0:QL0 + Q_RANK].astype(F32)
        r = lax.rsqrt(jnp.mean(ql * ql, axis=-1, keepdims=True) + NORM_EPS)
        qn = ql * r * gq_ref[...]
        qn_ref[...] = qn.astype(BF16)
        qnt_ref[...] = qn.T.astype(BF16)
        kvl = p_ref[:, KVL0:KVL0 + KV_RANK].astype(F32)
        r = lax.rsqrt(jnp.mean(kvl * kvl, axis=-1, keepdims=True) + NORM_EPS)
        kvn = kvl * r * gkv_ref[...]
        kvn_ref[...] = kvn.astype(BF16)
        kvnt_ref[...] = kvn.T.astype(BF16)

        ang = pos_ref[...] * invf_ref[...]
        cos_t = jnp.cos(ang)
        sin_t = jnp.sin(ang) * sgn_ref[...]
        cos_ref[...] = cos_t
        sin_ref[...] = sin_t
        kr_ref[...] = _rot(p_ref[:, KR0:KR0 + LANE].astype(F32), cos_t, sin_t).astype(BF16)

        z = fl_ref[...] + bf_ref[...]
        logf = jnp.minimum(z, 0.0) - jnp.log(1.0 + jnp.exp(-jnp.abs(z)))
        row = lax.broadcasted_iota(jnp.int32, (tm, tm), 0)
        col = lax.broadcasted_iota(jnp.int32, (tm, tm), 1)
        tri = (col <= row).astype(BF16)
        acc = carry_ref[0:1, :]
        for part in _split3(logf):
            acc = acc + jnp.dot(tri, part, preferred_element_type=F32)
        c_ref[...] = acc * (1.0 / FOX_SCALE)
        carry_ref[0:1, :] = carry_ref[0:1, :] + jnp.sum(logf, axis=0, keepdims=True)

    row_spec = lambda w: pl.BlockSpec((tm, w), lambda i: (i, 0))
    vec_spec = lambda w: pl.BlockSpec((1, w), lambda i: (0, 0))
    return pl.pallas_call(
        body, name="mid_fwd", grid=(t // tm,),
        in_specs=[row_spec(LAT_W), row_spec(LANE), vec_spec(Q_RANK), vec_spec(KV_RANK), vec_spec(LANE),
                  pl.BlockSpec((tm, 1), lambda i: (i, 0)), vec_spec(LANE), vec_spec(LANE)],
        out_specs=[row_spec(Q_RANK), row_spec(KV_RANK), row_spec(LANE), row_spec(LANE), row_spec(LANE), row_spec(LANE),
                   pl.BlockSpec((Q_RANK, tm), lambda i: (0, i)), pl.BlockSpec((KV_RANK, tm), lambda i: (0, i))],
        out_shape=[jax.ShapeDtypeStruct((t, Q_RANK), BF16), jax.ShapeDtypeStruct((t, KV_RANK), BF16),
                   jax.ShapeDtypeStruct((t, LANE), BF16), jax.ShapeDtypeStruct((t, LANE), F32),
                   jax.ShapeDtypeStruct((t, LANE), F32), jax.ShapeDtypeStruct((t, LANE), F32),
                   jax.ShapeDtypeStruct((Q_RANK, t), BF16), jax.ShapeDtypeStruct((KV_RANK, t), BF16)],
        scratch_shapes=[pltpu.VMEM((8, LANE), F32)],
        compiler_params=_params(1),
    )(proj, flog, g_q, g_kv, bf_row, pos_col, invf, sgn)


def _q_up_rope(qn, w_uq_n, cos_t, sin_t, *, tm):
    t = qn.shape[0]
    tn = 2 * 256

    def body(a_ref, b_ref, cos_ref, sin_ref, o_ref):
        q = jnp.dot(a_ref[...], b_ref[...], preferred_element_type=F32)
        c, s = cos_ref[...], sin_ref[...]
        for u in range(tn // 256):
            o_ref[:, 256 * u:256 * u + 128] = q[:, 256 * u:256 * u + 128].astype(BF16)
            o_ref[:, 256 * u + 128:256 * u + 256] = _rot(q[:, 256 * u + 128:256 * u + 256], c, s).astype(BF16)

    return pl.pallas_call(
        body, name="q_up_rope", grid=(t // tm, N_HEADS * 256 // tn),
        in_specs=[pl.BlockSpec((tm, Q_RANK), lambda i, j: (i, 0)), pl.BlockSpec((Q_RANK, tn), lambda i, j: (0, j)),
                  pl.BlockSpec((tm, LANE), lambda i, j: (i, 0)), pl.BlockSpec((tm, LANE), lambda i, j: (i, 0))],
        out_specs=pl.BlockSpec((tm, tn), lambda i, j: (i, j)),
        out_shape=jax.ShapeDtypeStruct((t, N_HEADS * 256), BF16),
        compiler_params=_params(2),
    )(qn, w_uq_n, cos_t, sin_t)


def _attn_fwd(fox, operands, *, t, tb, name):
    nb = t // tb
    scale = FOX_SCALE if fox else MLA_SCALE
    exp2_scale = scale * LOG2E
    pair = 2 * HEAD
    pair0 = N_HEADS // 2 if fox else 0
    q_w = HEAD if fox else 2 * HEAD
    nt_dims = (((1,), (1,)), ((), ()))
    tn_dims = (((0,), (0,)), ((), ()))

    def body(*refs):
        if fox:
            (q_ref, k_ref, v_ref, gate_ref, cq_ref, ck_ref, _, _, _,
             o_ref, og_ref, ogt_ref, lse_ref, m_s, l_s, acc_s) = refs
        else:
            q_ref, kv_ref, kr_ref, gate_ref, o_ref, og_ref, ogt_ref, lse_ref, m_s, l_s, acc_s = refs
        qi = pl.program_id(1)
        m_s[...] = jnp.full_like(m_s, -jnp.inf)
        l_s[...] = jnp.zeros_like(l_s)
        acc_s[...] = jnp.zeros_like(acc_s)

        def chunk(kc, masked):
            off = pl.multiple_of(kc * tb, tb)
            scores = []
            for u in range(2):
                q = q_ref[:, q_w * u:q_w * (u + 1)]
                if fox:
                    kk = k_ref[pl.ds(off, tb), HEAD * u:HEAD * (u + 1)]
                else:
                    kk = jnp.concatenate([kv_ref[pl.ds(off, tb), pair * u:pair * u + HEAD],
                                          kr_ref[pl.ds(off, tb), :]], axis=1)
                s = lax.dot_general(kk, q, nt_dims, preferred_element_type=F32)
                if fox:
                    s = s + cq_ref[u, 0] - ck_ref[u, pl.ds(off, tb), :]
                if masked:
                    row = lax.broadcasted_iota(jnp.int32, (tb, tb), 0)
                    col = lax.broadcasted_iota(jnp.int32, (tb, tb), 1)
                    s = jnp.where(row <= col, s, -jnp.inf)
                scores.append(s)
            for u in range(2):
                s = scores[u]
                m_prev = m_s[u]
                m_new = jnp.maximum(m_prev, jnp.max(s, axis=0, keepdims=True))
                alpha = jnp.exp2((m_prev - m_new) * exp2_scale)
                p = jnp.exp2((s - m_new) * exp2_scale)
                l_s[u] = alpha * l_s[u] + jnp.sum(p, axis=0, keepdims=True)
                if fox:
                    vv = v_ref[pl.ds(off, tb), HEAD * u:HEAD * (u + 1)]
                else:
                    vv = kv_ref[pl.ds(off, tb), pair * u + HEAD:pair * (u + 1)]
                acc_s[u] = alpha * acc_s[u] + lax.dot_general(vv, p.astype(BF16), tn_dims,
                                                              preferred_element_type=F32)
                m_s[u] = m_new

        def loop_body(kc, carry):
            chunk(kc, False)
            return carry

        lax.fori_loop(0, qi, loop_body, 0)
        chunk(qi, True)
        for u in range(2):
            cols = slice(HEAD * u, HEAD * (u + 1))
            o_t = acc_s[u] / l_s[u]
            o = o_t.T
            o_ref[:, cols] = o
            g = gate_ref[:, cols].astype(F32)
            silu = g * _sigmoid(g)
            og_ref[:, cols] = (o * silu).astype(BF16)
            ogt_ref[cols, :] = (o_t * silu.T).astype(BF16)
            lse_ref[u, 0] = m_s[u] * scale + jnp.log(l_s[u])

    any_spec = pl.BlockSpec(memory_space=pl.ANY)
    row_stat = pl.BlockSpec((2, 1, 1, tb), lambda g, i: (g, i, 0, 0))
    if fox:
        proj, c_col, c_row4, o_all, og_all, ogt_all = operands
        ins = [proj, proj, proj, proj, c_row4, c_col, o_all, og_all, ogt_all]
        in_specs = [pl.BlockSpec((tb, pair), lambda g, i: (i, FQ0 // pair + g)),
                    pl.BlockSpec((t, pair), lambda g, i: (0, FK0 // pair + g)),
                    pl.BlockSpec((t, pair), lambda g, i: (0, FV0 // pair + g)),
                    pl.BlockSpec((tb, pair), lambda g, i: (i, GF0 // pair + g)),
                    row_stat, pl.BlockSpec((2, t, 1), lambda g, i: (g, 0, 0)), any_spec, any_spec, any_spec]
        aliases = {6: 0, 7: 1, 8: 2}
    else:
        q_full, kv, kr, proj = operands
        ins = [q_full, kv, kr, proj]
        in_specs = [pl.BlockSpec((tb, 2 * pair), lambda g, i: (i, g)),
                    pl.BlockSpec((t, 2 * pair), lambda g, i: (0, g)),
                    pl.BlockSpec((t, HEAD), lambda g, i: (0, 0)),
                    pl.BlockSpec((tb, pair), lambda g, i: (i, GM0 // pair + g))]
        aliases = {}
    return pl.pallas_call(
        body, name=name, grid=(N_HEADS // 2, nb), in_specs=in_specs,
        out_specs=[pl.BlockSpec((tb, pair), lambda g, i: (i, pair0 + g)),
                   pl.BlockSpec((tb, pair), lambda g, i: (i, pair0 + g)),
                   pl.BlockSpec((pair, tb), lambda g, i: (pair0 + g, i)), row_stat],
        out_shape=[jax.ShapeDtypeStruct((t, 2 * N_HEADS * HEAD), F32), jax.ShapeDtypeStruct((t, 2 * N_HEADS * HEAD), BF16),
                   jax.ShapeDtypeStruct((2 * N_HEADS * HEAD, t), BF16), jax.ShapeDtypeStruct((N_HEADS, nb, 1, tb), F32)],
        scratch_shapes=[pltpu.VMEM((2, 1, tb), F32), pltpu.VMEM((2, 1, tb), F32), pltpu.VMEM((2, HEAD, tb), F32)],
        input_output_aliases=aliases,
        compiler_params=_params(2),
    )(*ins)


def _out_norm_loss(og, w_out_n, x, target, g, *, tm):
    t = og.shape[0]

    def body(og_ref, w_ref, x_ref, t_ref, g_ref, dy_ref, do_ref, dg_ref, loss_ref):
        i = pl.program_id(0)

        @pl.when(i == 0)
        def _():
            dg_ref[...] = jnp.zeros_like(dg_ref)
            loss_ref[...] = jnp.zeros_like(loss_ref)

        ov = jnp.dot(og_ref[...], w_ref[...], preferred_element_type=F32)
        gv = g_ref[...]
        r = lax.rsqrt(jnp.mean(ov * ov, axis=-1, keepdims=True) + NORM_EPS)
        oh = ov * r
        e = x_ref[...] + oh * gv - t_ref[...]
        loss_ref[...] += 0.5 * jnp.sum(jnp.mean(e * e, axis=-1, keepdims=True), axis=0, keepdims=True)
        dy = e * (1.0 / D_MODEL)
        dy_ref[...] = dy
        dyg = dy * gv
        do_ref[...] = (r * (dyg - oh * jnp.mean(dyg * oh, axis=-1, keepdims=True))).astype(BF16)
        dg_ref[...] += jnp.sum(dy * oh, axis=0, keepdims=True)

    row = pl.BlockSpec((tm, D_MODEL), lambda i: (i, 0))
    vec = pl.BlockSpec((1, D_MODEL), lambda i: (0, 0))
    whole_w = pl.BlockSpec((D_MODEL, D_MODEL), lambda i: (0, 0), pipeline_mode=pl.Buffered(1))
    return pl.pallas_call(
        body, name="out_norm_loss", grid=(t // tm,),
        in_specs=[row, whole_w, row, row, vec],
        out_specs=[row, row, vec, pl.BlockSpec((1, 1), lambda i: (0, 0))],
        out_shape=[jax.ShapeDtypeStruct((t, D_MODEL), F32), jax.ShapeDtypeStruct((t, D_MODEL), BF16),
                   jax.ShapeDtypeStruct((1, D_MODEL), F32), jax.ShapeDtypeStruct((1, 1), F32)],
        compiler_params=_params(1),
    )(og, w_out_n, x, target, g)


def _dog_gate(d_o_post, w_out_n, o_all, proj, *, tm):
    t = d_o_post.shape[0]
    n_group = 4
    pair = n_group * HEAD
    gate_blk = GM0 // pair
    assert GM0 % pair == 0 and GF0 == GM0 + N_HEADS * HEAD

    def body(do_ref, w_ref, o_ref, p_ref, dattn_ref, dproj_ref, delta_ref):
        j = pl.program_id(1)

        @pl.when(j == 0)
        def _():
            delta_ref[...] = jnp.zeros_like(delta_ref)

        dog = lax.dot_general(do_ref[...], w_ref[...], (((1,), (1,)), ((), ())), preferred_element_type=F32)
        g = p_ref[...].astype(F32)
        ov = o_ref[...]
        sg = _sigmoid(g)
        d_o = dog * (g * sg)
        dattn_ref[...] = d_o.astype(BF16)
        dproj_ref[...] = (dog * ov * (sg * (1.0 + g * (1.0 - sg)))).astype(BF16)
        prod = d_o * ov
        lane = lax.broadcasted_iota(jnp.int32, (tm, LANE), 1)
        delta = delta_ref[...]
        for u in range(n_group):
            part = jnp.sum(prod[:, HEAD * u:HEAD * (u + 1)], axis=-1, keepdims=True)
            delta = jnp.where(lane == n_group * j + u, part, delta)
        delta_ref[...] = delta

    return pl.pallas_call(
        body, name="dog_gate", grid=(t // tm, 2 * N_HEADS // n_group),
        in_specs=[pl.BlockSpec((tm, D_MODEL), lambda i, j: (i, 0)), pl.BlockSpec((pair, D_MODEL), lambda i, j: (j, 0)),
                  pl.BlockSpec((tm, pair), lambda i, j: (i, j)), pl.BlockSpec((tm, pair), lambda i, j: (i, gate_blk + j))],
        out_specs=[pl.BlockSpec((tm, pair), lambda i, j: (i, j)), pl.BlockSpec((tm, pair), lambda i, j: (i, gate_blk + j)),
                   pl.BlockSpec((tm, LANE), lambda i, j: (i, 0))],
        out_shape=[jax.ShapeDtypeStruct((t, 2048), BF16), jax.ShapeDtypeStruct((t, NP_IN), BF16),
                   jax.ShapeDtypeStruct((t, LANE), F32)],
        compiler_params=_params(2),
    )(d_o_post, w_out_n, o_all, proj)


def _attn_bwd(fox, operands, *, t, tb, name, rider=None):
    nb = t // tb
    n_pairs = N_HEADS // 2
    pair = 2 * HEAD
    scale = FOX_SCALE if fox else MLA_SCALE
    q_w = HEAD if fox else 2 * HEAD
    nt_dims = (((1,), (1,)), ((), ()))
    tn_dims = (((0,), (0,)), ((), ()))
    n_rin = len(rider.operands) if rider else 0
    n_rout = len(rider.out_shape) if rider else 0
    n_in, n_out, n_scr = (9, 3, 9) if fox else (6, 3, 2)

    def body(*refs):
        ends = np.cumsum([0, n_in, n_rin, n_out, n_rout, n_scr])
        in_refs, rider_in, out_refs, rider_out, scr_refs = (refs[a:b] for a, b in zip(ends[:-1], ends[1:]))
        rider_refs = (rider_in, rider_out, refs[ends[-1]:])
        if fox:
            q_ref, k_ref, v_ref, do_ref, lse_ref, dl_ref, cq_ref, ck_ref, _ = in_refs
            dproj_ref, dck_ref, dcq_ref = out_refs
            dq_acc, dk_s, dv_s, dc_s, dcq_s, stage_q, stage_k, stage_v, put_sems = scr_refs
        else:
            q_ref, kv_ref, kr_ref, do_ref, lse_ref, dl_ref = in_refs
            dq_acc, dkv_ref, dkr_ref = out_refs
            dk_s, dv_s = scr_refs
        g = pl.program_id(0)
        ki = pl.program_id(1)
        if rider:
            @pl.when(jnp.logical_and(g == 0, ki == 0))
            def _():
                rider.start(*rider_refs)

        @pl.when(ki == 0)
        def _():
            dq_acc[...] = jnp.zeros_like(dq_acc)
            if fox:
                dcq_s[...] = jnp.zeros_like(dcq_s)

        dk_s[...] = jnp.zeros_like(dk_s)
        dv_s[...] = jnp.zeros_like(dv_s)
        if fox:
            dc_s[...] = jnp.zeros_like(dc_s)
            keys = [k_ref[:, HEAD * u:HEAD * (u + 1)] for u in range(2)]
            vals = [v_ref[:, HEAD * u:HEAD * (u + 1)] for u in range(2)]
        else:
            keys = [jnp.concatenate([kv_ref[:, pair * u:pair * u + HEAD], kr_ref[...]], axis=1) for u in range(2)]
            vals = [kv_ref[:, pair * u + HEAD:pair * (u + 1)] for u in range(2)]

        def chunk(qc, masked):
            off = pl.multiple_of(qc * tb, tb)
            for u in range(2):
                kk, vv = keys[u], vals[u]
                qq = q_ref[pl.ds(off, tb), q_w * u:q_w * (u + 1)]
                dd = do_ref[pl.ds(off, tb), HEAD * u:HEAD * (u + 1)]
                s = lax.dot_general(kk, qq, nt_dims, preferred_element_type=F32)
                if fox:
                    s = s + cq_ref[u, qc] - ck_ref[u]
                if masked:
                    row = lax.broadcasted_iota(jnp.int32, (tb, tb), 0)
                    col = lax.broadcasted_iota(jnp.int32, (tb, tb), 1)
                    s = jnp.where(row <= col, s, -jnp.inf)
                p = jnp.exp2(s * (scale * LOG2E) - lse_ref[u, qc] * LOG2E)
                dv_s[u] += jnp.dot(p.astype(BF16), dd, preferred_element_type=F32)
                dp = lax.dot_general(vv, dd, nt_dims, preferred_element_type=F32)
                ds = p * (dp - dl_ref[u, qc])
                if fox:
                    dc_s[u] += jnp.sum(ds, axis=1, keepdims=True)
                    dcq_s[u, qc] += jnp.sum(ds, axis=0, keepdims=True)
                dsb = (ds * scale).astype(BF16)
                dk_s[u] += jnp.dot(dsb, qq, preferred_element_type=F32)
                dq_acc[pl.ds(off, tb), q_w * u:q_w * (u + 1)] += lax.dot_general(dsb, kk, tn_dims,
                                                                                 preferred_element_type=F32)

        chunk(ki, True)

        def loop_body(qc, carry):
            chunk(qc, False)
            return carry

        lax.fori_loop(ki + 1, nb, loop_body, 0)

        def put(stage_ref, rows, seg0, sem):
            col0 = pl.multiple_of(seg0 + g * pair, pair)
            return pltpu.make_async_copy(stage_ref, dproj_ref.at[rows, pl.ds(col0, pair)], sem)

        if fox:
            rows = pl.ds(pl.multiple_of(ki * tb, tb), tb)
            block_puts = [put(stage_k, rows, FK0, put_sems.at[1]), put(stage_v, rows, FV0, put_sems.at[2])]
            pair_put = put(stage_q, pl.ds(0, t), FQ0, put_sems.at[0])

            @pl.when(jnp.logical_or(g > 0, ki > 0))
            def _():
                for cp in block_puts:
                    cp.wait()

            for u in range(2):
                stage_k[:, HEAD * u:HEAD * (u + 1)] = dk_s[u].astype(BF16)
                stage_v[:, HEAD * u:HEAD * (u + 1)] = dv_s[u].astype(BF16)
                dck_ref[u] = -dc_s[u]
            for cp in block_puts:
                cp.start()

            @pl.when(ki == nb - 1)
            def _():
                @pl.when(g > 0)
                def _():
                    pair_put.wait()

                stage_q[...] = dq_acc[...].astype(BF16)
                pair_put.start()
                dcq_ref[...] = dcq_s[...]

            @pl.when(jnp.logical_and(g == n_pairs - 1, ki == nb - 1))
            def _():
                for cp in block_puts + [pair_put]:
                    cp.wait()
        else:
            dkv_ref[...] = jnp.concatenate([dk_s[0, :, :HEAD], dv_s[0], dk_s[1, :, :HEAD], dv_s[1]], axis=1).astype(BF16)
            dkr_ref[...] = jnp.concatenate([dk_s[0, :, HEAD:], dk_s[1, :, HEAD:]], axis=1)

        if rider:
            @pl.when(jnp.logical_and(g == n_pairs - 1, ki == nb - 1))
            def _():
                rider.wait(*rider_refs)

    stat = pl.BlockSpec((2, nb, 1, tb), lambda g, i: (g, 0, 0, 0))
    aliases = {}
    if fox:
        proj, d_o, lse4, delta4, c_row4, c_col, dproj = operands
        ins = [proj, proj, proj, d_o, lse4, delta4, c_row4, c_col, dproj]
        any_spec = pl.BlockSpec(memory_space=pl.ANY)
        in_specs = [pl.BlockSpec((t, pair), lambda g, i: (0, FQ0 // pair + g)),
                    pl.BlockSpec((tb, pair), lambda g, i: (i, FK0 // pair + g)),
                    pl.BlockSpec((tb, pair), lambda g, i: (i, FV0 // pair + g)),
                    pl.BlockSpec((t, pair), lambda g, i: (0, n_pairs + g)),
                    stat, stat, stat, pl.BlockSpec((2, tb, 1), lambda g, i: (g, i, 0)), any_spec]
        aliases = {8: 0}
        out_specs = [any_spec, pl.BlockSpec((2, tb, 1), lambda g, i: (g, i, 0)), stat]
        out_shape = [jax.ShapeDtypeStruct(dproj.shape, dproj.dtype), jax.ShapeDtypeStruct((N_HEADS, t, 1), F32),
                     jax.ShapeDtypeStruct((N_HEADS, nb, 1, tb), F32)]
        scratch = [pltpu.VMEM((t, pair), F32), pltpu.VMEM((2, tb, HEAD), F32), pltpu.VMEM((2, tb, HEAD), F32),
                   pltpu.VMEM((2, tb, 1), F32), pltpu.VMEM((2, nb, 1, tb), F32),
                   pltpu.VMEM((t, pair), BF16), pltpu.VMEM((tb, pair), BF16), pltpu.VMEM((tb, pair), BF16),
                   pltpu.SemaphoreType.DMA((3,))]
    else:
        q_full, kv, kr, d_o, lse4, delta4 = operands
        ins = [q_full, kv, kr, d_o, lse4, delta4]
        in_specs = [pl.BlockSpec((t, 2 * pair), lambda g, i: (0, g)),
                    pl.BlockSpec((tb, 2 * pair), lambda g, i: (i, g)),
                    pl.BlockSpec((tb, HEAD), lambda g, i: (i, 0)),
                    pl.BlockSpec((t, pair), lambda g, i: (0, g)),
                    stat, stat]
        out_specs = [pl.BlockSpec((t, 2 * pair), lambda g, i: (0, g)), pl.BlockSpec((tb, 2 * pair), lambda g, i: (i, g)),
                     pl.BlockSpec((tb, pair), lambda g, i: (i, g))]
        out_shape = [jax.ShapeDtypeStruct((t, 2048), F32), jax.ShapeDtypeStruct((t, 2048), BF16),
                     jax.ShapeDtypeStruct((t, 1024), F32)]
        scratch = [pltpu.VMEM((2, tb, 2 * HEAD), F32), pltpu.VMEM((2, tb, HEAD), F32)]
    assert (len(ins), len(out_specs), len(scratch)) == (n_in, n_out, n_scr)
    if rider:
        any_spec = pl.BlockSpec(memory_space=pl.ANY)
        aliases = {**aliases, **{n_in + i_in: n_out + i_out for i_in, i_out in rider.aliases.items()}}
        ins = ins + list(rider.operands)
        in_specs = in_specs + [any_spec] * n_rin
        out_specs = out_specs + [any_spec] * n_rout
        out_shape = out_shape + list(rider.out_shape)
        scratch = scratch + list(rider.scratch)
    return pl.pallas_call(
        body, name=name, grid=(n_pairs, nb), in_specs=in_specs, out_specs=out_specs, out_shape=out_shape,
        scratch_shapes=scratch, input_output_aliases=aliases, compiler_params=_params(2),
    )(*ins)


def _mid_bwd(dq_full, dkr, cos_t, sin_t, dck, flog, bf_row, *, tm):
    t = dq_full.shape[0]
    n = t // tm

    def body(dq_ref, dkr_ref, cos_ref, sin_ref, dck_ref, fl_ref, bf_ref,
             dq2_ref, dkraw_ref, dfl_ref, dbf_ref, carry_ref):
        i = pl.program_id(0)

        @pl.when(i == 0)
        def _():
            carry_ref[...] = jnp.zeros_like(carry_ref)
            dbf_ref[...] = jnp.zeros_like(dbf_ref)

        c, s = cos_ref[...], sin_ref[...]
        dkr_sum = jnp.zeros((tm, LANE), F32)
        for h in range(N_HEADS):
            dq2_ref[:, 256 * h:256 * h + 128] = dq_ref[:, 256 * h:256 * h + 128].astype(BF16)
            dq2_ref[:, 256 * h + 128:256 * h + 256] = _rot_bwd(dq_ref[:, 256 * h + 128:256 * h + 256], c, s).astype(BF16)
            dkr_sum = dkr_sum + dkr_ref[:, HEAD * h:HEAD * (h + 1)]
        dkraw_ref[...] = _rot_bwd(dkr_sum, c, s).astype(BF16)

        dc = dck_ref[...]
        row = lax.broadcasted_iota(jnp.int32, (tm, tm), 0)
        col = lax.broadcasted_iota(jnp.int32, (tm, tm), 1)
        tri = (col >= row).astype(BF16)
        acc = carry_ref[0:1, :]
        for part in _split3(dc):
            acc = acc + jnp.dot(tri, part, preferred_element_type=F32)
        carry_ref[0:1, :] = carry_ref[0:1, :] + jnp.sum(dc, axis=0, keepdims=True)
        z = fl_ref[...] + bf_ref[...]
        dz = acc / (1.0 + jnp.exp(z))
        dfl_ref[...] = dz.astype(BF16)
        dbf_ref[...] += jnp.sum(dz, axis=0, keepdims=True)

    rev = lambda w: pl.BlockSpec((tm, w), lambda i: (n - 1 - i, 0))
    vec = lambda w: pl.BlockSpec((1, w), lambda i: (0, 0))
    return pl.pallas_call(
        body, name="mid_bwd", grid=(n,),
        in_specs=[rev(2048), rev(1024), rev(LANE), rev(LANE), rev(LANE), rev(LANE), vec(LANE)],
        out_specs=[rev(2048), rev(LANE), rev(LANE), vec(LANE)],
        out_shape=[jax.ShapeDtypeStruct((t, 2048), BF16), jax.ShapeDtypeStruct((t, LANE), BF16),
                   jax.ShapeDtypeStruct((t, LANE), BF16), jax.ShapeDtypeStruct((1, LANE), F32)],
        scratch_shapes=[pltpu.VMEM((8, LANE), F32)],
        compiler_params=_params(1),
    )(dq_full, dkr, cos_t, sin_t, dck, flog, bf_row)


def _norm_bwd(proj, dqn, dkvn, g_q, g_kv, dkr_raw, dfl, dproj, *, tm):
    t = proj.shape[0]
    assert (KR0, FL0, KVL0, LAT_W) == (Q_RANK, Q_RANK + LANE, Q_RANK + 2 * LANE, Q_RANK + 2 * LANE + KV_RANK)

    def body(p_ref, dqn_ref, dkvn_ref, gq_ref, gkv_ref, dkr_ref, dfl_ref, _, dproj_ref, dgq_ref, dgkv_ref):
        i = pl.program_id(0)

        @pl.when(i == 0)
        def _():
            dgq_ref[...] = jnp.zeros_like(dgq_ref)
            dgkv_ref[...] = jnp.zeros_like(dgkv_ref)

        d_lat = []
        for lo, w, dn_ref, g_ref, dg_ref in ((QL0, Q_RANK, dqn_ref, gq_ref, dgq_ref),
                                             (KVL0, KV_RANK, dkvn_ref, gkv_ref, dgkv_ref)):
            xv = p_ref[:, lo:lo + w].astype(F32)
            r = lax.rsqrt(jnp.mean(xv * xv, axis=-1, keepdims=True) + NORM_EPS)
            xh = xv * r
            dn = dn_ref[...]
            dg_ref[...] += jnp.sum(dn * xh, axis=0, keepdims=True)
            dxh = dn * g_ref[...]
            d_lat.append((r * (dxh - xh * jnp.mean(dxh * xh, axis=-1, keepdims=True))).astype(BF16))
        dproj_ref[...] = jnp.concatenate([d_lat[0], dkr_ref[...], dfl_ref[...], d_lat[1]], axis=1)

    row = lambda w: pl.BlockSpec((tm, w), lambda i: (i, 0))
    vec = lambda w: pl.BlockSpec((1, w), lambda i: (0, 0))
    return pl.pallas_call(
        body, name="norm_bwd", grid=(t // tm,),
        in_specs=[row(LAT_W), row(Q_RANK), row(KV_RANK), vec(Q_RANK), vec(KV_RANK), row(LANE), row(LANE),
                  pl.BlockSpec(memory_space=pl.ANY)],
        out_specs=[row(LAT_W), vec(Q_RANK), vec(KV_RANK)],
        out_shape=[jax.ShapeDtypeStruct(dproj.shape, dproj.dtype),
                   jax.ShapeDtypeStruct((1, Q_RANK), F32), jax.ShapeDtypeStruct((1, KV_RANK), F32)],
        input_output_aliases={7: 0},
        compiler_params=_params(1),
    )(proj, dqn, dkvn, g_q, g_kv, dkr_raw, dfl, dproj)


def _prenorm_bwd(dh, x, g, dy, *, tm):
    t = x.shape[0]

    def body(dh_ref, x_ref, g_ref, dy_ref, gx_ref, dg_ref):
        i = pl.program_id(0)

        @pl.when(i == 0)
        def _():
            dg_ref[...] = jnp.zeros_like(dg_ref)

        xv = x_ref[...]
        r = lax.rsqrt(jnp.mean(xv * xv, axis=-1, keepdims=True) + NORM_EPS)
        xh = xv * r
        dn = dh_ref[...]
        dg_ref[...] += jnp.sum(dn * xh, axis=0, keepdims=True)
        dxh = dn * g_ref[...]
        gx_ref[...] = dy_ref[...] + r * (dxh - xh * jnp.mean(dxh * xh, axis=-1, keepdims=True))

    row = pl.BlockSpec((tm, D_MODEL), lambda i: (i, 0))
    vec = pl.BlockSpec((1, D_MODEL), lambda i: (0, 0))
    return pl.pallas_call(
        body, name="prenorm_bwd", grid=(t // tm,),
        in_specs=[row, row, vec, row], out_specs=[row, vec],
        out_shape=[jax.ShapeDtypeStruct((t, D_MODEL), F32), jax.ShapeDtypeStruct((1, D_MODEL), F32)],
        compiler_params=_params(1),
    )(dh, x, g, dy)


def _adam_math(w, g, m, v):
    m = ADAM_B1 * m + (1.0 - ADAM_B1) * g
    v = ADAM_B2 * v + (1.0 - ADAM_B2) * (g * g)
    m_hat = m / (1.0 - ADAM_B1 ** ADAM_STEP)
    v_hat = v / (1.0 - ADAM_B2 ** ADAM_STEP)
    delta = -ADAM_LR * (m_hat / (jnp.sqrt(v_hat) + ADAM_EPS) + ADAM_WD * w)
    return delta, m, v


def _adamw(land, w, m, v, *, tr, name):
    rows, cols = w.shape

    def body(l_ref, w_ref, m_ref, v_ref, g_ref, d_ref, nm_ref, nv_ref):
        g = l_ref[0].astype(F32)
        for s in range(1, N_CHIPS):
            g = g + l_ref[s].astype(F32)
        g_ref[...] = g
        d_ref[...], nm_ref[...], nv_ref[...] = _adam_math(w_ref[...], g, m_ref[...], v_ref[...])

    blk = pl.BlockSpec((tr, cols), lambda i: (i, 0))
    return pl.pallas_call(
        body, name=name, grid=(rows // tr,),
        in_specs=[pl.BlockSpec((N_CHIPS, tr, cols), lambda i: (0, i, 0)), blk, blk, blk],
        out_specs=[blk, blk, blk, blk],
        out_shape=[jax.ShapeDtypeStruct((rows, cols), F32)] * 4,
        compiler_params=_params(1),
    )(land, w, m, v)


def _adamw_small(gathered, w, m, v):
    def body(a_ref, w_ref, m_ref, v_ref, g_ref, d_ref, nm_ref, nv_ref):
        g = a_ref[0:SMALL_ROWS, :]
        for s in range(1, N_DEV):
            g = g + a_ref[SMALL_ROWS * s:SMALL_ROWS * (s + 1), :]
        g_ref[...] = g
        d_ref[...], nm_ref[...], nv_ref[...] = _adam_math(w_ref[...], g, m_ref[...], v_ref[...])

    return pl.pallas_call(
        body, name="adamw_small",
        out_shape=[jax.ShapeDtypeStruct((SMALL_ROWS, SMALL_COLS), F32)] * 4,
        compiler_params=_params(),
    )(gathered, w, m, v)


def _place():
    x, y, c = lax.axis_index("x"), lax.axis_index("y"), lax.axis_index("c")
    return x, y, c


def _flip(p, k):
    x, y, c = p
    return (1 - x if k & 4 else x, 1 - y if k & 2 else y, 1 - c if k & 1 else c)


def _index(p):
    return 4 * p[0] + 2 * p[1] + p[2]


class _AllGather:
    def __init__(self, shard):
        assert shard.shape[0] % 32 == 0
        self.half = shard.shape[0] // 2
        self.operands = [shard]
        self.out_shape = [jax.ShapeDtypeStruct((N_DEV,) + shard.shape, shard.dtype)]
        self.aliases = {}
        self.scratch = [pltpu.SemaphoreType.DMA((9,)), pltpu.SemaphoreType.DMA((9,)), pltpu.SemaphoreType.DMA(())]

    def _parts(self, ins, outs, scratch):
        (in_ref,), (out_ref,), (send_sems, recv_sems, local_sem) = ins, outs, scratch
        me = _place()

        def copy(k, block, to, part=None, src=None):
            dst = out_ref.at[_index(block)] if part is None else out_ref.at[_index(block), part]
            return pltpu.make_async_remote_copy(
                src_ref=dst if src is None else src, dst_ref=dst, send_sem=send_sems.at[k], recv_sem=recv_sems.at[k],
                device_id=to, device_id_type=MESH)

        mine = pltpu.make_async_copy(in_ref, out_ref.at[_index(me)], local_sem)
        first = [copy(0, me, _flip(me, 1), src=in_ref), copy(1, me, _flip(me, 4), src=in_ref),
                 copy(2, me, _flip(me, 2), src=in_ref)]
        return me, copy, mine, first

    def start(self, ins, outs, scratch):
        _, _, mine, first = self._parts(ins, outs, scratch)
        mine.start()
        for cp in first:
            cp.start()

    def wait(self, ins, outs, scratch):
        me, copy, mine, sent = self._parts(ins, outs, scratch)
        sibling, x_nbr, y_nbr, diagonal = _flip(me, 1), _flip(me, 4), _flip(me, 2), _flip(me, 6)
        top, bottom = pl.ds(0, self.half), pl.ds(self.half, self.half)
        arrivals = [(1, x_nbr, None, [(3, sibling, None), (5, y_nbr, top)]),
                    (2, y_nbr, None, [(4, sibling, None), (6, x_nbr, bottom)]),
                    (5, diagonal, top, [(7, sibling, top)]),
                    (6, diagonal, bottom, [(8, sibling, bottom)])]
        for k, block, part, onward in arrivals:
            copy(k, block, me, part).wait_recv()
            for k_on, to, part_on in onward:
                cp = copy(k_on, block, to, part_on)
                cp.start()
                sent.append(cp)
        other = lambda p: _flip(p, 1)
        for k, block, part in ((0, sibling, None), (3, other(x_nbr), None), (4, other(y_nbr), None),
                               (7, other(diagonal), top), (8, other(diagonal), bottom)):
            copy(k, block, me, part).wait_recv()
        for cp in sent:
            cp.wait_send()
        mine.wait()


class _Exchange:
    def __init__(self, tasks):
        self.tasks = tasks
        taken = [land for _, _, land, _, _ in tasks if land is not None]
        self.operands = [src for src, _, _, _, _ in tasks] + taken
        self.out_shape = [
            jax.ShapeDtypeStruct((N_CHIPS,) + ((2,) if by_core else ()) + (src.shape if same else src.shape[1:]), src.dtype)
            for src, _, _, same, by_core in tasks]
        self.aliases, n_taken = {}, 0
        for a, (_, _, land, _, _) in enumerate(tasks):
            if land is not None:
                self.aliases[len(tasks) + n_taken] = a
                n_taken += 1
        self.scratch = [pltpu.SemaphoreType.DMA((N_CHIPS,)), pltpu.SemaphoreType.DMA((N_CHIPS,)),
                        pltpu.SemaphoreType.DMA(())] * len(tasks)

    def _copies(self, ins, outs, scratch):
        x, y, core = _place()
        my = 2 * x + y
        for a, (_, chips, _, same, by_core) in enumerate(self.tasks):
            send_sems, recv_sems, local_sem = scratch[3 * a:3 * a + 3]
            slot = (lambda s, a=a, by_core=by_core: outs[a].at[s, core] if by_core else outs[a].at[s])
            for i, j in enumerate(chips):
                src = ins[a] if same else ins[a].at[i]
                pair = jnp.bitwise_xor(my, j)
                remote = pltpu.make_async_remote_copy(
                    src_ref=src, dst_ref=slot(my), send_sem=send_sems.at[pair], recv_sem=recv_sems.at[pair],
                    device_id=(j >> 1, j & 1, core), device_id_type=MESH)
                local = pltpu.make_async_copy(src, slot(my), local_sem)
                yield j, my, core, remote, local, slot, (send_sems, recv_sems)

    def start(self, ins, outs, scratch):
        for j, my, _, remote, local, _, _ in self._copies(ins, outs, scratch):
            pl.when(my != j)(remote.start)
            pl.when(my == j)(local.start)

    def wait(self, ins, outs, scratch):
        for j, my, core, remote, local, slot, (send_sems, recv_sems) in self._copies(ins, outs, scratch):
            pl.when(my != j)(remote.wait_send)

            @pl.when(my == j)
            def _():
                local.wait()
                for s in range(N_CHIPS):
                    if s != j:
                        pltpu.make_async_remote_copy(
                            src_ref=slot(s), dst_ref=slot(s), send_sem=send_sems.at[j ^ s], recv_sem=recv_sems.at[j ^ s],
                            device_id=(s >> 1, s & 1, core), device_id_type=MESH).wait_recv()


N_CHIPS = 4
ALL_CHIPS = tuple(range(N_CHIPS))


def _to_other_core(parts, *, name):
    n_arr = len(parts)
    hbm = pl.BlockSpec(memory_space=pl.ANY)

    def body(*refs):
        srcs, lands = refs[:n_arr], refs[n_arr:2 * n_arr]
        send_sems, recv_sems = refs[2 * n_arr:]
        me = _place()
        copies = [pltpu.make_async_remote_copy(src_ref=srcs[a].at[1 - me[2]], dst_ref=lands[a], send_sem=send_sems.at[a],
                                               recv_sem=recv_sems.at[a], device_id=_flip(me, 1), device_id_type=MESH)
                  for a in range(n_arr)]
        for cp in copies:
            cp.start()
        for cp in copies:
            cp.wait()

    return pl.pallas_call(
        body, name=name, in_specs=[hbm] * n_arr, out_specs=[hbm] * n_arr,
        out_shape=[jax.ShapeDtypeStruct(p.shape[1:], p.dtype) for p in parts],
        scratch_shapes=[pltpu.SemaphoreType.DMA((n_arr,)), pltpu.SemaphoreType.DMA((n_arr,))],
    )(*parts)


def _share_with_other_core(gathered, *, name):
    n_arr = len(gathered)
    hbm = pl.BlockSpec(memory_space=pl.ANY)

    def body(*refs):
        bufs = refs[n_arr:2 * n_arr]
        send_sems, recv_sems = refs[2 * n_arr:]
        me = _place()
        copies = []
        for a in range(n_arr):
            for j in range(N_CHIPS):
                block = bufs[a].at[j, me[2]]
                copies.append(pltpu.make_async_remote_copy(
                    src_ref=block, dst_ref=block, send_sem=send_sems.at[N_CHIPS * a + j],
                    recv_sem=recv_sems.at[N_CHIPS * a + j], device_id=_flip(me, 1), device_id_type=MESH))
        for cp in copies:
            cp.start()
        for cp in copies:
            cp.wait()

    return pl.pallas_call(
        body, name=name, in_specs=[hbm] * n_arr, out_specs=[hbm] * n_arr,
        out_shape=[jax.ShapeDtypeStruct(g.shape, g.dtype) for g in gathered],
        scratch_shapes=[pltpu.SemaphoreType.DMA((N_CHIPS * n_arr,)), pltpu.SemaphoreType.DMA((N_CHIPS * n_arr,))],
        input_output_aliases={a: a for a in range(n_arr)},
    )(*gathered)


def _pair_sum(mine, other, core, *, tr, name):
    _, n, rows, cols = mine.shape
    tr = min(tr, rows)

    def body(core_ref, a_ref, b_ref, o_ref):
        o_ref[...] = (a_ref[0].astype(F32) + b_ref[...].astype(F32)).astype(BF16)

    return pl.pallas_call(
        body, name=name,
        grid_spec=pltpu.PrefetchScalarGridSpec(
            num_scalar_prefetch=1, grid=(n, rows // tr),
            in_specs=[pl.BlockSpec((1, 1, tr, cols), lambda j, i, core_ref: (core_ref[0], j, i, 0)),
                      pl.BlockSpec((1, tr, cols), lambda j, i, core_ref: (j, i, 0))],
            out_specs=pl.BlockSpec((1, tr, cols), lambda j, i, core_ref: (j, i, 0))),
        out_shape=jax.ShapeDtypeStruct(other.shape, BF16),
        compiler_params=_params(2),
    )(core, mine, other)


def _gather_small(vec):
    def body(v_ref, out_ref, send_sems, recv_sems, local_sem):
        me = _place()

        def rows(p):
            return out_ref.at[pl.ds(pl.multiple_of(_index(p) * SMALL_ROWS, SMALL_ROWS), SMALL_ROWS), :]

        mine = pltpu.make_async_copy(v_ref, rows(me), local_sem)
        mine.start()
        sends = []
        for k in range(1, N_DEV):
            peer = _flip(me, k)
            cp = pltpu.make_async_remote_copy(src_ref=v_ref, dst_ref=rows(me), send_sem=send_sems.at[k - 1],
                                              recv_sem=recv_sems.at[k - 1], device_id=peer, device_id_type=MESH)
            cp.start()
            sends.append(cp)
        for k in range(1, N_DEV):
            peer = _flip(me, k)
            pltpu.make_async_remote_copy(src_ref=rows(peer), dst_ref=rows(peer), send_sem=send_sems.at[k - 1],
                                         recv_sem=recv_sems.at[k - 1], device_id=peer, device_id_type=MESH).wait_recv()
        for cp in sends:
            cp.wait_send()
        mine.wait()

    return pl.pallas_call(
        body, name="gather_small",
        in_specs=[pl.BlockSpec(memory_space=pltpu.VMEM)], out_specs=pl.BlockSpec(memory_space=pltpu.VMEM),
        out_shape=jax.ShapeDtypeStruct((N_DEV * SMALL_ROWS, SMALL_COLS), F32),
        scratch_shapes=[pltpu.SemaphoreType.DMA((7,)), pltpu.SemaphoreType.DMA((7,)), pltpu.SemaphoreType.DMA],
    )(vec)


def _w_in_nice(gathered):
    pieces, pos = [], 0
    for o0, width, n0 in sorted(_SEGMENTS, key=lambda seg: seg[2]):
        if n0 > pos:
            pieces.append(jnp.zeros((D_MODEL, n0 - pos), gathered.dtype))
        o = o0
        while o < o0 + width:
            d = o // SHARD_IN
            hi = min(o0 + width, (d + 1) * SHARD_IN)
            pieces.append(gathered[d][:, o - d * SHARD_IN:hi - d * SHARD_IN])
            o = hi
        pos = n0 + width
    pieces.append(jnp.zeros((D_MODEL, NP_IN - pos), gathered.dtype))
    return jnp.concatenate(pieces, axis=1)


def _w_in_blocks(chips, dw_lat, dw_rest):
    blocks = []
    for core in range(2):
        for chip in chips:
            lo = (2 * chip + core) * SHARD_IN
            runs = []
            for o0, width, n0 in _SEGMENTS:
                a, b = max(lo, o0), min(lo + SHARD_IN, o0 + width)
                if a < b:
                    n_a, n_b = n0 + a - o0, n0 + b - o0
                    runs.append(dw_lat[:, n_a:n_b] if n_b <= LAT_W else dw_rest[:, n_a - LAT_W:n_b - LAT_W])
            blocks.append(jnp.concatenate(runs, axis=1))
    return jnp.stack(blocks).reshape(2, len(chips), D_MODEL, SHARD_IN)


def _by_core(shards):
    return shards.reshape((N_CHIPS, 2) + shards.shape[1:]).swapaxes(0, 1)


EARLY_CHIPS = (1, 2)
LATE_CHIPS = (0, 3)


def _w_uq_nice(shard):
    z = jnp.zeros((Q_RANK, 32), shard.dtype)
    return jnp.concatenate([shard[:, :128], shard[:, 128:160], z, shard[:, 160:192], z], axis=1)


def _pack_small(g_pre, g_post, g_q, g_kv, b_f, extra=None):
    parts = [g_pre.reshape(-1), g_post.reshape(-1), g_q.reshape(-1), g_kv.reshape(-1), b_f.reshape(-1)]
    if extra is not None:
        parts.append(extra.reshape(-1))
    flat = jnp.concatenate(parts)
    flat = jnp.concatenate([flat, jnp.zeros((SMALL_ROWS * SMALL_COLS - flat.shape[0],), F32)])
    return flat.reshape(SMALL_ROWS, SMALL_COLS)


def _unpack_small(packed):
    flat = packed.reshape(-1)
    o = 0
    out = []
    for n in (D_MODEL, D_MODEL, Q_RANK, KV_RANK, N_HEADS):
        out.append(flat[o:o + n].reshape(1, n))
        o += n
    return out, flat[o]


def kernel(x, positions, g_pre, w_in, g_q_latent, w_uq, g_kv_latent, w_ukv, b_forget, w_out, g_post, loss_target, m_g_pre, m_w_in, m_g_q_latent, m_w_uq, m_g_kv_latent, m_w_ukv, m_b_forget, m_w_out, m_g_post, v_g_pre, v_w_in, v_g_q_latent, v_w_uq, v_g_kv_latent, v_w_ukv, v_b_forget, v_w_out, v_g_post):
    t = x.shape[1]
    tb = min(512, t)
    tm = min(256, t)
    nb = t // tb
    x2 = x.reshape(t, D_MODEL)
    target = loss_target.reshape(t, D_MODEL)
    pos_col = positions.reshape(t, 1).astype(F32)
    bf_row = jnp.concatenate([b_forget.reshape(1, N_HEADS), jnp.zeros((1, LANE - N_HEADS), F32)], axis=1)

    h, h_t, g_in = _prenorm(x2, g_pre, _AllGather(w_in[0].astype(BF16)), tm=tm)
    w_in_n = _w_in_nice(g_in)
    gather_rest = _Exchange([(w, ALL_CHIPS, None, True, True) for w in
                             (_w_uq_nice(w_uq[0].astype(BF16)), w_ukv[0].astype(BF16), w_out[0].astype(BF16))])
    core = lax.axis_index("c").astype(jnp.int32).reshape(1)
    proj, g_uq, g_ukv, g_out = _mm(h, w_in_n, name="proj_in", out_dtype=BF16, tm=2048, tn=512, tk=2048,
                                   rider=gather_rest)
    g_uq, g_ukv, g_out = _share_with_other_core([g_uq, g_ukv, g_out], name="share_weights")
    w_uq_n = g_uq.reshape(N_DEV, Q_RANK, 256).transpose(1, 0, 2).reshape(Q_RANK, N_HEADS * 256)
    w_ukv_n = g_ukv.reshape(N_DEV, KV_RANK, 256).transpose(1, 0, 2).reshape(KV_RANK, N_HEADS * 256)
    w_out_n = g_out.reshape(D_MODEL, D_MODEL)
    flog = _mm(h, w_in_n[:, FL0:FL0 + LANE], name="proj_flog", out_dtype=F32, tm=1024, tn=LANE, tk=2048)
    qn, kvn, kr, cos_t, sin_t, c, qn_t, kvn_t = _mid_fwd(proj, flog, g_q_latent, g_kv_latent, bf_row, pos_col, tm=tm)
    q_full = _q_up_rope(qn, w_uq_n, cos_t, sin_t, tm=min(1024, t))
    kv = _mm(kvn, w_ukv_n, name="kv_up", out_dtype=BF16, tm=1024, tn=512, tk=KV_RANK)
    c_heads = c[:, :N_HEADS].T
    c_col = c_heads.reshape(N_HEADS, t, 1)
    c_row4 = c_heads.reshape(N_HEADS, nb, 1, tb)
    o_all, og_all, og_t, lse4_mla = _attn_fwd(False, (q_full, kv, kr, proj), t=t, tb=tb, name="mla_fwd")
    o_all, og_all, og_t, lse4_fox = _attn_fwd(True, (proj, c_col, c_row4, o_all, og_all, og_t), t=t, tb=tb,
                                              name="fox_fwd")
    dy, d_o_post, dg_post, loss_part = _out_norm_loss(og_all, w_out_n, x2, target, g_post, tm=min(512, t))

    dw_out = _mm(og_t, d_o_post, name="dw_out", out_dtype=BF16, tm=1024, tn=1024, tk=1024)
    p_out = _by_core(dw_out.reshape(N_DEV, D_MODEL // N_DEV, D_MODEL))
    (o_out,) = _to_other_core([p_out], name="dw_out_to_core")
    s_out = _pair_sum(p_out, o_out, core, tr=256, name="dw_out_pair_sum")
    d_attn, dproj, delta = _dog_gate(d_o_post, w_out_n, o_all, proj, tm=min(1024, t))
    delta4 = delta[:, :2 * N_HEADS].T.reshape(2 * N_HEADS, nb, 1, tb)
    dproj, dck, dcq, l_out = _attn_bwd(True, (proj, d_attn, lse4_fox, delta4[N_HEADS:], c_row4, c_col, dproj),
                                       t=t, tb=tb, name="fox_bwd",
                                       rider=_Exchange([(s_out, ALL_CHIPS, None, False, False)]))
    dw_in_rest = _mm(h_t, dproj, name="dw_in_rest", out_dtype=BF16, tm=2048, tn=512, tk=1024,
                     b_cols=(LAT_W, NP_IN - LAT_W))
    p_in = _w_in_blocks(EARLY_CHIPS, None, dw_in_rest)
    (o_in,) = _to_other_core([p_in], name="dw_in_early_to_core")
    s_in = _pair_sum(p_in, o_in, core, tr=256, name="dw_in_early_pair_sum")
    dq_full, dkv, dkr, l_in = _attn_bwd(False, (q_full, kv, kr, d_attn, lse4_mla, delta4[:N_HEADS]),
                                        t=t, tb=tb, name="mla_bwd",
                                        rider=_Exchange([(s_in, EARLY_CHIPS, None, False, False)]))
    dc_heads = dck.reshape(N_HEADS, t) + dcq.reshape(N_HEADS, t)
    dck_rows = jnp.concatenate([dc_heads.T, jnp.zeros((t, LANE - N_HEADS), F32)], axis=1)
    dq2, dkr_raw, dfl, dbf = _mid_bwd(dq_full, dkr, cos_t, sin_t, dck_rows, flog, bf_row, tm=tm)
    dqn = _mm(dq2, w_uq_n, name="d_qn", nt=True, out_dtype=F32, tm=1024, tn=Q_RANK, tk=2048)
    dkvn = _mm(dkv, w_ukv_n, name="d_kvn", nt=True, out_dtype=F32, tm=1024, tn=KV_RANK, tk=2048)
    dw_uq = _mm(qn_t, dq2, name="dw_uq", out_dtype=BF16, tm=Q_RANK, tn=1024, tk=1024)
    dw_ukv = _mm(kvn_t, dkv, name="dw_ukv", out_dtype=BF16, tm=KV_RANK, tn=1024, tk=1024)
    dproj, dg_q, dg_kv = _norm_bwd(proj, dqn, dkvn, g_q_latent, g_kv_latent, dkr_raw, dfl, dproj, tm=tm)
    dw_uq_h = dw_uq.reshape(Q_RANK, N_HEADS, 256)
    s_uq = jnp.concatenate([dw_uq_h[:, :, :160], dw_uq_h[:, :, 192:224]], axis=2).transpose(1, 0, 2)
    s_ukv = dw_ukv.reshape(KV_RANK, N_HEADS, 256).transpose(1, 0, 2)
    up_parts = [_by_core(s_uq), _by_core(s_ukv)]
    up_other = _to_other_core(up_parts, name="dw_up_to_core")
    up_sums = [_pair_sum(p, o_, core, tr=256, name=f"dw_up_pair_sum_{i}") for i, (p, o_) in enumerate(zip(up_parts, up_other))]
    dw_in_lat, l_uq, l_ukv = _mm(h_t, dproj, name="dw_in_lat", out_dtype=BF16, tm=1024, tn=LAT_W, tk=1024,
                                 b_cols=(0, LAT_W), rider=_Exchange([(s, ALL_CHIPS, None, False, False) for s in up_sums]))
    p_late = _w_in_blocks(LATE_CHIPS, dw_in_lat, dw_in_rest)
    (o_late,) = _to_other_core([p_late], name="dw_in_late_to_core")
    s_late = _pair_sum(p_late, o_late, core, tr=256, name="dw_in_late_pair_sum")
    dh, l_in = _mm(dproj, w_in_n, name="d_h", nt=True, out_dtype=F32, tm=2048, tn=1024, tk=NP_IN // 4,
                   rider=_Exchange([(s_late, LATE_CHIPS, l_in, False, False)]))
    grad_x, dg_pre = _prenorm_bwd(dh, x2, g_pre, dy, tm=tm)

    small = _gather_small(_pack_small(dg_pre, dg_post, dg_q, dg_kv, dbf[:, :N_HEADS], loss_part))

    res_in = _adamw(l_in, w_in[0], m_w_in[0], v_w_in[0], tr=256, name="adamw_w_in")
    res_uq = _adamw(l_uq, w_uq[0], m_w_uq[0], v_w_uq[0], tr=256, name="adamw_w_uq")
    res_ukv = _adamw(l_ukv, w_ukv[0], m_w_ukv[0], v_w_ukv[0], tr=256, name="adamw_w_ukv")
    res_out = _adamw(l_out, w_out[0], m_w_out[0], v_w_out[0], tr=128, name="adamw_w_out")
    res_small = _adamw_small(
        small,
        _pack_small(g_pre, g_post, g_q_latent, g_kv_latent, b_forget),
        _pack_small(m_g_pre, m_g_post, m_g_q_latent, m_g_kv_latent, m_b_forget),
        _pack_small(v_g_pre, v_g_post, v_g_q_latent, v_g_kv_latent, v_b_forget))
    small_out = [_unpack_small(r) for r in res_small]
    loss = small_out[0][1]

    def leaves(kind):
        (s_pre, s_post, s_q, s_kv, s_bf), _ = small_out[kind]
        return [s_pre, res_in[kind][None], s_q, res_uq[kind][None], s_kv, res_ukv[kind][None], s_bf,
                res_out[kind][None], s_post]

    return (loss, grad_x.reshape(x.shape), *leaves(0), *leaves(1), *leaves(2), *leaves(3))
```

```python
import functools

import numpy as np
import jax
import jax.numpy as jnp
from jax import lax
from jax.experimental import pallas as pl
from jax.experimental.pallas import tpu as pltpu

F32 = jnp.float32
BF16 = jnp.bfloat16
MESH = pl.DeviceIdType.MESH

N_DEV = 8
D_MODEL = 2048
N_HEADS = 8
HEAD = 128
Q_RANK = 768
KV_RANK = 512
ROPE = 64
D_IN = 6472
SHARD_IN = D_IN // N_DEV
NORM_EPS = 1e-6
ROPE_THETA = 10000.0
MLA_SCALE = (HEAD + ROPE) ** -0.5
FOX_SCALE = HEAD ** -0.5

QL0, KR0, FL0, KVL0, GM0, GF0, FQ0, FK0, FV0, NP_IN = 0, 768, 896, 1024, 1536, 2560, 3584, 4608, 5632, 6656
LAT_W = GM0
LANE = 128
_SEGMENTS = ((0, 768, QL0), (768, 512, KVL0), (1280, 32, KR0), (1312, 32, KR0 + 64), (1344, 1024, GM0),
             (2368, 3072, FQ0), (5440, 8, FL0), (5448, 1024, GF0))
LOG2E = 1.4426950408889634

ADAM_LR = 0.001
ADAM_B1 = 0.9
ADAM_B2 = 0.999
ADAM_EPS = 1e-08
ADAM_WD = 0.01
ADAM_STEP = 10

VMEM_LIMIT_BYTES = 56 * 1024 * 1024
SMALL_ROWS, SMALL_COLS = 8, 768


def _params(n_grid=0):
    return pltpu.CompilerParams(vmem_limit_bytes=VMEM_LIMIT_BYTES,
                                dimension_semantics=("arbitrary",) * n_grid if n_grid else None)


def _sigmoid(z):
    return 1.0 / (1.0 + jnp.exp(-z))


def _split3(v):
    a = v.astype(BF16)
    r = v - a.astype(F32)
    b = r.astype(BF16)
    c = (r - b.astype(F32)).astype(BF16)
    return a, b, c


def _mm(a, b, *, name, nt=False, out_dtype=F32, tm=1024, tn=512, tk=2048, b_cols=None, rider=None):
    m, k_dim = a.shape
    n = b.shape[0] if nt else b.shape[1]
    col0 = 0
    if b_cols is not None:
        assert not nt
        col0, n = b_cols
    assert (b.shape[1] if nt else b.shape[0]) == k_dim
    tm, tn, tk = min(tm, m), min(tn, n), min(tk, k_dim)
    assert m % tm == 0 and n % tn == 0 and k_dim % tk == 0 and col0 % tn == 0, (name, a.shape, b.shape)
    nk = k_dim // tk
    j0 = col0 // tn
    grid = (m // tm, n // tn, nk)
    dims = (((1,), (1 if nt else 0,)), ((), ()))
    n_rin = len(rider.operands) if rider else 0
    n_rout = len(rider.out_shape) if rider else 0

    def body(*refs):
        a_ref, b_ref = refs[:2]
        o_ref = refs[2 + n_rin]
        acc_ref = refs[3 + n_rin + n_rout]
        i, j, k = pl.program_id(0), pl.program_id(1), pl.program_id(2)
        if rider:
            rider_refs = (refs[2:2 + n_rin], refs[3 + n_rin:3 + n_rin + n_rout], refs[4 + n_rin + n_rout:])

            @pl.when(jnp.logical_and(i == 0, jnp.logical_and(j == 0, k == 0)))
            def _():
                rider.start(*rider_refs)

        @pl.when(k == 0)
        def _():
            acc_ref[...] = jnp.zeros_like(acc_ref)

        acc_ref[...] += lax.dot_general(a_ref[...], b_ref[...], dims, preferred_element_type=F32)

        @pl.when(k == nk - 1)
        def _():
            o_ref[...] = acc_ref[...].astype(o_ref.dtype)

        if rider:
            @pl.when(jnp.logical_and(i == grid[0] - 1, jnp.logical_and(j == grid[1] - 1, k == nk - 1)))
            def _():
                rider.wait(*rider_refs)

    b_spec = (pl.BlockSpec((tn, tk), lambda i, j, k: (j, k)) if nt
              else pl.BlockSpec((tk, tn), lambda i, j, k: (k, j0 + j)))
    a_spec = pl.BlockSpec((tm, tk), lambda i, j, k: (i, k))
    any_spec = pl.BlockSpec(memory_space=pl.ANY)
    out = pl.pallas_call(
        body, name=name, grid=grid,
        in_specs=[a_spec, b_spec] + [any_spec] * n_rin,
        out_specs=[pl.BlockSpec((tm, tn), lambda i, j, k: (i, j))] + [any_spec] * n_rout,
        out_shape=[jax.ShapeDtypeStruct((m, n), out_dtype)] + (list(rider.out_shape) if rider else []),
        scratch_shapes=[pltpu.VMEM((tm, tn), F32)] + (list(rider.scratch) if rider else []),
        input_output_aliases={2 + i_in: 1 + i_out for i_in, i_out in rider.aliases.items()} if rider else {},
        compiler_params=_params(3),
    )(a, b, *(rider.operands if rider else ()))
    return out if rider else out[0]


def _prenorm(x, g, rider, *, tm):
    t = x.shape[0]
    n_steps = t // tm
    n_rin, n_rout = len(rider.operands), len(rider.out_shape)

    def body(*refs):
        x_ref, g_ref = refs[:2]
        h_ref, ht_ref = refs[2 + n_rin:4 + n_rin]
        rider_refs = (refs[2:2 + n_rin], refs[4 + n_rin:4 + n_rin + n_rout], refs[4 + n_rin + n_rout:])
        i = pl.program_id(0)
        pl.when(i == 0)(lambda: rider.start(*rider_refs))
        xv = x_ref[...]
        r = lax.rsqrt(jnp.mean(xv * xv, axis=-1, keepdims=True) + NORM_EPS)
        h = xv * r * g_ref[...]
        h_ref[...] = h.astype(BF16)
        ht_ref[...] = h.T.astype(BF16)
        pl.when(i == n_steps - 1)(lambda: rider.wait(*rider_refs))

    any_spec = pl.BlockSpec(memory_space=pl.ANY)
    return pl.pallas_call(
        body, name="prenorm", grid=(n_steps,),
        in_specs=[pl.BlockSpec((tm, D_MODEL), lambda i: (i, 0)), pl.BlockSpec((1, D_MODEL), lambda i: (0, 0))]
        + [any_spec] * n_rin,
        out_specs=[pl.BlockSpec((tm, D_MODEL), lambda i: (i, 0)), pl.BlockSpec((D_MODEL, tm), lambda i: (0, i))]
        + [any_spec] * n_rout,
        out_shape=[jax.ShapeDtypeStruct((t, D_MODEL), BF16), jax.ShapeDtypeStruct((D_MODEL, t), BF16)]
        + list(rider.out_shape),
        scratch_shapes=list(rider.scratch),
        compiler_params=_params(1),
    )(x, g, *rider.operands)


def _rope_rows():
    inv = (np.float32(ROPE_THETA) ** (-np.arange(0, ROPE, 2, dtype=np.float32) / np.float32(ROPE))).astype(np.float32)
    invf = np.zeros((1, LANE), np.float32)
    sgn = np.zeros((1, LANE), np.float32)
    invf[0, 0:32] = inv
    invf[0, 64:96] = inv
    sgn[0, 0:32] = -1.0
    sgn[0, 64:96] = 1.0
    return jnp.asarray(invf), jnp.asarray(sgn)


def _rot(v, cos_t, sin_t):
    return v * cos_t + pltpu.roll(v, 64, 1) * sin_t


def _rot_bwd(dv, cos_t, sin_t):
    return dv * cos_t + pltpu.roll(dv * sin_t, 64, 1)


def _mid_fwd(proj, flog, g_q, g_kv, bf_row, pos_col, *, tm):
    t = proj.shape[0]
    invf, sgn = _rope_rows()

    def body(p_ref, fl_ref, gq_ref, gkv_ref, bf_ref, pos_ref, invf_ref, sgn_ref,
             qn_ref, kvn_ref, kr_ref, cos_ref, sin_ref, c_ref, qnt_ref, kvnt_ref, carry_ref):
        i = pl.program_id(0)

        @pl.when(i == 0)
        def _():
            carry_ref[...] = jnp.zeros_like(carry_ref)

        ql = p_ref[:, QL0:QL0 + Q_RANK].astype(F32)
        r = lax.rsqrt(jnp.mean(ql * ql, axis=-1, keepdims=True) + NORM_EPS)
        qn = ql * r * gq_ref[...]
        qn_ref[...] = qn.astype(BF16)
        qnt_ref[...] = qn.T.astype(BF16)
        kvl = p_ref[:, KVL0:KVL0 + KV_RANK].astype(F32)
        r = lax.rsqrt(jnp.mean(kvl * kvl, axis=-1, keepdims=True) + NORM_EPS)
        kvn = kvl * r * gkv_ref[...]
        kvn_ref[...] = kvn.astype(BF16)
        kvnt_ref[...] = kvn.T.astype(BF16)

        ang = pos_ref[...] * invf_ref[...]
        cos_t = jnp.cos(ang)
        sin_t = jnp.sin(ang) * sgn_ref[...]
        cos_ref[...] = cos_t
        sin_ref[...] = sin_t
        kr_ref[...] = _rot(p_ref[:, KR0:KR0 + LANE].astype(F32), cos_t, sin_t).astype(BF16)

        z = fl_ref[...] + bf_ref[...]
        logf = jnp.minimum(z, 0.0) - jnp.log(1.0 + jnp.exp(-jnp.abs(z)))
        row = lax.broadcasted_iota(jnp.int32, (tm, tm), 0)
        col = lax.broadcasted_iota(jnp.int32, (tm, tm), 1)
        tri = (col <= row).astype(BF16)
        acc = carry_ref[0:1, :]
        for part in _split3(logf):
            acc = acc + jnp.dot(tri, part, preferred_element_type=F32)
        c_ref[...] = acc * (1.0 / FOX_SCALE)
        carry_ref[0:1, :] = carry_ref[0:1, :] + jnp.sum(logf, axis=0, keepdims=True)

    row_spec = lambda w: pl.BlockSpec((tm, w), lambda i: (i, 0))
    vec_spec = lambda w: pl.BlockSpec((1, w), lambda i: (0, 0))
    return pl.pallas_call(
        body, name="mid_fwd", grid=(t // tm,),
        in_specs=[row_spec(LAT_W), row_spec(LANE), vec_spec(Q_RANK), vec_spec(KV_RANK), vec_spec(LANE),
                  pl.BlockSpec((tm, 1), lambda i: (i, 0)), vec_spec(LANE), vec_spec(LANE)],
        out_specs=[row_spec(Q_RANK), row_spec(KV_RANK), row_spec(LANE), row_spec(LANE), row_spec(LANE), row_spec(LANE),
                   pl.BlockSpec((Q_RANK, tm), lambda i: (0, i)), pl.BlockSpec((KV_RANK, tm), lambda i: (0, i))],
        out_shape=[jax.ShapeDtypeStruct((t, Q_RANK), BF16), jax.ShapeDtypeStruct((t, KV_RANK), BF16),
                   jax.ShapeDtypeStruct((t, LANE), BF16), jax.ShapeDtypeStruct((t, LANE), F32),
                   jax.ShapeDtypeStruct((t, LANE), F32), jax.ShapeDtypeStruct((t, LANE), F32),
                   jax.ShapeDtypeStruct((Q_RANK, t), BF16), jax.ShapeDtypeStruct((KV_RANK, t), BF16)],
        scratch_shapes=[pltpu.VMEM((8, LANE), F32)],
        compiler_params=_params(1),
    )(proj, flog, g_q, g_kv, bf_row, pos_col, invf, sgn)


def _q_up_rope(qn, w_uq_n, cos_t, sin_t, *, tm):
    t = qn.shape[0]
    tn = 2 * 256

    def body(a_ref, b_ref, cos_ref, sin_ref, o_ref):
        q = jnp.dot(a_ref[...], b_ref[...], preferred_element_type=F32)
        c, s = cos_ref[...], sin_ref[...]
        for u in range(tn // 256):
            o_ref[:, 256 * u:256 * u + 128] = q[:, 256 * u:256 * u + 128].astype(BF16)
            o_ref[:, 256 * u + 128:256 * u + 256] = _rot(q[:, 256 * u + 128:256 * u + 256], c, s).astype(BF16)

    return pl.pallas_call(
        body, name="q_up_rope", grid=(t // tm, N_HEADS * 256 // tn),
        in_specs=[pl.BlockSpec((tm, Q_RANK), lambda i, j: (i, 0)), pl.BlockSpec((Q_RANK, tn), lambda i, j: (0, j)),
                  pl.BlockSpec((tm, LANE), lambda i, j: (i, 0)), pl.BlockSpec((tm, LANE), lambda i, j: (i, 0))],
        out_specs=pl.BlockSpec((tm, tn), lambda i, j: (i, j)),
        out_shape=jax.ShapeDtypeStruct((t, N_HEADS * 256), BF16),
        compiler_params=_params(2),
    )(qn, w_uq_n, cos_t, sin_t)


def _attn_fwd(fox, operands, *, t, tb, name):
    nb = t // tb
    scale = FOX_SCALE if fox else MLA_SCALE
    exp2_scale = scale * LOG2E
    pair = 2 * HEAD
    pair0 = N_HEADS // 2 if fox else 0
    q_w = HEAD if fox else 2 * HEAD
    nt_dims = (((1,), (1,)), ((), ()))
    tn_dims = (((0,), (0,)), ((), ()))

    def body(*refs):
        if fox:
            (q_ref, k_ref, v_ref, gate_ref, cq_ref, ck_ref, _, _, _,
             o_ref, og_ref, ogt_ref, lse_ref, m_s, l_s, acc_s) = refs
        else:
            q_ref, kv_ref, kr_ref, gate_ref, o_ref, og_ref, ogt_ref, lse_ref, m_s, l_s, acc_s = refs
        qi = pl.program_id(1)
        m_s[...] = jnp.full_like(m_s, -jnp.inf)
        l_s[...] = jnp.zeros_like(l_s)
        acc_s[...] = jnp.zeros_like(acc_s)

        def tile(kc, n_keys, qs, on_diagonal):
            off = pl.multiple_of(kc * tb, tb)
            n_q = qs.stop - qs.start
            scores = []
            for u in range(2):
                q = q_ref[qs, q_w * u:q_w * (u + 1)]
                if fox:
                    kk = k_ref[pl.ds(off, n_keys), HEAD * u:HEAD * (u + 1)]
                else:
                    kk = jnp.concatenate([kv_ref[pl.ds(off, n_keys), pair * u:pair * u + HEAD],
                                          kr_ref[pl.ds(off, n_keys), :]], axis=1)
                s = lax.dot_general(kk, q, nt_dims, preferred_element_type=F32)
                if fox:
                    s = s + cq_ref[u, 0, :, qs] - ck_ref[u, pl.ds(off, n_keys), :]
                if on_diagonal:
                    row = lax.broadcasted_iota(jnp.int32, (n_keys, n_q), 0)
                    col = lax.broadcasted_iota(jnp.int32, (n_keys, n_q), 1) + qs.start
                    s = jnp.where(row <= col, s, -jnp.inf)
                scores.append(s)
            for u in range(2):
                s = scores[u]
                m_prev = m_s[u, :, qs]
                m_new = jnp.maximum(m_prev, jnp.max(s, axis=0, keepdims=True))
                alpha = jnp.exp2((m_prev - m_new) * exp2_scale)
                p = jnp.exp2((s - m_new) * exp2_scale)
                l_s[u, :, qs] = alpha * l_s[u, :, qs] + jnp.sum(p, axis=0, keepdims=True)
                if fox:
                    vv = v_ref[pl.ds(off, n_keys), HEAD * u:HEAD * (u + 1)]
                else:
                    vv = kv_ref[pl.ds(off, n_keys), pair * u + HEAD:pair * (u + 1)]
                acc_s[u, :, qs] = alpha * acc_s[u, :, qs] + lax.dot_general(vv, p.astype(BF16), tn_dims,
                                                                            preferred_element_type=F32)
                m_s[u, :, qs] = m_new

        def loop_body(kc, carry):
            tile(kc, tb, slice(0, tb), False)
            return carry

        lax.fori_loop(0, qi, loop_body, 0)
        half = tb // 2
        tile(qi, half, slice(0, half), True)
        tile(qi, tb, slice(half, tb), True)
        for u in range(2):
            cols = slice(HEAD * u, HEAD * (u + 1))
            o_t = acc_s[u] / l_s[u]
            o = o_t.T
            o_ref[:, cols] = o
            g = gate_ref[:, cols].astype(F32)
            silu = g * _sigmoid(g)
            og_ref[:, cols] = (o * silu).astype(BF16)
            ogt_ref[cols, :] = (o_t * silu.T).astype(BF16)
            lse_ref[u, 0] = m_s[u] * scale + jnp.log(l_s[u])

    any_spec = pl.BlockSpec(memory_space=pl.ANY)
    row_stat = pl.BlockSpec((2, 1, 1, tb), lambda g, i: (g, i, 0, 0))
    if fox:
        proj, c_col, c_row4, o_all, og_all, ogt_all = operands
        ins = [proj, proj, proj, proj, c_row4, c_col, o_all, og_all, ogt_all]
        in_specs = [pl.BlockSpec((tb, pair), lambda g, i: (i, FQ0 // pair + g)),
                    pl.BlockSpec((t, pair), lambda g, i: (0, FK0 // pair + g)),
                    pl.BlockSpec((t, pair), lambda g, i: (0, FV0 // pair + g)),
                    pl.BlockSpec((tb, pair), lambda g, i: (i, GF0 // pair + g)),
                    row_stat, pl.BlockSpec((2, t, 1), lambda g, i: (g, 0, 0)), any_spec, any_spec, any_spec]
        aliases = {6: 0, 7: 1, 8: 2}
    else:
        q_full, kv, kr, proj = operands
        ins = [q_full, kv, kr, proj]
        in_specs = [pl.BlockSpec((tb, 2 * pair), lambda g, i: (i, g)),
                    pl.BlockSpec((t, 2 * pair), lambda g, i: (0, g)),
                    pl.BlockSpec((t, HEAD), lambda g, i: (0, 0)),
                    pl.BlockSpec((tb, pair), lambda g, i: (i, GM0 // pair + g))]
        aliases = {}
    return pl.pallas_call(
        body, name=name, grid=(N_HEADS // 2, nb), in_specs=in_specs,
        out_specs=[pl.BlockSpec((tb, pair), lambda g, i: (i, pair0 + g)),
                   pl.BlockSpec((tb, pair), lambda g, i: (i, pair0 + g)),
                   pl.BlockSpec((pair, tb), lambda g, i: (pair0 + g, i)), row_stat],
        out_shape=[jax.ShapeDtypeStruct((t, 2 * N_HEADS * HEAD), F32), jax.ShapeDtypeStruct((t, 2 * N_HEADS * HEAD), BF16),
                   jax.ShapeDtypeStruct((2 * N_HEADS * HEAD, t), BF16), jax.ShapeDtypeStruct((N_HEADS, nb, 1, tb), F32)],
        scratch_shapes=[pltpu.VMEM((2, 1, tb), F32), pltpu.VMEM((2, 1, tb), F32), pltpu.VMEM((2, HEAD, tb), F32)],
        input_output_aliases=aliases,
        compiler_params=_params(2),
    )(*ins)


def _out_norm_loss(og, w_out_n, x, target, g, *, tm):
    t = og.shape[0]

    def body(og_ref, w_ref, x_ref, t_ref, g_ref, dy_ref, do_ref, dg_ref, loss_ref):
        i = pl.program_id(0)

        @pl.when(i == 0)
        def _():
            dg_ref[...] = jnp.zeros_like(dg_ref)
            loss_ref[...] = jnp.zeros_like(loss_ref)

        ov = jnp.dot(og_ref[...], w_ref[...], preferred_element_type=F32)
        gv = g_ref[...]
        r = lax.rsqrt(jnp.mean(ov * ov, axis=-1, keepdims=True) + NORM_EPS)
        oh = ov * r
        e = x_ref[...] + oh * gv - t_ref[...]
        loss_ref[...] += 0.5 * jnp.sum(jnp.mean(e * e, axis=-1, keepdims=True), axis=0, keepdims=True)
        dy = e * (1.0 / D_MODEL)
        dy_ref[...] = dy
        dyg = dy * gv
        do_ref[...] = (r * (dyg - oh * jnp.mean(dyg * oh, axis=-1, keepdims=True))).astype(BF16)
        dg_ref[...] += jnp.sum(dy * oh, axis=0, keepdims=True)

    row = pl.BlockSpec((tm, D_MODEL), lambda i: (i, 0))
    vec = pl.BlockSpec((1, D_MODEL), lambda i: (0, 0))
    whole_w = pl.BlockSpec((D_MODEL, D_MODEL), lambda i: (0, 0), pipeline_mode=pl.Buffered(1))
    return pl.pallas_call(
        body, name="out_norm_loss", grid=(t // tm,),
        in_specs=[row, whole_w, row, row, vec],
        out_specs=[row, row, vec, pl.BlockSpec((1, 1), lambda i: (0, 0))],
        out_shape=[jax.ShapeDtypeStruct((t, D_MODEL), F32), jax.ShapeDtypeStruct((t, D_MODEL), BF16),
                   jax.ShapeDtypeStruct((1, D_MODEL), F32), jax.ShapeDtypeStruct((1, 1), F32)],
        compiler_params=_params(1),
    )(og, w_out_n, x, target, g)


def _dog_gate(d_o_post, w_out_n, o_all, proj, *, tm):
    t = d_o_post.shape[0]
    n_group = 4
    pair = n_group * HEAD
    gate_blk = GM0 // pair
    assert GM0 % pair == 0 and GF0 == GM0 + N_HEADS * HEAD

    def body(do_ref, w_ref, o_ref, p_ref, dattn_ref, dproj_ref, delta_ref):
        j = pl.program_id(1)

        @pl.when(j == 0)
        def _():
            delta_ref[...] = jnp.zeros_like(delta_ref)

        dog = lax.dot_general(do_ref[...], w_ref[...], (((1,), (1,)), ((), ())), preferred_element_type=F32)
        g = p_ref[...].astype(F32)
        ov = o_ref[...]
        sg = _sigmoid(g)
        d_o = dog * (g * sg)
        dattn_ref[...] = d_o.astype(BF16)
        dproj_ref[...] = (dog * ov * (sg * (1.0 + g * (1.0 - sg)))).astype(BF16)
        prod = d_o * ov
        lane = lax.broadcasted_iota(jnp.int32, (tm, LANE), 1)
        delta = delta_ref[...]
        for u in range(n_group):
            part = jnp.sum(prod[:, HEAD * u:HEAD * (u + 1)], axis=-1, keepdims=True)
            delta = jnp.where(lane == n_group * j + u, part, delta)
        delta_ref[...] = delta

    return pl.pallas_call(
        body, name="dog_gate", grid=(t // tm, 2 * N_HEADS // n_group),
        in_specs=[pl.BlockSpec((tm, D_MODEL), lambda i, j: (i, 0)), pl.BlockSpec((pair, D_MODEL), lambda i, j: (j, 0)),
                  pl.BlockSpec((tm, pair), lambda i, j: (i, j)), pl.BlockSpec((tm, pair), lambda i, j: (i, gate_blk + j))],
        out_specs=[pl.BlockSpec((tm, pair), lambda i, j: (i, j)), pl.BlockSpec((tm, pair), lambda i, j: (i, gate_blk + j)),
                   pl.BlockSpec((tm, LANE), lambda i, j: (i, 0))],
        out_shape=[jax.ShapeDtypeStruct((t, 2048), BF16), jax.ShapeDtypeStruct((t, NP_IN), BF16),
                   jax.ShapeDtypeStruct((t, LANE), F32)],
        compiler_params=_params(2),
    )(d_o_post, w_out_n, o_all, proj)


def _attn_bwd(fox, operands, *, t, tb, name, rider=None):
    nb = t // tb
    n_pairs = N_HEADS // 2
    pair = 2 * HEAD
    scale = FOX_SCALE if fox else MLA_SCALE
    q_w = HEAD if fox else 2 * HEAD
    nt_dims = (((1,), (1,)), ((), ()))
    tn_dims = (((0,), (0,)), ((), ()))
    n_rin = len(rider.operands) if rider else 0
    n_rout = len(rider.out_shape) if rider else 0
    n_in, n_out, n_scr = (9, 3, 9) if fox else (6, 3, 2)

    def body(*refs):
        ends = np.cumsum([0, n_in, n_rin, n_out, n_rout, n_scr])
        in_refs, rider_in, out_refs, rider_out, scr_refs = (refs[a:b] for a, b in zip(ends[:-1], ends[1:]))
        rider_refs = (rider_in, rider_out, refs[ends[-1]:])
        if fox:
            q_ref, k_ref, v_ref, do_ref, lse_ref, dl_ref, cq_ref, ck_ref, _ = in_refs
            dproj_ref, dck_ref, dcq_ref = out_refs
            dq_acc, dk_s, dv_s, dc_s, dcq_s, stage_q, stage_k, stage_v, put_sems = scr_refs
        else:
            q_ref, kv_ref, kr_ref, do_ref, lse_ref, dl_ref = in_refs
            dq_acc, dkv_ref, dkr_ref = out_refs
            dk_s, dv_s = scr_refs
        g = pl.program_id(0)
        ki = pl.program_id(1)
        if rider:
            @pl.when(jnp.logical_and(g == 0, ki == 0))
            def _():
                rider.start(*rider_refs)

        @pl.when(ki == 0)
        def _():
            dq_acc[...] = jnp.zeros_like(dq_acc)
            if fox:
                dcq_s[...] = jnp.zeros_like(dcq_s)

        dk_s[...] = jnp.zeros_like(dk_s)
        dv_s[...] = jnp.zeros_like(dv_s)
        if fox:
            dc_s[...] = jnp.zeros_like(dc_s)
            keys = [k_ref[:, HEAD * u:HEAD * (u + 1)] for u in range(2)]
            vals = [v_ref[:, HEAD * u:HEAD * (u + 1)] for u in range(2)]
        else:
            keys = [jnp.concatenate([kv_ref[:, pair * u:pair * u + HEAD], kr_ref[...]], axis=1) for u in range(2)]
            vals = [kv_ref[:, pair * u + HEAD:pair * (u + 1)] for u in range(2)]

        def tile(qc, n_keys, qs, on_diagonal):
            n_q = qs.stop - qs.start
            q_rows = pl.ds(pl.multiple_of(qc * tb + qs.start, n_q), n_q)
            k_rows = slice(0, n_keys)
            for u in range(2):
                kk, vv = keys[u][k_rows], vals[u][k_rows]
                qq = q_ref[q_rows, q_w * u:q_w * (u + 1)]
                dd = do_ref[q_rows, HEAD * u:HEAD * (u + 1)]
                s = lax.dot_general(kk, qq, nt_dims, preferred_element_type=F32)
                if fox:
                    s = s + cq_ref[u, qc, :, qs] - ck_ref[u, k_rows, :]
                if on_diagonal:
                    row = lax.broadcasted_iota(jnp.int32, (n_keys, n_q), 0)
                    col = lax.broadcasted_iota(jnp.int32, (n_keys, n_q), 1) + qs.start
                    s = jnp.where(row <= col, s, -jnp.inf)
                p = jnp.exp2(s * (scale * LOG2E) - lse_ref[u, qc, :, qs] * LOG2E)
                dv_s[u, k_rows] += jnp.dot(p.astype(BF16), dd, preferred_element_type=F32)
                dp = lax.dot_general(vv, dd, nt_dims, preferred_element_type=F32)
                ds = p * (dp - dl_ref[u, qc, :, qs])
                if fox:
                    dc_s[u, k_rows] += jnp.sum(ds, axis=1, keepdims=True)
                    dcq_s[u, qc, :, qs] += jnp.sum(ds, axis=0, keepdims=True)
                dsb = (ds * scale).astype(BF16)
                dk_s[u, k_rows] += jnp.dot(dsb, qq, preferred_element_type=F32)
                dq_acc[q_rows, q_w * u:q_w * (u + 1)] += lax.dot_general(dsb, kk, tn_dims,
                                                                         preferred_element_type=F32)

        half = tb // 2
        tile(ki, half, slice(0, half), True)
        tile(ki, tb, slice(half, tb), True)

        def loop_body(qc, carry):
            tile(qc, tb, slice(0, tb), False)
            return carry

        lax.fori_loop(ki + 1, nb, loop_body, 0)

        def put(stage_ref, rows, seg0, sem):
            col0 = pl.multiple_of(seg0 + g * pair, pair)
            return pltpu.make_async_copy(stage_ref, dproj_ref.at[rows, pl.ds(col0, pair)], sem)

        if fox:
            rows = pl.ds(pl.multiple_of(ki * tb, tb), tb)
            block_puts = [put(stage_k, rows, FK0, put_sems.at[1]), put(stage_v, rows, FV0, put_sems.at[2])]
            pair_put = put(stage_q, pl.ds(0, t), FQ0, put_sems.at[0])

            @pl.when(jnp.logical_or(g > 0, ki > 0))
            def _():
                for cp in block_puts:
                    cp.wait()

            for u in range(2):
                stage_k[:, HEAD * u:HEAD * (u + 1)] = dk_s[u].astype(BF16)
                stage_v[:, HEAD * u:HEAD * (u + 1)] = dv_s[u].astype(BF16)
                dck_ref[u] = -dc_s[u]
            for cp in block_puts:
                cp.start()

            @pl.when(ki == nb - 1)
            def _():
                @pl.when(g > 0)
                def _():
                    pair_put.wait()

                stage_q[...] = dq_acc[...].astype(BF16)
                pair_put.start()
                dcq_ref[...] = dcq_s[...]

            @pl.when(jnp.logical_and(g == n_pairs - 1, ki == nb - 1))
            def _():
                for cp in block_puts + [pair_put]:
                    cp.wait()
        else:
            dkv_ref[...] = jnp.concatenate([dk_s[0, :, :HEAD], dv_s[0], dk_s[1, :, :HEAD], dv_s[1]], axis=1).astype(BF16)
            dkr_ref[...] = jnp.concatenate([dk_s[0, :, HEAD:], dk_s[1, :, HEAD:]], axis=1)

        if rider:
            @pl.when(jnp.logical_and(g == n_pairs - 1, ki == nb - 1))
            def _():
                rider.wait(*rider_refs)

    stat = pl.BlockSpec((2, nb, 1, tb), lambda g, i: (g, 0, 0, 0))
    aliases = {}
    if fox:
        proj, d_o, lse4, delta4, c_row4, c_col, dproj = operands
        ins = [proj, proj, proj, d_o, lse4, delta4, c_row4, c_col, dproj]
        any_spec = pl.BlockSpec(memory_space=pl.ANY)
        in_specs = [pl.BlockSpec((t, pair), lambda g, i: (0, FQ0 // pair + g)),
                    pl.BlockSpec((tb, pair), lambda g, i: (i, FK0 // pair + g)),
                    pl.BlockSpec((tb, pair), lambda g, i: (i, FV0 // pair + g)),
                    pl.BlockSpec((t, pair), lambda g, i: (0, n_pairs + g)),
                    stat, stat, stat, pl.BlockSpec((2, tb, 1), lambda g, i: (g, i, 0)), any_spec]
        aliases = {8: 0}
        out_specs = [any_spec, pl.BlockSpec((2, tb, 1), lambda g, i: (g, i, 0)), stat]
        out_shape = [jax.ShapeDtypeStruct(dproj.shape, dproj.dtype), jax.ShapeDtypeStruct((N_HEADS, t, 1), F32),
                     jax.ShapeDtypeStruct((N_HEADS, nb, 1, tb), F32)]
        scratch = [pltpu.VMEM((t, pair), F32), pltpu.VMEM((2, tb, HEAD), F32), pltpu.VMEM((2, tb, HEAD), F32),
                   pltpu.VMEM((2, tb, 1), F32), pltpu.VMEM((2, nb, 1, tb), F32),
                   pltpu.VMEM((t, pair), BF16), pltpu.VMEM((tb, pair), BF16), pltpu.VMEM((tb, pair), BF16),
                   pltpu.SemaphoreType.DMA((3,))]
    else:
        q_full, kv, kr, d_o, lse4, delta4 = operands
        ins = [q_full, kv, kr, d_o, lse4, delta4]
        in_specs = [pl.BlockSpec((t, 2 * pair), lambda g, i: (0, g)),
                    pl.BlockSpec((tb, 2 * pair), lambda g, i: (i, g)),
                    pl.BlockSpec((tb, HEAD), lambda g, i: (i, 0)),
                    pl.BlockSpec((t, pair), lambda g, i: (0, g)),
                    stat, stat]
        out_specs = [pl.BlockSpec((t, 2 * pair), lambda g, i: (0, g)), pl.BlockSpec((tb, 2 * pair), lambda g, i: (i, g)),
                     pl.BlockSpec((tb, pair), lambda g, i: (i, g))]
        out_shape = [jax.ShapeDtypeStruct((t, 2048), F32), jax.ShapeDtypeStruct((t, 2048), BF16),
                     jax.ShapeDtypeStruct((t, 1024), F32)]
        scratch = [pltpu.VMEM((2, tb, 2 * HEAD), F32), pltpu.VMEM((2, tb, HEAD), F32)]
    assert (len(ins), len(out_specs), len(scratch)) == (n_in, n_out, n_scr)
    if rider:
        any_spec = pl.BlockSpec(memory_space=pl.ANY)
        aliases = {**aliases, **{n_in + i_in: n_out + i_out for i_in, i_out in rider.aliases.items()}}
        ins = ins + list(rider.operands)
        in_specs = in_specs + [any_spec] * n_rin
        out_specs = out_specs + [any_spec] * n_rout
        out_shape = out_shape + list(rider.out_shape)
        scratch = scratch + list(rider.scratch)
    return pl.pallas_call(
        body, name=name, grid=(n_pairs, nb), in_specs=in_specs, out_specs=out_specs, out_shape=out_shape,
        scratch_shapes=scratch, input_output_aliases=aliases, compiler_params=_params(2),
    )(*ins)


def _mid_bwd(dq_full, dkr, cos_t, sin_t, dck, flog, bf_row, *, tm):
    t = dq_full.shape[0]
    n = t // tm

    def body(dq_ref, dkr_ref, cos_ref, sin_ref, dck_ref, fl_ref, bf_ref,
             dq2_ref, dkraw_ref, dfl_ref, dbf_ref, carry_ref):
        i = pl.program_id(0)

        @pl.when(i == 0)
        def _():
            carry_ref[...] = jnp.zeros_like(carry_ref)
            dbf_ref[...] = jnp.zeros_like(dbf_ref)

        c, s = cos_ref[...], sin_ref[...]
        dkr_sum = jnp.zeros((tm, LANE), F32)
        for h in range(N_HEADS):
            dq2_ref[:, 256 * h:256 * h + 128] = dq_ref[:, 256 * h:256 * h + 128].astype(BF16)
            dq2_ref[:, 256 * h + 128:256 * h + 256] = _rot_bwd(dq_ref[:, 256 * h + 128:256 * h + 256], c, s).astype(BF16)
            dkr_sum = dkr_sum + dkr_ref[:, HEAD * h:HEAD * (h + 1)]
        dkraw_ref[...] = _rot_bwd(dkr_sum, c, s).astype(BF16)

        dc = dck_ref[...]
        row = lax.broadcasted_iota(jnp.int32, (tm, tm), 0)
        col = lax.broadcasted_iota(jnp.int32, (tm, tm), 1)
        tri = (col >= row).astype(BF16)
        acc = carry_ref[0:1, :]
        for part in _split3(dc):
            acc = acc + jnp.dot(tri, part, preferred_element_type=F32)
        carry_ref[0:1, :] = carry_ref[0:1, :] + jnp.sum(dc, axis=0, keepdims=True)
        z = fl_ref[...] + bf_ref[...]
        dz = acc / (1.0 + jnp.exp(z))
        dfl_ref[...] = dz.astype(BF16)
        dbf_ref[...] += jnp.sum(dz, axis=0, keepdims=True)

    rev = lambda w: pl.BlockSpec((tm, w), lambda i: (n - 1 - i, 0))
    vec = lambda w: pl.BlockSpec((1, w), lambda i: (0, 0))
    return pl.pallas_call(
        body, name="mid_bwd", grid=(n,),
        in_specs=[rev(2048), rev(1024), rev(LANE), rev(LANE), rev(LANE), rev(LANE), vec(LANE)],
        out_specs=[rev(2048), rev(LANE), rev(LANE), vec(LANE)],
        out_shape=[jax.ShapeDtypeStruct((t, 2048), BF16), jax.ShapeDtypeStruct((t, LANE), BF16),
                   jax.ShapeDtypeStruct((t, LANE), BF16), jax.ShapeDtypeStruct((1, LANE), F32)],
        scratch_shapes=[pltpu.VMEM((8, LANE), F32)],
        compiler_params=_params(1),
    )(dq_full, dkr, cos_t, sin_t, dck, flog, bf_row)


def _norm_bwd(proj, dqn, dkvn, g_q, g_kv, dkr_raw, dfl, dproj, *, tm):
    t = proj.shape[0]
    assert (KR0, FL0, KVL0, LAT_W) == (Q_RANK, Q_RANK + LANE, Q_RANK + 2 * LANE, Q_RANK + 2 * LANE + KV_RANK)

    def body(p_ref, dqn_ref, dkvn_ref, gq_ref, gkv_ref, dkr_ref, dfl_ref, _, dproj_ref, dgq_ref, dgkv_ref):
        i = pl.program_id(0)

        @pl.when(i == 0)
        def _():
            dgq_ref[...] = jnp.zeros_like(dgq_ref)
            dgkv_ref[...] = jnp.zeros_like(dgkv_ref)

        d_lat = []
        for lo, w, dn_ref, g_ref, dg_ref in ((QL0, Q_RANK, dqn_ref, gq_ref, dgq_ref),
                                             (KVL0, KV_RANK, dkvn_ref, gkv_ref, dgkv_ref)):
            xv = p_ref[:, lo:lo + w].astype(F32)
            r = lax.rsqrt(jnp.mean(xv * xv, axis=-1, keepdims=True) + NORM_EPS)
            xh = xv * r
            dn = dn_ref[...]
            dg_ref[...] += jnp.sum(dn * xh, axis=0, keepdims=True)
            dxh = dn * g_ref[...]
            d_lat.append((r * (dxh - xh * jnp.mean(dxh * xh, axis=-1, keepdims=True))).astype(BF16))
        dproj_ref[...] = jnp.concatenate([d_lat[0], dkr_ref[...], dfl_ref[...], d_lat[1]], axis=1)

    row = lambda w: pl.BlockSpec((tm, w), lambda i: (i, 0))
    vec = lambda w: pl.BlockSpec((1, w), lambda i: (0, 0))
    return pl.pallas_call(
        body, name="norm_bwd", grid=(t // tm,),
        in_specs=[row(LAT_W), row(Q_RANK), row(KV_RANK), vec(Q_RANK), vec(KV_RANK), row(LANE), row(LANE),
                  pl.BlockSpec(memory_space=pl.ANY)],
        out_specs=[row(LAT_W), vec(Q_RANK), vec(KV_RANK)],
        out_shape=[jax.ShapeDtypeStruct(dproj.shape, dproj.dtype),
                   jax.ShapeDtypeStruct((1, Q_RANK), F32), jax.ShapeDtypeStruct((1, KV_RANK), F32)],
        input_output_aliases={7: 0},
        compiler_params=_params(1),
    )(proj, dqn, dkvn, g_q, g_kv, dkr_raw, dfl, dproj)


def _prenorm_bwd(dh, x, g, dy, *, tm):
    t = x.shape[0]

    def body(dh_ref, x_ref, g_ref, dy_ref, gx_ref, dg_ref):
        i = pl.program_id(0)

        @pl.when(i == 0)
        def _():
            dg_ref[...] = jnp.zeros_like(dg_ref)

        xv = x_ref[...]
        r = lax.rsqrt(jnp.mean(xv * xv, axis=-1, keepdims=True) + NORM_EPS)
        xh = xv * r
        dn = dh_ref[...]
        dg_ref[...] += jnp.sum(dn * xh, axis=0, keepdims=True)
        dxh = dn * g_ref[...]
        gx_ref[...] = dy_ref[...] + r * (dxh - xh * jnp.mean(dxh * xh, axis=-1, keepdims=True))

    row = pl.BlockSpec((tm, D_MODEL), lambda i: (i, 0))
    vec = pl.BlockSpec((1, D_MODEL), lambda i: (0, 0))
    return pl.pallas_call(
        body, name="prenorm_bwd", grid=(t // tm,),
        in_specs=[row, row, vec, row], out_specs=[row, vec],
        out_shape=[jax.ShapeDtypeStruct((t, D_MODEL), F32), jax.ShapeDtypeStruct((1, D_MODEL), F32)],
        compiler_params=_params(1),
    )(dh, x, g, dy)


def _adam_math(w, g, m, v):
    m = ADAM_B1 * m + (1.0 - ADAM_B1) * g
    v = ADAM_B2 * v + (1.0 - ADAM_B2) * (g * g)
    m_hat = m / (1.0 - ADAM_B1 ** ADAM_STEP)
    v_hat = v / (1.0 - ADAM_B2 ** ADAM_STEP)
    delta = -ADAM_LR * (m_hat / (jnp.sqrt(v_hat) + ADAM_EPS) + ADAM_WD * w)
    return delta, m, v


def _adamw(land, w, m, v, *, tr, name):
    rows, cols = w.shape

    def body(l_ref, w_ref, m_ref, v_ref, g_ref, d_ref, nm_ref, nv_ref):
        g = l_ref[0].astype(F32)
        for s in range(1, N_CHIPS):
            g = g + l_ref[s].astype(F32)
        g_ref[...] = g
        d_ref[...], nm_ref[...], nv_ref[...] = _adam_math(w_ref[...], g, m_ref[...], v_ref[...])

    blk = pl.BlockSpec((tr, cols), lambda i: (i, 0))
    return pl.pallas_call(
        body, name=name, grid=(rows // tr,),
        in_specs=[pl.BlockSpec((N_CHIPS, tr, cols), lambda i: (0, i, 0)), blk, blk, blk],
        out_specs=[blk, blk, blk, blk],
        out_shape=[jax.ShapeDtypeStruct((rows, cols), F32)] * 4,
        compiler_params=_params(1),
    )(land, w, m, v)


def _adamw_small(gathered, w, m, v):
    def body(a_ref, w_ref, m_ref, v_ref, g_ref, d_ref, nm_ref, nv_ref):
        g = a_ref[0:SMALL_ROWS, :]
        for s in range(1, N_DEV):
            g = g + a_ref[SMALL_ROWS * s:SMALL_ROWS * (s + 1), :]
        g_ref[...] = g
        d_ref[...], nm_ref[...], nv_ref[...] = _adam_math(w_ref[...], g, m_ref[...], v_ref[...])

    return pl.pallas_call(
        body, name="adamw_small",
        out_shape=[jax.ShapeDtypeStruct((SMALL_ROWS, SMALL_COLS), F32)] * 4,
        compiler_params=_params(),
    )(gathered, w, m, v)


def _place():
    x, y, c = lax.axis_index("x"), lax.axis_index("y"), lax.axis_index("c")
    return x, y, c


def _flip(p, k):
    x, y, c = p
    return (1 - x if k & 4 else x, 1 - y if k & 2 else y, 1 - c if k & 1 else c)


def _index(p):
    return 4 * p[0] + 2 * p[1] + p[2]


class _AllGather:
    def __init__(self, shard):
        assert shard.shape[0] % 32 == 0
        self.half = shard.shape[0] // 2
        self.operands = [shard]
        self.out_shape = [jax.ShapeDtypeStruct((N_DEV,) + shard.shape, shard.dtype)]
        self.aliases = {}
        self.scratch = [pltpu.SemaphoreType.DMA((9,)), pltpu.SemaphoreType.DMA((9,)), pltpu.SemaphoreType.DMA(())]

    def _parts(self, ins, outs, scratch):
        (in_ref,), (out_ref,), (send_sems, recv_sems, local_sem) = ins, outs, scratch
        me = _place()

        def copy(k, block, to, part=None, src=None):
            dst = out_ref.at[_index(block)] if part is None else out_ref.at[_index(block), part]
            return pltpu.make_async_remote_copy(
                src_ref=dst if src is None else src, dst_ref=dst, send_sem=send_sems.at[k], recv_sem=recv_sems.at[k],
                device_id=to, device_id_type=MESH)

        mine = pltpu.make_async_copy(in_ref, out_ref.at[_index(me)], local_sem)
        first = [copy(0, me, _flip(me, 1), src=in_ref), copy(1, me, _flip(me, 4), src=in_ref),
                 copy(2, me, _flip(me, 2), src=in_ref)]
        return me, copy, mine, first

    def start(self, ins, outs, scratch):
        _, _, mine, first = self._parts(ins, outs, scratch)
        mine.start()
        for cp in first:
            cp.start()

    def wait(self, ins, outs, scratch):
        me, copy, mine, sent = self._parts(ins, outs, scratch)
        sibling, x_nbr, y_nbr, diagonal = _flip(me, 1), _flip(me, 4), _flip(me, 2), _flip(me, 6)
        top, bottom = pl.ds(0, self.half), pl.ds(self.half, self.half)
        arrivals = [(1, x_nbr, None, [(3, sibling, None), (5, y_nbr, top)]),
                    (2, y_nbr, None, [(4, sibling, None), (6, x_nbr, bottom)]),
                    (5, diagonal, top, [(7, sibling, top)]),
                    (6, diagonal, bottom, [(8, sibling, bottom)])]
        for k, block, part, onward in arrivals:
            copy(k, block, me, part).wait_recv()
            for k_on, to, part_on in onward:
                cp = copy(k_on, block, to, part_on)
                cp.start()
                sent.append(cp)
        other = lambda p: _flip(p, 1)
        for k, block, part in ((0, sibling, None), (3, other(x_nbr), None), (4, other(y_nbr), None),
                               (7, other(diagonal), top), (8, other(diagonal), bottom)):
            copy(k, block, me, part).wait_recv()
        for cp in sent:
            cp.wait_send()
        mine.wait()


class _Exchange:
    def __init__(self, tasks):
        self.tasks = tasks
        taken = [land for _, _, land, _, _ in tasks if land is not None]
        self.operands = [src for src, _, _, _, _ in tasks] + taken
        self.out_shape = [
            jax.ShapeDtypeStruct((N_CHIPS,) + ((2,) if by_core else ()) + (src.shape if same else src.shape[1:]), src.dtype)
            for src, _, _, same, by_core in tasks]
        self.aliases, n_taken = {}, 0
        for a, (_, _, land, _, _) in enumerate(tasks):
            if land is not None:
                self.aliases[len(tasks) + n_taken] = a
                n_taken += 1
        self.scratch = [pltpu.SemaphoreType.DMA((N_CHIPS,)), pltpu.SemaphoreType.DMA((N_CHIPS,)),
                        pltpu.SemaphoreType.DMA(())] * len(tasks)

    def _copies(self, ins, outs, scratch):
        x, y, core = _place()
        my = 2 * x + y
        for a, (_, chips, _, same, by_core) in enumerate(self.tasks):
            send_sems, recv_sems, local_sem = scratch[3 * a:3 * a + 3]
            slot = (lambda s, a=a, by_core=by_core: outs[a].at[s, core] if by_core else outs[a].at[s])
            for i, j in enumerate(chips):
                src = ins[a] if same else ins[a].at[i]
                pair = jnp.bitwise_xor(my, j)
                remote = pltpu.make_async_remote_copy(
                    src_ref=src, dst_ref=slot(my), send_sem=send_sems.at[pair], recv_sem=recv_sems.at[pair],
                    device_id=(j >> 1, j & 1, core), device_id_type=MESH)
                local = pltpu.make_async_copy(src, slot(my), local_sem)
                yield j, my, core, remote, local, slot, (send_sems, recv_sems)

    def start(self, ins, outs, scratch):
        for j, my, _, remote, local, _, _ in self._copies(ins, outs, scratch):
            pl.when(my != j)(remote.start)
            pl.when(my == j)(local.start)

    def wait(self, ins, outs, scratch):
        for j, my, core, remote, local, slot, (send_sems, recv_sems) in self._copies(ins, outs, scratch):
            pl.when(my != j)(remote.wait_send)

            @pl.when(my == j)
            def _():
                local.wait()
                for s in range(N_CHIPS):
                    if s != j:
                        pltpu.make_async_remote_copy(
                            src_ref=slot(s), dst_ref=slot(s), send_sem=send_sems.at[j ^ s], recv_sem=recv_sems.at[j ^ s],
                            device_id=(s >> 1, s & 1, core), device_id_type=MESH).wait_recv()


N_CHIPS = 4
ALL_CHIPS = tuple(range(N_CHIPS))


def _to_other_core(parts, *, name):
    n_arr = len(parts)
    hbm = pl.BlockSpec(memory_space=pl.ANY)

    def body(*refs):
        srcs, lands = refs[:n_arr], refs[n_arr:2 * n_arr]
        send_sems, recv_sems = refs[2 * n_arr:]
        me = _place()
        copies = [pltpu.make_async_remote_copy(src_ref=srcs[a].at[1 - me[2]], dst_ref=lands[a], send_sem=send_sems.at[a],
                                               recv_sem=recv_sems.at[a], device_id=_flip(me, 1), device_id_type=MESH)
                  for a in range(n_arr)]
        for cp in copies:
            cp.start()
        for cp in copies:
            cp.wait()

    return pl.pallas_call(
        body, name=name, in_specs=[hbm] * n_arr, out_specs=[hbm] * n_arr,
        out_shape=[jax.ShapeDtypeStruct(p.shape[1:], p.dtype) for p in parts],
        scratch_shapes=[pltpu.SemaphoreType.DMA((n_arr,)), pltpu.SemaphoreType.DMA((n_arr,))],
    )(*parts)


def _share_with_other_core(gathered, *, name):
    n_arr = len(gathered)
    hbm = pl.BlockSpec(memory_space=pl.ANY)

    def body(*refs):
        bufs = refs[n_arr:2 * n_arr]
        send_sems, recv_sems = refs[2 * n_arr:]
        me = _place()
        copies = []
        for a in range(n_arr):
            for j in range(N_CHIPS):
                block = bufs[a].at[j, me[2]]
                copies.append(pltpu.make_async_remote_copy(
                    src_ref=block, dst_ref=block, send_sem=send_sems.at[N_CHIPS * a + j],
                    recv_sem=recv_sems.at[N_CHIPS * a + j], device_id=_flip(me, 1), device_id_type=MESH))
        for cp in copies:
            cp.start()
        for cp in copies:
            cp.wait()

    return pl.pallas_call(
        body, name=name, in_specs=[hbm] * n_arr, out_specs=[hbm] * n_arr,
        out_shape=[jax.ShapeDtypeStruct(g.shape, g.dtype) for g in gathered],
        scratch_shapes=[pltpu.SemaphoreType.DMA((N_CHIPS * n_arr,)), pltpu.SemaphoreType.DMA((N_CHIPS * n_arr,))],
        input_output_aliases={a: a for a in range(n_arr)},
    )(*gathered)


def _pair_sum(mine, other, core, *, tr, name):
    _, n, rows, cols = mine.shape
    tr = min(tr, rows)

    def body(core_ref, a_ref, b_ref, o_ref):
        o_ref[...] = (a_ref[0].astype(F32) + b_ref[...].astype(F32)).astype(BF16)

    return pl.pallas_call(
        body, name=name,
        grid_spec=pltpu.PrefetchScalarGridSpec(
            num_scalar_prefetch=1, grid=(n, rows // tr),
            in_specs=[pl.BlockSpec((1, 1, tr, cols), lambda j, i, core_ref: (core_ref[0], j, i, 0)),
                      pl.BlockSpec((1, tr, cols), lambda j, i, core_ref: (j, i, 0))],
            out_specs=pl.BlockSpec((1, tr, cols), lambda j, i, core_ref: (j, i, 0))),
        out_shape=jax.ShapeDtypeStruct(other.shape, BF16),
        compiler_params=_params(2),
    )(core, mine, other)


def _gather_small(vec):
    def body(v_ref, out_ref, send_sems, recv_sems, local_sem):
        me = _place()

        def rows(p):
            return out_ref.at[pl.ds(pl.multiple_of(_index(p) * SMALL_ROWS, SMALL_ROWS), SMALL_ROWS), :]

        mine = pltpu.make_async_copy(v_ref, rows(me), local_sem)
        mine.start()
        sends = []
        for k in range(1, N_DEV):
            peer = _flip(me, k)
            cp = pltpu.make_async_remote_copy(src_ref=v_ref, dst_ref=rows(me), send_sem=send_sems.at[k - 1],
                                              recv_sem=recv_sems.at[k - 1], device_id=peer, device_id_type=MESH)
            cp.start()
            sends.append(cp)
        for k in range(1, N_DEV):
            peer = _flip(me, k)
            pltpu.make_async_remote_copy(src_ref=rows(peer), dst_ref=rows(peer), send_sem=send_sems.at[k - 1],
                                         recv_sem=recv_sems.at[k - 1], device_id=peer, device_id_type=MESH).wait_recv()
        for cp in sends:
            cp.wait_send()
        mine.wait()

    return pl.pallas_call(
        body, name="gather_small",
        in_specs=[pl.BlockSpec(memory_space=pltpu.VMEM)], out_specs=pl.BlockSpec(memory_space=pltpu.VMEM),
        out_shape=jax.ShapeDtypeStruct((N_DEV * SMALL_ROWS, SMALL_COLS), F32),
        scratch_shapes=[pltpu.SemaphoreType.DMA((7,)), pltpu.SemaphoreType.DMA((7,)), pltpu.SemaphoreType.DMA],
    )(vec)


def _w_in_nice(gathered):
    pieces, pos = [], 0
    for o0, width, n0 in sorted(_SEGMENTS, key=lambda seg: seg[2]):
        if n0 > pos:
            pieces.append(jnp.zeros((D_MODEL, n0 - pos), gathered.dtype))
        o = o0
        while o < o0 + width:
            d = o // SHARD_IN
            hi = min(o0 + width, (d + 1) * SHARD_IN)
            pieces.append(gathered[d][:, o - d * SHARD_IN:hi - d * SHARD_IN])
            o = hi
        pos = n0 + width
    pieces.append(jnp.zeros((D_MODEL, NP_IN - pos), gathered.dtype))
    return jnp.concatenate(pieces, axis=1)


def _w_in_blocks(chips, dw_lat, dw_rest):
    blocks = []
    for core in range(2):
        for chip in chips:
            lo = (2 * chip + core) * SHARD_IN
            runs = []
            for o0, width, n0 in _SEGMENTS:
                a, b = max(lo, o0), min(lo + SHARD_IN, o0 + width)
                if a < b:
                    n_a, n_b = n0 + a - o0, n0 + b - o0
                    runs.append(dw_lat[:, n_a:n_b] if n_b <= LAT_W else dw_rest[:, n_a - LAT_W:n_b - LAT_W])
            blocks.append(jnp.concatenate(runs, axis=1))
    return jnp.stack(blocks).reshape(2, len(chips), D_MODEL, SHARD_IN)


def _by_core(shards):
    return shards.reshape((N_CHIPS, 2) + shards.shape[1:]).swapaxes(0, 1)


EARLY_CHIPS = (1, 2)
LATE_CHIPS = (0, 3)


def _w_uq_nice(shard):
    z = jnp.zeros((Q_RANK, 32), shard.dtype)
    return jnp.concatenate([shard[:, :128], shard[:, 128:160], z, shard[:, 160:192], z], axis=1)


def _pack_small(g_pre, g_post, g_q, g_kv, b_f, extra=None):
    parts = [g_pre.reshape(-1), g_post.reshape(-1), g_q.reshape(-1), g_kv.reshape(-1), b_f.reshape(-1)]
    if extra is not None:
        parts.append(extra.reshape(-1))
    flat = jnp.concatenate(parts)
    flat = jnp.concatenate([flat, jnp.zeros((SMALL_ROWS * SMALL_COLS - flat.shape[0],), F32)])
    return flat.reshape(SMALL_ROWS, SMALL_COLS)


def _unpack_small(packed):
    flat = packed.reshape(-1)
    o = 0
    out = []
    for n in (D_MODEL, D_MODEL, Q_RANK, KV_RANK, N_HEADS):
        out.append(flat[o:o + n].reshape(1, n))
        o += n
    return out, flat[o]


def kernel(x, positions, g_pre, w_in, g_q_latent, w_uq, g_kv_latent, w_ukv, b_forget, w_out, g_post, loss_target, m_g_pre, m_w_in, m_g_q_latent, m_w_uq, m_g_kv_latent, m_w_ukv, m_b_forget, m_w_out, m_g_post, v_g_pre, v_w_in, v_g_q_latent, v_w_uq, v_g_kv_latent, v_w_ukv, v_b_forget, v_w_out, v_g_post):
    t = x.shape[1]
    tb = min(512, t)
    tm = min(256, t)
    nb = t // tb
    x2 = x.reshape(t, D_MODEL)
    target = loss_target.reshape(t, D_MODEL)
    pos_col = positions.reshape(t, 1).astype(F32)
    bf_row = jnp.concatenate([b_forget.reshape(1, N_HEADS), jnp.zeros((1, LANE - N_HEADS), F32)], axis=1)

    h, h_t, g_in = _prenorm(x2, g_pre, _AllGather(w_in[0].astype(BF16)), tm=tm)
    w_in_n = _w_in_nice(g_in)
    gather_rest = _Exchange([(w, ALL_CHIPS, None, True, True) for w in
                             (_w_uq_nice(w_uq[0].astype(BF16)), w_ukv[0].astype(BF16), w_out[0].astype(BF16))])
    core = lax.axis_index("c").astype(jnp.int32).reshape(1)
    proj, g_uq, g_ukv, g_out = _mm(h, w_in_n, name="proj_in", out_dtype=BF16, tm=2048, tn=512, tk=2048,
                                   rider=gather_rest)
    g_uq, g_ukv, g_out = _share_with_other_core([g_uq, g_ukv, g_out], name="share_weights")
    w_uq_n = g_uq.reshape(N_DEV, Q_RANK, 256).transpose(1, 0, 2).reshape(Q_RANK, N_HEADS * 256)
    w_ukv_n = g_ukv.reshape(N_DEV, KV_RANK, 256).transpose(1, 0, 2).reshape(KV_RANK, N_HEADS * 256)
    w_out_n = g_out.reshape(D_MODEL, D_MODEL)
    flog = _mm(h, w_in_n[:, FL0:FL0 + LANE], name="proj_flog", out_dtype=F32, tm=1024, tn=LANE, tk=2048)
    qn, kvn, kr, cos_t, sin_t, c, qn_t, kvn_t = _mid_fwd(proj, flog, g_q_latent, g_kv_latent, bf_row, pos_col, tm=tm)
    q_full = _q_up_rope(qn, w_uq_n, cos_t, sin_t, tm=min(1024, t))
    kv = _mm(kvn, w_ukv_n, name="kv_up", out_dtype=BF16, tm=1024, tn=512, tk=KV_RANK)
    c_heads = c[:, :N_HEADS].T
    c_col = c_heads.reshape(N_HEADS, t, 1)
    c_row4 = c_heads.reshape(N_HEADS, nb, 1, tb)
    o_all, og_all, og_t, lse4_mla = _attn_fwd(False, (q_full, kv, kr, proj), t=t, tb=tb, name="mla_fwd")
    o_all, og_all, og_t, lse4_fox = _attn_fwd(True, (proj, c_col, c_row4, o_all, og_all, og_t), t=t, tb=tb,
                                              name="fox_fwd")
    dy, d_o_post, dg_post, loss_part = _out_norm_loss(og_all, w_out_n, x2, target, g_post, tm=min(512, t))

    dw_out = _mm(og_t, d_o_post, name="dw_out", out_dtype=BF16, tm=1024, tn=1024, tk=1024)
    p_out = _by_core(dw_out.reshape(N_DEV, D_MODEL // N_DEV, D_MODEL))
    (o_out,) = _to_other_core([p_out], name="dw_out_to_core")
    s_out = _pair_sum(p_out, o_out, core, tr=256, name="dw_out_pair_sum")
    d_attn, dproj, delta = _dog_gate(d_o_post, w_out_n, o_all, proj, tm=min(1024, t))
    delta4 = delta[:, :2 * N_HEADS].T.reshape(2 * N_HEADS, nb, 1, tb)
    dproj, dck, dcq, l_out = _attn_bwd(True, (proj, d_attn, lse4_fox, delta4[N_HEADS:], c_row4, c_col, dproj),
                                       t=t, tb=tb, name="fox_bwd",
                                       rider=_Exchange([(s_out, ALL_CHIPS, None, False, False)]))
    dw_in_rest = _mm(h_t, dproj, name="dw_in_rest", out_dtype=BF16, tm=2048, tn=512, tk=1024,
                     b_cols=(LAT_W, NP_IN - LAT_W))
    p_in = _w_in_blocks(EARLY_CHIPS, None, dw_in_rest)
    (o_in,) = _to_other_core([p_in], name="dw_in_early_to_core")
    s_in = _pair_sum(p_in, o_in, core, tr=256, name="dw_in_early_pair_sum")
    dq_full, dkv, dkr, l_in = _attn_bwd(False, (q_full, kv, kr, d_attn, lse4_mla, delta4[:N_HEADS]),
                                        t=t, tb=tb, name="mla_bwd",
                                        rider=_Exchange([(s_in, EARLY_CHIPS, None, False, False)]))
    dc_heads = dck.reshape(N_HEADS, t) + dcq.reshape(N_HEADS, t)
    dck_rows = jnp.concatenate([dc_heads.T, jnp.zeros((t, LANE - N_HEADS), F32)], axis=1)
    dq2, dkr_raw, dfl, dbf = _mid_bwd(dq_full, dkr, cos_t, sin_t, dck_rows, flog, bf_row, tm=tm)
    dqn = _mm(dq2, w_uq_n, name="d_qn", nt=True, out_dtype=F32, tm=1024, tn=Q_RANK, tk=2048)
    dkvn = _mm(dkv, w_ukv_n, name="d_kvn", nt=True, out_dtype=F32, tm=1024, tn=KV_RANK, tk=2048)
    dw_uq = _mm(qn_t, dq2, name="dw_uq", out_dtype=BF16, tm=Q_RANK, tn=1024, tk=1024)
    dw_ukv = _mm(kvn_t, dkv, name="dw_ukv", out_dtype=BF16, tm=KV_RANK, tn=1024, tk=1024)
    dproj, dg_q, dg_kv = _norm_bwd(proj, dqn, dkvn, g_q_latent, g_kv_latent, dkr_raw, dfl, dproj, tm=tm)
    dw_uq_h = dw_uq.reshape(Q_RANK, N_HEADS, 256)
    s_uq = jnp.concatenate([dw_uq_h[:, :, :160], dw_uq_h[:, :, 192:224]], axis=2).transpose(1, 0, 2)
    s_ukv = dw_ukv.reshape(KV_RANK, N_HEADS, 256).transpose(1, 0, 2)
    up_parts = [_by_core(s_uq), _by_core(s_ukv)]
    up_other = _to_other_core(up_parts, name="dw_up_to_core")
    up_sums = [_pair_sum(p, o_, core, tr=256, name=f"dw_up_pair_sum_{i}") for i, (p, o_) in enumerate(zip(up_parts, up_other))]
    dw_in_lat, l_uq, l_ukv = _mm(h_t, dproj, name="dw_in_lat", out_dtype=BF16, tm=1024, tn=LAT_W, tk=1024,
                                 b_cols=(0, LAT_W), rider=_Exchange([(s, ALL_CHIPS, None, False, False) for s in up_sums]))
    p_late = _w_in_blocks(LATE_CHIPS, dw_in_lat, dw_in_rest)
    (o_late,) = _to_other_core([p_late], name="dw_in_late_to_core")
    s_late = _pair_sum(p_late, o_late, core, tr=256, name="dw_in_late_pair_sum")
    dh, l_in = _mm(dproj, w_in_n, name="d_h", nt=True, out_dtype=F32, tm=2048, tn=1024, tk=NP_IN // 4,
                   rider=_Exchange([(s_late, LATE_CHIPS, l_in, False, False)]))
    grad_x, dg_pre = _prenorm_bwd(dh, x2, g_pre, dy, tm=tm)

    small = _gather_small(_pack_small(dg_pre, dg_post, dg_q, dg_kv, dbf[:, :N_HEADS], loss_part))

    res_in = _adamw(l_in, w_in[0], m_w_in[0], v_w_in[0], tr=256, name="adamw_w_in")
    res_uq = _adamw(l_uq, w_uq[0], m_w_uq[0], v_w_uq[0], tr=256, name="adamw_w_uq")
    res_ukv = _adamw(l_ukv, w_ukv[0], m_w_ukv[0], v_w_ukv[0], tr=256, name="adamw_w_ukv")
    res_out = _adamw(l_out, w_out[0], m_w_out[0], v_w_out[0], tr=128, name="adamw_w_out")
    res_small = _adamw_small(
        small,
        _pack_small(g_pre, g_post, g_q_latent, g_kv_latent, b_forget),
        _pack_small(m_g_pre, m_g_post, m_g_q_latent, m_g_kv_latent, m_b_forget),
        _pack_small(v_g_pre, v_g_post, v_g_q_latent, v_g_kv_latent, v_b_forget))
    small_out = [_unpack_small(r) for r in res_small]
    loss = small_out[0][1]

    def leaves(kind):
        (s_pre, s_post, s_q, s_kv, s_bf), _ = small_out[kind]
        return [s_pre, res_in[kind][None], s_q, res_uq[kind][None], s_kv, res_ukv[kind][None], s_bf,
                res_out[kind][None], s_post]

    return (loss, grad_x.reshape(x.shape), *leaves(0), *leaves(1), *leaves(2), *leaves(3))
```

```python
import functools

import numpy as np
import jax
import jax.numpy as jnp
from jax import lax
from jax.experimental import pallas as pl
from jax.experimental.pallas import tpu as pltpu

F32 = jnp.float32
BF16 = jnp.bfloat16
MESH = pl.DeviceIdType.MESH

N_DEV = 8
D_MODEL = 2048
N_HEADS = 8
HEAD = 128
Q_RANK = 768
KV_RANK = 512
ROPE = 64
D_IN = 6472
SHARD_IN = D_IN // N_DEV
NORM_EPS = 1e-6
ROPE_THETA = 10000.0
MLA_SCALE = (HEAD + ROPE) ** -0.5
FOX_SCALE = HEAD ** -0.5

QL0, KR0, FL0, KVL0, GM0, GF0, FQ0, FK0, FV0, NP_IN = 0, 768, 896, 1024, 1536, 2560, 3584, 4608, 5632, 6656
LAT_W = GM0
LANE = 128
_SEGMENTS = ((0, 768, QL0), (768, 512, KVL0), (1280, 32, KR0), (1312, 32, KR0 + 64), (1344, 1024, GM0),
             (2368, 3072, FQ0), (5440, 8, FL0), (5448, 1024, GF0))
LOG2E = 1.4426950408889634

ADAM_LR = 0.001
ADAM_B1 = 0.9
ADAM_B2 = 0.999
ADAM_EPS = 1e-08
ADAM_WD = 0.01
ADAM_STEP = 10

VMEM_LIMIT_BYTES = 56 * 1024 * 1024
SMALL_ROWS, SMALL_COLS = 8, 768


def _params(n_grid=0):
    return pltpu.CompilerParams(vmem_limit_bytes=VMEM_LIMIT_BYTES,
                                dimension_semantics=("arbitrary",) * n_grid if n_grid else None)


def _sigmoid(z):
    return 1.0 / (1.0 + jnp.exp(-z))


def _split3(v):
    a = v.astype(BF16)
    r = v - a.astype(F32)
    b = r.astype(BF16)
    c = (r - b.astype(F32)).astype(BF16)
    return a, b, c


def _mm(a, b, *, name, nt=False, out_dtype=F32, tm=1024, tn=512, tk=2048, b_cols=None, rider=None):
    m, k_dim = a.shape
    n = b.shape[0] if nt else b.shape[1]
    col0 = 0
    if b_cols is not None:
        assert not nt
        col0, n = b_cols
    assert (b.shape[1] if nt else b.shape[0]) == k_dim
    tm, tn, tk = min(tm, m), min(tn, n), min(tk, k_dim)
    assert m % tm == 0 and n % tn == 0 and k_dim % tk == 0 and col0 % tn == 0, (name, a.shape, b.shape)
    nk = k_dim // tk
    j0 = col0 // tn
    grid = (m // tm, n // tn, nk)
    dims = (((1,), (1 if nt else 0,)), ((), ()))
    n_rin = len(rider.operands) if rider else 0
    n_rout = len(rider.out_shape) if rider else 0

    def body(*refs):
        a_ref, b_ref = refs[:2]
        o_ref = refs[2 + n_rin]
        acc_ref = refs[3 + n_rin + n_rout]
        i, j, k = pl.program_id(0), pl.program_id(1), pl.program_id(2)
        if rider:
            rider_refs = (refs[2:2 + n_rin], refs[3 + n_rin:3 + n_rin + n_rout], refs[4 + n_rin + n_rout:])

            @pl.when(jnp.logical_and(i == 0, jnp.logical_and(j == 0, k == 0)))
            def _():
                rider.start(*rider_refs)

        @pl.when(k == 0)
        def _():
            acc_ref[...] = jnp.zeros_like(acc_ref)

        acc_ref[...] += lax.dot_general(a_ref[...], b_ref[...], dims, preferred_element_type=F32)

        @pl.when(k == nk - 1)
        def _():
            o_ref[...] = acc_ref[...].astype(o_ref.dtype)

        if rider:
            @pl.when(jnp.logical_and(i == grid[0] - 1, jnp.logical_and(j == grid[1] - 1, k == nk - 1)))
            def _():
                rider.wait(*rider_refs)

    b_spec = (pl.BlockSpec((tn, tk), lambda i, j, k: (j, k)) if nt
              else pl.BlockSpec((tk, tn), lambda i, j, k: (k, j0 + j)))
    a_spec = pl.BlockSpec((tm, tk), lambda i, j, k: (i, k))
    any_spec = pl.BlockSpec(memory_space=pl.ANY)
    out = pl.pallas_call(
        body, name=name, grid=grid,
        in_specs=[a_spec, b_spec] + [any_spec] * n_rin,
        out_specs=[pl.BlockSpec((tm, tn), lambda i, j, k: (i, j))] + [any_spec] * n_rout,
        out_shape=[jax.ShapeDtypeStruct((m, n), out_dtype)] + (list(rider.out_shape) if rider else []),
        scratch_shapes=[pltpu.VMEM((tm, tn), F32)] + (list(rider.scratch) if rider else []),
        input_output_aliases={2 + i_in: 1 + i_out for i_in, i_out in rider.aliases.items()} if rider else {},
        compiler_params=_params(3),
    )(a, b, *(rider.operands if rider else ()))
    return out if rider else out[0]


def _prenorm(x, g, rider, *, tm):
    t = x.shape[0]
    n_steps = t // tm
    n_rin, n_rout = len(rider.operands), len(rider.out_shape)

    def body(*refs):
        x_ref, g_ref = refs[:2]
        h_ref, ht_ref = refs[2 + n_rin:4 + n_rin]
        rider_refs = (refs[2:2 + n_rin], refs[4 + n_rin:4 + n_rin + n_rout], refs[4 + n_rin + n_rout:])
        i = pl.program_id(0)
        pl.when(i == 0)(lambda: rider.start(*rider_refs))
        xv = x_ref[...]
        r = lax.rsqrt(jnp.mean(xv * xv, axis=-1, keepdims=True) + NORM_EPS)
        h = xv * r * g_ref[...]
        h_ref[...] = h.astype(BF16)
        ht_ref[...] = h.T.astype(BF16)
        pl.when(i == n_steps - 1)(lambda: rider.wait(*rider_refs))

    any_spec = pl.BlockSpec(memory_space=pl.ANY)
    return pl.pallas_call(
        body, name="prenorm", grid=(n_steps,),
        in_specs=[pl.BlockSpec((tm, D_MODEL), lambda i: (i, 0)), pl.BlockSpec((1, D_MODEL), lambda i: (0, 0))]
        + [any_spec] * n_rin,
        out_specs=[pl.BlockSpec((tm, D_MODEL), lambda i: (i, 0)), pl.BlockSpec((D_MODEL, tm), lambda i: (0, i))]
        + [any_spec] * n_rout,
        out_shape=[jax.ShapeDtypeStruct((t, D_MODEL), BF16), jax.ShapeDtypeStruct((D_MODEL, t), BF16)]
        + list(rider.out_shape),
        scratch_shapes=list(rider.scratch),
        compiler_params=_params(1),
    )(x, g, *rider.operands)


def _rope_rows():
    inv = (np.float32(ROPE_THETA) ** (-np.arange(0, ROPE, 2, dtype=np.float32) / np.float32(ROPE))).astype(np.float32)
    invf = np.zeros((1, LANE), np.float32)
    sgn = np.zeros((1, LANE), np.float32)
    invf[0, 0:32] = inv
    invf[0, 64:96] = inv
    sgn[0, 0:32] = -1.0
    sgn[0, 64:96] = 1.0
    return jnp.asarray(invf), jnp.asarray(sgn)


def _rot(v, cos_t, sin_t):
    return v * cos_t + pltpu.roll(v, 64, 1) * sin_t


def _rot_bwd(dv, cos_t, sin_t):
    return dv * cos_t + pltpu.roll(dv * sin_t, 64, 1)


def _mid_fwd(proj, flog, g_q, g_kv, bf_row, pos_col, *, tm):
    t = proj.shape[0]
    invf, sgn = _rope_rows()

    def body(p_ref, fl_ref, gq_ref, gkv_ref, bf_ref, pos_ref, invf_ref, sgn_ref,
             qn_ref, kvn_ref, kr_ref, cos_ref, sin_ref, c_ref, qnt_ref, kvnt_ref, carry_ref):
        i = pl.program_id(0)

        @pl.when(i == 0)
        def _():
            carry_ref[...] = jnp.zeros_like(carry_ref)

        ql = p_ref[:, QL0:QL0 + Q_RANK].astype(F32)
        r = lax.rsqrt(jnp.mean(ql * ql, axis=-1, keepdims=True) + NORM_EPS)
        qn = ql * r * gq_ref[...]
        qn_ref[...] = qn.astype(BF16)
        qnt_ref[...] = qn.T.astype(BF16)
        kvl = p_ref[:, KVL0:KVL0 + KV_RANK].astype(F32)
        r = lax.rsqrt(jnp.mean(kvl * kvl, axis=-1, keepdims=True) + NORM_EPS)
        kvn = kvl * r * gkv_ref[...]
        kvn_ref[...] = kvn.astype(BF16)
        kvnt_ref[...] = kvn.T.astype(BF16)

        ang = pos_ref[...] * invf_ref[...]
        cos_t = jnp.cos(ang)
        sin_t = jnp.sin(ang) * sgn_ref[...]
        cos_ref[...] = cos_t
        sin_ref[...] = sin_t
        kr_ref[...] = _rot(p_ref[:, KR0:KR0 + LANE].astype(F32), cos_t, sin_t).astype(BF16)

        z = fl_ref[...] + bf_ref[...]
        logf = jnp.minimum(z, 0.0) - jnp.log(1.0 + jnp.exp(-jnp.abs(z)))
        row = lax.broadcasted_iota(jnp.int32, (tm, tm), 0)
        col = lax.broadcasted_iota(jnp.int32, (tm, tm), 1)
        tri = (col <= row).astype(BF16)
        acc = carry_ref[0:1, :]
        for part in _split3(logf):
            acc = acc + jnp.dot(tri, part, preferred_element_type=F32)
        c_ref[...] = acc * (1.0 / FOX_SCALE)
        carry_ref[0:1, :] = carry_ref[0:1, :] + jnp.sum(logf, axis=0, keepdims=True)

    row_spec = lambda w: pl.BlockSpec((tm, w), lambda i: (i, 0))
    vec_spec = lambda w: pl.BlockSpec((1, w), lambda i: (0, 0))
    return pl.pallas_call(
        body, name="mid_fwd", grid=(t // tm,),
        in_specs=[row_spec(LAT_W), row_spec(LANE), vec_spec(Q_RANK), vec_spec(KV_RANK), vec_spec(LANE),
                  pl.BlockSpec((tm, 1), lambda i: (i, 0)), vec_spec(LANE), vec_spec(LANE)],
        out_specs=[row_spec(Q_RANK), row_spec(KV_RANK), row_spec(LANE), row_spec(LANE), row_spec(LANE), row_spec(LANE),
                   pl.BlockSpec((Q_RANK, tm), lambda i: (0, i)), pl.BlockSpec((KV_RANK, tm), lambda i: (0, i))],
        out_shape=[jax.ShapeDtypeStruct((t, Q_RANK), BF16), jax.ShapeDtypeStruct((t, KV_RANK), BF16),
                   jax.ShapeDtypeStruct((t, LANE), BF16), jax.ShapeDtypeStruct((t, LANE), F32),
                   jax.ShapeDtypeStruct((t, LANE), F32), jax.ShapeDtypeStruct((t, LANE), F32),
                   jax.ShapeDtypeStruct((Q_RANK, t), BF16), jax.ShapeDtypeStruct((KV_RANK, t), BF16)],
        scratch_shapes=[pltpu.VMEM((8, LANE), F32)],
        compiler_params=_params(1),
    )(proj, flog, g_q, g_kv, bf_row, pos_col, invf, sgn)


def _q_up_rope(qn, w_uq_n, cos_t, sin_t, *, tm):
    t = qn.shape[0]
    tn = 2 * 256

    def body(a_ref, b_ref, cos_ref, sin_ref, o_ref):
        q = jnp.dot(a_ref[...], b_ref[...], preferred_element_type=F32)
        c, s = cos_ref[...], sin_ref[...]
        for u in range(tn // 256):
            o_ref[:, 256 * u:256 * u + 128] = q[:, 256 * u:256 * u + 128].astype(BF16)
            o_ref[:, 256 * u + 128:256 * u + 256] = _rot(q[:, 256 * u + 128:256 * u + 256], c, s).astype(BF16)

    return pl.pallas_call(
        body, name="q_up_rope", grid=(t // tm, N_HEADS * 256 // tn),
        in_specs=[pl.BlockSpec((tm, Q_RANK), lambda i, j: (i, 0)), pl.BlockSpec((Q_RANK, tn), lambda i, j: (0, j)),
                  pl.BlockSpec((tm, LANE), lambda i, j: (i, 0)), pl.BlockSpec((tm, LANE), lambda i, j: (i, 0))],
        out_specs=pl.BlockSpec((tm, tn), lambda i, j: (i, j)),
        out_shape=jax.ShapeDtypeStruct((t, N_HEADS * 256), BF16),
        compiler_params=_params(2),
    )(qn, w_uq_n, cos_t, sin_t)


def _attn_fwd(fox, operands, *, t, tb, name):
    nb = t // tb
    scale = FOX_SCALE if fox else MLA_SCALE
    exp2_scale = scale * LOG2E
    pair = 2 * HEAD
    pair0 = N_HEADS // 2 if fox else 0
    q_w = HEAD if fox else 2 * HEAD
    nt_dims = (((1,), (1,)), ((), ()))
    tn_dims = (((0,), (0,)), ((), ()))

    def body(*refs):
        if fox:
            (q_ref, k_ref, v_ref, gate_ref, cq_ref, ck_ref, _, _, _,
             o_ref, og_ref, ogt_ref, lse_ref, m_s, l_s, acc_s) = refs
        else:
            q_ref, kv_ref, kr_ref, gate_ref, o_ref, og_ref, ogt_ref, lse_ref, m_s, l_s, acc_s = refs
        qi = pl.program_id(1)
        m_s[...] = jnp.full_like(m_s, -jnp.inf)
        l_s[...] = jnp.zeros_like(l_s)
        acc_s[...] = jnp.zeros_like(acc_s)

        def chunk(kc, masked):
            off = pl.multiple_of(kc * tb, tb)
            scores = []
            for u in range(2):
                q = q_ref[:, q_w * u:q_w * (u + 1)]
                if fox:
                    kk = k_ref[pl.ds(off, tb), HEAD * u:HEAD * (u + 1)]
                else:
                    kk = jnp.concatenate([kv_ref[pl.ds(off, tb), pair * u:pair * u + HEAD],
                                          kr_ref[pl.ds(off, tb), :]], axis=1)
                s = lax.dot_general(kk, q, nt_dims, preferred_element_type=F32)
                if fox:
                    s = s + cq_ref[u, 0] - ck_ref[u, pl.ds(off, tb), :]
                if masked:
                    row = lax.broadcasted_iota(jnp.int32, (tb, tb), 0)
                    col = lax.broadcasted_iota(jnp.int32, (tb, tb), 1)
                    s = jnp.where(row <= col, s, -jnp.inf)
                scores.append(s)
            for u in range(2):
                s = scores[u]
                m_prev = m_s[u]
                m_new = jnp.maximum(m_prev, jnp.max(s, axis=0, keepdims=True))
                alpha = jnp.exp2((m_prev - m_new) * exp2_scale)
                p = jnp.exp2((s - m_new) * exp2_scale)
                l_s[u] = alpha * l_s[u] + jnp.sum(p, axis=0, keepdims=True)
                if fox:
                    vv = v_ref[pl.ds(off, tb), HEAD * u:HEAD * (u + 1)]
                else:
                    vv = kv_ref[pl.ds(off, tb), pair * u + HEAD:pair * (u + 1)]
                acc_s[u] = alpha * acc_s[u] + lax.dot_general(vv, p.astype(BF16), tn_dims,
                                                              preferred_element_type=F32)
                m_s[u] = m_new

        def loop_body(kc, carry):
            chunk(kc, False)
            return carry

        lax.fori_loop(0, qi, loop_body, 0)
        chunk(qi, True)
        for u in range(2):
            cols = slice(HEAD * u, HEAD * (u + 1))
            o_t = acc_s[u] / l_s[u]
            o = o_t.T
            o_ref[:, cols] = o
            g = gate_ref[:, cols].astype(F32)
            silu = g * _sigmoid(g)
            og_ref[:, cols] = (o * silu).astype(BF16)
            ogt_ref[cols, :] = (o_t * silu.T).astype(BF16)
            lse_ref[u, 0] = m_s[u] * scale + jnp.log(l_s[u])

    any_spec = pl.BlockSpec(memory_space=pl.ANY)
    row_stat = pl.BlockSpec((2, 1, 1, tb), lambda g, i: (g, i, 0, 0))
    if fox:
        proj, c_col, c_row4, o_all, og_all, ogt_all = operands
        ins = [proj, proj, proj, proj, c_row4, c_col, o_all, og_all, ogt_all]
        in_specs = [pl.BlockSpec((tb, pair), lambda g, i: (i, FQ0 // pair + g)),
                    pl.BlockSpec((t, pair), lambda g, i: (0, FK0 // pair + g)),
                    pl.BlockSpec((t, pair), lambda g, i: (0, FV0 // pair + g)),
                    pl.BlockSpec((tb, pair), lambda g, i: (i, GF0 // pair + g)),
                    row_stat, pl.BlockSpec((2, t, 1), lambda g, i: (g, 0, 0)), any_spec, any_spec, any_spec]
        aliases = {6: 0, 7: 1, 8: 2}
    else:
        q_full, kv, kr, proj = operands
        ins = [q_full, kv, kr, proj]
        in_specs = [pl.BlockSpec((tb, 2 * pair), lambda g, i: (i, g)),
                    pl.BlockSpec((t, 2 * pair), lambda g, i: (0, g)),
                    pl.BlockSpec((t, HEAD), lambda g, i: (0, 0)),
                    pl.BlockSpec((tb, pair), lambda g, i: (i, GM0 // pair + g))]
        aliases = {}
    return pl.pallas_call(
        body, name=name, grid=(N_HEADS // 2, nb), in_specs=in_specs,
        out_specs=[pl.BlockSpec((tb, pair), lambda g, i: (i, pair0 + g)),
                   pl.BlockSpec((tb, pair), lambda g, i: (i, pair0 + g)),
                   pl.BlockSpec((pair, tb), lambda g, i: (pair0 + g, i)), row_stat],
        out_shape=[jax.ShapeDtypeStruct((t, 2 * N_HEADS * HEAD), F32), jax.ShapeDtypeStruct((t, 2 * N_HEADS * HEAD), BF16),
                   jax.ShapeDtypeStruct((2 * N_HEADS * HEAD, t), BF16), jax.ShapeDtypeStruct((N_HEADS, nb, 1, tb), F32)],
        scratch_shapes=[pltpu.VMEM((2, 1, tb), F32), pltpu.VMEM((2, 1, tb), F32), pltpu.VMEM((2, HEAD, tb), F32)],
        input_output_aliases=aliases,
        compiler_params=_params(2),
    )(*ins)


def _out_norm_loss(og, w_out_n, x, target, g, *, tm):
    t = og.shape[0]

    def body(og_ref, w_ref, x_ref, t_ref, g_ref, dy_ref, do_ref, dg_ref, loss_ref):
        i = pl.program_id(0)

        @pl.when(i == 0)
        def _():
            dg_ref[...] = jnp.zeros_like(dg_ref)
            loss_ref[...] = jnp.zeros_like(loss_ref)

        ov = jnp.dot(og_ref[...], w_ref[...], preferred_element_type=F32)
        gv = g_ref[...]
        r = lax.rsqrt(jnp.mean(ov * ov, axis=-1, keepdims=True) + NORM_EPS)
        oh = ov * r
        e = x_ref[...] + oh * gv - t_ref[...]
        loss_ref[...] += 0.5 * jnp.sum(jnp.mean(e * e, axis=-1, keepdims=True), axis=0, keepdims=True)
        dy = e * (1.0 / D_MODEL)
        dy_ref[...] = dy
        dyg = dy * gv
        do_ref[...] = (r * (dyg - oh * jnp.mean(dyg * oh, axis=-1, keepdims=True))).astype(BF16)
        dg_ref[...] += jnp.sum(dy * oh, axis=0, keepdims=True)

    row = pl.BlockSpec((tm, D_MODEL), lambda i: (i, 0))
    vec = pl.BlockSpec((1, D_MODEL), lambda i: (0, 0))
    whole_w = pl.BlockSpec((D_MODEL, D_MODEL), lambda i: (0, 0), pipeline_mode=pl.Buffered(1))
    return pl.pallas_call(
        body, name="out_norm_loss", grid=(t // tm,),
        in_specs=[row, whole_w, row, row, vec],
        out_specs=[row, row, vec, pl.BlockSpec((1, 1), lambda i: (0, 0))],
        out_shape=[jax.ShapeDtypeStruct((t, D_MODEL), F32), jax.ShapeDtypeStruct((t, D_MODEL), BF16),
                   jax.ShapeDtypeStruct((1, D_MODEL), F32), jax.ShapeDtypeStruct((1, 1), F32)],
        compiler_params=_params(1),
    )(og, w_out_n, x, target, g)


def _dog_gate(d_o_post, w_out_n, o_all, proj, rider, *, tm):
    t = d_o_post.shape[0]
    n_group = 4
    pair = n_group * HEAD
    gate_blk = GM0 // pair
    assert GM0 % pair == 0 and GF0 == GM0 + N_HEADS * HEAD
    grid = (t // tm, 2 * N_HEADS // n_group)
    n_rin, n_rout = len(rider.operands), len(rider.out_shape)

    def body(*refs):
        do_ref, w_ref, o_ref, p_ref = refs[:4]
        dattn_ref, dproj_ref, delta_ref = refs[4 + n_rin:7 + n_rin]
        rider_refs = (refs[4:4 + n_rin], refs[7 + n_rin:7 + n_rin + n_rout], refs[7 + n_rin + n_rout:])
        i, j = pl.program_id(0), pl.program_id(1)
        pl.when(jnp.logical_and(i == 0, j == 0))(lambda: rider.start(*rider_refs))

        @pl.when(j == 0)
        def _():
            delta_ref[...] = jnp.zeros_like(delta_ref)

        dog = lax.dot_general(do_ref[...], w_ref[...], (((1,), (1,)), ((), ())), preferred_element_type=F32)
        g = p_ref[...].astype(F32)
        ov = o_ref[...]
        sg = _sigmoid(g)
        d_o = dog * (g * sg)
        dattn_ref[...] = d_o.astype(BF16)
        dproj_ref[...] = (dog * ov * (sg * (1.0 + g * (1.0 - sg)))).astype(BF16)
        prod = d_o * ov
        lane = lax.broadcasted_iota(jnp.int32, (tm, LANE), 1)
        delta = delta_ref[...]
        for u in range(n_group):
            part = jnp.sum(prod[:, HEAD * u:HEAD * (u + 1)], axis=-1, keepdims=True)
            delta = jnp.where(lane == n_group * j + u, part, delta)
        delta_ref[...] = delta
        pl.when(jnp.logical_and(i == grid[0] - 1, j == grid[1] - 1))(lambda: rider.wait(*rider_refs))

    any_spec = pl.BlockSpec(memory_space=pl.ANY)
    return pl.pallas_call(
        body, name="dog_gate", grid=grid,
        in_specs=[pl.BlockSpec((tm, D_MODEL), lambda i, j: (i, 0)), pl.BlockSpec((pair, D_MODEL), lambda i, j: (j, 0)),
                  pl.BlockSpec((tm, pair), lambda i, j: (i, j)), pl.BlockSpec((tm, pair), lambda i, j: (i, gate_blk + j))]
        + [any_spec] * n_rin,
        out_specs=[pl.BlockSpec((tm, pair), lambda i, j: (i, j)), pl.BlockSpec((tm, pair), lambda i, j: (i, gate_blk + j)),
                   pl.BlockSpec((tm, LANE), lambda i, j: (i, 0))] + [any_spec] * n_rout,
        out_shape=[jax.ShapeDtypeStruct((t, 2048), BF16), jax.ShapeDtypeStruct((t, NP_IN), BF16),
                   jax.ShapeDtypeStruct((t, LANE), F32)] + list(rider.out_shape),
        scratch_shapes=list(rider.scratch),
        compiler_params=_params(2),
    )(d_o_post, w_out_n, o_all, proj, *rider.operands)


def _attn_bwd(fox, operands, *, t, tb, name, rider=None):
    nb = t // tb
    n_pairs = N_HEADS // 2
    pair = 2 * HEAD
    scale = FOX_SCALE if fox else MLA_SCALE
    q_w = HEAD if fox else 2 * HEAD
    nt_dims = (((1,), (1,)), ((), ()))
    tn_dims = (((0,), (0,)), ((), ()))
    n_rin = len(rider.operands) if rider else 0
    n_rout = len(rider.out_shape) if rider else 0
    n_in, n_out, n_scr = (9, 3, 9) if fox else (6, 3, 2)

    def body(*refs):
        ends = np.cumsum([0, n_in, n_rin, n_out, n_rout, n_scr])
        in_refs, rider_in, out_refs, rider_out, scr_refs = (refs[a:b] for a, b in zip(ends[:-1], ends[1:]))
        rider_refs = (rider_in, rider_out, refs[ends[-1]:])
        if fox:
            q_ref, k_ref, v_ref, do_ref, lse_ref, dl_ref, cq_ref, ck_ref, _ = in_refs
            dproj_ref, dck_ref, dcq_ref = out_refs
            dq_acc, dk_s, dv_s, dc_s, dcq_s, stage_q, stage_k, stage_v, put_sems = scr_refs
        else:
            q_ref, kv_ref, kr_ref, do_ref, lse_ref, dl_ref = in_refs
            dq_acc, dkv_ref, dkr_ref = out_refs
            dk_s, dv_s = scr_refs
        g = pl.program_id(0)
        ki = pl.program_id(1)
        if rider:
            @pl.when(jnp.logical_and(g == 0, ki == 0))
            def _():
                rider.start(*rider_refs)

        @pl.when(ki == 0)
        def _():
            dq_acc[...] = jnp.zeros_like(dq_acc)
            if fox:
                dcq_s[...] = jnp.zeros_like(dcq_s)

        dk_s[...] = jnp.zeros_like(dk_s)
        dv_s[...] = jnp.zeros_like(dv_s)
        if fox:
            dc_s[...] = jnp.zeros_like(dc_s)
            keys = [k_ref[:, HEAD * u:HEAD * (u + 1)] for u in range(2)]
            vals = [v_ref[:, HEAD * u:HEAD * (u + 1)] for u in range(2)]
        else:
            keys = [jnp.concatenate([kv_ref[:, pair * u:pair * u + HEAD], kr_ref[...]], axis=1) for u in range(2)]
            vals = [kv_ref[:, pair * u + HEAD:pair * (u + 1)] for u in range(2)]

        def chunk(qc, masked):
            off = pl.multiple_of(qc * tb, tb)
            for u in range(2):
                kk, vv = keys[u], vals[u]
                qq = q_ref[pl.ds(off, tb), q_w * u:q_w * (u + 1)]
                dd = do_ref[pl.ds(off, tb), HEAD * u:HEAD * (u + 1)]
                s = lax.dot_general(kk, qq, nt_dims, preferred_element_type=F32)
                if fox:
                    s = s + cq_ref[u, qc] - ck_ref[u]
                if masked:
                    row = lax.broadcasted_iota(jnp.int32, (tb, tb), 0)
                    col = lax.broadcasted_iota(jnp.int32, (tb, tb), 1)
                    s = jnp.where(row <= col, s, -jnp.inf)
                p = jnp.exp2(s * (scale * LOG2E) - lse_ref[u, qc] * LOG2E)
                dv_s[u] += jnp.dot(p.astype(BF16), dd, preferred_element_type=F32)
                dp = lax.dot_general(vv, dd, nt_dims, preferred_element_type=F32)
                ds = p * (dp - dl_ref[u, qc])
                if fox:
                    dc_s[u] += jnp.sum(ds, axis=1, keepdims=True)
                    dcq_s[u, qc] += jnp.sum(ds, axis=0, keepdims=True)
                dsb = (ds * scale).astype(BF16)
                dk_s[u] += jnp.dot(dsb, qq, preferred_element_type=F32)
                dq_acc[pl.ds(off, tb), q_w * u:q_w * (u + 1)] += lax.dot_general(dsb, kk, tn_dims,
                                                                                 preferred_element_type=F32)

        chunk(ki, True)

        def loop_body(qc, carry):
            chunk(qc, False)
            return carry

        lax.fori_loop(ki + 1, nb, loop_body, 0)

        def put(stage_ref, rows, seg0, sem):
            col0 = pl.multiple_of(seg0 + g * pair, pair)
            return pltpu.make_async_copy(stage_ref, dproj_ref.at[rows, pl.ds(col0, pair)], sem)

        if fox:
            rows = pl.ds(pl.multiple_of(ki * tb, tb), tb)
            block_puts = [put(stage_k, rows, FK0, put_sems.at[1]), put(stage_v, rows, FV0, put_sems.at[2])]
            pair_put = put(stage_q, pl.ds(0, t), FQ0, put_sems.at[0])

            @pl.when(jnp.logical_or(g > 0, ki > 0))
            def _():
                for cp in block_puts:
                    cp.wait()

            for u in range(2):
                stage_k[:, HEAD * u:HEAD * (u + 1)] = dk_s[u].astype(BF16)
                stage_v[:, HEAD * u:HEAD * (u + 1)] = dv_s[u].astype(BF16)
                dck_ref[u] = -dc_s[u]
            for cp in block_puts:
                cp.start()

            @pl.when(ki == nb - 1)
            def _():
                @pl.when(g > 0)
                def _():
                    pair_put.wait()

                stage_q[...] = dq_acc[...].astype(BF16)
                pair_put.start()
                dcq_ref[...] = dcq_s[...]

            @pl.when(jnp.logical_and(g == n_pairs - 1, ki == nb - 1))
            def _():
                for cp in block_puts + [pair_put]:
                    cp.wait()
        else:
            dkv_ref[...] = jnp.concatenate([dk_s[0, :, :HEAD], dv_s[0], dk_s[1, :, :HEAD], dv_s[1]], axis=1).astype(BF16)
            dkr_ref[...] = jnp.concatenate([dk_s[0, :, HEAD:], dk_s[1, :, HEAD:]], axis=1)

        if rider:
            @pl.when(jnp.logical_and(g == n_pairs - 1, ki == nb - 1))
            def _():
                rider.wait(*rider_refs)

    stat = pl.BlockSpec((2, nb, 1, tb), lambda g, i: (g, 0, 0, 0))
    aliases = {}
    if fox:
        proj, d_o, lse4, delta4, c_row4, c_col, dproj = operands
        ins = [proj, proj, proj, d_o, lse4, delta4, c_row4, c_col, dproj]
        any_spec = pl.BlockSpec(memory_space=pl.ANY)
        in_specs = [pl.BlockSpec((t, pair), lambda g, i: (0, FQ0 // pair + g)),
                    pl.BlockSpec((tb, pair), lambda g, i: (i, FK0 // pair + g)),
                    pl.BlockSpec((tb, pair), lambda g, i: (i, FV0 // pair + g)),
                    pl.BlockSpec((t, pair), lambda g, i: (0, n_pairs + g)),
                    stat, stat, stat, pl.BlockSpec((2, tb, 1), lambda g, i: (g, i, 0)), any_spec]
        aliases = {8: 0}
        out_specs = [any_spec, pl.BlockSpec((2, tb, 1), lambda g, i: (g, i, 0)), stat]
        out_shape = [jax.ShapeDtypeStruct(dproj.shape, dproj.dtype), jax.ShapeDtypeStruct((N_HEADS, t, 1), F32),
                     jax.ShapeDtypeStruct((N_HEADS, nb, 1, tb), F32)]
        scratch = [pltpu.VMEM((t, pair), F32), pltpu.VMEM((2, tb, HEAD), F32), pltpu.VMEM((2, tb, HEAD), F32),
                   pltpu.VMEM((2, tb, 1), F32), pltpu.VMEM((2, nb, 1, tb), F32),
                   pltpu.VMEM((t, pair), BF16), pltpu.VMEM((tb, pair), BF16), pltpu.VMEM((tb, pair), BF16),
                   pltpu.SemaphoreType.DMA((3,))]
    else:
        q_full, kv, kr, d_o, lse4, delta4 = operands
        ins = [q_full, kv, kr, d_o, lse4, delta4]
        in_specs = [pl.BlockSpec((t, 2 * pair), lambda g, i: (0, g)),
                    pl.BlockSpec((tb, 2 * pair), lambda g, i: (i, g)),
                    pl.BlockSpec((tb, HEAD), lambda g, i: (i, 0)),
                    pl.BlockSpec((t, pair), lambda g, i: (0, g)),
                    stat, stat]
        out_specs = [pl.BlockSpec((t, 2 * pair), lambda g, i: (0, g)), pl.BlockSpec((tb, 2 * pair), lambda g, i: (i, g)),
                     pl.BlockSpec((tb, pair), lambda g, i: (i, g))]
        out_shape = [jax.ShapeDtypeStruct((t, 2048), F32), jax.ShapeDtypeStruct((t, 2048), BF16),
                     jax.ShapeDtypeStruct((t, 1024), F32)]
        scratch = [pltpu.VMEM((2, tb, 2 * HEAD), F32), pltpu.VMEM((2, tb, HEAD), F32)]
    assert (len(ins), len(out_specs), len(scratch)) == (n_in, n_out, n_scr)
    if rider:
        any_spec = pl.BlockSpec(memory_space=pl.ANY)
        aliases = {**aliases, **{n_in + i_in: n_out + i_out for i_in, i_out in rider.aliases.items()}}
        ins = ins + list(rider.operands)
        in_specs = in_specs + [any_spec] * n_rin
        out_specs = out_specs + [any_spec] * n_rout
        out_shape = out_shape + list(rider.out_shape)
        scratch = scratch + list(rider.scratch)
    return pl.pallas_call(
        body, name=name, grid=(n_pairs, nb), in_specs=in_specs, out_specs=out_specs, out_shape=out_shape,
        scratch_shapes=scratch, input_output_aliases=aliases, compiler_params=_params(2),
    )(*ins)


def _mid_bwd(dq_full, dkr, cos_t, sin_t, dcq, dck, flog, bf_row, *, tm):
    t = dq_full.shape[0]
    n = t // tm

    def body(dq_ref, dkr_ref, cos_ref, sin_ref, dcq_ref, dck_ref, fl_ref, bf_ref,
             dq2_ref, dkraw_ref, dfl_ref, dbf_ref, carry_ref):
        i = pl.program_id(0)

        @pl.when(i == 0)
        def _():
            carry_ref[...] = jnp.zeros_like(carry_ref)
            dbf_ref[...] = jnp.zeros_like(dbf_ref)

        c, s = cos_ref[...], sin_ref[...]
        dkr_sum = jnp.zeros((tm, LANE), F32)
        for h in range(N_HEADS):
            dq2_ref[:, 256 * h:256 * h + 128] = dq_ref[:, 256 * h:256 * h + 128].astype(BF16)
            dq2_ref[:, 256 * h + 128:256 * h + 256] = _rot_bwd(dq_ref[:, 256 * h + 128:256 * h + 256], c, s).astype(BF16)
            dkr_sum = dkr_sum + dkr_ref[:, HEAD * h:HEAD * (h + 1)]
        dkraw_ref[...] = _rot_bwd(dkr_sum, c, s).astype(BF16)

        dc = dcq_ref[...]
        lane = lax.broadcasted_iota(jnp.int32, (tm, LANE), 1)
        for h in range(N_HEADS):
            dc = dc + jnp.where(lane == h, dck_ref[h], 0.0)
        row = lax.broadcasted_iota(jnp.int32, (tm, tm), 0)
        col = lax.broadcasted_iota(jnp.int32, (tm, tm), 1)
        tri = (col >= row).astype(BF16)
        acc = carry_ref[0:1, :]
        for part in _split3(dc):
            acc = acc + jnp.dot(tri, part, preferred_element_type=F32)
        carry_ref[0:1, :] = carry_ref[0:1, :] + jnp.sum(dc, axis=0, keepdims=True)
        z = fl_ref[...] + bf_ref[...]
        dz = acc / (1.0 + jnp.exp(z))
        dfl_ref[...] = dz.astype(BF16)
        dbf_ref[...] += jnp.sum(dz, axis=0, keepdims=True)

    rev = lambda w: pl.BlockSpec((tm, w), lambda i: (n - 1 - i, 0))
    vec = lambda w: pl.BlockSpec((1, w), lambda i: (0, 0))
    return pl.pallas_call(
        body, name="mid_bwd", grid=(n,),
        in_specs=[rev(2048), rev(1024), rev(LANE), rev(LANE), rev(LANE),
                  pl.BlockSpec((N_HEADS, tm, 1), lambda i: (0, n - 1 - i, 0)), rev(LANE), vec(LANE)],
        out_specs=[rev(2048), rev(LANE), rev(LANE), vec(LANE)],
        out_shape=[jax.ShapeDtypeStruct((t, 2048), BF16), jax.ShapeDtypeStruct((t, LANE), BF16),
                   jax.ShapeDtypeStruct((t, LANE), BF16), jax.ShapeDtypeStruct((1, LANE), F32)],
        scratch_shapes=[pltpu.VMEM((8, LANE), F32)],
        compiler_params=_params(1),
    )(dq_full, dkr, cos_t, sin_t, dcq, dck, flog, bf_row)


def _norm_bwd(proj, dqn, dkvn, g_q, g_kv, dkr_raw, dfl, dproj, *, tm):
    t = proj.shape[0]
    assert (KR0, FL0, KVL0, LAT_W) == (Q_RANK, Q_RANK + LANE, Q_RANK + 2 * LANE, Q_RANK + 2 * LANE + KV_RANK)

    def body(p_ref, dqn_ref, dkvn_ref, gq_ref, gkv_ref, dkr_ref, dfl_ref, _, dproj_ref, dgq_ref, dgkv_ref):
        i = pl.program_id(0)

        @pl.when(i == 0)
        def _():
            dgq_ref[...] = jnp.zeros_like(dgq_ref)
            dgkv_ref[...] = jnp.zeros_like(dgkv_ref)

        d_lat = []
        for lo, w, dn_ref, g_ref, dg_ref in ((QL0, Q_RANK, dqn_ref, gq_ref, dgq_ref),
                                             (KVL0, KV_RANK, dkvn_ref, gkv_ref, dgkv_ref)):
            xv = p_ref[:, lo:lo + w].astype(F32)
            r = lax.rsqrt(jnp.mean(xv * xv, axis=-1, keepdims=True) + NORM_EPS)
            xh = xv * r
            dn = dn_ref[...]
            dg_ref[...] += jnp.sum(dn * xh, axis=0, keepdims=True)
            dxh = dn * g_ref[...]
            d_lat.append((r * (dxh - xh * jnp.mean(dxh * xh, axis=-1, keepdims=True))).astype(BF16))
        dproj_ref[...] = jnp.concatenate([d_lat[0], dkr_ref[...], dfl_ref[...], d_lat[1]], axis=1)

    row = lambda w: pl.BlockSpec((tm, w), lambda i: (i, 0))
    vec = lambda w: pl.BlockSpec((1, w), lambda i: (0, 0))
    return pl.pallas_call(
        body, name="norm_bwd", grid=(t // tm,),
        in_specs=[row(LAT_W), row(Q_RANK), row(KV_RANK), vec(Q_RANK), vec(KV_RANK), row(LANE), row(LANE),
                  pl.BlockSpec(memory_space=pl.ANY)],
        out_specs=[row(LAT_W), vec(Q_RANK), vec(KV_RANK)],
        out_shape=[jax.ShapeDtypeStruct(dproj.shape, dproj.dtype),
                   jax.ShapeDtypeStruct((1, Q_RANK), F32), jax.ShapeDtypeStruct((1, KV_RANK), F32)],
        input_output_aliases={7: 0},
        compiler_params=_params(1),
    )(proj, dqn, dkvn, g_q, g_kv, dkr_raw, dfl, dproj)


def _prenorm_bwd(dh, x, g, dy, *, tm):
    t = x.shape[0]

    def body(dh_ref, x_ref, g_ref, dy_ref, gx_ref, dg_ref):
        i = pl.program_id(0)

        @pl.when(i == 0)
        def _():
            dg_ref[...] = jnp.zeros_like(dg_ref)

        xv = x_ref[...]
        r = lax.rsqrt(jnp.mean(xv * xv, axis=-1, keepdims=True) + NORM_EPS)
        xh = xv * r
        dn = dh_ref[...]
        dg_ref[...] += jnp.sum(dn * xh, axis=0, keepdims=True)
        dxh = dn * g_ref[...]
        gx_ref[...] = dy_ref[...] + r * (dxh - xh * jnp.mean(dxh * xh, axis=-1, keepdims=True))

    row = pl.BlockSpec((tm, D_MODEL), lambda i: (i, 0))
    vec = pl.BlockSpec((1, D_MODEL), lambda i: (0, 0))
    return pl.pallas_call(
        body, name="prenorm_bwd", grid=(t // tm,),
        in_specs=[row, row, vec, row], out_specs=[row, vec],
        out_shape=[jax.ShapeDtypeStruct((t, D_MODEL), F32), jax.ShapeDtypeStruct((1, D_MODEL), F32)],
        compiler_params=_params(1),
    )(dh, x, g, dy)


def _adam_math(w, g, m, v):
    m = ADAM_B1 * m + (1.0 - ADAM_B1) * g
    v = ADAM_B2 * v + (1.0 - ADAM_B2) * (g * g)
    m_hat = m / (1.0 - ADAM_B1 ** ADAM_STEP)
    v_hat = v / (1.0 - ADAM_B2 ** ADAM_STEP)
    delta = -ADAM_LR * (m_hat / (jnp.sqrt(v_hat) + ADAM_EPS) + ADAM_WD * w)
    return delta, m, v


def _adamw(land, w, m, v, *, tr, name):
    rows, cols = w.shape

    def body(l_ref, w_ref, m_ref, v_ref, g_ref, d_ref, nm_ref, nv_ref):
        g = l_ref[0].astype(F32)
        for s in range(1, N_CHIPS):
            g = g + l_ref[s].astype(F32)
        g_ref[...] = g
        d_ref[...], nm_ref[...], nv_ref[...] = _adam_math(w_ref[...], g, m_ref[...], v_ref[...])

    blk = pl.BlockSpec((tr, cols), lambda i: (i, 0))
    return pl.pallas_call(
        body, name=name, grid=(rows // tr,),
        in_specs=[pl.BlockSpec((N_CHIPS, tr, cols), lambda i: (0, i, 0)), blk, blk, blk],
        out_specs=[blk, blk, blk, blk],
        out_shape=[jax.ShapeDtypeStruct((rows, cols), F32)] * 4,
        compiler_params=_params(1),
    )(land, w, m, v)


def _adamw_small(gathered, w, m, v):
    def body(a_ref, w_ref, m_ref, v_ref, g_ref, d_ref, nm_ref, nv_ref):
        g = a_ref[0:SMALL_ROWS, :]
        for s in range(1, N_DEV):
            g = g + a_ref[SMALL_ROWS * s:SMALL_ROWS * (s + 1), :]
        g_ref[...] = g
        d_ref[...], nm_ref[...], nv_ref[...] = _adam_math(w_ref[...], g, m_ref[...], v_ref[...])

    return pl.pallas_call(
        body, name="adamw_small",
        out_shape=[jax.ShapeDtypeStruct((SMALL_ROWS, SMALL_COLS), F32)] * 4,
        compiler_params=_params(),
    )(gathered, w, m, v)


def _place():
    x, y, c = lax.axis_index("x"), lax.axis_index("y"), lax.axis_index("c")
    return x, y, c


def _flip(p, k):
    x, y, c = p
    return (1 - x if k & 4 else x, 1 - y if k & 2 else y, 1 - c if k & 1 else c)


def _index(p):
    return 4 * p[0] + 2 * p[1] + p[2]


class _AllGather:
    def __init__(self, shard):
        assert shard.shape[0] % 32 == 0
        self.half = shard.shape[0] // 2
        self.operands = [shard]
        self.out_shape = [jax.ShapeDtypeStruct((N_DEV,) + shard.shape, shard.dtype)]
        self.aliases = {}
        self.scratch = [pltpu.SemaphoreType.DMA((9,)), pltpu.SemaphoreType.DMA((9,)), pltpu.SemaphoreType.DMA(())]

    def _parts(self, ins, outs, scratch):
        (in_ref,), (out_ref,), (send_sems, recv_sems, local_sem) = ins, outs, scratch
        me = _place()

        def copy(k, block, to, part=None, src=None):
            dst = out_ref.at[_index(block)] if part is None else out_ref.at[_index(block), part]
            return pltpu.make_async_remote_copy(
                src_ref=dst if src is None else src, dst_ref=dst, send_sem=send_sems.at[k], recv_sem=recv_sems.at[k],
                device_id=to, device_id_type=MESH)

        mine = pltpu.make_async_copy(in_ref, out_ref.at[_index(me)], local_sem)
        first = [copy(0, me, _flip(me, 1), src=in_ref), copy(1, me, _flip(me, 4), src=in_ref),
                 copy(2, me, _flip(me, 2), src=in_ref)]
        return me, copy, mine, first

    def start(self, ins, outs, scratch):
        _, _, mine, first = self._parts(ins, outs, scratch)
        mine.start()
        for cp in first:
            cp.start()

    def wait(self, ins, outs, scratch):
        me, copy, mine, sent = self._parts(ins, outs, scratch)
        sibling, x_nbr, y_nbr, diagonal = _flip(me, 1), _flip(me, 4), _flip(me, 2), _flip(me, 6)
        top, bottom = pl.ds(0, self.half), pl.ds(self.half, self.half)
        arrivals = [(1, x_nbr, None, [(3, sibling, None), (5, y_nbr, top)]),
                    (2, y_nbr, None, [(4, sibling, None), (6, x_nbr, bottom)]),
                    (5, diagonal, top, [(7, sibling, top)]),
                    (6, diagonal, bottom, [(8, sibling, bottom)])]
        for k, block, part, onward in arrivals:
            copy(k, block, me, part).wait_recv()
            for k_on, to, part_on in onward:
                cp = copy(k_on, block, to, part_on)
                cp.start()
                sent.append(cp)
        other = lambda p: _flip(p, 1)
        for k, block, part in ((0, sibling, None), (3, other(x_nbr), None), (4, other(y_nbr), None),
                               (7, other(diagonal), top), (8, other(diagonal), bottom)):
            copy(k, block, me, part).wait_recv()
        for cp in sent:
            cp.wait_send()
        mine.wait()


class _Exchange:
    def __init__(self, tasks):
        self.tasks = tasks
        taken = [land for _, _, land, _, _ in tasks if land is not None]
        self.operands = [src for src, _, _, _, _ in tasks] + taken
        self.out_shape = [
            jax.ShapeDtypeStruct((N_CHIPS,) + ((2,) if by_core else ()) + (src.shape if same else src.shape[1:]), src.dtype)
            for src, _, _, same, by_core in tasks]
        self.aliases, n_taken = {}, 0
        for a, (_, _, land, _, _) in enumerate(tasks):
            if land is not None:
                self.aliases[len(tasks) + n_taken] = a
                n_taken += 1
        self.scratch = [pltpu.SemaphoreType.DMA((N_CHIPS,)), pltpu.SemaphoreType.DMA((N_CHIPS,)),
                        pltpu.SemaphoreType.DMA(())] * len(tasks)

    def _copies(self, ins, outs, scratch):
        x, y, core = _place()
        my = 2 * x + y
        for a, (_, chips, _, same, by_core) in enumerate(self.tasks):
            send_sems, recv_sems, local_sem = scratch[3 * a:3 * a + 3]
            slot = (lambda s, a=a, by_core=by_core: outs[a].at[s, core] if by_core else outs[a].at[s])
            for i, j in enumerate(chips):
                src = ins[a] if same else ins[a].at[i]
                pair = jnp.bitwise_xor(my, j)
                remote = pltpu.make_async_remote_copy(
                    src_ref=src, dst_ref=slot(my), send_sem=send_sems.at[pair], recv_sem=recv_sems.at[pair],
                    device_id=(j >> 1, j & 1, core), device_id_type=MESH)
                local = pltpu.make_async_copy(src, slot(my), local_sem)
                yield j, my, core, remote, local, slot, (send_sems, recv_sems)

    def start(self, ins, outs, scratch):
        for j, my, _, remote, local, _, _ in self._copies(ins, outs, scratch):
            pl.when(my != j)(remote.start)
            pl.when(my == j)(local.start)

    def wait(self, ins, outs, scratch):
        for j, my, core, remote, local, slot, (send_sems, recv_sems) in self._copies(ins, outs, scratch):
            pl.when(my != j)(remote.wait_send)

            @pl.when(my == j)
            def _():
                local.wait()
                for s in range(N_CHIPS):
                    if s != j:
                        pltpu.make_async_remote_copy(
                            src_ref=slot(s), dst_ref=slot(s), send_sem=send_sems.at[j ^ s], recv_sem=recv_sems.at[j ^ s],
                            device_id=(s >> 1, s & 1, core), device_id_type=MESH).wait_recv()


N_CHIPS = 4
ALL_CHIPS = tuple(range(N_CHIPS))


class _ToOtherCore:
    def __init__(self, parts):
        self.operands = list(parts)
        self.out_shape = [jax.ShapeDtypeStruct(p.shape[1:], p.dtype) for p in parts]
        self.aliases = {}
        self.scratch = [pltpu.SemaphoreType.DMA((len(parts),)), pltpu.SemaphoreType.DMA((len(parts),))]

    def _copies(self, ins, outs, scratch):
        send_sems, recv_sems = scratch
        me = _place()
        return [pltpu.make_async_remote_copy(src_ref=ins[a].at[1 - me[2]], dst_ref=outs[a], send_sem=send_sems.at[a],
                                             recv_sem=recv_sems.at[a], device_id=_flip(me, 1), device_id_type=MESH)
                for a in range(len(ins))]

    def start(self, ins, outs, scratch):
        for cp in self._copies(ins, outs, scratch):
            cp.start()

    def wait(self, ins, outs, scratch):
        for cp in self._copies(ins, outs, scratch):
            cp.wait()


def _to_other_core(parts, *, name):
    swap = _ToOtherCore(parts)
    n_arr = len(parts)
    hbm = pl.BlockSpec(memory_space=pl.ANY)

    def body(*refs):
        rider_refs = (refs[:n_arr], refs[n_arr:2 * n_arr], refs[2 * n_arr:])
        swap.start(*rider_refs)
        swap.wait(*rider_refs)

    return pl.pallas_call(
        body, name=name, in_specs=[hbm] * n_arr, out_specs=[hbm] * n_arr,
        out_shape=swap.out_shape, scratch_shapes=swap.scratch,
    )(*parts)


def _share_with_other_core(gathered, *, name):
    n_arr = len(gathered)
    hbm = pl.BlockSpec(memory_space=pl.ANY)

    def body(*refs):
        bufs = refs[n_arr:2 * n_arr]
        send_sems, recv_sems = refs[2 * n_arr:]
        me = _place()
        copies = []
        for a in range(n_arr):
            for j in range(N_CHIPS):
                block = bufs[a].at[j, me[2]]
                copies.append(pltpu.make_async_remote_copy(
                    src_ref=block, dst_ref=block, send_sem=send_sems.at[N_CHIPS * a + j],
                    recv_sem=recv_sems.at[N_CHIPS * a + j], device_id=_flip(me, 1), device_id_type=MESH))
        for cp in copies:
            cp.start()
        for cp in copies:
            cp.wait()

    return pl.pallas_call(
        body, name=name, in_specs=[hbm] * n_arr, out_specs=[hbm] * n_arr,
        out_shape=[jax.ShapeDtypeStruct(g.shape, g.dtype) for g in gathered],
        scratch_shapes=[pltpu.SemaphoreType.DMA((N_CHIPS * n_arr,)), pltpu.SemaphoreType.DMA((N_CHIPS * n_arr,))],
        input_output_aliases={a: a for a in range(n_arr)},
    )(*gathered)


def _pair_sum(mine, other, core, *, tr, name):
    _, n, rows, cols = mine.shape
    tr = min(tr, rows)

    def body(core_ref, a_ref, b_ref, o_ref):
        o_ref[...] = (a_ref[0].astype(F32) + b_ref[...].astype(F32)).astype(BF16)

    return pl.pallas_call(
        body, name=name,
        grid_spec=pltpu.PrefetchScalarGridSpec(
            num_scalar_prefetch=1, grid=(n, rows // tr),
            in_specs=[pl.BlockSpec((1, 1, tr, cols), lambda j, i, core_ref: (core_ref[0], j, i, 0)),
                      pl.BlockSpec((1, tr, cols), lambda j, i, core_ref: (j, i, 0))],
            out_specs=pl.BlockSpec((1, tr, cols), lambda j, i, core_ref: (j, i, 0))),
        out_shape=jax.ShapeDtypeStruct(other.shape, BF16),
        compiler_params=_params(2),
    )(core, mine, other)


def _gather_small(vec):
    def body(v_ref, out_ref, send_sems, recv_sems, local_sem):
        me = _place()

        def rows(p):
            return out_ref.at[pl.ds(pl.multiple_of(_index(p) * SMALL_ROWS, SMALL_ROWS), SMALL_ROWS), :]

        mine = pltpu.make_async_copy(v_ref, rows(me), local_sem)
        mine.start()
        sends = []
        for k in range(1, N_DEV):
            peer = _flip(me, k)
            cp = pltpu.make_async_remote_copy(src_ref=v_ref, dst_ref=rows(me), send_sem=send_sems.at[k - 1],
                                              recv_sem=recv_sems.at[k - 1], device_id=peer, device_id_type=MESH)
            cp.start()
            sends.append(cp)
        for k in range(1, N_DEV):
            peer = _flip(me, k)
            pltpu.make_async_remote_copy(src_ref=rows(peer), dst_ref=rows(peer), send_sem=send_sems.at[k - 1],
                                         recv_sem=recv_sems.at[k - 1], device_id=peer, device_id_type=MESH).wait_recv()
        for cp in sends:
            cp.wait_send()
        mine.wait()

    return pl.pallas_call(
        body, name="gather_small",
        in_specs=[pl.BlockSpec(memory_space=pltpu.VMEM)], out_specs=pl.BlockSpec(memory_space=pltpu.VMEM),
        out_shape=jax.ShapeDtypeStruct((N_DEV * SMALL_ROWS, SMALL_COLS), F32),
        scratch_shapes=[pltpu.SemaphoreType.DMA((7,)), pltpu.SemaphoreType.DMA((7,)), pltpu.SemaphoreType.DMA],
    )(vec)


def _w_in_nice(gathered):
    pieces, pos = [], 0
    for o0, width, n0 in sorted(_SEGMENTS, key=lambda seg: seg[2]):
        if n0 > pos:
            pieces.append(jnp.zeros((D_MODEL, n0 - pos), gathered.dtype))
        o = o0
        while o < o0 + width:
            d = o // SHARD_IN
            hi = min(o0 + width, (d + 1) * SHARD_IN)
            pieces.append(gathered[d][:, o - d * SHARD_IN:hi - d * SHARD_IN])
            o = hi
        pos = n0 + width
    pieces.append(jnp.zeros((D_MODEL, NP_IN - pos), gathered.dtype))
    return jnp.concatenate(pieces, axis=1)


def _w_in_blocks(chips, dw_lat, dw_rest):
    blocks = []
    for core in range(2):
        for chip in chips:
            lo = (2 * chip + core) * SHARD_IN
            runs = []
            for o0, width, n0 in _SEGMENTS:
                a, b = max(lo, o0), min(lo + SHARD_IN, o0 + width)
                if a < b:
                    n_a, n_b = n0 + a - o0, n0 + b - o0
                    runs.append(dw_lat[:, n_a:n_b] if n_b <= LAT_W else dw_rest[:, n_a - LAT_W:n_b - LAT_W])
            blocks.append(jnp.concatenate(runs, axis=1))
    return jnp.stack(blocks).reshape(2, len(chips), D_MODEL, SHARD_IN)


def _by_core(shards):
    return shards.reshape((N_CHIPS, 2) + shards.shape[1:]).swapaxes(0, 1)


EARLY_CHIPS = (1, 2)
LATE_CHIPS = (0, 3)


def _w_uq_nice(shard):
    z = jnp.zeros((Q_RANK, 32), shard.dtype)
    return jnp.concatenate([shard[:, :128], shard[:, 128:160], z, shard[:, 160:192], z], axis=1)


def _pack_small(g_pre, g_post, g_q, g_kv, b_f, extra=None):
    parts = [g_pre.reshape(-1), g_post.reshape(-1), g_q.reshape(-1), g_kv.reshape(-1), b_f.reshape(-1)]
    if extra is not None:
        parts.append(extra.reshape(-1))
    flat = jnp.concatenate(parts)
    flat = jnp.concatenate([flat, jnp.zeros((SMALL_ROWS * SMALL_COLS - flat.shape[0],), F32)])
    return flat.reshape(SMALL_ROWS, SMALL_COLS)


def _unpack_small(packed):
    flat = packed.reshape(-1)
    o = 0
    out = []
    for n in (D_MODEL, D_MODEL, Q_RANK, KV_RANK, N_HEADS):
        out.append(flat[o:o + n].reshape(1, n))
        o += n
    return out, flat[o]


def kernel(x, positions, g_pre, w_in, g_q_latent, w_uq, g_kv_latent, w_ukv, b_forget, w_out, g_post, loss_target, m_g_pre, m_w_in, m_g_q_latent, m_w_uq, m_g_kv_latent, m_w_ukv, m_b_forget, m_w_out, m_g_post, v_g_pre, v_w_in, v_g_q_latent, v_w_uq, v_g_kv_latent, v_w_ukv, v_b_forget, v_w_out, v_g_post):
    t = x.shape[1]
    tb = min(512, t)
    tm = min(256, t)
    nb = t // tb
    x2 = x.reshape(t, D_MODEL)
    target = loss_target.reshape(t, D_MODEL)
    pos_col = positions.reshape(t, 1).astype(F32)
    bf_row = jnp.concatenate([b_forget.reshape(1, N_HEADS), jnp.zeros((1, LANE - N_HEADS), F32)], axis=1)

    h, h_t, g_in = _prenorm(x2, g_pre, _AllGather(w_in[0].astype(BF16)), tm=tm)
    w_in_n = _w_in_nice(g_in)
    gather_rest = _Exchange([(w, ALL_CHIPS, None, True, True) for w in
                             (_w_uq_nice(w_uq[0].astype(BF16)), w_ukv[0].astype(BF16), w_out[0].astype(BF16))])
    core = lax.axis_index("c").astype(jnp.int32).reshape(1)
    proj, g_uq, g_ukv, g_out = _mm(h, w_in_n, name="proj_in", out_dtype=BF16, tm=2048, tn=512, tk=2048,
                                   rider=gather_rest)
    g_uq, g_ukv, g_out = _share_with_other_core([g_uq, g_ukv, g_out], name="share_weights")
    w_uq_n = g_uq.reshape(N_DEV, Q_RANK, 256).transpose(1, 0, 2).reshape(Q_RANK, N_HEADS * 256)
    w_ukv_n = g_ukv.reshape(N_DEV, KV_RANK, 256).transpose(1, 0, 2).reshape(KV_RANK, N_HEADS * 256)
    w_out_n = g_out.reshape(D_MODEL, D_MODEL)
    flog = _mm(h, w_in_n[:, FL0:FL0 + LANE], name="proj_flog", out_dtype=F32, tm=1024, tn=LANE, tk=2048)
    qn, kvn, kr, cos_t, sin_t, c, qn_t, kvn_t = _mid_fwd(proj, flog, g_q_latent, g_kv_latent, bf_row, pos_col, tm=tm)
    q_full = _q_up_rope(qn, w_uq_n, cos_t, sin_t, tm=min(1024, t))
    kv = _mm(kvn, w_ukv_n, name="kv_up", out_dtype=BF16, tm=1024, tn=512, tk=KV_RANK)
    c_heads = c[:, :N_HEADS].T
    c_col = c_heads.reshape(N_HEADS, t, 1)
    c_row4 = c_heads.reshape(N_HEADS, nb, 1, tb)
    o_all, og_all, og_t, lse4_mla = _attn_fwd(False, (q_full, kv, kr, proj), t=t, tb=tb, name="mla_fwd")
    o_all, og_all, og_t, lse4_fox = _attn_fwd(True, (proj, c_col, c_row4, o_all, og_all, og_t), t=t, tb=tb,
                                              name="fox_fwd")
    dy, d_o_post, dg_post, loss_part = _out_norm_loss(og_all, w_out_n, x2, target, g_post, tm=min(512, t))

    dw_out = _mm(og_t, d_o_post, name="dw_out", out_dtype=BF16, tm=1024, tn=1024, tk=1024)
    p_out = _by_core(dw_out.reshape(N_DEV, D_MODEL // N_DEV, D_MODEL))
    d_attn, dproj, delta, o_out = _dog_gate(d_o_post, w_out_n, o_all, proj, _ToOtherCore([p_out]), tm=min(1024, t))
    s_out = _pair_sum(p_out, o_out, core, tr=256, name="dw_out_pair_sum")
    delta4 = delta[:, :2 * N_HEADS].T.reshape(2 * N_HEADS, nb, 1, tb)
    dproj, dck, dcq, l_out = _attn_bwd(True, (proj, d_attn, lse4_fox, delta4[N_HEADS:], c_row4, c_col, dproj),
                                       t=t, tb=tb, name="fox_bwd",
                                       rider=_Exchange([(s_out, ALL_CHIPS, None, False, False)]))
    dw_in_rest = _mm(h_t, dproj, name="dw_in_rest", out_dtype=BF16, tm=2048, tn=512, tk=1024,
                     b_cols=(LAT_W, NP_IN - LAT_W))
    p_in = _w_in_blocks(EARLY_CHIPS, None, dw_in_rest)
    (o_in,) = _to_other_core([p_in], name="dw_in_early_to_core")
    s_in = _pair_sum(p_in, o_in, core, tr=256, name="dw_in_early_pair_sum")
    dq_full, dkv, dkr, l_in = _attn_bwd(False, (q_full, kv, kr, d_attn, lse4_mla, delta4[:N_HEADS]),
                                        t=t, tb=tb, name="mla_bwd",
                                        rider=_Exchange([(s_in, EARLY_CHIPS, None, False, False)]))
    dcq_rows = jnp.concatenate([dcq.reshape(N_HEADS, t).T, jnp.zeros((t, LANE - N_HEADS), F32)], axis=1)
    dq2, dkr_raw, dfl, dbf = _mid_bwd(dq_full, dkr, cos_t, sin_t, dcq_rows, dck, flog, bf_row, tm=tm)
    dqn = _mm(dq2, w_uq_n, name="d_qn", nt=True, out_dtype=F32, tm=1024, tn=Q_RANK, tk=2048)
    dkvn = _mm(dkv, w_ukv_n, name="d_kvn", nt=True, out_dtype=F32, tm=1024, tn=KV_RANK, tk=2048)
    dw_uq = _mm(qn_t, dq2, name="dw_uq", out_dtype=BF16, tm=Q_RANK, tn=1024, tk=1024)
    dw_ukv = _mm(kvn_t, dkv, name="dw_ukv", out_dtype=BF16, tm=KV_RANK, tn=1024, tk=1024)
    dproj, dg_q, dg_kv = _norm_bwd(proj, dqn, dkvn, g_q_latent, g_kv_latent, dkr_raw, dfl, dproj, tm=tm)
    dw_uq_h = dw_uq.reshape(Q_RANK, N_HEADS, 256)
    s_uq = jnp.concatenate([dw_uq_h[:, :, :160], dw_uq_h[:, :, 192:224]], axis=2).transpose(1, 0, 2)
    s_ukv = dw_ukv.reshape(KV_RANK, N_HEADS, 256).transpose(1, 0, 2)
    up_parts = [_by_core(s_uq), _by_core(s_ukv)]
    up_other = _to_other_core(up_parts, name="dw_up_to_core")
    up_sums = [_pair_sum(p, o_, core, tr=256, name=f"dw_up_pair_sum_{i}") for i, (p, o_) in enumerate(zip(up_parts, up_other))]
    dw_in_lat, l_uq, l_ukv = _mm(h_t, dproj, name="dw_in_lat", out_dtype=BF16, tm=1024, tn=LAT_W, tk=1024,
                                 b_cols=(0, LAT_W), rider=_Exchange([(s, ALL_CHIPS, None, False, False) for s in up_sums]))
    p_late = _w_in_blocks(LATE_CHIPS, dw_in_lat, dw_in_rest)
    (o_late,) = _to_other_core([p_late], name="dw_in_late_to_core")
    s_late = _pair_sum(p_late, o_late, core, tr=256, name="dw_in_late_pair_sum")
    dh, l_in = _mm(dproj, w_in_n, name="d_h", nt=True, out_dtype=F32, tm=2048, tn=1024, tk=NP_IN // 4,
                   rider=_Exchange([(s_late, LATE_CHIPS, l_in, False, False)]))
    grad_x, dg_pre = _prenorm_bwd(dh, x2, g_pre, dy, tm=tm)

    small = _gather_small(_pack_small(dg_pre, dg_post, dg_q, dg_kv, dbf[:, :N_HEADS], loss_part))

    res_in = _adamw(l_in, w_in[0], m_w_in[0], v_w_in[0], tr=256, name="adamw_w_in")
    res_uq = _adamw(l_uq, w_uq[0], m_w_uq[0], v_w_uq[0], tr=256, name="adamw_w_uq")
    res_ukv = _adamw(l_ukv, w_ukv[0], m_w_ukv[0], v_w_ukv[0], tr=256, name="adamw_w_ukv")
    res_out = _adamw(l_out, w_out[0], m_w_out[0], v_w_out[0], tr=128, name="adamw_w_out")
    res_small = _adamw_small(
        small,
        _pack_small(g_pre, g_post, g_q_latent, g_kv_latent, b_forget),
        _pack_small(m_g_pre, m_g_post, m_g_q_latent, m_g_kv_latent, m_b_forget),
        _pack_small(v_g_pre, v_g_post, v_g_q_latent, v_g_kv_latent, v_b_forget))
    small_out = [_unpack_small(r) for r in res_small]
    loss = small_out[0][1]

    def leaves(kind):
        (s_pre, s_post, s_q, s_kv, s_bf), _ = small_out[kind]
        return [s_pre, res_in[kind][None], s_q, res_uq[kind][None], s_kv, res_ukv[kind][None], s_bf,
                res_out[kind][None], s_post]

    return (loss, grad_x.reshape(x.shape), *leaves(0), *leaves(1), *leaves(2), *leaves(3))
```

```python
import functools

import numpy as np
import jax
import jax.numpy as jnp
from jax import lax
from jax.experimental import pallas as pl
from jax.experimental.pallas import tpu as pltpu

F32 = jnp.float32
BF16 = jnp.bfloat16
MESH = pl.DeviceIdType.MESH

N_DEV = 8
D_MODEL = 2048
N_HEADS = 8
HEAD = 128
Q_RANK = 768
KV_RANK = 512
ROPE = 64
D_IN = 6472
SHARD_IN = D_IN // N_DEV
NORM_EPS = 1e-6
ROPE_THETA = 10000.0
MLA_SCALE = (HEAD + ROPE) ** -0.5
FOX_SCALE = HEAD ** -0.5

QL0, KR0, FL0, KVL0, GM0, GF0, FQ0, FK0, FV0, NP_IN = 0, 768, 896, 1024, 1536, 2560, 3584, 4608, 5632, 6656
LAT_W = GM0
LANE = 128
_SEGMENTS = ((0, 768, QL0), (768, 512, KVL0), (1280, 32, KR0), (1312, 32, KR0 + 64), (1344, 1024, GM0),
             (2368, 3072, FQ0), (5440, 8, FL0), (5448, 1024, GF0))
LOG2E = 1.4426950408889634

ADAM_LR = 0.001
ADAM_B1 = 0.9
ADAM_B2 = 0.999
ADAM_EPS = 1e-08
ADAM_WD = 0.01
ADAM_STEP = 10

VMEM_LIMIT_BYTES = 56 * 1024 * 1024
SMALL_ROWS, SMALL_COLS = 8, 768


def _params(n_grid=0):
    return pltpu.CompilerParams(vmem_limit_bytes=VMEM_LIMIT_BYTES,
                                dimension_semantics=("arbitrary",) * n_grid if n_grid else None)


def _sigmoid(z):
    return 1.0 / (1.0 + jnp.exp(-z))


def _split3(v):
    a = v.astype(BF16)
    r = v - a.astype(F32)
    b = r.astype(BF16)
    c = (r - b.astype(F32)).astype(BF16)
    return a, b, c


def _mm(a, b, *, name, nt=False, out_dtype=F32, tm=1024, tn=512, tk=2048, b_cols=None, rider=None):
    m, k_dim = a.shape
    n = b.shape[0] if nt else b.shape[1]
    col0 = 0
    if b_cols is not None:
        assert not nt
        col0, n = b_cols
    assert (b.shape[1] if nt else b.shape[0]) == k_dim
    tm, tn, tk = min(tm, m), min(tn, n), min(tk, k_dim)
    assert m % tm == 0 and n % tn == 0 and k_dim % tk == 0 and col0 % tn == 0, (name, a.shape, b.shape)
    nk = k_dim // tk
    j0 = col0 // tn
    grid = (m // tm, n // tn, nk)
    dims = (((1,), (1 if nt else 0,)), ((), ()))
    n_rin = len(rider.operands) if rider else 0
    n_rout = len(rider.out_shape) if rider else 0

    def body(*refs):
        a_ref, b_ref = refs[:2]
        o_ref = refs[2 + n_rin]
        acc_ref = refs[3 + n_rin + n_rout]
        i, j, k = pl.program_id(0), pl.program_id(1), pl.program_id(2)
        if rider:
            rider_refs = (refs[2:2 + n_rin], refs[3 + n_rin:3 + n_rin + n_rout], refs[4 + n_rin + n_rout:])

            @pl.when(jnp.logical_and(i == 0, jnp.logical_and(j == 0, k == 0)))
            def _():
                rider.start(*rider_refs)

        @pl.when(k == 0)
        def _():
            acc_ref[...] = jnp.zeros_like(acc_ref)

        acc_ref[...] += lax.dot_general(a_ref[...], b_ref[...], dims, preferred_element_type=F32)

        @pl.when(k == nk - 1)
        def _():
            o_ref[...] = acc_ref[...].astype(o_ref.dtype)

        if rider:
            @pl.when(jnp.logical_and(i == grid[0] - 1, jnp.logical_and(j == grid[1] - 1, k == nk - 1)))
            def _():
                rider.wait(*rider_refs)

    b_spec = (pl.BlockSpec((tn, tk), lambda i, j, k: (j, k)) if nt
              else pl.BlockSpec((tk, tn), lambda i, j, k: (k, j0 + j)))
    a_spec = pl.BlockSpec((tm, tk), lambda i, j, k: (i, k))
    any_spec = pl.BlockSpec(memory_space=pl.ANY)
    out = pl.pallas_call(
        body, name=name, grid=grid,
        in_specs=[a_spec, b_spec] + [any_spec] * n_rin,
        out_specs=[pl.BlockSpec((tm, tn), lambda i, j, k: (i, j))] + [any_spec] * n_rout,
        out_shape=[jax.ShapeDtypeStruct((m, n), out_dtype)] + (list(rider.out_shape) if rider else []),
        scratch_shapes=[pltpu.VMEM((tm, tn), F32)] + (list(rider.scratch) if rider else []),
        input_output_aliases={2 + i_in: 1 + i_out for i_in, i_out in rider.aliases.items()} if rider else {},
        compiler_params=_params(3),
    )(a, b, *(rider.operands if rider else ()))
    return out if rider else out[0]


def _prenorm(x, g, rider, *, tm):
    t = x.shape[0]
    n_steps = t // tm
    n_rin, n_rout = len(rider.operands), len(rider.out_shape)

    def body(*refs):
        x_ref, g_ref = refs[:2]
        h_ref, ht_ref = refs[2 + n_rin:4 + n_rin]
        rider_refs = (refs[2:2 + n_rin], refs[4 + n_rin:4 + n_rin + n_rout], refs[4 + n_rin + n_rout:])
        i = pl.program_id(0)
        pl.when(i == 0)(lambda: rider.start(*rider_refs))
        xv = x_ref[...]
        r = lax.rsqrt(jnp.mean(xv * xv, axis=-1, keepdims=True) + NORM_EPS)
        h = xv * r * g_ref[...]
        h_ref[...] = h.astype(BF16)
        ht_ref[...] = h.T.astype(BF16)
        pl.when(i == n_steps - 1)(lambda: rider.wait(*rider_refs))

    any_spec = pl.BlockSpec(memory_space=pl.ANY)
    return pl.pallas_call(
        body, name="prenorm", grid=(n_steps,),
        in_specs=[pl.BlockSpec((tm, D_MODEL), lambda i: (i, 0)), pl.BlockSpec((1, D_MODEL), lambda i: (0, 0))]
        + [any_spec] * n_rin,
        out_specs=[pl.BlockSpec((tm, D_MODEL), lambda i: (i, 0)), pl.BlockSpec((D_MODEL, tm), lambda i: (0, i))]
        + [any_spec] * n_rout,
        out_shape=[jax.ShapeDtypeStruct((t, D_MODEL), BF16), jax.ShapeDtypeStruct((D_MODEL, t), BF16)]
        + list(rider.out_shape),
        scratch_shapes=list(rider.scratch),
        compiler_params=_params(1),
    )(x, g, *rider.operands)


def _rope_rows():
    inv = (np.float32(ROPE_THETA) ** (-np.arange(0, ROPE, 2, dtype=np.float32) / np.float32(ROPE))).astype(np.float32)
    invf = np.zeros((1, LANE), np.float32)
    sgn = np.zeros((1, LANE), np.float32)
    invf[0, 0:32] = inv
    invf[0, 64:96] = inv
    sgn[0, 0:32] = -1.0
    sgn[0, 64:96] = 1.0
    return jnp.asarray(invf), jnp.asarray(sgn)


def _rot(v, cos_t, sin_t):
    return v * cos_t + pltpu.roll(v, 64, 1) * sin_t


def _rot_bwd(dv, cos_t, sin_t):
    return dv * cos_t + pltpu.roll(dv * sin_t, 64, 1)


def _mid_fwd(proj, flog, g_q, g_kv, bf_row, pos_col, rider, *, tm):
    t = proj.shape[0]
    n_steps = t // tm
    invf, sgn = _rope_rows()
    n_rin, n_rout = len(rider.operands), len(rider.out_shape)

    def body(*refs):
        p_ref, fl_ref, gq_ref, gkv_ref, bf_ref, pos_ref, invf_ref, sgn_ref = refs[:8]
        qn_ref, kvn_ref, kr_ref, cos_ref, sin_ref, c_ref, qnt_ref, kvnt_ref = refs[8 + n_rin:16 + n_rin]
        carry_ref = refs[16 + n_rin + n_rout]
        rider_refs = (refs[8:8 + n_rin], refs[16 + n_rin:16 + n_rin + n_rout], refs[17 + n_rin + n_rout:])
        i = pl.program_id(0)
        pl.when(i == 0)(lambda: rider.start(*rider_refs))
        pl.when(i == n_steps - 1)(lambda: rider.wait(*rider_refs))

        @pl.when(i == 0)
        def _():
            carry_ref[...] = jnp.zeros_like(carry_ref)

        ql = p_ref[:, QL0:QL0 + Q_RANK].astype(F32)
        r = lax.rsqrt(jnp.mean(ql * ql, axis=-1, keepdims=True) + NORM_EPS)
        qn = ql * r * gq_ref[...]
        qn_ref[...] = qn.astype(BF16)
        qnt_ref[...] = qn.T.astype(BF16)
        kvl = p_ref[:, KVL0:KVL0 + KV_RANK].astype(F32)
        r = lax.rsqrt(jnp.mean(kvl * kvl, axis=-1, keepdims=True) + NORM_EPS)
        kvn = kvl * r * gkv_ref[...]
        kvn_ref[...] = kvn.astype(BF16)
        kvnt_ref[...] = kvn.T.astype(BF16)

        ang = pos_ref[...] * invf_ref[...]
        cos_t = jnp.cos(ang)
        sin_t = jnp.sin(ang) * sgn_ref[...]
        cos_ref[...] = cos_t
        sin_ref[...] = sin_t
        kr_ref[...] = _rot(p_ref[:, KR0:KR0 + LANE].astype(F32), cos_t, sin_t).astype(BF16)

        z = fl_ref[...] + bf_ref[...]
        logf = jnp.minimum(z, 0.0) - jnp.log(1.0 + jnp.exp(-jnp.abs(z)))
        row = lax.broadcasted_iota(jnp.int32, (tm, tm), 0)
        col = lax.broadcasted_iota(jnp.int32, (tm, tm), 1)
        tri = (col <= row).astype(BF16)
        acc = carry_ref[0:1, :]
        for part in _split3(logf):
            acc = acc + jnp.dot(tri, part, preferred_element_type=F32)
        c_ref[...] = acc * (1.0 / FOX_SCALE)
        carry_ref[0:1, :] = carry_ref[0:1, :] + jnp.sum(logf, axis=0, keepdims=True)

    row_spec = lambda w: pl.BlockSpec((tm, w), lambda i: (i, 0))
    vec_spec = lambda w: pl.BlockSpec((1, w), lambda i: (0, 0))
    return pl.pallas_call(
        body, name="mid_fwd", grid=(n_steps,),
        in_specs=[row_spec(LAT_W), row_spec(LANE), vec_spec(Q_RANK), vec_spec(KV_RANK), vec_spec(LANE),
                  pl.BlockSpec((tm, 1), lambda i: (i, 0)), vec_spec(LANE), vec_spec(LANE)]
        + [pl.BlockSpec(memory_space=pl.ANY)] * n_rin,
        out_specs=[row_spec(Q_RANK), row_spec(KV_RANK), row_spec(LANE), row_spec(LANE), row_spec(LANE), row_spec(LANE),
                   pl.BlockSpec((Q_RANK, tm), lambda i: (0, i)), pl.BlockSpec((KV_RANK, tm), lambda i: (0, i))]
        + [pl.BlockSpec(memory_space=pl.ANY)] * n_rout,
        out_shape=[jax.ShapeDtypeStruct((t, Q_RANK), BF16), jax.ShapeDtypeStruct((t, KV_RANK), BF16),
                   jax.ShapeDtypeStruct((t, LANE), BF16), jax.ShapeDtypeStruct((t, LANE), F32),
                   jax.ShapeDtypeStruct((t, LANE), F32), jax.ShapeDtypeStruct((t, LANE), F32),
                   jax.ShapeDtypeStruct((Q_RANK, t), BF16), jax.ShapeDtypeStruct((KV_RANK, t), BF16)]
        + list(rider.out_shape),
        scratch_shapes=[pltpu.VMEM((8, LANE), F32)] + list(rider.scratch),
        input_output_aliases={8 + i_in: 8 + i_out for i_in, i_out in rider.aliases.items()},
        compiler_params=_params(1),
    )(proj, flog, g_q, g_kv, bf_row, pos_col, invf, sgn, *rider.operands)


def _q_up_rope(qn, w_uq_n, cos_t, sin_t, *, tm):
    t = qn.shape[0]
    tn = 2 * 256

    def body(a_ref, b_ref, cos_ref, sin_ref, o_ref):
        q = jnp.dot(a_ref[...], b_ref[...], preferred_element_type=F32)
        c, s = cos_ref[...], sin_ref[...]
        for u in range(tn // 256):
            o_ref[:, 256 * u:256 * u + 128] = q[:, 256 * u:256 * u + 128].astype(BF16)
            o_ref[:, 256 * u + 128:256 * u + 256] = _rot(q[:, 256 * u + 128:256 * u + 256], c, s).astype(BF16)

    return pl.pallas_call(
        body, name="q_up_rope", grid=(t // tm, N_HEADS * 256 // tn),
        in_specs=[pl.BlockSpec((tm, Q_RANK), lambda i, j: (i, 0)), pl.BlockSpec((Q_RANK, tn), lambda i, j: (0, j)),
                  pl.BlockSpec((tm, LANE), lambda i, j: (i, 0)), pl.BlockSpec((tm, LANE), lambda i, j: (i, 0))],
        out_specs=pl.BlockSpec((tm, tn), lambda i, j: (i, j)),
        out_shape=jax.ShapeDtypeStruct((t, N_HEADS * 256), BF16),
        compiler_params=_params(2),
    )(qn, w_uq_n, cos_t, sin_t)


def _attn_fwd(fox, operands, *, t, tb, name):
    nb = t // tb
    scale = FOX_SCALE if fox else MLA_SCALE
    exp2_scale = scale * LOG2E
    pair = 2 * HEAD
    pair0 = N_HEADS // 2 if fox else 0
    q_w = HEAD if fox else 2 * HEAD
    nt_dims = (((1,), (1,)), ((), ()))
    tn_dims = (((0,), (0,)), ((), ()))

    def body(*refs):
        if fox:
            (q_ref, k_ref, v_ref, gate_ref, cq_ref, ck_ref, _, _, _,
             o_ref, og_ref, ogt_ref, lse_ref, m_s, l_s, acc_s) = refs
        else:
            q_ref, kv_ref, kr_ref, gate_ref, o_ref, og_ref, ogt_ref, lse_ref, m_s, l_s, acc_s = refs
        qi = pl.program_id(1)
        m_s[...] = jnp.full_like(m_s, -jnp.inf)
        l_s[...] = jnp.zeros_like(l_s)
        acc_s[...] = jnp.zeros_like(acc_s)

        def chunk(kc, masked):
            off = pl.multiple_of(kc * tb, tb)
            scores = []
            for u in range(2):
                q = q_ref[:, q_w * u:q_w * (u + 1)]
                if fox:
                    kk = k_ref[pl.ds(off, tb), HEAD * u:HEAD * (u + 1)]
                else:
                    kk = jnp.concatenate([kv_ref[pl.ds(off, tb), pair * u:pair * u + HEAD],
                                          kr_ref[pl.ds(off, tb), :]], axis=1)
                s = lax.dot_general(kk, q, nt_dims, preferred_element_type=F32)
                if fox:
                    s = s + cq_ref[u, 0] - ck_ref[u, pl.ds(off, tb), :]
                if masked:
                    row = lax.broadcasted_iota(jnp.int32, (tb, tb), 0)
                    col = lax.broadcasted_iota(jnp.int32, (tb, tb), 1)
                    s = jnp.where(row <= col, s, -jnp.inf)
                scores.append(s)
            for u in range(2):
                s = scores[u]
                m_prev = m_s[u]
                m_new = jnp.maximum(m_prev, jnp.max(s, axis=0, keepdims=True))
                alpha = jnp.exp2((m_prev - m_new) * exp2_scale)
                p = jnp.exp2((s - m_new) * exp2_scale)
                l_s[u] = alpha * l_s[u] + jnp.sum(p, axis=0, keepdims=True)
                if fox:
                    vv = v_ref[pl.ds(off, tb), HEAD * u:HEAD * (u + 1)]
                else:
                    vv = kv_ref[pl.ds(off, tb), pair * u + HEAD:pair * (u + 1)]
                acc_s[u] = alpha * acc_s[u] + lax.dot_general(vv, p.astype(BF16), tn_dims,
                                                              preferred_element_type=F32)
                m_s[u] = m_new

        def loop_body(kc, carry):
            chunk(kc, False)
            return carry

        lax.fori_loop(0, qi, loop_body, 0)
        chunk(qi, True)
        for u in range(2):
            cols = slice(HEAD * u, HEAD * (u + 1))
            o_t = acc_s[u] / l_s[u]
            o = o_t.T
            o_ref[:, cols] = o
            g = gate_ref[:, cols].astype(F32)
            silu = g * _sigmoid(g)
            og_ref[:, cols] = (o * silu).astype(BF16)
            ogt_ref[cols, :] = (o_t * silu.T).astype(BF16)
            lse_ref[u, 0] = m_s[u] * scale + jnp.log(l_s[u])

    any_spec = pl.BlockSpec(memory_space=pl.ANY)
    row_stat = pl.BlockSpec((2, 1, 1, tb), lambda g, i: (g, i, 0, 0))
    if fox:
        proj, c_col, c_row4, o_all, og_all, ogt_all = operands
        ins = [proj, proj, proj, proj, c_row4, c_col, o_all, og_all, ogt_all]
        in_specs = [pl.BlockSpec((tb, pair), lambda g, i: (i, FQ0 // pair + g)),
                    pl.BlockSpec((t, pair), lambda g, i: (0, FK0 // pair + g)),
                    pl.BlockSpec((t, pair), lambda g, i: (0, FV0 // pair + g)),
                    pl.BlockSpec((tb, pair), lambda g, i: (i, GF0 // pair + g)),
                    row_stat, pl.BlockSpec((2, t, 1), lambda g, i: (g, 0, 0)), any_spec, any_spec, any_spec]
        aliases = {6: 0, 7: 1, 8: 2}
    else:
        q_full, kv, kr, proj = operands
        ins = [q_full, kv, kr, proj]
        in_specs = [pl.BlockSpec((tb, 2 * pair), lambda g, i: (i, g)),
                    pl.BlockSpec((t, 2 * pair), lambda g, i: (0, g)),
                    pl.BlockSpec((t, HEAD), lambda g, i: (0, 0)),
                    pl.BlockSpec((tb, pair), lambda g, i: (i, GM0 // pair + g))]
        aliases = {}
    return pl.pallas_call(
        body, name=name, grid=(N_HEADS // 2, nb), in_specs=in_specs,
        out_specs=[pl.BlockSpec((tb, pair), lambda g, i: (i, pair0 + g)),
                   pl.BlockSpec((tb, pair), lambda g, i: (i, pair0 + g)),
                   pl.BlockSpec((pair, tb), lambda g, i: (pair0 + g, i)), row_stat],
        out_shape=[jax.ShapeDtypeStruct((t, 2 * N_HEADS * HEAD), F32), jax.ShapeDtypeStruct((t, 2 * N_HEADS * HEAD), BF16),
                   jax.ShapeDtypeStruct((2 * N_HEADS * HEAD, t), BF16), jax.ShapeDtypeStruct((N_HEADS, nb, 1, tb), F32)],
        scratch_shapes=[pltpu.VMEM((2, 1, tb), F32), pltpu.VMEM((2, 1, tb), F32), pltpu.VMEM((2, HEAD, tb), F32)],
        input_output_aliases=aliases,
        compiler_params=_params(2),
    )(*ins)


def _out_norm_loss(og, w_out_n, x, target, g, *, tm):
    t = og.shape[0]

    def body(og_ref, w_ref, x_ref, t_ref, g_ref, dy_ref, do_ref, dg_ref, loss_ref):
        i = pl.program_id(0)

        @pl.when(i == 0)
        def _():
            dg_ref[...] = jnp.zeros_like(dg_ref)
            loss_ref[...] = jnp.zeros_like(loss_ref)

        ov = jnp.dot(og_ref[...], w_ref[...], preferred_element_type=F32)
        gv = g_ref[...]
        r = lax.rsqrt(jnp.mean(ov * ov, axis=-1, keepdims=True) + NORM_EPS)
        oh = ov * r
        e = x_ref[...] + oh * gv - t_ref[...]
        loss_ref[...] += 0.5 * jnp.sum(jnp.mean(e * e, axis=-1, keepdims=True), axis=0, keepdims=True)
        dy = e * (1.0 / D_MODEL)
        dy_ref[...] = dy
        dyg = dy * gv
        do_ref[...] = (r * (dyg - oh * jnp.mean(dyg * oh, axis=-1, keepdims=True))).astype(BF16)
        dg_ref[...] += jnp.sum(dy * oh, axis=0, keepdims=True)

    row = pl.BlockSpec((tm, D_MODEL), lambda i: (i, 0))
    vec = pl.BlockSpec((1, D_MODEL), lambda i: (0, 0))
    whole_w = pl.BlockSpec((D_MODEL, D_MODEL), lambda i: (0, 0), pipeline_mode=pl.Buffered(1))
    return pl.pallas_call(
        body, name="out_norm_loss", grid=(t // tm,),
        in_specs=[row, whole_w, row, row, vec],
        out_specs=[row, row, vec, pl.BlockSpec((1, 1), lambda i: (0, 0))],
        out_shape=[jax.ShapeDtypeStruct((t, D_MODEL), F32), jax.ShapeDtypeStruct((t, D_MODEL), BF16),
                   jax.ShapeDtypeStruct((1, D_MODEL), F32), jax.ShapeDtypeStruct((1, 1), F32)],
        compiler_params=_params(1),
    )(og, w_out_n, x, target, g)


def _dog_gate(d_o_post, w_out_n, o_all, proj, rider, *, tm):
    t = d_o_post.shape[0]
    n_group = 4
    pair = n_group * HEAD
    gate_blk = GM0 // pair
    assert GM0 % pair == 0 and GF0 == GM0 + N_HEADS * HEAD
    grid = (t // tm, 2 * N_HEADS // n_group)
    n_rin, n_rout = len(rider.operands), len(rider.out_shape)

    def body(*refs):
        do_ref, w_ref, o_ref, p_ref = refs[:4]
        dattn_ref, dproj_ref, delta_ref = refs[4 + n_rin:7 + n_rin]
        rider_refs = (refs[4:4 + n_rin], refs[7 + n_rin:7 + n_rin + n_rout], refs[7 + n_rin + n_rout:])
        i, j = pl.program_id(0), pl.program_id(1)
        pl.when(jnp.logical_and(i == 0, j == 0))(lambda: rider.start(*rider_refs))

        @pl.when(j == 0)
        def _():
            delta_ref[...] = jnp.zeros_like(delta_ref)

        dog = lax.dot_general(do_ref[...], w_ref[...], (((1,), (1,)), ((), ())), preferred_element_type=F32)
        g = p_ref[...].astype(F32)
        ov = o_ref[...]
        sg = _sigmoid(g)
        d_o = dog * (g * sg)
        dattn_ref[...] = d_o.astype(BF16)
        dproj_ref[...] = (dog * ov * (sg * (1.0 + g * (1.0 - sg)))).astype(BF16)
        prod = d_o * ov
        lane = lax.broadcasted_iota(jnp.int32, (tm, LANE), 1)
        delta = delta_ref[...]
        for u in range(n_group):
            part = jnp.sum(prod[:, HEAD * u:HEAD * (u + 1)], axis=-1, keepdims=True)
            delta = jnp.where(lane == n_group * j + u, part, delta)
        delta_ref[...] = delta
        pl.when(jnp.logical_and(i == grid[0] - 1, j == grid[1] - 1))(lambda: rider.wait(*rider_refs))

    any_spec = pl.BlockSpec(memory_space=pl.ANY)
    return pl.pallas_call(
        body, name="dog_gate", grid=grid,
        in_specs=[pl.BlockSpec((tm, D_MODEL), lambda i, j: (i, 0)), pl.BlockSpec((pair, D_MODEL), lambda i, j: (j, 0)),
                  pl.BlockSpec((tm, pair), lambda i, j: (i, j)), pl.BlockSpec((tm, pair), lambda i, j: (i, gate_blk + j))]
        + [any_spec] * n_rin,
        out_specs=[pl.BlockSpec((tm, pair), lambda i, j: (i, j)), pl.BlockSpec((tm, pair), lambda i, j: (i, gate_blk + j)),
                   pl.BlockSpec((tm, LANE), lambda i, j: (i, 0))] + [any_spec] * n_rout,
        out_shape=[jax.ShapeDtypeStruct((t, 2048), BF16), jax.ShapeDtypeStruct((t, NP_IN), BF16),
                   jax.ShapeDtypeStruct((t, LANE), F32)] + list(rider.out_shape),
        scratch_shapes=list(rider.scratch),
        compiler_params=_params(2),
    )(d_o_post, w_out_n, o_all, proj, *rider.operands)


def _attn_bwd(fox, operands, *, t, tb, name, rider=None):
    nb = t // tb
    n_pairs = N_HEADS // 2
    pair = 2 * HEAD
    scale = FOX_SCALE if fox else MLA_SCALE
    q_w = HEAD if fox else 2 * HEAD
    nt_dims = (((1,), (1,)), ((), ()))
    tn_dims = (((0,), (0,)), ((), ()))
    n_rin = len(rider.operands) if rider else 0
    n_rout = len(rider.out_shape) if rider else 0
    n_in, n_out, n_scr = (9, 3, 9) if fox else (6, 3, 2)

    def body(*refs):
        ends = np.cumsum([0, n_in, n_rin, n_out, n_rout, n_scr])
        in_refs, rider_in, out_refs, rider_out, scr_refs = (refs[a:b] for a, b in zip(ends[:-1], ends[1:]))
        rider_refs = (rider_in, rider_out, refs[ends[-1]:])
        if fox:
            q_ref, k_ref, v_ref, do_ref, lse_ref, dl_ref, cq_ref, ck_ref, _ = in_refs
            dproj_ref, dck_ref, dcq_ref = out_refs
            dq_acc, dk_s, dv_s, dc_s, dcq_s, stage_q, stage_k, stage_v, put_sems = scr_refs
        else:
            q_ref, kv_ref, kr_ref, do_ref, lse_ref, dl_ref = in_refs
            dq_acc, dkv_ref, dkr_ref = out_refs
            dk_s, dv_s = scr_refs
        g = pl.program_id(0)
        ki = pl.program_id(1)
        if rider:
            @pl.when(jnp.logical_and(g == 0, ki == 0))
            def _():
                rider.start(*rider_refs)

        @pl.when(ki == 0)
        def _():
            dq_acc[...] = jnp.zeros_like(dq_acc)
            if fox:
                dcq_s[...] = jnp.zeros_like(dcq_s)

        dk_s[...] = jnp.zeros_like(dk_s)
        dv_s[...] = jnp.zeros_like(dv_s)
        if fox:
            dc_s[...] = jnp.zeros_like(dc_s)
            keys = [k_ref[:, HEAD * u:HEAD * (u + 1)] for u in range(2)]
            vals = [v_ref[:, HEAD * u:HEAD * (u + 1)] for u in range(2)]
        else:
            keys = [jnp.concatenate([kv_ref[:, pair * u:pair * u + HEAD], kr_ref[...]], axis=1) for u in range(2)]
            vals = [kv_ref[:, pair * u + HEAD:pair * (u + 1)] for u in range(2)]

        def chunk(qc, masked):
            off = pl.multiple_of(qc * tb, tb)
            for u in range(2):
                kk, vv = keys[u], vals[u]
                qq = q_ref[pl.ds(off, tb), q_w * u:q_w * (u + 1)]
                dd = do_ref[pl.ds(off, tb), HEAD * u:HEAD * (u + 1)]
                s = lax.dot_general(kk, qq, nt_dims, preferred_element_type=F32)
                if fox:
                    s = s + cq_ref[u, qc] - ck_ref[u]
                if masked:
                    row = lax.broadcasted_iota(jnp.int32, (tb, tb), 0)
                    col = lax.broadcasted_iota(jnp.int32, (tb, tb), 1)
                    s = jnp.where(row <= col, s, -jnp.inf)
                p = jnp.exp2(s * (scale * LOG2E) - lse_ref[u, qc] * LOG2E)
                dv_s[u] += jnp.dot(p.astype(BF16), dd, preferred_element_type=F32)
                dp = lax.dot_general(vv, dd, nt_dims, preferred_element_type=F32)
                ds = p * (dp - dl_ref[u, qc])
                if fox:
                    dc_s[u] += jnp.sum(ds, axis=1, keepdims=True)
                    dcq_s[u, qc] += jnp.sum(ds, axis=0, keepdims=True)
                dsb = (ds * scale).astype(BF16)
                dk_s[u] += jnp.dot(dsb, qq, preferred_element_type=F32)
                dq_acc[pl.ds(off, tb), q_w * u:q_w * (u + 1)] += lax.dot_general(dsb, kk, tn_dims,
                                                                                 preferred_element_type=F32)

        chunk(ki, True)

        def loop_body(qc, carry):
            chunk(qc, False)
            return carry

        lax.fori_loop(ki + 1, nb, loop_body, 0)

        def put(stage_ref, rows, seg0, sem):
            col0 = pl.multiple_of(seg0 + g * pair, pair)
            return pltpu.make_async_copy(stage_ref, dproj_ref.at[rows, pl.ds(col0, pair)], sem)

        if fox:
            rows = pl.ds(pl.multiple_of(ki * tb, tb), tb)
            block_puts = [put(stage_k, rows, FK0, put_sems.at[1]), put(stage_v, rows, FV0, put_sems.at[2])]
            pair_put = put(stage_q, pl.ds(0, t), FQ0, put_sems.at[0])

            @pl.when(jnp.logical_or(g > 0, ki > 0))
            def _():
                for cp in block_puts:
                    cp.wait()

            for u in range(2):
                stage_k[:, HEAD * u:HEAD * (u + 1)] = dk_s[u].astype(BF16)
                stage_v[:, HEAD * u:HEAD * (u + 1)] = dv_s[u].astype(BF16)
                dck_ref[u] = -dc_s[u]
            for cp in block_puts:
                cp.start()

            @pl.when(ki == nb - 1)
            def _():
                @pl.when(g > 0)
                def _():
                    pair_put.wait()

                stage_q[...] = dq_acc[...].astype(BF16)
                pair_put.start()
                dcq_ref[...] = dcq_s[...]

            @pl.when(jnp.logical_and(g == n_pairs - 1, ki == nb - 1))
            def _():
                for cp in block_puts + [pair_put]:
                    cp.wait()
        else:
            dkv_ref[...] = jnp.concatenate([dk_s[0, :, :HEAD], dv_s[0], dk_s[1, :, :HEAD], dv_s[1]], axis=1).astype(BF16)
            dkr_ref[...] = jnp.concatenate([dk_s[0, :, HEAD:], dk_s[1, :, HEAD:]], axis=1)

        if rider:
            @pl.when(jnp.logical_and(g == n_pairs - 1, ki == nb - 1))
            def _():
                rider.wait(*rider_refs)

    stat = pl.BlockSpec((2, nb, 1, tb), lambda g, i: (g, 0, 0, 0))
    aliases = {}
    if fox:
        proj, d_o, lse4, delta4, c_row4, c_col, dproj = operands
        ins = [proj, proj, proj, d_o, lse4, delta4, c_row4, c_col, dproj]
        any_spec = pl.BlockSpec(memory_space=pl.ANY)
        in_specs = [pl.BlockSpec((t, pair), lambda g, i: (0, FQ0 // pair + g)),
                    pl.BlockSpec((tb, pair), lambda g, i: (i, FK0 // pair + g)),
                    pl.BlockSpec((tb, pair), lambda g, i: (i, FV0 // pair + g)),
                    pl.BlockSpec((t, pair), lambda g, i: (0, n_pairs + g)),
                    stat, stat, stat, pl.BlockSpec((2, tb, 1), lambda g, i: (g, i, 0)), any_spec]
        aliases = {8: 0}
        out_specs = [any_spec, pl.BlockSpec((2, tb, 1), lambda g, i: (g, i, 0)), stat]
        out_shape = [jax.ShapeDtypeStruct(dproj.shape, dproj.dtype), jax.ShapeDtypeStruct((N_HEADS, t, 1), F32),
                     jax.ShapeDtypeStruct((N_HEADS, nb, 1, tb), F32)]
        scratch = [pltpu.VMEM((t, pair), F32), pltpu.VMEM((2, tb, HEAD), F32), pltpu.VMEM((2, tb, HEAD), F32),
                   pltpu.VMEM((2, tb, 1), F32), pltpu.VMEM((2, nb, 1, tb), F32),
                   pltpu.VMEM((t, pair), BF16), pltpu.VMEM((tb, pair), BF16), pltpu.VMEM((tb, pair), BF16),
                   pltpu.SemaphoreType.DMA((3,))]
    else:
        q_full, kv, kr, d_o, lse4, delta4 = operands
        ins = [q_full, kv, kr, d_o, lse4, delta4]
        in_specs = [pl.BlockSpec((t, 2 * pair), lambda g, i: (0, g)),
                    pl.BlockSpec((tb, 2 * pair), lambda g, i: (i, g)),
                    pl.BlockSpec((tb, HEAD), lambda g, i: (i, 0)),
                    pl.BlockSpec((t, pair), lambda g, i: (0, g)),
                    stat, stat]
        out_specs = [pl.BlockSpec((t, 2 * pair), lambda g, i: (0, g)), pl.BlockSpec((tb, 2 * pair), lambda g, i: (i, g)),
                     pl.BlockSpec((tb, pair), lambda g, i: (i, g))]
        out_shape = [jax.ShapeDtypeStruct((t, 2048), F32), jax.ShapeDtypeStruct((t, 2048), BF16),
                     jax.ShapeDtypeStruct((t, 1024), F32)]
        scratch = [pltpu.VMEM((2, tb, 2 * HEAD), F32), pltpu.VMEM((2, tb, HEAD), F32)]
    assert (len(ins), len(out_specs), len(scratch)) == (n_in, n_out, n_scr)
    if rider:
        any_spec = pl.BlockSpec(memory_space=pl.ANY)
        aliases = {**aliases, **{n_in + i_in: n_out + i_out for i_in, i_out in rider.aliases.items()}}
        ins = ins + list(rider.operands)
        in_specs = in_specs + [any_spec] * n_rin
        out_specs = out_specs + [any_spec] * n_rout
        out_shape = out_shape + list(rider.out_shape)
        scratch = scratch + list(rider.scratch)
    return pl.pallas_call(
        body, name=name, grid=(n_pairs, nb), in_specs=in_specs, out_specs=out_specs, out_shape=out_shape,
        scratch_shapes=scratch, input_output_aliases=aliases, compiler_params=_params(2),
    )(*ins)


def _mid_bwd(dq_full, dkr, cos_t, sin_t, dcq, dck, flog, bf_row, *, tm):
    t = dq_full.shape[0]
    n = t // tm

    def body(dq_ref, dkr_ref, cos_ref, sin_ref, dcq_ref, dck_ref, fl_ref, bf_ref,
             dq2_ref, dkraw_ref, dfl_ref, dbf_ref, carry_ref):
        i = pl.program_id(0)

        @pl.when(i == 0)
        def _():
            carry_ref[...] = jnp.zeros_like(carry_ref)
            dbf_ref[...] = jnp.zeros_like(dbf_ref)

        c, s = cos_ref[...], sin_ref[...]
        dkr_sum = jnp.zeros((tm, LANE), F32)
        for h in range(N_HEADS):
            dq2_ref[:, 256 * h:256 * h + 128] = dq_ref[:, 256 * h:256 * h + 128].astype(BF16)
            dq2_ref[:, 256 * h + 128:256 * h + 256] = _rot_bwd(dq_ref[:, 256 * h + 128:256 * h + 256], c, s).astype(BF16)
            dkr_sum = dkr_sum + dkr_ref[:, HEAD * h:HEAD * (h + 1)]
        dkraw_ref[...] = _rot_bwd(dkr_sum, c, s).astype(BF16)

        dc = dcq_ref[...]
        lane = lax.broadcasted_iota(jnp.int32, (tm, LANE), 1)
        for h in range(N_HEADS):
            dc = dc + jnp.where(lane == h, dck_ref[h], 0.0)
        row = lax.broadcasted_iota(jnp.int32, (tm, tm), 0)
        col = lax.broadcasted_iota(jnp.int32, (tm, tm), 1)
        tri = (col >= row).astype(BF16)
        acc = carry_ref[0:1, :]
        for part in _split3(dc):
            acc = acc + jnp.dot(tri, part, preferred_element_type=F32)
        carry_ref[0:1, :] = carry_ref[0:1, :] + jnp.sum(dc, axis=0, keepdims=True)
        z = fl_ref[...] + bf_ref[...]
        dz = acc / (1.0 + jnp.exp(z))
        dfl_ref[...] = dz.astype(BF16)
        dbf_ref[...] += jnp.sum(dz, axis=0, keepdims=True)

    rev = lambda w: pl.BlockSpec((tm, w), lambda i: (n - 1 - i, 0))
    vec = lambda w: pl.BlockSpec((1, w), lambda i: (0, 0))
    return pl.pallas_call(
        body, name="mid_bwd", grid=(n,),
        in_specs=[rev(2048), rev(1024), rev(LANE), rev(LANE), rev(LANE),
                  pl.BlockSpec((N_HEADS, tm, 1), lambda i: (0, n - 1 - i, 0)), rev(LANE), vec(LANE)],
        out_specs=[rev(2048), rev(LANE), rev(LANE), vec(LANE)],
        out_shape=[jax.ShapeDtypeStruct((t, 2048), BF16), jax.ShapeDtypeStruct((t, LANE), BF16),
                   jax.ShapeDtypeStruct((t, LANE), BF16), jax.ShapeDtypeStruct((1, LANE), F32)],
        scratch_shapes=[pltpu.VMEM((8, LANE), F32)],
        compiler_params=_params(1),
    )(dq_full, dkr, cos_t, sin_t, dcq, dck, flog, bf_row)


def _norm_bwd(proj, dqn, dkvn, g_q, g_kv, dkr_raw, dfl, dproj, rider, *, tm):
    t = proj.shape[0]
    n_steps = t // tm
    assert (KR0, FL0, KVL0, LAT_W) == (Q_RANK, Q_RANK + LANE, Q_RANK + 2 * LANE, Q_RANK + 2 * LANE + KV_RANK)
    n_rin, n_rout = len(rider.operands), len(rider.out_shape)

    def body(*refs):
        p_ref, dqn_ref, dkvn_ref, gq_ref, gkv_ref, dkr_ref, dfl_ref = refs[:7]
        dproj_ref, dgq_ref, dgkv_ref = refs[8 + n_rin:11 + n_rin]
        rider_refs = (refs[8:8 + n_rin], refs[11 + n_rin:11 + n_rin + n_rout], refs[11 + n_rin + n_rout:])
        i = pl.program_id(0)
        pl.when(i == 0)(lambda: rider.start(*rider_refs))
        pl.when(i == n_steps - 1)(lambda: rider.wait(*rider_refs))

        @pl.when(i == 0)
        def _():
            dgq_ref[...] = jnp.zeros_like(dgq_ref)
            dgkv_ref[...] = jnp.zeros_like(dgkv_ref)

        d_lat = []
        for lo, w, dn_ref, g_ref, dg_ref in ((QL0, Q_RANK, dqn_ref, gq_ref, dgq_ref),
                                             (KVL0, KV_RANK, dkvn_ref, gkv_ref, dgkv_ref)):
            xv = p_ref[:, lo:lo + w].astype(F32)
            r = lax.rsqrt(jnp.mean(xv * xv, axis=-1, keepdims=True) + NORM_EPS)
            xh = xv * r
            dn = dn_ref[...]
            dg_ref[...] += jnp.sum(dn * xh, axis=0, keepdims=True)
            dxh = dn * g_ref[...]
            d_lat.append((r * (dxh - xh * jnp.mean(dxh * xh, axis=-1, keepdims=True))).astype(BF16))
        dproj_ref[...] = jnp.concatenate([d_lat[0], dkr_ref[...], dfl_ref[...], d_lat[1]], axis=1)

    row = lambda w: pl.BlockSpec((tm, w), lambda i: (i, 0))
    vec = lambda w: pl.BlockSpec((1, w), lambda i: (0, 0))
    return pl.pallas_call(
        body, name="norm_bwd", grid=(n_steps,),
        in_specs=[row(LAT_W), row(Q_RANK), row(KV_RANK), vec(Q_RANK), vec(KV_RANK), row(LANE), row(LANE)]
        + [pl.BlockSpec(memory_space=pl.ANY)] * (1 + n_rin),
        out_specs=[row(LAT_W), vec(Q_RANK), vec(KV_RANK)] + [pl.BlockSpec(memory_space=pl.ANY)] * n_rout,
        out_shape=[jax.ShapeDtypeStruct(dproj.shape, dproj.dtype),
                   jax.ShapeDtypeStruct((1, Q_RANK), F32), jax.ShapeDtypeStruct((1, KV_RANK), F32)]
        + list(rider.out_shape),
        scratch_shapes=list(rider.scratch),
        input_output_aliases={7: 0, **{8 + i_in: 3 + i_out for i_in, i_out in rider.aliases.items()}},
        compiler_params=_params(1),
    )(proj, dqn, dkvn, g_q, g_kv, dkr_raw, dfl, dproj, *rider.operands)


def _prenorm_bwd(dh, x, g, dy, *, tm):
    t = x.shape[0]

    def body(dh_ref, x_ref, g_ref, dy_ref, gx_ref, dg_ref):
        i = pl.program_id(0)

        @pl.when(i == 0)
        def _():
            dg_ref[...] = jnp.zeros_like(dg_ref)

        xv = x_ref[...]
        r = lax.rsqrt(jnp.mean(xv * xv, axis=-1, keepdims=True) + NORM_EPS)
        xh = xv * r
        dn = dh_ref[...]
        dg_ref[...] += jnp.sum(dn * xh, axis=0, keepdims=True)
        dxh = dn * g_ref[...]
        gx_ref[...] = dy_ref[...] + r * (dxh - xh * jnp.mean(dxh * xh, axis=-1, keepdims=True))

    row = pl.BlockSpec((tm, D_MODEL), lambda i: (i, 0))
    vec = pl.BlockSpec((1, D_MODEL), lambda i: (0, 0))
    return pl.pallas_call(
        body, name="prenorm_bwd", grid=(t // tm,),
        in_specs=[row, row, vec, row], out_specs=[row, vec],
        out_shape=[jax.ShapeDtypeStruct((t, D_MODEL), F32), jax.ShapeDtypeStruct((1, D_MODEL), F32)],
        compiler_params=_params(1),
    )(dh, x, g, dy)


def _adam_math(w, g, m, v):
    m = ADAM_B1 * m + (1.0 - ADAM_B1) * g
    v = ADAM_B2 * v + (1.0 - ADAM_B2) * (g * g)
    m_hat = m / (1.0 - ADAM_B1 ** ADAM_STEP)
    v_hat = v / (1.0 - ADAM_B2 ** ADAM_STEP)
    delta = -ADAM_LR * (m_hat / (jnp.sqrt(v_hat) + ADAM_EPS) + ADAM_WD * w)
    return delta, m, v


def _adamw(land, w, m, v, *, tr, name):
    rows, cols = w.shape

    def body(l_ref, w_ref, m_ref, v_ref, g_ref, d_ref, nm_ref, nv_ref):
        g = l_ref[0].astype(F32)
        for s in range(1, N_CHIPS):
            g = g + l_ref[s].astype(F32)
        g_ref[...] = g
        d_ref[...], nm_ref[...], nv_ref[...] = _adam_math(w_ref[...], g, m_ref[...], v_ref[...])

    blk = pl.BlockSpec((tr, cols), lambda i: (i, 0))
    return pl.pallas_call(
        body, name=name, grid=(rows // tr,),
        in_specs=[pl.BlockSpec((N_CHIPS, tr, cols), lambda i: (0, i, 0)), blk, blk, blk],
        out_specs=[blk, blk, blk, blk],
        out_shape=[jax.ShapeDtypeStruct((rows, cols), F32)] * 4,
        compiler_params=_params(1),
    )(land, w, m, v)


def _adamw_small(gathered, w, m, v):
    def body(a_ref, w_ref, m_ref, v_ref, g_ref, d_ref, nm_ref, nv_ref):
        g = a_ref[0:SMALL_ROWS, :]
        for s in range(1, N_DEV):
            g = g + a_ref[SMALL_ROWS * s:SMALL_ROWS * (s + 1), :]
        g_ref[...] = g
        d_ref[...], nm_ref[...], nv_ref[...] = _adam_math(w_ref[...], g, m_ref[...], v_ref[...])

    return pl.pallas_call(
        body, name="adamw_small",
        out_shape=[jax.ShapeDtypeStruct((SMALL_ROWS, SMALL_COLS), F32)] * 4,
        compiler_params=_params(),
    )(gathered, w, m, v)


def _place():
    x, y, c = lax.axis_index("x"), lax.axis_index("y"), lax.axis_index("c")
    return x, y, c


def _flip(p, k):
    x, y, c = p
    return (1 - x if k & 4 else x, 1 - y if k & 2 else y, 1 - c if k & 1 else c)


def _index(p):
    return 4 * p[0] + 2 * p[1] + p[2]


class _AllGather:
    def __init__(self, shard):
        assert shard.shape[0] % 32 == 0
        self.half = shard.shape[0] // 2
        self.operands = [shard]
        self.out_shape = [jax.ShapeDtypeStruct((N_DEV,) + shard.shape, shard.dtype)]
        self.aliases = {}
        self.scratch = [pltpu.SemaphoreType.DMA((9,)), pltpu.SemaphoreType.DMA((9,)), pltpu.SemaphoreType.DMA(())]

    def _parts(self, ins, outs, scratch):
        (in_ref,), (out_ref,), (send_sems, recv_sems, local_sem) = ins, outs, scratch
        me = _place()

        def copy(k, block, to, part=None, src=None):
            dst = out_ref.at[_index(block)] if part is None else out_ref.at[_index(block), part]
            return pltpu.make_async_remote_copy(
                src_ref=dst if src is None else src, dst_ref=dst, send_sem=send_sems.at[k], recv_sem=recv_sems.at[k],
                device_id=to, device_id_type=MESH)

        mine = pltpu.make_async_copy(in_ref, out_ref.at[_index(me)], local_sem)
        first = [copy(0, me, _flip(me, 1), src=in_ref), copy(1, me, _flip(me, 4), src=in_ref),
                 copy(2, me, _flip(me, 2), src=in_ref)]
        return me, copy, mine, first

    def start(self, ins, outs, scratch):
        _, _, mine, first = self._parts(ins, outs, scratch)
        mine.start()
        for cp in first:
            cp.start()

    def wait(self, ins, outs, scratch):
        me, copy, mine, sent = self._parts(ins, outs, scratch)
        sibling, x_nbr, y_nbr, diagonal = _flip(me, 1), _flip(me, 4), _flip(me, 2), _flip(me, 6)
        top, bottom = pl.ds(0, self.half), pl.ds(self.half, self.half)
        arrivals = [(1, x_nbr, None, [(3, sibling, None), (5, y_nbr, top)]),
                    (2, y_nbr, None, [(4, sibling, None), (6, x_nbr, bottom)]),
                    (5, diagonal, top, [(7, sibling, top)]),
                    (6, diagonal, bottom, [(8, sibling, bottom)])]
        for k, block, part, onward in arrivals:
            copy(k, block, me, part).wait_recv()
            for k_on, to, part_on in onward:
                cp = copy(k_on, block, to, part_on)
                cp.start()
                sent.append(cp)
        other = lambda p: _flip(p, 1)
        for k, block, part in ((0, sibling, None), (3, other(x_nbr), None), (4, other(y_nbr), None),
                               (7, other(diagonal), top), (8, other(diagonal), bottom)):
            copy(k, block, me, part).wait_recv()
        for cp in sent:
            cp.wait_send()
        mine.wait()


class _Exchange:
    def __init__(self, tasks):
        self.tasks = tasks
        taken = [land for _, _, land, _, _ in tasks if land is not None]
        self.operands = [src for src, _, _, _, _ in tasks] + taken
        self.out_shape = [
            jax.ShapeDtypeStruct((N_CHIPS,) + ((2,) if by_core else ()) + (src.shape if same else src.shape[1:]), src.dtype)
            for src, _, _, same, by_core in tasks]
        self.aliases, n_taken = {}, 0
        for a, (_, _, land, _, _) in enumerate(tasks):
            if land is not None:
                self.aliases[len(tasks) + n_taken] = a
                n_taken += 1
        self.scratch = [pltpu.SemaphoreType.DMA((N_CHIPS,)), pltpu.SemaphoreType.DMA((N_CHIPS,)),
                        pltpu.SemaphoreType.DMA(())] * len(tasks)

    def _copies(self, ins, outs, scratch):
        x, y, core = _place()
        my = 2 * x + y
        for a, (_, chips, _, same, by_core) in enumerate(self.tasks):
            send_sems, recv_sems, local_sem = scratch[3 * a:3 * a + 3]
            slot = (lambda s, a=a, by_core=by_core: outs[a].at[s, core] if by_core else outs[a].at[s])
            for i, j in enumerate(chips):
                src = ins[a] if same else ins[a].at[i]
                pair = jnp.bitwise_xor(my, j)
                remote = pltpu.make_async_remote_copy(
                    src_ref=src, dst_ref=slot(my), send_sem=send_sems.at[pair], recv_sem=recv_sems.at[pair],
                    device_id=(j >> 1, j & 1, core), device_id_type=MESH)
                local = pltpu.make_async_copy(src, slot(my), local_sem)
                yield j, my, core, remote, local, slot, (send_sems, recv_sems)

    def start(self, ins, outs, scratch):
        for j, my, _, remote, local, _, _ in self._copies(ins, outs, scratch):
            pl.when(my != j)(remote.start)
            pl.when(my == j)(local.start)

    def wait(self, ins, outs, scratch):
        for j, my, core, remote, local, slot, (send_sems, recv_sems) in self._copies(ins, outs, scratch):
            pl.when(my != j)(remote.wait_send)

            @pl.when(my == j)
            def _():
                local.wait()
                for s in range(N_CHIPS):
                    if s != j:
                        pltpu.make_async_remote_copy(
                            src_ref=slot(s), dst_ref=slot(s), send_sem=send_sems.at[j ^ s], recv_sem=recv_sems.at[j ^ s],
                            device_id=(s >> 1, s & 1, core), device_id_type=MESH).wait_recv()


N_CHIPS = 4
ALL_CHIPS = tuple(range(N_CHIPS))


class _ToOtherCore:
    def __init__(self, parts):
        self.operands = list(parts)
        self.out_shape = [jax.ShapeDtypeStruct(p.shape[1:], p.dtype) for p in parts]
        self.aliases = {}
        self.scratch = [pltpu.SemaphoreType.DMA((len(parts),)), pltpu.SemaphoreType.DMA((len(parts),))]

    def _copies(self, ins, outs, scratch):
        send_sems, recv_sems = scratch
        me = _place()
        return [pltpu.make_async_remote_copy(src_ref=ins[a].at[1 - me[2]], dst_ref=outs[a], send_sem=send_sems.at[a],
                                             recv_sem=recv_sems.at[a], device_id=_flip(me, 1), device_id_type=MESH)
                for a in range(len(ins))]

    def start(self, ins, outs, scratch):
        for cp in self._copies(ins, outs, scratch):
            cp.start()

    def wait(self, ins, outs, scratch):
        for cp in self._copies(ins, outs, scratch):
            cp.wait()


def _to_other_core(parts, *, name):
    swap = _ToOtherCore(parts)
    n_arr = len(parts)
    hbm = pl.BlockSpec(memory_space=pl.ANY)

    def body(*refs):
        rider_refs = (refs[:n_arr], refs[n_arr:2 * n_arr], refs[2 * n_arr:])
        swap.start(*rider_refs)
        swap.wait(*rider_refs)

    return pl.pallas_call(
        body, name=name, in_specs=[hbm] * n_arr, out_specs=[hbm] * n_arr,
        out_shape=swap.out_shape, scratch_shapes=swap.scratch,
    )(*parts)


class _ShareWithOtherCore:
    def __init__(self, gathered):
        n_arr = len(gathered)
        self.operands = list(gathered)
        self.out_shape = [jax.ShapeDtypeStruct(g.shape, g.dtype) for g in gathered]
        self.aliases = {a: a for a in range(n_arr)}
        self.scratch = [pltpu.SemaphoreType.DMA((N_CHIPS * n_arr,)), pltpu.SemaphoreType.DMA((N_CHIPS * n_arr,))]

    def _copies(self, outs, scratch):
        send_sems, recv_sems = scratch
        me = _place()
        copies = []
        for a, buf in enumerate(outs):
            for j in range(N_CHIPS):
                block = buf.at[j, me[2]]
                copies.append(pltpu.make_async_remote_copy(
                    src_ref=block, dst_ref=block, send_sem=send_sems.at[N_CHIPS * a + j],
                    recv_sem=recv_sems.at[N_CHIPS * a + j], device_id=_flip(me, 1), device_id_type=MESH))
        return copies

    def start(self, ins, outs, scratch):
        for cp in self._copies(outs, scratch):
            cp.start()

    def wait(self, ins, outs, scratch):
        for cp in self._copies(outs, scratch):
            cp.wait()


def _pair_sum(mine, other, core, *, tr, name):
    _, n, rows, cols = mine.shape
    tr = min(tr, rows)

    def body(core_ref, a_ref, b_ref, o_ref):
        o_ref[...] = (a_ref[0].astype(F32) + b_ref[...].astype(F32)).astype(BF16)

    return pl.pallas_call(
        body, name=name,
        grid_spec=pltpu.PrefetchScalarGridSpec(
            num_scalar_prefetch=1, grid=(n, rows // tr),
            in_specs=[pl.BlockSpec((1, 1, tr, cols), lambda j, i, core_ref: (core_ref[0], j, i, 0)),
                      pl.BlockSpec((1, tr, cols), lambda j, i, core_ref: (j, i, 0))],
            out_specs=pl.BlockSpec((1, tr, cols), lambda j, i, core_ref: (j, i, 0))),
        out_shape=jax.ShapeDtypeStruct(other.shape, BF16),
        compiler_params=_params(2),
    )(core, mine, other)


def _gather_small(vec):
    def body(v_ref, out_ref, send_sems, recv_sems, local_sem):
        me = _place()

        def rows(p):
            return out_ref.at[pl.ds(pl.multiple_of(_index(p) * SMALL_ROWS, SMALL_ROWS), SMALL_ROWS), :]

        mine = pltpu.make_async_copy(v_ref, rows(me), local_sem)
        mine.start()
        sends = []
        for k in range(1, N_DEV):
            peer = _flip(me, k)
            cp = pltpu.make_async_remote_copy(src_ref=v_ref, dst_ref=rows(me), send_sem=send_sems.at[k - 1],
                                              recv_sem=recv_sems.at[k - 1], device_id=peer, device_id_type=MESH)
            cp.start()
            sends.append(cp)
        for k in range(1, N_DEV):
            peer = _flip(me, k)
            pltpu.make_async_remote_copy(src_ref=rows(peer), dst_ref=rows(peer), send_sem=send_sems.at[k - 1],
                                         recv_sem=recv_sems.at[k - 1], device_id=peer, device_id_type=MESH).wait_recv()
        for cp in sends:
            cp.wait_send()
        mine.wait()

    return pl.pallas_call(
        body, name="gather_small",
        in_specs=[pl.BlockSpec(memory_space=pltpu.VMEM)], out_specs=pl.BlockSpec(memory_space=pltpu.VMEM),
        out_shape=jax.ShapeDtypeStruct((N_DEV * SMALL_ROWS, SMALL_COLS), F32),
        scratch_shapes=[pltpu.SemaphoreType.DMA((7,)), pltpu.SemaphoreType.DMA((7,)), pltpu.SemaphoreType.DMA],
    )(vec)


def _w_in_nice(gathered):
    pieces, pos = [], 0
    for o0, width, n0 in sorted(_SEGMENTS, key=lambda seg: seg[2]):
        if n0 > pos:
            pieces.append(jnp.zeros((D_MODEL, n0 - pos), gathered.dtype))
        o = o0
        while o < o0 + width:
            d = o // SHARD_IN
            hi = min(o0 + width, (d + 1) * SHARD_IN)
            pieces.append(gathered[d][:, o - d * SHARD_IN:hi - d * SHARD_IN])
            o = hi
        pos = n0 + width
    pieces.append(jnp.zeros((D_MODEL, NP_IN - pos), gathered.dtype))
    return jnp.concatenate(pieces, axis=1)


def _w_in_blocks(chips, dw_lat, dw_rest):
    blocks = []
    for core in range(2):
        for chip in chips:
            lo = (2 * chip + core) * SHARD_IN
            runs = []
            for o0, width, n0 in _SEGMENTS:
                a, b = max(lo, o0), min(lo + SHARD_IN, o0 + width)
                if a < b:
                    n_a, n_b = n0 + a - o0, n0 + b - o0
                    runs.append(dw_lat[:, n_a:n_b] if n_b <= LAT_W else dw_rest[:, n_a - LAT_W:n_b - LAT_W])
            blocks.append(jnp.concatenate(runs, axis=1))
    return jnp.stack(blocks).reshape(2, len(chips), D_MODEL, SHARD_IN)


def _by_core(shards):
    return shards.reshape((N_CHIPS, 2) + shards.shape[1:]).swapaxes(0, 1)


EARLY_CHIPS = (1, 2)
LATE_CHIPS = (0, 3)


def _w_uq_nice(shard):
    z = jnp.zeros((Q_RANK, 32), shard.dtype)
    return jnp.concatenate([shard[:, :128], shard[:, 128:160], z, shard[:, 160:192], z], axis=1)


def _pack_small(g_pre, g_post, g_q, g_kv, b_f, extra=None):
    parts = [g_pre.reshape(-1), g_post.reshape(-1), g_q.reshape(-1), g_kv.reshape(-1), b_f.reshape(-1)]
    if extra is not None:
        parts.append(extra.reshape(-1))
    flat = jnp.concatenate(parts)
    flat = jnp.concatenate([flat, jnp.zeros((SMALL_ROWS * SMALL_COLS - flat.shape[0],), F32)])
    return flat.reshape(SMALL_ROWS, SMALL_COLS)


def _unpack_small(packed):
    flat = packed.reshape(-1)
    o = 0
    out = []
    for n in (D_MODEL, D_MODEL, Q_RANK, KV_RANK, N_HEADS):
        out.append(flat[o:o + n].reshape(1, n))
        o += n
    return out, flat[o]


def kernel(x, positions, g_pre, w_in, g_q_latent, w_uq, g_kv_latent, w_ukv, b_forget, w_out, g_post, loss_target, m_g_pre, m_w_in, m_g_q_latent, m_w_uq, m_g_kv_latent, m_w_ukv, m_b_forget, m_w_out, m_g_post, v_g_pre, v_w_in, v_g_q_latent, v_w_uq, v_g_kv_latent, v_w_ukv, v_b_forget, v_w_out, v_g_post):
    t = x.shape[1]
    tb = min(512, t)
    tm = min(256, t)
    nb = t // tb
    x2 = x.reshape(t, D_MODEL)
    target = loss_target.reshape(t, D_MODEL)
    pos_col = positions.reshape(t, 1).astype(F32)
    bf_row = jnp.concatenate([b_forget.reshape(1, N_HEADS), jnp.zeros((1, LANE - N_HEADS), F32)], axis=1)

    h, h_t, g_in = _prenorm(x2, g_pre, _AllGather(w_in[0].astype(BF16)), tm=tm)
    w_in_n = _w_in_nice(g_in)
    gather_rest = _Exchange([(w, ALL_CHIPS, None, True, True) for w in
                             (_w_uq_nice(w_uq[0].astype(BF16)), w_ukv[0].astype(BF16), w_out[0].astype(BF16))])
    core = lax.axis_index("c").astype(jnp.int32).reshape(1)
    proj, g_uq, g_ukv, g_out = _mm(h, w_in_n, name="proj_in", out_dtype=BF16, tm=2048, tn=512, tk=2048,
                                   rider=gather_rest)
    flog = _mm(h, w_in_n[:, FL0:FL0 + LANE], name="proj_flog", out_dtype=F32, tm=1024, tn=LANE, tk=2048)
    qn, kvn, kr, cos_t, sin_t, c, qn_t, kvn_t, g_uq, g_ukv, g_out = _mid_fwd(
        proj, flog, g_q_latent, g_kv_latent, bf_row, pos_col, _ShareWithOtherCore([g_uq, g_ukv, g_out]), tm=tm)
    w_uq_n = g_uq.reshape(N_DEV, Q_RANK, 256).transpose(1, 0, 2).reshape(Q_RANK, N_HEADS * 256)
    w_ukv_n = g_ukv.reshape(N_DEV, KV_RANK, 256).transpose(1, 0, 2).reshape(KV_RANK, N_HEADS * 256)
    w_out_n = g_out.reshape(D_MODEL, D_MODEL)
    q_full = _q_up_rope(qn, w_uq_n, cos_t, sin_t, tm=min(1024, t))
    kv = _mm(kvn, w_ukv_n, name="kv_up", out_dtype=BF16, tm=1024, tn=512, tk=KV_RANK)
    c_heads = c[:, :N_HEADS].T
    c_col = c_heads.reshape(N_HEADS, t, 1)
    c_row4 = c_heads.reshape(N_HEADS, nb, 1, tb)
    o_all, og_all, og_t, lse4_mla = _attn_fwd(False, (q_full, kv, kr, proj), t=t, tb=tb, name="mla_fwd")
    o_all, og_all, og_t, lse4_fox = _attn_fwd(True, (proj, c_col, c_row4, o_all, og_all, og_t), t=t, tb=tb,
                                              name="fox_fwd")
    dy, d_o_post, dg_post, loss_part = _out_norm_loss(og_all, w_out_n, x2, target, g_post, tm=min(512, t))

    dw_out = _mm(og_t, d_o_post, name="dw_out", out_dtype=BF16, tm=1024, tn=1024, tk=1024)
    p_out = _by_core(dw_out.reshape(N_DEV, D_MODEL // N_DEV, D_MODEL))
    d_attn, dproj, delta, o_out = _dog_gate(d_o_post, w_out_n, o_all, proj, _ToOtherCore([p_out]), tm=min(1024, t))
    s_out = _pair_sum(p_out, o_out, core, tr=256, name="dw_out_pair_sum")
    delta4 = delta[:, :2 * N_HEADS].T.reshape(2 * N_HEADS, nb, 1, tb)
    dproj, dck, dcq, l_out = _attn_bwd(True, (proj, d_attn, lse4_fox, delta4[N_HEADS:], c_row4, c_col, dproj),
                                       t=t, tb=tb, name="fox_bwd",
                                       rider=_Exchange([(s_out, ALL_CHIPS, None, False, False)]))
    dw_in_rest = _mm(h_t, dproj, name="dw_in_rest", out_dtype=BF16, tm=2048, tn=512, tk=1024,
                     b_cols=(LAT_W, NP_IN - LAT_W))
    p_in = _w_in_blocks(EARLY_CHIPS, None, dw_in_rest)
    (o_in,) = _to_other_core([p_in], name="dw_in_early_to_core")
    s_in = _pair_sum(p_in, o_in, core, tr=256, name="dw_in_early_pair_sum")
    dq_full, dkv, dkr, l_in = _attn_bwd(False, (q_full, kv, kr, d_attn, lse4_mla, delta4[:N_HEADS]),
                                        t=t, tb=tb, name="mla_bwd",
                                        rider=_Exchange([(s_in, EARLY_CHIPS, None, False, False)]))
    dcq_rows = jnp.concatenate([dcq.reshape(N_HEADS, t).T, jnp.zeros((t, LANE - N_HEADS), F32)], axis=1)
    dq2, dkr_raw, dfl, dbf = _mid_bwd(dq_full, dkr, cos_t, sin_t, dcq_rows, dck, flog, bf_row, tm=tm)
    dqn = _mm(dq2, w_uq_n, name="d_qn", nt=True, out_dtype=F32, tm=1024, tn=Q_RANK, tk=2048)
    dkvn = _mm(dkv, w_ukv_n, name="d_kvn", nt=True, out_dtype=F32, tm=1024, tn=KV_RANK, tk=2048)
    dw_uq = _mm(qn_t, dq2, name="dw_uq", out_dtype=BF16, tm=Q_RANK, tn=1024, tk=1024)
    dw_ukv = _mm(kvn_t, dkv, name="dw_ukv", out_dtype=BF16, tm=KV_RANK, tn=1024, tk=1024)
    dw_uq_h = dw_uq.reshape(Q_RANK, N_HEADS, 256)
    s_uq = jnp.concatenate([dw_uq_h[:, :, :160], dw_uq_h[:, :, 192:224]], axis=2).transpose(1, 0, 2)
    s_ukv = dw_ukv.reshape(KV_RANK, N_HEADS, 256).transpose(1, 0, 2)
    up_parts = [_by_core(s_uq), _by_core(s_ukv)]
    dproj, dg_q, dg_kv, *up_other = _norm_bwd(proj, dqn, dkvn, g_q_latent, g_kv_latent, dkr_raw, dfl, dproj,
                                              _ToOtherCore(up_parts), tm=tm)
    up_sums = [_pair_sum(p, o_, core, tr=256, name=f"dw_up_pair_sum_{i}") for i, (p, o_) in enumerate(zip(up_parts, up_other))]
    dw_in_lat, l_uq, l_ukv = _mm(h_t, dproj, name="dw_in_lat", out_dtype=BF16, tm=1024, tn=LAT_W, tk=1024,
                                 b_cols=(0, LAT_W), rider=_Exchange([(s, ALL_CHIPS, None, False, False) for s in up_sums]))
    p_late = _w_in_blocks(LATE_CHIPS, dw_in_lat, dw_in_rest)
    (o_late,) = _to_other_core([p_late], name="dw_in_late_to_core")
    s_late = _pair_sum(p_late, o_late, core, tr=256, name="dw_in_late_pair_sum")
    dh, l_in = _mm(dproj, w_in_n, name="d_h", nt=True, out_dtype=F32, tm=2048, tn=1024, tk=NP_IN // 4,
                   rider=_Exchange([(s_late, LATE_CHIPS, l_in, False, False)]))
    grad_x, dg_pre = _prenorm_bwd(dh, x2, g_pre, dy, tm=tm)

    small = _gather_small(_pack_small(dg_pre, dg_post, dg_q, dg_kv, dbf[:, :N_HEADS], loss_part))

    res_in = _adamw(l_in, w_in[0], m_w_in[0], v_w_in[0], tr=256, name="adamw_w_in")
    res_uq = _adamw(l_uq, w_uq[0], m_w_uq[0], v_w_uq[0], tr=256, name="adamw_w_uq")
    res_ukv = _adamw(l_ukv, w_ukv[0], m_w_ukv[0], v_w_ukv[0], tr=256, name="adamw_w_ukv")
    res_out = _adamw(l_out, w_out[0], m_w_out[0], v_w_out[0], tr=128, name="adamw_w_out")
    res_small = _adamw_small(
        small,
        _pack_small(g_pre, g_post, g_q_latent, g_kv_latent, b_forget),
        _pack_small(m_g_pre, m_g_post, m_g_q_latent, m_g_kv_latent, m_b_forget),
        _pack_small(v_g_pre, v_g_post, v_g_q_latent, v_g_kv_latent, v_b_forget))
    small_out = [_unpack_small(r) for r in res_small]
    loss = small_out[0][1]

    def leaves(kind):
        (s_pre, s_post, s_q, s_kv, s_bf), _ = small_out[kind]
        return [s_pre, res_in[kind][None], s_q, res_uq[kind][None], s_kv, res_ukv[kind][None], s_bf,
                res_out[kind][None], s_post]

    return (loss, grad_x.reshape(x.shape), *leaves(0), *leaves(1), *leaves(2), *leaves(3))
```

```python
import functools

import numpy as np
import jax
import jax.numpy as jnp
from jax import lax
from jax.experimental import pallas as pl
from jax.experimental.pallas import tpu as pltpu

F32 = jnp.float32
BF16 = jnp.bfloat16
MESH = pl.DeviceIdType.MESH

N_DEV = 8
D_MODEL = 2048
N_HEADS = 8
HEAD = 128
Q_RANK = 768
KV_RANK = 512
ROPE = 64
D_IN = 6472
SHARD_IN = D_IN // N_DEV
NORM_EPS = 1e-6
ROPE_THETA = 10000.0
MLA_SCALE = (HEAD + ROPE) ** -0.5
FOX_SCALE = HEAD ** -0.5

QL0, KR0, FL0, KVL0, GM0, GF0, FQ0, FK0, FV0, NP_IN = 0, 768, 896, 1024, 1536, 2560, 3584, 4608, 5632, 6656
LAT_W = GM0
LANE = 128
_SEGMENTS = ((0, 768, QL0), (768, 512, KVL0), (1280, 32, KR0), (1312, 32, KR0 + 64), (1344, 1024, GM0),
             (2368, 3072, FQ0), (5440, 8, FL0), (5448, 1024, GF0))
LOG2E = 1.4426950408889634

ADAM_LR = 0.001
ADAM_B1 = 0.9
ADAM_B2 = 0.999
ADAM_EPS = 1e-08
ADAM_WD = 0.01
ADAM_STEP = 10

VMEM_LIMIT_BYTES = 56 * 1024 * 1024
SMALL_ROWS, SMALL_COLS = 8, 768


def _params(n_grid=0):
    return pltpu.CompilerParams(vmem_limit_bytes=VMEM_LIMIT_BYTES,
                                dimension_semantics=("arbitrary",) * n_grid if n_grid else None)


def _sigmoid(z):
    return 1.0 / (1.0 + jnp.exp(-z))


def _split3(v):
    a = v.astype(BF16)
    r = v - a.astype(F32)
    b = r.astype(BF16)
    c = (r - b.astype(F32)).astype(BF16)
    return a, b, c


def _mm(a, b, *, name, nt=False, out_dtype=F32, tm=1024, tn=512, tk=2048, b_cols=None, rider=None, f32_cols=None):
    m, k_dim = a.shape
    n = b.shape[0] if nt else b.shape[1]
    col0 = 0
    if b_cols is not None:
        assert not nt
        col0, n = b_cols
    assert (b.shape[1] if nt else b.shape[0]) == k_dim
    tm, tn, tk = min(tm, m), min(tn, n), min(tk, k_dim)
    assert m % tm == 0 and n % tn == 0 and k_dim % tk == 0 and col0 % tn == 0, (name, a.shape, b.shape)
    nk = k_dim // tk
    j0 = col0 // tn
    grid = (m // tm, n // tn, nk)
    dims = (((1,), (1 if nt else 0,)), ((), ()))
    n_rin = len(rider.operands) if rider else 0
    n_rout = len(rider.out_shape) if rider else 0
    n_x = 0 if f32_cols is None else 1
    if n_x:
        x_tile, x_lo = divmod(f32_cols[0] - col0, tn)
        x_w = f32_cols[1]
        assert x_lo + x_w <= tn and x_lo % LANE == 0 and x_w % LANE == 0

    def body(*refs):
        a_ref, b_ref = refs[:2]
        o_ref = refs[2 + n_rin]
        acc_ref = refs[3 + n_rin + n_rout + n_x]
        i, j, k = pl.program_id(0), pl.program_id(1), pl.program_id(2)
        if rider:
            rider_refs = (refs[2:2 + n_rin], refs[3 + n_rin:3 + n_rin + n_rout], refs[4 + n_rin + n_rout + n_x:])

            @pl.when(jnp.logical_and(i == 0, jnp.logical_and(j == 0, k == 0)))
            def _():
                rider.start(*rider_refs)

        @pl.when(k == 0)
        def _():
            acc_ref[...] = jnp.zeros_like(acc_ref)

        acc_ref[...] += lax.dot_general(a_ref[...], b_ref[...], dims, preferred_element_type=F32)

        @pl.when(k == nk - 1)
        def _():
            o_ref[...] = acc_ref[...].astype(o_ref.dtype)

        if n_x:
            @pl.when(jnp.logical_and(k == nk - 1, j == x_tile))
            def _():
                refs[3 + n_rin + n_rout][...] = acc_ref[:, x_lo:x_lo + x_w]

        if rider:
            @pl.when(jnp.logical_and(i == grid[0] - 1, jnp.logical_and(j == grid[1] - 1, k == nk - 1)))
            def _():
                rider.wait(*rider_refs)

    b_spec = (pl.BlockSpec((tn, tk), lambda i, j, k: (j, k)) if nt
              else pl.BlockSpec((tk, tn), lambda i, j, k: (k, j0 + j)))
    a_spec = pl.BlockSpec((tm, tk), lambda i, j, k: (i, k))
    any_spec = pl.BlockSpec(memory_space=pl.ANY)
    out = pl.pallas_call(
        body, name=name, grid=grid,
        in_specs=[a_spec, b_spec] + [any_spec] * n_rin,
        out_specs=[pl.BlockSpec((tm, tn), lambda i, j, k: (i, j))] + [any_spec] * n_rout
        + ([pl.BlockSpec((tm, x_w), lambda i, j, k: (i, 0))] if n_x else []),
        out_shape=[jax.ShapeDtypeStruct((m, n), out_dtype)] + (list(rider.out_shape) if rider else [])
        + ([jax.ShapeDtypeStruct((m, x_w), F32)] if n_x else []),
        scratch_shapes=[pltpu.VMEM((tm, tn), F32)] + (list(rider.scratch) if rider else []),
        input_output_aliases={2 + i_in: 1 + i_out for i_in, i_out in rider.aliases.items()} if rider else {},
        compiler_params=_params(3),
    )(a, b, *(rider.operands if rider else ()))
    return out if rider or n_x else out[0]


def _prenorm(x, g, rider, *, tm):
    t = x.shape[0]
    n_steps = t // tm
    n_rin, n_rout = len(rider.operands), len(rider.out_shape)

    def body(*refs):
        x_ref, g_ref = refs[:2]
        h_ref, ht_ref = refs[2 + n_rin:4 + n_rin]
        rider_refs = (refs[2:2 + n_rin], refs[4 + n_rin:4 + n_rin + n_rout], refs[4 + n_rin + n_rout:])
        i = pl.program_id(0)
        pl.when(i == 0)(lambda: rider.start(*rider_refs))
        xv = x_ref[...]
        r = lax.rsqrt(jnp.mean(xv * xv, axis=-1, keepdims=True) + NORM_EPS)
        h = xv * r * g_ref[...]
        h_ref[...] = h.astype(BF16)
        ht_ref[...] = h.T.astype(BF16)
        pl.when(i == n_steps - 1)(lambda: rider.wait(*rider_refs))

    any_spec = pl.BlockSpec(memory_space=pl.ANY)
    return pl.pallas_call(
        body, name="prenorm", grid=(n_steps,),
        in_specs=[pl.BlockSpec((tm, D_MODEL), lambda i: (i, 0)), pl.BlockSpec((1, D_MODEL), lambda i: (0, 0))]
        + [any_spec] * n_rin,
        out_specs=[pl.BlockSpec((tm, D_MODEL), lambda i: (i, 0)), pl.BlockSpec((D_MODEL, tm), lambda i: (0, i))]
        + [any_spec] * n_rout,
        out_shape=[jax.ShapeDtypeStruct((t, D_MODEL), BF16), jax.ShapeDtypeStruct((D_MODEL, t), BF16)]
        + list(rider.out_shape),
        scratch_shapes=list(rider.scratch),
        compiler_params=_params(1),
    )(x, g, *rider.operands)


def _rope_rows():
    inv = (np.float32(ROPE_THETA) ** (-np.arange(0, ROPE, 2, dtype=np.float32) / np.float32(ROPE))).astype(np.float32)
    invf = np.zeros((1, LANE), np.float32)
    sgn = np.zeros((1, LANE), np.float32)
    invf[0, 0:32] = inv
    invf[0, 64:96] = inv
    sgn[0, 0:32] = -1.0
    sgn[0, 64:96] = 1.0
    return jnp.asarray(invf), jnp.asarray(sgn)


def _rot(v, cos_t, sin_t):
    return v * cos_t + pltpu.roll(v, 64, 1) * sin_t


def _rot_bwd(dv, cos_t, sin_t):
    return dv * cos_t + pltpu.roll(dv * sin_t, 64, 1)


def _mid_fwd(proj, flog, g_q, g_kv, bf_row, pos_col, rider, *, tm):
    t = proj.shape[0]
    n_steps = t // tm
    invf, sgn = _rope_rows()
    n_rin, n_rout = len(rider.operands), len(rider.out_shape)

    def body(*refs):
        p_ref, fl_ref, gq_ref, gkv_ref, bf_ref, pos_ref, invf_ref, sgn_ref = refs[:8]
        qn_ref, kvn_ref, kr_ref, cos_ref, sin_ref, c_ref, qnt_ref, kvnt_ref = refs[8 + n_rin:16 + n_rin]
        carry_ref = refs[16 + n_rin + n_rout]
        rider_refs = (refs[8:8 + n_rin], refs[16 + n_rin:16 + n_rin + n_rout], refs[17 + n_rin + n_rout:])
        i = pl.program_id(0)
        pl.when(i == 0)(lambda: rider.start(*rider_refs))
        pl.when(i == n_steps - 1)(lambda: rider.wait(*rider_refs))

        @pl.when(i == 0)
        def _():
            carry_ref[...] = jnp.zeros_like(carry_ref)

        ql = p_ref[:, QL0:QL0 + Q_RANK].astype(F32)
        r = lax.rsqrt(jnp.mean(ql * ql, axis=-1, keepdims=True) + NORM_EPS)
        qn = ql * r * gq_ref[...]
        qn_ref[...] = qn.astype(BF16)
        qnt_ref[...] = qn.T.astype(BF16)
        kvl = p_ref[:, KVL0:KVL0 + KV_RANK].astype(F32)
        r = lax.rsqrt(jnp.mean(kvl * kvl, axis=-1, keepdims=True) + NORM_EPS)
        kvn = kvl * r * gkv_ref[...]
        kvn_ref[...] = kvn.astype(BF16)
        kvnt_ref[...] = kvn.T.astype(BF16)

        ang = pos_ref[...] * invf_ref[...]
        cos_t = jnp.cos(ang)
        sin_t = jnp.sin(ang) * sgn_ref[...]
        cos_ref[...] = cos_t
        sin_ref[...] = sin_t
        kr_ref[...] = _rot(p_ref[:, KR0:KR0 + LANE].astype(F32), cos_t, sin_t).astype(BF16)

        z = fl_ref[...] + bf_ref[...]
        logf = jnp.minimum(z, 0.0) - jnp.log(1.0 + jnp.exp(-jnp.abs(z)))
        row = lax.broadcasted_iota(jnp.int32, (tm, tm), 0)
        col = lax.broadcasted_iota(jnp.int32, (tm, tm), 1)
        tri = (col <= row).astype(BF16)
        acc = carry_ref[0:1, :]
        for part in _split3(logf):
            acc = acc + jnp.dot(tri, part, preferred_element_type=F32)
        c_ref[...] = acc * (1.0 / FOX_SCALE)
        carry_ref[0:1, :] = carry_ref[0:1, :] + jnp.sum(logf, axis=0, keepdims=True)

    row_spec = lambda w: pl.BlockSpec((tm, w), lambda i: (i, 0))
    vec_spec = lambda w: pl.BlockSpec((1, w), lambda i: (0, 0))
    return pl.pallas_call(
        body, name="mid_fwd", grid=(n_steps,),
        in_specs=[row_spec(LAT_W), row_spec(LANE), vec_spec(Q_RANK), vec_spec(KV_RANK), vec_spec(LANE),
                  pl.BlockSpec((tm, 1), lambda i: (i, 0)), vec_spec(LANE), vec_spec(LANE)]
        + [pl.BlockSpec(memory_space=pl.ANY)] * n_rin,
        out_specs=[row_spec(Q_RANK), row_spec(KV_RANK), row_spec(LANE), row_spec(LANE), row_spec(LANE), row_spec(LANE),
                   pl.BlockSpec((Q_RANK, tm), lambda i: (0, i)), pl.BlockSpec((KV_RANK, tm), lambda i: (0, i))]
        + [pl.BlockSpec(memory_space=pl.ANY)] * n_rout,
        out_shape=[jax.ShapeDtypeStruct((t, Q_RANK), BF16), jax.ShapeDtypeStruct((t, KV_RANK), BF16),
                   jax.ShapeDtypeStruct((t, LANE), BF16), jax.ShapeDtypeStruct((t, LANE), F32),
                   jax.ShapeDtypeStruct((t, LANE), F32), jax.ShapeDtypeStruct((t, LANE), F32),
                   jax.ShapeDtypeStruct((Q_RANK, t), BF16), jax.ShapeDtypeStruct((KV_RANK, t), BF16)]
        + list(rider.out_shape),
        scratch_shapes=[pltpu.VMEM((8, LANE), F32)] + list(rider.scratch),
        input_output_aliases={8 + i_in: 8 + i_out for i_in, i_out in rider.aliases.items()},
        compiler_params=_params(1),
    )(proj, flog, g_q, g_kv, bf_row, pos_col, invf, sgn, *rider.operands)


def _q_up_rope(qn, w_uq_n, cos_t, sin_t, *, tm):
    t = qn.shape[0]
    tn = 2 * 256

    def body(a_ref, b_ref, cos_ref, sin_ref, o_ref):
        q = jnp.dot(a_ref[...], b_ref[...], preferred_element_type=F32)
        c, s = cos_ref[...], sin_ref[...]
        for u in range(tn // 256):
            o_ref[:, 256 * u:256 * u + 128] = q[:, 256 * u:256 * u + 128].astype(BF16)
            o_ref[:, 256 * u + 128:256 * u + 256] = _rot(q[:, 256 * u + 128:256 * u + 256], c, s).astype(BF16)

    return pl.pallas_call(
        body, name="q_up_rope", grid=(t // tm, N_HEADS * 256 // tn),
        in_specs=[pl.BlockSpec((tm, Q_RANK), lambda i, j: (i, 0)), pl.BlockSpec((Q_RANK, tn), lambda i, j: (0, j)),
                  pl.BlockSpec((tm, LANE), lambda i, j: (i, 0)), pl.BlockSpec((tm, LANE), lambda i, j: (i, 0))],
        out_specs=pl.BlockSpec((tm, tn), lambda i, j: (i, j)),
        out_shape=jax.ShapeDtypeStruct((t, N_HEADS * 256), BF16),
        compiler_params=_params(2),
    )(qn, w_uq_n, cos_t, sin_t)


def _attn_fwd(fox, operands, *, t, tb, name):
    nb = t // tb
    scale = FOX_SCALE if fox else MLA_SCALE
    exp2_scale = scale * LOG2E
    pair = 2 * HEAD
    pair0 = N_HEADS // 2 if fox else 0
    q_w = HEAD if fox else 2 * HEAD
    nt_dims = (((1,), (1,)), ((), ()))
    tn_dims = (((0,), (0,)), ((), ()))

    def body(*refs):
        if fox:
            (q_ref, k_ref, v_ref, gate_ref, cq_ref, ck_ref, _, _, _,
             o_ref, og_ref, ogt_ref, lse_ref, m_s, l_s, acc_s) = refs
        else:
            q_ref, kv_ref, kr_ref, gate_ref, o_ref, og_ref, ogt_ref, lse_ref, m_s, l_s, acc_s = refs
        qi = pl.program_id(1)
        m_s[...] = jnp.full_like(m_s, -jnp.inf)
        l_s[...] = jnp.zeros_like(l_s)
        acc_s[...] = jnp.zeros_like(acc_s)

        def chunk(kc, masked):
            off = pl.multiple_of(kc * tb, tb)
            scores = []
            for u in range(2):
                q = q_ref[:, q_w * u:q_w * (u + 1)]
                if fox:
                    kk = k_ref[pl.ds(off, tb), HEAD * u:HEAD * (u + 1)]
                else:
                    kk = jnp.concatenate([kv_ref[pl.ds(off, tb), pair * u:pair * u + HEAD],
                                          kr_ref[pl.ds(off, tb), :]], axis=1)
                s = lax.dot_general(kk, q, nt_dims, preferred_element_type=F32)
                if fox:
                    s = s + cq_ref[u, 0] - ck_ref[u, pl.ds(off, tb), :]
                if masked:
                    row = lax.broadcasted_iota(jnp.int32, (tb, tb), 0)
                    col = lax.broadcasted_iota(jnp.int32, (tb, tb), 1)
                    s = jnp.where(row <= col, s, -jnp.inf)
                scores.append(s)
            for u in range(2):
                s = scores[u]
                m_prev = m_s[u]
                m_new = jnp.maximum(m_prev, jnp.max(s, axis=0, keepdims=True))
                alpha = jnp.exp2((m_prev - m_new) * exp2_scale)
                p = jnp.exp2((s - m_new) * exp2_scale)
                l_s[u] = alpha * l_s[u] + jnp.sum(p, axis=0, keepdims=True)
                if fox:
                    vv = v_ref[pl.ds(off, tb), HEAD * u:HEAD * (u + 1)]
                else:
                    vv = kv_ref[pl.ds(off, tb), pair * u + HEAD:pair * (u + 1)]
                acc_s[u] = alpha * acc_s[u] + lax.dot_general(vv, p.astype(BF16), tn_dims,
                                                              preferred_element_type=F32)
                m_s[u] = m_new

        def loop_body(kc, carry):
            chunk(kc, False)
            return carry

        lax.fori_loop(0, qi, loop_body, 0)
        chunk(qi, True)
        for u in range(2):
            cols = slice(HEAD * u, HEAD * (u + 1))
            o_t = acc_s[u] / l_s[u]
            o = o_t.T
            o_ref[:, cols] = o
            g = gate_ref[:, cols].astype(F32)
            silu = g * _sigmoid(g)
            og_ref[:, cols] = (o * silu).astype(BF16)
            ogt_ref[cols, :] = (o_t * silu.T).astype(BF16)
            lse_ref[u, 0] = m_s[u] * scale + jnp.log(l_s[u])

    any_spec = pl.BlockSpec(memory_space=pl.ANY)
    row_stat = pl.BlockSpec((2, 1, 1, tb), lambda g, i: (g, i, 0, 0))
    if fox:
        proj, c_col, c_row4, o_all, og_all, ogt_all = operands
        ins = [proj, proj, proj, proj, c_row4, c_col, o_all, og_all, ogt_all]
        in_specs = [pl.BlockSpec((tb, pair), lambda g, i: (i, FQ0 // pair + g)),
                    pl.BlockSpec((t, pair), lambda g, i: (0, FK0 // pair + g)),
                    pl.BlockSpec((t, pair), lambda g, i: (0, FV0 // pair + g)),
                    pl.BlockSpec((tb, pair), lambda g, i: (i, GF0 // pair + g)),
                    row_stat, pl.BlockSpec((2, t, 1), lambda g, i: (g, 0, 0)), any_spec, any_spec, any_spec]
        aliases = {6: 0, 7: 1, 8: 2}
    else:
        q_full, kv, kr, proj = operands
        ins = [q_full, kv, kr, proj]
        in_specs = [pl.BlockSpec((tb, 2 * pair), lambda g, i: (i, g)),
                    pl.BlockSpec((t, 2 * pair), lambda g, i: (0, g)),
                    pl.BlockSpec((t, HEAD), lambda g, i: (0, 0)),
                    pl.BlockSpec((tb, pair), lambda g, i: (i, GM0 // pair + g))]
        aliases = {}
    return pl.pallas_call(
        body, name=name, grid=(N_HEADS // 2, nb), in_specs=in_specs,
        out_specs=[pl.BlockSpec((tb, pair), lambda g, i: (i, pair0 + g)),
                   pl.BlockSpec((tb, pair), lambda g, i: (i, pair0 + g)),
                   pl.BlockSpec((pair, tb), lambda g, i: (pair0 + g, i)), row_stat],
        out_shape=[jax.ShapeDtypeStruct((t, 2 * N_HEADS * HEAD), F32), jax.ShapeDtypeStruct((t, 2 * N_HEADS * HEAD), BF16),
                   jax.ShapeDtypeStruct((2 * N_HEADS * HEAD, t), BF16), jax.ShapeDtypeStruct((N_HEADS, nb, 1, tb), F32)],
        scratch_shapes=[pltpu.VMEM((2, 1, tb), F32), pltpu.VMEM((2, 1, tb), F32), pltpu.VMEM((2, HEAD, tb), F32)],
        input_output_aliases=aliases,
        compiler_params=_params(2),
    )(*ins)


def _out_norm_loss(og, w_out_n, x, target, g, *, tm):
    t = og.shape[0]

    def body(og_ref, w_ref, x_ref, t_ref, g_ref, dy_ref, do_ref, dg_ref, loss_ref):
        i = pl.program_id(0)

        @pl.when(i == 0)
        def _():
            dg_ref[...] = jnp.zeros_like(dg_ref)
            loss_ref[...] = jnp.zeros_like(loss_ref)

        ov = jnp.dot(og_ref[...], w_ref[...], preferred_element_type=F32)
        gv = g_ref[...]
        r = lax.rsqrt(jnp.mean(ov * ov, axis=-1, keepdims=True) + NORM_EPS)
        oh = ov * r
        e = x_ref[...] + oh * gv - t_ref[...]
        loss_ref[...] += 0.5 * jnp.sum(jnp.mean(e * e, axis=-1, keepdims=True), axis=0, keepdims=True)
        dy = e * (1.0 / D_MODEL)
        dy_ref[...] = dy
        dyg = dy * gv
        do_ref[...] = (r * (dyg - oh * jnp.mean(dyg * oh, axis=-1, keepdims=True))).astype(BF16)
        dg_ref[...] += jnp.sum(dy * oh, axis=0, keepdims=True)

    row = pl.BlockSpec((tm, D_MODEL), lambda i: (i, 0))
    vec = pl.BlockSpec((1, D_MODEL), lambda i: (0, 0))
    whole_w = pl.BlockSpec((D_MODEL, D_MODEL), lambda i: (0, 0), pipeline_mode=pl.Buffered(1))
    return pl.pallas_call(
        body, name="out_norm_loss", grid=(t // tm,),
        in_specs=[row, whole_w, row, row, vec],
        out_specs=[row, row, vec, pl.BlockSpec((1, 1), lambda i: (0, 0))],
        out_shape=[jax.ShapeDtypeStruct((t, D_MODEL), F32), jax.ShapeDtypeStruct((t, D_MODEL), BF16),
                   jax.ShapeDtypeStruct((1, D_MODEL), F32), jax.ShapeDtypeStruct((1, 1), F32)],
        compiler_params=_params(1),
    )(og, w_out_n, x, target, g)


def _dog_gate(d_o_post, w_out_n, o_all, proj, rider, *, tm):
    t = d_o_post.shape[0]
    n_group = 4
    pair = n_group * HEAD
    gate_blk = GM0 // pair
    assert GM0 % pair == 0 and GF0 == GM0 + N_HEADS * HEAD
    grid = (t // tm, 2 * N_HEADS // n_group)
    n_rin, n_rout = len(rider.operands), len(rider.out_shape)

    def body(*refs):
        do_ref, w_ref, o_ref, p_ref = refs[:4]
        dattn_ref, dproj_ref, delta_ref = refs[4 + n_rin:7 + n_rin]
        rider_refs = (refs[4:4 + n_rin], refs[7 + n_rin:7 + n_rin + n_rout], refs[7 + n_rin + n_rout:])
        i, j = pl.program_id(0), pl.program_id(1)
        pl.when(jnp.logical_and(i == 0, j == 0))(lambda: rider.start(*rider_refs))

        @pl.when(j == 0)
        def _():
            delta_ref[...] = jnp.zeros_like(delta_ref)

        dog = lax.dot_general(do_ref[...], w_ref[...], (((1,), (1,)), ((), ())), preferred_element_type=F32)
        g = p_ref[...].astype(F32)
        ov = o_ref[...]
        sg = _sigmoid(g)
        d_o = dog * (g * sg)
        dattn_ref[...] = d_o.astype(BF16)
        dproj_ref[...] = (dog * ov * (sg * (1.0 + g * (1.0 - sg)))).astype(BF16)
        prod = d_o * ov
        lane = lax.broadcasted_iota(jnp.int32, (tm, LANE), 1)
        delta = delta_ref[...]
        for u in range(n_group):
            part = jnp.sum(prod[:, HEAD * u:HEAD * (u + 1)], axis=-1, keepdims=True)
            delta = jnp.where(lane == n_group * j + u, part, delta)
        delta_ref[...] = delta
        pl.when(jnp.logical_and(i == grid[0] - 1, j == grid[1] - 1))(lambda: rider.wait(*rider_refs))

    any_spec = pl.BlockSpec(memory_space=pl.ANY)
    return pl.pallas_call(
        body, name="dog_gate", grid=grid,
        in_specs=[pl.BlockSpec((tm, D_MODEL), lambda i, j: (i, 0)), pl.BlockSpec((pair, D_MODEL), lambda i, j: (j, 0)),
                  pl.BlockSpec((tm, pair), lambda i, j: (i, j)), pl.BlockSpec((tm, pair), lambda i, j: (i, gate_blk + j))]
        + [any_spec] * n_rin,
        out_specs=[pl.BlockSpec((tm, pair), lambda i, j: (i, j)), pl.BlockSpec((tm, pair), lambda i, j: (i, gate_blk + j)),
                   pl.BlockSpec((tm, LANE), lambda i, j: (i, 0))] + [any_spec] * n_rout,
        out_shape=[jax.ShapeDtypeStruct((t, 2048), BF16), jax.ShapeDtypeStruct((t, NP_IN), BF16),
                   jax.ShapeDtypeStruct((t, LANE), F32)] + list(rider.out_shape),
        scratch_shapes=list(rider.scratch),
        compiler_params=_params(2),
    )(d_o_post, w_out_n, o_all, proj, *rider.operands)


def _attn_bwd(fox, operands, *, t, tb, name, rider=None):
    nb = t // tb
    n_pairs = N_HEADS // 2
    pair = 2 * HEAD
    scale = FOX_SCALE if fox else MLA_SCALE
    q_w = HEAD if fox else 2 * HEAD
    nt_dims = (((1,), (1,)), ((), ()))
    tn_dims = (((0,), (0,)), ((), ()))
    n_rin = len(rider.operands) if rider else 0
    n_rout = len(rider.out_shape) if rider else 0
    n_in, n_out, n_scr = (9, 3, 9) if fox else (6, 3, 2)

    def body(*refs):
        ends = np.cumsum([0, n_in, n_rin, n_out, n_rout, n_scr])
        in_refs, rider_in, out_refs, rider_out, scr_refs = (refs[a:b] for a, b in zip(ends[:-1], ends[1:]))
        rider_refs = (rider_in, rider_out, refs[ends[-1]:])
        if fox:
            q_ref, k_ref, v_ref, do_ref, lse_ref, dl_ref, cq_ref, ck_ref, _ = in_refs
            dproj_ref, dck_ref, dcq_ref = out_refs
            dq_acc, dk_s, dv_s, dc_s, dcq_s, stage_q, stage_k, stage_v, put_sems = scr_refs
        else:
            q_ref, kv_ref, kr_ref, do_ref, lse_ref, dl_ref = in_refs
            dq_acc, dkv_ref, dkr_ref = out_refs
            dk_s, dv_s = scr_refs
        g = pl.program_id(0)
        ki = pl.program_id(1)
        if rider:
            @pl.when(jnp.logical_and(g == 0, ki == 0))
            def _():
                rider.start(*rider_refs)

        @pl.when(ki == 0)
        def _():
            dq_acc[...] = jnp.zeros_like(dq_acc)
            if fox:
                dcq_s[...] = jnp.zeros_like(dcq_s)

        dk_s[...] = jnp.zeros_like(dk_s)
        dv_s[...] = jnp.zeros_like(dv_s)
        if fox:
            dc_s[...] = jnp.zeros_like(dc_s)
            keys = [k_ref[:, HEAD * u:HEAD * (u + 1)] for u in range(2)]
            vals = [v_ref[:, HEAD * u:HEAD * (u + 1)] for u in range(2)]
        else:
            keys = [jnp.concatenate([kv_ref[:, pair * u:pair * u + HEAD], kr_ref[...]], axis=1) for u in range(2)]
            vals = [kv_ref[:, pair * u + HEAD:pair * (u + 1)] for u in range(2)]

        def chunk(qc, masked):
            off = pl.multiple_of(qc * tb, tb)
            for u in range(2):
                kk, vv = keys[u], vals[u]
                qq = q_ref[pl.ds(off, tb), q_w * u:q_w * (u + 1)]
                dd = do_ref[pl.ds(off, tb), HEAD * u:HEAD * (u + 1)]
                s = lax.dot_general(kk, qq, nt_dims, preferred_element_type=F32)
                if fox:
                    s = s + cq_ref[u, qc] - ck_ref[u]
                if masked:
                    row = lax.broadcasted_iota(jnp.int32, (tb, tb), 0)
                    col = lax.broadcasted_iota(jnp.int32, (tb, tb), 1)
                    s = jnp.where(row <= col, s, -jnp.inf)
                p = jnp.exp2(s * (scale * LOG2E) - lse_ref[u, qc] * LOG2E)
                dv_s[u] += jnp.dot(p.astype(BF16), dd, preferred_element_type=F32)
                dp = lax.dot_general(vv, dd, nt_dims, preferred_element_type=F32)
                ds = p * (dp - dl_ref[u, qc])
                if fox:
                    dc_s[u] += jnp.sum(ds, axis=1, keepdims=True)
                    dcq_s[u, qc] += jnp.sum(ds, axis=0, keepdims=True)
                dsb = (ds * scale).astype(BF16)
                dk_s[u] += jnp.dot(dsb, qq, preferred_element_type=F32)
                dq_acc[pl.ds(off, tb), q_w * u:q_w * (u + 1)] += lax.dot_general(dsb, kk, tn_dims,
                                                                                 preferred_element_type=F32)

        chunk(ki, True)

        def loop_body(qc, carry):
            chunk(qc, False)
            return carry

        lax.fori_loop(ki + 1, nb, loop_body, 0)

        def put(stage_ref, rows, seg0, sem):
            col0 = pl.multiple_of(seg0 + g * pair, pair)
            return pltpu.make_async_copy(stage_ref, dproj_ref.at[rows, pl.ds(col0, pair)], sem)

        if fox:
            rows = pl.ds(pl.multiple_of(ki * tb, tb), tb)
            block_puts = [put(stage_k, rows, FK0, put_sems.at[1]), put(stage_v, rows, FV0, put_sems.at[2])]
            pair_put = put(stage_q, pl.ds(0, t), FQ0, put_sems.at[0])

            @pl.when(jnp.logical_or(g > 0, ki > 0))
            def _():
                for cp in block_puts:
                    cp.wait()

            for u in range(2):
                stage_k[:, HEAD * u:HEAD * (u + 1)] = dk_s[u].astype(BF16)
                stage_v[:, HEAD * u:HEAD * (u + 1)] = dv_s[u].astype(BF16)
                dck_ref[u] = -dc_s[u]
            for cp in block_puts:
                cp.start()

            @pl.when(ki == nb - 1)
            def _():
                @pl.when(g > 0)
                def _():
                    pair_put.wait()

                stage_q[...] = dq_acc[...].astype(BF16)
                pair_put.start()
                dcq_ref[...] = dcq_s[...]

            @pl.when(jnp.logical_and(g == n_pairs - 1, ki == nb - 1))
            def _():
                for cp in block_puts + [pair_put]:
                    cp.wait()
        else:
            dkv_ref[...] = jnp.concatenate([dk_s[0, :, :HEAD], dv_s[0], dk_s[1, :, :HEAD], dv_s[1]], axis=1).astype(BF16)
            dkr_ref[...] = jnp.concatenate([dk_s[0, :, HEAD:], dk_s[1, :, HEAD:]], axis=1)

        if rider:
            @pl.when(jnp.logical_and(g == n_pairs - 1, ki == nb - 1))
            def _():
                rider.wait(*rider_refs)

    stat = pl.BlockSpec((2, nb, 1, tb), lambda g, i: (g, 0, 0, 0))
    aliases = {}
    if fox:
        proj, d_o, lse4, delta4, c_row4, c_col, dproj = operands
        ins = [proj, proj, proj, d_o, lse4, delta4, c_row4, c_col, dproj]
        any_spec = pl.BlockSpec(memory_space=pl.ANY)
        in_specs = [pl.BlockSpec((t, pair), lambda g, i: (0, FQ0 // pair + g)),
                    pl.BlockSpec((tb, pair), lambda g, i: (i, FK0 // pair + g)),
                    pl.BlockSpec((tb, pair), lambda g, i: (i, FV0 // pair + g)),
                    pl.BlockSpec((t, pair), lambda g, i: (0, n_pairs + g)),
                    stat, stat, stat, pl.BlockSpec((2, tb, 1), lambda g, i: (g, i, 0)), any_spec]
        aliases = {8: 0}
        out_specs = [any_spec, pl.BlockSpec((2, tb, 1), lambda g, i: (g, i, 0)), stat]
        out_shape = [jax.ShapeDtypeStruct(dproj.shape, dproj.dtype), jax.ShapeDtypeStruct((N_HEADS, t, 1), F32),
                     jax.ShapeDtypeStruct((N_HEADS, nb, 1, tb), F32)]
        scratch = [pltpu.VMEM((t, pair), F32), pltpu.VMEM((2, tb, HEAD), F32), pltpu.VMEM((2, tb, HEAD), F32),
                   pltpu.VMEM((2, tb, 1), F32), pltpu.VMEM((2, nb, 1, tb), F32),
                   pltpu.VMEM((t, pair), BF16), pltpu.VMEM((tb, pair), BF16), pltpu.VMEM((tb, pair), BF16),
                   pltpu.SemaphoreType.DMA((3,))]
    else:
        q_full, kv, kr, d_o, lse4, delta4 = operands
        ins = [q_full, kv, kr, d_o, lse4, delta4]
        in_specs = [pl.BlockSpec((t, 2 * pair), lambda g, i: (0, g)),
                    pl.BlockSpec((tb, 2 * pair), lambda g, i: (i, g)),
                    pl.BlockSpec((tb, HEAD), lambda g, i: (i, 0)),
                    pl.BlockSpec((t, pair), lambda g, i: (0, g)),
                    stat, stat]
        out_specs = [pl.BlockSpec((t, 2 * pair), lambda g, i: (0, g)), pl.BlockSpec((tb, 2 * pair), lambda g, i: (i, g)),
                     pl.BlockSpec((tb, pair), lambda g, i: (i, g))]
        out_shape = [jax.ShapeDtypeStruct((t, 2048), F32), jax.ShapeDtypeStruct((t, 2048), BF16),
                     jax.ShapeDtypeStruct((t, 1024), F32)]
        scratch = [pltpu.VMEM((2, tb, 2 * HEAD), F32), pltpu.VMEM((2, tb, HEAD), F32)]
    assert (len(ins), len(out_specs), len(scratch)) == (n_in, n_out, n_scr)
    if rider:
        any_spec = pl.BlockSpec(memory_space=pl.ANY)
        aliases = {**aliases, **{n_in + i_in: n_out + i_out for i_in, i_out in rider.aliases.items()}}
        ins = ins + list(rider.operands)
        in_specs = in_specs + [any_spec] * n_rin
        out_specs = out_specs + [any_spec] * n_rout
        out_shape = out_shape + list(rider.out_shape)
        scratch = scratch + list(rider.scratch)
    return pl.pallas_call(
        body, name=name, grid=(n_pairs, nb), in_specs=in_specs, out_specs=out_specs, out_shape=out_shape,
        scratch_shapes=scratch, input_output_aliases=aliases, compiler_params=_params(2),
    )(*ins)


def _mid_bwd(dq_full, dkr, cos_t, sin_t, dcq, dck, flog, bf_row, *, tm):
    t = dq_full.shape[0]
    n = t // tm

    def body(dq_ref, dkr_ref, cos_ref, sin_ref, dcq_ref, dck_ref, fl_ref, bf_ref,
             dq2_ref, dkraw_ref, dfl_ref, dbf_ref, carry_ref):
        i = pl.program_id(0)

        @pl.when(i == 0)
        def _():
            carry_ref[...] = jnp.zeros_like(carry_ref)
            dbf_ref[...] = jnp.zeros_like(dbf_ref)

        c, s = cos_ref[...], sin_ref[...]
        dkr_sum = jnp.zeros((tm, LANE), F32)
        for h in range(N_HEADS):
            dq2_ref[:, 256 * h:256 * h + 128] = dq_ref[:, 256 * h:256 * h + 128].astype(BF16)
            dq2_ref[:, 256 * h + 128:256 * h + 256] = _rot_bwd(dq_ref[:, 256 * h + 128:256 * h + 256], c, s).astype(BF16)
            dkr_sum = dkr_sum + dkr_ref[:, HEAD * h:HEAD * (h + 1)]
        dkraw_ref[...] = _rot_bwd(dkr_sum, c, s).astype(BF16)

        dc = dcq_ref[...]
        lane = lax.broadcasted_iota(jnp.int32, (tm, LANE), 1)
        for h in range(N_HEADS):
            dc = dc + jnp.where(lane == h, dck_ref[h], 0.0)
        row = lax.broadcasted_iota(jnp.int32, (tm, tm), 0)
        col = lax.broadcasted_iota(jnp.int32, (tm, tm), 1)
        tri = (col >= row).astype(BF16)
        acc = carry_ref[0:1, :]
        for part in _split3(dc):
            acc = acc + jnp.dot(tri, part, preferred_element_type=F32)
        carry_ref[0:1, :] = carry_ref[0:1, :] + jnp.sum(dc, axis=0, keepdims=True)
        z = fl_ref[...] + bf_ref[...]
        dz = acc / (1.0 + jnp.exp(z))
        dfl_ref[...] = dz.astype(BF16)
        dbf_ref[...] += jnp.sum(dz, axis=0, keepdims=True)

    rev = lambda w: pl.BlockSpec((tm, w), lambda i: (n - 1 - i, 0))
    vec = lambda w: pl.BlockSpec((1, w), lambda i: (0, 0))
    return pl.pallas_call(
        body, name="mid_bwd", grid=(n,),
        in_specs=[rev(2048), rev(1024), rev(LANE), rev(LANE), rev(LANE),
                  pl.BlockSpec((N_HEADS, tm, 1), lambda i: (0, n - 1 - i, 0)), rev(LANE), vec(LANE)],
        out_specs=[rev(2048), rev(LANE), rev(LANE), vec(LANE)],
        out_shape=[jax.ShapeDtypeStruct((t, 2048), BF16), jax.ShapeDtypeStruct((t, LANE), BF16),
                   jax.ShapeDtypeStruct((t, LANE), BF16), jax.ShapeDtypeStruct((1, LANE), F32)],
        scratch_shapes=[pltpu.VMEM((8, LANE), F32)],
        compiler_params=_params(1),
    )(dq_full, dkr, cos_t, sin_t, dcq, dck, flog, bf_row)


def _norm_bwd(proj, dqn, dkvn, g_q, g_kv, dkr_raw, dfl, dproj, rider, *, tm):
    t = proj.shape[0]
    n_steps = t // tm
    assert (KR0, FL0, KVL0, LAT_W) == (Q_RANK, Q_RANK + LANE, Q_RANK + 2 * LANE, Q_RANK + 2 * LANE + KV_RANK)
    n_rin, n_rout = len(rider.operands), len(rider.out_shape)

    def body(*refs):
        p_ref, dqn_ref, dkvn_ref, gq_ref, gkv_ref, dkr_ref, dfl_ref = refs[:7]
        dproj_ref, dgq_ref, dgkv_ref = refs[8 + n_rin:11 + n_rin]
        rider_refs = (refs[8:8 + n_rin], refs[11 + n_rin:11 + n_rin + n_rout], refs[11 + n_rin + n_rout:])
        i = pl.program_id(0)
        pl.when(i == 0)(lambda: rider.start(*rider_refs))
        pl.when(i == n_steps - 1)(lambda: rider.wait(*rider_refs))

        @pl.when(i == 0)
        def _():
            dgq_ref[...] = jnp.zeros_like(dgq_ref)
            dgkv_ref[...] = jnp.zeros_like(dgkv_ref)

        d_lat = []
        for lo, w, dn_ref, g_ref, dg_ref in ((QL0, Q_RANK, dqn_ref, gq_ref, dgq_ref),
                                             (KVL0, KV_RANK, dkvn_ref, gkv_ref, dgkv_ref)):
            xv = p_ref[:, lo:lo + w].astype(F32)
            r = lax.rsqrt(jnp.mean(xv * xv, axis=-1, keepdims=True) + NORM_EPS)
            xh = xv * r
            dn = dn_ref[...]
            dg_ref[...] += jnp.sum(dn * xh, axis=0, keepdims=True)
            dxh = dn * g_ref[...]
            d_lat.append((r * (dxh - xh * jnp.mean(dxh * xh, axis=-1, keepdims=True))).astype(BF16))
        dproj_ref[...] = jnp.concatenate([d_lat[0], dkr_ref[...], dfl_ref[...], d_lat[1]], axis=1)

    row = lambda w: pl.BlockSpec((tm, w), lambda i: (i, 0))
    vec = lambda w: pl.BlockSpec((1, w), lambda i: (0, 0))
    return pl.pallas_call(
        body, name="norm_bwd", grid=(n_steps,),
        in_specs=[row(LAT_W), row(Q_RANK), row(KV_RANK), vec(Q_RANK), vec(KV_RANK), row(LANE), row(LANE)]
        + [pl.BlockSpec(memory_space=pl.ANY)] * (1 + n_rin),
        out_specs=[row(LAT_W), vec(Q_RANK), vec(KV_RANK)] + [pl.BlockSpec(memory_space=pl.ANY)] * n_rout,
        out_shape=[jax.ShapeDtypeStruct(dproj.shape, dproj.dtype),
                   jax.ShapeDtypeStruct((1, Q_RANK), F32), jax.ShapeDtypeStruct((1, KV_RANK), F32)]
        + list(rider.out_shape),
        scratch_shapes=list(rider.scratch),
        input_output_aliases={7: 0, **{8 + i_in: 3 + i_out for i_in, i_out in rider.aliases.items()}},
        compiler_params=_params(1),
    )(proj, dqn, dkvn, g_q, g_kv, dkr_raw, dfl, dproj, *rider.operands)


def _prenorm_bwd(dh, x, g, dy, *, tm):
    t = x.shape[0]

    def body(dh_ref, x_ref, g_ref, dy_ref, gx_ref, dg_ref):
        i = pl.program_id(0)

        @pl.when(i == 0)
        def _():
            dg_ref[...] = jnp.zeros_like(dg_ref)

        xv = x_ref[...]
        r = lax.rsqrt(jnp.mean(xv * xv, axis=-1, keepdims=True) + NORM_EPS)
        xh = xv * r
        dn = dh_ref[...]
        dg_ref[...] += jnp.sum(dn * xh, axis=0, keepdims=True)
        dxh = dn * g_ref[...]
        gx_ref[...] = dy_ref[...] + r * (dxh - xh * jnp.mean(dxh * xh, axis=-1, keepdims=True))

    row = pl.BlockSpec((tm, D_MODEL), lambda i: (i, 0))
    vec = pl.BlockSpec((1, D_MODEL), lambda i: (0, 0))
    return pl.pallas_call(
        body, name="prenorm_bwd", grid=(t // tm,),
        in_specs=[row, row, vec, row], out_specs=[row, vec],
        out_shape=[jax.ShapeDtypeStruct((t, D_MODEL), F32), jax.ShapeDtypeStruct((1, D_MODEL), F32)],
        compiler_params=_params(1),
    )(dh, x, g, dy)


def _adam_math(w, g, m, v):
    m = ADAM_B1 * m + (1.0 - ADAM_B1) * g
    v = ADAM_B2 * v + (1.0 - ADAM_B2) * (g * g)
    m_hat = m / (1.0 - ADAM_B1 ** ADAM_STEP)
    v_hat = v / (1.0 - ADAM_B2 ** ADAM_STEP)
    delta = -ADAM_LR * (m_hat / (jnp.sqrt(v_hat) + ADAM_EPS) + ADAM_WD * w)
    return delta, m, v


def _adamw(land, w, m, v, *, tr, name):
    rows, cols = w.shape

    def body(l_ref, w_ref, m_ref, v_ref, g_ref, d_ref, nm_ref, nv_ref):
        g = l_ref[0].astype(F32)
        for s in range(1, N_CHIPS):
            g = g + l_ref[s].astype(F32)
        g_ref[...] = g
        d_ref[...], nm_ref[...], nv_ref[...] = _adam_math(w_ref[...], g, m_ref[...], v_ref[...])

    blk = pl.BlockSpec((tr, cols), lambda i: (i, 0))
    return pl.pallas_call(
        body, name=name, grid=(rows // tr,),
        in_specs=[pl.BlockSpec((N_CHIPS, tr, cols), lambda i: (0, i, 0)), blk, blk, blk],
        out_specs=[blk, blk, blk, blk],
        out_shape=[jax.ShapeDtypeStruct((rows, cols), F32)] * 4,
        compiler_params=_params(1),
    )(land, w, m, v)


def _adamw_small(gathered, w, m, v):
    def body(a_ref, w_ref, m_ref, v_ref, g_ref, d_ref, nm_ref, nv_ref):
        g = a_ref[0:SMALL_ROWS, :]
        for s in range(1, N_DEV):
            g = g + a_ref[SMALL_ROWS * s:SMALL_ROWS * (s + 1), :]
        g_ref[...] = g
        d_ref[...], nm_ref[...], nv_ref[...] = _adam_math(w_ref[...], g, m_ref[...], v_ref[...])

    return pl.pallas_call(
        body, name="adamw_small",
        out_shape=[jax.ShapeDtypeStruct((SMALL_ROWS, SMALL_COLS), F32)] * 4,
        compiler_params=_params(),
    )(gathered, w, m, v)


def _place():
    x, y, c = lax.axis_index("x"), lax.axis_index("y"), lax.axis_index("c")
    return x, y, c


def _flip(p, k):
    x, y, c = p
    return (1 - x if k & 4 else x, 1 - y if k & 2 else y, 1 - c if k & 1 else c)


def _index(p):
    return 4 * p[0] + 2 * p[1] + p[2]


class _AllGather:
    def __init__(self, shard):
        assert shard.shape[0] % 32 == 0
        self.half = shard.shape[0] // 2
        self.operands = [shard]
        self.out_shape = [jax.ShapeDtypeStruct((N_DEV,) + shard.shape, shard.dtype)]
        self.aliases = {}
        self.scratch = [pltpu.SemaphoreType.DMA((9,)), pltpu.SemaphoreType.DMA((9,)), pltpu.SemaphoreType.DMA(())]

    def _parts(self, ins, outs, scratch):
        (in_ref,), (out_ref,), (send_sems, recv_sems, local_sem) = ins, outs, scratch
        me = _place()

        def copy(k, block, to, part=None, src=None):
            dst = out_ref.at[_index(block)] if part is None else out_ref.at[_index(block), part]
            return pltpu.make_async_remote_copy(
                src_ref=dst if src is None else src, dst_ref=dst, send_sem=send_sems.at[k], recv_sem=recv_sems.at[k],
                device_id=to, device_id_type=MESH)

        mine = pltpu.make_async_copy(in_ref, out_ref.at[_index(me)], local_sem)
        first = [copy(0, me, _flip(me, 1), src=in_ref), copy(1, me, _flip(me, 4), src=in_ref),
                 copy(2, me, _flip(me, 2), src=in_ref)]
        return me, copy, mine, first

    def start(self, ins, outs, scratch):
        _, _, mine, first = self._parts(ins, outs, scratch)
        mine.start()
        for cp in first:
            cp.start()

    def wait(self, ins, outs, scratch):
        me, copy, mine, sent = self._parts(ins, outs, scratch)
        sibling, x_nbr, y_nbr, diagonal = _flip(me, 1), _flip(me, 4), _flip(me, 2), _flip(me, 6)
        top, bottom = pl.ds(0, self.half), pl.ds(self.half, self.half)
        arrivals = [(1, x_nbr, None, [(3, sibling, None), (5, y_nbr, top)]),
                    (2, y_nbr, None, [(4, sibling, None), (6, x_nbr, bottom)]),
                    (5, diagonal, top, [(7, sibling, top)]),
                    (6, diagonal, bottom, [(8, sibling, bottom)])]
        for k, block, part, onward in arrivals:
            copy(k, block, me, part).wait_recv()
            for k_on, to, part_on in onward:
                cp = copy(k_on, block, to, part_on)
                cp.start()
                sent.append(cp)
        other = lambda p: _flip(p, 1)
        for k, block, part in ((0, sibling, None), (3, other(x_nbr), None), (4, other(y_nbr), None),
                               (7, other(diagonal), top), (8, other(diagonal), bottom)):
            copy(k, block, me, part).wait_recv()
        for cp in sent:
            cp.wait_send()
        mine.wait()


class _Exchange:
    def __init__(self, tasks):
        self.tasks = tasks
        taken = [land for _, _, land, _, _ in tasks if land is not None]
        self.operands = [src for src, _, _, _, _ in tasks] + taken
        self.out_shape = [
            jax.ShapeDtypeStruct((N_CHIPS,) + ((2,) if by_core else ()) + (src.shape if same else src.shape[1:]), src.dtype)
            for src, _, _, same, by_core in tasks]
        self.aliases, n_taken = {}, 0
        for a, (_, _, land, _, _) in enumerate(tasks):
            if land is not None:
                self.aliases[len(tasks) + n_taken] = a
                n_taken += 1
        self.scratch = [pltpu.SemaphoreType.DMA((N_CHIPS,)), pltpu.SemaphoreType.DMA((N_CHIPS,)),
                        pltpu.SemaphoreType.DMA(())] * len(tasks)

    def _copies(self, ins, outs, scratch):
        x, y, core = _place()
        my = 2 * x + y
        for a, (_, chips, _, same, by_core) in enumerate(self.tasks):
            send_sems, recv_sems, local_sem = scratch[3 * a:3 * a + 3]
            slot = (lambda s, a=a, by_core=by_core: outs[a].at[s, core] if by_core else outs[a].at[s])
            for i, j in enumerate(chips):
                src = ins[a] if same else ins[a].at[i]
                pair = jnp.bitwise_xor(my, j)
                remote = pltpu.make_async_remote_copy(
                    src_ref=src, dst_ref=slot(my), send_sem=send_sems.at[pair], recv_sem=recv_sems.at[pair],
                    device_id=(j >> 1, j & 1, core), device_id_type=MESH)
                local = pltpu.make_async_copy(src, slot(my), local_sem)
                yield j, my, core, remote, local, slot, (send_sems, recv_sems)

    def start(self, ins, outs, scratch):
        for j, my, _, remote, local, _, _ in self._copies(ins, outs, scratch):
            pl.when(my != j)(remote.start)
            pl.when(my == j)(local.start)

    def wait(self, ins, outs, scratch):
        for j, my, core, remote, local, slot, (send_sems, recv_sems) in self._copies(ins, outs, scratch):
            pl.when(my != j)(remote.wait_send)

            @pl.when(my == j)
            def _():
                local.wait()
                for s in range(N_CHIPS):
                    if s != j:
                        pltpu.make_async_remote_copy(
                            src_ref=slot(s), dst_ref=slot(s), send_sem=send_sems.at[j ^ s], recv_sem=recv_sems.at[j ^ s],
                            device_id=(s >> 1, s & 1, core), device_id_type=MESH).wait_recv()


N_CHIPS = 4
ALL_CHIPS = tuple(range(N_CHIPS))


class _ToOtherCore:
    def __init__(self, parts):
        self.operands = list(parts)
        self.out_shape = [jax.ShapeDtypeStruct(p.shape[1:], p.dtype) for p in parts]
        self.aliases = {}
        self.scratch = [pltpu.SemaphoreType.DMA((len(parts),)), pltpu.SemaphoreType.DMA((len(parts),))]

    def _copies(self, ins, outs, scratch):
        send_sems, recv_sems = scratch
        me = _place()
        return [pltpu.make_async_remote_copy(src_ref=ins[a].at[1 - me[2]], dst_ref=outs[a], send_sem=send_sems.at[a],
                                             recv_sem=recv_sems.at[a], device_id=_flip(me, 1), device_id_type=MESH)
                for a in range(len(ins))]

    def start(self, ins, outs, scratch):
        for cp in self._copies(ins, outs, scratch):
            cp.start()

    def wait(self, ins, outs, scratch):
        for cp in self._copies(ins, outs, scratch):
            cp.wait()


def _to_other_core(parts, *, name):
    swap = _ToOtherCore(parts)
    n_arr = len(parts)
    hbm = pl.BlockSpec(memory_space=pl.ANY)

    def body(*refs):
        rider_refs = (refs[:n_arr], refs[n_arr:2 * n_arr], refs[2 * n_arr:])
        swap.start(*rider_refs)
        swap.wait(*rider_refs)

    return pl.pallas_call(
        body, name=name, in_specs=[hbm] * n_arr, out_specs=[hbm] * n_arr,
        out_shape=swap.out_shape, scratch_shapes=swap.scratch,
    )(*parts)


class _ShareWithOtherCore:
    def __init__(self, gathered):
        n_arr = len(gathered)
        self.operands = list(gathered)
        self.out_shape = [jax.ShapeDtypeStruct(g.shape, g.dtype) for g in gathered]
        self.aliases = {a: a for a in range(n_arr)}
        self.scratch = [pltpu.SemaphoreType.DMA((N_CHIPS * n_arr,)), pltpu.SemaphoreType.DMA((N_CHIPS * n_arr,))]

    def _copies(self, outs, scratch):
        send_sems, recv_sems = scratch
        me = _place()
        copies = []
        for a, buf in enumerate(outs):
            for j in range(N_CHIPS):
                block = buf.at[j, me[2]]
                copies.append(pltpu.make_async_remote_copy(
                    src_ref=block, dst_ref=block, send_sem=send_sems.at[N_CHIPS * a + j],
                    recv_sem=recv_sems.at[N_CHIPS * a + j], device_id=_flip(me, 1), device_id_type=MESH))
        return copies

    def start(self, ins, outs, scratch):
        for cp in self._copies(outs, scratch):
            cp.start()

    def wait(self, ins, outs, scratch):
        for cp in self._copies(outs, scratch):
            cp.wait()


def _pair_sum(mine, other, core, *, tr, name):
    _, n, rows, cols = mine.shape
    tr = min(tr, rows)

    def body(core_ref, a_ref, b_ref, o_ref):
        o_ref[...] = (a_ref[0].astype(F32) + b_ref[...].astype(F32)).astype(BF16)

    return pl.pallas_call(
        body, name=name,
        grid_spec=pltpu.PrefetchScalarGridSpec(
            num_scalar_prefetch=1, grid=(n, rows // tr),
            in_specs=[pl.BlockSpec((1, 1, tr, cols), lambda j, i, core_ref: (core_ref[0], j, i, 0)),
                      pl.BlockSpec((1, tr, cols), lambda j, i, core_ref: (j, i, 0))],
            out_specs=pl.BlockSpec((1, tr, cols), lambda j, i, core_ref: (j, i, 0))),
        out_shape=jax.ShapeDtypeStruct(other.shape, BF16),
        compiler_params=_params(2),
    )(core, mine, other)


def _gather_small(vec):
    def body(v_ref, out_ref, send_sems, recv_sems, local_sem):
        me = _place()

        def rows(p):
            return out_ref.at[pl.ds(pl.multiple_of(_index(p) * SMALL_ROWS, SMALL_ROWS), SMALL_ROWS), :]

        mine = pltpu.make_async_copy(v_ref, rows(me), local_sem)
        mine.start()
        sends = []
        for k in range(1, N_DEV):
            peer = _flip(me, k)
            cp = pltpu.make_async_remote_copy(src_ref=v_ref, dst_ref=rows(me), send_sem=send_sems.at[k - 1],
                                              recv_sem=recv_sems.at[k - 1], device_id=peer, device_id_type=MESH)
            cp.start()
            sends.append(cp)
        for k in range(1, N_DEV):
            peer = _flip(me, k)
            pltpu.make_async_remote_copy(src_ref=rows(peer), dst_ref=rows(peer), send_sem=send_sems.at[k - 1],
                                         recv_sem=recv_sems.at[k - 1], device_id=peer, device_id_type=MESH).wait_recv()
        for cp in sends:
            cp.wait_send()
        mine.wait()

    return pl.pallas_call(
        body, name="gather_small",
        in_specs=[pl.BlockSpec(memory_space=pltpu.VMEM)], out_specs=pl.BlockSpec(memory_space=pltpu.VMEM),
        out_shape=jax.ShapeDtypeStruct((N_DEV * SMALL_ROWS, SMALL_COLS), F32),
        scratch_shapes=[pltpu.SemaphoreType.DMA((7,)), pltpu.SemaphoreType.DMA((7,)), pltpu.SemaphoreType.DMA],
    )(vec)


def _w_in_nice(gathered):
    pieces, pos = [], 0
    for o0, width, n0 in sorted(_SEGMENTS, key=lambda seg: seg[2]):
        if n0 > pos:
            pieces.append(jnp.zeros((D_MODEL, n0 - pos), gathered.dtype))
        o = o0
        while o < o0 + width:
            d = o // SHARD_IN
            hi = min(o0 + width, (d + 1) * SHARD_IN)
            pieces.append(gathered[d][:, o - d * SHARD_IN:hi - d * SHARD_IN])
            o = hi
        pos = n0 + width
    pieces.append(jnp.zeros((D_MODEL, NP_IN - pos), gathered.dtype))
    return jnp.concatenate(pieces, axis=1)


def _w_in_blocks(chips, dw_lat, dw_rest):
    blocks = []
    for core in range(2):
        for chip in chips:
            lo = (2 * chip + core) * SHARD_IN
            runs = []
            for o0, width, n0 in _SEGMENTS:
                a, b = max(lo, o0), min(lo + SHARD_IN, o0 + width)
                if a < b:
                    n_a, n_b = n0 + a - o0, n0 + b - o0
                    runs.append(dw_lat[:, n_a:n_b] if n_b <= LAT_W else dw_rest[:, n_a - LAT_W:n_b - LAT_W])
            blocks.append(jnp.concatenate(runs, axis=1))
    return jnp.stack(blocks).reshape(2, len(chips), D_MODEL, SHARD_IN)


def _by_core(shards):
    return shards.reshape((N_CHIPS, 2) + shards.shape[1:]).swapaxes(0, 1)


EARLY_CHIPS = (1, 2)
LATE_CHIPS = (0, 3)


def _w_uq_nice(shard):
    z = jnp.zeros((Q_RANK, 32), shard.dtype)
    return jnp.concatenate([shard[:, :128], shard[:, 128:160], z, shard[:, 160:192], z], axis=1)


def _pack_small(g_pre, g_post, g_q, g_kv, b_f, extra=None):
    parts = [g_pre.reshape(-1), g_post.reshape(-1), g_q.reshape(-1), g_kv.reshape(-1), b_f.reshape(-1)]
    if extra is not None:
        parts.append(extra.reshape(-1))
    flat = jnp.concatenate(parts)
    flat = jnp.concatenate([flat, jnp.zeros((SMALL_ROWS * SMALL_COLS - flat.shape[0],), F32)])
    return flat.reshape(SMALL_ROWS, SMALL_COLS)


def _unpack_small(packed):
    flat = packed.reshape(-1)
    o = 0
    out = []
    for n in (D_MODEL, D_MODEL, Q_RANK, KV_RANK, N_HEADS):
        out.append(flat[o:o + n].reshape(1, n))
        o += n
    return out, flat[o]


def kernel(x, positions, g_pre, w_in, g_q_latent, w_uq, g_kv_latent, w_ukv, b_forget, w_out, g_post, loss_target, m_g_pre, m_w_in, m_g_q_latent, m_w_uq, m_g_kv_latent, m_w_ukv, m_b_forget, m_w_out, m_g_post, v_g_pre, v_w_in, v_g_q_latent, v_w_uq, v_g_kv_latent, v_w_ukv, v_b_forget, v_w_out, v_g_post):
    t = x.shape[1]
    tb = min(512, t)
    tm = min(256, t)
    nb = t // tb
    x2 = x.reshape(t, D_MODEL)
    target = loss_target.reshape(t, D_MODEL)
    pos_col = positions.reshape(t, 1).astype(F32)
    bf_row = jnp.concatenate([b_forget.reshape(1, N_HEADS), jnp.zeros((1, LANE - N_HEADS), F32)], axis=1)

    h, h_t, g_in = _prenorm(x2, g_pre, _AllGather(w_in[0].astype(BF16)), tm=tm)
    w_in_n = _w_in_nice(g_in)
    gather_rest = _Exchange([(w, ALL_CHIPS, None, True, True) for w in
                             (_w_uq_nice(w_uq[0].astype(BF16)), w_ukv[0].astype(BF16), w_out[0].astype(BF16))])
    core = lax.axis_index("c").astype(jnp.int32).reshape(1)
    proj, g_uq, g_ukv, g_out, flog = _mm(h, w_in_n, name="proj_in", out_dtype=BF16, tm=2048, tn=512, tk=2048,
                                         rider=gather_rest, f32_cols=(FL0, LANE))
    qn, kvn, kr, cos_t, sin_t, c, qn_t, kvn_t, g_uq, g_ukv, g_out = _mid_fwd(
        proj, flog, g_q_latent, g_kv_latent, bf_row, pos_col, _ShareWithOtherCore([g_uq, g_ukv, g_out]), tm=tm)
    w_uq_n = g_uq.reshape(N_DEV, Q_RANK, 256).transpose(1, 0, 2).reshape(Q_RANK, N_HEADS * 256)
    w_ukv_n = g_ukv.reshape(N_DEV, KV_RANK, 256).transpose(1, 0, 2).reshape(KV_RANK, N_HEADS * 256)
    w_out_n = g_out.reshape(D_MODEL, D_MODEL)
    q_full = _q_up_rope(qn, w_uq_n, cos_t, sin_t, tm=min(1024, t))
    kv = _mm(kvn, w_ukv_n, name="kv_up", out_dtype=BF16, tm=1024, tn=512, tk=KV_RANK)
    c_heads = c[:, :N_HEADS].T
    c_col = c_heads.reshape(N_HEADS, t, 1)
    c_row4 = c_heads.reshape(N_HEADS, nb, 1, tb)
    o_all, og_all, og_t, lse4_mla = _attn_fwd(False, (q_full, kv, kr, proj), t=t, tb=tb, name="mla_fwd")
    o_all, og_all, og_t, lse4_fox = _attn_fwd(True, (proj, c_col, c_row4, o_all, og_all, og_t), t=t, tb=tb,
                                              name="fox_fwd")
    dy, d_o_post, dg_post, loss_part = _out_norm_loss(og_all, w_out_n, x2, target, g_post, tm=min(512, t))

    dw_out = _mm(og_t, d_o_post, name="dw_out", out_dtype=BF16, tm=1024, tn=1024, tk=1024)
    p_out = _by_core(dw_out.reshape(N_DEV, D_MODEL // N_DEV, D_MODEL))
    d_attn, dproj, delta, o_out = _dog_gate(d_o_post, w_out_n, o_all, proj, _ToOtherCore([p_out]), tm=min(1024, t))
    s_out = _pair_sum(p_out, o_out, core, tr=256, name="dw_out_pair_sum")
    delta4 = delta[:, :2 * N_HEADS].T.reshape(2 * N_HEADS, nb, 1, tb)
    dproj, dck, dcq, l_out = _attn_bwd(True, (proj, d_attn, lse4_fox, delta4[N_HEADS:], c_row4, c_col, dproj),
                                       t=t, tb=tb, name="fox_bwd",
                                       rider=_Exchange([(s_out, ALL_CHIPS, None, False, False)]))
    dw_in_rest = _mm(h_t, dproj, name="dw_in_rest", out_dtype=BF16, tm=2048, tn=512, tk=1024,
                     b_cols=(LAT_W, NP_IN - LAT_W))
    p_in = _w_in_blocks(EARLY_CHIPS, None, dw_in_rest)
    (o_in,) = _to_other_core([p_in], name="dw_in_early_to_core")
    s_in = _pair_sum(p_in, o_in, core, tr=256, name="dw_in_early_pair_sum")
    dq_full, dkv, dkr, l_in = _attn_bwd(False, (q_full, kv, kr, d_attn, lse4_mla, delta4[:N_HEADS]),
                                        t=t, tb=tb, name="mla_bwd",
                                        rider=_Exchange([(s_in, EARLY_CHIPS, None, False, False)]))
    dcq_rows = jnp.concatenate([dcq.reshape(N_HEADS, t).T, jnp.zeros((t, LANE - N_HEADS), F32)], axis=1)
    dq2, dkr_raw, dfl, dbf = _mid_bwd(dq_full, dkr, cos_t, sin_t, dcq_rows, dck, flog, bf_row, tm=tm)
    dqn = _mm(dq2, w_uq_n, name="d_qn", nt=True, out_dtype=F32, tm=1024, tn=Q_RANK, tk=2048)
    dkvn = _mm(dkv, w_ukv_n, name="d_kvn", nt=True, out_dtype=F32, tm=1024, tn=KV_RANK, tk=2048)
    dw_uq = _mm(qn_t, dq2, name="dw_uq", out_dtype=BF16, tm=Q_RANK, tn=1024, tk=1024)
    dw_ukv = _mm(kvn_t, dkv, name="dw_ukv", out_dtype=BF16, tm=KV_RANK, tn=1024, tk=1024)
    dw_uq_h = dw_uq.reshape(Q_RANK, N_HEADS, 256)
    s_uq = jnp.concatenate([dw_uq_h[:, :, :160], dw_uq_h[:, :, 192:224]], axis=2).transpose(1, 0, 2)
    s_ukv = dw_ukv.reshape(KV_RANK, N_HEADS, 256).transpose(1, 0, 2)
    up_parts = [_by_core(s_uq), _by_core(s_ukv)]
    dproj, dg_q, dg_kv, *up_other = _norm_bwd(proj, dqn, dkvn, g_q_latent, g_kv_latent, dkr_raw, dfl, dproj,
                                              _ToOtherCore(up_parts), tm=tm)
    up_sums = [_pair_sum(p, o_, core, tr=256, name=f"dw_up_pair_sum_{i}") for i, (p, o_) in enumerate(zip(up_parts, up_other))]
    dw_in_lat, l_uq, l_ukv = _mm(h_t, dproj, name="dw_in_lat", out_dtype=BF16, tm=1024, tn=LAT_W, tk=1024,
                                 b_cols=(0, LAT_W), rider=_Exchange([(s, ALL_CHIPS, None, False, False) for s in up_sums]))
    p_late = _w_in_blocks(LATE_CHIPS, dw_in_lat, dw_in_rest)
    (o_late,) = _to_other_core([p_late], name="dw_in_late_to_core")
    s_late = _pair_sum(p_late, o_late, core, tr=256, name="dw_in_late_pair_sum")
    dh, l_in = _mm(dproj, w_in_n, name="d_h", nt=True, out_dtype=F32, tm=2048, tn=1024, tk=NP_IN // 4,
                   rider=_Exchange([(s_late, LATE_CHIPS, l_in, False, False)]))
    grad_x, dg_pre = _prenorm_bwd(dh, x2, g_pre, dy, tm=tm)

    small = _gather_small(_pack_small(dg_pre, dg_post, dg_q, dg_kv, dbf[:, :N_HEADS], loss_part))

    res_in = _adamw(l_in, w_in[0], m_w_in[0], v_w_in[0], tr=256, name="adamw_w_in")
    res_uq = _adamw(l_uq, w_uq[0], m_w_uq[0], v_w_uq[0], tr=256, name="adamw_w_uq")
    res_ukv = _adamw(l_ukv, w_ukv[0], m_w_ukv[0], v_w_ukv[0], tr=256, name="adamw_w_ukv")
    res_out = _adamw(l_out, w_out[0], m_w_out[0], v_w_out[0], tr=128, name="adamw_w_out")
    res_small = _adamw_small(
        small,
        _pack_small(g_pre, g_post, g_q_latent, g_kv_latent, b_forget),
        _pack_small(m_g_pre, m_g_post, m_g_q_latent, m_g_kv_latent, m_b_forget),
        _pack_small(v_g_pre, v_g_post, v_g_q_latent, v_g_kv_latent, v_b_forget))
    small_out = [_unpack_small(r) for r in res_small]
    loss = small_out[0][1]

    def leaves(kind):
        (s_pre, s_post, s_q, s_kv, s_bf), _ = small_out[kind]
        return [s_pre, res_in[kind][None], s_q, res_uq[kind][None], s_kv, res_ukv[kind][None], s_bf,
                res_out[kind][None], s_post]

    return (loss, grad_x.reshape(x.shape), *leaves(0), *leaves(1), *leaves(2), *leaves(3))
```

```python
import numpy as np
import jax
import jax.numpy as jnp
from jax import lax
from jax.experimental import pallas as pl
from jax.experimental.pallas import tpu as pltpu

F32 = jnp.float32
BF16 = jnp.bfloat16
MESH = pl.DeviceIdType.MESH

N_DEV = 8
D_MODEL = 2048
N_HEADS = 8
HEAD = 128
Q_RANK = 768
KV_RANK = 512
ROPE = 64
D_IN = 6472
SHARD_IN = D_IN // N_DEV
NORM_EPS = 1e-6
ROPE_THETA = 10000.0
MLA_SCALE = (HEAD + ROPE) ** -0.5
FOX_SCALE = HEAD ** -0.5

QL0, KR0, FL0, KVL0, GM0, GF0, FQ0, FK0, FV0, NP_IN = 0, 768, 896, 1024, 1536, 2560, 3584, 4608, 5632, 6656
LAT_W = GM0
LANE = 128
_SEGMENTS = ((0, 768, QL0), (768, 512, KVL0), (1280, 32, KR0), (1312, 32, KR0 + 64), (1344, 1024, GM0),
             (2368, 3072, FQ0), (5440, 8, FL0), (5448, 1024, GF0))
LOG2E = 1.4426950408889634

ADAM_LR = 0.001
ADAM_B1 = 0.9
ADAM_B2 = 0.999
ADAM_EPS = 1e-08
ADAM_WD = 0.01
ADAM_STEP = 10

VMEM_LIMIT_BYTES = 56 * 1024 * 1024
SMALL_ROWS, SMALL_COLS = 8, 768


def _params(n_grid=0):
    return pltpu.CompilerParams(vmem_limit_bytes=VMEM_LIMIT_BYTES,
                                dimension_semantics=("arbitrary",) * n_grid if n_grid else None)


def _sigmoid(z):
    return 1.0 / (1.0 + jnp.exp(-z))


def _split3(v):
    a = v.astype(BF16)
    r = v - a.astype(F32)
    b = r.astype(BF16)
    c = (r - b.astype(F32)).astype(BF16)
    return a, b, c


def _mm(a, b, *, name, nt=False, out_dtype=F32, tm=1024, tn=512, tk=2048, b_cols=None, rider=None, f32_cols=None):
    m, k_dim = a.shape
    n = b.shape[0] if nt else b.shape[1]
    col0 = 0
    if b_cols is not None:
        assert not nt
        col0, n = b_cols
    assert (b.shape[1] if nt else b.shape[0]) == k_dim
    tm, tn, tk = min(tm, m), min(tn, n), min(tk, k_dim)
    assert m % tm == 0 and n % tn == 0 and k_dim % tk == 0 and col0 % tn == 0, (name, a.shape, b.shape)
    nk = k_dim // tk
    j0 = col0 // tn
    grid = (m // tm, n // tn, nk)
    dims = (((1,), (1 if nt else 0,)), ((), ()))
    n_rin = len(rider.operands) if rider else 0
    n_rout = len(rider.out_shape) if rider else 0
    n_x = 0 if f32_cols is None else 1
    if n_x:
        x_tile, x_lo = divmod(f32_cols[0] - col0, tn)
        x_w = f32_cols[1]
        assert x_lo + x_w <= tn and x_lo % LANE == 0 and x_w % LANE == 0

    def body(*refs):
        a_ref, b_ref = refs[:2]
        o_ref = refs[2 + n_rin]
        acc_ref = refs[3 + n_rin + n_rout + n_x]
        i, j, k = pl.program_id(0), pl.program_id(1), pl.program_id(2)
        if rider:
            rider_refs = (refs[2:2 + n_rin], refs[3 + n_rin:3 + n_rin + n_rout], refs[4 + n_rin + n_rout + n_x:])

            @pl.when(jnp.logical_and(i == 0, jnp.logical_and(j == 0, k == 0)))
            def _():
                rider.start(*rider_refs)

        @pl.when(k == 0)
        def _():
            acc_ref[...] = jnp.zeros_like(acc_ref)

        acc_ref[...] += lax.dot_general(a_ref[...], b_ref[...], dims, preferred_element_type=F32)

        @pl.when(k == nk - 1)
        def _():
            o_ref[...] = acc_ref[...].astype(o_ref.dtype)

        if n_x:
            @pl.when(jnp.logical_and(k == nk - 1, j == x_tile))
            def _():
                refs[3 + n_rin + n_rout][...] = acc_ref[:, x_lo:x_lo + x_w]

        if rider:
            @pl.when(jnp.logical_and(i == grid[0] - 1, jnp.logical_and(j == grid[1] - 1, k == nk - 1)))
            def _():
                rider.wait(*rider_refs)

    b_spec = (pl.BlockSpec((tn, tk), lambda i, j, k: (j, k)) if nt
              else pl.BlockSpec((tk, tn), lambda i, j, k: (k, j0 + j)))
    a_spec = pl.BlockSpec((tm, tk), lambda i, j, k: (i, k))
    any_spec = pl.BlockSpec(memory_space=pl.ANY)
    out = pl.pallas_call(
        body, name=name, grid=grid,
        in_specs=[a_spec, b_spec] + [any_spec] * n_rin,
        out_specs=[pl.BlockSpec((tm, tn), lambda i, j, k: (i, j))] + [any_spec] * n_rout
        + ([pl.BlockSpec((tm, x_w), lambda i, j, k: (i, 0))] if n_x else []),
        out_shape=[jax.ShapeDtypeStruct((m, n), out_dtype)] + (list(rider.out_shape) if rider else [])
        + ([jax.ShapeDtypeStruct((m, x_w), F32)] if n_x else []),
        scratch_shapes=[pltpu.VMEM((tm, tn), F32)] + (list(rider.scratch) if rider else []),
        input_output_aliases={2 + i_in: 1 + i_out for i_in, i_out in rider.aliases.items()} if rider else {},
        compiler_params=_params(3),
    )(a, b, *(rider.operands if rider else ()))
    return out if rider or n_x else out[0]


def _prenorm(x, g, rider, *, tm):
    t = x.shape[0]
    n_steps = t // tm
    n_rin, n_rout = len(rider.operands), len(rider.out_shape)

    def body(*refs):
        x_ref, g_ref = refs[:2]
        h_ref, ht_ref = refs[2 + n_rin:4 + n_rin]
        rider_refs = (refs[2:2 + n_rin], refs[4 + n_rin:4 + n_rin + n_rout], refs[4 + n_rin + n_rout:])
        i = pl.program_id(0)
        pl.when(i == 0)(lambda: rider.start(*rider_refs))
        xv = x_ref[...]
        r = lax.rsqrt(jnp.mean(xv * xv, axis=-1, keepdims=True) + NORM_EPS)
        h = xv * r * g_ref[...]
        h_ref[...] = h.astype(BF16)
        ht_ref[...] = h.T.astype(BF16)
        pl.when(i == n_steps - 1)(lambda: rider.wait(*rider_refs))

    any_spec = pl.BlockSpec(memory_space=pl.ANY)
    return pl.pallas_call(
        body, name="prenorm", grid=(n_steps,),
        in_specs=[pl.BlockSpec((tm, D_MODEL), lambda i: (i, 0)), pl.BlockSpec((1, D_MODEL), lambda i: (0, 0))]
        + [any_spec] * n_rin,
        out_specs=[pl.BlockSpec((tm, D_MODEL), lambda i: (i, 0)), pl.BlockSpec((D_MODEL, tm), lambda i: (0, i))]
        + [any_spec] * n_rout,
        out_shape=[jax.ShapeDtypeStruct((t, D_MODEL), BF16), jax.ShapeDtypeStruct((D_MODEL, t), BF16)]
        + list(rider.out_shape),
        scratch_shapes=list(rider.scratch),
        compiler_params=_params(1),
    )(x, g, *rider.operands)


def _rope_rows():
    inv = (np.float32(ROPE_THETA) ** (-np.arange(0, ROPE, 2, dtype=np.float32) / np.float32(ROPE))).astype(np.float32)
    invf = np.zeros((1, LANE), np.float32)
    sgn = np.zeros((1, LANE), np.float32)
    invf[0, 0:32] = inv
    invf[0, 64:96] = inv
    sgn[0, 0:32] = -1.0
    sgn[0, 64:96] = 1.0
    return jnp.asarray(invf), jnp.asarray(sgn)


def _rot(v, cos_t, sin_t):
    return v * cos_t + pltpu.roll(v, 64, 1) * sin_t


def _rot_bwd(dv, cos_t, sin_t):
    return dv * cos_t + pltpu.roll(dv * sin_t, 64, 1)


def _mid_fwd(proj, flog, g_q, g_kv, bf_row, pos_col, rider, *, tm):
    t = proj.shape[0]
    n_steps = t // tm
    invf, sgn = _rope_rows()
    n_rin, n_rout = len(rider.operands), len(rider.out_shape)

    def body(*refs):
        p_ref, fl_ref, gq_ref, gkv_ref, bf_ref, pos_ref, invf_ref, sgn_ref = refs[:8]
        qn_ref, kvn_ref, kr_ref, cos_ref, sin_ref, c_ref, qnt_ref, kvnt_ref = refs[8 + n_rin:16 + n_rin]
        carry_ref = refs[16 + n_rin + n_rout]
        rider_refs = (refs[8:8 + n_rin], refs[16 + n_rin:16 + n_rin + n_rout], refs[17 + n_rin + n_rout:])
        i = pl.program_id(0)
        pl.when(i == 0)(lambda: rider.start(*rider_refs))
        pl.when(i == n_steps - 1)(lambda: rider.wait(*rider_refs))

        @pl.when(i == 0)
        def _():
            carry_ref[...] = jnp.zeros_like(carry_ref)

        ql = p_ref[:, QL0:QL0 + Q_RANK].astype(F32)
        r = lax.rsqrt(jnp.mean(ql * ql, axis=-1, keepdims=True) + NORM_EPS)
        qn = ql * r * gq_ref[...]
        qn_ref[...] = qn.astype(BF16)
        qnt_ref[...] = qn.T.astype(BF16)
        kvl = p_ref[:, KVL0:KVL0 + KV_RANK].astype(F32)
        r = lax.rsqrt(jnp.mean(kvl * kvl, axis=-1, keepdims=True) + NORM_EPS)
        kvn = kvl * r * gkv_ref[...]
        kvn_ref[...] = kvn.astype(BF16)
        kvnt_ref[...] = kvn.T.astype(BF16)

        ang = pos_ref[...] * invf_ref[...]
        cos_t = jnp.cos(ang)
        sin_t = jnp.sin(ang) * sgn_ref[...]
        cos_ref[...] = cos_t
        sin_ref[...] = sin_t
        kr_ref[...] = _rot(p_ref[:, KR0:KR0 + LANE].astype(F32), cos_t, sin_t).astype(BF16)

        z = fl_ref[...] + bf_ref[...]
        logf = jnp.minimum(z, 0.0) - jnp.log(1.0 + jnp.exp(-jnp.abs(z)))
        row = lax.broadcasted_iota(jnp.int32, (tm, tm), 0)
        col = lax.broadcasted_iota(jnp.int32, (tm, tm), 1)
        tri = (col <= row).astype(BF16)
        acc = carry_ref[0:1, :]
        for part in _split3(logf):
            acc = acc + jnp.dot(tri, part, preferred_element_type=F32)
        c_ref[...] = acc * (1.0 / FOX_SCALE)
        carry_ref[0:1, :] = carry_ref[0:1, :] + jnp.sum(logf, axis=0, keepdims=True)

    row_spec = lambda w: pl.BlockSpec((tm, w), lambda i: (i, 0))
    vec_spec = lambda w: pl.BlockSpec((1, w), lambda i: (0, 0))
    return pl.pallas_call(
        body, name="mid_fwd", grid=(n_steps,),
        in_specs=[row_spec(LAT_W), row_spec(LANE), vec_spec(Q_RANK), vec_spec(KV_RANK), vec_spec(LANE),
                  pl.BlockSpec((tm, 1), lambda i: (i, 0)), vec_spec(LANE), vec_spec(LANE)]
        + [pl.BlockSpec(memory_space=pl.ANY)] * n_rin,
        out_specs=[row_spec(Q_RANK), row_spec(KV_RANK), row_spec(LANE), row_spec(LANE), row_spec(LANE), row_spec(LANE),
                   pl.BlockSpec((Q_RANK, tm), lambda i: (0, i)), pl.BlockSpec((KV_RANK, tm), lambda i: (0, i))]
        + [pl.BlockSpec(memory_space=pl.ANY)] * n_rout,
        out_shape=[jax.ShapeDtypeStruct((t, Q_RANK), BF16), jax.ShapeDtypeStruct((t, KV_RANK), BF16),
                   jax.ShapeDtypeStruct((t, LANE), BF16), jax.ShapeDtypeStruct((t, LANE), F32),
                   jax.ShapeDtypeStruct((t, LANE), F32), jax.ShapeDtypeStruct((t, LANE), F32),
                   jax.ShapeDtypeStruct((Q_RANK, t), BF16), jax.ShapeDtypeStruct((KV_RANK, t), BF16)]
        + list(rider.out_shape),
        scratch_shapes=[pltpu.VMEM((8, LANE), F32)] + list(rider.scratch),
        input_output_aliases={8 + i_in: 8 + i_out for i_in, i_out in rider.aliases.items()},
        compiler_params=_params(1),
    )(proj, flog, g_q, g_kv, bf_row, pos_col, invf, sgn, *rider.operands)


def _q_up_rope(qn, w_uq_n, cos_t, sin_t, *, tm):
    t = qn.shape[0]
    tn = 2 * 256

    def body(a_ref, b_ref, cos_ref, sin_ref, o_ref):
        q = jnp.dot(a_ref[...], b_ref[...], preferred_element_type=F32)
        c, s = cos_ref[...], sin_ref[...]
        for u in range(tn // 256):
            o_ref[:, 256 * u:256 * u + 128] = q[:, 256 * u:256 * u + 128].astype(BF16)
            o_ref[:, 256 * u + 128:256 * u + 256] = _rot(q[:, 256 * u + 128:256 * u + 256], c, s).astype(BF16)

    return pl.pallas_call(
        body, name="q_up_rope", grid=(t // tm, N_HEADS * 256 // tn),
        in_specs=[pl.BlockSpec((tm, Q_RANK), lambda i, j: (i, 0)), pl.BlockSpec((Q_RANK, tn), lambda i, j: (0, j)),
                  pl.BlockSpec((tm, LANE), lambda i, j: (i, 0)), pl.BlockSpec((tm, LANE), lambda i, j: (i, 0))],
        out_specs=pl.BlockSpec((tm, tn), lambda i, j: (i, j)),
        out_shape=jax.ShapeDtypeStruct((t, N_HEADS * 256), BF16),
        compiler_params=_params(2),
    )(qn, w_uq_n, cos_t, sin_t)


def _attn_fwd(fox, operands, *, t, tb, name):
    nb = t // tb
    scale = FOX_SCALE if fox else MLA_SCALE
    exp2_scale = scale * LOG2E
    pair = 2 * HEAD
    pair0 = N_HEADS // 2 if fox else 0
    q_w = HEAD if fox else 2 * HEAD
    nt_dims = (((1,), (1,)), ((), ()))
    tn_dims = (((0,), (0,)), ((), ()))

    def body(*refs):
        if fox:
            (q_ref, k_ref, v_ref, gate_ref, cq_ref, ck_ref, _, _, _,
             o_ref, og_ref, ogt_ref, lse_ref, m_s, l_s, acc_s) = refs
        else:
            q_ref, kv_ref, kr_ref, gate_ref, o_ref, og_ref, ogt_ref, lse_ref, m_s, l_s, acc_s = refs
        qi = pl.program_id(1)
        m_s[...] = jnp.full_like(m_s, -jnp.inf)
        l_s[...] = jnp.zeros_like(l_s)
        acc_s[...] = jnp.zeros_like(acc_s)

        def chunk(kc, masked):
            off = pl.multiple_of(kc * tb, tb)
            scores = []
            for u in range(2):
                q = q_ref[:, q_w * u:q_w * (u + 1)]
                if fox:
                    kk = k_ref[pl.ds(off, tb), HEAD * u:HEAD * (u + 1)]
                else:
                    kk = jnp.concatenate([kv_ref[pl.ds(off, tb), pair * u:pair * u + HEAD],
                                          kr_ref[pl.ds(off, tb), :]], axis=1)
                s = lax.dot_general(kk, q, nt_dims, preferred_element_type=F32)
                if fox:
                    s = s + cq_ref[u, 0] - ck_ref[u, pl.ds(off, tb), :]
                if masked:
                    row = lax.broadcasted_iota(jnp.int32, (tb, tb), 0)
                    col = lax.broadcasted_iota(jnp.int32, (tb, tb), 1)
                    s = jnp.where(row <= col, s, -jnp.inf)
                scores.append(s)
            for u in range(2):
                s = scores[u]
                m_prev = m_s[u]
                m_new = jnp.maximum(m_prev, jnp.max(s, axis=0, keepdims=True))
                alpha = jnp.exp2((m_prev - m_new) * exp2_scale)
                p = jnp.exp2((s - m_new) * exp2_scale)
                l_s[u] = alpha * l_s[u] + jnp.sum(p, axis=0, keepdims=True)
                if fox:
                    vv = v_ref[pl.ds(off, tb), HEAD * u:HEAD * (u + 1)]
                else:
                    vv = kv_ref[pl.ds(off, tb), pair * u + HEAD:pair * (u + 1)]
                acc_s[u] = alpha * acc_s[u] + lax.dot_general(vv, p.astype(BF16), tn_dims,
                                                              preferred_element_type=F32)
                m_s[u] = m_new

        def loop_body(kc, carry):
            chunk(kc, False)
            return carry

        lax.fori_loop(0, qi, loop_body, 0)
        chunk(qi, True)
        for u in range(2):
            cols = slice(HEAD * u, HEAD * (u + 1))
            o_t = acc_s[u] / l_s[u]
            o = o_t.T
            o_ref[:, cols] = o
            g = gate_ref[:, cols].astype(F32)
            silu = g * _sigmoid(g)
            og_ref[:, cols] = (o * silu).astype(BF16)
            ogt_ref[cols, :] = (o_t * silu.T).astype(BF16)
            lse_ref[u, 0] = m_s[u] * scale + jnp.log(l_s[u])

    any_spec = pl.BlockSpec(memory_space=pl.ANY)
    row_stat = pl.BlockSpec((2, 1, 1, tb), lambda g, i: (g, i, 0, 0))
    if fox:
        proj, c_col, c_row4, o_all, og_all, ogt_all = operands
        ins = [proj, proj, proj, proj, c_row4, c_col, o_all, og_all, ogt_all]
        in_specs = [pl.BlockSpec((tb, pair), lambda g, i: (i, FQ0 // pair + g)),
                    pl.BlockSpec((t, pair), lambda g, i: (0, FK0 // pair + g)),
                    pl.BlockSpec((t, pair), lambda g, i: (0, FV0 // pair + g)),
                    pl.BlockSpec((tb, pair), lambda g, i: (i, GF0 // pair + g)),
                    row_stat, pl.BlockSpec((2, t, 1), lambda g, i: (g, 0, 0)), any_spec, any_spec, any_spec]
        aliases = {6: 0, 7: 1, 8: 2}
    else:
        q_full, kv, kr, proj = operands
        ins = [q_full, kv, kr, proj]
        in_specs = [pl.BlockSpec((tb, 2 * pair), lambda g, i: (i, g)),
                    pl.BlockSpec((t, 2 * pair), lambda g, i: (0, g)),
                    pl.BlockSpec((t, HEAD), lambda g, i: (0, 0)),
                    pl.BlockSpec((tb, pair), lambda g, i: (i, GM0 // pair + g))]
        aliases = {}
    return pl.pallas_call(
        body, name=name, grid=(N_HEADS // 2, nb), in_specs=in_specs,
        out_specs=[pl.BlockSpec((tb, pair), lambda g, i: (i, pair0 + g)),
                   pl.BlockSpec((tb, pair), lambda g, i: (i, pair0 + g)),
                   pl.BlockSpec((pair, tb), lambda g, i: (pair0 + g, i)), row_stat],
        out_shape=[jax.ShapeDtypeStruct((t, 2 * N_HEADS * HEAD), F32), jax.ShapeDtypeStruct((t, 2 * N_HEADS * HEAD), BF16),
                   jax.ShapeDtypeStruct((2 * N_HEADS * HEAD, t), BF16), jax.ShapeDtypeStruct((N_HEADS, nb, 1, tb), F32)],
        scratch_shapes=[pltpu.VMEM((2, 1, tb), F32), pltpu.VMEM((2, 1, tb), F32), pltpu.VMEM((2, HEAD, tb), F32)],
        input_output_aliases=aliases,
        compiler_params=_params(2),
    )(*ins)


def _out_norm_loss(og, w_out_n, x, target, g, *, tm):
    t = og.shape[0]

    def body(og_ref, w_ref, x_ref, t_ref, g_ref, dy_ref, do_ref, dg_ref, loss_ref):
        i = pl.program_id(0)

        @pl.when(i == 0)
        def _():
            dg_ref[...] = jnp.zeros_like(dg_ref)
            loss_ref[...] = jnp.zeros_like(loss_ref)

        ov = jnp.dot(og_ref[...], w_ref[...], preferred_element_type=F32)
        gv = g_ref[...]
        r = lax.rsqrt(jnp.mean(ov * ov, axis=-1, keepdims=True) + NORM_EPS)
        oh = ov * r
        e = x_ref[...] + oh * gv - t_ref[...]
        loss_ref[...] += 0.5 * jnp.sum(jnp.mean(e * e, axis=-1, keepdims=True), axis=0, keepdims=True)
        dy = e * (1.0 / D_MODEL)
        dy_ref[...] = dy
        dyg = dy * gv
        do_ref[...] = (r * (dyg - oh * jnp.mean(dyg * oh, axis=-1, keepdims=True))).astype(BF16)
        dg_ref[...] += jnp.sum(dy * oh, axis=0, keepdims=True)

    row = pl.BlockSpec((tm, D_MODEL), lambda i: (i, 0))
    vec = pl.BlockSpec((1, D_MODEL), lambda i: (0, 0))
    whole_w = pl.BlockSpec((D_MODEL, D_MODEL), lambda i: (0, 0), pipeline_mode=pl.Buffered(1))
    return pl.pallas_call(
        body, name="out_norm_loss", grid=(t // tm,),
        in_specs=[row, whole_w, row, row, vec],
        out_specs=[row, row, vec, pl.BlockSpec((1, 1), lambda i: (0, 0))],
        out_shape=[jax.ShapeDtypeStruct((t, D_MODEL), F32), jax.ShapeDtypeStruct((t, D_MODEL), BF16),
                   jax.ShapeDtypeStruct((1, D_MODEL), F32), jax.ShapeDtypeStruct((1, 1), F32)],
        compiler_params=_params(1),
    )(og, w_out_n, x, target, g)


def _dog_gate(d_o_post, w_out_n, o_all, proj, rider, *, tm):
    t = d_o_post.shape[0]
    n_group = 4
    pair = n_group * HEAD
    gate_blk = GM0 // pair
    assert GM0 % pair == 0 and GF0 == GM0 + N_HEADS * HEAD
    grid = (t // tm, 2 * N_HEADS // n_group)
    n_rin, n_rout = len(rider.operands), len(rider.out_shape)

    def body(*refs):
        do_ref, w_ref, o_ref, p_ref = refs[:4]
        dattn_ref, dproj_ref, delta_ref = refs[4 + n_rin:7 + n_rin]
        rider_refs = (refs[4:4 + n_rin], refs[7 + n_rin:7 + n_rin + n_rout], refs[7 + n_rin + n_rout:])
        i, j = pl.program_id(0), pl.program_id(1)
        pl.when(jnp.logical_and(i == 0, j == 0))(lambda: rider.start(*rider_refs))

        @pl.when(j == 0)
        def _():
            delta_ref[...] = jnp.zeros_like(delta_ref)

        dog = lax.dot_general(do_ref[...], w_ref[...], (((1,), (1,)), ((), ())), preferred_element_type=F32)
        g = p_ref[...].astype(F32)
        ov = o_ref[...]
        sg = _sigmoid(g)
        d_o = dog * (g * sg)
        dattn_ref[...] = d_o.astype(BF16)
        dproj_ref[...] = (dog * ov * (sg * (1.0 + g * (1.0 - sg)))).astype(BF16)
        prod = d_o * ov
        lane = lax.broadcasted_iota(jnp.int32, (tm, LANE), 1)
        delta = delta_ref[...]
        for u in range(n_group):
            part = jnp.sum(prod[:, HEAD * u:HEAD * (u + 1)], axis=-1, keepdims=True)
            delta = jnp.where(lane == n_group * j + u, part, delta)
        delta_ref[...] = delta
        pl.when(jnp.logical_and(i == grid[0] - 1, j == grid[1] - 1))(lambda: rider.wait(*rider_refs))

    any_spec = pl.BlockSpec(memory_space=pl.ANY)
    return pl.pallas_call(
        body, name="dog_gate", grid=grid,
        in_specs=[pl.BlockSpec((tm, D_MODEL), lambda i, j: (i, 0)), pl.BlockSpec((pair, D_MODEL), lambda i, j: (j, 0)),
                  pl.BlockSpec((tm, pair), lambda i, j: (i, j)), pl.BlockSpec((tm, pair), lambda i, j: (i, gate_blk + j))]
        + [any_spec] * n_rin,
        out_specs=[pl.BlockSpec((tm, pair), lambda i, j: (i, j)), pl.BlockSpec((tm, pair), lambda i, j: (i, gate_blk + j)),
                   pl.BlockSpec((tm, LANE), lambda i, j: (i, 0))] + [any_spec] * n_rout,
        out_shape=[jax.ShapeDtypeStruct((t, 2048), BF16), jax.ShapeDtypeStruct((t, NP_IN), BF16),
                   jax.ShapeDtypeStruct((t, LANE), F32)] + list(rider.out_shape),
        scratch_shapes=list(rider.scratch),
        compiler_params=_params(2),
    )(d_o_post, w_out_n, o_all, proj, *rider.operands)


def _attn_bwd(fox, operands, *, t, tb, name, rider=None):
    nb = t // tb
    n_pairs = N_HEADS // 2
    pair = 2 * HEAD
    scale = FOX_SCALE if fox else MLA_SCALE
    q_w = HEAD if fox else 2 * HEAD
    nt_dims = (((1,), (1,)), ((), ()))
    tn_dims = (((0,), (0,)), ((), ()))
    n_rin = len(rider.operands) if rider else 0
    n_rout = len(rider.out_shape) if rider else 0
    n_in, n_out, n_scr = (9, 3, 9) if fox else (6, 3, 2)

    def body(*refs):
        ends = np.cumsum([0, n_in, n_rin, n_out, n_rout, n_scr])
        in_refs, rider_in, out_refs, rider_out, scr_refs = (refs[a:b] for a, b in zip(ends[:-1], ends[1:]))
        rider_refs = (rider_in, rider_out, refs[ends[-1]:])
        if fox:
            q_ref, k_ref, v_ref, do_ref, lse_ref, dl_ref, cq_ref, ck_ref, _ = in_refs
            dproj_ref, dck_ref, dcq_ref = out_refs
            dq_acc, dk_s, dv_s, dc_s, dcq_s, stage_q, stage_k, stage_v, put_sems = scr_refs
        else:
            q_ref, kv_ref, kr_ref, do_ref, lse_ref, dl_ref = in_refs
            dq_acc, dkv_ref, dkr_ref = out_refs
            dk_s, dv_s = scr_refs
        g = pl.program_id(0)
        ki = pl.program_id(1)
        if rider:
            @pl.when(jnp.logical_and(g == 0, ki == 0))
            def _():
                rider.start(*rider_refs)

        @pl.when(ki == 0)
        def _():
            dq_acc[...] = jnp.zeros_like(dq_acc)
            if fox:
                dcq_s[...] = jnp.zeros_like(dcq_s)

        dk_s[...] = jnp.zeros_like(dk_s)
        dv_s[...] = jnp.zeros_like(dv_s)
        if fox:
            dc_s[...] = jnp.zeros_like(dc_s)
            keys = [k_ref[:, HEAD * u:HEAD * (u + 1)] for u in range(2)]
            vals = [v_ref[:, HEAD * u:HEAD * (u + 1)] for u in range(2)]
        else:
            keys = [jnp.concatenate([kv_ref[:, pair * u:pair * u + HEAD], kr_ref[...]], axis=1) for u in range(2)]
            vals = [kv_ref[:, pair * u + HEAD:pair * (u + 1)] for u in range(2)]

        def chunk(qc, masked):
            off = pl.multiple_of(qc * tb, tb)
            for u in range(2):
                kk, vv = keys[u], vals[u]
                qq = q_ref[pl.ds(off, tb), q_w * u:q_w * (u + 1)]
                dd = do_ref[pl.ds(off, tb), HEAD * u:HEAD * (u + 1)]
                s = lax.dot_general(kk, qq, nt_dims, preferred_element_type=F32)
                if fox:
                    s = s + cq_ref[u, qc] - ck_ref[u]
                if masked:
                    row = lax.broadcasted_iota(jnp.int32, (tb, tb), 0)
                    col = lax.broadcasted_iota(jnp.int32, (tb, tb), 1)
                    s = jnp.where(row <= col, s, -jnp.inf)
                p = jnp.exp2(s * (scale * LOG2E) - lse_ref[u, qc] * LOG2E)
                dv_s[u] += jnp.dot(p.astype(BF16), dd, preferred_element_type=F32)
                dp = lax.dot_general(vv, dd, nt_dims, preferred_element_type=F32)
                ds = p * (dp - dl_ref[u, qc])
                if fox:
                    dc_s[u] += jnp.sum(ds, axis=1, keepdims=True)
                    dcq_s[u, qc] += jnp.sum(ds, axis=0, keepdims=True)
                dsb = (ds * scale).astype(BF16)
                dk_s[u] += jnp.dot(dsb, qq, preferred_element_type=F32)
                dq_acc[pl.ds(off, tb), q_w * u:q_w * (u + 1)] += lax.dot_general(dsb, kk, tn_dims,
                                                                                 preferred_element_type=F32)

        chunk(ki, True)

        def loop_body(qc, carry):
            chunk(qc, False)
            return carry

        lax.fori_loop(ki + 1, nb, loop_body, 0)

        def put(stage_ref, rows, seg0, sem):
            col0 = pl.multiple_of(seg0 + g * pair, pair)
            return pltpu.make_async_copy(stage_ref, dproj_ref.at[rows, pl.ds(col0, pair)], sem)

        if fox:
            rows = pl.ds(pl.multiple_of(ki * tb, tb), tb)
            block_puts = [put(stage_k, rows, FK0, put_sems.at[1]), put(stage_v, rows, FV0, put_sems.at[2])]
            pair_put = put(stage_q, pl.ds(0, t), FQ0, put_sems.at[0])

            @pl.when(jnp.logical_or(g > 0, ki > 0))
            def _():
                for cp in block_puts:
                    cp.wait()

            for u in range(2):
                stage_k[:, HEAD * u:HEAD * (u + 1)] = dk_s[u].astype(BF16)
                stage_v[:, HEAD * u:HEAD * (u + 1)] = dv_s[u].astype(BF16)
                dck_ref[u] = -dc_s[u]
            for cp in block_puts:
                cp.start()

            @pl.when(ki == nb - 1)
            def _():
                @pl.when(g > 0)
                def _():
                    pair_put.wait()

                stage_q[...] = dq_acc[...].astype(BF16)
                pair_put.start()
                dcq_ref[...] = dcq_s[...]

            @pl.when(jnp.logical_and(g == n_pairs - 1, ki == nb - 1))
            def _():
                for cp in block_puts + [pair_put]:
                    cp.wait()
        else:
            dkv_ref[...] = jnp.concatenate([dk_s[0, :, :HEAD], dv_s[0], dk_s[1, :, :HEAD], dv_s[1]], axis=1).astype(BF16)
            dkr_ref[...] = jnp.concatenate([dk_s[0, :, HEAD:], dk_s[1, :, HEAD:]], axis=1)

        if rider:
            @pl.when(jnp.logical_and(g == n_pairs - 1, ki == nb - 1))
            def _():
                rider.wait(*rider_refs)

    stat = pl.BlockSpec((2, nb, 1, tb), lambda g, i: (g, 0, 0, 0))
    aliases = {}
    if fox:
        proj, d_o, lse4, delta4, c_row4, c_col, dproj = operands
        ins = [proj, proj, proj, d_o, lse4, delta4, c_row4, c_col, dproj]
        any_spec = pl.BlockSpec(memory_space=pl.ANY)
        in_specs = [pl.BlockSpec((t, pair), lambda g, i: (0, FQ0 // pair + g)),
                    pl.BlockSpec((tb, pair), lambda g, i: (i, FK0 // pair + g)),
                    pl.BlockSpec((tb, pair), lambda g, i: (i, FV0 // pair + g)),
                    pl.BlockSpec((t, pair), lambda g, i: (0, n_pairs + g)),
                    stat, stat, stat, pl.BlockSpec((2, tb, 1), lambda g, i: (g, i, 0)), any_spec]
        aliases = {8: 0}
        out_specs = [any_spec, pl.BlockSpec((2, tb, 1), lambda g, i: (g, i, 0)), stat]
        out_shape = [jax.ShapeDtypeStruct(dproj.shape, dproj.dtype), jax.ShapeDtypeStruct((N_HEADS, t, 1), F32),
                     jax.ShapeDtypeStruct((N_HEADS, nb, 1, tb), F32)]
        scratch = [pltpu.VMEM((t, pair), F32), pltpu.VMEM((2, tb, HEAD), F32), pltpu.VMEM((2, tb, HEAD), F32),
                   pltpu.VMEM((2, tb, 1), F32), pltpu.VMEM((2, nb, 1, tb), F32),
                   pltpu.VMEM((t, pair), BF16), pltpu.VMEM((tb, pair), BF16), pltpu.VMEM((tb, pair), BF16),
                   pltpu.SemaphoreType.DMA((3,))]
    else:
        q_full, kv, kr, d_o, lse4, delta4 = operands
        ins = [q_full, kv, kr, d_o, lse4, delta4]
        in_specs = [pl.BlockSpec((t, 2 * pair), lambda g, i: (0, g)),
                    pl.BlockSpec((tb, 2 * pair), lambda g, i: (i, g)),
                    pl.BlockSpec((tb, HEAD), lambda g, i: (i, 0)),
                    pl.BlockSpec((t, pair), lambda g, i: (0, g)),
                    stat, stat]
        out_specs = [pl.BlockSpec((t, 2 * pair), lambda g, i: (0, g)), pl.BlockSpec((tb, 2 * pair), lambda g, i: (i, g)),
                     pl.BlockSpec((tb, pair), lambda g, i: (i, g))]
        out_shape = [jax.ShapeDtypeStruct((t, 2048), F32), jax.ShapeDtypeStruct((t, 2048), BF16),
                     jax.ShapeDtypeStruct((t, 1024), F32)]
        scratch = [pltpu.VMEM((2, tb, 2 * HEAD), F32), pltpu.VMEM((2, tb, HEAD), F32)]
    assert (len(ins), len(out_specs), len(scratch)) == (n_in, n_out, n_scr)
    if rider:
        any_spec = pl.BlockSpec(memory_space=pl.ANY)
        aliases = {**aliases, **{n_in + i_in: n_out + i_out for i_in, i_out in rider.aliases.items()}}
        ins = ins + list(rider.operands)
        in_specs = in_specs + [any_spec] * n_rin
        out_specs = out_specs + [any_spec] * n_rout
        out_shape = out_shape + list(rider.out_shape)
        scratch = scratch + list(rider.scratch)
    return pl.pallas_call(
        body, name=name, grid=(n_pairs, nb), in_specs=in_specs, out_specs=out_specs, out_shape=out_shape,
        scratch_shapes=scratch, input_output_aliases=aliases, compiler_params=_params(2),
    )(*ins)


def _mid_bwd(dq_full, dkr, cos_t, sin_t, dcq, dck, flog, bf_row, *, tm):
    t = dq_full.shape[0]
    n = t // tm

    def body(dq_ref, dkr_ref, cos_ref, sin_ref, dcq_ref, dck_ref, fl_ref, bf_ref,
             dq2_ref, dkraw_ref, dfl_ref, dbf_ref, carry_ref):
        i = pl.program_id(0)

        @pl.when(i == 0)
        def _():
            carry_ref[...] = jnp.zeros_like(carry_ref)
            dbf_ref[...] = jnp.zeros_like(dbf_ref)

        c, s = cos_ref[...], sin_ref[...]
        dkr_sum = jnp.zeros((tm, LANE), F32)
        for h in range(N_HEADS):
            dq2_ref[:, 256 * h:256 * h + 128] = dq_ref[:, 256 * h:256 * h + 128].astype(BF16)
            dq2_ref[:, 256 * h + 128:256 * h + 256] = _rot_bwd(dq_ref[:, 256 * h + 128:256 * h + 256], c, s).astype(BF16)
            dkr_sum = dkr_sum + dkr_ref[:, HEAD * h:HEAD * (h + 1)]
        dkraw_ref[...] = _rot_bwd(dkr_sum, c, s).astype(BF16)

        dc = dcq_ref[...]
        lane = lax.broadcasted_iota(jnp.int32, (tm, LANE), 1)
        for h in range(N_HEADS):
            dc = dc + jnp.where(lane == h, dck_ref[h], 0.0)
        row = lax.broadcasted_iota(jnp.int32, (tm, tm), 0)
        col = lax.broadcasted_iota(jnp.int32, (tm, tm), 1)
        tri = (col >= row).astype(BF16)
        acc = carry_ref[0:1, :]
        for part in _split3(dc):
            acc = acc + jnp.dot(tri, part, preferred_element_type=F32)
        carry_ref[0:1, :] = carry_ref[0:1, :] + jnp.sum(dc, axis=0, keepdims=True)
        z = fl_ref[...] + bf_ref[...]
        dz = acc / (1.0 + jnp.exp(z))
        dfl_ref[...] = dz.astype(BF16)
        dbf_ref[...] += jnp.sum(dz, axis=0, keepdims=True)

    rev = lambda w: pl.BlockSpec((tm, w), lambda i: (n - 1 - i, 0))
    vec = lambda w: pl.BlockSpec((1, w), lambda i: (0, 0))
    return pl.pallas_call(
        body, name="mid_bwd", grid=(n,),
        in_specs=[rev(2048), rev(1024), rev(LANE), rev(LANE), rev(LANE),
                  pl.BlockSpec((N_HEADS, tm, 1), lambda i: (0, n - 1 - i, 0)), rev(LANE), vec(LANE)],
        out_specs=[rev(2048), rev(LANE), rev(LANE), vec(LANE)],
        out_shape=[jax.ShapeDtypeStruct((t, 2048), BF16), jax.ShapeDtypeStruct((t, LANE), BF16),
                   jax.ShapeDtypeStruct((t, LANE), BF16), jax.ShapeDtypeStruct((1, LANE), F32)],
        scratch_shapes=[pltpu.VMEM((8, LANE), F32)],
        compiler_params=_params(1),
    )(dq_full, dkr, cos_t, sin_t, dcq, dck, flog, bf_row)


def _norm_bwd(proj, dqn, dkvn, g_q, g_kv, dkr_raw, dfl, dproj, rider, *, tm):
    t = proj.shape[0]
    n_steps = t // tm
    assert (KR0, FL0, KVL0, LAT_W) == (Q_RANK, Q_RANK + LANE, Q_RANK + 2 * LANE, Q_RANK + 2 * LANE + KV_RANK)
    n_rin, n_rout = len(rider.operands), len(rider.out_shape)

    def body(*refs):
        p_ref, dqn_ref, dkvn_ref, gq_ref, gkv_ref, dkr_ref, dfl_ref = refs[:7]
        dproj_ref, dgq_ref, dgkv_ref = refs[8 + n_rin:11 + n_rin]
        rider_refs = (refs[8:8 + n_rin], refs[11 + n_rin:11 + n_rin + n_rout], refs[11 + n_rin + n_rout:])
        i = pl.program_id(0)
        pl.when(i == 0)(lambda: rider.start(*rider_refs))
        pl.when(i == n_steps - 1)(lambda: rider.wait(*rider_refs))

        @pl.when(i == 0)
        def _():
            dgq_ref[...] = jnp.zeros_like(dgq_ref)
            dgkv_ref[...] = jnp.zeros_like(dgkv_ref)

        d_lat = []
        for lo, w, dn_ref, g_ref, dg_ref in ((QL0, Q_RANK, dqn_ref, gq_ref, dgq_ref),
                                             (KVL0, KV_RANK, dkvn_ref, gkv_ref, dgkv_ref)):
            xv = p_ref[:, lo:lo + w].astype(F32)
            r = lax.rsqrt(jnp.mean(xv * xv, axis=-1, keepdims=True) + NORM_EPS)
            xh = xv * r
            dn = dn_ref[...]
            dg_ref[...] += jnp.sum(dn * xh, axis=0, keepdims=True)
            dxh = dn * g_ref[...]
            d_lat.append((r * (dxh - xh * jnp.mean(dxh * xh, axis=-1, keepdims=True))).astype(BF16))
        dproj_ref[...] = jnp.concatenate([d_lat[0], dkr_ref[...], dfl_ref[...], d_lat[1]], axis=1)

    row = lambda w: pl.BlockSpec((tm, w), lambda i: (i, 0))
    vec = lambda w: pl.BlockSpec((1, w), lambda i: (0, 0))
    return pl.pallas_call(
        body, name="norm_bwd", grid=(n_steps,),
        in_specs=[row(LAT_W), row(Q_RANK), row(KV_RANK), vec(Q_RANK), vec(KV_RANK), row(LANE), row(LANE)]
        + [pl.BlockSpec(memory_space=pl.ANY)] * (1 + n_rin),
        out_specs=[row(LAT_W), vec(Q_RANK), vec(KV_RANK)] + [pl.BlockSpec(memory_space=pl.ANY)] * n_rout,
        out_shape=[jax.ShapeDtypeStruct(dproj.shape, dproj.dtype),
                   jax.ShapeDtypeStruct((1, Q_RANK), F32), jax.ShapeDtypeStruct((1, KV_RANK), F32)]
        + list(rider.out_shape),
        scratch_shapes=list(rider.scratch),
        input_output_aliases={7: 0, **{8 + i_in: 3 + i_out for i_in, i_out in rider.aliases.items()}},
        compiler_params=_params(1),
    )(proj, dqn, dkvn, g_q, g_kv, dkr_raw, dfl, dproj, *rider.operands)


def _prenorm_bwd(dh, x, g, dy, *, tm):
    t = x.shape[0]

    def body(dh_ref, x_ref, g_ref, dy_ref, gx_ref, dg_ref):
        i = pl.program_id(0)

        @pl.when(i == 0)
        def _():
            dg_ref[...] = jnp.zeros_like(dg_ref)

        xv = x_ref[...]
        r = lax.rsqrt(jnp.mean(xv * xv, axis=-1, keepdims=True) + NORM_EPS)
        xh = xv * r
        dn = dh_ref[...]
        dg_ref[...] += jnp.sum(dn * xh, axis=0, keepdims=True)
        dxh = dn * g_ref[...]
        gx_ref[...] = dy_ref[...] + r * (dxh - xh * jnp.mean(dxh * xh, axis=-1, keepdims=True))

    row = pl.BlockSpec((tm, D_MODEL), lambda i: (i, 0))
    vec = pl.BlockSpec((1, D_MODEL), lambda i: (0, 0))
    return pl.pallas_call(
        body, name="prenorm_bwd", grid=(t // tm,),
        in_specs=[row, row, vec, row], out_specs=[row, vec],
        out_shape=[jax.ShapeDtypeStruct((t, D_MODEL), F32), jax.ShapeDtypeStruct((1, D_MODEL), F32)],
        compiler_params=_params(1),
    )(dh, x, g, dy)


def _adam_math(w, g, m, v):
    m = ADAM_B1 * m + (1.0 - ADAM_B1) * g
    v = ADAM_B2 * v + (1.0 - ADAM_B2) * (g * g)
    m_hat = m / (1.0 - ADAM_B1 ** ADAM_STEP)
    v_hat = v / (1.0 - ADAM_B2 ** ADAM_STEP)
    delta = -ADAM_LR * (m_hat / (jnp.sqrt(v_hat) + ADAM_EPS) + ADAM_WD * w)
    return delta, m, v


def _adamw(land, w, m, v, *, tr, name):
    rows, cols = w.shape

    def body(l_ref, w_ref, m_ref, v_ref, g_ref, d_ref, nm_ref, nv_ref):
        g = l_ref[0].astype(F32)
        for s in range(1, N_CHIPS):
            g = g + l_ref[s].astype(F32)
        g_ref[...] = g
        d_ref[...], nm_ref[...], nv_ref[...] = _adam_math(w_ref[...], g, m_ref[...], v_ref[...])

    blk = pl.BlockSpec((tr, cols), lambda i: (i, 0))
    return pl.pallas_call(
        body, name=name, grid=(rows // tr,),
        in_specs=[pl.BlockSpec((N_CHIPS, tr, cols), lambda i: (0, i, 0)), blk, blk, blk],
        out_specs=[blk, blk, blk, blk],
        out_shape=[jax.ShapeDtypeStruct((rows, cols), F32)] * 4,
        compiler_params=_params(1),
    )(land, w, m, v)


def _adamw_small(gathered, w, m, v):
    def body(a_ref, w_ref, m_ref, v_ref, g_ref, d_ref, nm_ref, nv_ref):
        g = a_ref[0:SMALL_ROWS, :]
        for s in range(1, N_DEV):
            g = g + a_ref[SMALL_ROWS * s:SMALL_ROWS * (s + 1), :]
        g_ref[...] = g
        d_ref[...], nm_ref[...], nv_ref[...] = _adam_math(w_ref[...], g, m_ref[...], v_ref[...])

    return pl.pallas_call(
        body, name="adamw_small",
        out_shape=[jax.ShapeDtypeStruct((SMALL_ROWS, SMALL_COLS), F32)] * 4,
        compiler_params=_params(),
    )(gathered, w, m, v)


def _place():
    x, y, c = lax.axis_index("x"), lax.axis_index("y"), lax.axis_index("c")
    return x, y, c


def _flip(p, k):
    x, y, c = p
    return (1 - x if k & 4 else x, 1 - y if k & 2 else y, 1 - c if k & 1 else c)


def _index(p):
    return 4 * p[0] + 2 * p[1] + p[2]


class _AllGather:
    def __init__(self, shard):
        assert shard.shape[0] % 32 == 0
        self.half = shard.shape[0] // 2
        self.operands = [shard]
        self.out_shape = [jax.ShapeDtypeStruct((N_DEV,) + shard.shape, shard.dtype)]
        self.aliases = {}
        self.scratch = [pltpu.SemaphoreType.DMA((9,)), pltpu.SemaphoreType.DMA((9,)), pltpu.SemaphoreType.DMA(())]

    def _parts(self, ins, outs, scratch):
        (in_ref,), (out_ref,), (send_sems, recv_sems, local_sem) = ins, outs, scratch
        me = _place()

        def copy(k, block, to, part=None, src=None):
            dst = out_ref.at[_index(block)] if part is None else out_ref.at[_index(block), part]
            return pltpu.make_async_remote_copy(
                src_ref=dst if src is None else src, dst_ref=dst, send_sem=send_sems.at[k], recv_sem=recv_sems.at[k],
                device_id=to, device_id_type=MESH)

        mine = pltpu.make_async_copy(in_ref, out_ref.at[_index(me)], local_sem)
        first = [copy(0, me, _flip(me, 1), src=in_ref), copy(1, me, _flip(me, 4), src=in_ref),
                 copy(2, me, _flip(me, 2), src=in_ref)]
        return me, copy, mine, first

    def start(self, ins, outs, scratch):
        _, _, mine, first = self._parts(ins, outs, scratch)
        mine.start()
        for cp in first:
            cp.start()

    def wait(self, ins, outs, scratch):
        me, copy, mine, sent = self._parts(ins, outs, scratch)
        sibling, x_nbr, y_nbr, diagonal = _flip(me, 1), _flip(me, 4), _flip(me, 2), _flip(me, 6)
        top, bottom = pl.ds(0, self.half), pl.ds(self.half, self.half)
        arrivals = [(1, x_nbr, None, [(3, sibling, None), (5, y_nbr, top)]),
                    (2, y_nbr, None, [(4, sibling, None), (6, x_nbr, bottom)]),
                    (5, diagonal, top, [(7, sibling, top)]),
                    (6, diagonal, bottom, [(8, sibling, bottom)])]
        for k, block, part, onward in arrivals:
            copy(k, block, me, part).wait_recv()
            for k_on, to, part_on in onward:
                cp = copy(k_on, block, to, part_on)
                cp.start()
                sent.append(cp)
        other = lambda p: _flip(p, 1)
        for k, block, part in ((0, sibling, None), (3, other(x_nbr), None), (4, other(y_nbr), None),
                               (7, other(diagonal), top), (8, other(diagonal), bottom)):
            copy(k, block, me, part).wait_recv()
        for cp in sent:
            cp.wait_send()
        mine.wait()


class _Exchange:
    def __init__(self, tasks):
        self.tasks = tasks
        taken = [land for _, _, land, _, _ in tasks if land is not None]
        self.operands = [src for src, _, _, _, _ in tasks] + taken
        self.out_shape = [
            jax.ShapeDtypeStruct((N_CHIPS,) + ((2,) if by_core else ()) + (src.shape if same else src.shape[1:]), src.dtype)
            for src, _, _, same, by_core in tasks]
        self.aliases, n_taken = {}, 0
        for a, (_, _, land, _, _) in enumerate(tasks):
            if land is not None:
                self.aliases[len(tasks) + n_taken] = a
                n_taken += 1
        self.scratch = [pltpu.SemaphoreType.DMA((N_CHIPS,)), pltpu.SemaphoreType.DMA((N_CHIPS,)),
                        pltpu.SemaphoreType.DMA(())] * len(tasks)

    def _copies(self, ins, outs, scratch):
        x, y, core = _place()
        my = 2 * x + y
        for a, (_, chips, _, same, by_core) in enumerate(self.tasks):
            send_sems, recv_sems, local_sem = scratch[3 * a:3 * a + 3]
            slot = (lambda s, a=a, by_core=by_core: outs[a].at[s, core] if by_core else outs[a].at[s])
            for i, j in enumerate(chips):
                src = ins[a] if same else ins[a].at[i]
                pair = jnp.bitwise_xor(my, j)
                remote = pltpu.make_async_remote_copy(
                    src_ref=src, dst_ref=slot(my), send_sem=send_sems.at[pair], recv_sem=recv_sems.at[pair],
                    device_id=(j >> 1, j & 1, core), device_id_type=MESH)
                local = pltpu.make_async_copy(src, slot(my), local_sem)
                yield j, my, core, remote, local, slot, (send_sems, recv_sems)

    def start(self, ins, outs, scratch):
        for j, my, _, remote, local, _, _ in self._copies(ins, outs, scratch):
            pl.when(my != j)(remote.start)
            pl.when(my == j)(local.start)

    def wait(self, ins, outs, scratch):
        for j, my, core, remote, local, slot, (send_sems, recv_sems) in self._copies(ins, outs, scratch):
            pl.when(my != j)(remote.wait_send)

            @pl.when(my == j)
            def _():
                local.wait()
                for s in range(N_CHIPS):
                    if s != j:
                        pltpu.make_async_remote_copy(
                            src_ref=slot(s), dst_ref=slot(s), send_sem=send_sems.at[j ^ s], recv_sem=recv_sems.at[j ^ s],
                            device_id=(s >> 1, s & 1, core), device_id_type=MESH).wait_recv()


N_CHIPS = 4
ALL_CHIPS = tuple(range(N_CHIPS))


class _ToOtherCore:
    def __init__(self, parts):
        self.operands = list(parts)
        self.out_shape = [jax.ShapeDtypeStruct(p.shape[1:], p.dtype) for p in parts]
        self.aliases = {}
        self.scratch = [pltpu.SemaphoreType.DMA((len(parts),)), pltpu.SemaphoreType.DMA((len(parts),))]

    def _copies(self, ins, outs, scratch):
        send_sems, recv_sems = scratch
        me = _place()
        return [pltpu.make_async_remote_copy(src_ref=ins[a].at[1 - me[2]], dst_ref=outs[a], send_sem=send_sems.at[a],
                                             recv_sem=recv_sems.at[a], device_id=_flip(me, 1), device_id_type=MESH)
                for a in range(len(ins))]

    def start(self, ins, outs, scratch):
        for cp in self._copies(ins, outs, scratch):
            cp.start()

    def wait(self, ins, outs, scratch):
        for cp in self._copies(ins, outs, scratch):
            cp.wait()


def _to_other_core(parts, *, name):
    swap = _ToOtherCore(parts)
    n_arr = len(parts)
    hbm = pl.BlockSpec(memory_space=pl.ANY)

    def body(*refs):
        rider_refs = (refs[:n_arr], refs[n_arr:2 * n_arr], refs[2 * n_arr:])
        swap.start(*rider_refs)
        swap.wait(*rider_refs)

    return pl.pallas_call(
        body, name=name, in_specs=[hbm] * n_arr, out_specs=[hbm] * n_arr,
        out_shape=swap.out_shape, scratch_shapes=swap.scratch,
    )(*parts)


class _ShareWithOtherCore:
    def __init__(self, gathered):
        n_arr = len(gathered)
        self.operands = list(gathered)
        self.out_shape = [jax.ShapeDtypeStruct(g.shape, g.dtype) for g in gathered]
        self.aliases = {a: a for a in range(n_arr)}
        self.scratch = [pltpu.SemaphoreType.DMA((N_CHIPS * n_arr,)), pltpu.SemaphoreType.DMA((N_CHIPS * n_arr,))]

    def _copies(self, outs, scratch):
        send_sems, recv_sems = scratch
        me = _place()
        copies = []
        for a, buf in enumerate(outs):
            for j in range(N_CHIPS):
                block = buf.at[j, me[2]]
                copies.append(pltpu.make_async_remote_copy(
                    src_ref=block, dst_ref=block, send_sem=send_sems.at[N_CHIPS * a + j],
                    recv_sem=recv_sems.at[N_CHIPS * a + j], device_id=_flip(me, 1), device_id_type=MESH))
        return copies

    def start(self, ins, outs, scratch):
        for cp in self._copies(outs, scratch):
            cp.start()

    def wait(self, ins, outs, scratch):
        for cp in self._copies(outs, scratch):
            cp.wait()


def _pair_sum(mine, other, core, *, tr, name):
    _, n, rows, cols = mine.shape
    tr = min(tr, rows)

    def body(core_ref, a_ref, b_ref, o_ref):
        o_ref[...] = (a_ref[0].astype(F32) + b_ref[...].astype(F32)).astype(BF16)

    return pl.pallas_call(
        body, name=name,
        grid_spec=pltpu.PrefetchScalarGridSpec(
            num_scalar_prefetch=1, grid=(n, rows // tr),
            in_specs=[pl.BlockSpec((1, 1, tr, cols), lambda j, i, core_ref: (core_ref[0], j, i, 0)),
                      pl.BlockSpec((1, tr, cols), lambda j, i, core_ref: (j, i, 0))],
            out_specs=pl.BlockSpec((1, tr, cols), lambda j, i, core_ref: (j, i, 0))),
        out_shape=jax.ShapeDtypeStruct(other.shape, BF16),
        compiler_params=_params(2),
    )(core, mine, other)


def _gather_small(vec):
    def body(v_ref, out_ref, send_sems, recv_sems, local_sem):
        me = _place()

        def rows(p):
            return out_ref.at[pl.ds(pl.multiple_of(_index(p) * SMALL_ROWS, SMALL_ROWS), SMALL_ROWS), :]

        mine = pltpu.make_async_copy(v_ref, rows(me), local_sem)
        mine.start()
        sends = []
        for k in range(1, N_DEV):
            peer = _flip(me, k)
            cp = pltpu.make_async_remote_copy(src_ref=v_ref, dst_ref=rows(me), send_sem=send_sems.at[k - 1],
                                              recv_sem=recv_sems.at[k - 1], device_id=peer, device_id_type=MESH)
            cp.start()
            sends.append(cp)
        for k in range(1, N_DEV):
            peer = _flip(me, k)
            pltpu.make_async_remote_copy(src_ref=rows(peer), dst_ref=rows(peer), send_sem=send_sems.at[k - 1],
                                         recv_sem=recv_sems.at[k - 1], device_id=peer, device_id_type=MESH).wait_recv()
        for cp in sends:
            cp.wait_send()
        mine.wait()

    return pl.pallas_call(
        body, name="gather_small",
        in_specs=[pl.BlockSpec(memory_space=pltpu.VMEM)], out_specs=pl.BlockSpec(memory_space=pltpu.VMEM),
        out_shape=jax.ShapeDtypeStruct((N_DEV * SMALL_ROWS, SMALL_COLS), F32),
        scratch_shapes=[pltpu.SemaphoreType.DMA((7,)), pltpu.SemaphoreType.DMA((7,)), pltpu.SemaphoreType.DMA],
    )(vec)


def _w_in_nice(gathered):
    pieces, pos = [], 0
    for o0, width, n0 in sorted(_SEGMENTS, key=lambda seg: seg[2]):
        if n0 > pos:
            pieces.append(jnp.zeros((D_MODEL, n0 - pos), gathered.dtype))
        o = o0
        while o < o0 + width:
            d = o // SHARD_IN
            hi = min(o0 + width, (d + 1) * SHARD_IN)
            pieces.append(gathered[d][:, o - d * SHARD_IN:hi - d * SHARD_IN])
            o = hi
        pos = n0 + width
    pieces.append(jnp.zeros((D_MODEL, NP_IN - pos), gathered.dtype))
    return jnp.concatenate(pieces, axis=1)


def _w_in_blocks(chips, dw_lat, dw_rest):
    blocks = []
    for core in range(2):
        for chip in chips:
            lo = (2 * chip + core) * SHARD_IN
            runs = []
            for o0, width, n0 in _SEGMENTS:
                a, b = max(lo, o0), min(lo + SHARD_IN, o0 + width)
                if a < b:
                    n_a, n_b = n0 + a - o0, n0 + b - o0
                    runs.append(dw_lat[:, n_a:n_b] if n_b <= LAT_W else dw_rest[:, n_a - LAT_W:n_b - LAT_W])
            blocks.append(jnp.concatenate(runs, axis=1))
    return jnp.stack(blocks).reshape(2, len(chips), D_MODEL, SHARD_IN)


def _by_core(shards):
    return shards.reshape((N_CHIPS, 2) + shards.shape[1:]).swapaxes(0, 1)


EARLY_CHIPS = (1, 2)
LATE_CHIPS = (0, 3)


def _w_uq_nice(shard):
    z = jnp.zeros((Q_RANK, 32), shard.dtype)
    return jnp.concatenate([shard[:, :128], shard[:, 128:160], z, shard[:, 160:192], z], axis=1)


def _pack_small(g_pre, g_post, g_q, g_kv, b_f, extra=None):
    parts = [g_pre.reshape(-1), g_post.reshape(-1), g_q.reshape(-1), g_kv.reshape(-1), b_f.reshape(-1)]
    if extra is not None:
        parts.append(extra.reshape(-1))
    flat = jnp.concatenate(parts)
    flat = jnp.concatenate([flat, jnp.zeros((SMALL_ROWS * SMALL_COLS - flat.shape[0],), F32)])
    return flat.reshape(SMALL_ROWS, SMALL_COLS)


def _unpack_small(packed):
    flat = packed.reshape(-1)
    o = 0
    out = []
    for n in (D_MODEL, D_MODEL, Q_RANK, KV_RANK, N_HEADS):
        out.append(flat[o:o + n].reshape(1, n))
        o += n
    return out, flat[o]


def kernel(x, positions, g_pre, w_in, g_q_latent, w_uq, g_kv_latent, w_ukv, b_forget, w_out, g_post, loss_target, m_g_pre, m_w_in, m_g_q_latent, m_w_uq, m_g_kv_latent, m_w_ukv, m_b_forget, m_w_out, m_g_post, v_g_pre, v_w_in, v_g_q_latent, v_w_uq, v_g_kv_latent, v_w_ukv, v_b_forget, v_w_out, v_g_post):
    t = x.shape[1]
    tb = min(512, t)
    tm = min(256, t)
    nb = t // tb
    x2 = x.reshape(t, D_MODEL)
    target = loss_target.reshape(t, D_MODEL)
    pos_col = positions.reshape(t, 1).astype(F32)
    bf_row = jnp.concatenate([b_forget.reshape(1, N_HEADS), jnp.zeros((1, LANE - N_HEADS), F32)], axis=1)

    h, h_t, g_in = _prenorm(x2, g_pre, _AllGather(w_in[0].astype(BF16)), tm=tm)
    w_in_n = _w_in_nice(g_in)
    gather_rest = _Exchange([(w, ALL_CHIPS, None, True, True) for w in
                             (_w_uq_nice(w_uq[0].astype(BF16)), w_ukv[0].astype(BF16), w_out[0].astype(BF16))])
    core = lax.axis_index("c").astype(jnp.int32).reshape(1)
    proj, g_uq, g_ukv, g_out, flog = _mm(h, w_in_n, name="proj_in", out_dtype=BF16, tm=2048, tn=512, tk=2048,
                                         rider=gather_rest, f32_cols=(FL0, LANE))
    qn, kvn, kr, cos_t, sin_t, c, qn_t, kvn_t, g_uq, g_ukv, g_out = _mid_fwd(
        proj, flog, g_q_latent, g_kv_latent, bf_row, pos_col, _ShareWithOtherCore([g_uq, g_ukv, g_out]), tm=tm)
    w_uq_n = g_uq.reshape(N_DEV, Q_RANK, 256).transpose(1, 0, 2).reshape(Q_RANK, N_HEADS * 256)
    w_ukv_n = g_ukv.reshape(N_DEV, KV_RANK, 256).transpose(1, 0, 2).reshape(KV_RANK, N_HEADS * 256)
    w_out_n = g_out.reshape(D_MODEL, D_MODEL)
    q_full = _q_up_rope(qn, w_uq_n, cos_t, sin_t, tm=min(1024, t))
    kv = _mm(kvn, w_ukv_n, name="kv_up", out_dtype=BF16, tm=1024, tn=512, tk=KV_RANK)
    c_heads = c[:, :N_HEADS].T
    c_col = c_heads.reshape(N_HEADS, t, 1)
    c_row4 = c_heads.reshape(N_HEADS, nb, 1, tb)
    o_all, og_all, og_t, lse4_mla = _attn_fwd(False, (q_full, kv, kr, proj), t=t, tb=tb, name="mla_fwd")
    o_all, og_all, og_t, lse4_fox = _attn_fwd(True, (proj, c_col, c_row4, o_all, og_all, og_t), t=t, tb=tb,
                                              name="fox_fwd")
    dy, d_o_post, dg_post, loss_part = _out_norm_loss(og_all, w_out_n, x2, target, g_post, tm=min(512, t))

    dw_out = _mm(og_t, d_o_post, name="dw_out", out_dtype=BF16, tm=1024, tn=1024, tk=1024)
    p_out = _by_core(dw_out.reshape(N_DEV, D_MODEL // N_DEV, D_MODEL))
    d_attn, dproj, delta, o_out = _dog_gate(d_o_post, w_out_n, o_all, proj, _ToOtherCore([p_out]), tm=min(1024, t))
    s_out = _pair_sum(p_out, o_out, core, tr=256, name="dw_out_pair_sum")
    delta4 = delta[:, :2 * N_HEADS].T.reshape(2 * N_HEADS, nb, 1, tb)
    dproj, dck, dcq, l_out = _attn_bwd(True, (proj, d_attn, lse4_fox, delta4[N_HEADS:], c_row4, c_col, dproj),
                                       t=t, tb=tb, name="fox_bwd",
                                       rider=_Exchange([(s_out, ALL_CHIPS, None, False, False)]))
    dw_in_rest = _mm(h_t, dproj, name="dw_in_rest", out_dtype=BF16, tm=2048, tn=512, tk=2048,
                     b_cols=(LAT_W, NP_IN - LAT_W))
    p_in = _w_in_blocks(EARLY_CHIPS, None, dw_in_rest)
    (o_in,) = _to_other_core([p_in], name="dw_in_early_to_core")
    s_in = _pair_sum(p_in, o_in, core, tr=256, name="dw_in_early_pair_sum")
    dq_full, dkv, dkr, l_in = _attn_bwd(False, (q_full, kv, kr, d_attn, lse4_mla, delta4[:N_HEADS]),
                                        t=t, tb=tb, name="mla_bwd",
                                        rider=_Exchange([(s_in, EARLY_CHIPS, None, False, False)]))
    dcq_rows = jnp.concatenate([dcq.reshape(N_HEADS, t).T, jnp.zeros((t, LANE - N_HEADS), F32)], axis=1)
    dq2, dkr_raw, dfl, dbf = _mid_bwd(dq_full, dkr, cos_t, sin_t, dcq_rows, dck, flog, bf_row, tm=tm)
    dqn = _mm(dq2, w_uq_n, name="d_qn", nt=True, out_dtype=F32, tm=1024, tn=Q_RANK, tk=2048)
    dkvn = _mm(dkv, w_ukv_n, name="d_kvn", nt=True, out_dtype=F32, tm=1024, tn=KV_RANK, tk=2048)
    dw_uq = _mm(qn_t, dq2, name="dw_uq", out_dtype=BF16, tm=Q_RANK, tn=1024, tk=1024)
    dw_ukv = _mm(kvn_t, dkv, name="dw_ukv", out_dtype=BF16, tm=KV_RANK, tn=1024, tk=1024)
    dw_uq_h = dw_uq.reshape(Q_RANK, N_HEADS, 256)
    s_uq = jnp.concatenate([dw_uq_h[:, :, :160], dw_uq_h[:, :, 192:224]], axis=2).transpose(1, 0, 2)
    s_ukv = dw_ukv.reshape(KV_RANK, N_HEADS, 256).transpose(1, 0, 2)
    up_parts = [_by_core(s_uq), _by_core(s_ukv)]
    dproj, dg_q, dg_kv, *up_other = _norm_bwd(proj, dqn, dkvn, g_q_latent, g_kv_latent, dkr_raw, dfl, dproj,
                                              _ToOtherCore(up_parts), tm=tm)
    up_sums = [_pair_sum(p, o_, core, tr=256, name=f"dw_up_pair_sum_{i}") for i, (p, o_) in enumerate(zip(up_parts, up_other))]
    dw_in_lat, l_uq, l_ukv = _mm(h_t, dproj, name="dw_in_lat", out_dtype=BF16, tm=1024, tn=LAT_W, tk=1024,
                                 b_cols=(0, LAT_W), rider=_Exchange([(s, ALL_CHIPS, None, False, False) for s in up_sums]))
    p_late = _w_in_blocks(LATE_CHIPS, dw_in_lat, dw_in_rest)
    (o_late,) = _to_other_core([p_late], name="dw_in_late_to_core")
    s_late = _pair_sum(p_late, o_late, core, tr=256, name="dw_in_late_pair_sum")
    dh, l_in = _mm(dproj, w_in_n, name="d_h", nt=True, out_dtype=F32, tm=2048, tn=1024, tk=NP_IN // 4,
                   rider=_Exchange([(s_late, LATE_CHIPS, l_in, False, False)]))
    grad_x, dg_pre = _prenorm_bwd(dh, x2, g_pre, dy, tm=min(512, t))

    small = _gather_small(_pack_small(dg_pre, dg_post, dg_q, dg_kv, dbf[:, :N_HEADS], loss_part))

    res_in = _adamw(l_in, w_in[0], m_w_in[0], v_w_in[0], tr=256, name="adamw_w_in")
    res_uq = _adamw(l_uq, w_uq[0], m_w_uq[0], v_w_uq[0], tr=256, name="adamw_w_uq")
    res_ukv = _adamw(l_ukv, w_ukv[0], m_w_ukv[0], v_w_ukv[0], tr=256, name="adamw_w_ukv")
    res_out = _adamw(l_out, w_out[0], m_w_out[0], v_w_out[0], tr=128, name="adamw_w_out")
    res_small = _adamw_small(
        small,
        _pack_small(g_pre, g_post, g_q_latent, g_kv_latent, b_forget),
        _pack_small(m_g_pre, m_g_post, m_g_q_latent, m_g_kv_latent, m_b_forget),
        _pack_small(v_g_pre, v_g_post, v_g_q_latent, v_g_kv_latent, v_b_forget))
    small_out = [_unpack_small(r) for r in res_small]
    loss = small_out[0][1]

    def leaves(kind):
        (s_pre, s_post, s_q, s_kv, s_bf), _ = small_out[kind]
        return [s_pre, res_in[kind][None], s_q, res_uq[kind][None], s_kv, res_ukv[kind][None], s_bf,
                res_out[kind][None], s_post]

    return (loss, grad_x.reshape(x.shape), *leaves(0), *leaves(1), *leaves(2), *leaves(3))
```

```python
import numpy as np
import jax
import jax.numpy as jnp
from jax import lax
from jax.experimental import pallas as pl
from jax.experimental.pallas import tpu as pltpu

F32 = jnp.float32
BF16 = jnp.bfloat16
MESH = pl.DeviceIdType.MESH

N_DEV = 8
D_MODEL = 2048
N_HEADS = 8
HEAD = 128
Q_RANK = 768
KV_RANK = 512
ROPE = 64
D_IN = 6472
SHARD_IN = D_IN // N_DEV
NORM_EPS = 1e-6
ROPE_THETA = 10000.0
MLA_SCALE = (HEAD + ROPE) ** -0.5
FOX_SCALE = HEAD ** -0.5

QL0, KR0, FL0, KVL0, GM0, GF0, FQ0, FK0, FV0, NP_IN = 0, 768, 896, 1024, 1536, 2560, 3584, 4608, 5632, 6656
LAT_W = GM0
LANE = 128
_SEGMENTS = ((0, 768, QL0), (768, 512, KVL0), (1280, 32, KR0), (1312, 32, KR0 + 64), (1344, 1024, GM0),
             (2368, 3072, FQ0), (5440, 8, FL0), (5448, 1024, GF0))
LOG2E = 1.4426950408889634

ADAM_LR = 0.001
ADAM_B1 = 0.9
ADAM_B2 = 0.999
ADAM_EPS = 1e-08
ADAM_WD = 0.01
ADAM_STEP = 10

VMEM_LIMIT_BYTES = 56 * 1024 * 1024
SMALL_ROWS, SMALL_COLS = 8, 768


def _params(n_grid=0):
    return pltpu.CompilerParams(vmem_limit_bytes=VMEM_LIMIT_BYTES,
                                dimension_semantics=("arbitrary",) * n_grid if n_grid else None)


def _sigmoid(z):
    return 1.0 / (1.0 + jnp.exp(-z))


def _split3(v):
    a = v.astype(BF16)
    r = v - a.astype(F32)
    b = r.astype(BF16)
    c = (r - b.astype(F32)).astype(BF16)
    return a, b, c


def _mm(a, b, *, name, nt=False, out_dtype=F32, tm=1024, tn=512, tk=2048, b_cols=None, rider=None, f32_cols=None):
    m, k_dim = a.shape
    n = b.shape[0] if nt else b.shape[1]
    col0 = 0
    if b_cols is not None:
        assert not nt
        col0, n = b_cols
    assert (b.shape[1] if nt else b.shape[0]) == k_dim
    tm, tn, tk = min(tm, m), min(tn, n), min(tk, k_dim)
    assert m % tm == 0 and n % tn == 0 and k_dim % tk == 0 and col0 % tn == 0, (name, a.shape, b.shape)
    nk = k_dim // tk
    j0 = col0 // tn
    grid = (m // tm, n // tn, nk)
    dims = (((1,), (1 if nt else 0,)), ((), ()))
    n_rin = len(rider.operands) if rider else 0
    n_rout = len(rider.out_shape) if rider else 0
    n_x = 0 if f32_cols is None else 1
    if n_x:
        x_tile, x_lo = divmod(f32_cols[0] - col0, tn)
        x_w = f32_cols[1]
        assert x_lo + x_w <= tn and x_lo % LANE == 0 and x_w % LANE == 0

    def body(*refs):
        a_ref, b_ref = refs[:2]
        o_ref = refs[2 + n_rin]
        acc_ref = refs[3 + n_rin + n_rout + n_x]
        i, j, k = pl.program_id(0), pl.program_id(1), pl.program_id(2)
        if rider:
            rider_refs = (refs[2:2 + n_rin], refs[3 + n_rin:3 + n_rin + n_rout], refs[4 + n_rin + n_rout + n_x:])

            @pl.when(jnp.logical_and(i == 0, jnp.logical_and(j == 0, k == 0)))
            def _():
                rider.start(*rider_refs)

        @pl.when(k == 0)
        def _():
            acc_ref[...] = jnp.zeros_like(acc_ref)

        acc_ref[...] += lax.dot_general(a_ref[...], b_ref[...], dims, preferred_element_type=F32)

        @pl.when(k == nk - 1)
        def _():
            o_ref[...] = acc_ref[...].astype(o_ref.dtype)

        if n_x:
            @pl.when(jnp.logical_and(k == nk - 1, j == x_tile))
            def _():
                refs[3 + n_rin + n_rout][...] = acc_ref[:, x_lo:x_lo + x_w]

        if rider:
            @pl.when(jnp.logical_and(i == grid[0] - 1, jnp.logical_and(j == grid[1] - 1, k == nk - 1)))
            def _():
                rider.wait(*rider_refs)

    b_spec = (pl.BlockSpec((tn, tk), lambda i, j, k: (j, k)) if nt
              else pl.BlockSpec((tk, tn), lambda i, j, k: (k, j0 + j)))
    a_spec = pl.BlockSpec((tm, tk), lambda i, j, k: (i, k))
    any_spec = pl.BlockSpec(memory_space=pl.ANY)
    out = pl.pallas_call(
        body, name=name, grid=grid,
        in_specs=[a_spec, b_spec] + [any_spec] * n_rin,
        out_specs=[pl.BlockSpec((tm, tn), lambda i, j, k: (i, j))] + [any_spec] * n_rout
        + ([pl.BlockSpec((tm, x_w), lambda i, j, k: (i, 0))] if n_x else []),
        out_shape=[jax.ShapeDtypeStruct((m, n), out_dtype)] + (list(rider.out_shape) if rider else [])
        + ([jax.ShapeDtypeStruct((m, x_w), F32)] if n_x else []),
        scratch_shapes=[pltpu.VMEM((tm, tn), F32)] + (list(rider.scratch) if rider else []),
        input_output_aliases={2 + i_in: 1 + i_out for i_in, i_out in rider.aliases.items()} if rider else {},
        compiler_params=_params(3),
    )(a, b, *(rider.operands if rider else ()))
    return out if rider or n_x else out[0]


def _prenorm(x, g, rider, *, tm):
    t = x.shape[0]
    n_steps = t // tm
    n_rin, n_rout = len(rider.operands), len(rider.out_shape)

    def body(*refs):
        x_ref, g_ref = refs[:2]
        h_ref, ht_ref = refs[2 + n_rin:4 + n_rin]
        rider_refs = (refs[2:2 + n_rin], refs[4 + n_rin:4 + n_rin + n_rout], refs[4 + n_rin + n_rout:])
        i = pl.program_id(0)
        pl.when(i == 0)(lambda: rider.start(*rider_refs))
        xv = x_ref[...]
        r = lax.rsqrt(jnp.mean(xv * xv, axis=-1, keepdims=True) + NORM_EPS)
        h = xv * r * g_ref[...]
        h_ref[...] = h.astype(BF16)
        ht_ref[...] = h.T.astype(BF16)
        pl.when(i == n_steps - 1)(lambda: rider.wait(*rider_refs))

    any_spec = pl.BlockSpec(memory_space=pl.ANY)
    return pl.pallas_call(
        body, name="prenorm", grid=(n_steps,),
        in_specs=[pl.BlockSpec((tm, D_MODEL), lambda i: (i, 0)), pl.BlockSpec((1, D_MODEL), lambda i: (0, 0))]
        + [any_spec] * n_rin,
        out_specs=[pl.BlockSpec((tm, D_MODEL), lambda i: (i, 0)), pl.BlockSpec((D_MODEL, tm), lambda i: (0, i))]
        + [any_spec] * n_rout,
        out_shape=[jax.ShapeDtypeStruct((t, D_MODEL), BF16), jax.ShapeDtypeStruct((D_MODEL, t), BF16)]
        + list(rider.out_shape),
        scratch_shapes=list(rider.scratch),
        compiler_params=_params(1),
    )(x, g, *rider.operands)


def _rope_rows():
    inv = (np.float32(ROPE_THETA) ** (-np.arange(0, ROPE, 2, dtype=np.float32) / np.float32(ROPE))).astype(np.float32)
    invf = np.zeros((1, LANE), np.float32)
    sgn = np.zeros((1, LANE), np.float32)
    invf[0, 0:32] = inv
    invf[0, 64:96] = inv
    sgn[0, 0:32] = -1.0
    sgn[0, 64:96] = 1.0
    return jnp.asarray(invf), jnp.asarray(sgn)


def _rot(v, cos_t, sin_t):
    return v * cos_t + pltpu.roll(v, 64, 1) * sin_t


def _rot_bwd(dv, cos_t, sin_t):
    return dv * cos_t + pltpu.roll(dv * sin_t, 64, 1)


def _mid_fwd(proj, flog, g_q, g_kv, bf_row, pos_col, rider, *, tm):
    t = proj.shape[0]
    n_steps = t // tm
    invf, sgn = _rope_rows()
    n_rin, n_rout = len(rider.operands), len(rider.out_shape)

    def body(*refs):
        p_ref, fl_ref, gq_ref, gkv_ref, bf_ref, pos_ref, invf_ref, sgn_ref = refs[:8]
        qn_ref, kvn_ref, kr_ref, cos_ref, sin_ref, c_ref, qnt_ref, kvnt_ref = refs[8 + n_rin:16 + n_rin]
        carry_ref = refs[16 + n_rin + n_rout]
        rider_refs = (refs[8:8 + n_rin], refs[16 + n_rin:16 + n_rin + n_rout], refs[17 + n_rin + n_rout:])
        i = pl.program_id(0)
        pl.when(i == 0)(lambda: rider.start(*rider_refs))
        pl.when(i == n_steps - 1)(lambda: rider.wait(*rider_refs))

        @pl.when(i == 0)
        def _():
            carry_ref[...] = jnp.zeros_like(carry_ref)

        ql = p_ref[:, QL0:QL0 + Q_RANK].astype(F32)
        r = lax.rsqrt(jnp.mean(ql * ql, axis=-1, keepdims=True) + NORM_EPS)
        qn = ql * r * gq_ref[...]
        qn_ref[...] = qn.astype(BF16)
        qnt_ref[...] = qn.T.astype(BF16)
        kvl = p_ref[:, KVL0:KVL0 + KV_RANK].astype(F32)
        r = lax.rsqrt(jnp.mean(kvl * kvl, axis=-1, keepdims=True) + NORM_EPS)
        kvn = kvl * r * gkv_ref[...]
        kvn_ref[...] = kvn.astype(BF16)
        kvnt_ref[...] = kvn.T.astype(BF16)

        ang = pos_ref[...] * invf_ref[...]
        cos_t = jnp.cos(ang)
        sin_t = jnp.sin(ang) * sgn_ref[...]
        cos_ref[...] = cos_t
        sin_ref[...] = sin_t
        kr_ref[...] = _rot(p_ref[:, KR0:KR0 + LANE].astype(F32), cos_t, sin_t).astype(BF16)

        z = fl_ref[...] + bf_ref[...]
        logf = jnp.minimum(z, 0.0) - jnp.log(1.0 + jnp.exp(-jnp.abs(z)))
        row = lax.broadcasted_iota(jnp.int32, (tm, tm), 0)
        col = lax.broadcasted_iota(jnp.int32, (tm, tm), 1)
        tri = (col <= row).astype(BF16)
        acc = carry_ref[0:1, :]
        for part in _split3(logf):
            acc = acc + jnp.dot(tri, part, preferred_element_type=F32)
        c_ref[...] = acc * (1.0 / FOX_SCALE)
        carry_ref[0:1, :] = carry_ref[0:1, :] + jnp.sum(logf, axis=0, keepdims=True)

    row_spec = lambda w: pl.BlockSpec((tm, w), lambda i: (i, 0))
    vec_spec = lambda w: pl.BlockSpec((1, w), lambda i: (0, 0))
    return pl.pallas_call(
        body, name="mid_fwd", grid=(n_steps,),
        in_specs=[row_spec(LAT_W), row_spec(LANE), vec_spec(Q_RANK), vec_spec(KV_RANK), vec_spec(LANE),
                  pl.BlockSpec((tm, 1), lambda i: (i, 0)), vec_spec(LANE), vec_spec(LANE)]
        + [pl.BlockSpec(memory_space=pl.ANY)] * n_rin,
        out_specs=[row_spec(Q_RANK), row_spec(KV_RANK), row_spec(LANE), row_spec(LANE), row_spec(LANE), row_spec(LANE),
                   pl.BlockSpec((Q_RANK, tm), lambda i: (0, i)), pl.BlockSpec((KV_RANK, tm), lambda i: (0, i))]
        + [pl.BlockSpec(memory_space=pl.ANY)] * n_rout,
        out_shape=[jax.ShapeDtypeStruct((t, Q_RANK), BF16), jax.ShapeDtypeStruct((t, KV_RANK), BF16),
                   jax.ShapeDtypeStruct((t, LANE), BF16), jax.ShapeDtypeStruct((t, LANE), F32),
                   jax.ShapeDtypeStruct((t, LANE), F32), jax.ShapeDtypeStruct((t, LANE), F32),
                   jax.ShapeDtypeStruct((Q_RANK, t), BF16), jax.ShapeDtypeStruct((KV_RANK, t), BF16)]
        + list(rider.out_shape),
        scratch_shapes=[pltpu.VMEM((8, LANE), F32)] + list(rider.scratch),
        input_output_aliases={8 + i_in: 8 + i_out for i_in, i_out in rider.aliases.items()},
        compiler_params=_params(1),
    )(proj, flog, g_q, g_kv, bf_row, pos_col, invf, sgn, *rider.operands)


def _q_up_rope(qn, w_uq_n, cos_t, sin_t, *, tm):
    t = qn.shape[0]
    tn = 2 * 256

    def body(a_ref, b_ref, cos_ref, sin_ref, o_ref):
        q = jnp.dot(a_ref[...], b_ref[...], preferred_element_type=F32)
        c, s = cos_ref[...], sin_ref[...]
        for u in range(tn // 256):
            o_ref[:, 256 * u:256 * u + 128] = q[:, 256 * u:256 * u + 128].astype(BF16)
            o_ref[:, 256 * u + 128:256 * u + 256] = _rot(q[:, 256 * u + 128:256 * u + 256], c, s).astype(BF16)

    return pl.pallas_call(
        body, name="q_up_rope", grid=(t // tm, N_HEADS * 256 // tn),
        in_specs=[pl.BlockSpec((tm, Q_RANK), lambda i, j: (i, 0)), pl.BlockSpec((Q_RANK, tn), lambda i, j: (0, j)),
                  pl.BlockSpec((tm, LANE), lambda i, j: (i, 0)), pl.BlockSpec((tm, LANE), lambda i, j: (i, 0))],
        out_specs=pl.BlockSpec((tm, tn), lambda i, j: (i, j)),
        out_shape=jax.ShapeDtypeStruct((t, N_HEADS * 256), BF16),
        compiler_params=_params(2),
    )(qn, w_uq_n, cos_t, sin_t)


def _attn_fwd(fox, operands, *, t, tb, name):
    nb = t // tb
    scale = FOX_SCALE if fox else MLA_SCALE
    exp2_scale = scale * LOG2E
    pair = 2 * HEAD
    pair0 = N_HEADS // 2 if fox else 0
    q_w = HEAD if fox else 2 * HEAD
    nt_dims = (((1,), (1,)), ((), ()))
    tn_dims = (((0,), (0,)), ((), ()))

    def body(*refs):
        if fox:
            (q_ref, k_ref, v_ref, gate_ref, cq_ref, ck_ref, _, _, _,
             o_ref, og_ref, ogt_ref, lse_ref, m_s, l_s, acc_s) = refs
        else:
            q_ref, kv_ref, kr_ref, gate_ref, o_ref, og_ref, ogt_ref, lse_ref, m_s, l_s, acc_s = refs
        qi = pl.program_id(1)
        m_s[...] = jnp.full_like(m_s, -jnp.inf)
        l_s[...] = jnp.zeros_like(l_s)
        acc_s[...] = jnp.zeros_like(acc_s)

        def chunk(kc, masked):
            off = pl.multiple_of(kc * tb, tb)
            scores = []
            for u in range(2):
                q = q_ref[:, q_w * u:q_w * (u + 1)]
                if fox:
                    kk = k_ref[pl.ds(off, tb), HEAD * u:HEAD * (u + 1)]
                else:
                    kk = jnp.concatenate([kv_ref[pl.ds(off, tb), pair * u:pair * u + HEAD],
                                          kr_ref[pl.ds(off, tb), :]], axis=1)
                s = lax.dot_general(kk, q, nt_dims, preferred_element_type=F32)
                if fox:
                    s = s + cq_ref[u, 0] - ck_ref[u, pl.ds(off, tb), :]
                if masked:
                    row = lax.broadcasted_iota(jnp.int32, (tb, tb), 0)
                    col = lax.broadcasted_iota(jnp.int32, (tb, tb), 1)
                    s = jnp.where(row <= col, s, -jnp.inf)
                scores.append(s)
            for u in range(2):
                s = scores[u]
                m_prev = m_s[u]
                m_new = jnp.maximum(m_prev, jnp.max(s, axis=0, keepdims=True))
                alpha = jnp.exp2((m_prev - m_new) * exp2_scale)
                p = jnp.exp2((s - m_new) * exp2_scale)
                l_s[u] = alpha * l_s[u] + jnp.sum(p, axis=0, keepdims=True)
                if fox:
                    vv = v_ref[pl.ds(off, tb), HEAD * u:HEAD * (u + 1)]
                else:
                    vv = kv_ref[pl.ds(off, tb), pair * u + HEAD:pair * (u + 1)]
                acc_s[u] = alpha * acc_s[u] + lax.dot_general(vv, p.astype(BF16), tn_dims,
                                                              preferred_element_type=F32)
                m_s[u] = m_new

        def loop_body(kc, carry):
            chunk(kc, False)
            return carry

        lax.fori_loop(0, qi, loop_body, 0)
        chunk(qi, True)
        for u in range(2):
            cols = slice(HEAD * u, HEAD * (u + 1))
            o_t = acc_s[u] / l_s[u]
            o = o_t.T
            o_ref[:, cols] = o
            g = gate_ref[:, cols].astype(F32)
            silu = g * _sigmoid(g)
            og_ref[:, cols] = (o * silu).astype(BF16)
            ogt_ref[cols, :] = (o_t * silu.T).astype(BF16)
            lse_ref[u, 0] = m_s[u] * scale + jnp.log(l_s[u])

    any_spec = pl.BlockSpec(memory_space=pl.ANY)
    row_stat = pl.BlockSpec((2, 1, 1, tb), lambda g, i: (g, i, 0, 0))
    if fox:
        proj, c_col, c_row4, o_all, og_all, ogt_all = operands
        ins = [proj, proj, proj, proj, c_row4, c_col, o_all, og_all, ogt_all]
        in_specs = [pl.BlockSpec((tb, pair), lambda g, i: (i, FQ0 // pair + g)),
                    pl.BlockSpec((t, pair), lambda g, i: (0, FK0 // pair + g)),
                    pl.BlockSpec((t, pair), lambda g, i: (0, FV0 // pair + g)),
                    pl.BlockSpec((tb, pair), lambda g, i: (i, GF0 // pair + g)),
                    row_stat, pl.BlockSpec((2, t, 1), lambda g, i: (g, 0, 0)), any_spec, any_spec, any_spec]
        aliases = {6: 0, 7: 1, 8: 2}
    else:
        q_full, kv, kr, proj = operands
        ins = [q_full, kv, kr, proj]
        in_specs = [pl.BlockSpec((tb, 2 * pair), lambda g, i: (i, g)),
                    pl.BlockSpec((t, 2 * pair), lambda g, i: (0, g)),
                    pl.BlockSpec((t, HEAD), lambda g, i: (0, 0)),
                    pl.BlockSpec((tb, pair), lambda g, i: (i, GM0 // pair + g))]
        aliases = {}
    return pl.pallas_call(
        body, name=name, grid=(N_HEADS // 2, nb), in_specs=in_specs,
        out_specs=[pl.BlockSpec((tb, pair), lambda g, i: (i, pair0 + g)),
                   pl.BlockSpec((tb, pair), lambda g, i: (i, pair0 + g)),
                   pl.BlockSpec((pair, tb), lambda g, i: (pair0 + g, i)), row_stat],
        out_shape=[jax.ShapeDtypeStruct((t, 2 * N_HEADS * HEAD), F32), jax.ShapeDtypeStruct((t, 2 * N_HEADS * HEAD), BF16),
                   jax.ShapeDtypeStruct((2 * N_HEADS * HEAD, t), BF16), jax.ShapeDtypeStruct((N_HEADS, nb, 1, tb), F32)],
        scratch_shapes=[pltpu.VMEM((2, 1, tb), F32), pltpu.VMEM((2, 1, tb), F32), pltpu.VMEM((2, HEAD, tb), F32)],
        input_output_aliases=aliases,
        compiler_params=_params(2),
    )(*ins)


def _out_norm_loss(og, w_out_n, x, target, g, *, tm):
    t = og.shape[0]

    def body(og_ref, w_ref, x_ref, t_ref, g_ref, dy_ref, do_ref, dg_ref, loss_ref):
        i = pl.program_id(0)

        @pl.when(i == 0)
        def _():
            dg_ref[...] = jnp.zeros_like(dg_ref)
            loss_ref[...] = jnp.zeros_like(loss_ref)

        ov = jnp.dot(og_ref[...], w_ref[...], preferred_element_type=F32)
        gv = g_ref[...]
        r = lax.rsqrt(jnp.mean(ov * ov, axis=-1, keepdims=True) + NORM_EPS)
        oh = ov * r
        e = x_ref[...] + oh * gv - t_ref[...]
        loss_ref[...] += 0.5 * jnp.sum(jnp.mean(e * e, axis=-1, keepdims=True), axis=0, keepdims=True)
        dy = e * (1.0 / D_MODEL)
        dy_ref[...] = dy
        dyg = dy * gv
        do_ref[...] = (r * (dyg - oh * jnp.mean(dyg * oh, axis=-1, keepdims=True))).astype(BF16)
        dg_ref[...] += jnp.sum(dy * oh, axis=0, keepdims=True)

    row = pl.BlockSpec((tm, D_MODEL), lambda i: (i, 0))
    vec = pl.BlockSpec((1, D_MODEL), lambda i: (0, 0))
    whole_w = pl.BlockSpec((D_MODEL, D_MODEL), lambda i: (0, 0), pipeline_mode=pl.Buffered(1))
    return pl.pallas_call(
        body, name="out_norm_loss", grid=(t // tm,),
        in_specs=[row, whole_w, row, row, vec],
        out_specs=[row, row, vec, pl.BlockSpec((1, 1), lambda i: (0, 0))],
        out_shape=[jax.ShapeDtypeStruct((t, D_MODEL), F32), jax.ShapeDtypeStruct((t, D_MODEL), BF16),
                   jax.ShapeDtypeStruct((1, D_MODEL), F32), jax.ShapeDtypeStruct((1, 1), F32)],
        compiler_params=_params(1),
    )(og, w_out_n, x, target, g)


def _dog_gate(d_o_post, w_out_n, o_all, proj, rider, *, tm):
    t = d_o_post.shape[0]
    n_group = 4
    pair = n_group * HEAD
    gate_blk = GM0 // pair
    assert GM0 % pair == 0 and GF0 == GM0 + N_HEADS * HEAD
    grid = (t // tm, 2 * N_HEADS // n_group)
    n_rin, n_rout = len(rider.operands), len(rider.out_shape)

    def body(*refs):
        do_ref, w_ref, o_ref, p_ref = refs[:4]
        dattn_ref, dproj_ref, delta_ref = refs[4 + n_rin:7 + n_rin]
        rider_refs = (refs[4:4 + n_rin], refs[7 + n_rin:7 + n_rin + n_rout], refs[7 + n_rin + n_rout:])
        i, j = pl.program_id(0), pl.program_id(1)
        pl.when(jnp.logical_and(i == 0, j == 0))(lambda: rider.start(*rider_refs))

        @pl.when(j == 0)
        def _():
            delta_ref[...] = jnp.zeros_like(delta_ref)

        dog = lax.dot_general(do_ref[...], w_ref[...], (((1,), (1,)), ((), ())), preferred_element_type=F32)
        g = p_ref[...].astype(F32)
        ov = o_ref[...]
        sg = _sigmoid(g)
        d_o = dog * (g * sg)
        dattn_ref[...] = d_o.astype(BF16)
        dproj_ref[...] = (dog * ov * (sg * (1.0 + g * (1.0 - sg)))).astype(BF16)
        prod = d_o * ov
        lane = lax.broadcasted_iota(jnp.int32, (tm, LANE), 1)
        delta = delta_ref[...]
        for u in range(n_group):
            part = jnp.sum(prod[:, HEAD * u:HEAD * (u + 1)], axis=-1, keepdims=True)
            delta = jnp.where(lane == n_group * j + u, part, delta)
        delta_ref[...] = delta
        pl.when(jnp.logical_and(i == grid[0] - 1, j == grid[1] - 1))(lambda: rider.wait(*rider_refs))

    any_spec = pl.BlockSpec(memory_space=pl.ANY)
    return pl.pallas_call(
        body, name="dog_gate", grid=grid,
        in_specs=[pl.BlockSpec((tm, D_MODEL), lambda i, j: (i, 0)), pl.BlockSpec((pair, D_MODEL), lambda i, j: (j, 0)),
                  pl.BlockSpec((tm, pair), lambda i, j: (i, j)), pl.BlockSpec((tm, pair), lambda i, j: (i, gate_blk + j))]
        + [any_spec] * n_rin,
        out_specs=[pl.BlockSpec((tm, pair), lambda i, j: (i, j)), pl.BlockSpec((tm, pair), lambda i, j: (i, gate_blk + j)),
                   pl.BlockSpec((tm, LANE), lambda i, j: (i, 0))] + [any_spec] * n_rout,
        out_shape=[jax.ShapeDtypeStruct((t, 2048), BF16), jax.ShapeDtypeStruct((t, NP_IN), BF16),
                   jax.ShapeDtypeStruct((t, LANE), F32)] + list(rider.out_shape),
        scratch_shapes=list(rider.scratch),
        compiler_params=_params(2),
    )(d_o_post, w_out_n, o_all, proj, *rider.operands)


def _attn_bwd(fox, operands, *, t, tb, name, rider=None):
    nb = t // tb
    n_pairs = N_HEADS // 2
    pair = 2 * HEAD
    scale = FOX_SCALE if fox else MLA_SCALE
    q_w = HEAD if fox else 2 * HEAD
    nt_dims = (((1,), (1,)), ((), ()))
    tn_dims = (((0,), (0,)), ((), ()))
    n_rin = len(rider.operands) if rider else 0
    n_rout = len(rider.out_shape) if rider else 0
    n_in, n_out, n_scr = (9, 3, 9) if fox else (6, 3, 2)

    def body(*refs):
        ends = np.cumsum([0, n_in, n_rin, n_out, n_rout, n_scr])
        in_refs, rider_in, out_refs, rider_out, scr_refs = (refs[a:b] for a, b in zip(ends[:-1], ends[1:]))
        rider_refs = (rider_in, rider_out, refs[ends[-1]:])
        if fox:
            q_ref, k_ref, v_ref, do_ref, lse_ref, dl_ref, cq_ref, ck_ref, _ = in_refs
            dproj_ref, dck_ref, dcq_ref = out_refs
            dq_acc, dk_s, dv_s, dc_s, dcq_s, stage_q, stage_k, stage_v, put_sems = scr_refs
        else:
            q_ref, kv_ref, kr_ref, do_ref, lse_ref, dl_ref = in_refs
            dq_acc, dkv_ref, dkr_ref = out_refs
            dk_s, dv_s = scr_refs
        g = pl.program_id(0)
        ki = pl.program_id(1)
        if rider:
            @pl.when(jnp.logical_and(g == 0, ki == 0))
            def _():
                rider.start(*rider_refs)

        @pl.when(ki == 0)
        def _():
            dq_acc[...] = jnp.zeros_like(dq_acc)
            if fox:
                dcq_s[...] = jnp.zeros_like(dcq_s)

        dk_s[...] = jnp.zeros_like(dk_s)
        dv_s[...] = jnp.zeros_like(dv_s)
        if fox:
            dc_s[...] = jnp.zeros_like(dc_s)
            keys = [k_ref[:, HEAD * u:HEAD * (u + 1)] for u in range(2)]
            vals = [v_ref[:, HEAD * u:HEAD * (u + 1)] for u in range(2)]
        else:
            keys = [jnp.concatenate([kv_ref[:, pair * u:pair * u + HEAD], kr_ref[...]], axis=1) for u in range(2)]
            vals = [kv_ref[:, pair * u + HEAD:pair * (u + 1)] for u in range(2)]

        def chunk(qc, masked):
            off = pl.multiple_of(qc * tb, tb)
            for u in range(2):
                kk, vv = keys[u], vals[u]
                qq = q_ref[pl.ds(off, tb), q_w * u:q_w * (u + 1)]
                dd = do_ref[pl.ds(off, tb), HEAD * u:HEAD * (u + 1)]
                s = lax.dot_general(kk, qq, nt_dims, preferred_element_type=F32)
                if fox:
                    s = s + cq_ref[u, qc] - ck_ref[u]
                if masked:
                    row = lax.broadcasted_iota(jnp.int32, (tb, tb), 0)
                    col = lax.broadcasted_iota(jnp.int32, (tb, tb), 1)
                    s = jnp.where(row <= col, s, -jnp.inf)
                p = jnp.exp2(s * (scale * LOG2E) - lse_ref[u, qc] * LOG2E)
                dv_s[u] += jnp.dot(p.astype(BF16), dd, preferred_element_type=F32)
                dp = lax.dot_general(vv, dd, nt_dims, preferred_element_type=F32)
                ds = p * (dp - dl_ref[u, qc])
                if fox:
                    dc_s[u] += jnp.sum(ds, axis=1, keepdims=True)
                    dcq_s[u, qc] += jnp.sum(ds, axis=0, keepdims=True)
                dsb = (ds * scale).astype(BF16)
                dk_s[u] += jnp.dot(dsb, qq, preferred_element_type=F32)
                dq_acc[pl.ds(off, tb), q_w * u:q_w * (u + 1)] += lax.dot_general(dsb, kk, tn_dims,
                                                                                 preferred_element_type=F32)

        chunk(ki, True)

        def loop_body(qc, carry):
            chunk(qc, False)
            return carry

        lax.fori_loop(ki + 1, nb, loop_body, 0)

        def put(stage_ref, rows, seg0, sem):
            col0 = pl.multiple_of(seg0 + g * pair, pair)
            return pltpu.make_async_copy(stage_ref, dproj_ref.at[rows, pl.ds(col0, pair)], sem)

        if fox:
            rows = pl.ds(pl.multiple_of(ki * tb, tb), tb)
            block_puts = [put(stage_k, rows, FK0, put_sems.at[1]), put(stage_v, rows, FV0, put_sems.at[2])]
            pair_put = put(stage_q, pl.ds(0, t), FQ0, put_sems.at[0])

            @pl.when(jnp.logical_or(g > 0, ki > 0))
            def _():
                for cp in block_puts:
                    cp.wait()

            for u in range(2):
                stage_k[:, HEAD * u:HEAD * (u + 1)] = dk_s[u].astype(BF16)
                stage_v[:, HEAD * u:HEAD * (u + 1)] = dv_s[u].astype(BF16)
                dck_ref[u] = -dc_s[u]
            for cp in block_puts:
                cp.start()

            @pl.when(ki == nb - 1)
            def _():
                @pl.when(g > 0)
                def _():
                    pair_put.wait()

                stage_q[...] = dq_acc[...].astype(BF16)
                pair_put.start()
                dcq_ref[...] = dcq_s[...]

            @pl.when(jnp.logical_and(g == n_pairs - 1, ki == nb - 1))
            def _():
                for cp in block_puts + [pair_put]:
                    cp.wait()
        else:
            dkv_ref[...] = jnp.concatenate([dk_s[0, :, :HEAD], dv_s[0], dk_s[1, :, :HEAD], dv_s[1]], axis=1).astype(BF16)
            dkr_ref[...] = jnp.concatenate([dk_s[0, :, HEAD:], dk_s[1, :, HEAD:]], axis=1)

        if rider:
            @pl.when(jnp.logical_and(g == n_pairs - 1, ki == nb - 1))
            def _():
                rider.wait(*rider_refs)

    stat = pl.BlockSpec((2, nb, 1, tb), lambda g, i: (g, 0, 0, 0))
    aliases = {}
    if fox:
        proj, d_o, lse4, delta4, c_row4, c_col, dproj = operands
        ins = [proj, proj, proj, d_o, lse4, delta4, c_row4, c_col, dproj]
        any_spec = pl.BlockSpec(memory_space=pl.ANY)
        in_specs = [pl.BlockSpec((t, pair), lambda g, i: (0, FQ0 // pair + g)),
                    pl.BlockSpec((tb, pair), lambda g, i: (i, FK0 // pair + g)),
                    pl.BlockSpec((tb, pair), lambda g, i: (i, FV0 // pair + g)),
                    pl.BlockSpec((t, pair), lambda g, i: (0, n_pairs + g)),
                    stat, stat, stat, pl.BlockSpec((2, tb, 1), lambda g, i: (g, i, 0)), any_spec]
        aliases = {8: 0}
        out_specs = [any_spec, pl.BlockSpec((2, tb, 1), lambda g, i: (g, i, 0)), stat]
        out_shape = [jax.ShapeDtypeStruct(dproj.shape, dproj.dtype), jax.ShapeDtypeStruct((N_HEADS, t, 1), F32),
                     jax.ShapeDtypeStruct((N_HEADS, nb, 1, tb), F32)]
        scratch = [pltpu.VMEM((t, pair), F32), pltpu.VMEM((2, tb, HEAD), F32), pltpu.VMEM((2, tb, HEAD), F32),
                   pltpu.VMEM((2, tb, 1), F32), pltpu.VMEM((2, nb, 1, tb), F32),
                   pltpu.VMEM((t, pair), BF16), pltpu.VMEM((tb, pair), BF16), pltpu.VMEM((tb, pair), BF16),
                   pltpu.SemaphoreType.DMA((3,))]
    else:
        q_full, kv, kr, d_o, lse4, delta4 = operands
        ins = [q_full, kv, kr, d_o, lse4, delta4]
        in_specs = [pl.BlockSpec((t, 2 * pair), lambda g, i: (0, g)),
                    pl.BlockSpec((tb, 2 * pair), lambda g, i: (i, g)),
                    pl.BlockSpec((tb, HEAD), lambda g, i: (i, 0)),
                    pl.BlockSpec((t, pair), lambda g, i: (0, g)),
                    stat, stat]
        out_specs = [pl.BlockSpec((t, 2 * pair), lambda g, i: (0, g)), pl.BlockSpec((tb, 2 * pair), lambda g, i: (i, g)),
                     pl.BlockSpec((tb, pair), lambda g, i: (i, g))]
        out_shape = [jax.ShapeDtypeStruct((t, 2048), F32), jax.ShapeDtypeStruct((t, 2048), BF16),
                     jax.ShapeDtypeStruct((t, 1024), F32)]
        scratch = [pltpu.VMEM((2, tb, 2 * HEAD), F32), pltpu.VMEM((2, tb, HEAD), F32)]
    assert (len(ins), len(out_specs), len(scratch)) == (n_in, n_out, n_scr)
    if rider:
        any_spec = pl.BlockSpec(memory_space=pl.ANY)
        aliases = {**aliases, **{n_in + i_in: n_out + i_out for i_in, i_out in rider.aliases.items()}}
        ins = ins + list(rider.operands)
        in_specs = in_specs + [any_spec] * n_rin
        out_specs = out_specs + [any_spec] * n_rout
        out_shape = out_shape + list(rider.out_shape)
        scratch = scratch + list(rider.scratch)
    return pl.pallas_call(
        body, name=name, grid=(n_pairs, nb), in_specs=in_specs, out_specs=out_specs, out_shape=out_shape,
        scratch_shapes=scratch, input_output_aliases=aliases, compiler_params=_params(2),
    )(*ins)


def _mid_bwd(dq_full, dkr, cos_t, sin_t, dcq, dck, flog, bf_row, *, tm):
    t = dq_full.shape[0]
    n = t // tm

    def body(dq_ref, dkr_ref, cos_ref, sin_ref, dcq_ref, dck_ref, fl_ref, bf_ref,
             dq2_ref, dkraw_ref, dfl_ref, dbf_ref, carry_ref):
        i = pl.program_id(0)

        @pl.when(i == 0)
        def _():
            carry_ref[...] = jnp.zeros_like(carry_ref)
            dbf_ref[...] = jnp.zeros_like(dbf_ref)

        c, s = cos_ref[...], sin_ref[...]
        dkr_sum = jnp.zeros((tm, LANE), F32)
        for h in range(N_HEADS):
            dq2_ref[:, 256 * h:256 * h + 128] = dq_ref[:, 256 * h:256 * h + 128].astype(BF16)
            dq2_ref[:, 256 * h + 128:256 * h + 256] = _rot_bwd(dq_ref[:, 256 * h + 128:256 * h + 256], c, s).astype(BF16)
            dkr_sum = dkr_sum + dkr_ref[:, HEAD * h:HEAD * (h + 1)]
        dkraw_ref[...] = _rot_bwd(dkr_sum, c, s).astype(BF16)

        dc = dcq_ref[...]
        lane = lax.broadcasted_iota(jnp.int32, (tm, LANE), 1)
        for h in range(N_HEADS):
            dc = dc + jnp.where(lane == h, dck_ref[h], 0.0)
        row = lax.broadcasted_iota(jnp.int32, (tm, tm), 0)
        col = lax.broadcasted_iota(jnp.int32, (tm, tm), 1)
        tri = (col >= row).astype(BF16)
        acc = carry_ref[0:1, :]
        for part in _split3(dc):
            acc = acc + jnp.dot(tri, part, preferred_element_type=F32)
        carry_ref[0:1, :] = carry_ref[0:1, :] + jnp.sum(dc, axis=0, keepdims=True)
        z = fl_ref[...] + bf_ref[...]
        dz = acc / (1.0 + jnp.exp(z))
        dfl_ref[...] = dz.astype(BF16)
        dbf_ref[...] += jnp.sum(dz, axis=0, keepdims=True)

    rev = lambda w: pl.BlockSpec((tm, w), lambda i: (n - 1 - i, 0))
    vec = lambda w: pl.BlockSpec((1, w), lambda i: (0, 0))
    return pl.pallas_call(
        body, name="mid_bwd", grid=(n,),
        in_specs=[rev(2048), rev(1024), rev(LANE), rev(LANE), rev(LANE),
                  pl.BlockSpec((N_HEADS, tm, 1), lambda i: (0, n - 1 - i, 0)), rev(LANE), vec(LANE)],
        out_specs=[rev(2048), rev(LANE), rev(LANE), vec(LANE)],
        out_shape=[jax.ShapeDtypeStruct((t, 2048), BF16), jax.ShapeDtypeStruct((t, LANE), BF16),
                   jax.ShapeDtypeStruct((t, LANE), BF16), jax.ShapeDtypeStruct((1, LANE), F32)],
        scratch_shapes=[pltpu.VMEM((8, LANE), F32)],
        compiler_params=_params(1),
    )(dq_full, dkr, cos_t, sin_t, dcq, dck, flog, bf_row)


def _norm_bwd(proj, dqn, dkvn, g_q, g_kv, dkr_raw, dfl, dproj, rider, *, tm):
    t = proj.shape[0]
    n_steps = t // tm
    assert (KR0, FL0, KVL0, LAT_W) == (Q_RANK, Q_RANK + LANE, Q_RANK + 2 * LANE, Q_RANK + 2 * LANE + KV_RANK)
    n_rin, n_rout = len(rider.operands), len(rider.out_shape)

    def body(*refs):
        p_ref, dqn_ref, dkvn_ref, gq_ref, gkv_ref, dkr_ref, dfl_ref = refs[:7]
        dproj_ref, dgq_ref, dgkv_ref = refs[8 + n_rin:11 + n_rin]
        rider_refs = (refs[8:8 + n_rin], refs[11 + n_rin:11 + n_rin + n_rout], refs[11 + n_rin + n_rout:])
        i = pl.program_id(0)
        pl.when(i == 0)(lambda: rider.start(*rider_refs))
        pl.when(i == n_steps - 1)(lambda: rider.wait(*rider_refs))

        @pl.when(i == 0)
        def _():
            dgq_ref[...] = jnp.zeros_like(dgq_ref)
            dgkv_ref[...] = jnp.zeros_like(dgkv_ref)

        d_lat = []
        for lo, w, dn_ref, g_ref, dg_ref in ((QL0, Q_RANK, dqn_ref, gq_ref, dgq_ref),
                                             (KVL0, KV_RANK, dkvn_ref, gkv_ref, dgkv_ref)):
            xv = p_ref[:, lo:lo + w].astype(F32)
            r = lax.rsqrt(jnp.mean(xv * xv, axis=-1, keepdims=True) + NORM_EPS)
            xh = xv * r
            dn = dn_ref[...]
            dg_ref[...] += jnp.sum(dn * xh, axis=0, keepdims=True)
            dxh = dn * g_ref[...]
            d_lat.append((r * (dxh - xh * jnp.mean(dxh * xh, axis=-1, keepdims=True))).astype(BF16))
        dproj_ref[...] = jnp.concatenate([d_lat[0], dkr_ref[...], dfl_ref[...], d_lat[1]], axis=1)

    row = lambda w: pl.BlockSpec((tm, w), lambda i: (i, 0))
    vec = lambda w: pl.BlockSpec((1, w), lambda i: (0, 0))
    return pl.pallas_call(
        body, name="norm_bwd", grid=(n_steps,),
        in_specs=[row(LAT_W), row(Q_RANK), row(KV_RANK), vec(Q_RANK), vec(KV_RANK), row(LANE), row(LANE)]
        + [pl.BlockSpec(memory_space=pl.ANY)] * (1 + n_rin),
        out_specs=[row(LAT_W), vec(Q_RANK), vec(KV_RANK)] + [pl.BlockSpec(memory_space=pl.ANY)] * n_rout,
        out_shape=[jax.ShapeDtypeStruct(dproj.shape, dproj.dtype),
                   jax.ShapeDtypeStruct((1, Q_RANK), F32), jax.ShapeDtypeStruct((1, KV_RANK), F32)]
        + list(rider.out_shape),
        scratch_shapes=list(rider.scratch),
        input_output_aliases={7: 0, **{8 + i_in: 3 + i_out for i_in, i_out in rider.aliases.items()}},
        compiler_params=_params(1),
    )(proj, dqn, dkvn, g_q, g_kv, dkr_raw, dfl, dproj, *rider.operands)


def _prenorm_bwd(dh, x, g, dy, *, tm):
    t = x.shape[0]

    def body(dh_ref, x_ref, g_ref, dy_ref, gx_ref, dg_ref):
        i = pl.program_id(0)

        @pl.when(i == 0)
        def _():
            dg_ref[...] = jnp.zeros_like(dg_ref)

        xv = x_ref[...]
        r = lax.rsqrt(jnp.mean(xv * xv, axis=-1, keepdims=True) + NORM_EPS)
        xh = xv * r
        dn = dh_ref[...]
        dg_ref[...] += jnp.sum(dn * xh, axis=0, keepdims=True)
        dxh = dn * g_ref[...]
        gx_ref[...] = dy_ref[...] + r * (dxh - xh * jnp.mean(dxh * xh, axis=-1, keepdims=True))

    row = pl.BlockSpec((tm, D_MODEL), lambda i: (i, 0))
    vec = pl.BlockSpec((1, D_MODEL), lambda i: (0, 0))
    return pl.pallas_call(
        body, name="prenorm_bwd", grid=(t // tm,),
        in_specs=[row, row, vec, row], out_specs=[row, vec],
        out_shape=[jax.ShapeDtypeStruct((t, D_MODEL), F32), jax.ShapeDtypeStruct((1, D_MODEL), F32)],
        compiler_params=_params(1),
    )(dh, x, g, dy)


def _adam_math(w, g, m, v):
    m = ADAM_B1 * m + (1.0 - ADAM_B1) * g
    v = ADAM_B2 * v + (1.0 - ADAM_B2) * (g * g)
    m_hat = m / (1.0 - ADAM_B1 ** ADAM_STEP)
    v_hat = v / (1.0 - ADAM_B2 ** ADAM_STEP)
    delta = -ADAM_LR * (m_hat / (jnp.sqrt(v_hat) + ADAM_EPS) + ADAM_WD * w)
    return delta, m, v


def _adamw(land, w, m, v, *, tr, name):
    rows, cols = w.shape

    def body(l_ref, w_ref, m_ref, v_ref, g_ref, d_ref, nm_ref, nv_ref):
        g = l_ref[0].astype(F32)
        for s in range(1, N_CHIPS):
            g = g + l_ref[s].astype(F32)
        g_ref[...] = g
        d_ref[...], nm_ref[...], nv_ref[...] = _adam_math(w_ref[...], g, m_ref[...], v_ref[...])

    blk = pl.BlockSpec((tr, cols), lambda i: (i, 0))
    return pl.pallas_call(
        body, name=name, grid=(rows // tr,),
        in_specs=[pl.BlockSpec((N_CHIPS, tr, cols), lambda i: (0, i, 0)), blk, blk, blk],
        out_specs=[blk, blk, blk, blk],
        out_shape=[jax.ShapeDtypeStruct((rows, cols), F32)] * 4,
        compiler_params=_params(1),
    )(land, w, m, v)


def _adamw_small(gathered, w, m, v):
    def body(a_ref, w_ref, m_ref, v_ref, g_ref, d_ref, nm_ref, nv_ref):
        g = a_ref[0:SMALL_ROWS, :]
        for s in range(1, N_DEV):
            g = g + a_ref[SMALL_ROWS * s:SMALL_ROWS * (s + 1), :]
        g_ref[...] = g
        d_ref[...], nm_ref[...], nv_ref[...] = _adam_math(w_ref[...], g, m_ref[...], v_ref[...])

    return pl.pallas_call(
        body, name="adamw_small",
        out_shape=[jax.ShapeDtypeStruct((SMALL_ROWS, SMALL_COLS), F32)] * 4,
        compiler_params=_params(),
    )(gathered, w, m, v)


def _place():
    x, y, c = lax.axis_index("x"), lax.axis_index("y"), lax.axis_index("c")
    return x, y, c


def _flip(p, k):
    x, y, c = p
    return (1 - x if k & 4 else x, 1 - y if k & 2 else y, 1 - c if k & 1 else c)


def _index(p):
    return 4 * p[0] + 2 * p[1] + p[2]


class _AllGather:
    def __init__(self, shard):
        assert shard.shape[0] % 32 == 0
        self.half = shard.shape[0] // 2
        self.operands = [shard]
        self.out_shape = [jax.ShapeDtypeStruct((N_DEV,) + shard.shape, shard.dtype)]
        self.aliases = {}
        self.scratch = [pltpu.SemaphoreType.DMA((9,)), pltpu.SemaphoreType.DMA((9,)), pltpu.SemaphoreType.DMA(())]

    def _parts(self, ins, outs, scratch):
        (in_ref,), (out_ref,), (send_sems, recv_sems, local_sem) = ins, outs, scratch
        me = _place()

        def copy(k, block, to, part=None, src=None):
            dst = out_ref.at[_index(block)] if part is None else out_ref.at[_index(block), part]
            return pltpu.make_async_remote_copy(
                src_ref=dst if src is None else src, dst_ref=dst, send_sem=send_sems.at[k], recv_sem=recv_sems.at[k],
                device_id=to, device_id_type=MESH)

        mine = pltpu.make_async_copy(in_ref, out_ref.at[_index(me)], local_sem)
        first = [copy(0, me, _flip(me, 1), src=in_ref), copy(1, me, _flip(me, 4), src=in_ref),
                 copy(2, me, _flip(me, 2), src=in_ref)]
        return me, copy, mine, first

    def start(self, ins, outs, scratch):
        _, _, mine, first = self._parts(ins, outs, scratch)
        mine.start()
        for cp in first:
            cp.start()

    def wait(self, ins, outs, scratch):
        me, copy, mine, sent = self._parts(ins, outs, scratch)
        sibling, x_nbr, y_nbr, diagonal = _flip(me, 1), _flip(me, 4), _flip(me, 2), _flip(me, 6)
        top, bottom = pl.ds(0, self.half), pl.ds(self.half, self.half)
        arrivals = [(1, x_nbr, None, [(3, sibling, None), (5, y_nbr, top)]),
                    (2, y_nbr, None, [(4, sibling, None), (6, x_nbr, bottom)]),
                    (5, diagonal, top, [(7, sibling, top)]),
                    (6, diagonal, bottom, [(8, sibling, bottom)])]
        for k, block, part, onward in arrivals:
            copy(k, block, me, part).wait_recv()
            for k_on, to, part_on in onward:
                cp = copy(k_on, block, to, part_on)
                cp.start()
                sent.append(cp)
        other = lambda p: _flip(p, 1)
        for k, block, part in ((0, sibling, None), (3, other(x_nbr), None), (4, other(y_nbr), None),
                               (7, other(diagonal), top), (8, other(diagonal), bottom)):
            copy(k, block, me, part).wait_recv()
        for cp in sent:
            cp.wait_send()
        mine.wait()


class _Exchange:
    def __init__(self, tasks):
        self.tasks = tasks
        taken = [land for _, _, land, _, _ in tasks if land is not None]
        self.operands = [src for src, _, _, _, _ in tasks] + taken
        self.out_shape = [
            jax.ShapeDtypeStruct((N_CHIPS,) + ((2,) if by_core else ()) + (src.shape if same else src.shape[1:]), src.dtype)
            for src, _, _, same, by_core in tasks]
        self.aliases, n_taken = {}, 0
        for a, (_, _, land, _, _) in enumerate(tasks):
            if land is not None:
                self.aliases[len(tasks) + n_taken] = a
                n_taken += 1
        self.scratch = [pltpu.SemaphoreType.DMA((N_CHIPS,)), pltpu.SemaphoreType.DMA((N_CHIPS,)),
                        pltpu.SemaphoreType.DMA(())] * len(tasks)

    def _copies(self, ins, outs, scratch):
        x, y, core = _place()
        my = 2 * x + y
        for a, (_, chips, _, same, by_core) in enumerate(self.tasks):
            send_sems, recv_sems, local_sem = scratch[3 * a:3 * a + 3]
            slot = (lambda s, a=a, by_core=by_core: outs[a].at[s, core] if by_core else outs[a].at[s])
            for i, j in enumerate(chips):
                src = ins[a] if same else ins[a].at[i]
                pair = jnp.bitwise_xor(my, j)
                remote = pltpu.make_async_remote_copy(
                    src_ref=src, dst_ref=slot(my), send_sem=send_sems.at[pair], recv_sem=recv_sems.at[pair],
                    device_id=(j >> 1, j & 1, core), device_id_type=MESH)
                local = pltpu.make_async_copy(src, slot(my), local_sem)
                yield j, my, core, remote, local, slot, (send_sems, recv_sems)

    def start(self, ins, outs, scratch):
        for j, my, _, remote, local, _, _ in self._copies(ins, outs, scratch):
            pl.when(my != j)(remote.start)
            pl.when(my == j)(local.start)

    def wait(self, ins, outs, scratch):
        for j, my, core, remote, local, slot, (send_sems, recv_sems) in self._copies(ins, outs, scratch):
            pl.when(my != j)(remote.wait_send)

            @pl.when(my == j)
            def _():
                local.wait()
                for s in range(N_CHIPS):
                    if s != j:
                        pltpu.make_async_remote_copy(
                            src_ref=slot(s), dst_ref=slot(s), send_sem=send_sems.at[j ^ s], recv_sem=recv_sems.at[j ^ s],
                            device_id=(s >> 1, s & 1, core), device_id_type=MESH).wait_recv()


N_CHIPS = 4
ALL_CHIPS = tuple(range(N_CHIPS))


class _ToOtherCore:
    def __init__(self, parts):
        self.operands = list(parts)
        self.out_shape = [jax.ShapeDtypeStruct(p.shape[1:], p.dtype) for p in parts]
        self.aliases = {}
        self.scratch = [pltpu.SemaphoreType.DMA((len(parts),)), pltpu.SemaphoreType.DMA((len(parts),))]

    def _copies(self, ins, outs, scratch):
        send_sems, recv_sems = scratch
        me = _place()
        return [pltpu.make_async_remote_copy(src_ref=ins[a].at[1 - me[2]], dst_ref=outs[a], send_sem=send_sems.at[a],
                                             recv_sem=recv_sems.at[a], device_id=_flip(me, 1), device_id_type=MESH)
                for a in range(len(ins))]

    def start(self, ins, outs, scratch):
        for cp in self._copies(ins, outs, scratch):
            cp.start()

    def wait(self, ins, outs, scratch):
        for cp in self._copies(ins, outs, scratch):
            cp.wait()


def _to_other_core(parts, *, name):
    swap = _ToOtherCore(parts)
    n_arr = len(parts)
    hbm = pl.BlockSpec(memory_space=pl.ANY)

    def body(*refs):
        rider_refs = (refs[:n_arr], refs[n_arr:2 * n_arr], refs[2 * n_arr:])
        swap.start(*rider_refs)
        swap.wait(*rider_refs)

    return pl.pallas_call(
        body, name=name, in_specs=[hbm] * n_arr, out_specs=[hbm] * n_arr,
        out_shape=swap.out_shape, scratch_shapes=swap.scratch,
    )(*parts)


class _ShareWithOtherCore:
    def __init__(self, gathered):
        n_arr = len(gathered)
        self.operands = list(gathered)
        self.out_shape = [jax.ShapeDtypeStruct(g.shape, g.dtype) for g in gathered]
        self.aliases = {a: a for a in range(n_arr)}
        self.scratch = [pltpu.SemaphoreType.DMA((N_CHIPS * n_arr,)), pltpu.SemaphoreType.DMA((N_CHIPS * n_arr,))]

    def _copies(self, outs, scratch):
        send_sems, recv_sems = scratch
        me = _place()
        copies = []
        for a, buf in enumerate(outs):
            for j in range(N_CHIPS):
                block = buf.at[j, me[2]]
                copies.append(pltpu.make_async_remote_copy(
                    src_ref=block, dst_ref=block, send_sem=send_sems.at[N_CHIPS * a + j],
                    recv_sem=recv_sems.at[N_CHIPS * a + j], device_id=_flip(me, 1), device_id_type=MESH))
        return copies

    def start(self, ins, outs, scratch):
        for cp in self._copies(outs, scratch):
            cp.start()

    def wait(self, ins, outs, scratch):
        for cp in self._copies(outs, scratch):
            cp.wait()


def _pair_sum(mine, other, core, *, tr, name):
    _, n, rows, cols = mine.shape
    tr = min(tr, rows)

    def body(core_ref, a_ref, b_ref, o_ref):
        o_ref[...] = (a_ref[0].astype(F32) + b_ref[...].astype(F32)).astype(BF16)

    return pl.pallas_call(
        body, name=name,
        grid_spec=pltpu.PrefetchScalarGridSpec(
            num_scalar_prefetch=1, grid=(n, rows // tr),
            in_specs=[pl.BlockSpec((1, 1, tr, cols), lambda j, i, core_ref: (core_ref[0], j, i, 0)),
                      pl.BlockSpec((1, tr, cols), lambda j, i, core_ref: (j, i, 0))],
            out_specs=pl.BlockSpec((1, tr, cols), lambda j, i, core_ref: (j, i, 0))),
        out_shape=jax.ShapeDtypeStruct(other.shape, BF16),
        compiler_params=_params(2),
    )(core, mine, other)


def _gather_small(vec):
    def body(v_ref, out_ref, send_sems, recv_sems, local_sem):
        me = _place()

        def rows(p):
            return out_ref.at[pl.ds(pl.multiple_of(_index(p) * SMALL_ROWS, SMALL_ROWS), SMALL_ROWS), :]

        mine = pltpu.make_async_copy(v_ref, rows(me), local_sem)
        mine.start()
        sends = []
        for k in range(1, N_DEV):
            peer = _flip(me, k)
            cp = pltpu.make_async_remote_copy(src_ref=v_ref, dst_ref=rows(me), send_sem=send_sems.at[k - 1],
                                              recv_sem=recv_sems.at[k - 1], device_id=peer, device_id_type=MESH)
            cp.start()
            sends.append(cp)
        for k in range(1, N_DEV):
            peer = _flip(me, k)
            pltpu.make_async_remote_copy(src_ref=rows(peer), dst_ref=rows(peer), send_sem=send_sems.at[k - 1],
                                         recv_sem=recv_sems.at[k - 1], device_id=peer, device_id_type=MESH).wait_recv()
        for cp in sends:
            cp.wait_send()
        mine.wait()

    return pl.pallas_call(
        body, name="gather_small",
        in_specs=[pl.BlockSpec(memory_space=pltpu.VMEM)], out_specs=pl.BlockSpec(memory_space=pltpu.VMEM),
        out_shape=jax.ShapeDtypeStruct((N_DEV * SMALL_ROWS, SMALL_COLS), F32),
        scratch_shapes=[pltpu.SemaphoreType.DMA((7,)), pltpu.SemaphoreType.DMA((7,)), pltpu.SemaphoreType.DMA],
    )(vec)


def _w_in_nice(gathered):
    pieces, pos = [], 0
    for o0, width, n0 in sorted(_SEGMENTS, key=lambda seg: seg[2]):
        if n0 > pos:
            pieces.append(jnp.zeros((D_MODEL, n0 - pos), gathered.dtype))
        o = o0
        while o < o0 + width:
            d = o // SHARD_IN
            hi = min(o0 + width, (d + 1) * SHARD_IN)
            pieces.append(gathered[d][:, o - d * SHARD_IN:hi - d * SHARD_IN])
            o = hi
        pos = n0 + width
    pieces.append(jnp.zeros((D_MODEL, NP_IN - pos), gathered.dtype))
    return jnp.concatenate(pieces, axis=1)


def _w_in_blocks(chips, dw_lat, dw_rest):
    blocks = []
    for core in range(2):
        for chip in chips:
            lo = (2 * chip + core) * SHARD_IN
            runs = []
            for o0, width, n0 in _SEGMENTS:
                a, b = max(lo, o0), min(lo + SHARD_IN, o0 + width)
                if a < b:
                    n_a, n_b = n0 + a - o0, n0 + b - o0
                    runs.append(dw_lat[:, n_a:n_b] if n_b <= LAT_W else dw_rest[:, n_a - LAT_W:n_b - LAT_W])
            blocks.append(jnp.concatenate(runs, axis=1))
    return jnp.stack(blocks).reshape(2, len(chips), D_MODEL, SHARD_IN)


def _by_core(shards):
    return shards.reshape((N_CHIPS, 2) + shards.shape[1:]).swapaxes(0, 1)


EARLY_CHIPS = (1, 2)
LATE_CHIPS = (0, 3)


def _w_uq_nice(shard):
    z = jnp.zeros((Q_RANK, 32), shard.dtype)
    return jnp.concatenate([shard[:, :128], shard[:, 128:160], z, shard[:, 160:192], z], axis=1)


def _pack_small(g_pre, g_post, g_q, g_kv, b_f, extra=None):
    parts = [g_pre.reshape(-1), g_post.reshape(-1), g_q.reshape(-1), g_kv.reshape(-1), b_f.reshape(-1)]
    if extra is not None:
        parts.append(extra.reshape(-1))
    flat = jnp.concatenate(parts)
    flat = jnp.concatenate([flat, jnp.zeros((SMALL_ROWS * SMALL_COLS - flat.shape[0],), F32)])
    return flat.reshape(SMALL_ROWS, SMALL_COLS)


def _unpack_small(packed):
    flat = packed.reshape(-1)
    o = 0
    out = []
    for n in (D_MODEL, D_MODEL, Q_RANK, KV_RANK, N_HEADS):
        out.append(flat[o:o + n].reshape(1, n))
        o += n
    return out, flat[o]


def kernel(x, positions, g_pre, w_in, g_q_latent, w_uq, g_kv_latent, w_ukv, b_forget, w_out, g_post, loss_target, m_g_pre, m_w_in, m_g_q_latent, m_w_uq, m_g_kv_latent, m_w_ukv, m_b_forget, m_w_out, m_g_post, v_g_pre, v_w_in, v_g_q_latent, v_w_uq, v_g_kv_latent, v_w_ukv, v_b_forget, v_w_out, v_g_post):
    t = x.shape[1]
    tb = min(512, t)
    tm = min(256, t)
    nb = t // tb
    x2 = x.reshape(t, D_MODEL)
    target = loss_target.reshape(t, D_MODEL)
    pos_col = positions.reshape(t, 1).astype(F32)
    bf_row = jnp.concatenate([b_forget.reshape(1, N_HEADS), jnp.zeros((1, LANE - N_HEADS), F32)], axis=1)

    h, h_t, g_in = _prenorm(x2, g_pre, _AllGather(w_in[0].astype(BF16)), tm=tm)
    w_in_n = _w_in_nice(g_in)
    gather_rest = _Exchange([(w, ALL_CHIPS, None, True, True) for w in
                             (_w_uq_nice(w_uq[0].astype(BF16)), w_ukv[0].astype(BF16), w_out[0].astype(BF16))])
    core = lax.axis_index("c").astype(jnp.int32).reshape(1)
    proj, g_uq, g_ukv, g_out, flog = _mm(h, w_in_n, name="proj_in", out_dtype=BF16, tm=2048, tn=512, tk=2048,
                                         rider=gather_rest, f32_cols=(FL0, LANE))
    qn, kvn, kr, cos_t, sin_t, c, qn_t, kvn_t, g_uq, g_ukv, g_out = _mid_fwd(
        proj, flog, g_q_latent, g_kv_latent, bf_row, pos_col, _ShareWithOtherCore([g_uq, g_ukv, g_out]), tm=tm)
    w_uq_n = g_uq.reshape(N_DEV, Q_RANK, 256).transpose(1, 0, 2).reshape(Q_RANK, N_HEADS * 256)
    w_ukv_n = g_ukv.reshape(N_DEV, KV_RANK, 256).transpose(1, 0, 2).reshape(KV_RANK, N_HEADS * 256)
    w_out_n = g_out.reshape(D_MODEL, D_MODEL)
    q_full = _q_up_rope(qn, w_uq_n, cos_t, sin_t, tm=min(1024, t))
    kv = _mm(kvn, w_ukv_n, name="kv_up", out_dtype=BF16, tm=1024, tn=1024, tk=KV_RANK)
    c_heads = c[:, :N_HEADS].T
    c_col = c_heads.reshape(N_HEADS, t, 1)
    c_row4 = c_heads.reshape(N_HEADS, nb, 1, tb)
    o_all, og_all, og_t, lse4_mla = _attn_fwd(False, (q_full, kv, kr, proj), t=t, tb=tb, name="mla_fwd")
    o_all, og_all, og_t, lse4_fox = _attn_fwd(True, (proj, c_col, c_row4, o_all, og_all, og_t), t=t, tb=tb,
                                              name="fox_fwd")
    dy, d_o_post, dg_post, loss_part = _out_norm_loss(og_all, w_out_n, x2, target, g_post, tm=min(512, t))

    dw_out = _mm(og_t, d_o_post, name="dw_out", out_dtype=BF16, tm=1024, tn=1024, tk=2048)
    p_out = _by_core(dw_out.reshape(N_DEV, D_MODEL // N_DEV, D_MODEL))
    d_attn, dproj, delta, o_out = _dog_gate(d_o_post, w_out_n, o_all, proj, _ToOtherCore([p_out]), tm=min(1024, t))
    s_out = _pair_sum(p_out, o_out, core, tr=256, name="dw_out_pair_sum")
    delta4 = delta[:, :2 * N_HEADS].T.reshape(2 * N_HEADS, nb, 1, tb)
    dproj, dck, dcq, l_out = _attn_bwd(True, (proj, d_attn, lse4_fox, delta4[N_HEADS:], c_row4, c_col, dproj),
                                       t=t, tb=tb, name="fox_bwd",
                                       rider=_Exchange([(s_out, ALL_CHIPS, None, False, False)]))
    dw_in_rest = _mm(h_t, dproj, name="dw_in_rest", out_dtype=BF16, tm=2048, tn=512, tk=2048,
                     b_cols=(LAT_W, NP_IN - LAT_W))
    p_in = _w_in_blocks(EARLY_CHIPS, None, dw_in_rest)
    (o_in,) = _to_other_core([p_in], name="dw_in_early_to_core")
    s_in = _pair_sum(p_in, o_in, core, tr=256, name="dw_in_early_pair_sum")
    dq_full, dkv, dkr, l_in = _attn_bwd(False, (q_full, kv, kr, d_attn, lse4_mla, delta4[:N_HEADS]),
                                        t=t, tb=tb, name="mla_bwd",
                                        rider=_Exchange([(s_in, EARLY_CHIPS, None, False, False)]))
    dcq_rows = jnp.concatenate([dcq.reshape(N_HEADS, t).T, jnp.zeros((t, LANE - N_HEADS), F32)], axis=1)
    dq2, dkr_raw, dfl, dbf = _mid_bwd(dq_full, dkr, cos_t, sin_t, dcq_rows, dck, flog, bf_row, tm=tm)
    dqn = _mm(dq2, w_uq_n, name="d_qn", nt=True, out_dtype=F32, tm=1024, tn=Q_RANK, tk=2048)
    dkvn = _mm(dkv, w_ukv_n, name="d_kvn", nt=True, out_dtype=F32, tm=1024, tn=KV_RANK, tk=2048)
    dw_uq = _mm(qn_t, dq2, name="dw_uq", out_dtype=BF16, tm=Q_RANK, tn=1024, tk=2048)
    dw_ukv = _mm(kvn_t, dkv, name="dw_ukv", out_dtype=BF16, tm=KV_RANK, tn=1024, tk=2048)
    dw_uq_h = dw_uq.reshape(Q_RANK, N_HEADS, 256)
    s_uq = jnp.concatenate([dw_uq_h[:, :, :160], dw_uq_h[:, :, 192:224]], axis=2).transpose(1, 0, 2)
    s_ukv = dw_ukv.reshape(KV_RANK, N_HEADS, 256).transpose(1, 0, 2)
    up_parts = [_by_core(s_uq), _by_core(s_ukv)]
    dproj, dg_q, dg_kv, *up_other = _norm_bwd(proj, dqn, dkvn, g_q_latent, g_kv_latent, dkr_raw, dfl, dproj,
                                              _ToOtherCore(up_parts), tm=tm)
    up_sums = [_pair_sum(p, o_, core, tr=256, name=f"dw_up_pair_sum_{i}") for i, (p, o_) in enumerate(zip(up_parts, up_other))]
    dw_in_lat, l_uq, l_ukv = _mm(h_t, dproj, name="dw_in_lat", out_dtype=BF16, tm=1024, tn=LAT_W, tk=2048,
                                 b_cols=(0, LAT_W), rider=_Exchange([(s, ALL_CHIPS, None, False, False) for s in up_sums]))
    p_late = _w_in_blocks(LATE_CHIPS, dw_in_lat, dw_in_rest)
    (o_late,) = _to_other_core([p_late], name="dw_in_late_to_core")
    s_late = _pair_sum(p_late, o_late, core, tr=256, name="dw_in_late_pair_sum")
    dh, l_in = _mm(dproj, w_in_n, name="d_h", nt=True, out_dtype=F32, tm=2048, tn=1024, tk=NP_IN // 4,
                   rider=_Exchange([(s_late, LATE_CHIPS, l_in, False, False)]))
    grad_x, dg_pre = _prenorm_bwd(dh, x2, g_pre, dy, tm=min(512, t))

    small = _gather_small(_pack_small(dg_pre, dg_post, dg_q, dg_kv, dbf[:, :N_HEADS], loss_part))

    res_in = _adamw(l_in, w_in[0], m_w_in[0], v_w_in[0], tr=256, name="adamw_w_in")
    res_uq = _adamw(l_uq, w_uq[0], m_w_uq[0], v_w_uq[0], tr=256, name="adamw_w_uq")
    res_ukv = _adamw(l_ukv, w_ukv[0], m_w_ukv[0], v_w_ukv[0], tr=256, name="adamw_w_ukv")
    res_out = _adamw(l_out, w_out[0], m_w_out[0], v_w_out[0], tr=128, name="adamw_w_out")
    res_small = _adamw_small(
        small,
        _pack_small(g_pre, g_post, g_q_latent, g_kv_latent, b_forget),
        _pack_small(m_g_pre, m_g_post, m_g_q_latent, m_g_kv_latent, m_b_forget),
        _pack_small(v_g_pre, v_g_post, v_g_q_latent, v_g_kv_latent, v_b_forget))
    small_out = [_unpack_small(r) for r in res_small]
    loss = small_out[0][1]

    def leaves(kind):
        (s_pre, s_post, s_q, s_kv, s_bf), _ = small_out[kind]
        return [s_pre, res_in[kind][None], s_q, res_uq[kind][None], s_kv, res_ukv[kind][None], s_bf,
                res_out[kind][None], s_post]

    return (loss, grad_x.reshape(x.shape), *leaves(0), *leaves(1), *leaves(2), *leaves(3))
```

```python
import numpy as np
import jax
import jax.numpy as jnp
from jax import lax
from jax.experimental import pallas as pl
from jax.experimental.pallas import tpu as pltpu

F32 = jnp.float32
BF16 = jnp.bfloat16
MESH = pl.DeviceIdType.MESH

N_DEV = 8
D_MODEL = 2048
N_HEADS = 8
HEAD = 128
Q_RANK = 768
KV_RANK = 512
ROPE = 64
D_IN = 6472
SHARD_IN = D_IN // N_DEV
NORM_EPS = 1e-6
ROPE_THETA = 10000.0
MLA_SCALE = (HEAD + ROPE) ** -0.5
FOX_SCALE = HEAD ** -0.5

QL0, KR0, FL0, KVL0, GM0, GF0, FQ0, FK0, FV0, NP_IN = 0, 768, 896, 1024, 1536, 2560, 3584, 4608, 5632, 6656
LAT_W = GM0
LANE = 128
_SEGMENTS = ((0, 768, QL0), (768, 512, KVL0), (1280, 32, KR0), (1312, 32, KR0 + 64), (1344, 1024, GM0),
             (2368, 3072, FQ0), (5440, 8, FL0), (5448, 1024, GF0))
LOG2E = 1.4426950408889634

ADAM_LR = 0.001
ADAM_B1 = 0.9
ADAM_B2 = 0.999
ADAM_EPS = 1e-08
ADAM_WD = 0.01
ADAM_STEP = 10

VMEM_LIMIT_BYTES = 56 * 1024 * 1024
SMALL_ROWS, SMALL_COLS = 8, 768


def _params(n_grid=0):
    return pltpu.CompilerParams(vmem_limit_bytes=VMEM_LIMIT_BYTES,
                                dimension_semantics=("arbitrary",) * n_grid if n_grid else None)


def _sigmoid(z):
    return 1.0 / (1.0 + jnp.exp(-z))


def _split3(v):
    a = v.astype(BF16)
    r = v - a.astype(F32)
    b = r.astype(BF16)
    c = (r - b.astype(F32)).astype(BF16)
    return a, b, c


def _mm(a, b, *, name, nt=False, out_dtype=F32, tm=1024, tn=512, tk=2048, b_cols=None, rider=None, f32_cols=None):
    m, k_dim = a.shape
    n = b.shape[0] if nt else b.shape[1]
    col0 = 0
    if b_cols is not None:
        assert not nt
        col0, n = b_cols
    assert (b.shape[1] if nt else b.shape[0]) == k_dim
    tm, tn, tk = min(tm, m), min(tn, n), min(tk, k_dim)
    assert m % tm == 0 and n % tn == 0 and k_dim % tk == 0 and col0 % tn == 0, (name, a.shape, b.shape)
    nk = k_dim // tk
    j0 = col0 // tn
    grid = (m // tm, n // tn, nk)
    dims = (((1,), (1 if nt else 0,)), ((), ()))
    n_rin = len(rider.operands) if rider else 0
    n_rout = len(rider.out_shape) if rider else 0
    n_x = 0 if f32_cols is None else 1
    if n_x:
        x_tile, x_lo = divmod(f32_cols[0] - col0, tn)
        x_w = f32_cols[1]
        assert x_lo + x_w <= tn and x_lo % LANE == 0 and x_w % LANE == 0

    def body(*refs):
        a_ref, b_ref = refs[:2]
        o_ref = refs[2 + n_rin]
        acc_ref = refs[3 + n_rin + n_rout + n_x]
        i, j, k = pl.program_id(0), pl.program_id(1), pl.program_id(2)
        if rider:
            rider_refs = (refs[2:2 + n_rin], refs[3 + n_rin:3 + n_rin + n_rout], refs[4 + n_rin + n_rout + n_x:])

            @pl.when(jnp.logical_and(i == 0, jnp.logical_and(j == 0, k == 0)))
            def _():
                rider.start(*rider_refs)

        @pl.when(k == 0)
        def _():
            acc_ref[...] = jnp.zeros_like(acc_ref)

        acc_ref[...] += lax.dot_general(a_ref[...], b_ref[...], dims, preferred_element_type=F32)

        @pl.when(k == nk - 1)
        def _():
            o_ref[...] = acc_ref[...].astype(o_ref.dtype)

        if n_x:
            @pl.when(jnp.logical_and(k == nk - 1, j == x_tile))
            def _():
                refs[3 + n_rin + n_rout][...] = acc_ref[:, x_lo:x_lo + x_w]

        if rider:
            @pl.when(jnp.logical_and(i == grid[0] - 1, jnp.logical_and(j == grid[1] - 1, k == nk - 1)))
            def _():
                rider.wait(*rider_refs)

    b_spec = (pl.BlockSpec((tn, tk), lambda i, j, k: (j, k)) if nt
              else pl.BlockSpec((tk, tn), lambda i, j, k: (k, j0 + j)))
    a_spec = pl.BlockSpec((tm, tk), lambda i, j, k: (i, k))
    any_spec = pl.BlockSpec(memory_space=pl.ANY)
    out = pl.pallas_call(
        body, name=name, grid=grid,
        in_specs=[a_spec, b_spec] + [any_spec] * n_rin,
        out_specs=[pl.BlockSpec((tm, tn), lambda i, j, k: (i, j))] + [any_spec] * n_rout
        + ([pl.BlockSpec((tm, x_w), lambda i, j, k: (i, 0))] if n_x else []),
        out_shape=[jax.ShapeDtypeStruct((m, n), out_dtype)] + (list(rider.out_shape) if rider else [])
        + ([jax.ShapeDtypeStruct((m, x_w), F32)] if n_x else []),
        scratch_shapes=[pltpu.VMEM((tm, tn), F32)] + (list(rider.scratch) if rider else []),
        input_output_aliases={2 + i_in: 1 + i_out for i_in, i_out in rider.aliases.items()} if rider else {},
        compiler_params=_params(3),
    )(a, b, *(rider.operands if rider else ()))
    return out if rider or n_x else out[0]


def _prenorm(x, g, rider, *, tm):
    t = x.shape[0]
    n_steps = t // tm
    n_rin, n_rout = len(rider.operands), len(rider.out_shape)

    def body(*refs):
        x_ref, g_ref = refs[:2]
        h_ref, ht_ref = refs[2 + n_rin:4 + n_rin]
        rider_refs = (refs[2:2 + n_rin], refs[4 + n_rin:4 + n_rin + n_rout], refs[4 + n_rin + n_rout:])
        i = pl.program_id(0)
        pl.when(i == 0)(lambda: rider.start(*rider_refs))
        xv = x_ref[...]
        r = lax.rsqrt(jnp.mean(xv * xv, axis=-1, keepdims=True) + NORM_EPS)
        h = xv * r * g_ref[...]
        h_ref[...] = h.astype(BF16)
        ht_ref[...] = h.T.astype(BF16)
        pl.when(i == n_steps - 1)(lambda: rider.wait(*rider_refs))

    any_spec = pl.BlockSpec(memory_space=pl.ANY)
    return pl.pallas_call(
        body, name="prenorm", grid=(n_steps,),
        in_specs=[pl.BlockSpec((tm, D_MODEL), lambda i: (i, 0)), pl.BlockSpec((1, D_MODEL), lambda i: (0, 0))]
        + [any_spec] * n_rin,
        out_specs=[pl.BlockSpec((tm, D_MODEL), lambda i: (i, 0)), pl.BlockSpec((D_MODEL, tm), lambda i: (0, i))]
        + [any_spec] * n_rout,
        out_shape=[jax.ShapeDtypeStruct((t, D_MODEL), BF16), jax.ShapeDtypeStruct((D_MODEL, t), BF16)]
        + list(rider.out_shape),
        scratch_shapes=list(rider.scratch),
        compiler_params=_params(1),
    )(x, g, *rider.operands)


def _rope_rows():
    inv = (np.float32(ROPE_THETA) ** (-np.arange(0, ROPE, 2, dtype=np.float32) / np.float32(ROPE))).astype(np.float32)
    invf = np.zeros((1, LANE), np.float32)
    sgn = np.zeros((1, LANE), np.float32)
    invf[0, 0:32] = inv
    invf[0, 64:96] = inv
    sgn[0, 0:32] = -1.0
    sgn[0, 64:96] = 1.0
    return jnp.asarray(invf), jnp.asarray(sgn)


def _rot(v, cos_t, sin_t):
    return v * cos_t + pltpu.roll(v, 64, 1) * sin_t


def _rot_bwd(dv, cos_t, sin_t):
    return dv * cos_t + pltpu.roll(dv * sin_t, 64, 1)


def _mid_fwd(proj, flog, g_q, g_kv, bf_row, pos_col, rider, *, tm):
    t = proj.shape[0]
    n_steps = t // tm
    invf, sgn = _rope_rows()
    n_rin, n_rout = len(rider.operands), len(rider.out_shape)

    def body(*refs):
        p_ref, fl_ref, gq_ref, gkv_ref, bf_ref, pos_ref, invf_ref, sgn_ref = refs[:8]
        qn_ref, kvn_ref, kr_ref, cos_ref, sin_ref, c_ref, qnt_ref, kvnt_ref = refs[8 + n_rin:16 + n_rin]
        carry_ref = refs[16 + n_rin + n_rout]
        rider_refs = (refs[8:8 + n_rin], refs[16 + n_rin:16 + n_rin + n_rout], refs[17 + n_rin + n_rout:])
        i = pl.program_id(0)
        pl.when(i == 0)(lambda: rider.start(*rider_refs))
        pl.when(i == n_steps - 1)(lambda: rider.wait(*rider_refs))

        @pl.when(i == 0)
        def _():
            carry_ref[...] = jnp.zeros_like(carry_ref)

        ql = p_ref[:, QL0:QL0 + Q_RANK].astype(F32)
        r = lax.rsqrt(jnp.mean(ql * ql, axis=-1, keepdims=True) + NORM_EPS)
        qn = ql * r * gq_ref[...]
        qn_ref[...] = qn.astype(BF16)
        qnt_ref[...] = qn.T.astype(BF16)
        kvl = p_ref[:, KVL0:KVL0 + KV_RANK].astype(F32)
        r = lax.rsqrt(jnp.mean(kvl * kvl, axis=-1, keepdims=True) + NORM_EPS)
        kvn = kvl * r * gkv_ref[...]
        kvn_ref[...] = kvn.astype(BF16)
        kvnt_ref[...] = kvn.T.astype(BF16)

        ang = pos_ref[...] * invf_ref[...]
        cos_t = jnp.cos(ang)
        sin_t = jnp.sin(ang) * sgn_ref[...]
        cos_ref[...] = cos_t
        sin_ref[...] = sin_t
        kr_ref[...] = _rot(p_ref[:, KR0:KR0 + LANE].astype(F32), cos_t, sin_t).astype(BF16)

        z = fl_ref[...] + bf_ref[...]
        logf = jnp.minimum(z, 0.0) - jnp.log(1.0 + jnp.exp(-jnp.abs(z)))
        row = lax.broadcasted_iota(jnp.int32, (tm, tm), 0)
        col = lax.broadcasted_iota(jnp.int32, (tm, tm), 1)
        tri = (col <= row).astype(BF16)
        acc = carry_ref[0:1, :]
        for part in _split3(logf):
            acc = acc + jnp.dot(tri, part, preferred_element_type=F32)
        c_ref[...] = acc * (1.0 / FOX_SCALE)
        carry_ref[0:1, :] = carry_ref[0:1, :] + jnp.sum(logf, axis=0, keepdims=True)

    row_spec = lambda w: pl.BlockSpec((tm, w), lambda i: (i, 0))
    vec_spec = lambda w: pl.BlockSpec((1, w), lambda i: (0, 0))
    return pl.pallas_call(
        body, name="mid_fwd", grid=(n_steps,),
        in_specs=[row_spec(LAT_W), row_spec(LANE), vec_spec(Q_RANK), vec_spec(KV_RANK), vec_spec(LANE),
                  pl.BlockSpec((tm, 1), lambda i: (i, 0)), vec_spec(LANE), vec_spec(LANE)]
        + [pl.BlockSpec(memory_space=pl.ANY)] * n_rin,
        out_specs=[row_spec(Q_RANK), row_spec(KV_RANK), row_spec(LANE), row_spec(LANE), row_spec(LANE), row_spec(LANE),
                   pl.BlockSpec((Q_RANK, tm), lambda i: (0, i)), pl.BlockSpec((KV_RANK, tm), lambda i: (0, i))]
        + [pl.BlockSpec(memory_space=pl.ANY)] * n_rout,
        out_shape=[jax.ShapeDtypeStruct((t, Q_RANK), BF16), jax.ShapeDtypeStruct((t, KV_RANK), BF16),
                   jax.ShapeDtypeStruct((t, LANE), BF16), jax.ShapeDtypeStruct((t, LANE), F32),
                   jax.ShapeDtypeStruct((t, LANE), F32), jax.ShapeDtypeStruct((t, LANE), F32),
                   jax.ShapeDtypeStruct((Q_RANK, t), BF16), jax.ShapeDtypeStruct((KV_RANK, t), BF16)]
        + list(rider.out_shape),
        scratch_shapes=[pltpu.VMEM((8, LANE), F32)] + list(rider.scratch),
        input_output_aliases={8 + i_in: 8 + i_out for i_in, i_out in rider.aliases.items()},
        compiler_params=_params(1),
    )(proj, flog, g_q, g_kv, bf_row, pos_col, invf, sgn, *rider.operands)


def _q_up_rope(qn, w_uq_n, cos_t, sin_t, *, tm):
    t = qn.shape[0]
    tn = 2 * 256

    def body(a_ref, b_ref, cos_ref, sin_ref, o_ref):
        q = jnp.dot(a_ref[...], b_ref[...], preferred_element_type=F32)
        c, s = cos_ref[...], sin_ref[...]
        for u in range(tn // 256):
            o_ref[:, 256 * u:256 * u + 128] = q[:, 256 * u:256 * u + 128].astype(BF16)
            o_ref[:, 256 * u + 128:256 * u + 256] = _rot(q[:, 256 * u + 128:256 * u + 256], c, s).astype(BF16)

    return pl.pallas_call(
        body, name="q_up_rope", grid=(t // tm, N_HEADS * 256 // tn),
        in_specs=[pl.BlockSpec((tm, Q_RANK), lambda i, j: (i, 0)), pl.BlockSpec((Q_RANK, tn), lambda i, j: (0, j)),
                  pl.BlockSpec((tm, LANE), lambda i, j: (i, 0)), pl.BlockSpec((tm, LANE), lambda i, j: (i, 0))],
        out_specs=pl.BlockSpec((tm, tn), lambda i, j: (i, j)),
        out_shape=jax.ShapeDtypeStruct((t, N_HEADS * 256), BF16),
        compiler_params=_params(2),
    )(qn, w_uq_n, cos_t, sin_t)


def _attn_fwd(fox, operands, *, t, tb, name):
    nb = t // tb
    scale = FOX_SCALE if fox else MLA_SCALE
    exp2_scale = scale * LOG2E
    pair = 2 * HEAD
    pair0 = N_HEADS // 2 if fox else 0
    q_w = HEAD if fox else 2 * HEAD
    nt_dims = (((1,), (1,)), ((), ()))
    tn_dims = (((0,), (0,)), ((), ()))

    def body(*refs):
        if fox:
            (q_ref, k_ref, v_ref, gate_ref, cq_ref, ck_ref, _, _, _,
             o_ref, og_ref, ogt_ref, lse_ref, m_s, l_s, acc_s) = refs
        else:
            q_ref, kv_ref, kr_ref, gate_ref, o_ref, og_ref, ogt_ref, lse_ref, m_s, l_s, acc_s = refs
        qi = pl.program_id(1)
        m_s[...] = jnp.full_like(m_s, -jnp.inf)
        l_s[...] = jnp.zeros_like(l_s)
        acc_s[...] = jnp.zeros_like(acc_s)

        def chunk(kc, masked):
            off = pl.multiple_of(kc * tb, tb)
            scores = []
            for u in range(2):
                q = q_ref[:, q_w * u:q_w * (u + 1)]
                if fox:
                    kk = k_ref[pl.ds(off, tb), HEAD * u:HEAD * (u + 1)]
                else:
                    kk = jnp.concatenate([kv_ref[pl.ds(off, tb), pair * u:pair * u + HEAD],
                                          kr_ref[pl.ds(off, tb), :]], axis=1)
                s = lax.dot_general(kk, q, nt_dims, preferred_element_type=F32)
                if fox:
                    s = s + cq_ref[u, 0] - ck_ref[u, pl.ds(off, tb), :]
                if masked:
                    row = lax.broadcasted_iota(jnp.int32, (tb, tb), 0)
                    col = lax.broadcasted_iota(jnp.int32, (tb, tb), 1)
                    s = jnp.where(row <= col, s, -jnp.inf)
                scores.append(s)
            for u in range(2):
                s = scores[u]
                m_prev = m_s[u]
                m_new = jnp.maximum(m_prev, jnp.max(s, axis=0, keepdims=True))
                alpha = jnp.exp2((m_prev - m_new) * exp2_scale)
                p = jnp.exp2((s - m_new) * exp2_scale)
                l_s[u] = alpha * l_s[u] + jnp.sum(p, axis=0, keepdims=True)
                if fox:
                    vv = v_ref[pl.ds(off, tb), HEAD * u:HEAD * (u + 1)]
                else:
                    vv = kv_ref[pl.ds(off, tb), pair * u + HEAD:pair * (u + 1)]
                acc_s[u] = alpha * acc_s[u] + lax.dot_general(vv, p.astype(BF16), tn_dims,
                                                              preferred_element_type=F32)
                m_s[u] = m_new

        def loop_body(kc, carry):
            chunk(kc, False)
            return carry

        lax.fori_loop(0, qi, loop_body, 0)
        chunk(qi, True)
        for u in range(2):
            cols = slice(HEAD * u, HEAD * (u + 1))
            o_t = acc_s[u] / l_s[u]
            o = o_t.T
            o_ref[:, cols] = o
            g = gate_ref[:, cols].astype(F32)
            silu = g * _sigmoid(g)
            og_ref[:, cols] = (o * silu).astype(BF16)
            ogt_ref[cols, :] = (o_t * silu.T).astype(BF16)
            lse_ref[u, 0] = m_s[u] * scale + jnp.log(l_s[u])

    any_spec = pl.BlockSpec(memory_space=pl.ANY)
    row_stat = pl.BlockSpec((2, 1, 1, tb), lambda g, i: (g, i, 0, 0))
    if fox:
        proj, c_col, c_row4, o_all, og_all, ogt_all = operands
        ins = [proj, proj, proj, proj, c_row4, c_col, o_all, og_all, ogt_all]
        in_specs = [pl.BlockSpec((tb, pair), lambda g, i: (i, FQ0 // pair + g)),
                    pl.BlockSpec((t, pair), lambda g, i: (0, FK0 // pair + g)),
                    pl.BlockSpec((t, pair), lambda g, i: (0, FV0 // pair + g)),
                    pl.BlockSpec((tb, pair), lambda g, i: (i, GF0 // pair + g)),
                    row_stat, pl.BlockSpec((2, t, 1), lambda g, i: (g, 0, 0)), any_spec, any_spec, any_spec]
        aliases = {6: 0, 7: 1, 8: 2}
    else:
        q_full, kv, kr, proj = operands
        ins = [q_full, kv, kr, proj]
        in_specs = [pl.BlockSpec((tb, 2 * pair), lambda g, i: (i, g)),
                    pl.BlockSpec((t, 2 * pair), lambda g, i: (0, g)),
                    pl.BlockSpec((t, HEAD), lambda g, i: (0, 0)),
                    pl.BlockSpec((tb, pair), lambda g, i: (i, GM0 // pair + g))]
        aliases = {}
    return pl.pallas_call(
        body, name=name, grid=(N_HEADS // 2, nb), in_specs=in_specs,
        out_specs=[pl.BlockSpec((tb, pair), lambda g, i: (i, pair0 + g)),
                   pl.BlockSpec((tb, pair), lambda g, i: (i, pair0 + g)),
                   pl.BlockSpec((pair, tb), lambda g, i: (pair0 + g, i)), row_stat],
        out_shape=[jax.ShapeDtypeStruct((t, 2 * N_HEADS * HEAD), F32), jax.ShapeDtypeStruct((t, 2 * N_HEADS * HEAD), BF16),
                   jax.ShapeDtypeStruct((2 * N_HEADS * HEAD, t), BF16), jax.ShapeDtypeStruct((N_HEADS, nb, 1, tb), F32)],
        scratch_shapes=[pltpu.VMEM((2, 1, tb), F32), pltpu.VMEM((2, 1, tb), F32), pltpu.VMEM((2, HEAD, tb), F32)],
        input_output_aliases=aliases,
        compiler_params=_params(2),
    )(*ins)


def _out_norm_loss(og, w_out_n, x, target, g, *, tm):
    t = og.shape[0]

    def body(og_ref, w_ref, x_ref, t_ref, g_ref, dy_ref, do_ref, dg_ref, loss_ref):
        i = pl.program_id(0)

        @pl.when(i == 0)
        def _():
            dg_ref[...] = jnp.zeros_like(dg_ref)
            loss_ref[...] = jnp.zeros_like(loss_ref)

        ov = jnp.dot(og_ref[...], w_ref[...], preferred_element_type=F32)
        gv = g_ref[...]
        r = lax.rsqrt(jnp.mean(ov * ov, axis=-1, keepdims=True) + NORM_EPS)
        oh = ov * r
        e = x_ref[...] + oh * gv - t_ref[...]
        loss_ref[...] += 0.5 * jnp.sum(jnp.mean(e * e, axis=-1, keepdims=True), axis=0, keepdims=True)
        dy = e * (1.0 / D_MODEL)
        dy_ref[...] = dy
        dyg = dy * gv
        do_ref[...] = (r * (dyg - oh * jnp.mean(dyg * oh, axis=-1, keepdims=True))).astype(BF16)
        dg_ref[...] += jnp.sum(dy * oh, axis=0, keepdims=True)

    row = pl.BlockSpec((tm, D_MODEL), lambda i: (i, 0))
    vec = pl.BlockSpec((1, D_MODEL), lambda i: (0, 0))
    whole_w = pl.BlockSpec((D_MODEL, D_MODEL), lambda i: (0, 0), pipeline_mode=pl.Buffered(1))
    return pl.pallas_call(
        body, name="out_norm_loss", grid=(t // tm,),
        in_specs=[row, whole_w, row, row, vec],
        out_specs=[row, row, vec, pl.BlockSpec((1, 1), lambda i: (0, 0))],
        out_shape=[jax.ShapeDtypeStruct((t, D_MODEL), F32), jax.ShapeDtypeStruct((t, D_MODEL), BF16),
                   jax.ShapeDtypeStruct((1, D_MODEL), F32), jax.ShapeDtypeStruct((1, 1), F32)],
        compiler_params=_params(1),
    )(og, w_out_n, x, target, g)


def _dog_gate(d_o_post, w_out_n, o_all, proj, rider, *, tm):
    t = d_o_post.shape[0]
    n_group = 4
    pair = n_group * HEAD
    gate_blk = GM0 // pair
    assert GM0 % pair == 0 and GF0 == GM0 + N_HEADS * HEAD
    grid = (t // tm, 2 * N_HEADS // n_group)
    n_rin, n_rout = len(rider.operands), len(rider.out_shape)

    def body(*refs):
        do_ref, w_ref, o_ref, p_ref = refs[:4]
        dattn_ref, dproj_ref, delta_ref = refs[4 + n_rin:7 + n_rin]
        rider_refs = (refs[4:4 + n_rin], refs[7 + n_rin:7 + n_rin + n_rout], refs[7 + n_rin + n_rout:])
        i, j = pl.program_id(0), pl.program_id(1)
        pl.when(jnp.logical_and(i == 0, j == 0))(lambda: rider.start(*rider_refs))

        @pl.when(j == 0)
        def _():
            delta_ref[...] = jnp.zeros_like(delta_ref)

        dog = lax.dot_general(do_ref[...], w_ref[...], (((1,), (1,)), ((), ())), preferred_element_type=F32)
        g = p_ref[...].astype(F32)
        ov = o_ref[...]
        sg = _sigmoid(g)
        d_o = dog * (g * sg)
        dattn_ref[...] = d_o.astype(BF16)
        dproj_ref[...] = (dog * ov * (sg * (1.0 + g * (1.0 - sg)))).astype(BF16)
        prod = d_o * ov
        lane = lax.broadcasted_iota(jnp.int32, (tm, LANE), 1)
        delta = delta_ref[...]
        for u in range(n_group):
            part = jnp.sum(prod[:, HEAD * u:HEAD * (u + 1)], axis=-1, keepdims=True)
            delta = jnp.where(lane == n_group * j + u, part, delta)
        delta_ref[...] = delta
        pl.when(jnp.logical_and(i == grid[0] - 1, j == grid[1] - 1))(lambda: rider.wait(*rider_refs))

    any_spec = pl.BlockSpec(memory_space=pl.ANY)
    return pl.pallas_call(
        body, name="dog_gate", grid=grid,
        in_specs=[pl.BlockSpec((tm, D_MODEL), lambda i, j: (i, 0)), pl.BlockSpec((pair, D_MODEL), lambda i, j: (j, 0)),
                  pl.BlockSpec((tm, pair), lambda i, j: (i, j)), pl.BlockSpec((tm, pair), lambda i, j: (i, gate_blk + j))]
        + [any_spec] * n_rin,
        out_specs=[pl.BlockSpec((tm, pair), lambda i, j: (i, j)), pl.BlockSpec((tm, pair), lambda i, j: (i, gate_blk + j)),
                   pl.BlockSpec((tm, LANE), lambda i, j: (i, 0))] + [any_spec] * n_rout,
        out_shape=[jax.ShapeDtypeStruct((t, 2048), BF16), jax.ShapeDtypeStruct((t, NP_IN), BF16),
                   jax.ShapeDtypeStruct((t, LANE), F32)] + list(rider.out_shape),
        scratch_shapes=list(rider.scratch),
        compiler_params=_params(2),
    )(d_o_post, w_out_n, o_all, proj, *rider.operands)


def _attn_bwd(fox, operands, *, t, tb, name, rider=None):
    nb = t // tb
    n_pairs = N_HEADS // 2
    pair = 2 * HEAD
    scale = FOX_SCALE if fox else MLA_SCALE
    q_w = HEAD if fox else 2 * HEAD
    nt_dims = (((1,), (1,)), ((), ()))
    tn_dims = (((0,), (0,)), ((), ()))
    n_rin = len(rider.operands) if rider else 0
    n_rout = len(rider.out_shape) if rider else 0
    n_in, n_out, n_scr = (9, 3, 9) if fox else (6, 3, 2)

    def body(*refs):
        ends = np.cumsum([0, n_in, n_rin, n_out, n_rout, n_scr])
        in_refs, rider_in, out_refs, rider_out, scr_refs = (refs[a:b] for a, b in zip(ends[:-1], ends[1:]))
        rider_refs = (rider_in, rider_out, refs[ends[-1]:])
        if fox:
            q_ref, k_ref, v_ref, do_ref, lse_ref, dl_ref, cq_ref, ck_ref, _ = in_refs
            dproj_ref, dck_ref, dcq_ref = out_refs
            dq_acc, dk_s, dv_s, dc_s, dcq_s, stage_q, stage_k, stage_v, put_sems = scr_refs
        else:
            q_ref, kv_ref, kr_ref, do_ref, lse_ref, dl_ref = in_refs
            dq_acc, dkv_ref, dkr_ref = out_refs
            dk_s, dv_s = scr_refs
        g = pl.program_id(0)
        ki = pl.program_id(1)
        if rider:
            @pl.when(jnp.logical_and(g == 0, ki == 0))
            def _():
                rider.start(*rider_refs)

        @pl.when(ki == 0)
        def _():
            dq_acc[...] = jnp.zeros_like(dq_acc)
            if fox:
                dcq_s[...] = jnp.zeros_like(dcq_s)

        dk_s[...] = jnp.zeros_like(dk_s)
        dv_s[...] = jnp.zeros_like(dv_s)
        if fox:
            dc_s[...] = jnp.zeros_like(dc_s)
            keys = [k_ref[:, HEAD * u:HEAD * (u + 1)] for u in range(2)]
            vals = [v_ref[:, HEAD * u:HEAD * (u + 1)] for u in range(2)]
        else:
            keys = [jnp.concatenate([kv_ref[:, pair * u:pair * u + HEAD], kr_ref[...]], axis=1) for u in range(2)]
            vals = [kv_ref[:, pair * u + HEAD:pair * (u + 1)] for u in range(2)]

        def chunk(qc, masked):
            off = pl.multiple_of(qc * tb, tb)
            for u in range(2):
                kk, vv = keys[u], vals[u]
                qq = q_ref[pl.ds(off, tb), q_w * u:q_w * (u + 1)]
                dd = do_ref[pl.ds(off, tb), HEAD * u:HEAD * (u + 1)]
                s = lax.dot_general(kk, qq, nt_dims, preferred_element_type=F32)
                if fox:
                    s = s + cq_ref[u, qc] - ck_ref[u]
                if masked:
                    row = lax.broadcasted_iota(jnp.int32, (tb, tb), 0)
                    col = lax.broadcasted_iota(jnp.int32, (tb, tb), 1)
                    s = jnp.where(row <= col, s, -jnp.inf)
                p = jnp.exp2(s * (scale * LOG2E) - lse_ref[u, qc] * LOG2E)
                dv_s[u] += jnp.dot(p.astype(BF16), dd, preferred_element_type=F32)
                dp = lax.dot_general(vv, dd, nt_dims, preferred_element_type=F32)
                ds = p * (dp - dl_ref[u, qc])
                if fox:
                    dc_s[u] += jnp.sum(ds, axis=1, keepdims=True)
                    dcq_s[u, qc] += jnp.sum(ds, axis=0, keepdims=True)
                dsb = (ds * scale).astype(BF16)
                dk_s[u] += jnp.dot(dsb, qq, preferred_element_type=F32)
                dq_acc[pl.ds(off, tb), q_w * u:q_w * (u + 1)] += lax.dot_general(dsb, kk, tn_dims,
                                                                                 preferred_element_type=F32)

        chunk(ki, True)

        def loop_body(qc, carry):
            chunk(qc, False)
            return carry

        lax.fori_loop(ki + 1, nb, loop_body, 0)

        def put(stage_ref, rows, seg0, sem):
            col0 = pl.multiple_of(seg0 + g * pair, pair)
            return pltpu.make_async_copy(stage_ref, dproj_ref.at[rows, pl.ds(col0, pair)], sem)

        if fox:
            rows = pl.ds(pl.multiple_of(ki * tb, tb), tb)
            block_puts = [put(stage_k, rows, FK0, put_sems.at[1]), put(stage_v, rows, FV0, put_sems.at[2])]
            pair_put = put(stage_q, pl.ds(0, t), FQ0, put_sems.at[0])

            @pl.when(jnp.logical_or(g > 0, ki > 0))
            def _():
                for cp in block_puts:
                    cp.wait()

            for u in range(2):
                stage_k[:, HEAD * u:HEAD * (u + 1)] = dk_s[u].astype(BF16)
                stage_v[:, HEAD * u:HEAD * (u + 1)] = dv_s[u].astype(BF16)
                dck_ref[u] = -dc_s[u]
            for cp in block_puts:
                cp.start()

            @pl.when(ki == nb - 1)
            def _():
                @pl.when(g > 0)
                def _():
                    pair_put.wait()

                stage_q[...] = dq_acc[...].astype(BF16)
                pair_put.start()
                dcq_ref[...] = dcq_s[...]

            @pl.when(jnp.logical_and(g == n_pairs - 1, ki == nb - 1))
            def _():
                for cp in block_puts + [pair_put]:
                    cp.wait()
        else:
            dkv_ref[...] = jnp.concatenate([dk_s[0, :, :HEAD], dv_s[0], dk_s[1, :, :HEAD], dv_s[1]], axis=1).astype(BF16)
            dkr_ref[...] = jnp.concatenate([dk_s[0, :, HEAD:], dk_s[1, :, HEAD:]], axis=1)

        if rider:
            @pl.when(jnp.logical_and(g == n_pairs - 1, ki == nb - 1))
            def _():
                rider.wait(*rider_refs)

    stat = pl.BlockSpec((2, nb, 1, tb), lambda g, i: (g, 0, 0, 0))
    aliases = {}
    if fox:
        proj, d_o, lse4, delta4, c_row4, c_col, dproj = operands
        ins = [proj, proj, proj, d_o, lse4, delta4, c_row4, c_col, dproj]
        any_spec = pl.BlockSpec(memory_space=pl.ANY)
        in_specs = [pl.BlockSpec((t, pair), lambda g, i: (0, FQ0 // pair + g)),
                    pl.BlockSpec((tb, pair), lambda g, i: (i, FK0 // pair + g)),
                    pl.BlockSpec((tb, pair), lambda g, i: (i, FV0 // pair + g)),
                    pl.BlockSpec((t, pair), lambda g, i: (0, n_pairs + g)),
                    stat, stat, stat, pl.BlockSpec((2, tb, 1), lambda g, i: (g, i, 0)), any_spec]
        aliases = {8: 0}
        out_specs = [any_spec, pl.BlockSpec((2, tb, 1), lambda g, i: (g, i, 0)), stat]
        out_shape = [jax.ShapeDtypeStruct(dproj.shape, dproj.dtype), jax.ShapeDtypeStruct((N_HEADS, t, 1), F32),
                     jax.ShapeDtypeStruct((N_HEADS, nb, 1, tb), F32)]
        scratch = [pltpu.VMEM((t, pair), F32), pltpu.VMEM((2, tb, HEAD), F32), pltpu.VMEM((2, tb, HEAD), F32),
                   pltpu.VMEM((2, tb, 1), F32), pltpu.VMEM((2, nb, 1, tb), F32),
                   pltpu.VMEM((t, pair), BF16), pltpu.VMEM((tb, pair), BF16), pltpu.VMEM((tb, pair), BF16),
                   pltpu.SemaphoreType.DMA((3,))]
    else:
        q_full, kv, kr, d_o, lse4, delta4 = operands
        ins = [q_full, kv, kr, d_o, lse4, delta4]
        in_specs = [pl.BlockSpec((t, 2 * pair), lambda g, i: (0, g)),
                    pl.BlockSpec((tb, 2 * pair), lambda g, i: (i, g)),
                    pl.BlockSpec((tb, HEAD), lambda g, i: (i, 0)),
                    pl.BlockSpec((t, pair), lambda g, i: (0, g)),
                    stat, stat]
        out_specs = [pl.BlockSpec((t, 2 * pair), lambda g, i: (0, g)), pl.BlockSpec((tb, 2 * pair), lambda g, i: (i, g)),
                     pl.BlockSpec((tb, pair), lambda g, i: (i, g))]
        out_shape = [jax.ShapeDtypeStruct((t, 2048), F32), jax.ShapeDtypeStruct((t, 2048), BF16),
                     jax.ShapeDtypeStruct((t, 1024), F32)]
        scratch = [pltpu.VMEM((2, tb, 2 * HEAD), F32), pltpu.VMEM((2, tb, HEAD), F32)]
    assert (len(ins), len(out_specs), len(scratch)) == (n_in, n_out, n_scr)
    if rider:
        any_spec = pl.BlockSpec(memory_space=pl.ANY)
        aliases = {**aliases, **{n_in + i_in: n_out + i_out for i_in, i_out in rider.aliases.items()}}
        ins = ins + list(rider.operands)
        in_specs = in_specs + [any_spec] * n_rin
        out_specs = out_specs + [any_spec] * n_rout
        out_shape = out_shape + list(rider.out_shape)
        scratch = scratch + list(rider.scratch)
    return pl.pallas_call(
        body, name=name, grid=(n_pairs, nb), in_specs=in_specs, out_specs=out_specs, out_shape=out_shape,
        scratch_shapes=scratch, input_output_aliases=aliases, compiler_params=_params(2),
    )(*ins)


def _mid_bwd(dq_full, dkr, cos_t, sin_t, dcq, dck, flog, bf_row, *, tm):
    t = dq_full.shape[0]
    n = t // tm

    def body(dq_ref, dkr_ref, cos_ref, sin_ref, dcq_ref, dck_ref, fl_ref, bf_ref,
             dq2_ref, dkraw_ref, dfl_ref, dbf_ref, carry_ref):
        i = pl.program_id(0)

        @pl.when(i == 0)
        def _():
            carry_ref[...] = jnp.zeros_like(carry_ref)
            dbf_ref[...] = jnp.zeros_like(dbf_ref)

        c, s = cos_ref[...], sin_ref[...]
        dkr_sum = jnp.zeros((tm, LANE), F32)
        for h in range(N_HEADS):
            dq2_ref[:, 256 * h:256 * h + 128] = dq_ref[:, 256 * h:256 * h + 128].astype(BF16)
            dq2_ref[:, 256 * h + 128:256 * h + 256] = _rot_bwd(dq_ref[:, 256 * h + 128:256 * h + 256], c, s).astype(BF16)
            dkr_sum = dkr_sum + dkr_ref[:, HEAD * h:HEAD * (h + 1)]
        dkraw_ref[...] = _rot_bwd(dkr_sum, c, s).astype(BF16)

        dc = dcq_ref[...]
        lane = lax.broadcasted_iota(jnp.int32, (tm, LANE), 1)
        for h in range(N_HEADS):
            dc = dc + jnp.where(lane == h, dck_ref[h], 0.0)
        row = lax.broadcasted_iota(jnp.int32, (tm, tm), 0)
        col = lax.broadcasted_iota(jnp.int32, (tm, tm), 1)
        tri = (col >= row).astype(BF16)
        acc = carry_ref[0:1, :]
        for part in _split3(dc):
            acc = acc + jnp.dot(tri, part, preferred_element_type=F32)
        carry_ref[0:1, :] = carry_ref[0:1, :] + jnp.sum(dc, axis=0, keepdims=True)
        z = fl_ref[...] + bf_ref[...]
        dz = acc / (1.0 + jnp.exp(z))
        dfl_ref[...] = dz.astype(BF16)
        dbf_ref[...] += jnp.sum(dz, axis=0, keepdims=True)

    rev = lambda w: pl.BlockSpec((tm, w), lambda i: (n - 1 - i, 0))
    vec = lambda w: pl.BlockSpec((1, w), lambda i: (0, 0))
    return pl.pallas_call(
        body, name="mid_bwd", grid=(n,),
        in_specs=[rev(2048), rev(1024), rev(LANE), rev(LANE), rev(LANE),
                  pl.BlockSpec((N_HEADS, tm, 1), lambda i: (0, n - 1 - i, 0)), rev(LANE), vec(LANE)],
        out_specs=[rev(2048), rev(LANE), rev(LANE), vec(LANE)],
        out_shape=[jax.ShapeDtypeStruct((t, 2048), BF16), jax.ShapeDtypeStruct((t, LANE), BF16),
                   jax.ShapeDtypeStruct((t, LANE), BF16), jax.ShapeDtypeStruct((1, LANE), F32)],
        scratch_shapes=[pltpu.VMEM((8, LANE), F32)],
        compiler_params=_params(1),
    )(dq_full, dkr, cos_t, sin_t, dcq, dck, flog, bf_row)


def _norm_bwd(proj, dqn, dkvn, g_q, g_kv, dkr_raw, dfl, dproj, rider, *, tm):
    t = proj.shape[0]
    n_steps = t // tm
    assert (KR0, FL0, KVL0, LAT_W) == (Q_RANK, Q_RANK + LANE, Q_RANK + 2 * LANE, Q_RANK + 2 * LANE + KV_RANK)
    n_rin, n_rout = len(rider.operands), len(rider.out_shape)

    def body(*refs):
        p_ref, dqn_ref, dkvn_ref, gq_ref, gkv_ref, dkr_ref, dfl_ref = refs[:7]
        dproj_ref, dgq_ref, dgkv_ref = refs[8 + n_rin:11 + n_rin]
        rider_refs = (refs[8:8 + n_rin], refs[11 + n_rin:11 + n_rin + n_rout], refs[11 + n_rin + n_rout:])
        i = pl.program_id(0)
        pl.when(i == 0)(lambda: rider.start(*rider_refs))
        pl.when(i == n_steps - 1)(lambda: rider.wait(*rider_refs))

        @pl.when(i == 0)
        def _():
            dgq_ref[...] = jnp.zeros_like(dgq_ref)
            dgkv_ref[...] = jnp.zeros_like(dgkv_ref)

        d_lat = []
        for lo, w, dn_ref, g_ref, dg_ref in ((QL0, Q_RANK, dqn_ref, gq_ref, dgq_ref),
                                             (KVL0, KV_RANK, dkvn_ref, gkv_ref, dgkv_ref)):
            xv = p_ref[:, lo:lo + w].astype(F32)
            r = lax.rsqrt(jnp.mean(xv * xv, axis=-1, keepdims=True) + NORM_EPS)
            xh = xv * r
            dn = dn_ref[...]
            dg_ref[...] += jnp.sum(dn * xh, axis=0, keepdims=True)
            dxh = dn * g_ref[...]
            d_lat.append((r * (dxh - xh * jnp.mean(dxh * xh, axis=-1, keepdims=True))).astype(BF16))
        dproj_ref[...] = jnp.concatenate([d_lat[0], dkr_ref[...], dfl_ref[...], d_lat[1]], axis=1)

    row = lambda w: pl.BlockSpec((tm, w), lambda i: (i, 0))
    vec = lambda w: pl.BlockSpec((1, w), lambda i: (0, 0))
    return pl.pallas_call(
        body, name="norm_bwd", grid=(n_steps,),
        in_specs=[row(LAT_W), row(Q_RANK), row(KV_RANK), vec(Q_RANK), vec(KV_RANK), row(LANE), row(LANE)]
        + [pl.BlockSpec(memory_space=pl.ANY)] * (1 + n_rin),
        out_specs=[row(LAT_W), vec(Q_RANK), vec(KV_RANK)] + [pl.BlockSpec(memory_space=pl.ANY)] * n_rout,
        out_shape=[jax.ShapeDtypeStruct(dproj.shape, dproj.dtype),
                   jax.ShapeDtypeStruct((1, Q_RANK), F32), jax.ShapeDtypeStruct((1, KV_RANK), F32)]
        + list(rider.out_shape),
        scratch_shapes=list(rider.scratch),
        input_output_aliases={7: 0, **{8 + i_in: 3 + i_out for i_in, i_out in rider.aliases.items()}},
        compiler_params=_params(1),
    )(proj, dqn, dkvn, g_q, g_kv, dkr_raw, dfl, dproj, *rider.operands)


def _prenorm_bwd(dh, x, g, dy, *, tm):
    t = x.shape[0]

    def body(dh_ref, x_ref, g_ref, dy_ref, gx_ref, dg_ref):
        i = pl.program_id(0)

        @pl.when(i == 0)
        def _():
            dg_ref[...] = jnp.zeros_like(dg_ref)

        xv = x_ref[...]
        r = lax.rsqrt(jnp.mean(xv * xv, axis=-1, keepdims=True) + NORM_EPS)
        xh = xv * r
        dn = dh_ref[...]
        dg_ref[...] += jnp.sum(dn * xh, axis=0, keepdims=True)
        dxh = dn * g_ref[...]
        gx_ref[...] = dy_ref[...] + r * (dxh - xh * jnp.mean(dxh * xh, axis=-1, keepdims=True))

    row = pl.BlockSpec((tm, D_MODEL), lambda i: (i, 0))
    vec = pl.BlockSpec((1, D_MODEL), lambda i: (0, 0))
    return pl.pallas_call(
        body, name="prenorm_bwd", grid=(t // tm,),
        in_specs=[row, row, vec, row], out_specs=[row, vec],
        out_shape=[jax.ShapeDtypeStruct((t, D_MODEL), F32), jax.ShapeDtypeStruct((1, D_MODEL), F32)],
        compiler_params=_params(1),
    )(dh, x, g, dy)


def _adam_math(w, g, m, v):
    m = ADAM_B1 * m + (1.0 - ADAM_B1) * g
    v = ADAM_B2 * v + (1.0 - ADAM_B2) * (g * g)
    m_hat = m / (1.0 - ADAM_B1 ** ADAM_STEP)
    v_hat = v / (1.0 - ADAM_B2 ** ADAM_STEP)
    delta = -ADAM_LR * (m_hat / (jnp.sqrt(v_hat) + ADAM_EPS) + ADAM_WD * w)
    return delta, m, v


def _adamw(land, w, m, v, *, tr, name):
    rows, cols = w.shape

    def body(l_ref, w_ref, m_ref, v_ref, g_ref, d_ref, nm_ref, nv_ref):
        g = l_ref[0].astype(F32)
        for s in range(1, N_CHIPS):
            g = g + l_ref[s].astype(F32)
        g_ref[...] = g
        d_ref[...], nm_ref[...], nv_ref[...] = _adam_math(w_ref[...], g, m_ref[...], v_ref[...])

    blk = pl.BlockSpec((tr, cols), lambda i: (i, 0))
    return pl.pallas_call(
        body, name=name, grid=(rows // tr,),
        in_specs=[pl.BlockSpec((N_CHIPS, tr, cols), lambda i: (0, i, 0)), blk, blk, blk],
        out_specs=[blk, blk, blk, blk],
        out_shape=[jax.ShapeDtypeStruct((rows, cols), F32)] * 4,
        compiler_params=_params(1),
    )(land, w, m, v)


def _adamw_small(gathered, w, m, v):
    def body(a_ref, w_ref, m_ref, v_ref, g_ref, d_ref, nm_ref, nv_ref):
        g = a_ref[0:SMALL_ROWS, :]
        for s in range(1, N_DEV):
            g = g + a_ref[SMALL_ROWS * s:SMALL_ROWS * (s + 1), :]
        g_ref[...] = g
        d_ref[...], nm_ref[...], nv_ref[...] = _adam_math(w_ref[...], g, m_ref[...], v_ref[...])

    return pl.pallas_call(
        body, name="adamw_small",
        out_shape=[jax.ShapeDtypeStruct((SMALL_ROWS, SMALL_COLS), F32)] * 4,
        compiler_params=_params(),
    )(gathered, w, m, v)


def _place():
    x, y, c = lax.axis_index("x"), lax.axis_index("y"), lax.axis_index("c")
    return x, y, c


def _flip(p, k):
    x, y, c = p
    return (1 - x if k & 4 else x, 1 - y if k & 2 else y, 1 - c if k & 1 else c)


def _index(p):
    return 4 * p[0] + 2 * p[1] + p[2]


class _AllGather:
    def __init__(self, shard):
        assert shard.shape[0] % 32 == 0
        self.half = shard.shape[0] // 2
        self.operands = [shard]
        self.out_shape = [jax.ShapeDtypeStruct((N_DEV,) + shard.shape, shard.dtype)]
        self.aliases = {}
        self.scratch = [pltpu.SemaphoreType.DMA((9,)), pltpu.SemaphoreType.DMA((9,)), pltpu.SemaphoreType.DMA(())]

    def _parts(self, ins, outs, scratch):
        (in_ref,), (out_ref,), (send_sems, recv_sems, local_sem) = ins, outs, scratch
        me = _place()

        def copy(k, block, to, part=None, src=None):
            dst = out_ref.at[_index(block)] if part is None else out_ref.at[_index(block), part]
            return pltpu.make_async_remote_copy(
                src_ref=dst if src is None else src, dst_ref=dst, send_sem=send_sems.at[k], recv_sem=recv_sems.at[k],
                device_id=to, device_id_type=MESH)

        mine = pltpu.make_async_copy(in_ref, out_ref.at[_index(me)], local_sem)
        first = [copy(0, me, _flip(me, 1), src=in_ref), copy(1, me, _flip(me, 4), src=in_ref),
                 copy(2, me, _flip(me, 2), src=in_ref)]
        return me, copy, mine, first

    def start(self, ins, outs, scratch):
        _, _, mine, first = self._parts(ins, outs, scratch)
        mine.start()
        for cp in first:
            cp.start()

    def wait(self, ins, outs, scratch):
        me, copy, mine, sent = self._parts(ins, outs, scratch)
        sibling, x_nbr, y_nbr, diagonal = _flip(me, 1), _flip(me, 4), _flip(me, 2), _flip(me, 6)
        top, bottom = pl.ds(0, self.half), pl.ds(self.half, self.half)
        arrivals = [(1, x_nbr, None, [(3, sibling, None), (5, y_nbr, top)]),
                    (2, y_nbr, None, [(4, sibling, None), (6, x_nbr, bottom)]),
                    (5, diagonal, top, [(7, sibling, top)]),
                    (6, diagonal, bottom, [(8, sibling, bottom)])]
        for k, block, part, onward in arrivals:
            copy(k, block, me, part).wait_recv()
            for k_on, to, part_on in onward:
                cp = copy(k_on, block, to, part_on)
                cp.start()
                sent.append(cp)
        other = lambda p: _flip(p, 1)
        for k, block, part in ((0, sibling, None), (3, other(x_nbr), None), (4, other(y_nbr), None),
                               (7, other(diagonal), top), (8, other(diagonal), bottom)):
            copy(k, block, me, part).wait_recv()
        for cp in sent:
            cp.wait_send()
        mine.wait()


class _Exchange:
    def __init__(self, tasks):
        self.tasks = tasks
        taken = [land for _, _, land, _, _ in tasks if land is not None]
        self.operands = [src for src, _, _, _, _ in tasks] + taken
        self.out_shape = [
            jax.ShapeDtypeStruct((N_CHIPS,) + ((2,) if by_core else ()) + (src.shape if same else src.shape[1:]), src.dtype)
            for src, _, _, same, by_core in tasks]
        self.aliases, n_taken = {}, 0
        for a, (_, _, land, _, _) in enumerate(tasks):
            if land is not None:
                self.aliases[len(tasks) + n_taken] = a
                n_taken += 1
        self.scratch = [pltpu.SemaphoreType.DMA((N_CHIPS,)), pltpu.SemaphoreType.DMA((N_CHIPS,)),
                        pltpu.SemaphoreType.DMA(())] * len(tasks)

    def _copies(self, ins, outs, scratch):
        x, y, core = _place()
        my = 2 * x + y
        for a, (_, chips, _, same, by_core) in enumerate(self.tasks):
            send_sems, recv_sems, local_sem = scratch[3 * a:3 * a + 3]
            slot = (lambda s, a=a, by_core=by_core: outs[a].at[s, core] if by_core else outs[a].at[s])
            for i, j in enumerate(chips):
                src = ins[a] if same else ins[a].at[i]
                pair = jnp.bitwise_xor(my, j)
                remote = pltpu.make_async_remote_copy(
                    src_ref=src, dst_ref=slot(my), send_sem=send_sems.at[pair], recv_sem=recv_sems.at[pair],
                    device_id=(j >> 1, j & 1, core), device_id_type=MESH)
                local = pltpu.make_async_copy(src, slot(my), local_sem)
                yield j, my, core, remote, local, slot, (send_sems, recv_sems)

    def start(self, ins, outs, scratch):
        for j, my, _, remote, local, _, _ in self._copies(ins, outs, scratch):
            pl.when(my != j)(remote.start)
            pl.when(my == j)(local.start)

    def wait(self, ins, outs, scratch):
        for j, my, core, remote, local, slot, (send_sems, recv_sems) in self._copies(ins, outs, scratch):
            pl.when(my != j)(remote.wait_send)

            @pl.when(my == j)
            def _():
                local.wait()
                for s in range(N_CHIPS):
                    if s != j:
                        pltpu.make_async_remote_copy(
                            src_ref=slot(s), dst_ref=slot(s), send_sem=send_sems.at[j ^ s], recv_sem=recv_sems.at[j ^ s],
                            device_id=(s >> 1, s & 1, core), device_id_type=MESH).wait_recv()


N_CHIPS = 4
ALL_CHIPS = tuple(range(N_CHIPS))


class _ToOtherCore:
    def __init__(self, parts):
        self.operands = list(parts)
        self.out_shape = [jax.ShapeDtypeStruct(p.shape[1:], p.dtype) for p in parts]
        self.aliases = {}
        self.scratch = [pltpu.SemaphoreType.DMA((len(parts),)), pltpu.SemaphoreType.DMA((len(parts),))]

    def _copies(self, ins, outs, scratch):
        send_sems, recv_sems = scratch
        me = _place()
        return [pltpu.make_async_remote_copy(src_ref=ins[a].at[1 - me[2]], dst_ref=outs[a], send_sem=send_sems.at[a],
                                             recv_sem=recv_sems.at[a], device_id=_flip(me, 1), device_id_type=MESH)
                for a in range(len(ins))]

    def start(self, ins, outs, scratch):
        for cp in self._copies(ins, outs, scratch):
            cp.start()

    def wait(self, ins, outs, scratch):
        for cp in self._copies(ins, outs, scratch):
            cp.wait()


def _to_other_core(parts, *, name):
    swap = _ToOtherCore(parts)
    n_arr = len(parts)
    hbm = pl.BlockSpec(memory_space=pl.ANY)

    def body(*refs):
        rider_refs = (refs[:n_arr], refs[n_arr:2 * n_arr], refs[2 * n_arr:])
        swap.start(*rider_refs)
        swap.wait(*rider_refs)

    return pl.pallas_call(
        body, name=name, in_specs=[hbm] * n_arr, out_specs=[hbm] * n_arr,
        out_shape=swap.out_shape, scratch_shapes=swap.scratch,
    )(*parts)


class _ShareWithOtherCore:
    def __init__(self, gathered):
        n_arr = len(gathered)
        self.operands = list(gathered)
        self.out_shape = [jax.ShapeDtypeStruct(g.shape, g.dtype) for g in gathered]
        self.aliases = {a: a for a in range(n_arr)}
        self.scratch = [pltpu.SemaphoreType.DMA((N_CHIPS * n_arr,)), pltpu.SemaphoreType.DMA((N_CHIPS * n_arr,))]

    def _copies(self, outs, scratch):
        send_sems, recv_sems = scratch
        me = _place()
        copies = []
        for a, buf in enumerate(outs):
            for j in range(N_CHIPS):
                block = buf.at[j, me[2]]
                copies.append(pltpu.make_async_remote_copy(
                    src_ref=block, dst_ref=block, send_sem=send_sems.at[N_CHIPS * a + j],
                    recv_sem=recv_sems.at[N_CHIPS * a + j], device_id=_flip(me, 1), device_id_type=MESH))
        return copies

    def start(self, ins, outs, scratch):
        for cp in self._copies(outs, scratch):
            cp.start()

    def wait(self, ins, outs, scratch):
        for cp in self._copies(outs, scratch):
            cp.wait()


def _pair_sum(mine, other, core, *, tr, name):
    _, n, rows, cols = mine.shape
    tr = min(tr, rows)

    def body(core_ref, a_ref, b_ref, o_ref):
        o_ref[...] = (a_ref[0].astype(F32) + b_ref[...].astype(F32)).astype(BF16)

    return pl.pallas_call(
        body, name=name,
        grid_spec=pltpu.PrefetchScalarGridSpec(
            num_scalar_prefetch=1, grid=(n, rows // tr),
            in_specs=[pl.BlockSpec((1, 1, tr, cols), lambda j, i, core_ref: (core_ref[0], j, i, 0)),
                      pl.BlockSpec((1, tr, cols), lambda j, i, core_ref: (j, i, 0))],
            out_specs=pl.BlockSpec((1, tr, cols), lambda j, i, core_ref: (j, i, 0))),
        out_shape=jax.ShapeDtypeStruct(other.shape, BF16),
        compiler_params=_params(2),
    )(core, mine, other)


def _gather_small(vec):
    def body(v_ref, out_ref, send_sems, recv_sems, local_sem):
        me = _place()

        def rows(p):
            return out_ref.at[pl.ds(pl.multiple_of(_index(p) * SMALL_ROWS, SMALL_ROWS), SMALL_ROWS), :]

        mine = pltpu.make_async_copy(v_ref, rows(me), local_sem)
        mine.start()
        sends = []
        for k in range(1, N_DEV):
            peer = _flip(me, k)
            cp = pltpu.make_async_remote_copy(src_ref=v_ref, dst_ref=rows(me), send_sem=send_sems.at[k - 1],
                                              recv_sem=recv_sems.at[k - 1], device_id=peer, device_id_type=MESH)
            cp.start()
            sends.append(cp)
        for k in range(1, N_DEV):
            peer = _flip(me, k)
            pltpu.make_async_remote_copy(src_ref=rows(peer), dst_ref=rows(peer), send_sem=send_sems.at[k - 1],
                                         recv_sem=recv_sems.at[k - 1], device_id=peer, device_id_type=MESH).wait_recv()
        for cp in sends:
            cp.wait_send()
        mine.wait()

    return pl.pallas_call(
        body, name="gather_small",
        in_specs=[pl.BlockSpec(memory_space=pltpu.VMEM)], out_specs=pl.BlockSpec(memory_space=pltpu.VMEM),
        out_shape=jax.ShapeDtypeStruct((N_DEV * SMALL_ROWS, SMALL_COLS), F32),
        scratch_shapes=[pltpu.SemaphoreType.DMA((7,)), pltpu.SemaphoreType.DMA((7,)), pltpu.SemaphoreType.DMA],
    )(vec)


def _w_in_nice(gathered):
    pieces, pos = [], 0
    for o0, width, n0 in sorted(_SEGMENTS, key=lambda seg: seg[2]):
        if n0 > pos:
            pieces.append(jnp.zeros((D_MODEL, n0 - pos), gathered.dtype))
        o = o0
        while o < o0 + width:
            d = o // SHARD_IN
            hi = min(o0 + width, (d + 1) * SHARD_IN)
            pieces.append(gathered[d][:, o - d * SHARD_IN:hi - d * SHARD_IN])
            o = hi
        pos = n0 + width
    pieces.append(jnp.zeros((D_MODEL, NP_IN - pos), gathered.dtype))
    return jnp.concatenate(pieces, axis=1)


def _w_in_blocks(chips, dw_lat, dw_rest):
    blocks = []
    for core in range(2):
        for chip in chips:
            lo = (2 * chip + core) * SHARD_IN
            runs = []
            for o0, width, n0 in _SEGMENTS:
                a, b = max(lo, o0), min(lo + SHARD_IN, o0 + width)
                if a < b:
                    n_a, n_b = n0 + a - o0, n0 + b - o0
                    runs.append(dw_lat[:, n_a:n_b] if n_b <= LAT_W else dw_rest[:, n_a - LAT_W:n_b - LAT_W])
            blocks.append(jnp.concatenate(runs, axis=1))
    return jnp.stack(blocks).reshape(2, len(chips), D_MODEL, SHARD_IN)


def _by_core(shards):
    return shards.reshape((N_CHIPS, 2) + shards.shape[1:]).swapaxes(0, 1)


EARLY_CHIPS = (1, 2)
LATE_CHIPS = (0, 3)


def _w_uq_nice(shard):
    z = jnp.zeros((Q_RANK, 32), shard.dtype)
    return jnp.concatenate([shard[:, :128], shard[:, 128:160], z, shard[:, 160:192], z], axis=1)


def _pack_small(g_pre, g_post, g_q, g_kv, b_f, extra=None):
    parts = [g_pre.reshape(-1), g_post.reshape(-1), g_q.reshape(-1), g_kv.reshape(-1), b_f.reshape(-1)]
    if extra is not None:
        parts.append(extra.reshape(-1))
    flat = jnp.concatenate(parts)
    flat = jnp.concatenate([flat, jnp.zeros((SMALL_ROWS * SMALL_COLS - flat.shape[0],), F32)])
    return flat.reshape(SMALL_ROWS, SMALL_COLS)


def _unpack_small(packed):
    flat = packed.reshape(-1)
    o = 0
    out = []
    for n in (D_MODEL, D_MODEL, Q_RANK, KV_RANK, N_HEADS):
        out.append(flat[o:o + n].reshape(1, n))
        o += n
    return out, flat[o]


def kernel(x, positions, g_pre, w_in, g_q_latent, w_uq, g_kv_latent, w_ukv, b_forget, w_out, g_post, loss_target, m_g_pre, m_w_in, m_g_q_latent, m_w_uq, m_g_kv_latent, m_w_ukv, m_b_forget, m_w_out, m_g_post, v_g_pre, v_w_in, v_g_q_latent, v_w_uq, v_g_kv_latent, v_w_ukv, v_b_forget, v_w_out, v_g_post):
    t = x.shape[1]
    tb = min(512, t)
    tm = min(512, t)
    nb = t // tb
    x2 = x.reshape(t, D_MODEL)
    target = loss_target.reshape(t, D_MODEL)
    pos_col = positions.reshape(t, 1).astype(F32)
    bf_row = jnp.concatenate([b_forget.reshape(1, N_HEADS), jnp.zeros((1, LANE - N_HEADS), F32)], axis=1)

    h, h_t, g_in = _prenorm(x2, g_pre, _AllGather(w_in[0].astype(BF16)), tm=tm)
    w_in_n = _w_in_nice(g_in)
    gather_rest = _Exchange([(w, ALL_CHIPS, None, True, True) for w in
                             (_w_uq_nice(w_uq[0].astype(BF16)), w_ukv[0].astype(BF16), w_out[0].astype(BF16))])
    core = lax.axis_index("c").astype(jnp.int32).reshape(1)
    proj, g_uq, g_ukv, g_out, flog = _mm(h, w_in_n, name="proj_in", out_dtype=BF16, tm=2048, tn=512, tk=2048,
                                         rider=gather_rest, f32_cols=(FL0, LANE))
    qn, kvn, kr, cos_t, sin_t, c, qn_t, kvn_t, g_uq, g_ukv, g_out = _mid_fwd(
        proj, flog, g_q_latent, g_kv_latent, bf_row, pos_col, _ShareWithOtherCore([g_uq, g_ukv, g_out]), tm=tm)
    w_uq_n = g_uq.reshape(N_DEV, Q_RANK, 256).transpose(1, 0, 2).reshape(Q_RANK, N_HEADS * 256)
    w_ukv_n = g_ukv.reshape(N_DEV, KV_RANK, 256).transpose(1, 0, 2).reshape(KV_RANK, N_HEADS * 256)
    w_out_n = g_out.reshape(D_MODEL, D_MODEL)
    q_full = _q_up_rope(qn, w_uq_n, cos_t, sin_t, tm=min(1024, t))
    kv = _mm(kvn, w_ukv_n, name="kv_up", out_dtype=BF16, tm=1024, tn=512, tk=KV_RANK)
    c_heads = c[:, :N_HEADS].T
    c_col = c_heads.reshape(N_HEADS, t, 1)
    c_row4 = c_heads.reshape(N_HEADS, nb, 1, tb)
    o_all, og_all, og_t, lse4_mla = _attn_fwd(False, (q_full, kv, kr, proj), t=t, tb=tb, name="mla_fwd")
    o_all, og_all, og_t, lse4_fox = _attn_fwd(True, (proj, c_col, c_row4, o_all, og_all, og_t), t=t, tb=tb,
                                              name="fox_fwd")
    dy, d_o_post, dg_post, loss_part = _out_norm_loss(og_all, w_out_n, x2, target, g_post, tm=min(512, t))

    dw_out = _mm(og_t, d_o_post, name="dw_out", out_dtype=BF16, tm=1024, tn=1024, tk=1024)
    p_out = _by_core(dw_out.reshape(N_DEV, D_MODEL // N_DEV, D_MODEL))
    d_attn, dproj, delta, o_out = _dog_gate(d_o_post, w_out_n, o_all, proj, _ToOtherCore([p_out]), tm=min(1024, t))
    s_out = _pair_sum(p_out, o_out, core, tr=256, name="dw_out_pair_sum")
    delta4 = delta[:, :2 * N_HEADS].T.reshape(2 * N_HEADS, nb, 1, tb)
    dproj, dck, dcq, l_out = _attn_bwd(True, (proj, d_attn, lse4_fox, delta4[N_HEADS:], c_row4, c_col, dproj),
                                       t=t, tb=tb, name="fox_bwd",
                                       rider=_Exchange([(s_out, ALL_CHIPS, None, False, False)]))
    dw_in_rest = _mm(h_t, dproj, name="dw_in_rest", out_dtype=BF16, tm=2048, tn=512, tk=2048,
                     b_cols=(LAT_W, NP_IN - LAT_W))
    p_in = _w_in_blocks(EARLY_CHIPS, None, dw_in_rest)
    (o_in,) = _to_other_core([p_in], name="dw_in_early_to_core")
    s_in = _pair_sum(p_in, o_in, core, tr=256, name="dw_in_early_pair_sum")
    dq_full, dkv, dkr, l_in = _attn_bwd(False, (q_full, kv, kr, d_attn, lse4_mla, delta4[:N_HEADS]),
                                        t=t, tb=tb, name="mla_bwd",
                                        rider=_Exchange([(s_in, EARLY_CHIPS, None, False, False)]))
    dcq_rows = jnp.concatenate([dcq.reshape(N_HEADS, t).T, jnp.zeros((t, LANE - N_HEADS), F32)], axis=1)
    dq2, dkr_raw, dfl, dbf = _mid_bwd(dq_full, dkr, cos_t, sin_t, dcq_rows, dck, flog, bf_row, tm=tm)
    dqn = _mm(dq2, w_uq_n, name="d_qn", nt=True, out_dtype=F32, tm=1024, tn=Q_RANK, tk=2048)
    dkvn = _mm(dkv, w_ukv_n, name="d_kvn", nt=True, out_dtype=F32, tm=1024, tn=KV_RANK, tk=2048)
    dw_uq = _mm(qn_t, dq2, name="dw_uq", out_dtype=BF16, tm=Q_RANK, tn=1024, tk=1024)
    dw_ukv = _mm(kvn_t, dkv, name="dw_ukv", out_dtype=BF16, tm=KV_RANK, tn=1024, tk=1024)
    dw_uq_h = dw_uq.reshape(Q_RANK, N_HEADS, 256)
    s_uq = jnp.concatenate([dw_uq_h[:, :, :160], dw_uq_h[:, :, 192:224]], axis=2).transpose(1, 0, 2)
    s_ukv = dw_ukv.reshape(KV_RANK, N_HEADS, 256).transpose(1, 0, 2)
    up_parts = [_by_core(s_uq), _by_core(s_ukv)]
    dproj, dg_q, dg_kv, *up_other = _norm_bwd(proj, dqn, dkvn, g_q_latent, g_kv_latent, dkr_raw, dfl, dproj,
                                              _ToOtherCore(up_parts), tm=tm)
    up_sums = [_pair_sum(p, o_, core, tr=256, name=f"dw_up_pair_sum_{i}") for i, (p, o_) in enumerate(zip(up_parts, up_other))]
    dw_in_lat, l_uq, l_ukv = _mm(h_t, dproj, name="dw_in_lat", out_dtype=BF16, tm=1024, tn=LAT_W, tk=1024,
                                 b_cols=(0, LAT_W), rider=_Exchange([(s, ALL_CHIPS, None, False, False) for s in up_sums]))
    p_late = _w_in_blocks(LATE_CHIPS, dw_in_lat, dw_in_rest)
    (o_late,) = _to_other_core([p_late], name="dw_in_late_to_core")
    s_late = _pair_sum(p_late, o_late, core, tr=256, name="dw_in_late_pair_sum")
    dh, l_in = _mm(dproj, w_in_n, name="d_h", nt=True, out_dtype=F32, tm=2048, tn=1024, tk=NP_IN // 4,
                   rider=_Exchange([(s_late, LATE_CHIPS, l_in, False, False)]))
    grad_x, dg_pre = _prenorm_bwd(dh, x2, g_pre, dy, tm=min(512, t))

    small = _gather_small(_pack_small(dg_pre, dg_post, dg_q, dg_kv, dbf[:, :N_HEADS], loss_part))

    res_in = _adamw(l_in, w_in[0], m_w_in[0], v_w_in[0], tr=256, name="adamw_w_in")
    res_uq = _adamw(l_uq, w_uq[0], m_w_uq[0], v_w_uq[0], tr=256, name="adamw_w_uq")
    res_ukv = _adamw(l_ukv, w_ukv[0], m_w_ukv[0], v_w_ukv[0], tr=256, name="adamw_w_ukv")
    res_out = _adamw(l_out, w_out[0], m_w_out[0], v_w_out[0], tr=128, name="adamw_w_out")
    res_small = _adamw_small(
        small,
        _pack_small(g_pre, g_post, g_q_latent, g_kv_latent, b_forget),
        _pack_small(m_g_pre, m_g_post, m_g_q_latent, m_g_kv_latent, m_b_forget),
        _pack_small(v_g_pre, v_g_post, v_g_q_latent, v_g_kv_latent, v_b_forget))
    small_out = [_unpack_small(r) for r in res_small]
    loss = small_out[0][1]

    def leaves(kind):
        (s_pre, s_post, s_q, s_kv, s_bf), _ = small_out[kind]
        return [s_pre, res_in[kind][None], s_q, res_uq[kind][None], s_kv, res_ukv[kind][None], s_bf,
                res_out[kind][None], s_post]

    return (loss, grad_x.reshape(x.shape), *leaves(0), *leaves(1), *leaves(2), *leaves(3))
```
